```python
import functools
import jax
import jax.numpy as jnp
from jax import lax
import numpy as np

D_MODEL = 1024
BATCH = 2
SEQ = 8192
DEPTH = 2
DEC_BATCH = 128
DEC_SEQ = 8
PAST_LEN = 2048
PAGE_SIZE = 128

N_EVEN = (DEPTH + 1) // 2
N_ODD = DEPTH // 2
EPS = 1e-6
D_A = D_MODEL // 2
D_B = D_MODEL // 2
CONV_A_W = 3
CONV_B_W = 31
EV_IN = 3 * D_A + 2 * D_B
NSA_HEADS = 8
NSA_HD = 64
NSA_KV = 2
NSA_W = NSA_HEADS * NSA_HD
L_CMP = 32
L_SEL = 64
N_SEL = 16
WINDOW = 512
Q_BLOCK = 128
M_DIN = D_MODEL // 2
M_HDIM = 64
M_HEADS = M_DIN // M_HDIM
M_DSTATE = 64
M_GROUPS = 2
M_CONV_W = 4
M_CONV_DIM = M_DIN + 2 * M_GROUPS * M_DSTATE
SSD_CHUNK = 128
OD_IN = NSA_W + 6 * NSA_KV * NSA_HD + 3 * NSA_HEADS + M_DIN + M_CONV_DIM + M_HEADS
N_MEM = 256
XA_HEADS = 4
XA_HD = 128
XA_W = XA_HEADS * XA_HD
MOE_GROUPS = 4
MOE_EPG = 8
MOE_E = MOE_GROUPS * MOE_EPG
MOE_TOPK = 2
MOE_FF = 512
MOE_BLOCK = 128

kernel_name = 'hybrid_conv_nsa_ssd_hmoe_step'


def rmsnorm(x, g):
    xf = x.astype(jnp.float32)
    y = xf * lax.rsqrt(jnp.mean(xf * xf, axis=-1, keepdims=True) + EPS)
    return (y * g.astype(jnp.float32)).astype(x.dtype)


def layernorm(x, g, b):
    xf = x.astype(jnp.float32)
    xc = xf - jnp.mean(xf, axis=-1, keepdims=True)
    y = xc * lax.rsqrt(jnp.mean(xc * xc, axis=-1, keepdims=True) + EPS)
    return (y * g.astype(jnp.float32) + b.astype(jnp.float32)).astype(x.dtype)


def causal_dwconv(buf, u, w, b=None):
    full = jnp.concatenate([buf.astype(u.dtype), u], axis=1)
    out = lax.conv_general_dilated(full, w.astype(u.dtype)[:, None, :], (1,), 'VALID',
                                   dimension_numbers=('NWC', 'WIO', 'NWC'),
                                   feature_group_count=u.shape[-1])
    if b is not None:
        out = out + b.astype(u.dtype)
    return out, full[:, full.shape[1] - (w.shape[0] - 1):]


def alibi_slopes(n):
    return 2.0 ** (-8.0 * jnp.arange(1, n + 1, dtype=jnp.float32) / n)


def masked_softmax(s, mask):
    p = jax.nn.softmax(jnp.where(mask, s, -1e30), axis=-1)
    return p * mask


def even_mixer(h, buf_a, buf_b, w_in, conv_a, conv_b, conv_b_bias, ln_g, ln_b, w_out):
    u = h @ w_in
    xa, ba, ca, pb, gb = jnp.split(u, [D_A, 2 * D_A, 3 * D_A, 3 * D_A + D_B], axis=-1)
    conv_out_a, new_a = causal_dwconv(buf_a, ca * xa, conv_a)
    y_a = ba * conv_out_a
    conv_out_b, new_b = causal_dwconv(buf_b, pb * jax.nn.sigmoid(gb), conv_b, conv_b_bias)
    y_b = jax.nn.silu(layernorm(conv_out_b, ln_g, ln_b))
    return jnp.concatenate([y_a, y_b], axis=-1) @ w_out, new_a, new_b


def compress(k, pe, w):
    b_, s_ = k.shape[:2]
    nc = s_ // L_CMP
    kb = k[:, :nc * L_CMP].reshape(b_, nc, L_CMP, NSA_KV, NSA_HD) + pe[None, None, :, None, :]
    return jnp.einsum('bnlgd,lde->bnge', kb, w), jnp.arange(nc) * L_CMP + (L_CMP - 1)


def nsa_block(q, q_pos, kcb, vcb, c_pos, ks_t, vs_t, kw, vw, w_pos, gates):
    b_, nq = q.shape[:2]
    rep = NSA_HEADS // NSA_KV
    qg = q.reshape(b_, nq, NSA_KV, rep, NSA_HD)
    scale = NSA_HD ** -0.5
    slopes = alibi_slopes(NSA_HEADS).reshape(NSA_KV, rep)[None, :, :, None, None]
    d_c = q_pos[:, None] - c_pos[None, :]
    s_c = jnp.einsum('bqgrd,bngd->bgrqn', qg, kcb).astype(jnp.float32) * scale - slopes * d_c.astype(jnp.float32)
    p_c = masked_softmax(s_c, d_c >= 0)
    o_c = jnp.einsum('bgrqn,bngd->bqgrd', p_c.astype(vcb.dtype), vcb)
    ns = ks_t.shape[2] // L_SEL
    per = L_SEL // L_CMP
    nc = kcb.shape[1]
    imp = jnp.pad(p_c.sum(axis=2), ((0, 0), (0, 0), (0, 0), (0, ns * per - nc)))
    imp = imp.reshape(b_, NSA_KV, nq, ns, per).sum(-1)
    blk = jnp.arange(ns)[None, :]
    cur = (q_pos // L_SEL)[:, None]
    valid = blk <= cur
    forced = valid & ((blk == 0) | (blk >= cur - 1))
    score = jnp.where(forced, jnp.inf, jnp.where(valid, imp, -jnp.inf))
    k_eff = min(N_SEL, ns)
    top_v, top_i = lax.top_k(score, k_eff)
    tok = (top_i[..., None] * L_SEL + jnp.arange(L_SEL)).reshape(b_, NSA_KV, nq, k_eff * L_SEL)
    ok = jnp.repeat(top_v > -jnp.inf, L_SEL, axis=-1) & (tok <= q_pos[:, None])
    take = jax.vmap(jax.vmap(lambda rows, idx: rows[idx]))
    k_sel = take(ks_t, tok)
    v_sel = take(vs_t, tok)
    d_s = (q_pos[:, None] - tok)[:, :, None]
    s_s = jnp.einsum('bqgrd,bgqkd->bgrqk', qg, k_sel).astype(jnp.float32) * scale - slopes * d_s.astype(jnp.float32)
    p_s = masked_softmax(s_s, ok[:, :, None])
    o_s = jnp.einsum('bgrqk,bgqkd->bqgrd', p_s.astype(v_sel.dtype), v_sel)
    d_w = q_pos[:, None] - w_pos[None, :]
    m_w = (d_w >= 0) & (d_w <= WINDOW) & (w_pos[None, :] >= 0)
    s_w = jnp.einsum('bqgrd,bkgd->bgrqk', qg, kw).astype(jnp.float32) * scale - slopes * d_w.astype(jnp.float32)
    p_w = masked_softmax(s_w, m_w)
    o_w = jnp.einsum('bgrqk,bkgd->bqgrd', p_w.astype(vw.dtype), vw)
    g = gates.reshape(b_, nq, NSA_KV, rep, 3)
    o = g[..., 0:1] * o_c + g[..., 1:2] * o_s + g[..., 2:3] * o_w
    return o.reshape(b_, nq, NSA_W)


def nsa_prompt(q, kc, vc, ks, vs, kw, vw, gates, cmp_w):
    pe, wck, wcv = cmp_w
    b_, t_ = q.shape[:2]
    kcb, c_pos = compress(kc, pe, wck)
    vcb, _ = compress(vc, pe, wcv)
    s_pad = -(-t_ // L_SEL) * L_SEL
    sp = ((0, 0), (0, s_pad - t_), (0, 0), (0, 0))
    ks_t = jnp.pad(ks, sp).transpose(0, 2, 1, 3)
    vs_t = jnp.pad(vs, sp).transpose(0, 2, 1, 3)
    wp = ((0, 0), (WINDOW, 0), (0, 0), (0, 0))
    kw_p = jnp.pad(kw, wp)
    vw_p = jnp.pad(vw, wp)

    def block(i):
        st = i * Q_BLOCK
        q_pos = st + jnp.arange(Q_BLOCK)
        w_pos = st - WINDOW + jnp.arange(Q_BLOCK + WINDOW)
        sl = lambda a, n: lax.dynamic_slice_in_dim(a, st, n, axis=1)
        return nsa_block(sl(q, Q_BLOCK), q_pos, kcb, vcb, c_pos, ks_t, vs_t,
                         sl(kw_p, Q_BLOCK + WINDOW), sl(vw_p, Q_BLOCK + WINDOW), w_pos, sl(gates, Q_BLOCK))

    out = lax.map(block, jnp.arange(t_ // Q_BLOCK))
    keep = min(WINDOW, t_)
    return jnp.moveaxis(out, 0, 1).reshape(b_, t_, NSA_W), kw[:, t_ - keep:], vw[:, t_ - keep:]


def nsa_sample(q, kc, vc, ks, vs, kw, vw, gates, pools, page_table, win_kv, cmp_w):
    pe, wck, wcv = cmp_w
    b_, t_ = q.shape[:2]
    past_len = page_table.shape[1] * pools[0].shape[1]

    def with_past(pool, new):
        past = pool[page_table].reshape(b_, past_len, NSA_KV, NSA_HD)
        return jnp.concatenate([past.astype(new.dtype), new], axis=1)

    kc_a, vc_a, ks_a, vs_a = [with_past(p, n) for p, n in zip(pools, (kc, vc, ks, vs))]
    s_len = past_len + t_
    kcb, c_pos = compress(kc_a, pe, wck)
    vcb, _ = compress(vc_a, pe, wcv)
    s_pad = -(-s_len // L_SEL) * L_SEL
    sp = ((0, 0), (0, s_pad - s_len), (0, 0), (0, 0))
    ks_t = jnp.pad(ks_a, sp).transpose(0, 2, 1, 3)
    vs_t = jnp.pad(vs_a, sp).transpose(0, 2, 1, 3)
    win_k, win_v = win_kv
    wb = win_k.shape[1]
    kw_a = jnp.concatenate([win_k.astype(kw.dtype), kw], axis=1)
    vw_a = jnp.concatenate([win_v.astype(vw.dtype), vw], axis=1)
    w_pos = past_len - wb + jnp.arange(wb + t_)
    q_pos = past_len + jnp.arange(t_)
    o = nsa_block(q, q_pos, kcb, vcb, c_pos, ks_t, vs_t, kw_a, vw_a, w_pos, gates)
    return o, kw_a[:, t_:], vw_a[:, t_:]


def ssd(x, dt, a, bm, cm, h0):
    b_, l_, nh, p_ = x.shape
    q_len = SSD_CHUNK if l_ % SSD_CHUNK == 0 else l_
    nc = l_ // q_len
    rep = nh // M_GROUPS
    bh = jnp.repeat(bm.astype(jnp.float32), rep, axis=2).reshape(b_, nc, q_len, nh, M_DSTATE)
    ch = jnp.repeat(cm.astype(jnp.float32), rep, axis=2).reshape(b_, nc, q_len, nh, M_DSTATE)
    xdt = (x.astype(jnp.float32) * dt[..., None]).reshape(b_, nc, q_len, nh, p_)
    cum = jnp.cumsum((dt * a).reshape(b_, nc, q_len, nh), axis=2)
    causal = jnp.tril(jnp.ones((q_len, q_len), bool))[None, None, :, :, None]
    diff = cum[:, :, :, None, :] - cum[:, :, None, :, :]
    lmat = jnp.exp(jnp.where(causal, diff, -jnp.inf))
    cb = jnp.einsum('bcthn,bcshn->bctsh', ch, bh) * lmat
    y_diag = jnp.einsum('bctsh,bcshp->bcthp', cb, xdt)
    decay_end = jnp.exp(cum[:, :, -1:, :] - cum)
    s_chunk = jnp.einsum('bcsh,bcshn,bcshp->bchpn', decay_end, bh, xdt)
    chunk_decay = jnp.exp(cum[:, :, -1, :])

    def step(h, inp):
        dec, s_c = inp
        return dec[:, :, None, None] * h + s_c, h

    h_fin, h_in = lax.scan(step, h0.astype(jnp.float32),
                           (jnp.moveaxis(chunk_decay, 1, 0), jnp.moveaxis(s_chunk, 1, 0)))
    h_in = jnp.moveaxis(h_in, 0, 1)
    y_off = jnp.einsum('bcthn,bchpn,bcth->bcthp', ch, h_in, jnp.exp(cum))
    y = (y_diag + y_off).reshape(b_, l_, nh, p_)
    return y.astype(x.dtype), h_fin.astype(h0.dtype)


def mamba_mixer(z, xbc, dt_raw, conv_buf, h0, conv_w, conv_b, dt_bias, a_log, d_skip, norm_g):
    b_, t_ = z.shape[:2]
    xbc_c, new_buf = causal_dwconv(conv_buf, xbc, conv_w, conv_b)
    xbc_c = jax.nn.silu(xbc_c)
    xs, bm, cm = jnp.split(xbc_c, [M_DIN, M_DIN + M_GROUPS * M_DSTATE], axis=-1)
    xs = xs.reshape(b_, t_, M_HEADS, M_HDIM)
    bm = bm.reshape(b_, t_, M_GROUPS, M_DSTATE)
    cm = cm.reshape(b_, t_, M_GROUPS, M_DSTATE)
    dt = jax.nn.softplus(dt_raw.astype(jnp.float32) + dt_bias.astype(jnp.float32))
    a = -jnp.exp(a_log.astype(jnp.float32))
    y, h_fin = ssd(xs, dt, a, bm, cm, h0)
    y = (y + d_skip[:, None] * xs).reshape(b_, t_, M_DIN)
    return rmsnorm(y * jax.nn.silu(z), norm_g), new_buf, h_fin


def odd_mixer(h, w_in, w_out, nsa_fn, ssm_conv_buf, ssm_h0, mamba_w):
    b_, t_ = h.shape[:2]
    sizes = [NSA_W] + [NSA_KV * NSA_HD] * 6 + [3 * NSA_HEADS, M_DIN, M_CONV_DIM, M_HEADS]
    parts = jnp.split(h @ w_in, np.cumsum(sizes)[:-1].tolist(), axis=-1)
    q = parts[0].reshape(b_, t_, NSA_HEADS, NSA_HD)
    kvs = [p.reshape(b_, t_, NSA_KV, NSA_HD) for p in parts[1:7]]
    gates = jax.nn.sigmoid(parts[7].astype(jnp.float32)).reshape(b_, t_, NSA_HEADS, 3).astype(h.dtype)
    o_nsa, win_k, win_v = nsa_fn(q, *kvs, gates)
    y_ssm, new_conv, h_fin = mamba_mixer(parts[8], parts[9], parts[10], ssm_conv_buf, ssm_h0, *mamba_w)
    out = jnp.concatenate([o_nsa, y_ssm], axis=-1) @ w_out
    return out, kvs[:4], win_k, win_v, new_conv, h_fin


def cross_attn(h, k, v, wq, wo):
    b_, t_ = h.shape[:2]
    q = (h @ wq).reshape(b_, t_, XA_HEADS, XA_HD)
    s = jnp.einsum('bthd,bmhd->bhtm', q, k.astype(q.dtype)).astype(jnp.float32) * XA_HD ** -0.5
    p = jax.nn.softmax(s, axis=-1)
    o = jnp.einsum('bhtm,bmhd->bthd', p.astype(h.dtype), v.astype(h.dtype))
    return o.reshape(b_, t_, XA_W) @ wo


def moe(h, wg, bg, we, be, w1, w3, w2):
    n = h.shape[0]
    lg = (h @ wg).astype(jnp.float32) + bg.astype(jnp.float32)
    grp = jnp.argmax(lg, axis=-1)
    gw = jnp.take_along_axis(jax.nn.softmax(lg, axis=-1), grp[:, None], axis=1)
    le = ((h @ we).astype(jnp.float32) + be.astype(jnp.float32)).reshape(n, MOE_GROUPS, MOE_EPG)
    le = jnp.take_along_axis(le, grp[:, None, None], axis=1)[:, 0]
    tv, ti = lax.top_k(jax.nn.softmax(le, axis=-1), MOE_TOPK)
    wts = gw * tv / jnp.sum(tv, axis=-1, keepdims=True)
    eid = (grp[:, None] * MOE_EPG + ti).reshape(-1).astype(jnp.int32)
    tok = jnp.repeat(jnp.arange(n, dtype=jnp.int32), MOE_TOPK)
    gate = wts.reshape(-1).astype(h.dtype)
    npair = n * MOE_TOPK
    order = jnp.argsort(eid)
    se = eid[order]
    counts = jax.ops.segment_sum(jnp.ones_like(eid), eid, num_segments=MOE_E)
    start = jnp.cumsum(counts) - counts
    padded = (counts + MOE_BLOCK - 1) // MOE_BLOCK * MOE_BLOCK
    pend = jnp.cumsum(padded)
    pstart = pend - padded
    dest = pstart[se] + jnp.arange(npair, dtype=jnp.int32) - start[se]
    nb = -(-npair // MOE_BLOCK) + MOE_E
    buf_tok = jnp.full((nb * MOE_BLOCK,), n, jnp.int32).at[dest].set(tok[order])
    buf_gate = jnp.zeros((nb * MOE_BLOCK,), h.dtype).at[dest].set(gate[order])
    blk_exp = jnp.minimum(jnp.searchsorted(pend, jnp.arange(nb, dtype=jnp.int32) * MOE_BLOCK, side='right'), MOE_E - 1)
    x_pad = jnp.concatenate([h, jnp.zeros((1, h.shape[1]), h.dtype)], axis=0)

    def run(b):
        rows = x_pad[lax.dynamic_slice_in_dim(buf_tok, b * MOE_BLOCK, MOE_BLOCK)]
        e = blk_exp[b]
        hid = jax.nn.silu(rows @ w1[e]) * (rows @ w3[e])
        return hid @ w2[e]

    out = lax.map(run, jnp.arange(nb)).reshape(nb * MOE_BLOCK, h.shape[1])
    return jax.ops.segment_sum(out * buf_gate[:, None], buf_tok, num_segments=n + 1)[:n]


def setup_inputs(seed: int = 0) -> dict:
    key = jax.random.key(seed)
    it = iter(list(jax.random.split(key, 80)))

    def nrm(shape, scale=1.0):
        return scale * jax.random.normal(next(it), shape, jnp.float32)

    def gain(shape):
        return 1.0 + 0.02 * nrm(shape)

    n_pages = PAST_LEN // PAGE_SIZE
    n_used = DEC_BATCH * n_pages
    n_pool = n_used + max(1, n_used // 4)
    w_buf = min(WINDOW, PAST_LEN)
    paged = (N_ODD, n_pool, PAGE_SIZE, NSA_KV, NSA_HD)
    inp = {}
    inp['x_prompt'] = nrm((BATCH, SEQ, D_MODEL))
    inp['x_sample'] = nrm((DEC_BATCH, DEC_SEQ, D_MODEL))
    inp['state_conv_a'] = nrm((N_EVEN, DEC_BATCH, CONV_A_W - 1, D_A))
    inp['state_conv_b'] = nrm((N_EVEN, DEC_BATCH, CONV_B_W - 1, D_B))
    inp['cache_cmp_k'] = nrm(paged)
    inp['cache_cmp_v'] = nrm(paged)
    inp['cache_sel_k'] = nrm(paged)
    inp['cache_sel_v'] = nrm(paged)
    inp['cache_win_k'] = nrm((N_ODD, DEC_BATCH, w_buf, NSA_KV, NSA_HD))
    inp['cache_win_v'] = nrm((N_ODD, DEC_BATCH, w_buf, NSA_KV, NSA_HD))
    inp['state_ssm'] = nrm((N_ODD, DEC_BATCH, M_HEADS, M_HDIM, M_DSTATE), 0.5)
    inp['state_ssm_conv'] = nrm((N_ODD, DEC_BATCH, M_CONV_W - 1, M_CONV_DIM))
    inp['cache_mem_k'] = nrm((DEPTH, DEC_BATCH, N_MEM, XA_HEADS, XA_HD))
    inp['cache_mem_v'] = nrm((DEPTH, DEC_BATCH, N_MEM, XA_HEADS, XA_HD))
    inp['page_table'] = jax.random.permutation(next(it), n_pool)[:n_used].reshape(DEC_BATCH, n_pages).astype(jnp.int32)
    inp['mem_prompt'] = nrm((BATCH, N_MEM, D_MODEL))
    inp['norm_mix'] = gain((DEPTH, D_MODEL))
    inp['norm_xattn'] = gain((DEPTH, D_MODEL))
    inp['norm_ffn'] = gain((DEPTH, D_MODEL))
    inp['norm_final'] = gain((D_MODEL,))
    inp['ev_w_in'] = nrm((N_EVEN, D_MODEL, EV_IN), D_MODEL ** -0.5)
    inp['ev_conv_a'] = nrm((N_EVEN, CONV_A_W, D_A), CONV_A_W ** -0.5)
    inp['ev_conv_b'] = nrm((N_EVEN, CONV_B_W, D_B), CONV_B_W ** -0.5)
    inp['ev_conv_b_bias'] = nrm((N_EVEN, D_B), 0.02)
    inp['ev_ln_g'] = gain((N_EVEN, D_B))
    inp['ev_ln_b'] = nrm((N_EVEN, D_B), 0.02)
    inp['ev_w_out'] = nrm((N_EVEN, D_A + D_B, D_MODEL), (D_A + D_B) ** -0.5)
    inp['od_w_in'] = nrm((N_ODD, D_MODEL, OD_IN), D_MODEL ** -0.5)
    inp['od_cmp_pe'] = nrm((N_ODD, L_CMP, NSA_HD), 0.1)
    inp['od_cmp_wk'] = nrm((N_ODD, L_CMP, NSA_HD, NSA_HD), (L_CMP * NSA_HD) ** -0.5)
    inp['od_cmp_wv'] = nrm((N_ODD, L_CMP, NSA_HD, NSA_HD), (L_CMP * NSA_HD) ** -0.5)
    inp['od_ssm_conv_w'] = nrm((N_ODD, M_CONV_W, M_CONV_DIM), 0.5)
    inp['od_ssm_conv_b'] = nrm((N_ODD, M_CONV_DIM), 0.02)
    dt0 = jnp.exp(jax.random.uniform(next(it), (N_ODD, M_HEADS), jnp.float32, np.log(1e-3), np.log(1e-1)))
    inp['od_dt_bias'] = dt0 + jnp.log(-jnp.expm1(-dt0))
    inp['od_a_log'] = jnp.log(jax.random.uniform(next(it), (N_ODD, M_HEADS), jnp.float32, 1.0, 16.0))
    inp['od_d_skip'] = gain((N_ODD, M_HEADS))
    inp['od_ssm_norm'] = gain((N_ODD, M_DIN))
    inp['od_w_out'] = nrm((N_ODD, NSA_W + M_DIN, D_MODEL), (NSA_W + M_DIN) ** -0.5)
    inp['xa_wq'] = nrm((DEPTH, D_MODEL, XA_W), D_MODEL ** -0.5)
    inp['xa_wk'] = nrm((DEPTH, D_MODEL, XA_W), D_MODEL ** -0.5)
    inp['xa_wv'] = nrm((DEPTH, D_MODEL, XA_W), D_MODEL ** -0.5)
    inp['xa_wo'] = nrm((DEPTH, XA_W, D_MODEL), XA_W ** -0.5)
    inp['moe_wg'] = nrm((DEPTH, D_MODEL, MOE_GROUPS), D_MODEL ** -0.5)
    inp['moe_bg'] = nrm((DEPTH, MOE_GROUPS), 0.01)
    inp['moe_we'] = nrm((DEPTH, D_MODEL, MOE_E), D_MODEL ** -0.5)
    inp['moe_be'] = nrm((DEPTH, MOE_E), 0.01)
    inp['moe_w1'] = nrm((DEPTH, MOE_E, D_MODEL, MOE_FF), D_MODEL ** -0.5)
    inp['moe_w3'] = nrm((DEPTH, MOE_E, D_MODEL, MOE_FF), D_MODEL ** -0.5)
    inp['moe_w2'] = nrm((DEPTH, MOE_E, MOE_FF, D_MODEL), MOE_FF ** -0.5)
    return inp


def reference(x_prompt, x_sample, state_conv_a, state_conv_b, cache_cmp_k, cache_cmp_v, cache_sel_k,
              cache_sel_v, cache_win_k, cache_win_v, state_ssm, state_ssm_conv, cache_mem_k, cache_mem_v,
              page_table, mem_prompt, norm_mix, norm_xattn, norm_ffn, norm_final, ev_w_in, ev_conv_a,
              ev_conv_b, ev_conv_b_bias, ev_ln_g, ev_ln_b, ev_w_out, od_w_in, od_cmp_pe, od_cmp_wk,
              od_cmp_wv, od_ssm_conv_w, od_ssm_conv_b, od_dt_bias, od_a_log, od_d_skip, od_ssm_norm,
              od_w_out, xa_wq, xa_wk, xa_wv, xa_wo, moe_wg, moe_bg, moe_we, moe_be, moe_w1, moe_w3, moe_w2):
    bp = x_prompt.shape[0]
    n_mem = mem_prompt.shape[1]
    hp, hs = x_prompt, x_sample
    ca_p, ca_s, cb_p, cb_s = [], [], [], []
    rows_p = [[], [], [], []]
    rows_s = [[], [], [], []]
    wk_p, wk_s, wv_p, wv_s = [], [], [], []
    sm_p, sm_s, sc_p, sc_s = [], [], [], []
    mk_p, mv_p = [], []
    for i in range(DEPTH):
        j = i // 2
        if i % 2 == 0:
            ev = (ev_w_in[j], ev_conv_a[j], ev_conv_b[j], ev_conv_b_bias[j], ev_ln_g[j], ev_ln_b[j], ev_w_out[j])
            o, na, nb = even_mixer(rmsnorm(hp, norm_mix[i]),
                                   jnp.zeros((bp, CONV_A_W - 1, D_A), hp.dtype),
                                   jnp.zeros((bp, CONV_B_W - 1, D_B), hp.dtype), *ev)
            hp = hp + o
            ca_p.append(na)
            cb_p.append(nb)
            o, na, nb = even_mixer(rmsnorm(hs, norm_mix[i]), state_conv_a[j], state_conv_b[j], *ev)
            hs = hs + o
            ca_s.append(na)
            cb_s.append(nb)
        else:
            mw = (od_ssm_conv_w[j], od_ssm_conv_b[j], od_dt_bias[j], od_a_log[j], od_d_skip[j], od_ssm_norm[j])
            cw = (od_cmp_pe[j], od_cmp_wk[j], od_cmp_wv[j])
            nsa_p = functools.partial(nsa_prompt, cmp_w=cw)
            o, kv, nwk, nwv, nsc, nsm = odd_mixer(
                rmsnorm(hp, norm_mix[i]), od_w_in[j], od_w_out[j], nsa_p,
                jnp.zeros((bp, M_CONV_W - 1, M_CONV_DIM), hp.dtype),
                jnp.zeros((bp, M_HEADS, M_HDIM, M_DSTATE), hp.dtype), mw)
            hp = hp + o
            for lst, r in zip(rows_p, kv):
                lst.append(r)
            wk_p.append(nwk)
            wv_p.append(nwv)
            sc_p.append(nsc)
            sm_p.append(nsm)
            nsa_s = functools.partial(
                nsa_sample, pools=(cache_cmp_k[j], cache_cmp_v[j], cache_sel_k[j], cache_sel_v[j]),
                page_table=page_table, win_kv=(cache_win_k[j], cache_win_v[j]), cmp_w=cw)
            o, kv, nwk, nwv, nsc, nsm = odd_mixer(
                rmsnorm(hs, norm_mix[i]), od_w_in[j], od_w_out[j], nsa_s,
                state_ssm_conv[j], state_ssm[j], mw)
            hs = hs + o
            for lst, r in zip(rows_s, kv):
                lst.append(r)
            wk_s.append(nwk)
            wv_s.append(nwv)
            sc_s.append(nsc)
            sm_s.append(nsm)
        mk = (mem_prompt @ xa_wk[i]).reshape(bp, n_mem, XA_HEADS, XA_HD)
        mv = (mem_prompt @ xa_wv[i]).reshape(bp, n_mem, XA_HEADS, XA_HD)
        mk_p.append(mk)
        mv_p.append(mv)
        hp = hp + cross_attn(rmsnorm(hp, norm_xattn[i]), mk, mv, xa_wq[i], xa_wo[i])
        hs = hs + cross_attn(rmsnorm(hs, norm_xattn[i]), cache_mem_k[i], cache_mem_v[i], xa_wq[i], xa_wo[i])
        mo = (moe_wg[i], moe_bg[i], moe_we[i], moe_be[i], moe_w1[i], moe_w3[i], moe_w2[i])
        hp = hp + moe(rmsnorm(hp, norm_ffn[i]).reshape(-1, D_MODEL), *mo).reshape(hp.shape)
        hs = hs + moe(rmsnorm(hs, norm_ffn[i]).reshape(-1, D_MODEL), *mo).reshape(hs.shape)
    y_prompt = rmsnorm(hp, norm_final)
    y_sample = rmsnorm(hs, norm_final)
    conv_a_p = jnp.stack(ca_p)
    conv_a_s = jnp.stack(ca_s)
    conv_b_p = jnp.stack(cb_p)
    conv_b_s = jnp.stack(cb_s)
    cmp_k_p = jnp.stack(rows_p[0])
    cmp_k_s = jnp.stack(rows_s[0])
    cmp_v_p = jnp.stack(rows_p[1])
    cmp_v_s = jnp.stack(rows_s[1])
    sel_k_p = jnp.stack(rows_p[2])
    sel_k_s = jnp.stack(rows_s[2])
    sel_v_p = jnp.stack(rows_p[3])
    sel_v_s = jnp.stack(rows_s[3])
    win_k_p = jnp.stack(wk_p)
    win_k_s = jnp.stack(wk_s)
    win_v_p = jnp.stack(wv_p)
    win_v_s = jnp.stack(wv_s)
    ssm_p = jnp.stack(sm_p)
    ssm_s = jnp.stack(sm_s)
    ssm_conv_p = jnp.stack(sc_p)
    ssm_conv_s = jnp.stack(sc_s)
    mem_k_p = jnp.stack(mk_p)
    mem_v_p = jnp.stack(mv_p)
    return (y_prompt, y_sample, conv_a_p, conv_a_s, conv_b_p, conv_b_s,
            cmp_k_p, cmp_k_s, cmp_v_p, cmp_v_s, sel_k_p, sel_k_s, sel_v_p, sel_v_s,
            win_k_p, win_k_s, win_v_p, win_v_s, ssm_p, ssm_s, ssm_conv_p, ssm_conv_s,
            mem_k_p, mem_v_p)
```

```python
import functools

import jax
import jax.numpy as jnp
from jax import lax
from jax.experimental import pallas as pl
from jax.experimental.pallas import tpu as pltpu

F32 = jnp.float32
BF16 = jnp.bfloat16
EPS = 1e-6
NEG = -1e30
VMEM_LIMIT = 56 * 1024 * 1024

D_A = 512
D_B = 512
CONV_A_W = 3
CONV_B_W = 31
NSA_HEADS = 8
NSA_HD = 64
NSA_KV = 2
NSA_REP = NSA_HEADS // NSA_KV
NSA_W = NSA_HEADS * NSA_HD
NSA_KW = NSA_KV * NSA_HD
L_CMP = 32
L_SEL = 64
N_SEL = 16
WINDOW = 512
M_DIN = 512
M_HDIM = 64
M_HEADS = 8
M_DSTATE = 64
M_GROUPS = 2
M_CONV_W = 4
M_CONV_DIM = M_DIN + 2 * M_GROUPS * M_DSTATE
XA_HEADS = 4
XA_HD = 128
MOE_GROUPS = 4
MOE_EPG = 8
MOE_E = 32
MOE_TOPK = 2


def _params(*sem):
    return pltpu.CompilerParams(dimension_semantics=sem, vmem_limit_bytes=VMEM_LIMIT)


def _row_tile(n, pref):
    t = min(n, pref)
    while n % t or (t % 8 and t != n):
        t -= 1
    return t


def _bdot(a, b):
    return jnp.dot(a.astype(BF16), b.astype(BF16), preferred_element_type=F32)


def _split3(a):
    hi = a.astype(BF16)
    r1 = a - hi.astype(F32)
    mid = r1.astype(BF16)
    lo = (r1 - mid.astype(F32)).astype(BF16)
    return hi, mid, lo


def _dot_exact_rhs(a, b_bf16):
    hi, mid, lo = _split3(a)
    d = lambda x: jnp.dot(x, b_bf16, preferred_element_type=F32)
    return d(hi) + d(mid) + d(lo)


def _dot_exact_lhs(a_bf16, b):
    hi, mid, lo = _split3(b)
    d = lambda x: jnp.dot(a_bf16, x, preferred_element_type=F32)
    return d(hi) + d(mid) + d(lo)


def _rms(x, g):
    ms = jnp.mean(x * x, axis=-1, keepdims=True)
    return x * lax.rsqrt(ms + EPS) * g


def _sigmoid(x):
    return 1.0 / (1.0 + jnp.exp(-x))


def _silu(x):
    return x * _sigmoid(x)


def _norm_matmul_kernel(x_ref, g_ref, w_ref, *o_refs, norm, splits):
    x = x_ref[...]
    if norm:
        x = _rms(x, g_ref[...])
    res = jnp.dot(x.astype(BF16), w_ref[...].astype(BF16), preferred_element_type=F32)
    off = 0
    for o_ref, width in zip(o_refs, splits):
        o_ref[...] = res[:, off:off + width]
        off += width


def norm_matmul(x, g, w, *, norm=True, splits=None, tm=512):
    n, k = x.shape
    m = w.shape[1]
    tm = _row_tile(n, tm)
    if g is None:
        g = jnp.ones((k,), F32)
    widths = (m,) if splits is None else tuple(splits)
    assert sum(widths) == m
    outs = pl.pallas_call(
        functools.partial(_norm_matmul_kernel, norm=norm, splits=widths),
        grid=(n // tm,),
        in_specs=[pl.BlockSpec((tm, k), lambda i: (i, 0)),
                  pl.BlockSpec((1, k), lambda i: (0, 0)),
                  pl.BlockSpec((k, m), lambda i: (0, 0))],
        out_specs=[pl.BlockSpec((tm, wd), lambda i: (i, 0)) for wd in widths],
        out_shape=[jax.ShapeDtypeStruct((n, wd), F32) for wd in widths],
        compiler_params=_params("parallel"),
        name="norm_matmul",
    )(x, g.reshape(1, k), w)
    return outs[0] if splits is None else outs


def _rmsnorm_kernel(x_ref, g_ref, o_ref):
    o_ref[...] = _rms(x_ref[...], g_ref[...])


def rmsnorm_rows(x, g, *, tm=512):
    n, k = x.shape
    tm = _row_tile(n, tm)
    return pl.pallas_call(
        _rmsnorm_kernel,
        grid=(n // tm,),
        in_specs=[pl.BlockSpec((tm, k), lambda i: (i, 0)), pl.BlockSpec((1, k), lambda i: (0, 0))],
        out_specs=pl.BlockSpec((tm, k), lambda i: (i, 0)),
        out_shape=jax.ShapeDtypeStruct((n, k), F32),
        compiler_params=_params("parallel"),
        name="rmsnorm_rows",
    )(x, g.reshape(1, k))


def _matmul_res_kernel(*refs, n_in):
    res_ref = refs[2 * n_in]
    o_ref = refs[2 * n_in + 1]
    acc = res_ref[...]
    for j in range(n_in):
        acc = acc + jnp.dot(refs[2 * j][...].astype(BF16), refs[2 * j + 1][...].astype(BF16),
                            preferred_element_type=F32)
    o_ref[...] = acc


def matmul_res(xs, ws, res, *, tm=512):
    n, m = res.shape
    tm = _row_tile(n, tm)
    in_specs, args = [], []
    for x, w in zip(xs, ws):
        k = x.shape[1]
        in_specs += [pl.BlockSpec((tm, k), lambda i: (i, 0)), pl.BlockSpec((k, m), lambda i: (0, 0))]
        args += [x, w]
    in_specs.append(pl.BlockSpec((tm, m), lambda i: (i, 0)))
    return pl.pallas_call(
        functools.partial(_matmul_res_kernel, n_in=len(xs)),
        grid=(n // tm,),
        in_specs=in_specs,
        out_specs=pl.BlockSpec((tm, m), lambda i: (i, 0)),
        out_shape=jax.ShapeDtypeStruct((n, m), F32),
        compiler_params=_params("parallel"),
        name="matmul_res",
    )(*args, res)


_HALO_A = 8
_HALO_B = 32


def _even_conv_kernel(u_ref, sa_ref, sb_ref, wa_ref, wb_ref, bb_ref, lg_ref, lb_ref,
                      y_ref, na_ref, nb_ref, ea_ref, eb_ref, *, tt):
    t = pl.program_id(1)
    nt = pl.num_programs(1)

    @pl.when(t == 0)
    def _():
        ea_ref[...] = jnp.zeros_like(ea_ref)
        eb_ref[...] = jnp.zeros_like(eb_ref)
        ea_ref[_HALO_A - (CONV_A_W - 1):_HALO_A, :] = sa_ref[0]
        eb_ref[_HALO_B - (CONV_B_W - 1):_HALO_B, :] = sb_ref[0]

    @pl.when(t > 0)
    def _():
        ea_ref[0:_HALO_A, :] = ea_ref[tt:tt + _HALO_A, :]
        eb_ref[0:_HALO_B, :] = eb_ref[tt:tt + _HALO_B, :]

    xa = u_ref[0, :, 0:D_A]
    ba = u_ref[0, :, D_A:2 * D_A]
    ca = u_ref[0, :, 2 * D_A:3 * D_A]
    pb = u_ref[0, :, 3 * D_A:3 * D_A + D_B]
    gb = u_ref[0, :, 3 * D_A + D_B:3 * D_A + 2 * D_B]
    ea_ref[_HALO_A:_HALO_A + tt, :] = ca * xa
    eb_ref[_HALO_B:_HALO_B + tt, :] = pb * _sigmoid(gb)

    acc = jnp.zeros((tt, D_A), F32)
    for k in range(CONV_A_W):
        off = _HALO_A - (CONV_A_W - 1) + k
        acc = acc + ea_ref[off:off + tt, :] * wa_ref[k:k + 1, :]
    y_ref[0, :, 0:D_A] = ba * acc

    acc = jnp.zeros((tt, D_B), F32)
    for k in range(CONV_B_W):
        off = _HALO_B - (CONV_B_W - 1) + k
        acc = acc + eb_ref[off:off + tt, :] * wb_ref[k:k + 1, :]
    acc = acc + bb_ref[...]
    mu = jnp.mean(acc, axis=-1, keepdims=True)
    xc = acc - mu
    var = jnp.mean(xc * xc, axis=-1, keepdims=True)
    yb = xc * lax.rsqrt(var + EPS) * lg_ref[...] + lb_ref[...]
    y_ref[0, :, D_A:D_A + D_B] = _silu(yb)

    @pl.when(t == nt - 1)
    def _():
        na_ref[0] = ea_ref[_HALO_A + tt - (CONV_A_W - 1):_HALO_A + tt, :]
        nb_ref[0] = eb_ref[_HALO_B + tt - (CONV_B_W - 1):_HALO_B + tt, :]


def even_conv(u, sa, sb, wa, wb, bb, lg, lb, *, tt=256):
    b, t, w = u.shape
    tt = _row_tile(t, tt)
    full = lambda shape: pl.BlockSpec(shape, lambda i, j: (0,) * len(shape))
    return pl.pallas_call(
        functools.partial(_even_conv_kernel, tt=tt),
        grid=(b, t // tt),
        in_specs=[pl.BlockSpec((1, tt, w), lambda i, j: (i, j, 0)),
                  pl.BlockSpec((1, CONV_A_W - 1, D_A), lambda i, j: (i, 0, 0)),
                  pl.BlockSpec((1, CONV_B_W - 1, D_B), lambda i, j: (i, 0, 0)),
                  full((CONV_A_W, D_A)), full((CONV_B_W, D_B)), full((1, D_B)),
                  full((1, D_B)), full((1, D_B))],
        out_specs=[pl.BlockSpec((1, tt, D_A + D_B), lambda i, j: (i, j, 0)),
                   pl.BlockSpec((1, CONV_A_W - 1, D_A), lambda i, j: (i, 0, 0)),
                   pl.BlockSpec((1, CONV_B_W - 1, D_B), lambda i, j: (i, 0, 0))],
        out_shape=[jax.ShapeDtypeStruct((b, t, D_A + D_B), F32),
                   jax.ShapeDtypeStruct((b, CONV_A_W - 1, D_A), F32),
                   jax.ShapeDtypeStruct((b, CONV_B_W - 1, D_B), F32)],
        scratch_shapes=[pltpu.VMEM((_HALO_A + tt, D_A), F32), pltpu.VMEM((_HALO_B + tt, D_B), F32)],
        compiler_params=_params("parallel", "arbitrary"),
        name="even_conv",
    )(u, sa, sb, wa, wb, bb.reshape(1, D_B), lg.reshape(1, D_B), lb.reshape(1, D_B))


def _xattn_kernel(q_ref, k_ref, v_ref, o_ref):
    scale = XA_HD ** -0.5
    for h in range(XA_HEADS):
        sl = slice(h * XA_HD, (h + 1) * XA_HD)
        q = q_ref[0, :, sl].astype(BF16)
        k = k_ref[0, :, sl].astype(BF16)
        v = v_ref[0, :, sl].astype(BF16)
        s = lax.dot_general(q, k, (((1,), (1,)), ((), ())), preferred_element_type=F32) * scale
        m = jnp.max(s, axis=-1, keepdims=True)
        p = jnp.exp(s - m)
        p = p / jnp.sum(p, axis=-1, keepdims=True)
        o_ref[0, :, sl] = jnp.dot(p.astype(BF16), v, preferred_element_type=F32)


def xattn(q, k, v, *, tq=512):
    b, t, w = q.shape
    m = k.shape[1]
    tq = _row_tile(t, tq)
    return pl.pallas_call(
        _xattn_kernel,
        grid=(b, t // tq),
        in_specs=[pl.BlockSpec((1, tq, w), lambda i, j: (i, j, 0)),
                  pl.BlockSpec((1, m, w), lambda i, j: (i, 0, 0)),
                  pl.BlockSpec((1, m, w), lambda i, j: (i, 0, 0))],
        out_specs=pl.BlockSpec((1, tq, w), lambda i, j: (i, j, 0)),
        out_shape=jax.ShapeDtypeStruct((b, t, w), F32),
        compiler_params=_params("parallel", "parallel"),
        name="xattn",
    )(q, k, v)


MOE_BLK = 256
_ROUTER_W = 128


def _router_kernel(x_ref, g_ref, w_ref, lg_ref, xn_ref):
    x = _rms(x_ref[...], g_ref[...])
    xn_ref[...] = x
    w = w_ref[...]
    xh = x.astype(BF16)
    xl = (x - xh.astype(F32)).astype(BF16)
    wh = w.astype(BF16)
    wl = (w - wh.astype(F32)).astype(BF16)
    d = lambda a, b: jnp.dot(a, b, preferred_element_type=F32)
    lg_ref[...] = d(xh, wh) + (d(xh, wl) + d(xl, wh))


def moe_router(x, g, w_router, *, tm=512):
    n, k = x.shape
    tm = _row_tile(n, tm)
    return pl.pallas_call(
        _router_kernel,
        grid=(n // tm,),
        in_specs=[pl.BlockSpec((tm, k), lambda i: (i, 0)),
                  pl.BlockSpec((1, k), lambda i: (0, 0)),
                  pl.BlockSpec((k, _ROUTER_W), lambda i: (0, 0))],
        out_specs=[pl.BlockSpec((tm, _ROUTER_W), lambda i: (i, 0)),
                   pl.BlockSpec((tm, k), lambda i: (i, 0))],
        out_shape=[jax.ShapeDtypeStruct((n, _ROUTER_W), F32), jax.ShapeDtypeStruct((n, k), F32)],
        compiler_params=_params("parallel"),
        name="moe_router",
    )(x, g.reshape(1, k), w_router)


def _expert_kernel(be_ref, act_ref, x_ref, gate_ref, w1_ref, w3_ref, w2_ref, o_ref):
    i = pl.program_id(0)

    @pl.when(act_ref[i] > 0)
    def _():
        x = x_ref[...].astype(BF16)
        h1 = jnp.dot(x, w1_ref[0].astype(BF16), preferred_element_type=F32)
        h3 = jnp.dot(x, w3_ref[0].astype(BF16), preferred_element_type=F32)
        hid = (_silu(h1) * h3).astype(BF16)
        out = jnp.dot(hid, w2_ref[0].astype(BF16), preferred_element_type=F32)
        o_ref[...] = out * gate_ref[...]

    @pl.when(act_ref[i] == 0)
    def _():
        o_ref[...] = jnp.zeros_like(o_ref)


def moe_experts(xg, gate, blk_exp, blk_act, w1, w3, w2):
    rows, d = xg.shape
    nb = rows // MOE_BLK
    ff = w1.shape[2]
    return pl.pallas_call(
        _expert_kernel,
        grid_spec=pltpu.PrefetchScalarGridSpec(
            num_scalar_prefetch=2,
            grid=(nb,),
            in_specs=[pl.BlockSpec((MOE_BLK, d), lambda i, be, act: (i, 0)),
                      pl.BlockSpec((MOE_BLK, 1), lambda i, be, act: (i, 0)),
                      pl.BlockSpec((1, d, ff), lambda i, be, act: (be[i], 0, 0)),
                      pl.BlockSpec((1, d, ff), lambda i, be, act: (be[i], 0, 0)),
                      pl.BlockSpec((1, ff, d), lambda i, be, act: (be[i], 0, 0))],
            out_specs=pl.BlockSpec((MOE_BLK, d), lambda i, be, act: (i, 0)),
        ),
        out_shape=jax.ShapeDtypeStruct((rows, d), F32),
        compiler_params=_params("arbitrary"),
        name="moe_experts",
    )(blk_exp, blk_act, xg, gate, w1, w3, w2)


def moe_layer(h, g, wg, bg, we, be, w1, w3, w2):
    n, d = h.shape
    w_router = jnp.concatenate([wg, we, jnp.zeros((d, _ROUTER_W - MOE_GROUPS - MOE_E), F32)], axis=1)
    logits, xn = moe_router(h, g, w_router)
    lg = logits[:, :MOE_GROUPS] + bg
    grp = jnp.argmax(lg, axis=-1)
    gw = jnp.take_along_axis(jax.nn.softmax(lg, axis=-1), grp[:, None], axis=1)
    le = (logits[:, MOE_GROUPS:MOE_GROUPS + MOE_E] + be).reshape(n, MOE_GROUPS, MOE_EPG)
    le = jnp.take_along_axis(le, grp[:, None, None], axis=1)[:, 0]
    tv, ti = lax.top_k(jax.nn.softmax(le, axis=-1), MOE_TOPK)
    wts = gw * tv / jnp.sum(tv, axis=-1, keepdims=True)
    eid = (grp[:, None] * MOE_EPG + ti).reshape(-1).astype(jnp.int32)
    npair = n * MOE_TOPK
    tok = jnp.repeat(jnp.arange(n, dtype=jnp.int32), MOE_TOPK)
    order = jnp.argsort(eid)
    se = eid[order]
    counts = jnp.sum(eid[:, None] == jnp.arange(MOE_E, dtype=jnp.int32)[None, :], axis=0).astype(jnp.int32)
    start = jnp.cumsum(counts) - counts
    padded = (counts + MOE_BLK - 1) // MOE_BLK * MOE_BLK
    pend = jnp.cumsum(padded)
    pstart = pend - padded
    dest_sorted = pstart[se] + jnp.arange(npair, dtype=jnp.int32) - start[se]
    nb = -(-npair // MOE_BLK) + MOE_E
    buf_tok = jnp.zeros((nb * MOE_BLK,), jnp.int32).at[dest_sorted].set(tok[order])
    buf_gate = jnp.zeros((nb * MOE_BLK,), F32).at[dest_sorted].set(wts.reshape(-1)[order])
    dest = jnp.zeros((npair,), jnp.int32).at[order].set(dest_sorted)
    blk_lo = jnp.arange(nb, dtype=jnp.int32) * MOE_BLK
    blk_exp = jnp.minimum(jnp.searchsorted(pend, blk_lo, side='right'), MOE_E - 1).astype(jnp.int32)
    blk_act = (blk_lo < pend[-1]).astype(jnp.int32)
    out = moe_experts(xn[buf_tok], buf_gate[:, None], blk_exp, blk_act, w1, w3, w2)
    pair_out = out[dest].reshape(n, MOE_TOPK, d)
    return h + (pair_out[:, 0] + pair_out[:, 1])


_KT = 128
_NT = (((1,), (1,)), ((), ()))
_BIG = 3e38


def _compress_kernel(x_ref, pe_ref, w_ref, o_ref):
    acc = jnp.zeros(o_ref.shape, F32)
    for l in range(L_CMP):
        y = x_ref[:, l, :] + pe_ref[l:l + 1, :]
        acc = acc + jnp.dot(y.astype(BF16), w_ref[l], preferred_element_type=F32)
    o_ref[...] = acc


def nsa_compress(x, pe, w, *, tb=256):
    nb = x.shape[0]
    tb = _row_tile(nb, tb)
    pe2 = jnp.concatenate([pe] * NSA_KV, axis=1)
    z = jnp.zeros_like(w)
    w2 = jnp.concatenate([jnp.concatenate([w, z], axis=2), jnp.concatenate([z, w], axis=2)], axis=1).astype(BF16)
    return pl.pallas_call(
        _compress_kernel,
        grid=(nb // tb,),
        in_specs=[pl.BlockSpec((tb, L_CMP, NSA_KW), lambda i: (i, 0, 0)),
                  pl.BlockSpec((L_CMP, NSA_KW), lambda i: (0, 0)),
                  pl.BlockSpec((L_CMP, NSA_KW, NSA_KW), lambda i: (0, 0, 0))],
        out_specs=pl.BlockSpec((tb, NSA_KW), lambda i: (i, 0)),
        out_shape=jax.ShapeDtypeStruct((nb, NSA_KW), F32),
        compiler_params=_params("parallel"),
        name="nsa_compress",
    )(x, pe2, w2)


def _qz(q, g, nq):
    lane = lax.broadcasted_iota(jnp.int32, (nq, NSA_KW), 1)
    keep = (lane >> 6) == g
    parts = []
    for r in range(NSA_REP):
        h = g * NSA_REP + r
        slab = q[:, (h // 2) * NSA_KW:(h // 2 + 1) * NSA_KW]
        if h % 2 != g:
            slab = pltpu.roll(slab, NSA_HD, axis=1)
        parts.append(jnp.where(keep, slab, 0.0))
    return jnp.concatenate(parts, axis=0).astype(BF16)


def _softmax_init(m_ref, l_ref, acc_ref):
    m_ref[...] = jnp.full(m_ref.shape, NEG, F32)
    l_ref[...] = jnp.zeros(l_ref.shape, F32)
    acc_ref[...] = jnp.zeros(acc_ref.shape, F32)


def _softmax_tile(st, dpos, maskf, slope, vt, m_ref, l_ref, acc_ref, g):
    s = st - slope * dpos
    sm = jnp.where(maskf > 0.0, s, NEG)
    m_old = m_ref[g]
    m_new = jnp.maximum(m_old, jnp.max(sm, axis=0, keepdims=True))
    alpha = jnp.exp(m_old - m_new)
    p = jnp.exp(sm - m_new) * maskf
    l_ref[g] = alpha * l_ref[g] + jnp.sum(p, axis=0, keepdims=True)
    acc_ref[g] = alpha * acc_ref[g] + jnp.dot(vt, p.astype(BF16), preferred_element_type=F32)
    m_ref[g] = m_new


def _softmax_done(l_ref, acc_ref, g):
    l = l_ref[g]
    return acc_ref[g] * jnp.where(l > 0.0, 1.0 / l, 0.0)


def _compressed_branch(qz, kcb, vcbt_g, qpos, slope, ncb):
    st = lax.dot_general(kcb, qz, _NT, preferred_element_type=F32)
    row = lax.broadcasted_iota(jnp.int32, (ncb, 1), 0)
    half = ncb // 2
    blk = jnp.where(row < half, 2 * row, 2 * (row - half) + 1)
    c_pos = blk * L_CMP + (L_CMP - 1)
    d_c = qpos - c_pos
    maskf = jnp.where(d_c >= 0, 1.0, 0.0)
    s = st - slope * d_c.astype(F32)
    sm = jnp.where(d_c >= 0, s, NEG)
    m = jnp.max(sm, axis=0, keepdims=True)
    p = jnp.exp(sm - m) * maskf
    l = jnp.sum(p, axis=0, keepdims=True)
    p = p * jnp.where(l > 0.0, 1.0 / l, 0.0)
    o = jnp.dot(vcbt_g, p.astype(BF16), preferred_element_type=F32)
    return o, p


def _select_blocks(imp_sel, qpos, nsp):
    cols = imp_sel.shape[1]
    blk = lax.broadcasted_iota(jnp.int32, (nsp, cols), 0)
    cur = qpos >> 6
    valid = blk <= cur
    forced = jnp.where(valid, jnp.where(blk == 0, 1.0, jnp.where(blk >= cur - 1, 1.0, 0.0)), 0.0)
    score = jnp.where(forced > 0.0, _BIG, jnp.where(valid, imp_sel, -1.0))
    sel = jnp.zeros((nsp, cols), F32)
    for _ in range(N_SEL):
        m = jnp.max(score, axis=0, keepdims=True)
        idx = jnp.min(jnp.where(score == m, blk, nsp + 1), axis=0, keepdims=True)
        pick = blk == idx
        sel = jnp.where(pick, 1.0, sel)
        score = jnp.where(pick, -2.0, score)
    return jnp.where(valid, sel, 0.0)


def _nsa_prompt_kernel(q_ref, glog_ref, kcb_ref, vcbt_ref, ks_ref, vst_ref, kw_ref, vwt_ref,
                       qoff_ref, slope_ref, o_ref, sel_ref, m_ref, l_ref, acc_ref, *, nq, ncb):
    i = pl.program_id(1)
    st0 = i * nq
    qpos = st0 + qoff_ref[...]
    q = q_ref[0] * (NSA_HD ** -0.5)
    key_io = lax.broadcasted_iota(jnp.int32, (_KT, 1), 0)
    nsp = ncb // 2
    for g in range(NSA_KV):
        qz = _qz(q, g, nq)
        slope = slope_ref[g]
        rows = slice(g * NSA_HD, (g + 1) * NSA_HD)
        o_c, p_c = _compressed_branch(qz, kcb_ref[0], vcbt_ref[0, rows, :], qpos, slope, ncb)
        imp = p_c[:, 0:nq]
        for r in range(1, NSA_REP):
            imp = imp + p_c[:, r * nq:(r + 1) * nq]
        imp_sel = imp[0:nsp] + imp[nsp:ncb]
        sel = _select_blocks(imp_sel, qpos[:, 0:nq], nsp)
        sel_ref[g] = jnp.concatenate([sel] * NSA_REP, axis=1)

        _softmax_init(m_ref, l_ref, acc_ref)

        def sel_body(kt, carry):
            off = pl.multiple_of(kt * _KT, _KT)
            k = ks_ref[0, pl.ds(off, _KT), :]
            st = lax.dot_general(k, qz, _NT, preferred_element_type=F32)
            dpos = qpos - (off + key_io)
            r0 = sel_ref[g, pl.ds(2 * kt, 1), :]
            r1 = sel_ref[g, pl.ds(2 * kt + 1, 1), :]
            selrow = jnp.where(key_io < L_SEL, r0, r1)
            maskf = jnp.where(dpos >= 0, selrow, 0.0)
            vt = vst_ref[0, rows, pl.ds(off, _KT)]
            _softmax_tile(st, dpos.astype(F32), maskf, slope, vt, m_ref, l_ref, acc_ref, g)
            return carry

        lax.fori_loop(0, i + 1, sel_body, 0)
        o_s = _softmax_done(l_ref, acc_ref, g)

        _softmax_init(m_ref, l_ref, acc_ref)

        def win_body(kt, carry):
            off = pl.multiple_of(kt * _KT, _KT)
            k = kw_ref[0, pl.ds(off, _KT), :]
            st = lax.dot_general(k, qz, _NT, preferred_element_type=F32)
            dpos = qpos - (off + key_io)
            maskf = jnp.where(dpos >= 0, jnp.where(dpos <= WINDOW, 1.0, 0.0), 0.0)
            vt = vwt_ref[0, rows, pl.ds(off, _KT)]
            _softmax_tile(st, dpos.astype(F32), maskf, slope, vt, m_ref, l_ref, acc_ref, g)
            return carry

        lax.fori_loop(jnp.maximum(i - WINDOW // _KT, 0), i + 1, win_body, 0)
        o_w = _softmax_done(l_ref, acc_ref, g)

        gc = _sigmoid(glog_ref[0, 0, g:g + 1, :])
        gs = _sigmoid(glog_ref[0, 0, 2 + g:3 + g, :])
        gw = _sigmoid(glog_ref[0, 0, 4 + g:5 + g, :])
        o_ref[0, 0, g] = gc * o_c + gs * o_s + gw * o_w


def _nsa_cols(nq):
    c = NSA_REP * nq
    qoff = (jnp.arange(c, dtype=jnp.int32) % nq).reshape(1, c)
    slopes = 2.0 ** (-8.0 * jnp.arange(1, NSA_HEADS + 1, dtype=F32) / NSA_HEADS)
    slope = jnp.repeat(slopes.reshape(NSA_KV, NSA_REP), nq, axis=1).reshape(NSA_KV, 1, c)
    return qoff, slope


def _gate_cols(glog, nq):
    b, t, _ = glog.shape
    x = glog.reshape(b, t // nq, nq, NSA_KV, NSA_REP, 3)
    return x.transpose(0, 1, 5, 3, 4, 2).reshape(b, t // nq, 3 * NSA_KV, NSA_REP * nq)


def _even_odd(x):
    return jnp.concatenate([x[:, 0::2], x[:, 1::2]], axis=1)


def _uncols(o, nq):
    b, nb = o.shape[:2]
    x = o.reshape(b, nb, NSA_KV, NSA_HD, NSA_REP, nq)
    return x.transpose(0, 1, 5, 2, 4, 3).reshape(b, nb * nq, NSA_W)


def nsa_prompt(q, glog, kcb, vcb, ks, vs, kw, vw, *, nq=128):
    b, t, _ = q.shape
    ncb = kcb.shape[1]
    c = NSA_REP * nq
    nblk = t // nq
    qoff, slope = _nsa_cols(nq)
    kcb_p = _even_odd(kcb).astype(BF16)
    vcbt = _even_odd(vcb).transpose(0, 2, 1).astype(BF16)
    per_b = lambda shape: pl.BlockSpec((1,) + shape, lambda i, j: (i, 0, 0))
    out = pl.pallas_call(
        functools.partial(_nsa_prompt_kernel, nq=nq, ncb=ncb),
        grid=(b, nblk),
        in_specs=[pl.BlockSpec((1, nq, NSA_W), lambda i, j: (i, j, 0)),
                  pl.BlockSpec((1, 1, 3 * NSA_KV, c), lambda i, j: (i, j, 0, 0)),
                  per_b((ncb, NSA_KW)), per_b((NSA_KW, ncb)),
                  per_b((t, NSA_KW)), per_b((NSA_KW, t)), per_b((t, NSA_KW)), per_b((NSA_KW, t)),
                  pl.BlockSpec((1, c), lambda i, j: (0, 0)),
                  pl.BlockSpec((NSA_KV, 1, c), lambda i, j: (0, 0, 0))],
        out_specs=pl.BlockSpec((1, 1, NSA_KV, NSA_HD, c), lambda i, j: (i, j, 0, 0, 0)),
        out_shape=jax.ShapeDtypeStruct((b, nblk, NSA_KV, NSA_HD, c), F32),
        scratch_shapes=[pltpu.VMEM((NSA_KV, ncb // 2, c), F32),
                        pltpu.VMEM((NSA_KV, 1, c), F32), pltpu.VMEM((NSA_KV, 1, c), F32),
                        pltpu.VMEM((NSA_KV, NSA_HD, c), F32)],
        compiler_params=_params("parallel", "arbitrary"),
        name="nsa_prompt",
    )(q, _gate_cols(glog, nq), kcb_p, vcbt, ks.astype(BF16), vs.transpose(0, 2, 1).astype(BF16),
      kw.astype(BF16), vw.transpose(0, 2, 1).astype(BF16), qoff, slope)
    return _uncols(out, nq)


def _nsa_sample_kernel(pt_ref, q_ref, glog_ref, kcb_ref, vcbt_ref, *refs, nq, ncb, npages, page, past, wb):
    ks_pages = refs[0:npages]
    vs_pages = refs[npages:2 * npages]
    (nks_ref, nvs_ref, nkw_ref, nvw_ref, wk_ref, wv_ref, qoff_ref, slope_ref, rmat_ref,
     o_ref, wko_ref, wvo_ref, m_ref, l_ref, acc_ref) = refs[2 * npages:]
    del pt_ref
    c = NSA_REP * nq
    qpos = past + qoff_ref[...]
    q = q_ref[0] * (NSA_HD ** -0.5)
    key_io = lax.broadcasted_iota(jnp.int32, (_KT, 1), 0)
    nsp = sel_rows = -(-(past + nq) // L_SEL)
    nsp = -(-nsp // 8) * 8
    pad_rows = lambda x: jnp.concatenate([x, jnp.zeros((_KT - nq, NSA_KW), F32)], axis=0)

    qzs, slopes, sels, o_cs = [], [], [], []
    for g in range(NSA_KV):
        qz = _qz(q, g, nq)
        slope = slope_ref[g]
        o_c, p_c = _compressed_branch(qz, kcb_ref[0], vcbt_ref[0, g * NSA_HD:(g + 1) * NSA_HD, :], qpos, slope, ncb)
        imp = _dot_exact_rhs(p_c, rmat_ref[...])
        imp_sel = imp[0:ncb // 2] + imp[ncb // 2:ncb]
        imp_sel = jnp.concatenate([imp_sel, jnp.zeros((nsp - ncb // 2, c), F32)], axis=0)
        qzs.append(qz)
        slopes.append(slope)
        sels.append(_select_blocks(imp_sel, qpos, nsp))
        o_cs.append(o_c)
    del sel_rows

    def run_tile(k, v, tok0, mask_fn):
        kb = k.astype(BF16)
        vt = v.T
        dpos = qpos - (tok0 + key_io)
        for g in range(NSA_KV):
            st = lax.dot_general(kb, qzs[g], _NT, preferred_element_type=F32)
            vt_g = vt[g * NSA_HD:(g + 1) * NSA_HD, :].astype(BF16)
            _softmax_tile(st, dpos.astype(F32), mask_fn(g, dpos), slopes[g], vt_g, m_ref, l_ref, acc_ref, g)

    _softmax_init(m_ref, l_ref, acc_ref)
    for j in range(npages + 1):
        blk0 = j * (page // L_SEL)

        def sel_mask(g, dpos, blk0=blk0):
            selrow = jnp.where(key_io < L_SEL, sels[g][blk0:blk0 + 1, :], sels[g][blk0 + 1:blk0 + 2, :])
            return jnp.where(dpos >= 0, selrow, 0.0)

        if j < npages:
            run_tile(ks_pages[j][0], vs_pages[j][0], j * page, sel_mask)
        else:
            run_tile(pad_rows(nks_ref[0]), pad_rows(nvs_ref[0]), past, sel_mask)
    o_ss = [_softmax_done(l_ref, acc_ref, g) for g in range(NSA_KV)]

    _softmax_init(m_ref, l_ref, acc_ref)
    win_mask = lambda g, dpos: jnp.where(dpos >= 0, jnp.where(dpos <= WINDOW, 1.0, 0.0), 0.0)
    for j in range(wb // _KT):
        run_tile(wk_ref[0, j * _KT:(j + 1) * _KT, :], wv_ref[0, j * _KT:(j + 1) * _KT, :], past - wb + j * _KT, win_mask)
    run_tile(pad_rows(nkw_ref[0]), pad_rows(nvw_ref[0]), past, win_mask)
    for g in range(NSA_KV):
        o_w = _softmax_done(l_ref, acc_ref, g)
        gc = _sigmoid(glog_ref[0, 0, g:g + 1, :])
        gs = _sigmoid(glog_ref[0, 0, 2 + g:3 + g, :])
        gw = _sigmoid(glog_ref[0, 0, 4 + g:5 + g, :])
        o_ref[0, 0, g] = gc * o_cs[g] + gs * o_ss[g] + gw * o_w

    wko_ref[0, 0:wb - nq, :] = wk_ref[0, nq:wb, :]
    wko_ref[0, wb - nq:wb, :] = nkw_ref[0]
    wvo_ref[0, 0:wb - nq, :] = wv_ref[0, nq:wb, :]
    wvo_ref[0, wb - nq:wb, :] = nvw_ref[0]


def nsa_sample(q, glog, kcb, vcb, pool_k, pool_v, page_table, nks, nvs, nkw, nvw, win_k, win_v):
    b, nq, _ = q.shape
    ncb = kcb.shape[1]
    npages = page_table.shape[1]
    page = pool_k.shape[1]
    past = npages * page
    wb = win_k.shape[1]
    assert page == _KT and wb % _KT == 0 and nq % 8 == 0 and nq <= L_SEL and ncb % 2 == 0
    c = NSA_REP * nq
    qoff, slope = _nsa_cols(nq)
    col = jnp.arange(c, dtype=jnp.int32)
    rmat = (col[:, None] % nq == col[None, :] % nq).astype(BF16)
    kcb_p = _even_odd(kcb).astype(BF16)
    vcbt = _even_odd(vcb).transpose(0, 2, 1).astype(BF16)
    per_b = lambda shape: pl.BlockSpec((1,) + shape, lambda i, pt: (i,) + (0,) * len(shape))
    const = lambda shape: pl.BlockSpec(shape, lambda i, pt: (0,) * len(shape))
    page_spec = lambda j: pl.BlockSpec((1, page, NSA_KW), lambda i, pt: (pt[i, j], 0, 0))
    in_specs = ([per_b((nq, NSA_W)), per_b((1, 3 * NSA_KV, c)), per_b((ncb, NSA_KW)), per_b((NSA_KW, ncb))]
                + [page_spec(j) for j in range(npages)] * 2
                + [per_b((nq, NSA_KW))] * 4 + [per_b((wb, NSA_KW))] * 2
                + [const((1, c)), const((NSA_KV, 1, c)), const((c, c))])
    out, wko, wvo = pl.pallas_call(
        functools.partial(_nsa_sample_kernel, nq=nq, ncb=ncb, npages=npages, page=page, past=past, wb=wb),
        grid_spec=pltpu.PrefetchScalarGridSpec(
            num_scalar_prefetch=1,
            grid=(b,),
            in_specs=in_specs,
            out_specs=[per_b((1, NSA_KV, NSA_HD, c)), per_b((wb, NSA_KW)), per_b((wb, NSA_KW))],
            scratch_shapes=[pltpu.VMEM((NSA_KV, 1, c), F32), pltpu.VMEM((NSA_KV, 1, c), F32),
                            pltpu.VMEM((NSA_KV, NSA_HD, c), F32)],
        ),
        out_shape=[jax.ShapeDtypeStruct((b, 1, NSA_KV, NSA_HD, c), F32),
                   jax.ShapeDtypeStruct((b, wb, NSA_KW), F32), jax.ShapeDtypeStruct((b, wb, NSA_KW), F32)],
        compiler_params=_params("arbitrary"),
        name="nsa_sample",
    )(page_table, q, _gate_cols(glog, nq), kcb_p, vcbt, *([pool_k] * npages), *([pool_v] * npages),
      nks, nvs, nkw, nvw, win_k, win_v, qoff, slope, rmat)
    return _uncols(out, nq), wko, wvo


_HALO_M = 8
_TN = (((0,), (0,)), ((), ()))


def _softplus(x):
    return jnp.maximum(x, 0.0) + jnp.log1p(jnp.exp(-jnp.abs(x)))


def _ssd_kernel(xbc_ref, z_ref, sm_ref, dtt_ref, cs_ref, h0_ref, cw_ref, cb_ref, dtb_ref, dtbt_ref,
                al_ref, alt_ref, dsk_ref, ng_ref, y_ref, ncs_ref, hf_ref, ext_ref, h_ref, yh_ref, *, ql, dt_col):
    c = pl.program_id(1)
    nc = pl.num_programs(1)

    @pl.when(c == 0)
    def _():
        ext_ref[...] = jnp.zeros_like(ext_ref)
        ext_ref[_HALO_M - (M_CONV_W - 1):_HALO_M, :] = cs_ref[0]
        h_ref[...] = h0_ref[0]

    @pl.when(c > 0)
    def _():
        ext_ref[0:_HALO_M, :] = ext_ref[ql:ql + _HALO_M, :]

    ext_ref[_HALO_M:_HALO_M + ql, :] = xbc_ref[0]
    acc = jnp.zeros((ql, M_CONV_DIM), F32)
    for k in range(M_CONV_W):
        off = _HALO_M - (M_CONV_W - 1) + k
        acc = acc + ext_ref[off:off + ql, :] * cw_ref[k:k + 1, :]
    xbc = _silu(acc + cb_ref[...])
    xs = xbc[:, 0:M_DIN]
    bm = xbc[:, M_DIN:M_DIN + M_GROUPS * M_DSTATE]
    cm = xbc[:, M_DIN + M_GROUPS * M_DSTATE:M_CONV_DIM]

    dt = _softplus(sm_ref[0, :, dt_col:dt_col + M_HEADS] + dtb_ref[...])
    dtt = _softplus(dtt_ref[0] + dtbt_ref[...])
    dta = dt * (-jnp.exp(al_ref[...]))
    dtat = dtt * (-jnp.exp(alt_ref[...]))
    ti = lax.broadcasted_iota(jnp.int32, (ql, ql), 0)
    si = lax.broadcasted_iota(jnp.int32, (ql, ql), 1)
    causal = si <= ti
    cum = _dot_exact_lhs(jnp.where(causal, 1.0, 0.0).astype(BF16), dta)
    cumt = _dot_exact_rhs(dtat, jnp.where(ti <= si, 1.0, 0.0).astype(BF16))
    cum_last = cum[ql - 1:ql, :]
    edec = jnp.exp(cum)
    eend = jnp.exp(cum_last - cum)
    elast = jnp.exp(cum_last)

    rep = M_HEADS // M_GROUPS
    for gi in range(M_GROUPS):
        b_g = bm[:, gi * M_DSTATE:(gi + 1) * M_DSTATE]
        c_g = cm[:, gi * M_DSTATE:(gi + 1) * M_DSTATE].astype(BF16)
        cb = lax.dot_general(c_g, b_g.astype(BF16), _NT, preferred_element_type=F32)
        for hh in range(rep):
            h = gi * rep + hh
            hs = slice(h * M_HDIM, (h + 1) * M_HDIM)
            lmat = jnp.where(causal, jnp.exp(cum[:, h:h + 1] - cumt[h:h + 1, :]), 0.0)
            x_h = xs[:, hs]
            xdt = (x_h * dt[:, h:h + 1]).astype(BF16)
            y_diag = jnp.dot((cb * lmat).astype(BF16), xdt, preferred_element_type=F32)
            h_in = h_ref[h]
            y_off = lax.dot_general(c_g, h_in.astype(BF16), _NT, preferred_element_type=F32) * edec[:, h:h + 1]
            bd = (b_g * eend[:, h:h + 1]).astype(BF16)
            s_chunk = lax.dot_general(xdt, bd, _TN, preferred_element_type=F32)
            h_ref[h] = elast[:, h:h + 1] * h_in + s_chunk
            yh_ref[:, hs] = y_diag + y_off + dsk_ref[:, hs] * x_h

    yz = yh_ref[...] * _silu(z_ref[0])
    y_ref[0] = _rms(yz, ng_ref[...])

    @pl.when(c == nc - 1)
    def _():
        ncs_ref[0] = ext_ref[_HALO_M + ql - (M_CONV_W - 1):_HALO_M + ql, :]
        hf_ref[0] = h_ref[...]


def ssd_mixer(xbc, z, small, dt_col, conv_state, h0, conv_w, conv_b, dt_bias, a_log, d_skip, norm_g, *, ql):
    b, t, _ = xbc.shape
    nc = t // ql
    sw = small.shape[2]
    dtt = small[:, :, dt_col:dt_col + M_HEADS].transpose(0, 2, 1)
    const = lambda shape: pl.BlockSpec(shape, lambda i, j: (0,) * len(shape))
    per_b = lambda shape: pl.BlockSpec((1,) + shape, lambda i, j: (i,) + (0,) * len(shape))
    row = lambda x: x.reshape(1, -1)
    colv = lambda x: x.reshape(-1, 1)
    return pl.pallas_call(
        functools.partial(_ssd_kernel, ql=ql, dt_col=dt_col),
        grid=(b, nc),
        in_specs=[pl.BlockSpec((1, ql, M_CONV_DIM), lambda i, j: (i, j, 0)),
                  pl.BlockSpec((1, ql, M_DIN), lambda i, j: (i, j, 0)),
                  pl.BlockSpec((1, ql, sw), lambda i, j: (i, j, 0)),
                  pl.BlockSpec((1, M_HEADS, ql), lambda i, j: (i, 0, j)),
                  per_b((M_CONV_W - 1, M_CONV_DIM)), per_b((M_HEADS, M_HDIM, M_DSTATE)),
                  const((M_CONV_W, M_CONV_DIM)), const((1, M_CONV_DIM)),
                  const((1, M_HEADS)), const((M_HEADS, 1)), const((1, M_HEADS)), const((M_HEADS, 1)),
                  const((1, M_DIN)), const((1, M_DIN))],
        out_specs=[pl.BlockSpec((1, ql, M_DIN), lambda i, j: (i, j, 0)),
                   per_b((M_CONV_W - 1, M_CONV_DIM)), per_b((M_HEADS, M_HDIM, M_DSTATE))],
        out_shape=[jax.ShapeDtypeStruct((b, t, M_DIN), F32),
                   jax.ShapeDtypeStruct((b, M_CONV_W - 1, M_CONV_DIM), F32),
                   jax.ShapeDtypeStruct((b, M_HEADS, M_HDIM, M_DSTATE), F32)],
        scratch_shapes=[pltpu.VMEM((_HALO_M + ql, M_CONV_DIM), F32),
                        pltpu.VMEM((M_HEADS, M_HDIM, M_DSTATE), F32),
                        pltpu.VMEM((ql, M_DIN), F32)],
        compiler_params=_params("parallel", "arbitrary"),
        name="ssd_mixer",
    )(xbc, z, small, dtt, conv_state, h0, conv_w, row(conv_b), row(dt_bias), colv(dt_bias),
      row(a_log), colv(a_log), row(jnp.repeat(d_skip, M_HDIM)), row(norm_g))


_SMALL_W = 128
_OD_SPLITS = (NSA_W, 6 * NSA_KW, M_DIN, M_CONV_DIM, _SMALL_W)


def _odd_w_in(w):
    o_kv = NSA_W
    o_gate = o_kv + 6 * NSA_KW
    o_z = o_gate + 3 * NSA_HEADS
    o_xbc = o_z + M_DIN
    o_dt = o_xbc + M_CONV_DIM
    pad = jnp.zeros((w.shape[0], _SMALL_W - 3 * NSA_HEADS - M_HEADS), F32)
    return jnp.concatenate([w[:, :o_gate], w[:, o_z:o_xbc], w[:, o_xbc:o_dt],
                            w[:, o_gate:o_z], w[:, o_dt:], pad], axis=1)


def kernel(x_prompt, x_sample, state_conv_a, state_conv_b, cache_cmp_k, cache_cmp_v, cache_sel_k, cache_sel_v, cache_win_k, cache_win_v, state_ssm, state_ssm_conv, cache_mem_k, cache_mem_v, page_table, mem_prompt, norm_mix, norm_xattn, norm_ffn, norm_final, ev_w_in, ev_conv_a, ev_conv_b, ev_conv_b_bias, ev_ln_g, ev_ln_b, ev_w_out, od_w_in, od_cmp_pe, od_cmp_wk, od_cmp_wv, od_ssm_conv_w, od_ssm_conv_b, od_dt_bias, od_a_log, od_d_skip, od_ssm_norm, od_w_out, xa_wq, xa_wk, xa_wv, xa_wo, moe_wg, moe_bg, moe_we, moe_be, moe_w1, moe_w3, moe_w2):
    bp, tp, d = x_prompt.shape
    bs, ts, _ = x_sample.shape
    n_p, n_s = bp * tp, bs * ts
    n_mem = mem_prompt.shape[1]
    depth = norm_mix.shape[0]
    n_pool, page = cache_cmp_k.shape[1:3]
    wb = cache_win_k.shape[2]
    dt_col = 3 * NSA_HEADS

    def groups(a):
        return a[:n_p].reshape(bp, tp, a.shape[-1]), a[n_p:].reshape(bs, ts, a.shape[-1])

    def rows(a_p, a_s):
        return jnp.concatenate([a_p.reshape(n_p, a_p.shape[-1]), a_s.reshape(n_s, a_s.shape[-1])], axis=0)

    h = rows(x_prompt, x_sample)
    out = {k: [] for k in ("ca_p", "ca_s", "cb_p", "cb_s", "wk_p", "wk_s", "wv_p", "wv_s",
                           "sm_p", "sm_s", "sc_p", "sc_s", "mk_p", "mv_p")}
    rows_p = [[], [], [], []]
    rows_s = [[], [], [], []]
    for i in range(depth):
        j = i // 2
        if i % 2 == 0:
            u_p, u_s = groups(norm_matmul(h, norm_mix[i], ev_w_in[j]))
            ev = (ev_conv_a[j], ev_conv_b[j], ev_conv_b_bias[j], ev_ln_g[j], ev_ln_b[j])
            y_p, na_p, nb_p = even_conv(u_p, jnp.zeros((bp, CONV_A_W - 1, D_A), F32),
                                        jnp.zeros((bp, CONV_B_W - 1, D_B), F32), *ev)
            y_s, na_s, nb_s = even_conv(u_s, state_conv_a[j], state_conv_b[j], *ev)
            h = matmul_res([rows(y_p, y_s)], [ev_w_out[j]], h)
            out["ca_p"].append(na_p)
            out["ca_s"].append(na_s)
            out["cb_p"].append(nb_p)
            out["cb_s"].append(nb_s)
        else:
            uq, ukv, uz, uxbc, usm = norm_matmul(h, norm_mix[i], _odd_w_in(od_w_in[j]), splits=_OD_SPLITS)
            q_p, q_s = groups(uq)
            kv_p, kv_s = groups(ukv)
            z_p, z_s = groups(uz)
            xbc_p, xbc_s = groups(uxbc)
            sm_p, sm_s = groups(usm)
            part = lambda a, k: a[:, :, k * NSA_KW:(k + 1) * NSA_KW]
            kvp = [part(kv_p, k) for k in range(6)]
            kvs = [part(kv_s, k) for k in range(6)]
            pe, wck, wcv = od_cmp_pe[j], od_cmp_wk[j], od_cmp_wv[j]
            mw = (od_ssm_conv_w[j], od_ssm_conv_b[j], od_dt_bias[j], od_a_log[j], od_d_skip[j], od_ssm_norm[j])
            blocks = lambda a: a.reshape(-1, L_CMP, NSA_KW)
            ncb = tp // L_CMP
            kcb_p = nsa_compress(blocks(kvp[0][:, :ncb * L_CMP]), pe, wck).reshape(bp, ncb, NSA_KW)
            vcb_p = nsa_compress(blocks(kvp[1][:, :ncb * L_CMP]), pe, wcv).reshape(bp, ncb, NSA_KW)
            o_p = nsa_prompt(q_p, sm_p[:, :, :dt_col], kcb_p, vcb_p, kvp[2], kvp[3], kvp[4], kvp[5])
            keep = min(WINDOW, tp)
            y_p, nsc_p, nsm_p = ssd_mixer(xbc_p, z_p, sm_p, dt_col, jnp.zeros((bp, M_CONV_W - 1, M_CONV_DIM), F32),
                                          jnp.zeros((bp, M_HEADS, M_HDIM, M_DSTATE), F32), *mw, ql=128)
            kcp = nsa_compress(blocks(cache_cmp_k[j]), pe, wck).reshape(n_pool, page // L_CMP, NSA_KW)
            vcp = nsa_compress(blocks(cache_cmp_v[j]), pe, wcv).reshape(n_pool, page // L_CMP, NSA_KW)
            kcb_s = kcp[page_table].reshape(bs, -1, NSA_KW)
            vcb_s = vcp[page_table].reshape(bs, -1, NSA_KW)
            o_s, nwk_s, nwv_s = nsa_sample(
                q_s, sm_s[:, :, :dt_col], kcb_s, vcb_s,
                cache_sel_k[j].reshape(n_pool, page, NSA_KW), cache_sel_v[j].reshape(n_pool, page, NSA_KW), page_table,
                kvs[2], kvs[3], kvs[4], kvs[5],
                cache_win_k[j].reshape(bs, wb, NSA_KW), cache_win_v[j].reshape(bs, wb, NSA_KW))
            y_s, nsc_s, nsm_s = ssd_mixer(xbc_s, z_s, sm_s, dt_col, state_ssm_conv[j], state_ssm[j], *mw, ql=ts)
            w_out = od_w_out[j]
            h = matmul_res([rows(o_p, o_s), rows(y_p, y_s)], [w_out[:NSA_W], w_out[NSA_W:]], h)
            heads = lambda a: a.reshape(a.shape[0], a.shape[1], NSA_KV, NSA_HD)
            for k in range(4):
                rows_p[k].append(heads(kvp[k]))
                rows_s[k].append(heads(kvs[k]))
            out["wk_p"].append(heads(kvp[4][:, tp - keep:]))
            out["wv_p"].append(heads(kvp[5][:, tp - keep:]))
            out["wk_s"].append(heads(nwk_s))
            out["wv_s"].append(heads(nwv_s))
            out["sc_p"].append(nsc_p)
            out["sc_s"].append(nsc_s)
            out["sm_p"].append(nsm_p)
            out["sm_s"].append(nsm_s)
        mk, mv = norm_matmul(mem_prompt.reshape(bp * n_mem, d), None,
                             jnp.concatenate([xa_wk[i], xa_wv[i]], axis=1), norm=False,
                             splits=(XA_HEADS * XA_HD, XA_HEADS * XA_HD))
        mk = mk.reshape(bp, n_mem, XA_HEADS * XA_HD)
        mv = mv.reshape(bp, n_mem, XA_HEADS * XA_HD)
        out["mk_p"].append(mk.reshape(bp, n_mem, XA_HEADS, XA_HD))
        out["mv_p"].append(mv.reshape(bp, n_mem, XA_HEADS, XA_HD))
        qx_p, qx_s = groups(norm_matmul(h, norm_xattn[i], xa_wq[i]))
        ox_p = xattn(qx_p, mk, mv)
        ox_s = xattn(qx_s, cache_mem_k[i].reshape(bs, n_mem, XA_HEADS * XA_HD),
                     cache_mem_v[i].reshape(bs, n_mem, XA_HEADS * XA_HD))
        h = matmul_res([rows(ox_p, ox_s)], [xa_wo[i]], h)
        h = moe_layer(h, norm_ffn[i], moe_wg[i], moe_bg[i], moe_we[i], moe_be[i], moe_w1[i], moe_w3[i], moe_w2[i])
    y = rmsnorm_rows(h, norm_final)
    y_prompt = y[:n_p].reshape(bp, tp, d)
    y_sample = y[n_p:].reshape(bs, ts, d)
    st = lambda k: jnp.stack(out[k])
    return (y_prompt, y_sample, st("ca_p"), st("ca_s"), st("cb_p"), st("cb_s"),
            jnp.stack(rows_p[0]), jnp.stack(rows_s[0]), jnp.stack(rows_p[1]), jnp.stack(rows_s[1]),
            jnp.stack(rows_p[2]), jnp.stack(rows_s[2]), jnp.stack(rows_p[3]), jnp.stack(rows_s[3]),
            st("wk_p"), st("wk_s"), st("wv_p"), st("wv_s"), st("sm_p"), st("sm_s"), st("sc_p"), st("sc_s"),
            st("mk_p"), st("mv_p"))
```

```python
import functools

import jax
import jax.numpy as jnp
from jax import lax
from jax.experimental import pallas as pl
from jax.experimental.pallas import tpu as pltpu

F32 = jnp.float32
BF16 = jnp.bfloat16
EPS = 1e-6
NEG = -1e30
VMEM_LIMIT = 56 * 1024 * 1024

D_A = 512
D_B = 512
CONV_A_W = 3
CONV_B_W = 31
NSA_HEADS = 8
NSA_HD = 64
NSA_KV = 2
NSA_REP = NSA_HEADS // NSA_KV
NSA_W = NSA_HEADS * NSA_HD
NSA_KW = NSA_KV * NSA_HD
L_CMP = 32
L_SEL = 64
N_SEL = 16
WINDOW = 512
M_DIN = 512
M_HDIM = 64
M_HEADS = 8
M_DSTATE = 64
M_GROUPS = 2
M_CONV_W = 4
M_CONV_DIM = M_DIN + 2 * M_GROUPS * M_DSTATE
XA_HEADS = 4
XA_HD = 128
MOE_GROUPS = 4
MOE_EPG = 8
MOE_E = 32
MOE_TOPK = 2


def _params(*sem):
    return pltpu.CompilerParams(dimension_semantics=sem, vmem_limit_bytes=VMEM_LIMIT)


def _row_tile(n, pref):
    t = min(n, pref)
    while n % t or (t % 8 and t != n):
        t -= 1
    return t


def _bdot(a, b):
    return jnp.dot(a.astype(BF16), b.astype(BF16), preferred_element_type=F32)


def _split3(a):
    hi = a.astype(BF16)
    r1 = a - hi.astype(F32)
    mid = r1.astype(BF16)
    lo = (r1 - mid.astype(F32)).astype(BF16)
    return hi, mid, lo


def _dot_exact_rhs(a, b_bf16):
    hi, mid, lo = _split3(a)
    d = lambda x: jnp.dot(x, b_bf16, preferred_element_type=F32)
    return d(hi) + d(mid) + d(lo)


def _dot_exact_lhs(a_bf16, b):
    hi, mid, lo = _split3(b)
    d = lambda x: jnp.dot(a_bf16, x, preferred_element_type=F32)
    return d(hi) + d(mid) + d(lo)


def _rms(x, g):
    ms = jnp.mean(x * x, axis=-1, keepdims=True)
    return x * lax.rsqrt(ms + EPS) * g


def _sigmoid(x):
    return 1.0 / (1.0 + jnp.exp(-x))


def _silu(x):
    return x * _sigmoid(x)


def _norm_matmul_kernel(x_ref, g_ref, w_ref, *o_refs, norm, splits):
    x = x_ref[...]
    if norm:
        x = _rms(x, g_ref[...])
    res = jnp.dot(x.astype(BF16), w_ref[...].astype(BF16), preferred_element_type=F32)
    off = 0
    for o_ref, width in zip(o_refs, splits):
        o_ref[...] = res[:, off:off + width]
        off += width


def norm_matmul(x, g, w, *, norm=True, splits=None, tm=512):
    n, k = x.shape
    m = w.shape[1]
    tm = _row_tile(n, tm)
    if g is None:
        g = jnp.ones((k,), F32)
    widths = (m,) if splits is None else tuple(splits)
    assert sum(widths) == m
    outs = pl.pallas_call(
        functools.partial(_norm_matmul_kernel, norm=norm, splits=widths),
        grid=(n // tm,),
        in_specs=[pl.BlockSpec((tm, k), lambda i: (i, 0)),
                  pl.BlockSpec((1, k), lambda i: (0, 0)),
                  pl.BlockSpec((k, m), lambda i: (0, 0))],
        out_specs=[pl.BlockSpec((tm, wd), lambda i: (i, 0)) for wd in widths],
        out_shape=[jax.ShapeDtypeStruct((n, wd), F32) for wd in widths],
        compiler_params=_params("parallel"),
        name="norm_matmul",
    )(x, g.reshape(1, k), w)
    return outs[0] if splits is None else outs


def _rmsnorm_kernel(x_ref, g_ref, o_ref):
    o_ref[...] = _rms(x_ref[...], g_ref[...])


def rmsnorm_rows(x, g, *, tm=512):
    n, k = x.shape
    tm = _row_tile(n, tm)
    return pl.pallas_call(
        _rmsnorm_kernel,
        grid=(n // tm,),
        in_specs=[pl.BlockSpec((tm, k), lambda i: (i, 0)), pl.BlockSpec((1, k), lambda i: (0, 0))],
        out_specs=pl.BlockSpec((tm, k), lambda i: (i, 0)),
        out_shape=jax.ShapeDtypeStruct((n, k), F32),
        compiler_params=_params("parallel"),
        name="rmsnorm_rows",
    )(x, g.reshape(1, k))


def _matmul_res_kernel(*refs, n_in):
    res_ref = refs[2 * n_in]
    o_ref = refs[2 * n_in + 1]
    acc = res_ref[...]
    for j in range(n_in):
        acc = acc + jnp.dot(refs[2 * j][...].astype(BF16), refs[2 * j + 1][...].astype(BF16),
                            preferred_element_type=F32)
    o_ref[...] = acc


def matmul_res(xs, ws, res, *, tm=512):
    n, m = res.shape
    tm = _row_tile(n, tm)
    in_specs, args = [], []
    for x, w in zip(xs, ws):
        k = x.shape[1]
        in_specs += [pl.BlockSpec((tm, k), lambda i: (i, 0)), pl.BlockSpec((k, m), lambda i: (0, 0))]
        args += [x, w]
    in_specs.append(pl.BlockSpec((tm, m), lambda i: (i, 0)))
    return pl.pallas_call(
        functools.partial(_matmul_res_kernel, n_in=len(xs)),
        grid=(n // tm,),
        in_specs=in_specs,
        out_specs=pl.BlockSpec((tm, m), lambda i: (i, 0)),
        out_shape=jax.ShapeDtypeStruct((n, m), F32),
        compiler_params=_params("parallel"),
        name="matmul_res",
    )(*args, res)


_HALO_A = 8
_HALO_B = 32


def _even_conv_kernel(u_ref, sa_ref, sb_ref, wa_ref, wb_ref, bb_ref, lg_ref, lb_ref,
                      y_ref, na_ref, nb_ref, ea_ref, eb_ref, *, tt):
    t = pl.program_id(1)
    nt = pl.num_programs(1)

    @pl.when(t == 0)
    def _():
        ea_ref[...] = jnp.zeros_like(ea_ref)
        eb_ref[...] = jnp.zeros_like(eb_ref)
        ea_ref[_HALO_A - (CONV_A_W - 1):_HALO_A, :] = sa_ref[0]
        eb_ref[_HALO_B - (CONV_B_W - 1):_HALO_B, :] = sb_ref[0]

    @pl.when(t > 0)
    def _():
        ea_ref[0:_HALO_A, :] = ea_ref[tt:tt + _HALO_A, :]
        eb_ref[0:_HALO_B, :] = eb_ref[tt:tt + _HALO_B, :]

    xa = u_ref[0, :, 0:D_A]
    ba = u_ref[0, :, D_A:2 * D_A]
    ca = u_ref[0, :, 2 * D_A:3 * D_A]
    pb = u_ref[0, :, 3 * D_A:3 * D_A + D_B]
    gb = u_ref[0, :, 3 * D_A + D_B:3 * D_A + 2 * D_B]
    ea_ref[_HALO_A:_HALO_A + tt, :] = ca * xa
    eb_ref[_HALO_B:_HALO_B + tt, :] = pb * _sigmoid(gb)

    acc = jnp.zeros((tt, D_A), F32)
    for k in range(CONV_A_W):
        off = _HALO_A - (CONV_A_W - 1) + k
        acc = acc + ea_ref[off:off + tt, :] * wa_ref[k:k + 1, :]
    y_ref[0, :, 0:D_A] = ba * acc

    acc = jnp.zeros((tt, D_B), F32)
    for k in range(CONV_B_W):
        off = _HALO_B - (CONV_B_W - 1) + k
        acc = acc + eb_ref[off:off + tt, :] * wb_ref[k:k + 1, :]
    acc = acc + bb_ref[...]
    mu = jnp.mean(acc, axis=-1, keepdims=True)
    xc = acc - mu
    var = jnp.mean(xc * xc, axis=-1, keepdims=True)
    yb = xc * lax.rsqrt(var + EPS) * lg_ref[...] + lb_ref[...]
    y_ref[0, :, D_A:D_A + D_B] = _silu(yb)

    @pl.when(t == nt - 1)
    def _():
        na_ref[0] = ea_ref[_HALO_A + tt - (CONV_A_W - 1):_HALO_A + tt, :]
        nb_ref[0] = eb_ref[_HALO_B + tt - (CONV_B_W - 1):_HALO_B + tt, :]


def even_conv(u, sa, sb, wa, wb, bb, lg, lb, *, tt=256):
    b, t, w = u.shape
    tt = _row_tile(t, tt)
    full = lambda shape: pl.BlockSpec(shape, lambda i, j: (0,) * len(shape))
    return pl.pallas_call(
        functools.partial(_even_conv_kernel, tt=tt),
        grid=(b, t // tt),
        in_specs=[pl.BlockSpec((1, tt, w), lambda i, j: (i, j, 0)),
                  pl.BlockSpec((1, CONV_A_W - 1, D_A), lambda i, j: (i, 0, 0)),
                  pl.BlockSpec((1, CONV_B_W - 1, D_B), lambda i, j: (i, 0, 0)),
                  full((CONV_A_W, D_A)), full((CONV_B_W, D_B)), full((1, D_B)),
                  full((1, D_B)), full((1, D_B))],
        out_specs=[pl.BlockSpec((1, tt, D_A + D_B), lambda i, j: (i, j, 0)),
                   pl.BlockSpec((1, CONV_A_W - 1, D_A), lambda i, j: (i, 0, 0)),
                   pl.BlockSpec((1, CONV_B_W - 1, D_B), lambda i, j: (i, 0, 0))],
        out_shape=[jax.ShapeDtypeStruct((b, t, D_A + D_B), F32),
                   jax.ShapeDtypeStruct((b, CONV_A_W - 1, D_A), F32),
                   jax.ShapeDtypeStruct((b, CONV_B_W - 1, D_B), F32)],
        scratch_shapes=[pltpu.VMEM((_HALO_A + tt, D_A), F32), pltpu.VMEM((_HALO_B + tt, D_B), F32)],
        compiler_params=_params("parallel", "arbitrary"),
        name="even_conv",
    )(u, sa, sb, wa, wb, bb.reshape(1, D_B), lg.reshape(1, D_B), lb.reshape(1, D_B))


def _xattn_kernel(q_ref, k_ref, v_ref, o_ref, *, cache_layout):
    scale = XA_HD ** -0.5
    for h in range(XA_HEADS):
        sl = slice(h * XA_HD, (h + 1) * XA_HD)
        q = q_ref[0, :, sl].astype(BF16)
        if cache_layout:
            k = k_ref[0, 0, :, h, :].astype(BF16)
            v = v_ref[0, 0, :, h, :].astype(BF16)
        else:
            k = k_ref[0, :, sl].astype(BF16)
            v = v_ref[0, :, sl].astype(BF16)
        s = lax.dot_general(q, k, (((1,), (1,)), ((), ())), preferred_element_type=F32) * scale
        m = jnp.max(s, axis=-1, keepdims=True)
        p = jnp.exp(s - m)
        p = p / jnp.sum(p, axis=-1, keepdims=True)
        o_ref[0, :, sl] = jnp.dot(p.astype(BF16), v, preferred_element_type=F32)


def xattn(q, k, v, *, layer=None, tq=512):
    b, t, w = q.shape
    tq = _row_tile(t, tq)
    if layer is None:
        m = k.shape[1]
        kv_spec = pl.BlockSpec((1, m, w), lambda i, j: (i, 0, 0))
    else:
        m = k.shape[2]
        kv_spec = pl.BlockSpec((1, 1, m, XA_HEADS, XA_HD), lambda i, j: (layer, i, 0, 0, 0))
    return pl.pallas_call(
        functools.partial(_xattn_kernel, cache_layout=layer is not None),
        grid=(b, t // tq),
        in_specs=[pl.BlockSpec((1, tq, w), lambda i, j: (i, j, 0)), kv_spec, kv_spec],
        out_specs=pl.BlockSpec((1, tq, w), lambda i, j: (i, j, 0)),
        out_shape=jax.ShapeDtypeStruct((b, t, w), F32),
        compiler_params=_params("parallel", "parallel"),
        name="xattn",
    )(q, k, v)


MOE_BLK = 256
_ROUTER_W = 128


def _router_kernel(x_ref, g_ref, w_ref, lg_ref, xn_ref):
    x = _rms(x_ref[...], g_ref[...])
    xn_ref[...] = x
    w = w_ref[...]
    xh = x.astype(BF16)
    xl = (x - xh.astype(F32)).astype(BF16)
    wh = w.astype(BF16)
    wl = (w - wh.astype(F32)).astype(BF16)
    d = lambda a, b: jnp.dot(a, b, preferred_element_type=F32)
    lg_ref[...] = d(xh, wh) + (d(xh, wl) + d(xl, wh))


def moe_router(x, g, w_router, *, tm=512):
    n, k = x.shape
    tm = _row_tile(n, tm)
    return pl.pallas_call(
        _router_kernel,
        grid=(n // tm,),
        in_specs=[pl.BlockSpec((tm, k), lambda i: (i, 0)),
                  pl.BlockSpec((1, k), lambda i: (0, 0)),
                  pl.BlockSpec((k, _ROUTER_W), lambda i: (0, 0))],
        out_specs=[pl.BlockSpec((tm, _ROUTER_W), lambda i: (i, 0)),
                   pl.BlockSpec((tm, k), lambda i: (i, 0))],
        out_shape=[jax.ShapeDtypeStruct((n, _ROUTER_W), F32), jax.ShapeDtypeStruct((n, k), F32)],
        compiler_params=_params("parallel"),
        name="moe_router",
    )(x, g.reshape(1, k), w_router)


def _expert_kernel(be_ref, act_ref, x_ref, gate_ref, w1_ref, w3_ref, w2_ref, o_ref):
    i = pl.program_id(0)

    @pl.when(act_ref[i] > 0)
    def _():
        x = x_ref[...].astype(BF16)
        h1 = jnp.dot(x, w1_ref[0, 0].astype(BF16), preferred_element_type=F32)
        h3 = jnp.dot(x, w3_ref[0, 0].astype(BF16), preferred_element_type=F32)
        hid = (_silu(h1) * h3).astype(BF16)
        out = jnp.dot(hid, w2_ref[0, 0].astype(BF16), preferred_element_type=F32)
        o_ref[...] = out * gate_ref[...]

    @pl.when(act_ref[i] == 0)
    def _():
        o_ref[...] = jnp.zeros_like(o_ref)


def moe_experts(xg, gate, blk_exp, blk_act, w1, w3, w2, layer):
    rows, d = xg.shape
    nb = rows // MOE_BLK
    ff = w1.shape[3]
    return pl.pallas_call(
        _expert_kernel,
        grid_spec=pltpu.PrefetchScalarGridSpec(
            num_scalar_prefetch=2,
            grid=(nb,),
            in_specs=[pl.BlockSpec((MOE_BLK, d), lambda i, be, act: (i, 0)),
                      pl.BlockSpec((MOE_BLK, 1), lambda i, be, act: (i, 0)),
                      pl.BlockSpec((1, 1, d, ff), lambda i, be, act: (layer, be[i], 0, 0)),
                      pl.BlockSpec((1, 1, d, ff), lambda i, be, act: (layer, be[i], 0, 0)),
                      pl.BlockSpec((1, 1, ff, d), lambda i, be, act: (layer, be[i], 0, 0))],
            out_specs=pl.BlockSpec((MOE_BLK, d), lambda i, be, act: (i, 0)),
        ),
        out_shape=jax.ShapeDtypeStruct((rows, d), F32),
        compiler_params=_params("arbitrary"),
        name="moe_experts",
    )(blk_exp, blk_act, xg, gate, w1, w3, w2)


def moe_layer(h, g, wg, bg, we, be, w1, w3, w2, layer):
    n, d = h.shape
    w_router = jnp.concatenate([wg, we, jnp.zeros((d, _ROUTER_W - MOE_GROUPS - MOE_E), F32)], axis=1)
    logits, xn = moe_router(h, g, w_router)
    lg = logits[:, :MOE_GROUPS] + bg
    grp = jnp.argmax(lg, axis=-1)
    gw = jnp.take_along_axis(jax.nn.softmax(lg, axis=-1), grp[:, None], axis=1)
    le = (logits[:, MOE_GROUPS:MOE_GROUPS + MOE_E] + be).reshape(n, MOE_GROUPS, MOE_EPG)
    le = jnp.take_along_axis(le, grp[:, None, None], axis=1)[:, 0]
    tv, ti = lax.top_k(jax.nn.softmax(le, axis=-1), MOE_TOPK)
    wts = gw * tv / jnp.sum(tv, axis=-1, keepdims=True)
    eid = (grp[:, None] * MOE_EPG + ti).reshape(-1).astype(jnp.int32)
    npair = n * MOE_TOPK
    experts = jnp.arange(MOE_E, dtype=jnp.int32)
    order = jnp.argsort(eid).astype(jnp.int32)
    rank = jnp.argsort(order).astype(jnp.int32)
    counts = jnp.sum(eid[:, None] == experts[None, :], axis=0).astype(jnp.int32)
    start = jnp.cumsum(counts) - counts
    padded = (counts + MOE_BLK - 1) // MOE_BLK * MOE_BLK
    pend = jnp.cumsum(padded)
    shift = pend - padded - start
    nb = -(-npair // MOE_BLK) + MOE_E
    blk_lo = jnp.arange(nb, dtype=jnp.int32) * MOE_BLK
    blk_exp = jnp.minimum(jnp.sum(pend[None, :] <= blk_lo[:, None], axis=1), MOE_E - 1).astype(jnp.int32)
    blk_act = (blk_lo < pend[-1]).astype(jnp.int32)
    src = (blk_lo - shift[blk_exp])[:, None] + jnp.arange(MOE_BLK, dtype=jnp.int32)[None, :]
    live = src < (start + counts)[blk_exp][:, None]
    pair = order[jnp.where(live, src, 0).reshape(-1)]
    buf_tok = pair // MOE_TOPK
    buf_gate = jnp.where(live.reshape(-1), wts.reshape(-1)[pair], 0.0)
    dest = (rank + shift[eid]).reshape(n, MOE_TOPK)
    out = moe_experts(xn[buf_tok], buf_gate[:, None], blk_exp, blk_act, w1, w3, w2, layer)
    return h + (out[dest[:, 0]] + out[dest[:, 1]])


_KT = 128
_NT = (((1,), (1,)), ((), ()))
_BIG = 3e38
_M0 = -1e29


def _compress_kernel(x_ref, pe_ref, w_ref, o_ref):
    acc = jnp.zeros(o_ref.shape, F32)
    for l in range(L_CMP):
        y = x_ref[:, l, :] + pe_ref[l:l + 1, :]
        acc = acc + jnp.dot(y.astype(BF16), w_ref[l], preferred_element_type=F32)
    o_ref[...] = acc


def nsa_compress(x, pe, w, *, tb=256):
    nb = x.shape[0]
    tb = _row_tile(nb, tb)
    pe2 = jnp.concatenate([pe] * NSA_KV, axis=1)
    z = jnp.zeros_like(w)
    w2 = jnp.concatenate([jnp.concatenate([w, z], axis=2), jnp.concatenate([z, w], axis=2)], axis=1).astype(BF16)
    return pl.pallas_call(
        _compress_kernel,
        grid=(nb // tb,),
        in_specs=[pl.BlockSpec((tb, L_CMP, NSA_KW), lambda i: (i, 0, 0)),
                  pl.BlockSpec((L_CMP, NSA_KW), lambda i: (0, 0)),
                  pl.BlockSpec((L_CMP, NSA_KW, NSA_KW), lambda i: (0, 0, 0))],
        out_specs=pl.BlockSpec((tb, NSA_KW), lambda i: (i, 0)),
        out_shape=jax.ShapeDtypeStruct((nb, NSA_KW), F32),
        compiler_params=_params("parallel"),
        name="nsa_compress",
    )(x, pe2, w2)


def _compress_cache_kernel(x_ref, pe_ref, w_ref, o_ref):
    for g in range(NSA_KV):
        acc = jnp.zeros((o_ref.shape[0], NSA_HD), F32)
        for l in range(L_CMP):
            y = x_ref[0, :, l, g, :] + pe_ref[l:l + 1, :]
            acc = acc + jnp.dot(y.astype(BF16), w_ref[l], preferred_element_type=F32)
        o_ref[:, g * NSA_HD:(g + 1) * NSA_HD] = acc


def nsa_compress_cache(cache, layer, pe, w, *, tb=256):
    nl, n_pool, page = cache.shape[:3]
    nb = n_pool * page // L_CMP
    tb = _row_tile(nb, tb)
    x = cache.reshape(nl, nb, L_CMP, NSA_KV, NSA_HD)
    return pl.pallas_call(
        _compress_cache_kernel,
        grid=(nb // tb,),
        in_specs=[pl.BlockSpec((1, tb, L_CMP, NSA_KV, NSA_HD), lambda i: (layer, i, 0, 0, 0)),
                  pl.BlockSpec((L_CMP, NSA_HD), lambda i: (0, 0)),
                  pl.BlockSpec((L_CMP, NSA_HD, NSA_HD), lambda i: (0, 0, 0))],
        out_specs=pl.BlockSpec((tb, NSA_KW), lambda i: (i, 0)),
        out_shape=jax.ShapeDtypeStruct((nb, NSA_KW), F32),
        compiler_params=_params("parallel"),
        name="nsa_compress_cache",
    )(x, pe, w.astype(BF16))


def _qz(q, g, nq):
    lane = lax.broadcasted_iota(jnp.int32, (nq, NSA_KW), 1)
    keep = (lane >> 6) == g
    parts = []
    for r in range(NSA_REP):
        h = g * NSA_REP + r
        slab = q[:, (h // 2) * NSA_KW:(h // 2 + 1) * NSA_KW]
        if h % 2 != g:
            slab = pltpu.roll(slab, NSA_HD, axis=1)
        parts.append(jnp.where(keep, slab, 0.0))
    return jnp.concatenate(parts, axis=0).astype(BF16)


def _softmax_init(m_ref, l_ref, acc_ref):
    m_ref[...] = jnp.full(m_ref.shape, NEG, F32)
    l_ref[...] = jnp.zeros(l_ref.shape, F32)
    acc_ref[...] = jnp.zeros(acc_ref.shape, F32)


def _softmax_tile(st, dpos, maskf, slope, vt, m_ref, l_ref, acc_ref, g):
    s = st - slope * dpos
    sm = jnp.where(maskf > 0.0, s, NEG)
    m_old = m_ref[g]
    m_new = jnp.maximum(m_old, jnp.max(sm, axis=0, keepdims=True))
    alpha = jnp.exp(m_old - m_new)
    p = jnp.exp(sm - m_new) * maskf
    l_ref[g] = alpha * l_ref[g] + jnp.sum(p, axis=0, keepdims=True)
    acc_ref[g] = alpha * acc_ref[g] + jnp.dot(vt, p.astype(BF16), preferred_element_type=F32)
    m_ref[g] = m_new


def _softmax_done(l_ref, acc_ref, g):
    l = l_ref[g]
    return acc_ref[g] * jnp.where(l > 0.0, 1.0 / l, 0.0)


def _compressed_branch(qz, kcb, vcbt_g, qpos, slope, ncb):
    st = lax.dot_general(kcb, qz, _NT, preferred_element_type=F32)
    row = lax.broadcasted_iota(jnp.int32, (ncb, 1), 0)
    half = ncb // 2
    blk = jnp.where(row < half, 2 * row, 2 * (row - half) + 1)
    c_pos = blk * L_CMP + (L_CMP - 1)
    d_c = qpos - c_pos
    maskf = jnp.where(d_c >= 0, 1.0, 0.0)
    s = st - slope * d_c.astype(F32)
    sm = jnp.where(d_c >= 0, s, NEG)
    m = jnp.max(sm, axis=0, keepdims=True)
    p = jnp.exp(sm - m) * maskf
    l = jnp.sum(p, axis=0, keepdims=True)
    p = p * jnp.where(l > 0.0, 1.0 / l, 0.0)
    o = jnp.dot(vcbt_g, p.astype(BF16), preferred_element_type=F32)
    return o, p


def _select_blocks(imp_sel, qpos, nsp):
    cols = imp_sel.shape[1]
    blk = lax.broadcasted_iota(jnp.int32, (nsp, cols), 0)
    cur = qpos >> 6
    valid = blk <= cur
    forced = jnp.where(valid, jnp.where(blk == 0, 1.0, jnp.where(blk >= cur - 1, 1.0, 0.0)), 0.0)
    score = jnp.where(forced > 0.0, _BIG, jnp.where(valid, imp_sel, -1.0))
    sel = jnp.zeros((nsp, cols), F32)
    for _ in range(N_SEL):
        m = jnp.max(score, axis=0, keepdims=True)
        idx = jnp.min(jnp.where(score == m, blk, nsp + 1), axis=0, keepdims=True)
        pick = blk == idx
        sel = jnp.where(pick, 1.0, sel)
        score = jnp.where(pick, -2.0, score)
    return jnp.where(valid, sel, 0.0)


def _nsa_prompt_kernel(q_ref, glog_ref, kcb_ref, vcbt_ref, ks_ref, vst_ref, kw_ref, vwt_ref,
                       qoff_ref, slope_ref, qaux_ref, o_ref, sel_ref, m_ref, l_ref, acc_ref, *, nq, ncb):
    i = pl.program_id(1)
    st0 = i * nq
    qoff = qoff_ref[...]
    qpos = st0 + qoff
    q = q_ref[0] * (NSA_HD ** -0.5)
    key_io = lax.broadcasted_iota(jnp.int32, (_KT, 1), 0)
    lane = lax.broadcasted_iota(jnp.int32, (_KT, NSA_KW), 1)
    key_lane = jnp.where(lane == 1, lax.broadcasted_iota(jnp.int32, (_KT, NSA_KW), 0), 0).astype(F32)
    causal = key_io <= qoff
    anti = key_io >= qoff
    nsp = ncb // 2
    wt = WINDOW // _KT

    def m_init():
        m_ref[...] = jnp.full(m_ref.shape, _M0, F32)
        l_ref[...] = jnp.zeros(l_ref.shape, F32)
        acc_ref[...] = jnp.zeros(acc_ref.shape, F32)

    for g in range(NSA_KV):
        qz = _qz(q, g, nq)
        qx = jnp.concatenate([qz, qaux_ref[g]], axis=1)
        slope = slope_ref[g]
        rows = slice(g * NSA_HD, (g + 1) * NSA_HD)
        o_c, p_c = _compressed_branch(qz, kcb_ref[0], vcbt_ref[0, rows, :], qpos, slope, ncb)
        imp = p_c[:, 0:nq]
        for r in range(1, NSA_REP):
            imp = imp + p_c[:, r * nq:(r + 1) * nq]
        imp_sel = imp[0:nsp] + imp[nsp:ncb]
        sel = _select_blocks(imp_sel, qpos[:, 0:nq], nsp)
        sel_ref[g] = jnp.concatenate([jnp.where(sel > 0.0, 0.0, NEG)] * NSA_REP, axis=1)
        blk = lax.broadcasted_iota(jnp.int32, (nsp, 1), 0)
        row_any = jnp.max(sel, axis=1, keepdims=True)
        first = jnp.min(jnp.where(row_any > 0.0, jnp.where(blk >= 2, blk, 2 * nsp), 2 * nsp), axis=0, keepdims=True)
        kt_lo = jnp.minimum(first[0, 0] >> 1, i)

        def tile(k_ref, vt_ref, kt, mode, with_sel):
            off = kt * _KT if isinstance(kt, int) else pl.multiple_of(kt * _KT, _KT)
            kaux =jnp.where(lane == 0, (kt - i).astype(F32), key_lane).astype(BF16)
            kx = jnp.concatenate([k_ref[0, pl.ds(off, _KT), :], kaux], axis=1)
            s = lax.dot_general(kx, qx, _NT, preferred_element_type=F32)
            if with_sel:
                half = _KT // 2
                s = jnp.concatenate([s[0:half] + sel_ref[g, pl.ds(2 * kt, 1), :],
                                     s[half:_KT] + sel_ref[g, pl.ds(2 * kt + 1, 1), :]], axis=0)
            if mode == "causal":
                s = jnp.where(causal, s, NEG)
            elif mode == "anti":
                s = jnp.where(anti, s, NEG)
            m_old = m_ref[g]
            m_new = jnp.maximum(m_old, jnp.max(s, axis=0, keepdims=True))
            alpha = jnp.exp(m_old - m_new)
            p = jnp.exp(s - m_new)
            l_ref[g] = alpha * l_ref[g] + jnp.sum(p, axis=0, keepdims=True)
            vt = vt_ref[0, rows, pl.ds(off, _KT)]
            acc_ref[g] = alpha * acc_ref[g] + jnp.dot(vt, p.astype(BF16), preferred_element_type=F32)
            m_ref[g] = m_new

        m_init()

        @pl.when(i > 0)
        def _():
            tile(ks_ref, vst_ref, 0, "none", True)

        def sel_body(kt, carry):
            tile(ks_ref, vst_ref, kt, "none", True)
            return carry

        lax.fori_loop(jnp.maximum(kt_lo, 1), i, sel_body, 0)
        tile(ks_ref, vst_ref, i, "causal", True)
        o_s = _softmax_done(l_ref, acc_ref, g)

        m_init()

        @pl.when(i >= wt)
        def _():
            tile(kw_ref, vwt_ref, i - wt, "anti", False)

        def win_body(kt, carry):
            tile(kw_ref, vwt_ref, kt, "none", False)
            return carry

        lax.fori_loop(jnp.maximum(i - wt + 1, 0), i, win_body, 0)
        tile(kw_ref, vwt_ref, i, "causal", False)
        o_w = _softmax_done(l_ref, acc_ref, g)

        gc = _sigmoid(glog_ref[0, 0, g:g + 1, :])
        gs = _sigmoid(glog_ref[0, 0, 2 + g:3 + g, :])
        gw = _sigmoid(glog_ref[0, 0, 4 + g:5 + g, :])
        o_ref[0, 0, g] = gc * o_c + gs * o_s + gw * o_w


def _nsa_cols(nq):
    c = NSA_REP * nq
    qoff = (jnp.arange(c, dtype=jnp.int32) % nq).reshape(1, c)
    slopes = 2.0 ** (-8.0 * jnp.arange(1, NSA_HEADS + 1, dtype=F32) / NSA_HEADS)
    slope = jnp.repeat(slopes.reshape(NSA_KV, NSA_REP), nq, axis=1).reshape(NSA_KV, 1, c)
    return qoff, slope


def _gate_cols(glog, nq):
    b, t, _ = glog.shape
    x = glog.reshape(b, t // nq, nq, NSA_KV, NSA_REP, 3)
    return x.transpose(0, 1, 5, 3, 4, 2).reshape(b, t // nq, 3 * NSA_KV, NSA_REP * nq)


def _even_odd(x):
    return jnp.concatenate([x[:, 0::2], x[:, 1::2]], axis=1)


def _uncols(o, nq):
    b, nb = o.shape[:2]
    x = o.reshape(b, nb, NSA_KV, NSA_HD, NSA_REP, nq)
    return x.transpose(0, 1, 5, 2, 4, 3).reshape(b, nb * nq, NSA_W)


def nsa_prompt(q, glog, kcb, vcb, ks, vs, kw, vw, *, nq=128):
    b, t, _ = q.shape
    ncb = kcb.shape[1]
    c = NSA_REP * nq
    nblk = t // nq
    assert nq == _KT
    qoff, slope = _nsa_cols(nq)
    lane = jnp.arange(NSA_KW)[None, None, :]
    slope_col = slope.reshape(NSA_KV, c, 1)
    qaux = jnp.where(lane == 0, slope_col * _KT, jnp.where(lane == 1, slope_col, 0.0)).astype(BF16)
    kcb_p = _even_odd(kcb).astype(BF16)
    vcbt = _even_odd(vcb).transpose(0, 2, 1).astype(BF16)
    per_b = lambda shape: pl.BlockSpec((1,) + shape, lambda i, j: (i, 0, 0))
    out = pl.pallas_call(
        functools.partial(_nsa_prompt_kernel, nq=nq, ncb=ncb),
        grid=(b, nblk),
        in_specs=[pl.BlockSpec((1, nq, NSA_W), lambda i, j: (i, j, 0)),
                  pl.BlockSpec((1, 1, 3 * NSA_KV, c), lambda i, j: (i, j, 0, 0)),
                  per_b((ncb, NSA_KW)), per_b((NSA_KW, ncb)),
                  per_b((t, NSA_KW)), per_b((NSA_KW, t)), per_b((t, NSA_KW)), per_b((NSA_KW, t)),
                  pl.BlockSpec((1, c), lambda i, j: (0, 0)),
                  pl.BlockSpec((NSA_KV, 1, c), lambda i, j: (0, 0, 0)),
                  pl.BlockSpec((NSA_KV, c, NSA_KW), lambda i, j: (0, 0, 0))],
        out_specs=pl.BlockSpec((1, 1, NSA_KV, NSA_HD, c), lambda i, j: (i, j, 0, 0, 0)),
        out_shape=jax.ShapeDtypeStruct((b, nblk, NSA_KV, NSA_HD, c), F32),
        scratch_shapes=[pltpu.VMEM((NSA_KV, ncb // 2, c), F32),
                        pltpu.VMEM((NSA_KV, 1, c), F32), pltpu.VMEM((NSA_KV, 1, c), F32),
                        pltpu.VMEM((NSA_KV, NSA_HD, c), F32)],
        compiler_params=_params("parallel", "arbitrary"),
        name="nsa_prompt",
    )(q, _gate_cols(glog, nq), kcb_p, vcbt, ks.astype(BF16), vs.transpose(0, 2, 1).astype(BF16),
      kw.astype(BF16), vw.transpose(0, 2, 1).astype(BF16), qoff, slope, qaux)
    return _uncols(out, nq)


def _nsa_sample_kernel(pt_ref, q_ref, glog_ref, kcb_ref, vcbt_ref, *refs, nq, ncb, npages, page, past, wb):
    ks_pages = refs[0:npages]
    vs_pages = refs[npages:2 * npages]
    (nks_ref, nvs_ref, nkw_ref, nvw_ref, wk_ref, wv_ref, qoff_ref, slope_ref, rmat_ref,
     o_ref, wko_ref, wvo_ref, m_ref, l_ref, acc_ref) = refs[2 * npages:]
    del pt_ref
    c = NSA_REP * nq
    qpos = past + qoff_ref[...]
    q = q_ref[0] * (NSA_HD ** -0.5)
    key_io = lax.broadcasted_iota(jnp.int32, (_KT, 1), 0)
    nsp = sel_rows = -(-(past + nq) // L_SEL)
    nsp = -(-nsp // 8) * 8
    pad_rows = lambda x: jnp.concatenate([x, jnp.zeros((_KT - nq, x.shape[1]), F32)], axis=0)

    qzs, slopes, sels, o_cs = [], [], [], []
    for g in range(NSA_KV):
        qz = _qz(q, g, nq)
        slope = slope_ref[g]
        o_c, p_c = _compressed_branch(qz, kcb_ref[0], vcbt_ref[0, g * NSA_HD:(g + 1) * NSA_HD, :], qpos, slope, ncb)
        imp = _dot_exact_rhs(p_c, rmat_ref[...])
        imp_sel = imp[0:ncb // 2] + imp[ncb // 2:ncb]
        imp_sel = jnp.concatenate([imp_sel, jnp.zeros((nsp - ncb // 2, c), F32)], axis=0)
        qzs.append(qz)
        slopes.append(slope)
        sels.append(_select_blocks(imp_sel, qpos, nsp))
        o_cs.append(o_c)
    del sel_rows

    qgs = [qzs[g][:, g * NSA_HD:(g + 1) * NSA_HD] for g in range(NSA_KV)]

    def run_tile(kv_fn, tok0, mask_fn):
        dpos = qpos - (tok0 + key_io)
        for g in range(NSA_KV):
            k, v = kv_fn(g)
            st = lax.dot_general(k.astype(BF16), qgs[g], _NT, preferred_element_type=F32)
            maskf = mask_fn(g, dpos)
            s = st - slopes[g] * dpos.astype(F32)
            sm = jnp.where(maskf > 0.0, s, NEG)
            m_old = m_ref[g]
            m_new = jnp.maximum(m_old, jnp.max(sm, axis=0, keepdims=True))
            alpha = jnp.exp(m_old - m_new)
            p = jnp.exp(sm - m_new) * maskf
            l_ref[g] = alpha * l_ref[g] + jnp.sum(p, axis=0, keepdims=True)
            pv = lax.dot_general(v.astype(BF16), p.astype(BF16), _TN, preferred_element_type=F32)
            acc_ref[g] = alpha * acc_ref[g] + pv
            m_ref[g] = m_new

    cache_tile = lambda k_ref, v_ref, lo: (lambda g: (k_ref[0, 0, lo:lo + _KT, g, :], v_ref[0, 0, lo:lo + _KT, g, :]))
    new_tile = lambda k_ref, v_ref: (lambda g: (pad_rows(k_ref[0, :, g * NSA_HD:(g + 1) * NSA_HD]),
                                                pad_rows(v_ref[0, :, g * NSA_HD:(g + 1) * NSA_HD])))

    _softmax_init(m_ref, l_ref, acc_ref)
    for j in range(npages + 1):
        blk0 = j * (page // L_SEL)

        def sel_mask(g, dpos, blk0=blk0):
            selrow = jnp.where(key_io < L_SEL, sels[g][blk0:blk0 + 1, :], sels[g][blk0 + 1:blk0 + 2, :])
            return jnp.where(dpos >= 0, selrow, 0.0)

        if j < npages:
            run_tile(cache_tile(ks_pages[j], vs_pages[j], 0), j * page, sel_mask)
        else:
            run_tile(new_tile(nks_ref, nvs_ref), past, sel_mask)
    o_ss = [_softmax_done(l_ref, acc_ref, g) for g in range(NSA_KV)]

    _softmax_init(m_ref, l_ref, acc_ref)
    win_mask = lambda g, dpos: jnp.where(dpos >= 0, jnp.where(dpos <= WINDOW, 1.0, 0.0), 0.0)
    for j in range(wb // _KT):
        run_tile(cache_tile(wk_ref, wv_ref, j * _KT), past - wb + j * _KT, win_mask)
    run_tile(new_tile(nkw_ref, nvw_ref), past, win_mask)
    for g in range(NSA_KV):
        o_w = _softmax_done(l_ref, acc_ref, g)
        gc = _sigmoid(glog_ref[0, 0, g:g + 1, :])
        gs = _sigmoid(glog_ref[0, 0, 2 + g:3 + g, :])
        gw = _sigmoid(glog_ref[0, 0, 4 + g:5 + g, :])
        o_ref[0, 0, g] = gc * o_cs[g] + gs * o_ss[g] + gw * o_w

    wko_ref[0, 0:wb - nq] = wk_ref[0, 0, nq:wb]
    wvo_ref[0, 0:wb - nq] = wv_ref[0, 0, nq:wb]
    for g in range(NSA_KV):
        wko_ref[0, wb - nq:wb, g, :] = nkw_ref[0, :, g * NSA_HD:(g + 1) * NSA_HD]
        wvo_ref[0, wb - nq:wb, g, :] = nvw_ref[0, :, g * NSA_HD:(g + 1) * NSA_HD]


def nsa_sample(q, glog, kcb, vcb, pool_k, pool_v, page_table, nks, nvs, nkw, nvw, win_k, win_v, layer):
    b, nq, _ = q.shape
    ncb = kcb.shape[1]
    npages = page_table.shape[1]
    page = pool_k.shape[2]
    past = npages * page
    wb = win_k.shape[2]
    assert page == _KT and wb % _KT == 0 and nq % 8 == 0 and nq <= L_SEL and ncb % 2 == 0
    c = NSA_REP * nq
    qoff, slope = _nsa_cols(nq)
    col = jnp.arange(c, dtype=jnp.int32)
    rmat = (col[:, None] % nq == col[None, :] % nq).astype(BF16)
    kcb_p = _even_odd(kcb).astype(BF16)
    vcbt = _even_odd(vcb).transpose(0, 2, 1).astype(BF16)
    per_b = lambda shape: pl.BlockSpec((1,) + shape, lambda i, pt: (i,) + (0,) * len(shape))
    const = lambda shape: pl.BlockSpec(shape, lambda i, pt: (0,) * len(shape))
    page_spec = lambda j: pl.BlockSpec((1, 1, page, NSA_KV, NSA_HD), lambda i, pt: (layer, pt[i, j], 0, 0, 0))
    win_spec = pl.BlockSpec((1, 1, wb, NSA_KV, NSA_HD), lambda i, pt: (layer, i, 0, 0, 0))
    in_specs = ([per_b((nq, NSA_W)), per_b((1, 3 * NSA_KV, c)), per_b((ncb, NSA_KW)), per_b((NSA_KW, ncb))]
                + [page_spec(j) for j in range(npages)] * 2
                + [per_b((nq, NSA_KW))] * 4 + [win_spec] * 2
                + [const((1, c)), const((NSA_KV, 1, c)), const((c, c))])
    out, wko, wvo = pl.pallas_call(
        functools.partial(_nsa_sample_kernel, nq=nq, ncb=ncb, npages=npages, page=page, past=past, wb=wb),
        grid_spec=pltpu.PrefetchScalarGridSpec(
            num_scalar_prefetch=1,
            grid=(b,),
            in_specs=in_specs,
            out_specs=[per_b((1, NSA_KV, NSA_HD, c)), per_b((wb, NSA_KV, NSA_HD)), per_b((wb, NSA_KV, NSA_HD))],
            scratch_shapes=[pltpu.VMEM((NSA_KV, 1, c), F32), pltpu.VMEM((NSA_KV, 1, c), F32),
                            pltpu.VMEM((NSA_KV, NSA_HD, c), F32)],
        ),
        out_shape=[jax.ShapeDtypeStruct((b, 1, NSA_KV, NSA_HD, c), F32),
                   jax.ShapeDtypeStruct((b, wb, NSA_KV, NSA_HD), F32),
                   jax.ShapeDtypeStruct((b, wb, NSA_KV, NSA_HD), F32)],
        compiler_params=_params("arbitrary"),
        name="nsa_sample",
    )(page_table, q, _gate_cols(glog, nq), kcb_p, vcbt, *([pool_k] * npages), *([pool_v] * npages),
      nks, nvs, nkw, nvw, win_k, win_v, qoff, slope, rmat)
    return _uncols(out, nq), wko, wvo


_HALO_M = 8
_TN = (((0,), (0,)), ((), ()))


def _softplus(x):
    return jnp.maximum(x, 0.0) + jnp.log1p(jnp.exp(-jnp.abs(x)))


def _ssd_kernel(xbc_ref, z_ref, sm_ref, dtt_ref, cs_ref, h0_ref, cw_ref, cb_ref, dtb_ref, dtbt_ref,
                al_ref, alt_ref, dsk_ref, ng_ref, y_ref, ncs_ref, hf_ref, ext_ref, h_ref, yh_ref, *, ql, dt_col):
    c = pl.program_id(1)
    nc = pl.num_programs(1)

    @pl.when(c == 0)
    def _():
        ext_ref[...] = jnp.zeros_like(ext_ref)
        ext_ref[_HALO_M - (M_CONV_W - 1):_HALO_M, :] = cs_ref[0]
        h_ref[...] = h0_ref[0]

    @pl.when(c > 0)
    def _():
        ext_ref[0:_HALO_M, :] = ext_ref[ql:ql + _HALO_M, :]

    ext_ref[_HALO_M:_HALO_M + ql, :] = xbc_ref[0]
    acc = jnp.zeros((ql, M_CONV_DIM), F32)
    for k in range(M_CONV_W):
        off = _HALO_M - (M_CONV_W - 1) + k
        acc = acc + ext_ref[off:off + ql, :] * cw_ref[k:k + 1, :]
    xbc = _silu(acc + cb_ref[...])
    xs = xbc[:, 0:M_DIN]
    bm = xbc[:, M_DIN:M_DIN + M_GROUPS * M_DSTATE]
    cm = xbc[:, M_DIN + M_GROUPS * M_DSTATE:M_CONV_DIM]

    dt = _softplus(sm_ref[0, :, dt_col:dt_col + M_HEADS] + dtb_ref[...])
    dtt = _softplus(dtt_ref[0] + dtbt_ref[...])
    dta = dt * (-jnp.exp(al_ref[...]))
    dtat = dtt * (-jnp.exp(alt_ref[...]))
    ti = lax.broadcasted_iota(jnp.int32, (ql, ql), 0)
    si = lax.broadcasted_iota(jnp.int32, (ql, ql), 1)
    causal = si <= ti
    cum = _dot_exact_lhs(jnp.where(causal, 1.0, 0.0).astype(BF16), dta)
    cumt = _dot_exact_rhs(dtat, jnp.where(ti <= si, 1.0, 0.0).astype(BF16))
    cum_last = cum[ql - 1:ql, :]
    edec = jnp.exp(cum)
    eend = jnp.exp(cum_last - cum)
    elast = jnp.exp(cum_last)

    rep = M_HEADS // M_GROUPS
    for gi in range(M_GROUPS):
        b_g = bm[:, gi * M_DSTATE:(gi + 1) * M_DSTATE]
        c_g = cm[:, gi * M_DSTATE:(gi + 1) * M_DSTATE].astype(BF16)
        cb = lax.dot_general(c_g, b_g.astype(BF16), _NT, preferred_element_type=F32)
        for hh in range(rep):
            h = gi * rep + hh
            hs = slice(h * M_HDIM, (h + 1) * M_HDIM)
            lmat = jnp.where(causal, jnp.exp(cum[:, h:h + 1] - cumt[h:h + 1, :]), 0.0)
            x_h = xs[:, hs]
            xdt = (x_h * dt[:, h:h + 1]).astype(BF16)
            y_diag = jnp.dot((cb * lmat).astype(BF16), xdt, preferred_element_type=F32)
            h_in = h_ref[h]
            y_off = lax.dot_general(c_g, h_in.astype(BF16), _NT, preferred_element_type=F32) * edec[:, h:h + 1]
            bd = (b_g * eend[:, h:h + 1]).astype(BF16)
            s_chunk = lax.dot_general(xdt, bd, _TN, preferred_element_type=F32)
            h_ref[h] = elast[:, h:h + 1] * h_in + s_chunk
            yh_ref[:, hs] = y_diag + y_off + dsk_ref[:, hs] * x_h

    yz = yh_ref[...] * _silu(z_ref[0])
    y_ref[0] = _rms(yz, ng_ref[...])

    @pl.when(c == nc - 1)
    def _():
        ncs_ref[0] = ext_ref[_HALO_M + ql - (M_CONV_W - 1):_HALO_M + ql, :]
        hf_ref[0] = h_ref[...]


def ssd_mixer(xbc, z, small, dt_col, conv_state, h0, conv_w, conv_b, dt_bias, a_log, d_skip, norm_g, *, ql):
    b, t, _ = xbc.shape
    nc = t // ql
    sw = small.shape[2]
    dtt = small[:, :, dt_col:dt_col + M_HEADS].transpose(0, 2, 1)
    const = lambda shape: pl.BlockSpec(shape, lambda i, j: (0,) * len(shape))
    per_b = lambda shape: pl.BlockSpec((1,) + shape, lambda i, j: (i,) + (0,) * len(shape))
    row = lambda x: x.reshape(1, -1)
    colv = lambda x: x.reshape(-1, 1)
    return pl.pallas_call(
        functools.partial(_ssd_kernel, ql=ql, dt_col=dt_col),
        grid=(b, nc),
        in_specs=[pl.BlockSpec((1, ql, M_CONV_DIM), lambda i, j: (i, j, 0)),
                  pl.BlockSpec((1, ql, M_DIN), lambda i, j: (i, j, 0)),
                  pl.BlockSpec((1, ql, sw), lambda i, j: (i, j, 0)),
                  pl.BlockSpec((1, M_HEADS, ql), lambda i, j: (i, 0, j)),
                  per_b((M_CONV_W - 1, M_CONV_DIM)), per_b((M_HEADS, M_HDIM, M_DSTATE)),
                  const((M_CONV_W, M_CONV_DIM)), const((1, M_CONV_DIM)),
                  const((1, M_HEADS)), const((M_HEADS, 1)), const((1, M_HEADS)), const((M_HEADS, 1)),
                  const((1, M_DIN)), const((1, M_DIN))],
        out_specs=[pl.BlockSpec((1, ql, M_DIN), lambda i, j: (i, j, 0)),
                   per_b((M_CONV_W - 1, M_CONV_DIM)), per_b((M_HEADS, M_HDIM, M_DSTATE))],
        out_shape=[jax.ShapeDtypeStruct((b, t, M_DIN), F32),
                   jax.ShapeDtypeStruct((b, M_CONV_W - 1, M_CONV_DIM), F32),
                   jax.ShapeDtypeStruct((b, M_HEADS, M_HDIM, M_DSTATE), F32)],
        scratch_shapes=[pltpu.VMEM((_HALO_M + ql, M_CONV_DIM), F32),
                        pltpu.VMEM((M_HEADS, M_HDIM, M_DSTATE), F32),
                        pltpu.VMEM((ql, M_DIN), F32)],
        compiler_params=_params("parallel", "arbitrary"),
        name="ssd_mixer",
    )(xbc, z, small, dtt, conv_state, h0, conv_w, row(conv_b), row(dt_bias), colv(dt_bias),
      row(a_log), colv(a_log), row(jnp.repeat(d_skip, M_HDIM)), row(norm_g))


_SMALL_W = 128
_OD_SPLITS = (NSA_W, 6 * NSA_KW, M_DIN, M_CONV_DIM, _SMALL_W)


def _odd_w_in(w):
    o_kv = NSA_W
    o_gate = o_kv + 6 * NSA_KW
    o_z = o_gate + 3 * NSA_HEADS
    o_xbc = o_z + M_DIN
    o_dt = o_xbc + M_CONV_DIM
    pad = jnp.zeros((w.shape[0], _SMALL_W - 3 * NSA_HEADS - M_HEADS), F32)
    return jnp.concatenate([w[:, :o_gate], w[:, o_z:o_xbc], w[:, o_xbc:o_dt],
                            w[:, o_gate:o_z], w[:, o_dt:], pad], axis=1)


def kernel(x_prompt, x_sample, state_conv_a, state_conv_b, cache_cmp_k, cache_cmp_v, cache_sel_k, cache_sel_v, cache_win_k, cache_win_v, state_ssm, state_ssm_conv, cache_mem_k, cache_mem_v, page_table, mem_prompt, norm_mix, norm_xattn, norm_ffn, norm_final, ev_w_in, ev_conv_a, ev_conv_b, ev_conv_b_bias, ev_ln_g, ev_ln_b, ev_w_out, od_w_in, od_cmp_pe, od_cmp_wk, od_cmp_wv, od_ssm_conv_w, od_ssm_conv_b, od_dt_bias, od_a_log, od_d_skip, od_ssm_norm, od_w_out, xa_wq, xa_wk, xa_wv, xa_wo, moe_wg, moe_bg, moe_we, moe_be, moe_w1, moe_w3, moe_w2):
    bp, tp, d = x_prompt.shape
    bs, ts, _ = x_sample.shape
    n_p, n_s = bp * tp, bs * ts
    n_mem = mem_prompt.shape[1]
    depth = norm_mix.shape[0]
    n_pool, page = cache_cmp_k.shape[1:3]
    wb = cache_win_k.shape[2]
    dt_col = 3 * NSA_HEADS

    def groups(a):
        return a[:n_p].reshape(bp, tp, a.shape[-1]), a[n_p:].reshape(bs, ts, a.shape[-1])

    def rows(a_p, a_s):
        return jnp.concatenate([a_p.reshape(n_p, a_p.shape[-1]), a_s.reshape(n_s, a_s.shape[-1])], axis=0)

    h = rows(x_prompt, x_sample)
    out = {k: [] for k in ("ca_p", "ca_s", "cb_p", "cb_s", "wk_p", "wk_s", "wv_p", "wv_s",
                           "sm_p", "sm_s", "sc_p", "sc_s", "mk_p", "mv_p")}
    rows_p = [[], [], [], []]
    rows_s = [[], [], [], []]
    for i in range(depth):
        j = i // 2
        if i % 2 == 0:
            u_p, u_s = groups(norm_matmul(h, norm_mix[i], ev_w_in[j]))
            ev = (ev_conv_a[j], ev_conv_b[j], ev_conv_b_bias[j], ev_ln_g[j], ev_ln_b[j])
            y_p, na_p, nb_p = even_conv(u_p, jnp.zeros((bp, CONV_A_W - 1, D_A), F32),
                                        jnp.zeros((bp, CONV_B_W - 1, D_B), F32), *ev)
            y_s, na_s, nb_s = even_conv(u_s, state_conv_a[j], state_conv_b[j], *ev)
            h = matmul_res([rows(y_p, y_s)], [ev_w_out[j]], h)
            out["ca_p"].append(na_p)
            out["ca_s"].append(na_s)
            out["cb_p"].append(nb_p)
            out["cb_s"].append(nb_s)
        else:
            uq, ukv, uz, uxbc, usm = norm_matmul(h, norm_mix[i], _odd_w_in(od_w_in[j]), splits=_OD_SPLITS)
            q_p, q_s = groups(uq)
            kv_p, kv_s = groups(ukv)
            z_p, z_s = groups(uz)
            xbc_p, xbc_s = groups(uxbc)
            sm_p, sm_s = groups(usm)
            part = lambda a, k: a[:, :, k * NSA_KW:(k + 1) * NSA_KW]
            kvp = [part(kv_p, k) for k in range(6)]
            kvs = [part(kv_s, k) for k in range(6)]
            pe, wck, wcv = od_cmp_pe[j], od_cmp_wk[j], od_cmp_wv[j]
            mw = (od_ssm_conv_w[j], od_ssm_conv_b[j], od_dt_bias[j], od_a_log[j], od_d_skip[j], od_ssm_norm[j])
            blocks = lambda a: a.reshape(-1, L_CMP, NSA_KW)
            ncb = tp // L_CMP
            kcb_p = nsa_compress(blocks(kvp[0][:, :ncb * L_CMP]), pe, wck).reshape(bp, ncb, NSA_KW)
            vcb_p = nsa_compress(blocks(kvp[1][:, :ncb * L_CMP]), pe, wcv).reshape(bp, ncb, NSA_KW)
            o_p = nsa_prompt(q_p, sm_p[:, :, :dt_col], kcb_p, vcb_p, kvp[2], kvp[3], kvp[4], kvp[5])
            keep = min(WINDOW, tp)
            y_p, nsc_p, nsm_p = ssd_mixer(xbc_p, z_p, sm_p, dt_col, jnp.zeros((bp, M_CONV_W - 1, M_CONV_DIM), F32),
                                          jnp.zeros((bp, M_HEADS, M_HDIM, M_DSTATE), F32), *mw, ql=128)
            kcp = nsa_compress_cache(cache_cmp_k, j, pe, wck).reshape(n_pool, page // L_CMP, NSA_KW)
            vcp = nsa_compress_cache(cache_cmp_v, j, pe, wcv).reshape(n_pool, page // L_CMP, NSA_KW)
            kcb_s = kcp[page_table].reshape(bs, -1, NSA_KW)
            vcb_s = vcp[page_table].reshape(bs, -1, NSA_KW)
            o_s, nwk_s, nwv_s = nsa_sample(
                q_s, sm_s[:, :, :dt_col], kcb_s, vcb_s, cache_sel_k, cache_sel_v, page_table,
                kvs[2], kvs[3], kvs[4], kvs[5], cache_win_k, cache_win_v, j)
            y_s, nsc_s, nsm_s = ssd_mixer(xbc_s, z_s, sm_s, dt_col, state_ssm_conv[j], state_ssm[j], *mw, ql=ts)
            w_out = od_w_out[j]
            h = matmul_res([rows(o_p, o_s), rows(y_p, y_s)], [w_out[:NSA_W], w_out[NSA_W:]], h)
            heads = lambda a: a.reshape(a.shape[0], a.shape[1], NSA_KV, NSA_HD)
            for k in range(4):
                rows_p[k].append(heads(kvp[k]))
                rows_s[k].append(heads(kvs[k]))
            out["wk_p"].append(heads(kvp[4][:, tp - keep:]))
            out["wv_p"].append(heads(kvp[5][:, tp - keep:]))
            out["wk_s"].append(nwk_s)
            out["wv_s"].append(nwv_s)
            out["sc_p"].append(nsc_p)
            out["sc_s"].append(nsc_s)
            out["sm_p"].append(nsm_p)
            out["sm_s"].append(nsm_s)
        mk, mv = norm_matmul(mem_prompt.reshape(bp * n_mem, d), None,
                             jnp.concatenate([xa_wk[i], xa_wv[i]], axis=1), norm=False,
                             splits=(XA_HEADS * XA_HD, XA_HEADS * XA_HD))
        mk = mk.reshape(bp, n_mem, XA_HEADS * XA_HD)
        mv = mv.reshape(bp, n_mem, XA_HEADS * XA_HD)
        out["mk_p"].append(mk.reshape(bp, n_mem, XA_HEADS, XA_HD))
        out["mv_p"].append(mv.reshape(bp, n_mem, XA_HEADS, XA_HD))
        qx_p, qx_s = groups(norm_matmul(h, norm_xattn[i], xa_wq[i]))
        ox_p = xattn(qx_p, mk, mv)
        ox_s = xattn(qx_s, cache_mem_k, cache_mem_v, layer=i)
        h = matmul_res([rows(ox_p, ox_s)], [xa_wo[i]], h)
        h = moe_layer(h, norm_ffn[i], moe_wg[i], moe_bg[i], moe_we[i], moe_be[i], moe_w1, moe_w3, moe_w2, i)
    y = rmsnorm_rows(h, norm_final)
    y_prompt = y[:n_p].reshape(bp, tp, d)
    y_sample = y[n_p:].reshape(bs, ts, d)
    st = lambda k: jnp.stack(out[k])
    return (y_prompt, y_sample, st("ca_p"), st("ca_s"), st("cb_p"), st("cb_s"),
            jnp.stack(rows_p[0]), jnp.stack(rows_s[0]), jnp.stack(rows_p[1]), jnp.stack(rows_s[1]),
            jnp.stack(rows_p[2]), jnp.stack(rows_s[2]), jnp.stack(rows_p[3]), jnp.stack(rows_s[3]),
            st("wk_p"), st("wk_s"), st("wv_p"), st("wv_s"), st("sm_p"), st("sm_s"), st("sc_p"), st("sc_s"),
            st("mk_p"), st("mv_p"))
```

```python
import functools

import jax
import jax.numpy as jnp
from jax import lax
from jax.experimental import pallas as pl
from jax.experimental.pallas import tpu as pltpu

F32 = jnp.float32
BF16 = jnp.bfloat16
EPS = 1e-6
NEG = -1e30
VMEM_LIMIT = 56 * 1024 * 1024

D_A = 512
D_B = 512
CONV_A_W = 3
CONV_B_W = 31
NSA_HEADS = 8
NSA_HD = 64
NSA_KV = 2
NSA_REP = NSA_HEADS // NSA_KV
NSA_W = NSA_HEADS * NSA_HD
NSA_KW = NSA_KV * NSA_HD
L_CMP = 32
L_SEL = 64
N_SEL = 16
WINDOW = 512
M_DIN = 512
M_HDIM = 64
M_HEADS = 8
M_DSTATE = 64
M_GROUPS = 2
M_CONV_W = 4
M_CONV_DIM = M_DIN + 2 * M_GROUPS * M_DSTATE
XA_HEADS = 4
XA_HD = 128
MOE_GROUPS = 4
MOE_EPG = 8
MOE_E = 32
MOE_TOPK = 2


def _params(*sem):
    return pltpu.CompilerParams(dimension_semantics=sem, vmem_limit_bytes=VMEM_LIMIT)


def _row_tile(n, pref):
    t = min(n, pref)
    while n % t or (t % 8 and t != n):
        t -= 1
    return t


def _bdot(a, b):
    return jnp.dot(a.astype(BF16), b.astype(BF16), preferred_element_type=F32)


def _split3(a):
    hi = a.astype(BF16)
    r1 = a - hi.astype(F32)
    mid = r1.astype(BF16)
    lo = (r1 - mid.astype(F32)).astype(BF16)
    return hi, mid, lo


def _dot_exact_rhs(a, b_bf16):
    hi, mid, lo = _split3(a)
    d = lambda x: jnp.dot(x, b_bf16, preferred_element_type=F32)
    return d(hi) + d(mid) + d(lo)


def _dot_exact_lhs(a_bf16, b):
    hi, mid, lo = _split3(b)
    d = lambda x: jnp.dot(a_bf16, x, preferred_element_type=F32)
    return d(hi) + d(mid) + d(lo)


def _rms(x, g):
    ms = jnp.mean(x * x, axis=-1, keepdims=True)
    return x * lax.rsqrt(ms + EPS) * g


def _sigmoid(x):
    return 1.0 / (1.0 + jnp.exp(-x))


def _silu(x):
    return x * _sigmoid(x)


def _norm_matmul_kernel(x_ref, g_ref, w_ref, *o_refs, norm, splits):
    x = x_ref[...]
    if norm:
        x = _rms(x, g_ref[...])
    res = jnp.dot(x.astype(BF16), w_ref[...].astype(BF16), preferred_element_type=F32)
    off = 0
    for o_ref, width in zip(o_refs, splits):
        o_ref[...] = res[:, off:off + width]
        off += width


def norm_matmul(x, g, w, *, norm=True, splits=None, tm=512):
    n, k = x.shape
    m = w.shape[1]
    tm = _row_tile(n, tm)
    if g is None:
        g = jnp.ones((k,), F32)
    widths = (m,) if splits is None else tuple(splits)
    assert sum(widths) == m
    outs = pl.pallas_call(
        functools.partial(_norm_matmul_kernel, norm=norm, splits=widths),
        grid=(n // tm,),
        in_specs=[pl.BlockSpec((tm, k), lambda i: (i, 0)),
                  pl.BlockSpec((1, k), lambda i: (0, 0)),
                  pl.BlockSpec((k, m), lambda i: (0, 0))],
        out_specs=[pl.BlockSpec((tm, wd), lambda i: (i, 0)) for wd in widths],
        out_shape=[jax.ShapeDtypeStruct((n, wd), F32) for wd in widths],
        compiler_params=_params("parallel"),
        name="norm_matmul",
    )(x, g.reshape(1, k), w)
    return outs[0] if splits is None else outs


def _rmsnorm_kernel(x_ref, g_ref, o_ref):
    o_ref[...] = _rms(x_ref[...], g_ref[...])


def rmsnorm_rows(x, g, *, tm=512):
    n, k = x.shape
    tm = _row_tile(n, tm)
    return pl.pallas_call(
        _rmsnorm_kernel,
        grid=(n // tm,),
        in_specs=[pl.BlockSpec((tm, k), lambda i: (i, 0)), pl.BlockSpec((1, k), lambda i: (0, 0))],
        out_specs=pl.BlockSpec((tm, k), lambda i: (i, 0)),
        out_shape=jax.ShapeDtypeStruct((n, k), F32),
        compiler_params=_params("parallel"),
        name="rmsnorm_rows",
    )(x, g.reshape(1, k))


def _matmul_res_kernel(*refs, n_in):
    res_ref = refs[2 * n_in]
    o_ref = refs[2 * n_in + 1]
    acc = res_ref[...]
    for j in range(n_in):
        acc = acc + jnp.dot(refs[2 * j][...].astype(BF16), refs[2 * j + 1][...].astype(BF16),
                            preferred_element_type=F32)
    o_ref[...] = acc


def matmul_res(xs, ws, res, *, tm=512):
    n, m = res.shape
    tm = _row_tile(n, tm)
    in_specs, args = [], []
    for x, w in zip(xs, ws):
        k = x.shape[1]
        in_specs += [pl.BlockSpec((tm, k), lambda i: (i, 0)), pl.BlockSpec((k, m), lambda i: (0, 0))]
        args += [x, w]
    in_specs.append(pl.BlockSpec((tm, m), lambda i: (i, 0)))
    return pl.pallas_call(
        functools.partial(_matmul_res_kernel, n_in=len(xs)),
        grid=(n // tm,),
        in_specs=in_specs,
        out_specs=pl.BlockSpec((tm, m), lambda i: (i, 0)),
        out_shape=jax.ShapeDtypeStruct((n, m), F32),
        compiler_params=_params("parallel"),
        name="matmul_res",
    )(*args, res)


_HALO_A = 8
_HALO_B = 32


def _even_conv_kernel(u_ref, sa_ref, sb_ref, wa_ref, wb_ref, bb_ref, lg_ref, lb_ref,
                      y_ref, na_ref, nb_ref, ea_ref, eb_ref, *, tt):
    t = pl.program_id(1)
    nt = pl.num_programs(1)

    @pl.when(t == 0)
    def _():
        ea_ref[...] = jnp.zeros_like(ea_ref)
        eb_ref[...] = jnp.zeros_like(eb_ref)
        ea_ref[_HALO_A - (CONV_A_W - 1):_HALO_A, :] = sa_ref[0]
        eb_ref[_HALO_B - (CONV_B_W - 1):_HALO_B, :] = sb_ref[0]

    @pl.when(t > 0)
    def _():
        ea_ref[0:_HALO_A, :] = ea_ref[tt:tt + _HALO_A, :]
        eb_ref[0:_HALO_B, :] = eb_ref[tt:tt + _HALO_B, :]

    xa = u_ref[0, :, 0:D_A]
    ba = u_ref[0, :, D_A:2 * D_A]
    ca = u_ref[0, :, 2 * D_A:3 * D_A]
    pb = u_ref[0, :, 3 * D_A:3 * D_A + D_B]
    gb = u_ref[0, :, 3 * D_A + D_B:3 * D_A + 2 * D_B]
    ea_ref[_HALO_A:_HALO_A + tt, :] = ca * xa
    eb_ref[_HALO_B:_HALO_B + tt, :] = pb * _sigmoid(gb)

    acc = jnp.zeros((tt, D_A), F32)
    for k in range(CONV_A_W):
        off = _HALO_A - (CONV_A_W - 1) + k
        acc = acc + ea_ref[off:off + tt, :] * wa_ref[k:k + 1, :]
    y_ref[0, :, 0:D_A] = ba * acc

    acc = jnp.zeros((tt, D_B), F32)
    for k in range(CONV_B_W):
        off = _HALO_B - (CONV_B_W - 1) + k
        acc = acc + eb_ref[off:off + tt, :] * wb_ref[k:k + 1, :]
    acc = acc + bb_ref[...]
    mu = jnp.mean(acc, axis=-1, keepdims=True)
    xc = acc - mu
    var = jnp.mean(xc * xc, axis=-1, keepdims=True)
    yb = xc * lax.rsqrt(var + EPS) * lg_ref[...] + lb_ref[...]
    y_ref[0, :, D_A:D_A + D_B] = _silu(yb)

    @pl.when(t == nt - 1)
    def _():
        na_ref[0] = ea_ref[_HALO_A + tt - (CONV_A_W - 1):_HALO_A + tt, :]
        nb_ref[0] = eb_ref[_HALO_B + tt - (CONV_B_W - 1):_HALO_B + tt, :]


def even_conv(u, sa, sb, wa, wb, bb, lg, lb, *, tt=256):
    b, t, w = u.shape
    tt = _row_tile(t, tt)
    full = lambda shape: pl.BlockSpec(shape, lambda i, j: (0,) * len(shape))
    return pl.pallas_call(
        functools.partial(_even_conv_kernel, tt=tt),
        grid=(b, t // tt),
        in_specs=[pl.BlockSpec((1, tt, w), lambda i, j: (i, j, 0)),
                  pl.BlockSpec((1, CONV_A_W - 1, D_A), lambda i, j: (i, 0, 0)),
                  pl.BlockSpec((1, CONV_B_W - 1, D_B), lambda i, j: (i, 0, 0)),
                  full((CONV_A_W, D_A)), full((CONV_B_W, D_B)), full((1, D_B)),
                  full((1, D_B)), full((1, D_B))],
        out_specs=[pl.BlockSpec((1, tt, D_A + D_B), lambda i, j: (i, j, 0)),
                   pl.BlockSpec((1, CONV_A_W - 1, D_A), lambda i, j: (i, 0, 0)),
                   pl.BlockSpec((1, CONV_B_W - 1, D_B), lambda i, j: (i, 0, 0))],
        out_shape=[jax.ShapeDtypeStruct((b, t, D_A + D_B), F32),
                   jax.ShapeDtypeStruct((b, CONV_A_W - 1, D_A), F32),
                   jax.ShapeDtypeStruct((b, CONV_B_W - 1, D_B), F32)],
        scratch_shapes=[pltpu.VMEM((_HALO_A + tt, D_A), F32), pltpu.VMEM((_HALO_B + tt, D_B), F32)],
        compiler_params=_params("parallel", "arbitrary"),
        name="even_conv",
    )(u, sa, sb, wa, wb, bb.reshape(1, D_B), lg.reshape(1, D_B), lb.reshape(1, D_B))


def _xattn_kernel(q_ref, k_ref, v_ref, o_ref, *, cache_layout):
    scale = XA_HD ** -0.5
    for h in range(XA_HEADS):
        sl = slice(h * XA_HD, (h + 1) * XA_HD)
        q = q_ref[0, :, sl].astype(BF16)
        if cache_layout:
            k = k_ref[0, 0, :, h, :].astype(BF16)
            v = v_ref[0, 0, :, h, :].astype(BF16)
        else:
            k = k_ref[0, :, sl].astype(BF16)
            v = v_ref[0, :, sl].astype(BF16)
        s = lax.dot_general(q, k, (((1,), (1,)), ((), ())), preferred_element_type=F32) * scale
        m = jnp.max(s, axis=-1, keepdims=True)
        p = jnp.exp(s - m)
        p = p / jnp.sum(p, axis=-1, keepdims=True)
        o_ref[0, :, sl] = jnp.dot(p.astype(BF16), v, preferred_element_type=F32)


def xattn(q, k, v, *, layer=None, tq=512):
    b, t, w = q.shape
    tq = _row_tile(t, tq)
    if layer is None:
        m = k.shape[1]
        kv_spec = pl.BlockSpec((1, m, w), lambda i, j: (i, 0, 0))
    else:
        m = k.shape[2]
        kv_spec = pl.BlockSpec((1, 1, m, XA_HEADS, XA_HD), lambda i, j: (layer, i, 0, 0, 0))
    return pl.pallas_call(
        functools.partial(_xattn_kernel, cache_layout=layer is not None),
        grid=(b, t // tq),
        in_specs=[pl.BlockSpec((1, tq, w), lambda i, j: (i, j, 0)), kv_spec, kv_spec],
        out_specs=pl.BlockSpec((1, tq, w), lambda i, j: (i, j, 0)),
        out_shape=jax.ShapeDtypeStruct((b, t, w), F32),
        compiler_params=_params("parallel", "parallel"),
        name="xattn",
    )(q, k, v)


MOE_BLK = 256
_ROUTER_W = 128


def _router_kernel(x_ref, g_ref, w_ref, lg_ref, xn_ref):
    x = _rms(x_ref[...], g_ref[...])
    xn_ref[...] = x
    w = w_ref[...]
    xh = x.astype(BF16)
    xl = (x - xh.astype(F32)).astype(BF16)
    wh = w.astype(BF16)
    wl = (w - wh.astype(F32)).astype(BF16)
    d = lambda a, b: jnp.dot(a, b, preferred_element_type=F32)
    lg_ref[...] = d(xh, wh) + (d(xh, wl) + d(xl, wh))


def moe_router(x, g, w_router, *, tm=512):
    n, k = x.shape
    tm = _row_tile(n, tm)
    return pl.pallas_call(
        _router_kernel,
        grid=(n // tm,),
        in_specs=[pl.BlockSpec((tm, k), lambda i: (i, 0)),
                  pl.BlockSpec((1, k), lambda i: (0, 0)),
                  pl.BlockSpec((k, _ROUTER_W), lambda i: (0, 0))],
        out_specs=[pl.BlockSpec((tm, _ROUTER_W), lambda i: (i, 0)),
                   pl.BlockSpec((tm, k), lambda i: (i, 0))],
        out_shape=[jax.ShapeDtypeStruct((n, _ROUTER_W), F32), jax.ShapeDtypeStruct((n, k), F32)],
        compiler_params=_params("parallel"),
        name="moe_router",
    )(x, g.reshape(1, k), w_router)


def _expert_kernel(be_ref, act_ref, x_ref, gate_ref, w1_ref, w3_ref, w2_ref, o_ref):
    i = pl.program_id(0)

    @pl.when(act_ref[i] > 0)
    def _():
        x = x_ref[...].astype(BF16)
        h1 = jnp.dot(x, w1_ref[0, 0].astype(BF16), preferred_element_type=F32)
        h3 = jnp.dot(x, w3_ref[0, 0].astype(BF16), preferred_element_type=F32)
        hid = (_silu(h1) * h3).astype(BF16)
        out = jnp.dot(hid, w2_ref[0, 0].astype(BF16), preferred_element_type=F32)
        o_ref[...] = out * gate_ref[...]

    @pl.when(act_ref[i] == 0)
    def _():
        o_ref[...] = jnp.zeros_like(o_ref)


def moe_experts(xg, gate, blk_exp, blk_act, w1, w3, w2, layer):
    rows, d = xg.shape
    nb = rows // MOE_BLK
    ff = w1.shape[3]
    return pl.pallas_call(
        _expert_kernel,
        grid_spec=pltpu.PrefetchScalarGridSpec(
            num_scalar_prefetch=2,
            grid=(nb,),
            in_specs=[pl.BlockSpec((MOE_BLK, d), lambda i, be, act: (i, 0)),
                      pl.BlockSpec((MOE_BLK, 1), lambda i, be, act: (i, 0)),
                      pl.BlockSpec((1, 1, d, ff), lambda i, be, act: (layer, be[i], 0, 0)),
                      pl.BlockSpec((1, 1, d, ff), lambda i, be, act: (layer, be[i], 0, 0)),
                      pl.BlockSpec((1, 1, ff, d), lambda i, be, act: (layer, be[i], 0, 0))],
            out_specs=pl.BlockSpec((MOE_BLK, d), lambda i, be, act: (i, 0)),
        ),
        out_shape=jax.ShapeDtypeStruct((rows, d), F32),
        compiler_params=_params("arbitrary"),
        name="moe_experts",
    )(blk_exp, blk_act, xg, gate, w1, w3, w2)


def moe_layer(h, g, wg, bg, we, be, w1, w3, w2, layer):
    n, d = h.shape
    w_router = jnp.concatenate([wg, we, jnp.zeros((d, _ROUTER_W - MOE_GROUPS - MOE_E), F32)], axis=1)
    logits, xn = moe_router(h, g, w_router)
    lg = logits[:, :MOE_GROUPS] + bg
    grp = jnp.argmax(lg, axis=-1)
    gw = jnp.take_along_axis(jax.nn.softmax(lg, axis=-1), grp[:, None], axis=1)
    le = (logits[:, MOE_GROUPS:MOE_GROUPS + MOE_E] + be).reshape(n, MOE_GROUPS, MOE_EPG)
    le = jnp.take_along_axis(le, grp[:, None, None], axis=1)[:, 0]
    tv, ti = lax.top_k(jax.nn.softmax(le, axis=-1), MOE_TOPK)
    wts = gw * tv / jnp.sum(tv, axis=-1, keepdims=True)
    eid = (grp[:, None] * MOE_EPG + ti).reshape(-1).astype(jnp.int32)
    npair = n * MOE_TOPK
    experts = jnp.arange(MOE_E, dtype=jnp.int32)
    order = jnp.argsort(eid).astype(jnp.int32)
    rank = jnp.argsort(order).astype(jnp.int32)
    counts = jnp.sum(eid[:, None] == experts[None, :], axis=0).astype(jnp.int32)
    start = jnp.cumsum(counts) - counts
    padded = (counts + MOE_BLK - 1) // MOE_BLK * MOE_BLK
    pend = jnp.cumsum(padded)
    shift = pend - padded - start
    nb = -(-npair // MOE_BLK) + MOE_E
    blk_lo = jnp.arange(nb, dtype=jnp.int32) * MOE_BLK
    blk_exp = jnp.minimum(jnp.sum(pend[None, :] <= blk_lo[:, None], axis=1), MOE_E - 1).astype(jnp.int32)
    blk_act = (blk_lo < pend[-1]).astype(jnp.int32)
    src = (blk_lo - shift[blk_exp])[:, None] + jnp.arange(MOE_BLK, dtype=jnp.int32)[None, :]
    live = src < (start + counts)[blk_exp][:, None]
    pair = order[jnp.where(live, src, 0).reshape(-1)]
    buf_tok = pair // MOE_TOPK
    buf_gate = jnp.where(live.reshape(-1), wts.reshape(-1)[pair], 0.0)
    dest = (rank + shift[eid]).reshape(n, MOE_TOPK)
    out = moe_experts(xn[buf_tok], buf_gate[:, None], blk_exp, blk_act, w1, w3, w2, layer)
    return h + (out[dest[:, 0]] + out[dest[:, 1]])


_KT = 128
_NT = (((1,), (1,)), ((), ()))
_BIG = 3e38
_M0 = -1e29


def _compress_kernel(x_ref, pe_ref, w_ref, o_ref):
    acc = jnp.zeros(o_ref.shape, F32)
    for l in range(L_CMP):
        y = x_ref[:, l, :] + pe_ref[l:l + 1, :]
        acc = acc + jnp.dot(y.astype(BF16), w_ref[l], preferred_element_type=F32)
    o_ref[...] = acc


def nsa_compress(x, pe, w, *, tb=256):
    nb = x.shape[0]
    tb = _row_tile(nb, tb)
    pe2 = jnp.concatenate([pe] * NSA_KV, axis=1)
    z = jnp.zeros_like(w)
    w2 = jnp.concatenate([jnp.concatenate([w, z], axis=2), jnp.concatenate([z, w], axis=2)], axis=1).astype(BF16)
    return pl.pallas_call(
        _compress_kernel,
        grid=(nb // tb,),
        in_specs=[pl.BlockSpec((tb, L_CMP, NSA_KW), lambda i: (i, 0, 0)),
                  pl.BlockSpec((L_CMP, NSA_KW), lambda i: (0, 0)),
                  pl.BlockSpec((L_CMP, NSA_KW, NSA_KW), lambda i: (0, 0, 0))],
        out_specs=pl.BlockSpec((tb, NSA_KW), lambda i: (i, 0)),
        out_shape=jax.ShapeDtypeStruct((nb, NSA_KW), F32),
        compiler_params=_params("parallel"),
        name="nsa_compress",
    )(x, pe2, w2)


def _compress_cache_kernel(x_ref, pe_ref, w_ref, o_ref):
    for g in range(NSA_KV):
        acc = jnp.zeros((o_ref.shape[0], NSA_HD), F32)
        for l in range(L_CMP):
            y = x_ref[0, :, l, g, :] + pe_ref[l:l + 1, :]
            acc = acc + jnp.dot(y.astype(BF16), w_ref[l], preferred_element_type=F32)
        o_ref[:, g * NSA_HD:(g + 1) * NSA_HD] = acc


def nsa_compress_cache(cache, layer, pe, w, *, tb=256):
    nl, n_pool, page = cache.shape[:3]
    nb = n_pool * page // L_CMP
    tb = _row_tile(nb, tb)
    x = cache.reshape(nl, nb, L_CMP, NSA_KV, NSA_HD)
    return pl.pallas_call(
        _compress_cache_kernel,
        grid=(nb // tb,),
        in_specs=[pl.BlockSpec((1, tb, L_CMP, NSA_KV, NSA_HD), lambda i: (layer, i, 0, 0, 0)),
                  pl.BlockSpec((L_CMP, NSA_HD), lambda i: (0, 0)),
                  pl.BlockSpec((L_CMP, NSA_HD, NSA_HD), lambda i: (0, 0, 0))],
        out_specs=pl.BlockSpec((tb, NSA_KW), lambda i: (i, 0)),
        out_shape=jax.ShapeDtypeStruct((nb, NSA_KW), F32),
        compiler_params=_params("parallel"),
        name="nsa_compress_cache",
    )(x, pe, w.astype(BF16))


def _qz(q, g, nq):
    lane = lax.broadcasted_iota(jnp.int32, (nq, NSA_KW), 1)
    keep = (lane >> 6) == g
    parts = []
    for r in range(NSA_REP):
        h = g * NSA_REP + r
        slab = q[:, (h // 2) * NSA_KW:(h // 2 + 1) * NSA_KW]
        if h % 2 != g:
            slab = pltpu.roll(slab, NSA_HD, axis=1)
        parts.append(jnp.where(keep, slab, 0.0))
    return jnp.concatenate(parts, axis=0).astype(BF16)


def _softmax_init(m_ref, l_ref, acc_ref):
    m_ref[...] = jnp.full(m_ref.shape, NEG, F32)
    l_ref[...] = jnp.zeros(l_ref.shape, F32)
    acc_ref[...] = jnp.zeros(acc_ref.shape, F32)


def _softmax_tile(st, dpos, maskf, slope, vt, m_ref, l_ref, acc_ref, g):
    s = st - slope * dpos
    sm = jnp.where(maskf > 0.0, s, NEG)
    m_old = m_ref[g]
    m_new = jnp.maximum(m_old, jnp.max(sm, axis=0, keepdims=True))
    alpha = jnp.exp(m_old - m_new)
    p = jnp.exp(sm - m_new) * maskf
    l_ref[g] = alpha * l_ref[g] + jnp.sum(p, axis=0, keepdims=True)
    acc_ref[g] = alpha * acc_ref[g] + jnp.dot(vt, p.astype(BF16), preferred_element_type=F32)
    m_ref[g] = m_new


def _softmax_done(l_ref, acc_ref, g):
    l = l_ref[g]
    return acc_ref[g] * jnp.where(l > 0.0, 1.0 / l, 0.0)


def _compressed_branch(qz, kcb, vcbt_g, qpos, slope, ncb):
    st = lax.dot_general(kcb, qz, _NT, preferred_element_type=F32)
    row = lax.broadcasted_iota(jnp.int32, (ncb, 1), 0)
    half = ncb // 2
    blk = jnp.where(row < half, 2 * row, 2 * (row - half) + 1)
    c_pos = blk * L_CMP + (L_CMP - 1)
    d_c = qpos - c_pos
    maskf = jnp.where(d_c >= 0, 1.0, 0.0)
    s = st - slope * d_c.astype(F32)
    sm = jnp.where(d_c >= 0, s, NEG)
    m = jnp.max(sm, axis=0, keepdims=True)
    p = jnp.exp(sm - m) * maskf
    l = jnp.sum(p, axis=0, keepdims=True)
    p = p * jnp.where(l > 0.0, 1.0 / l, 0.0)
    o = jnp.dot(vcbt_g, p.astype(BF16), preferred_element_type=F32)
    return o, p


def _select_blocks(imp_sel, qpos, nsp):
    cols = imp_sel.shape[1]
    blk = lax.broadcasted_iota(jnp.int32, (nsp, cols), 0)
    cur = qpos >> 6
    valid = blk <= cur
    forced = jnp.where(valid, jnp.where(blk == 0, 1.0, jnp.where(blk >= cur - 1, 1.0, 0.0)), 0.0)
    score = jnp.where(forced > 0.0, _BIG, jnp.where(valid, imp_sel, -1.0))
    sel = jnp.zeros((nsp, cols), F32)
    for _ in range(N_SEL):
        m = jnp.max(score, axis=0, keepdims=True)
        idx = jnp.min(jnp.where(score == m, blk, nsp + 1), axis=0, keepdims=True)
        pick = blk == idx
        sel = jnp.where(pick, 1.0, sel)
        score = jnp.where(pick, -2.0, score)
    return jnp.where(valid, sel, 0.0)


def _nsa_prompt_kernel(q_ref, glog_ref, kcb_ref, vcbt_ref, ks_ref, vst_ref, kw_ref, vwt_ref,
                       qoff_ref, slope_ref, qaux_ref, o_ref, sel_ref, m_ref, l_ref, acc_ref, *, nq, ncb):
    i = pl.program_id(1)
    st0 = i * nq
    qoff = qoff_ref[...]
    qpos = st0 + qoff
    q = q_ref[0] * (NSA_HD ** -0.5)
    key_io = lax.broadcasted_iota(jnp.int32, (_KT, 1), 0)
    lane = lax.broadcasted_iota(jnp.int32, (_KT, NSA_KW), 1)
    key_lane = jnp.where(lane == 1, lax.broadcasted_iota(jnp.int32, (_KT, NSA_KW), 0), 0).astype(F32)
    causal = key_io <= qoff
    anti = key_io >= qoff
    nsp = ncb // 2
    wt = WINDOW // _KT

    def m_init():
        m_ref[...] = jnp.full(m_ref.shape, _M0, F32)
        l_ref[...] = jnp.zeros(l_ref.shape, F32)
        acc_ref[...] = jnp.zeros(acc_ref.shape, F32)

    for g in range(NSA_KV):
        qz = _qz(q, g, nq)
        qx = jnp.concatenate([qz, qaux_ref[g]], axis=1)
        slope = slope_ref[g]
        rows = slice(g * NSA_HD, (g + 1) * NSA_HD)
        o_c, p_c = _compressed_branch(qz, kcb_ref[0], vcbt_ref[0, rows, :], qpos, slope, ncb)
        imp = p_c[:, 0:nq]
        for r in range(1, NSA_REP):
            imp = imp + p_c[:, r * nq:(r + 1) * nq]
        imp_sel = imp[0:nsp] + imp[nsp:ncb]
        sel = _select_blocks(imp_sel, qpos[:, 0:nq], nsp)
        sel_ref[g] = jnp.concatenate([jnp.where(sel > 0.0, 0.0, NEG)] * NSA_REP, axis=1)
        blk = lax.broadcasted_iota(jnp.int32, (nsp, 1), 0)
        row_any = jnp.max(sel, axis=1, keepdims=True)
        first = jnp.min(jnp.where(row_any > 0.0, jnp.where(blk >= 2, blk, 2 * nsp), 2 * nsp), axis=0, keepdims=True)
        kt_lo = jnp.minimum(first[0, 0] >> 1, i)

        def tile(k_ref, vt_ref, kt, mode, with_sel):
            off = kt * _KT if isinstance(kt, int) else pl.multiple_of(kt * _KT, _KT)
            kaux =jnp.where(lane == 0, (kt - i).astype(F32), key_lane).astype(BF16)
            kx = jnp.concatenate([k_ref[0, pl.ds(off, _KT), :], kaux], axis=1)
            s = lax.dot_general(kx, qx, _NT, preferred_element_type=F32)
            if with_sel:
                half = _KT // 2
                s = jnp.concatenate([s[0:half] + sel_ref[g, pl.ds(2 * kt, 1), :],
                                     s[half:_KT] + sel_ref[g, pl.ds(2 * kt + 1, 1), :]], axis=0)
            if mode == "causal":
                s = jnp.where(causal, s, NEG)
            elif mode == "anti":
                s = jnp.where(anti, s, NEG)
            m_old = m_ref[g]
            m_new = jnp.maximum(m_old, jnp.max(s, axis=0, keepdims=True))
            alpha = jnp.exp(m_old - m_new)
            p = jnp.exp(s - m_new)
            l_ref[g] = alpha * l_ref[g] + jnp.sum(p, axis=0, keepdims=True)
            vt = vt_ref[0, rows, pl.ds(off, _KT)]
            acc_ref[g] = alpha * acc_ref[g] + jnp.dot(vt, p.astype(BF16), preferred_element_type=F32)
            m_ref[g] = m_new

        m_init()

        @pl.when(i > 0)
        def _():
            tile(ks_ref, vst_ref, 0, "none", True)

        def sel_body(kt, carry):
            tile(ks_ref, vst_ref, kt, "none", True)
            return carry

        lax.fori_loop(jnp.maximum(kt_lo, 1), i, sel_body, 0)
        tile(ks_ref, vst_ref, i, "causal", True)
        o_s = _softmax_done(l_ref, acc_ref, g)

        m_init()

        @pl.when(i >= wt)
        def _():
            tile(kw_ref, vwt_ref, i - wt, "anti", False)

        def win_body(kt, carry):
            tile(kw_ref, vwt_ref, kt, "none", False)
            return carry

        lax.fori_loop(jnp.maximum(i - wt + 1, 0), i, win_body, 0)
        tile(kw_ref, vwt_ref, i, "causal", False)
        o_w = _softmax_done(l_ref, acc_ref, g)

        gc = _sigmoid(glog_ref[0, 0, g:g + 1, :])
        gs = _sigmoid(glog_ref[0, 0, 2 + g:3 + g, :])
        gw = _sigmoid(glog_ref[0, 0, 4 + g:5 + g, :])
        o_ref[0, 0, g] = gc * o_c + gs * o_s + gw * o_w


_TS = 256


def _pos_lanes(nkeys, tile_off):
    lane = lax.broadcasted_iota(jnp.int32, (nkeys, NSA_KW), 1)
    key = lax.broadcasted_iota(jnp.int32, (nkeys, NSA_KW), 0)
    hi = (tile_off + (key >> 7)).astype(F32)
    lo = (key & (_KT - 1)).astype(F32)
    return jnp.where(lane == 0, hi, jnp.where(lane == 1, lo, 0.0)).astype(BF16)


def _nsa_prompt_kernel2(q_ref, glog_ref, kcb_ref, vcbt_ref, ks_ref, vst_ref, kw_ref, vwt_ref,
                        qoff_ref, slope_ref, qaux_ref, o_ref, sel_ref, m_ref, l_ref, acc_ref, *, nq, ncb):
    i = pl.program_id(1)
    st0 = i * nq
    qoff = qoff_ref[...]
    qpos = st0 + qoff
    q = q_ref[0] * (NSA_HD ** -0.5)
    nsp = ncb // 2
    bpt = _TS // L_SEL
    idiag = st0 // _TS
    groups = range(NSA_KV)
    rows = [slice(g * NSA_HD, (g + 1) * NSA_HD) for g in groups]

    qxs, o_cs, firsts = [], [], []
    for g in groups:
        qz = _qz(q, g, nq)
        qxs.append(jnp.concatenate([qz, qaux_ref[g]], axis=1))
        o_c, p_c = _compressed_branch(qz, kcb_ref[0], vcbt_ref[0, rows[g], :], qpos, slope_ref[g], ncb)
        o_cs.append(o_c)
        imp = p_c[:, 0:nq]
        for r in range(1, NSA_REP):
            imp = imp + p_c[:, r * nq:(r + 1) * nq]
        sel = _select_blocks(imp[0:nsp] + imp[nsp:ncb], qpos[:, 0:nq], nsp)
        sel_ref[g] = jnp.concatenate([jnp.where(sel > 0.0, 0.0, NEG)] * NSA_REP, axis=1)
        blk = lax.broadcasted_iota(jnp.int32, (nsp, 1), 0)
        row_any = jnp.max(sel, axis=1, keepdims=True)
        first = jnp.min(jnp.where(row_any > 0.0, jnp.where(blk >= bpt, blk, nsp * bpt), nsp * bpt),
                        axis=0, keepdims=True)
        firsts.append(first[0, 0])
    lo = jnp.minimum(jnp.minimum(firsts[0], firsts[1]) // bpt, idiag)

    m_ref[...] = jnp.full(m_ref.shape, _M0, F32)
    l_ref[...] = jnp.zeros(l_ref.shape, F32)
    acc_ref[...] = jnp.zeros(acc_ref.shape, F32)

    def sel_tile(kt, diag):
        off = kt * _TS if isinstance(kt, int) else pl.multiple_of(kt * _TS, _TS)
        kx = jnp.concatenate([ks_ref[0, pl.ds(off, _TS), :], _pos_lanes(_TS, kt * (_TS // _KT) - i)], axis=1)
        if diag:
            visible = (off + lax.broadcasted_iota(jnp.int32, (_TS, 1), 0)) <= qpos
        for g in groups:
            s = lax.dot_general(kx, qxs[g], _NT, preferred_element_type=F32)
            s = jnp.concatenate([s[b * L_SEL:(b + 1) * L_SEL] + sel_ref[g, pl.ds(bpt * kt + b, 1), :]
                                 for b in range(bpt)], axis=0)
            if diag:
                s = jnp.where(visible, s, NEG)
            m_old = m_ref[g]
            m_new = jnp.maximum(m_old, jnp.max(s, axis=0, keepdims=True))
            alpha = jnp.exp(m_old - m_new)
            p = jnp.exp(s - m_new)
            l_ref[g] = alpha * l_ref[g] + jnp.sum(p, axis=0, keepdims=True)
            vt = vst_ref[0, rows[g], pl.ds(off, _TS)]
            acc_ref[g] = alpha * acc_ref[g] + jnp.dot(vt, p.astype(BF16), preferred_element_type=F32)
            m_ref[g] = m_new

    @pl.when(idiag > 0)
    def _():
        sel_tile(0, False)

    def sel_body(kt, carry):
        sel_tile(kt, False)
        return carry

    lax.fori_loop(lo, idiag, sel_body, 0)
    sel_tile(idiag, True)

    wk = WINDOW + nq
    wstart = pl.multiple_of(jnp.maximum(i - WINDOW // nq, 0) * nq, nq)
    kxw = jnp.concatenate([kw_ref[0, pl.ds(wstart, wk), :], _pos_lanes(wk, wstart // _KT - i)], axis=1)
    d_w = qpos - (wstart + lax.broadcasted_iota(jnp.int32, (wk, 1), 0))
    in_window = jnp.where(d_w >= 0, d_w, WINDOW + 1) <= WINDOW
    for g in groups:
        s = jnp.where(in_window, lax.dot_general(kxw, qxs[g], _NT, preferred_element_type=F32), NEG)
        m = jnp.max(s, axis=0, keepdims=True)
        p = jnp.exp(s - m)
        l = jnp.sum(p, axis=0, keepdims=True)
        o_w = jnp.dot(vwt_ref[0, rows[g], pl.ds(wstart, wk)], p.astype(BF16), preferred_element_type=F32) / l
        o_s = _softmax_done(l_ref, acc_ref, g)
        gc = _sigmoid(glog_ref[0, 0, g:g + 1, :])
        gs = _sigmoid(glog_ref[0, 0, 2 + g:3 + g, :])
        gw = _sigmoid(glog_ref[0, 0, 4 + g:5 + g, :])
        o_ref[0, 0, g] = gc * o_cs[g] + gs * o_s + gw * o_w


def _nsa_cols(nq):
    c = NSA_REP * nq
    qoff = (jnp.arange(c, dtype=jnp.int32) % nq).reshape(1, c)
    slopes = 2.0 ** (-8.0 * jnp.arange(1, NSA_HEADS + 1, dtype=F32) / NSA_HEADS)
    slope = jnp.repeat(slopes.reshape(NSA_KV, NSA_REP), nq, axis=1).reshape(NSA_KV, 1, c)
    return qoff, slope


def _gate_cols(glog, nq):
    b, t, _ = glog.shape
    x = glog.reshape(b, t // nq, nq, NSA_KV, NSA_REP, 3)
    return x.transpose(0, 1, 5, 3, 4, 2).reshape(b, t // nq, 3 * NSA_KV, NSA_REP * nq)


def _even_odd(x):
    return jnp.concatenate([x[:, 0::2], x[:, 1::2]], axis=1)


def _uncols(o, nq):
    b, nb = o.shape[:2]
    x = o.reshape(b, nb, NSA_KV, NSA_HD, NSA_REP, nq)
    return x.transpose(0, 1, 5, 2, 4, 3).reshape(b, nb * nq, NSA_W)


def nsa_prompt(q, glog, kcb, vcb, ks, vs, kw, vw, *, nq=128):
    b, t, _ = q.shape
    ncb = kcb.shape[1]
    c = NSA_REP * nq
    nblk = t // nq
    assert nq == _KT and t % _TS == 0 and t >= WINDOW + nq and ncb == t // L_CMP
    qoff, slope = _nsa_cols(nq)
    lane = jnp.arange(NSA_KW)[None, None, :]
    slope_col = slope.reshape(NSA_KV, c, 1)
    qaux = jnp.where(lane == 0, slope_col * _KT, jnp.where(lane == 1, slope_col, 0.0)).astype(BF16)
    kcb_p = _even_odd(kcb).astype(BF16)
    vcbt = _even_odd(vcb).transpose(0, 2, 1).astype(BF16)
    per_b = lambda shape: pl.BlockSpec((1,) + shape, lambda i, j: (i, 0, 0))
    out = pl.pallas_call(
        functools.partial(_nsa_prompt_kernel2, nq=nq, ncb=ncb),
        grid=(b, nblk),
        in_specs=[pl.BlockSpec((1, nq, NSA_W), lambda i, j: (i, j, 0)),
                  pl.BlockSpec((1, 1, 3 * NSA_KV, c), lambda i, j: (i, j, 0, 0)),
                  per_b((ncb, NSA_KW)), per_b((NSA_KW, ncb)),
                  per_b((t, NSA_KW)), per_b((NSA_KW, t)), per_b((t, NSA_KW)), per_b((NSA_KW, t)),
                  pl.BlockSpec((1, c), lambda i, j: (0, 0)),
                  pl.BlockSpec((NSA_KV, 1, c), lambda i, j: (0, 0, 0)),
                  pl.BlockSpec((NSA_KV, c, NSA_KW), lambda i, j: (0, 0, 0))],
        out_specs=pl.BlockSpec((1, 1, NSA_KV, NSA_HD, c), lambda i, j: (i, j, 0, 0, 0)),
        out_shape=jax.ShapeDtypeStruct((b, nblk, NSA_KV, NSA_HD, c), F32),
        scratch_shapes=[pltpu.VMEM((NSA_KV, ncb // 2, c), F32),
                        pltpu.VMEM((NSA_KV, 1, c), F32), pltpu.VMEM((NSA_KV, 1, c), F32),
                        pltpu.VMEM((NSA_KV, NSA_HD, c), F32)],
        compiler_params=_params("parallel", "arbitrary"),
        name="nsa_prompt",
    )(q, _gate_cols(glog, nq), kcb_p, vcbt, ks.astype(BF16), vs.transpose(0, 2, 1).astype(BF16),
      kw.astype(BF16), vw.transpose(0, 2, 1).astype(BF16), qoff, slope, qaux)
    return _uncols(out, nq)


def _nsa_sample_kernel(pt_ref, q_ref, glog_ref, kcb_ref, vcbt_ref, *refs, nq, ncb, npages, page, past, wb):
    ks_pages = refs[0:npages]
    vs_pages = refs[npages:2 * npages]
    (nks_ref, nvs_ref, nkw_ref, nvw_ref, wk_ref, wv_ref, qoff_ref, slope_ref, rmat_ref,
     o_ref, wko_ref, wvo_ref, m_ref, l_ref, acc_ref) = refs[2 * npages:]
    del pt_ref
    c = NSA_REP * nq
    qpos = past + qoff_ref[...]
    q = q_ref[0] * (NSA_HD ** -0.5)
    key_io = lax.broadcasted_iota(jnp.int32, (_KT, 1), 0)
    nsp = sel_rows = -(-(past + nq) // L_SEL)
    nsp = -(-nsp // 8) * 8
    pad_rows = lambda x: jnp.concatenate([x, jnp.zeros((_KT - nq, x.shape[1]), F32)], axis=0)

    qzs, slopes, sels, o_cs = [], [], [], []
    for g in range(NSA_KV):
        qz = _qz(q, g, nq)
        slope = slope_ref[g]
        o_c, p_c = _compressed_branch(qz, kcb_ref[0], vcbt_ref[0, g * NSA_HD:(g + 1) * NSA_HD, :], qpos, slope, ncb)
        imp = _dot_exact_rhs(p_c, rmat_ref[...])
        imp_sel = imp[0:ncb // 2] + imp[ncb // 2:ncb]
        imp_sel = jnp.concatenate([imp_sel, jnp.zeros((nsp - ncb // 2, c), F32)], axis=0)
        qzs.append(qz)
        slopes.append(slope)
        sels.append(_select_blocks(imp_sel, qpos, nsp))
        o_cs.append(o_c)
    del sel_rows

    qgs = [qzs[g][:, g * NSA_HD:(g + 1) * NSA_HD] for g in range(NSA_KV)]

    def run_tile(kv_fn, tok0, mask_fn):
        dpos = qpos - (tok0 + key_io)
        for g in range(NSA_KV):
            k, v = kv_fn(g)
            st = lax.dot_general(k.astype(BF16), qgs[g], _NT, preferred_element_type=F32)
            maskf = mask_fn(g, dpos)
            s = st - slopes[g] * dpos.astype(F32)
            sm = jnp.where(maskf > 0.0, s, NEG)
            m_old = m_ref[g]
            m_new = jnp.maximum(m_old, jnp.max(sm, axis=0, keepdims=True))
            alpha = jnp.exp(m_old - m_new)
            p = jnp.exp(sm - m_new) * maskf
            l_ref[g] = alpha * l_ref[g] + jnp.sum(p, axis=0, keepdims=True)
            pv = lax.dot_general(v.astype(BF16), p.astype(BF16), _TN, preferred_element_type=F32)
            acc_ref[g] = alpha * acc_ref[g] + pv
            m_ref[g] = m_new

    cache_tile = lambda k_ref, v_ref, lo: (lambda g: (k_ref[0, 0, lo:lo + _KT, g, :], v_ref[0, 0, lo:lo + _KT, g, :]))
    new_tile = lambda k_ref, v_ref: (lambda g: (pad_rows(k_ref[0, :, g * NSA_HD:(g + 1) * NSA_HD]),
                                                pad_rows(v_ref[0, :, g * NSA_HD:(g + 1) * NSA_HD])))

    _softmax_init(m_ref, l_ref, acc_ref)
    for j in range(npages + 1):
        blk0 = j * (page // L_SEL)

        def sel_mask(g, dpos, blk0=blk0):
            selrow = jnp.where(key_io < L_SEL, sels[g][blk0:blk0 + 1, :], sels[g][blk0 + 1:blk0 + 2, :])
            return jnp.where(dpos >= 0, selrow, 0.0)

        if j < npages:
            run_tile(cache_tile(ks_pages[j], vs_pages[j], 0), j * page, sel_mask)
        else:
            run_tile(new_tile(nks_ref, nvs_ref), past, sel_mask)
    o_ss = [_softmax_done(l_ref, acc_ref, g) for g in range(NSA_KV)]

    _softmax_init(m_ref, l_ref, acc_ref)
    win_mask = lambda g, dpos: jnp.where(dpos >= 0, jnp.where(dpos <= WINDOW, 1.0, 0.0), 0.0)
    for j in range(wb // _KT):
        run_tile(cache_tile(wk_ref, wv_ref, j * _KT), past - wb + j * _KT, win_mask)
    run_tile(new_tile(nkw_ref, nvw_ref), past, win_mask)
    for g in range(NSA_KV):
        o_w = _softmax_done(l_ref, acc_ref, g)
        gc = _sigmoid(glog_ref[0, 0, g:g + 1, :])
        gs = _sigmoid(glog_ref[0, 0, 2 + g:3 + g, :])
        gw = _sigmoid(glog_ref[0, 0, 4 + g:5 + g, :])
        o_ref[0, 0, g] = gc * o_cs[g] + gs * o_ss[g] + gw * o_w

    wko_ref[0, 0:wb - nq] = wk_ref[0, 0, nq:wb]
    wvo_ref[0, 0:wb - nq] = wv_ref[0, 0, nq:wb]
    for g in range(NSA_KV):
        wko_ref[0, wb - nq:wb, g, :] = nkw_ref[0, :, g * NSA_HD:(g + 1) * NSA_HD]
        wvo_ref[0, wb - nq:wb, g, :] = nvw_ref[0, :, g * NSA_HD:(g + 1) * NSA_HD]


def nsa_sample(q, glog, kcb, vcb, pool_k, pool_v, page_table, nks, nvs, nkw, nvw, win_k, win_v, layer):
    b, nq, _ = q.shape
    ncb = kcb.shape[1]
    npages = page_table.shape[1]
    page = pool_k.shape[2]
    past = npages * page
    wb = win_k.shape[2]
    assert page == _KT and wb % _KT == 0 and nq % 8 == 0 and nq <= L_SEL and ncb % 2 == 0
    c = NSA_REP * nq
    qoff, slope = _nsa_cols(nq)
    col = jnp.arange(c, dtype=jnp.int32)
    rmat = (col[:, None] % nq == col[None, :] % nq).astype(BF16)
    kcb_p = _even_odd(kcb).astype(BF16)
    vcbt = _even_odd(vcb).transpose(0, 2, 1).astype(BF16)
    per_b = lambda shape: pl.BlockSpec((1,) + shape, lambda i, pt: (i,) + (0,) * len(shape))
    const = lambda shape: pl.BlockSpec(shape, lambda i, pt: (0,) * len(shape))
    page_spec = lambda j: pl.BlockSpec((1, 1, page, NSA_KV, NSA_HD), lambda i, pt: (layer, pt[i, j], 0, 0, 0))
    win_spec = pl.BlockSpec((1, 1, wb, NSA_KV, NSA_HD), lambda i, pt: (layer, i, 0, 0, 0))
    in_specs = ([per_b((nq, NSA_W)), per_b((1, 3 * NSA_KV, c)), per_b((ncb, NSA_KW)), per_b((NSA_KW, ncb))]
                + [page_spec(j) for j in range(npages)] * 2
                + [per_b((nq, NSA_KW))] * 4 + [win_spec] * 2
                + [const((1, c)), const((NSA_KV, 1, c)), const((c, c))])
    out, wko, wvo = pl.pallas_call(
        functools.partial(_nsa_sample_kernel, nq=nq, ncb=ncb, npages=npages, page=page, past=past, wb=wb),
        grid_spec=pltpu.PrefetchScalarGridSpec(
            num_scalar_prefetch=1,
            grid=(b,),
            in_specs=in_specs,
            out_specs=[per_b((1, NSA_KV, NSA_HD, c)), per_b((wb, NSA_KV, NSA_HD)), per_b((wb, NSA_KV, NSA_HD))],
            scratch_shapes=[pltpu.VMEM((NSA_KV, 1, c), F32), pltpu.VMEM((NSA_KV, 1, c), F32),
                            pltpu.VMEM((NSA_KV, NSA_HD, c), F32)],
        ),
        out_shape=[jax.ShapeDtypeStruct((b, 1, NSA_KV, NSA_HD, c), F32),
                   jax.ShapeDtypeStruct((b, wb, NSA_KV, NSA_HD), F32),
                   jax.ShapeDtypeStruct((b, wb, NSA_KV, NSA_HD), F32)],
        compiler_params=_params("arbitrary"),
        name="nsa_sample",
    )(page_table, q, _gate_cols(glog, nq), kcb_p, vcbt, *([pool_k] * npages), *([pool_v] * npages),
      nks, nvs, nkw, nvw, win_k, win_v, qoff, slope, rmat)
    return _uncols(out, nq), wko, wvo


def _compress_pages_kernel(x_ref, pe_ref, w_ref, o_ref):
    for g in range(NSA_KV):
        acc = jnp.zeros(o_ref.shape[1:], F32)
        for d in range(NSA_HD):
            y = x_ref[0, :, g, d, :] + pe_ref[d:d + 1, :]
            acc = acc + jnp.dot(y.astype(BF16), w_ref[d], preferred_element_type=F32)
        o_ref[g] = acc


def nsa_compress_pages(cache_t, layer, pe, w, *, tp=256):
    n_pool, page = cache_t.shape[1], cache_t.shape[4]
    nblk = page // L_CMP
    tp = _row_tile(n_pool, tp)
    pe_t = jnp.tile(pe.T, (1, nblk))
    eye = jnp.eye(nblk, dtype=F32)
    wd = jnp.einsum('nm,lde->dnlme', eye, w).reshape(NSA_HD, page, nblk * NSA_HD).astype(BF16)
    out = pl.pallas_call(
        _compress_pages_kernel,
        grid=(n_pool // tp,),
        in_specs=[pl.BlockSpec((1, tp, NSA_KV, NSA_HD, page), lambda i: (layer, i, 0, 0, 0)),
                  pl.BlockSpec((NSA_HD, page), lambda i: (0, 0)),
                  pl.BlockSpec((NSA_HD, page, nblk * NSA_HD), lambda i: (0, 0, 0))],
        out_specs=pl.BlockSpec((NSA_KV, tp, nblk * NSA_HD), lambda i: (0, i, 0)),
        out_shape=jax.ShapeDtypeStruct((NSA_KV, n_pool, nblk * NSA_HD), F32),
        compiler_params=_params("parallel"),
        name="nsa_compress_pages",
    )(cache_t, pe_t, wd)
    return out.reshape(NSA_KV, n_pool, nblk, NSA_HD).transpose(1, 2, 0, 3).reshape(n_pool, nblk, NSA_KW)


def _row_softmax(s, mask):
    sm = jnp.where(mask, s, NEG)
    m = jnp.max(sm, axis=1, keepdims=True)
    p = jnp.where(mask, jnp.exp(sm - m), 0.0)
    l = jnp.sum(p, axis=1, keepdims=True)
    return p * jnp.where(l > 0.0, 1.0 / l, 0.0)


def _nsa_decode_kernel(pt_ref, q_ref, gl_ref, kcb_ref, vcb_ref, *refs, nq, ncb, npages, page, past, wb):
    ks_pages = refs[0:npages]
    vs_pages = refs[npages:2 * npages]
    (nks_ref, nvs_ref, nkw_ref, nvw_ref, wk_ref, wv_ref, slope_ref, expand_ref,
     o_ref, wko_ref, wvo_ref) = refs[2 * npages:]
    del pt_ref
    c = NSA_REP * nq
    nblk_lanes = _KT
    q = q_ref[0] * (NSA_HD ** -0.5)
    row = lax.broadcasted_iota(jnp.int32, (c, 1), 0)
    qpos = past + (row & (nq - 1))
    qposf = qpos.astype(F32)
    pad_rows = lambda x: jnp.concatenate([x, jnp.zeros((_KT - nq, x.shape[1]), F32)], axis=0)
    lane = lax.broadcasted_iota(jnp.int32, (1, _KT), 1)
    n_sel_keys = (npages + 1) * page
    key_all = lax.broadcasted_iota(jnp.int32, (1, n_sel_keys), 1)
    expand = expand_ref[...]
    pr = lax.broadcasted_iota(jnp.int32, (ncb, nblk_lanes), 0)
    pc = lax.broadcasted_iota(jnp.int32, (ncb, nblk_lanes), 1)
    pair = jnp.where(jnp.where(pr < ncb // 2, pr, pr - ncb // 2) == pc, 1.0, 0.0).astype(BF16)
    new_kt = pad_rows(nkw_ref[0]).T
    new_vt = pad_rows(nvw_ref[0]).T

    for g in range(NSA_KV):
        gs = slice(g * NSA_HD, (g + 1) * NSA_HD)
        qg = jnp.concatenate([q[:, (g * NSA_REP + r) * NSA_HD:(g * NSA_REP + r + 1) * NSA_HD]
                              for r in range(NSA_REP)], axis=0).astype(BF16)
        slope = slope_ref[g]
        s_c = lax.dot_general(qg, kcb_ref[0, :, gs], _NT, preferred_element_type=F32)
        cl = lax.broadcasted_iota(jnp.int32, (1, ncb), 1)
        half = ncb // 2
        c_pos = jnp.where(cl < half, 2 * cl, 2 * (cl - half) + 1) * L_CMP + (L_CMP - 1)
        d_c = qpos - c_pos
        p_c = _row_softmax(s_c - slope * d_c.astype(F32), d_c >= 0)
        o_c = jnp.dot(p_c.astype(BF16), vcb_ref[0, :, gs], preferred_element_type=F32)
        imp = p_c[0:nq]
        for r in range(1, NSA_REP):
            imp = imp + p_c[r * nq:(r + 1) * nq]
        imp_sel = _dot_exact_rhs(imp, pair)
        cur = (past + lax.broadcasted_iota(jnp.int32, (nq, 1), 0)) >> 6
        valid = lane <= cur
        forced = jnp.where(valid, jnp.where(lane == 0, 1.0, jnp.where(lane >= cur - 1, 1.0, 0.0)), 0.0)
        score = jnp.where(forced > 0.0, _BIG, jnp.where(valid, imp_sel, -1.0))
        sel = jnp.zeros((nq, nblk_lanes), F32)
        for _ in range(N_SEL):
            m = jnp.max(score, axis=1, keepdims=True)
            idx = jnp.min(jnp.where(score == m, lane, nblk_lanes + 1), axis=1, keepdims=True)
            pick = lane == idx
            sel = jnp.where(pick, 1.0, sel)
            score = jnp.where(pick, -2.0, score)
        sel = jnp.where(valid, sel, 0.0)
        sel_keys = jnp.dot(sel.astype(BF16), expand, preferred_element_type=F32)
        sel_keys = jnp.concatenate([sel_keys] * NSA_REP, axis=0)

        nk = pad_rows(nks_ref[0, :, gs]).astype(BF16)
        nv = pad_rows(nvs_ref[0, :, gs]).astype(BF16)
        s_parts = [jnp.dot(qg, ks_pages[j][0, 0, g].astype(BF16), preferred_element_type=F32) for j in range(npages)]
        s_parts.append(lax.dot_general(qg, nk, _NT, preferred_element_type=F32))
        s_s = jnp.concatenate(s_parts, axis=1)
        d_s = qpos - key_all
        p_s = _row_softmax(s_s - slope * d_s.astype(F32), jnp.where(d_s >= 0, sel_keys, 0.0) > 0.0).astype(BF16)
        o_s = jnp.dot(p_s[:, npages * page:], nv, preferred_element_type=F32)
        for j in range(npages):
            o_s = o_s + lax.dot_general(p_s[:, j * page:(j + 1) * page], vs_pages[j][0, 0, g].astype(BF16), _NT,
                                        preferred_element_type=F32)

        nkw = pad_rows(nkw_ref[0, :, gs]).astype(BF16)
        nvw = pad_rows(nvw_ref[0, :, gs]).astype(BF16)
        s_w = jnp.concatenate([jnp.dot(qg, wk_ref[0, 0, g].astype(BF16), preferred_element_type=F32),
                               lax.dot_general(qg, nkw, _NT, preferred_element_type=F32)], axis=1)
        w_pos = past - wb + lax.broadcasted_iota(jnp.int32, (1, wb + _KT), 1)
        d_w = qpos - w_pos
        p_w = _row_softmax(s_w - slope * d_w.astype(F32), jnp.where(d_w >= 0, d_w, WINDOW + 1) <= WINDOW).astype(BF16)
        o_w = (lax.dot_general(p_w[:, 0:wb], wv_ref[0, 0, g].astype(BF16), _NT, preferred_element_type=F32)
               + jnp.dot(p_w[:, wb:], nvw, preferred_element_type=F32))

        gate = _sigmoid(gl_ref[0, g])
        o_ref[0, g] = gate[:, 0:1] * o_c + gate[:, 1:2] * o_s + gate[:, 2:3] * o_w

        for src_ref, new_full, dst_ref in ((wk_ref, new_kt, wko_ref), (wv_ref, new_vt, wvo_ref)):
            new_t = pltpu.roll(new_full[gs, :], _KT - nq, axis=1)
            shifted = pltpu.roll(src_ref[0, 0, g], wb - nq, axis=1)
            dst_ref[0, g, :, 0:wb - _KT] = shifted[:, 0:wb - _KT]
            dst_ref[0, g, :, wb - _KT:wb] = jnp.where(lane >= _KT - nq, new_t, shifted[:, wb - _KT:wb])


def nsa_decode(q, glog, kcb, vcb, pool_k, pool_v, page_table, nks, nvs, nkw, nvw, win_k, win_v, layer):
    b, nq, _ = q.shape
    ncb = kcb.shape[1]
    npages = page_table.shape[1]
    page = pool_k.shape[4]
    past = npages * page
    wb = win_k.shape[4]
    assert page == _KT and wb % _KT == 0 and nq & (nq - 1) == 0 and nq % 8 == 0 and nq <= L_SEL
    assert ncb % 2 == 0 and ncb // 2 <= _KT and -(-(past + nq) // L_SEL) <= _KT
    c = NSA_REP * nq
    _, slope = _nsa_cols(nq)
    gl = glog.reshape(b, nq, NSA_KV, NSA_REP, 3).transpose(0, 2, 3, 1, 4).reshape(b, NSA_KV, c, 3)
    per_b = lambda shape: pl.BlockSpec((1,) + shape, lambda i, pt: (i,) + (0,) * len(shape))
    const = lambda shape: pl.BlockSpec(shape, lambda i, pt: (0,) * len(shape))
    page_spec = lambda j: pl.BlockSpec((1, 1, NSA_KV, NSA_HD, page), lambda i, pt: (layer, pt[i, j], 0, 0, 0))
    win_spec = pl.BlockSpec((1, 1, NSA_KV, NSA_HD, wb), lambda i, pt: (layer, i, 0, 0, 0))
    n_keys = (npages + 1) * page
    expand = (jnp.arange(n_keys)[None, :] // L_SEL == jnp.arange(_KT)[:, None]).astype(BF16)
    in_specs = ([per_b((nq, NSA_W)), per_b((NSA_KV, c, 3)), per_b((ncb, NSA_KW)), per_b((ncb, NSA_KW))]
                + [page_spec(j) for j in range(npages)] * 2
                + [per_b((nq, NSA_KW))] * 4 + [win_spec] * 2 + [const((NSA_KV, c, 1)), const((_KT, n_keys))])
    out, wko, wvo = pl.pallas_call(
        functools.partial(_nsa_decode_kernel, nq=nq, ncb=ncb, npages=npages, page=page, past=past, wb=wb),
        grid_spec=pltpu.PrefetchScalarGridSpec(
            num_scalar_prefetch=1,
            grid=(b,),
            in_specs=in_specs,
            out_specs=[per_b((NSA_KV, c, NSA_HD)), per_b((NSA_KV, NSA_HD, wb)), per_b((NSA_KV, NSA_HD, wb))],
        ),
        out_shape=[jax.ShapeDtypeStruct((b, NSA_KV, c, NSA_HD), F32),
                   jax.ShapeDtypeStruct((b, NSA_KV, NSA_HD, wb), F32),
                   jax.ShapeDtypeStruct((b, NSA_KV, NSA_HD, wb), F32)],
        compiler_params=_params("arbitrary"),
        name="nsa_decode",
    )(page_table, q, gl, _even_odd(kcb).astype(BF16), _even_odd(vcb).astype(BF16),
      *([pool_k] * npages), *([pool_v] * npages), nks, nvs, nkw, nvw, win_k, win_v,
      slope.reshape(NSA_KV, c, 1), expand)
    o = out.reshape(b, NSA_KV, NSA_REP, nq, NSA_HD).transpose(0, 3, 1, 2, 4).reshape(b, nq, NSA_W)
    return o, wko, wvo


def _xattn_cache_kernel(q_ref, k_ref, v_ref, o_ref, *, nq):
    scale = XA_HD ** -0.5
    q = jnp.concatenate([q_ref[0, :, h * XA_HD:(h + 1) * XA_HD] for h in range(XA_HEADS)], axis=0)
    k = k_ref[0, 0].astype(BF16)
    v = v_ref[0, 0].astype(BF16)
    s = lax.dot_general(q.astype(BF16), k, _NT, preferred_element_type=F32) * scale
    col_h = lax.broadcasted_iota(jnp.int32, s.shape, 1) & (XA_HEADS - 1)
    row_h = lax.broadcasted_iota(jnp.int32, s.shape, 0) >> (nq.bit_length() - 1)
    mine = col_h == row_h
    m = jnp.max(jnp.where(mine, s, NEG), axis=1, keepdims=True)
    p = jnp.where(mine, jnp.exp(s - m), 0.0)
    p = p / jnp.sum(p, axis=1, keepdims=True)
    o = jnp.dot(p.astype(BF16), v, preferred_element_type=F32)
    for h in range(XA_HEADS):
        o_ref[0, :, h * XA_HD:(h + 1) * XA_HD] = o[h * nq:(h + 1) * nq]


def xattn_cache(q, cache_k, cache_v, layer):
    b, nq, w = q.shape
    m = cache_k.shape[2]
    assert XA_HEADS & (XA_HEADS - 1) == 0 and nq % 8 == 0
    kv = lambda a: a.reshape(a.shape[0], b, m * XA_HEADS, XA_HD)
    kv_spec = pl.BlockSpec((1, 1, m * XA_HEADS, XA_HD), lambda i: (layer, i, 0, 0))
    return pl.pallas_call(
        functools.partial(_xattn_cache_kernel, nq=nq),
        grid=(b,),
        in_specs=[pl.BlockSpec((1, nq, w), lambda i: (i, 0, 0)), kv_spec, kv_spec],
        out_specs=pl.BlockSpec((1, nq, w), lambda i: (i, 0, 0)),
        out_shape=jax.ShapeDtypeStruct((b, nq, w), F32),
        compiler_params=_params("parallel"),
        name="xattn_cache",
    )(q, kv(cache_k), kv(cache_v))


_HALO_M = 8
_TN = (((0,), (0,)), ((), ()))


def _softplus(x):
    return jnp.maximum(x, 0.0) + jnp.log1p(jnp.exp(-jnp.abs(x)))


def _ssd_kernel(xbc_ref, z_ref, sm_ref, dtt_ref, cs_ref, h0_ref, cw_ref, cb_ref, dtb_ref, dtbt_ref,
                al_ref, alt_ref, dsk_ref, ng_ref, y_ref, ncs_ref, hf_ref, ext_ref, h_ref, yh_ref, *, ql, dt_col):
    c = pl.program_id(1)
    nc = pl.num_programs(1)

    @pl.when(c == 0)
    def _():
        ext_ref[...] = jnp.zeros_like(ext_ref)
        ext_ref[_HALO_M - (M_CONV_W - 1):_HALO_M, :] = cs_ref[0]
        h_ref[...] = h0_ref[0]

    @pl.when(c > 0)
    def _():
        ext_ref[0:_HALO_M, :] = ext_ref[ql:ql + _HALO_M, :]

    ext_ref[_HALO_M:_HALO_M + ql, :] = xbc_ref[0]
    acc = jnp.zeros((ql, M_CONV_DIM), F32)
    for k in range(M_CONV_W):
        off = _HALO_M - (M_CONV_W - 1) + k
        acc = acc + ext_ref[off:off + ql, :] * cw_ref[k:k + 1, :]
    xbc = _silu(acc + cb_ref[...])
    xs = xbc[:, 0:M_DIN]
    bm = xbc[:, M_DIN:M_DIN + M_GROUPS * M_DSTATE]
    cm = xbc[:, M_DIN + M_GROUPS * M_DSTATE:M_CONV_DIM]

    dt = _softplus(sm_ref[0, :, dt_col:dt_col + M_HEADS] + dtb_ref[...])
    dtt = _softplus(dtt_ref[0] + dtbt_ref[...])
    dta = dt * (-jnp.exp(al_ref[...]))
    dtat = dtt * (-jnp.exp(alt_ref[...]))
    ti = lax.broadcasted_iota(jnp.int32, (ql, ql), 0)
    si = lax.broadcasted_iota(jnp.int32, (ql, ql), 1)
    causal = si <= ti
    cum = _dot_exact_lhs(jnp.where(causal, 1.0, 0.0).astype(BF16), dta)
    cumt = _dot_exact_rhs(dtat, jnp.where(ti <= si, 1.0, 0.0).astype(BF16))
    cum_last = cum[ql - 1:ql, :]
    edec = jnp.exp(cum)
    eend = jnp.exp(cum_last - cum)
    elast = jnp.exp(cum_last)

    rep = M_HEADS // M_GROUPS
    for gi in range(M_GROUPS):
        b_g = bm[:, gi * M_DSTATE:(gi + 1) * M_DSTATE]
        c_g = cm[:, gi * M_DSTATE:(gi + 1) * M_DSTATE].astype(BF16)
        cb = lax.dot_general(c_g, b_g.astype(BF16), _NT, preferred_element_type=F32)
        for hh in range(rep):
            h = gi * rep + hh
            hs = slice(h * M_HDIM, (h + 1) * M_HDIM)
            lmat = jnp.where(causal, jnp.exp(cum[:, h:h + 1] - cumt[h:h + 1, :]), 0.0)
            x_h = xs[:, hs]
            xdt = (x_h * dt[:, h:h + 1]).astype(BF16)
            y_diag = jnp.dot((cb * lmat).astype(BF16), xdt, preferred_element_type=F32)
            h_in = h_ref[h]
            y_off = lax.dot_general(c_g, h_in.astype(BF16), _NT, preferred_element_type=F32) * edec[:, h:h + 1]
            bd = (b_g * eend[:, h:h + 1]).astype(BF16)
            s_chunk = lax.dot_general(xdt, bd, _TN, preferred_element_type=F32)
            h_ref[h] = elast[:, h:h + 1] * h_in + s_chunk
            yh_ref[:, hs] = y_diag + y_off + dsk_ref[:, hs] * x_h

    yz = yh_ref[...] * _silu(z_ref[0])
    y_ref[0] = _rms(yz, ng_ref[...])

    @pl.when(c == nc - 1)
    def _():
        ncs_ref[0] = ext_ref[_HALO_M + ql - (M_CONV_W - 1):_HALO_M + ql, :]
        hf_ref[0] = h_ref[...]


def ssd_mixer(xbc, z, small, dt_col, conv_state, h0, conv_w, conv_b, dt_bias, a_log, d_skip, norm_g, *, ql):
    b, t, _ = xbc.shape
    nc = t // ql
    sw = small.shape[2]
    dtt = small[:, :, dt_col:dt_col + M_HEADS].transpose(0, 2, 1)
    const = lambda shape: pl.BlockSpec(shape, lambda i, j: (0,) * len(shape))
    per_b = lambda shape: pl.BlockSpec((1,) + shape, lambda i, j: (i,) + (0,) * len(shape))
    row = lambda x: x.reshape(1, -1)
    colv = lambda x: x.reshape(-1, 1)
    return pl.pallas_call(
        functools.partial(_ssd_kernel, ql=ql, dt_col=dt_col),
        grid=(b, nc),
        in_specs=[pl.BlockSpec((1, ql, M_CONV_DIM), lambda i, j: (i, j, 0)),
                  pl.BlockSpec((1, ql, M_DIN), lambda i, j: (i, j, 0)),
                  pl.BlockSpec((1, ql, sw), lambda i, j: (i, j, 0)),
                  pl.BlockSpec((1, M_HEADS, ql), lambda i, j: (i, 0, j)),
                  per_b((M_CONV_W - 1, M_CONV_DIM)), per_b((M_HEADS, M_HDIM, M_DSTATE)),
                  const((M_CONV_W, M_CONV_DIM)), const((1, M_CONV_DIM)),
                  const((1, M_HEADS)), const((M_HEADS, 1)), const((1, M_HEADS)), const((M_HEADS, 1)),
                  const((1, M_DIN)), const((1, M_DIN))],
        out_specs=[pl.BlockSpec((1, ql, M_DIN), lambda i, j: (i, j, 0)),
                   per_b((M_CONV_W - 1, M_CONV_DIM)), per_b((M_HEADS, M_HDIM, M_DSTATE))],
        out_shape=[jax.ShapeDtypeStruct((b, t, M_DIN), F32),
                   jax.ShapeDtypeStruct((b, M_CONV_W - 1, M_CONV_DIM), F32),
                   jax.ShapeDtypeStruct((b, M_HEADS, M_HDIM, M_DSTATE), F32)],
        scratch_shapes=[pltpu.VMEM((_HALO_M + ql, M_CONV_DIM), F32),
                        pltpu.VMEM((M_HEADS, M_HDIM, M_DSTATE), F32),
                        pltpu.VMEM((ql, M_DIN), F32)],
        compiler_params=_params("parallel", "arbitrary"),
        name="ssd_mixer",
    )(xbc, z, small, dtt, conv_state, h0, conv_w, row(conv_b), row(dt_bias), colv(dt_bias),
      row(a_log), colv(a_log), row(jnp.repeat(d_skip, M_HDIM)), row(norm_g))


_SMALL_W = 128
_OD_SPLITS = (NSA_W, 6 * NSA_KW, M_DIN, M_CONV_DIM, _SMALL_W)


def _odd_w_in(w):
    o_kv = NSA_W
    o_gate = o_kv + 6 * NSA_KW
    o_z = o_gate + 3 * NSA_HEADS
    o_xbc = o_z + M_DIN
    o_dt = o_xbc + M_CONV_DIM
    pad = jnp.zeros((w.shape[0], _SMALL_W - 3 * NSA_HEADS - M_HEADS), F32)
    return jnp.concatenate([w[:, :o_gate], w[:, o_z:o_xbc], w[:, o_xbc:o_dt],
                            w[:, o_gate:o_z], w[:, o_dt:], pad], axis=1)


def kernel(x_prompt, x_sample, state_conv_a, state_conv_b, cache_cmp_k, cache_cmp_v, cache_sel_k, cache_sel_v, cache_win_k, cache_win_v, state_ssm, state_ssm_conv, cache_mem_k, cache_mem_v, page_table, mem_prompt, norm_mix, norm_xattn, norm_ffn, norm_final, ev_w_in, ev_conv_a, ev_conv_b, ev_conv_b_bias, ev_ln_g, ev_ln_b, ev_w_out, od_w_in, od_cmp_pe, od_cmp_wk, od_cmp_wv, od_ssm_conv_w, od_ssm_conv_b, od_dt_bias, od_a_log, od_d_skip, od_ssm_norm, od_w_out, xa_wq, xa_wk, xa_wv, xa_wo, moe_wg, moe_bg, moe_we, moe_be, moe_w1, moe_w3, moe_w2):
    bp, tp, d = x_prompt.shape
    bs, ts, _ = x_sample.shape
    n_p, n_s = bp * tp, bs * ts
    n_mem = mem_prompt.shape[1]
    depth = norm_mix.shape[0]
    n_pool, page = cache_cmp_k.shape[1:3]
    wb = cache_win_k.shape[2]
    dt_col = 3 * NSA_HEADS

    def groups(a):
        return a[:n_p].reshape(bp, tp, a.shape[-1]), a[n_p:].reshape(bs, ts, a.shape[-1])

    def rows(a_p, a_s):
        return jnp.concatenate([a_p.reshape(n_p, a_p.shape[-1]), a_s.reshape(n_s, a_s.shape[-1])], axis=0)

    h = rows(x_prompt, x_sample)
    out = {k: [] for k in ("ca_p", "ca_s", "cb_p", "cb_s", "wk_p", "wk_s", "wv_p", "wv_s",
                           "sm_p", "sm_s", "sc_p", "sc_s", "mk_p", "mv_p")}
    rows_p = [[], [], [], []]
    rows_s = [[], [], [], []]
    for i in range(depth):
        j = i // 2
        if i % 2 == 0:
            u_p, u_s = groups(norm_matmul(h, norm_mix[i], ev_w_in[j]))
            ev = (ev_conv_a[j], ev_conv_b[j], ev_conv_b_bias[j], ev_ln_g[j], ev_ln_b[j])
            y_p, na_p, nb_p = even_conv(u_p, jnp.zeros((bp, CONV_A_W - 1, D_A), F32),
                                        jnp.zeros((bp, CONV_B_W - 1, D_B), F32), *ev)
            y_s, na_s, nb_s = even_conv(u_s, state_conv_a[j], state_conv_b[j], *ev)
            h = matmul_res([rows(y_p, y_s)], [ev_w_out[j]], h)
            out["ca_p"].append(na_p)
            out["ca_s"].append(na_s)
            out["cb_p"].append(nb_p)
            out["cb_s"].append(nb_s)
        else:
            uq, ukv, uz, uxbc, usm = norm_matmul(h, norm_mix[i], _odd_w_in(od_w_in[j]), splits=_OD_SPLITS)
            q_p, q_s = groups(uq)
            kv_p, kv_s = groups(ukv)
            z_p, z_s = groups(uz)
            xbc_p, xbc_s = groups(uxbc)
            sm_p, sm_s = groups(usm)
            part = lambda a, k: a[:, :, k * NSA_KW:(k + 1) * NSA_KW]
            kvp = [part(kv_p, k) for k in range(6)]
            kvs = [part(kv_s, k) for k in range(6)]
            pe, wck, wcv = od_cmp_pe[j], od_cmp_wk[j], od_cmp_wv[j]
            mw = (od_ssm_conv_w[j], od_ssm_conv_b[j], od_dt_bias[j], od_a_log[j], od_d_skip[j], od_ssm_norm[j])
            blocks = lambda a: a.reshape(-1, L_CMP, NSA_KW)
            ncb = tp // L_CMP
            kcb_p = nsa_compress(blocks(kvp[0][:, :ncb * L_CMP]), pe, wck).reshape(bp, ncb, NSA_KW)
            vcb_p = nsa_compress(blocks(kvp[1][:, :ncb * L_CMP]), pe, wcv).reshape(bp, ncb, NSA_KW)
            o_p = nsa_prompt(q_p, sm_p[:, :, :dt_col], kcb_p, vcb_p, kvp[2], kvp[3], kvp[4], kvp[5])
            keep = min(WINDOW, tp)
            y_p, nsc_p, nsm_p = ssd_mixer(xbc_p, z_p, sm_p, dt_col, jnp.zeros((bp, M_CONV_W - 1, M_CONV_DIM), F32),
                                          jnp.zeros((bp, M_HEADS, M_HDIM, M_DSTATE), F32), *mw, ql=128)
            tokens_last = lambda a: jnp.transpose(a, (0, 1, 3, 4, 2))
            kcp = nsa_compress_pages(tokens_last(cache_cmp_k), j, pe, wck)
            vcp = nsa_compress_pages(tokens_last(cache_cmp_v), j, pe, wcv)
            kcb_s = kcp[page_table].reshape(bs, -1, NSA_KW)
            vcb_s = vcp[page_table].reshape(bs, -1, NSA_KW)
            o_s, nwk_s, nwv_s = nsa_decode(
                q_s, sm_s[:, :, :dt_col], kcb_s, vcb_s, tokens_last(cache_sel_k), tokens_last(cache_sel_v),
                page_table, kvs[2], kvs[3], kvs[4], kvs[5], tokens_last(cache_win_k), tokens_last(cache_win_v), j)
            nwk_s = jnp.transpose(nwk_s, (0, 3, 1, 2))
            nwv_s = jnp.transpose(nwv_s, (0, 3, 1, 2))
            y_s, nsc_s, nsm_s = ssd_mixer(xbc_s, z_s, sm_s, dt_col, state_ssm_conv[j], state_ssm[j], *mw, ql=ts)
            w_out = od_w_out[j]
            h = matmul_res([rows(o_p, o_s), rows(y_p, y_s)], [w_out[:NSA_W], w_out[NSA_W:]], h)
            heads = lambda a: a.reshape(a.shape[0], a.shape[1], NSA_KV, NSA_HD)
            for k in range(4):
                rows_p[k].append(heads(kvp[k]))
                rows_s[k].append(heads(kvs[k]))
            out["wk_p"].append(heads(kvp[4][:, tp - keep:]))
            out["wv_p"].append(heads(kvp[5][:, tp - keep:]))
            out["wk_s"].append(nwk_s)
            out["wv_s"].append(nwv_s)
            out["sc_p"].append(nsc_p)
            out["sc_s"].append(nsc_s)
            out["sm_p"].append(nsm_p)
            out["sm_s"].append(nsm_s)
        mk, mv = norm_matmul(mem_prompt.reshape(bp * n_mem, d), None,
                             jnp.concatenate([xa_wk[i], xa_wv[i]], axis=1), norm=False,
                             splits=(XA_HEADS * XA_HD, XA_HEADS * XA_HD))
        mk = mk.reshape(bp, n_mem, XA_HEADS * XA_HD)
        mv = mv.reshape(bp, n_mem, XA_HEADS * XA_HD)
        out["mk_p"].append(mk.reshape(bp, n_mem, XA_HEADS, XA_HD))
        out["mv_p"].append(mv.reshape(bp, n_mem, XA_HEADS, XA_HD))
        qx_p, qx_s = groups(norm_matmul(h, norm_xattn[i], xa_wq[i]))
        ox_p = xattn(qx_p, mk, mv)
        ox_s = xattn_cache(qx_s, cache_mem_k, cache_mem_v, i)
        h = matmul_res([rows(ox_p, ox_s)], [xa_wo[i]], h)
        h = moe_layer(h, norm_ffn[i], moe_wg[i], moe_bg[i], moe_we[i], moe_be[i], moe_w1, moe_w3, moe_w2, i)
    y = rmsnorm_rows(h, norm_final)
    y_prompt = y[:n_p].reshape(bp, tp, d)
    y_sample = y[n_p:].reshape(bs, ts, d)
    st = lambda k: jnp.stack(out[k])
    return (y_prompt, y_sample, st("ca_p"), st("ca_s"), st("cb_p"), st("cb_s"),
            jnp.stack(rows_p[0]), jnp.stack(rows_s[0]), jnp.stack(rows_p[1]), jnp.stack(rows_s[1]),
            jnp.stack(rows_p[2]), jnp.stack(rows_s[2]), jnp.stack(rows_p[3]), jnp.stack(rows_s[3]),
            st("wk_p"), st("wk_s"), st("wv_p"), st("wv_s"), st("sm_p"), st("sm_s"), st("sc_p"), st("sc_s"),
            st("mk_p"), st("mv_p"))
```

```python
import functools

import jax
import jax.numpy as jnp
from jax import lax
from jax.experimental import pallas as pl
from jax.experimental.pallas import tpu as pltpu

F32 = jnp.float32
BF16 = jnp.bfloat16
EPS = 1e-6
NEG = -1e30
VMEM_LIMIT = 56 * 1024 * 1024

D_A = 512
D_B = 512
CONV_A_W = 3
CONV_B_W = 31
NSA_HEADS = 8
NSA_HD = 64
NSA_KV = 2
NSA_REP = NSA_HEADS // NSA_KV
NSA_W = NSA_HEADS * NSA_HD
NSA_KW = NSA_KV * NSA_HD
L_CMP = 32
L_SEL = 64
N_SEL = 16
WINDOW = 512
M_DIN = 512
M_HDIM = 64
M_HEADS = 8
M_DSTATE = 64
M_GROUPS = 2
M_CONV_W = 4
M_CONV_DIM = M_DIN + 2 * M_GROUPS * M_DSTATE
XA_HEADS = 4
XA_HD = 128
MOE_GROUPS = 4
MOE_EPG = 8
MOE_E = 32
MOE_TOPK = 2


def _params(*sem):
    return pltpu.CompilerParams(dimension_semantics=sem, vmem_limit_bytes=VMEM_LIMIT)


def _row_tile(n, pref):
    t = min(n, pref)
    while n % t or (t % 8 and t != n):
        t -= 1
    return t


def _bdot(a, b):
    return jnp.dot(a.astype(BF16), b.astype(BF16), preferred_element_type=F32)


def _split3(a):
    hi = a.astype(BF16)
    r1 = a - hi.astype(F32)
    mid = r1.astype(BF16)
    lo = (r1 - mid.astype(F32)).astype(BF16)
    return hi, mid, lo


def _dot_exact_rhs(a, b_bf16):
    hi, mid, lo = _split3(a)
    d = lambda x: jnp.dot(x, b_bf16, preferred_element_type=F32)
    return d(hi) + d(mid) + d(lo)


def _dot_exact_lhs(a_bf16, b):
    hi, mid, lo = _split3(b)
    d = lambda x: jnp.dot(a_bf16, x, preferred_element_type=F32)
    return d(hi) + d(mid) + d(lo)


def _rms(x, g):
    ms = jnp.mean(x * x, axis=-1, keepdims=True)
    return x * lax.rsqrt(ms + EPS) * g


def _sigmoid(x):
    return 1.0 / (1.0 + jnp.exp(-x))


def _silu(x):
    return x * _sigmoid(x)


def _norm_matmul_kernel(x_ref, g_ref, w_ref, *o_refs, norm, splits):
    x = x_ref[...]
    if norm:
        x = _rms(x, g_ref[...])
    res = jnp.dot(x.astype(BF16), w_ref[...].astype(BF16), preferred_element_type=F32)
    off = 0
    for o_ref, width in zip(o_refs, splits):
        o_ref[...] = res[:, off:off + width]
        off += width


def norm_matmul(x, g, w, *, norm=True, splits=None, tm=512):
    n, k = x.shape
    m = w.shape[1]
    tm = _row_tile(n, tm)
    if g is None:
        g = jnp.ones((k,), F32)
    widths = (m,) if splits is None else tuple(splits)
    assert sum(widths) == m
    outs = pl.pallas_call(
        functools.partial(_norm_matmul_kernel, norm=norm, splits=widths),
        grid=(n // tm,),
        in_specs=[pl.BlockSpec((tm, k), lambda i: (i, 0)),
                  pl.BlockSpec((1, k), lambda i: (0, 0)),
                  pl.BlockSpec((k, m), lambda i: (0, 0))],
        out_specs=[pl.BlockSpec((tm, wd), lambda i: (i, 0)) for wd in widths],
        out_shape=[jax.ShapeDtypeStruct((n, wd), F32) for wd in widths],
        compiler_params=_params("parallel"),
        name="norm_matmul",
    )(x, g.reshape(1, k), w)
    return outs[0] if splits is None else outs


def _rmsnorm_kernel(x_ref, g_ref, o_ref):
    o_ref[...] = _rms(x_ref[...], g_ref[...])


def rmsnorm_rows(x, g, *, tm=512):
    n, k = x.shape
    tm = _row_tile(n, tm)
    return pl.pallas_call(
        _rmsnorm_kernel,
        grid=(n // tm,),
        in_specs=[pl.BlockSpec((tm, k), lambda i: (i, 0)), pl.BlockSpec((1, k), lambda i: (0, 0))],
        out_specs=pl.BlockSpec((tm, k), lambda i: (i, 0)),
        out_shape=jax.ShapeDtypeStruct((n, k), F32),
        compiler_params=_params("parallel"),
        name="rmsnorm_rows",
    )(x, g.reshape(1, k))


def _matmul_res_kernel(*refs, n_in):
    res_ref = refs[2 * n_in]
    o_ref = refs[2 * n_in + 1]
    acc = res_ref[...]
    for j in range(n_in):
        acc = acc + jnp.dot(refs[2 * j][...].astype(BF16), refs[2 * j + 1][...].astype(BF16),
                            preferred_element_type=F32)
    o_ref[...] = acc


def matmul_res(xs, ws, res, *, tm=512):
    n, m = res.shape
    tm = _row_tile(n, tm)
    in_specs, args = [], []
    for x, w in zip(xs, ws):
        k = x.shape[1]
        in_specs += [pl.BlockSpec((tm, k), lambda i: (i, 0)), pl.BlockSpec((k, m), lambda i: (0, 0))]
        args += [x, w]
    in_specs.append(pl.BlockSpec((tm, m), lambda i: (i, 0)))
    return pl.pallas_call(
        functools.partial(_matmul_res_kernel, n_in=len(xs)),
        grid=(n // tm,),
        in_specs=in_specs,
        out_specs=pl.BlockSpec((tm, m), lambda i: (i, 0)),
        out_shape=jax.ShapeDtypeStruct((n, m), F32),
        compiler_params=_params("parallel"),
        name="matmul_res",
    )(*args, res)


_HALO_A = 8
_HALO_B = 32


def _even_conv_kernel(u_ref, sa_ref, sb_ref, wa_ref, wb_ref, bb_ref, lg_ref, lb_ref,
                      y_ref, na_ref, nb_ref, ea_ref, eb_ref, ear_ref, ebr_ref, *, tt):
    rnd = lambda x: x.astype(BF16).astype(F32)
    t = pl.program_id(1)
    nt = pl.num_programs(1)

    @pl.when(t == 0)
    def _():
        ea_ref[...] = jnp.zeros_like(ea_ref)
        eb_ref[...] = jnp.zeros_like(eb_ref)
        ea_ref[_HALO_A - (CONV_A_W - 1):_HALO_A, :] = sa_ref[0]
        eb_ref[_HALO_B - (CONV_B_W - 1):_HALO_B, :] = sb_ref[0]
        ear_ref[...] = rnd(ea_ref[...])
        ebr_ref[...] = rnd(eb_ref[...])

    @pl.when(t > 0)
    def _():
        ea_ref[0:_HALO_A, :] = ea_ref[tt:tt + _HALO_A, :]
        eb_ref[0:_HALO_B, :] = eb_ref[tt:tt + _HALO_B, :]
        ear_ref[0:_HALO_A, :] = ear_ref[tt:tt + _HALO_A, :]
        ebr_ref[0:_HALO_B, :] = ebr_ref[tt:tt + _HALO_B, :]

    xa = u_ref[0, :, 0:D_A]
    ba = u_ref[0, :, D_A:2 * D_A]
    ca = u_ref[0, :, 2 * D_A:3 * D_A]
    pb = u_ref[0, :, 3 * D_A:3 * D_A + D_B]
    gb = u_ref[0, :, 3 * D_A + D_B:3 * D_A + 2 * D_B]
    va = ca * xa
    vb = pb * _sigmoid(gb)
    ea_ref[_HALO_A:_HALO_A + tt, :] = va
    eb_ref[_HALO_B:_HALO_B + tt, :] = vb
    ear_ref[_HALO_A:_HALO_A + tt, :] = rnd(va)
    ebr_ref[_HALO_B:_HALO_B + tt, :] = rnd(vb)

    acc = jnp.zeros((tt, D_A), F32)
    for k in range(CONV_A_W):
        off = _HALO_A - (CONV_A_W - 1) + k
        acc = acc + ear_ref[off:off + tt, :] * wa_ref[k:k + 1, :]
    y_ref[0, :, 0:D_A] = ba * acc

    acc = jnp.zeros((tt, D_B), F32)
    for k in range(CONV_B_W):
        off = _HALO_B - (CONV_B_W - 1) + k
        acc = acc + ebr_ref[off:off + tt, :] * wb_ref[k:k + 1, :]
    acc = acc + bb_ref[...]
    mu = jnp.mean(acc, axis=-1, keepdims=True)
    xc = acc - mu
    var = jnp.mean(xc * xc, axis=-1, keepdims=True)
    yb = xc * lax.rsqrt(var + EPS) * lg_ref[...] + lb_ref[...]
    y_ref[0, :, D_A:D_A + D_B] = _silu(yb)

    @pl.when(t == nt - 1)
    def _():
        na_ref[0] = ea_ref[_HALO_A + tt - (CONV_A_W - 1):_HALO_A + tt, :]
        nb_ref[0] = eb_ref[_HALO_B + tt - (CONV_B_W - 1):_HALO_B + tt, :]


def even_conv(u, sa, sb, wa, wb, bb, lg, lb, *, tt=256):
    b, t, w = u.shape
    tt = _row_tile(t, tt)
    full = lambda shape: pl.BlockSpec(shape, lambda i, j: (0,) * len(shape))
    return pl.pallas_call(
        functools.partial(_even_conv_kernel, tt=tt),
        grid=(b, t // tt),
        in_specs=[pl.BlockSpec((1, tt, w), lambda i, j: (i, j, 0)),
                  pl.BlockSpec((1, CONV_A_W - 1, D_A), lambda i, j: (i, 0, 0)),
                  pl.BlockSpec((1, CONV_B_W - 1, D_B), lambda i, j: (i, 0, 0)),
                  full((CONV_A_W, D_A)), full((CONV_B_W, D_B)), full((1, D_B)),
                  full((1, D_B)), full((1, D_B))],
        out_specs=[pl.BlockSpec((1, tt, D_A + D_B), lambda i, j: (i, j, 0)),
                   pl.BlockSpec((1, CONV_A_W - 1, D_A), lambda i, j: (i, 0, 0)),
                   pl.BlockSpec((1, CONV_B_W - 1, D_B), lambda i, j: (i, 0, 0))],
        out_shape=[jax.ShapeDtypeStruct((b, t, D_A + D_B), F32),
                   jax.ShapeDtypeStruct((b, CONV_A_W - 1, D_A), F32),
                   jax.ShapeDtypeStruct((b, CONV_B_W - 1, D_B), F32)],
        scratch_shapes=[pltpu.VMEM((_HALO_A + tt, D_A), F32), pltpu.VMEM((_HALO_B + tt, D_B), F32),
                        pltpu.VMEM((_HALO_A + tt, D_A), F32), pltpu.VMEM((_HALO_B + tt, D_B), F32)],
        compiler_params=_params("parallel", "arbitrary"),
        name="even_conv",
    )(u, sa, sb, wa, wb, bb.reshape(1, D_B), lg.reshape(1, D_B), lb.reshape(1, D_B))


def _xattn_kernel(q_ref, k_ref, v_ref, o_ref, *, cache_layout):
    scale = XA_HD ** -0.5
    for h in range(XA_HEADS):
        sl = slice(h * XA_HD, (h + 1) * XA_HD)
        q = q_ref[0, :, sl].astype(BF16)
        if cache_layout:
            k = k_ref[0, 0, :, h, :].astype(BF16)
            v = v_ref[0, 0, :, h, :].astype(BF16)
        else:
            k = k_ref[0, :, sl].astype(BF16)
            v = v_ref[0, :, sl].astype(BF16)
        s = lax.dot_general(q, k, (((1,), (1,)), ((), ())), preferred_element_type=F32) * scale
        m = jnp.max(s, axis=-1, keepdims=True)
        p = jnp.exp(s - m)
        p = p / jnp.sum(p, axis=-1, keepdims=True)
        o_ref[0, :, sl] = jnp.dot(p.astype(BF16), v, preferred_element_type=F32)


def xattn(q, k, v, *, layer=None, tq=512):
    b, t, w = q.shape
    tq = _row_tile(t, tq)
    if layer is None:
        m = k.shape[1]
        kv_spec = pl.BlockSpec((1, m, w), lambda i, j: (i, 0, 0))
    else:
        m = k.shape[2]
        kv_spec = pl.BlockSpec((1, 1, m, XA_HEADS, XA_HD), lambda i, j: (layer, i, 0, 0, 0))
    return pl.pallas_call(
        functools.partial(_xattn_kernel, cache_layout=layer is not None),
        grid=(b, t // tq),
        in_specs=[pl.BlockSpec((1, tq, w), lambda i, j: (i, j, 0)), kv_spec, kv_spec],
        out_specs=pl.BlockSpec((1, tq, w), lambda i, j: (i, j, 0)),
        out_shape=jax.ShapeDtypeStruct((b, t, w), F32),
        compiler_params=_params("parallel", "parallel"),
        name="xattn",
    )(q, k, v)


MOE_BLK = 256
_ROUTER_W = 128


def _router_kernel(x_ref, g_ref, w_ref, lg_ref, xn_ref):
    x = _rms(x_ref[...], g_ref[...])
    xb = x.astype(BF16)
    xn_ref[...] = xb
    lg_ref[...] = jnp.dot(xb, w_ref[...].astype(BF16), preferred_element_type=F32)


def moe_router(x, g, w_router, *, tm=512):
    n, k = x.shape
    tm = _row_tile(n, tm)
    return pl.pallas_call(
        _router_kernel,
        grid=(n // tm,),
        in_specs=[pl.BlockSpec((tm, k), lambda i: (i, 0)),
                  pl.BlockSpec((1, k), lambda i: (0, 0)),
                  pl.BlockSpec((k, _ROUTER_W), lambda i: (0, 0))],
        out_specs=[pl.BlockSpec((tm, _ROUTER_W), lambda i: (i, 0)),
                   pl.BlockSpec((tm, k), lambda i: (i, 0))],
        out_shape=[jax.ShapeDtypeStruct((n, _ROUTER_W), F32), jax.ShapeDtypeStruct((n, k), BF16)],
        compiler_params=_params("parallel"),
        name="moe_router",
    )(x, g.reshape(1, k), w_router)


def _expert_kernel(be_ref, act_ref, x_ref, gate_ref, w1_ref, w3_ref, w2_ref, o_ref, w1b_ref, w3b_ref, w2b_ref):
    i = pl.program_id(0)
    prev = be_ref[jnp.maximum(i - 1, 0)]

    @pl.when((act_ref[i] > 0) & ((i == 0) | (be_ref[i] != prev)))
    def _():
        w1b_ref[...] = w1_ref[0, 0].astype(BF16)
        w3b_ref[...] = w3_ref[0, 0].astype(BF16)
        w2b_ref[...] = w2_ref[0, 0].astype(BF16)

    @pl.when(act_ref[i] > 0)
    def _():
        x = x_ref[...].astype(BF16)
        h1 = jnp.dot(x, w1b_ref[...], preferred_element_type=F32)
        h3 = jnp.dot(x, w3b_ref[...], preferred_element_type=F32)
        hid = (_silu(h1) * h3).astype(BF16)
        out = jnp.dot(hid, w2b_ref[...], preferred_element_type=F32)
        o_ref[...] = out * gate_ref[...]

    @pl.when(act_ref[i] == 0)
    def _():
        o_ref[...] = jnp.zeros_like(o_ref)


def moe_experts(xg, gate, blk_exp, blk_act, w1, w3, w2, layer):
    rows, d = xg.shape
    nb = rows // MOE_BLK
    ff = w1.shape[3]
    return pl.pallas_call(
        _expert_kernel,
        grid_spec=pltpu.PrefetchScalarGridSpec(
            num_scalar_prefetch=2,
            grid=(nb,),
            in_specs=[pl.BlockSpec((MOE_BLK, d), lambda i, be, act: (i, 0)),
                      pl.BlockSpec((MOE_BLK, 1), lambda i, be, act: (i, 0)),
                      pl.BlockSpec((1, 1, d, ff), lambda i, be, act: (layer, be[i], 0, 0)),
                      pl.BlockSpec((1, 1, d, ff), lambda i, be, act: (layer, be[i], 0, 0)),
                      pl.BlockSpec((1, 1, ff, d), lambda i, be, act: (layer, be[i], 0, 0))],
            out_specs=pl.BlockSpec((MOE_BLK, d), lambda i, be, act: (i, 0)),
            scratch_shapes=[pltpu.VMEM((d, ff), BF16), pltpu.VMEM((d, ff), BF16), pltpu.VMEM((ff, d), BF16)],
        ),
        out_shape=jax.ShapeDtypeStruct((rows, d), F32),
        compiler_params=_params("arbitrary"),
        name="moe_experts",
    )(blk_exp, blk_act, xg, gate, w1, w3, w2)


def moe_layer(h, g, wg, bg, we, be, w1, w3, w2, layer):
    n, d = h.shape
    w_router = jnp.concatenate([wg, we, jnp.zeros((d, _ROUTER_W - MOE_GROUPS - MOE_E), F32)], axis=1)
    logits, xn = moe_router(h, g, w_router)
    lg = logits[:, :MOE_GROUPS] + bg
    grp = jnp.argmax(lg, axis=-1)
    gw = jnp.take_along_axis(jax.nn.softmax(lg, axis=-1), grp[:, None], axis=1)
    le = (logits[:, MOE_GROUPS:MOE_GROUPS + MOE_E] + be).reshape(n, MOE_GROUPS, MOE_EPG)
    le = jnp.take_along_axis(le, grp[:, None, None], axis=1)[:, 0]
    tv, ti = lax.top_k(jax.nn.softmax(le, axis=-1), MOE_TOPK)
    wts = gw * tv / jnp.sum(tv, axis=-1, keepdims=True)
    eid = (grp[:, None] * MOE_EPG + ti).reshape(-1).astype(jnp.int32)
    npair = n * MOE_TOPK
    experts = jnp.arange(MOE_E, dtype=jnp.int32)
    order = jnp.argsort(eid).astype(jnp.int32)
    rank = jnp.argsort(order).astype(jnp.int32)
    counts = jnp.sum(eid[:, None] == experts[None, :], axis=0).astype(jnp.int32)
    start = jnp.cumsum(counts) - counts
    padded = (counts + MOE_BLK - 1) // MOE_BLK * MOE_BLK
    pend = jnp.cumsum(padded)
    shift = pend - padded - start
    nb = -(-npair // MOE_BLK) + MOE_E
    blk_lo = jnp.arange(nb, dtype=jnp.int32) * MOE_BLK
    blk_exp = jnp.minimum(jnp.sum(pend[None, :] <= blk_lo[:, None], axis=1), MOE_E - 1).astype(jnp.int32)
    blk_act = (blk_lo < pend[-1]).astype(jnp.int32)
    src = (blk_lo - shift[blk_exp])[:, None] + jnp.arange(MOE_BLK, dtype=jnp.int32)[None, :]
    live = src < (start + counts)[blk_exp][:, None]
    pair = order[jnp.where(live, src, 0).reshape(-1)]
    buf_tok = pair // MOE_TOPK
    buf_gate = jnp.where(live.reshape(-1), wts.reshape(-1)[pair], 0.0)
    dest = (rank + shift[eid]).reshape(n, MOE_TOPK)
    out = moe_experts(xn[buf_tok], buf_gate[:, None], blk_exp, blk_act, w1, w3, w2, layer)
    return h + (out[dest[:, 0]] + out[dest[:, 1]])


_KT = 128
_NT = (((1,), (1,)), ((), ()))
_BIG = 3e38
_M0 = -1e29


def _compress_kernel(x_ref, pe_ref, w_ref, o_ref):
    acc = jnp.zeros(o_ref.shape, F32)
    for l in range(L_CMP):
        y = x_ref[:, l, :] + pe_ref[l:l + 1, :]
        acc = acc + jnp.dot(y.astype(BF16), w_ref[l], preferred_element_type=F32)
    o_ref[...] = acc


def nsa_compress(x, pe, w, *, tb=256):
    nb = x.shape[0]
    tb = _row_tile(nb, tb)
    pe2 = jnp.concatenate([pe] * NSA_KV, axis=1)
    z = jnp.zeros_like(w)
    w2 = jnp.concatenate([jnp.concatenate([w, z], axis=2), jnp.concatenate([z, w], axis=2)], axis=1).astype(BF16)
    return pl.pallas_call(
        _compress_kernel,
        grid=(nb // tb,),
        in_specs=[pl.BlockSpec((tb, L_CMP, NSA_KW), lambda i: (i, 0, 0)),
                  pl.BlockSpec((L_CMP, NSA_KW), lambda i: (0, 0)),
                  pl.BlockSpec((L_CMP, NSA_KW, NSA_KW), lambda i: (0, 0, 0))],
        out_specs=pl.BlockSpec((tb, NSA_KW), lambda i: (i, 0)),
        out_shape=jax.ShapeDtypeStruct((nb, NSA_KW), F32),
        compiler_params=_params("parallel"),
        name="nsa_compress",
    )(x, pe2, w2)


def _compress_cache_kernel(x_ref, pe_ref, w_ref, o_ref):
    for g in range(NSA_KV):
        acc = jnp.zeros((o_ref.shape[0], NSA_HD), F32)
        for l in range(L_CMP):
            y = x_ref[0, :, l, g, :] + pe_ref[l:l + 1, :]
            acc = acc + jnp.dot(y.astype(BF16), w_ref[l], preferred_element_type=F32)
        o_ref[:, g * NSA_HD:(g + 1) * NSA_HD] = acc


def nsa_compress_cache(cache, layer, pe, w, *, tb=256):
    nl, n_pool, page = cache.shape[:3]
    nb = n_pool * page // L_CMP
    tb = _row_tile(nb, tb)
    x = cache.reshape(nl, nb, L_CMP, NSA_KV, NSA_HD)
    return pl.pallas_call(
        _compress_cache_kernel,
        grid=(nb // tb,),
        in_specs=[pl.BlockSpec((1, tb, L_CMP, NSA_KV, NSA_HD), lambda i: (layer, i, 0, 0, 0)),
                  pl.BlockSpec((L_CMP, NSA_HD), lambda i: (0, 0)),
                  pl.BlockSpec((L_CMP, NSA_HD, NSA_HD), lambda i: (0, 0, 0))],
        out_specs=pl.BlockSpec((tb, NSA_KW), lambda i: (i, 0)),
        out_shape=jax.ShapeDtypeStruct((nb, NSA_KW), F32),
        compiler_params=_params("parallel"),
        name="nsa_compress_cache",
    )(x, pe, w.astype(BF16))


def _qz(q, g, nq):
    lane = lax.broadcasted_iota(jnp.int32, (nq, NSA_KW), 1)
    keep = (lane >> 6) == g
    parts = []
    for r in range(NSA_REP):
        h = g * NSA_REP + r
        slab = q[:, (h // 2) * NSA_KW:(h // 2 + 1) * NSA_KW]
        if h % 2 != g:
            slab = pltpu.roll(slab, NSA_HD, axis=1)
        parts.append(jnp.where(keep, slab, 0.0))
    return jnp.concatenate(parts, axis=0).astype(BF16)


def _softmax_init(m_ref, l_ref, acc_ref):
    m_ref[...] = jnp.full(m_ref.shape, NEG, F32)
    l_ref[...] = jnp.zeros(l_ref.shape, F32)
    acc_ref[...] = jnp.zeros(acc_ref.shape, F32)


def _softmax_tile(st, dpos, maskf, slope, vt, m_ref, l_ref, acc_ref, g):
    s = st - slope * dpos
    sm = jnp.where(maskf > 0.0, s, NEG)
    m_old = m_ref[g]
    m_new = jnp.maximum(m_old, jnp.max(sm, axis=0, keepdims=True))
    alpha = jnp.exp(m_old - m_new)
    p = jnp.exp(sm - m_new) * maskf
    l_ref[g] = alpha * l_ref[g] + jnp.sum(p, axis=0, keepdims=True)
    acc_ref[g] = alpha * acc_ref[g] + jnp.dot(vt, p.astype(BF16), preferred_element_type=F32)
    m_ref[g] = m_new


def _softmax_done(l_ref, acc_ref, g):
    l = l_ref[g]
    return acc_ref[g] * jnp.where(l > 0.0, 1.0 / l, 0.0)


def _compressed_branch(qz, kcb, vcbt_g, qpos, slope, ncb):
    st = lax.dot_general(kcb, qz, _NT, preferred_element_type=F32)
    row = lax.broadcasted_iota(jnp.int32, (ncb, 1), 0)
    half = ncb // 2
    blk = jnp.where(row < half, 2 * row, 2 * (row - half) + 1)
    c_pos = blk * L_CMP + (L_CMP - 1)
    d_c = qpos - c_pos
    maskf = jnp.where(d_c >= 0, 1.0, 0.0)
    s = st - slope * d_c.astype(F32)
    sm = jnp.where(d_c >= 0, s, NEG)
    m = jnp.max(sm, axis=0, keepdims=True)
    p = jnp.exp(sm - m) * maskf
    l = jnp.sum(p, axis=0, keepdims=True)
    p = p * jnp.where(l > 0.0, 1.0 / l, 0.0)
    o = jnp.dot(vcbt_g, p.astype(BF16), preferred_element_type=F32)
    return o, p


def _select_blocks(imp_sel, qpos, nsp):
    cols = imp_sel.shape[1]
    blk = lax.broadcasted_iota(jnp.int32, (nsp, cols), 0)
    cur = qpos >> 6
    valid = blk <= cur
    forced = jnp.where(valid, jnp.where(blk == 0, 1.0, jnp.where(blk >= cur - 1, 1.0, 0.0)), 0.0)
    score = jnp.where(forced > 0.0, _BIG, jnp.where(valid, imp_sel, -1.0))
    sel = jnp.zeros((nsp, cols), F32)
    for _ in range(N_SEL):
        m = jnp.max(score, axis=0, keepdims=True)
        idx = jnp.min(jnp.where(score == m, blk, nsp + 1), axis=0, keepdims=True)
        pick = blk == idx
        sel = jnp.where(pick, 1.0, sel)
        score = jnp.where(pick, -2.0, score)
    return jnp.where(valid, sel, 0.0)


def _nsa_prompt_kernel(q_ref, glog_ref, kcb_ref, vcbt_ref, ks_ref, vst_ref, kw_ref, vwt_ref,
                       qoff_ref, slope_ref, qaux_ref, o_ref, sel_ref, m_ref, l_ref, acc_ref, *, nq, ncb):
    i = pl.program_id(1)
    st0 = i * nq
    qoff = qoff_ref[...]
    qpos = st0 + qoff
    q = q_ref[0] * (NSA_HD ** -0.5)
    key_io = lax.broadcasted_iota(jnp.int32, (_KT, 1), 0)
    lane = lax.broadcasted_iota(jnp.int32, (_KT, NSA_KW), 1)
    key_lane = jnp.where(lane == 1, lax.broadcasted_iota(jnp.int32, (_KT, NSA_KW), 0), 0).astype(F32)
    causal = key_io <= qoff
    anti = key_io >= qoff
    nsp = ncb // 2
    wt = WINDOW // _KT

    def m_init():
        m_ref[...] = jnp.full(m_ref.shape, _M0, F32)
        l_ref[...] = jnp.zeros(l_ref.shape, F32)
        acc_ref[...] = jnp.zeros(acc_ref.shape, F32)

    for g in range(NSA_KV):
        qz = _qz(q, g, nq)
        qx = jnp.concatenate([qz, qaux_ref[g]], axis=1)
        slope = slope_ref[g]
        rows = slice(g * NSA_HD, (g + 1) * NSA_HD)
        o_c, p_c = _compressed_branch(qz, kcb_ref[0], vcbt_ref[0, rows, :], qpos, slope, ncb)
        imp = p_c[:, 0:nq]
        for r in range(1, NSA_REP):
            imp = imp + p_c[:, r * nq:(r + 1) * nq]
        imp_sel = imp[0:nsp] + imp[nsp:ncb]
        sel = _select_blocks(imp_sel, qpos[:, 0:nq], nsp)
        sel_ref[g] = jnp.concatenate([jnp.where(sel > 0.0, 0.0, NEG)] * NSA_REP, axis=1)
        blk = lax.broadcasted_iota(jnp.int32, (nsp, 1), 0)
        row_any = jnp.max(sel, axis=1, keepdims=True)
        first = jnp.min(jnp.where(row_any > 0.0, jnp.where(blk >= 2, blk, 2 * nsp), 2 * nsp), axis=0, keepdims=True)
        kt_lo = jnp.minimum(first[0, 0] >> 1, i)

        def tile(k_ref, vt_ref, kt, mode, with_sel):
            off = kt * _KT if isinstance(kt, int) else pl.multiple_of(kt * _KT, _KT)
            kaux =jnp.where(lane == 0, (kt - i).astype(F32), key_lane).astype(BF16)
            kx = jnp.concatenate([k_ref[0, pl.ds(off, _KT), :], kaux], axis=1)
            s = lax.dot_general(kx, qx, _NT, preferred_element_type=F32)
            if with_sel:
                half = _KT // 2
                s = jnp.concatenate([s[0:half] + sel_ref[g, pl.ds(2 * kt, 1), :],
                                     s[half:_KT] + sel_ref[g, pl.ds(2 * kt + 1, 1), :]], axis=0)
            if mode == "causal":
                s = jnp.where(causal, s, NEG)
            elif mode == "anti":
                s = jnp.where(anti, s, NEG)
            m_old = m_ref[g]
            m_new = jnp.maximum(m_old, jnp.max(s, axis=0, keepdims=True))
            alpha = jnp.exp(m_old - m_new)
            p = jnp.exp(s - m_new)
            l_ref[g] = alpha * l_ref[g] + jnp.sum(p, axis=0, keepdims=True)
            vt = vt_ref[0, rows, pl.ds(off, _KT)]
            acc_ref[g] = alpha * acc_ref[g] + jnp.dot(vt, p.astype(BF16), preferred_element_type=F32)
            m_ref[g] = m_new

        m_init()

        @pl.when(i > 0)
        def _():
            tile(ks_ref, vst_ref, 0, "none", True)

        def sel_body(kt, carry):
            tile(ks_ref, vst_ref, kt, "none", True)
            return carry

        lax.fori_loop(jnp.maximum(kt_lo, 1), i, sel_body, 0)
        tile(ks_ref, vst_ref, i, "causal", True)
        o_s = _softmax_done(l_ref, acc_ref, g)

        m_init()

        @pl.when(i >= wt)
        def _():
            tile(kw_ref, vwt_ref, i - wt, "anti", False)

        def win_body(kt, carry):
            tile(kw_ref, vwt_ref, kt, "none", False)
            return carry

        lax.fori_loop(jnp.maximum(i - wt + 1, 0), i, win_body, 0)
        tile(kw_ref, vwt_ref, i, "causal", False)
        o_w = _softmax_done(l_ref, acc_ref, g)

        gc = _sigmoid(glog_ref[0, 0, g:g + 1, :])
        gs = _sigmoid(glog_ref[0, 0, 2 + g:3 + g, :])
        gw = _sigmoid(glog_ref[0, 0, 4 + g:5 + g, :])
        o_ref[0, 0, g] = gc * o_c + gs * o_s + gw * o_w


_TS = 256


def _pos_lanes(nkeys, tile_off):
    lane = lax.broadcasted_iota(jnp.int32, (nkeys, NSA_KW), 1)
    key = lax.broadcasted_iota(jnp.int32, (nkeys, NSA_KW), 0)
    hi = (tile_off + (key >> 7)).astype(F32)
    lo = (key & (_KT - 1)).astype(F32)
    return jnp.where(lane == 0, hi, jnp.where(lane == 1, lo, 0.0)).astype(BF16)


def _nsa_prompt_kernel2(q_ref, glog_ref, kcb_ref, vcbt_ref, ks_ref, vst_ref, kw_ref, vwt_ref,
                        qoff_ref, slope_ref, qaux_ref, o_ref, sel_ref, m_ref, l_ref, acc_ref, *, nq, ncb):
    i = pl.program_id(1)
    st0 = i * nq
    qoff = qoff_ref[...]
    qpos = st0 + qoff
    q = q_ref[0] * (NSA_HD ** -0.5)
    nsp = ncb // 2
    bpt = _TS // L_SEL
    idiag = st0 // _TS
    groups = range(NSA_KV)
    rows = [slice(g * NSA_HD, (g + 1) * NSA_HD) for g in groups]

    qxs, o_cs, firsts = [], [], []
    for g in groups:
        qz = _qz(q, g, nq)
        qxs.append(jnp.concatenate([qz, qaux_ref[g]], axis=1))
        o_c, p_c = _compressed_branch(qz, kcb_ref[0], vcbt_ref[0, rows[g], :], qpos, slope_ref[g], ncb)
        o_cs.append(o_c)
        imp = p_c[:, 0:nq]
        for r in range(1, NSA_REP):
            imp = imp + p_c[:, r * nq:(r + 1) * nq]
        sel = _select_blocks(imp[0:nsp] + imp[nsp:ncb], qpos[:, 0:nq], nsp)
        sel_ref[g] = jnp.concatenate([jnp.where(sel > 0.0, 0.0, NEG)] * NSA_REP, axis=1)
        blk = lax.broadcasted_iota(jnp.int32, (nsp, 1), 0)
        row_any = jnp.max(sel, axis=1, keepdims=True)
        first = jnp.min(jnp.where(row_any > 0.0, jnp.where(blk >= bpt, blk, nsp * bpt), nsp * bpt),
                        axis=0, keepdims=True)
        firsts.append(first[0, 0])
    lo = jnp.minimum(jnp.minimum(firsts[0], firsts[1]) // bpt, idiag)

    m_ref[...] = jnp.full(m_ref.shape, _M0, F32)
    l_ref[...] = jnp.zeros(l_ref.shape, F32)
    acc_ref[...] = jnp.zeros(acc_ref.shape, F32)

    def sel_tile(kt, diag):
        off = kt * _TS if isinstance(kt, int) else pl.multiple_of(kt * _TS, _TS)
        kx = jnp.concatenate([ks_ref[0, pl.ds(off, _TS), :], _pos_lanes(_TS, kt * (_TS // _KT) - i)], axis=1)
        if diag:
            visible = (off + lax.broadcasted_iota(jnp.int32, (_TS, 1), 0)) <= qpos
        for g in groups:
            s = lax.dot_general(kx, qxs[g], _NT, preferred_element_type=F32)
            s = jnp.concatenate([s[b * L_SEL:(b + 1) * L_SEL] + sel_ref[g, pl.ds(bpt * kt + b, 1), :]
                                 for b in range(bpt)], axis=0)
            if diag:
                s = jnp.where(visible, s, NEG)
            m_old = m_ref[g]
            m_new = jnp.maximum(m_old, jnp.max(s, axis=0, keepdims=True))
            alpha = jnp.exp(m_old - m_new)
            p = jnp.exp(s - m_new)
            l_ref[g] = alpha * l_ref[g] + jnp.sum(p, axis=0, keepdims=True)
            vt = vst_ref[0, rows[g], pl.ds(off, _TS)]
            acc_ref[g] = alpha * acc_ref[g] + jnp.dot(vt, p.astype(BF16), preferred_element_type=F32)
            m_ref[g] = m_new

    @pl.when(idiag > 0)
    def _():
        sel_tile(0, False)

    def sel_body(kt, carry):
        sel_tile(kt, False)
        return carry

    lax.fori_loop(lo, idiag, sel_body, 0)
    sel_tile(idiag, True)

    wk = WINDOW + nq
    wstart = pl.multiple_of(jnp.maximum(i - WINDOW // nq, 0) * nq, nq)
    kxw = jnp.concatenate([kw_ref[0, pl.ds(wstart, wk), :], _pos_lanes(wk, wstart // _KT - i)], axis=1)
    d_w = qpos - (wstart + lax.broadcasted_iota(jnp.int32, (wk, 1), 0))
    in_window = jnp.where(d_w >= 0, d_w, WINDOW + 1) <= WINDOW
    for g in groups:
        s = jnp.where(in_window, lax.dot_general(kxw, qxs[g], _NT, preferred_element_type=F32), NEG)
        m = jnp.max(s, axis=0, keepdims=True)
        p = jnp.exp(s - m)
        l = jnp.sum(p, axis=0, keepdims=True)
        o_w = jnp.dot(vwt_ref[0, rows[g], pl.ds(wstart, wk)], p.astype(BF16), preferred_element_type=F32) / l
        o_s = _softmax_done(l_ref, acc_ref, g)
        gc = _sigmoid(glog_ref[0, 0, g:g + 1, :])
        gs = _sigmoid(glog_ref[0, 0, 2 + g:3 + g, :])
        gw = _sigmoid(glog_ref[0, 0, 4 + g:5 + g, :])
        o_ref[0, 0, g] = gc * o_cs[g] + gs * o_s + gw * o_w


def _nsa_cols(nq):
    c = NSA_REP * nq
    qoff = (jnp.arange(c, dtype=jnp.int32) % nq).reshape(1, c)
    slopes = 2.0 ** (-8.0 * jnp.arange(1, NSA_HEADS + 1, dtype=F32) / NSA_HEADS)
    slope = jnp.repeat(slopes.reshape(NSA_KV, NSA_REP), nq, axis=1).reshape(NSA_KV, 1, c)
    return qoff, slope


def _gate_cols(glog, nq):
    b, t, _ = glog.shape
    x = glog.reshape(b, t // nq, nq, NSA_KV, NSA_REP, 3)
    return x.transpose(0, 1, 5, 3, 4, 2).reshape(b, t // nq, 3 * NSA_KV, NSA_REP * nq)


def _even_odd(x):
    return jnp.concatenate([x[:, 0::2], x[:, 1::2]], axis=1)


def _uncols(o, nq):
    b, nb = o.shape[:2]
    x = o.reshape(b, nb, NSA_KV, NSA_HD, NSA_REP, nq)
    return x.transpose(0, 1, 5, 2, 4, 3).reshape(b, nb * nq, NSA_W)


def nsa_prompt(q, glog, kcb, vcb, ks, vs, kw, vw, *, nq=128):
    b, t, _ = q.shape
    ncb = kcb.shape[1]
    c = NSA_REP * nq
    nblk = t // nq
    assert nq == _KT and t % _TS == 0 and t >= WINDOW + nq and ncb == t // L_CMP
    qoff, slope = _nsa_cols(nq)
    lane = jnp.arange(NSA_KW)[None, None, :]
    slope_col = slope.reshape(NSA_KV, c, 1)
    qaux = jnp.where(lane == 0, slope_col * _KT, jnp.where(lane == 1, slope_col, 0.0)).astype(BF16)
    kcb_p = _even_odd(kcb).astype(BF16)
    vcbt = _even_odd(vcb).transpose(0, 2, 1).astype(BF16)
    per_b = lambda shape: pl.BlockSpec((1,) + shape, lambda i, j: (i, 0, 0))
    out = pl.pallas_call(
        functools.partial(_nsa_prompt_kernel2, nq=nq, ncb=ncb),
        grid=(b, nblk),
        in_specs=[pl.BlockSpec((1, nq, NSA_W), lambda i, j: (i, j, 0)),
                  pl.BlockSpec((1, 1, 3 * NSA_KV, c), lambda i, j: (i, j, 0, 0)),
                  per_b((ncb, NSA_KW)), per_b((NSA_KW, ncb)),
                  per_b((t, NSA_KW)), per_b((NSA_KW, t)), per_b((t, NSA_KW)), per_b((NSA_KW, t)),
                  pl.BlockSpec((1, c), lambda i, j: (0, 0)),
                  pl.BlockSpec((NSA_KV, 1, c), lambda i, j: (0, 0, 0)),
                  pl.BlockSpec((NSA_KV, c, NSA_KW), lambda i, j: (0, 0, 0))],
        out_specs=pl.BlockSpec((1, 1, NSA_KV, NSA_HD, c), lambda i, j: (i, j, 0, 0, 0)),
        out_shape=jax.ShapeDtypeStruct((b, nblk, NSA_KV, NSA_HD, c), F32),
        scratch_shapes=[pltpu.VMEM((NSA_KV, ncb // 2, c), F32),
                        pltpu.VMEM((NSA_KV, 1, c), F32), pltpu.VMEM((NSA_KV, 1, c), F32),
                        pltpu.VMEM((NSA_KV, NSA_HD, c), F32)],
        compiler_params=_params("parallel", "arbitrary"),
        name="nsa_prompt",
    )(q, _gate_cols(glog, nq), kcb_p, vcbt, ks.astype(BF16), vs.transpose(0, 2, 1).astype(BF16),
      kw.astype(BF16), vw.transpose(0, 2, 1).astype(BF16), qoff, slope, qaux)
    return _uncols(out, nq)


def _nsa_sample_kernel(pt_ref, q_ref, glog_ref, kcb_ref, vcbt_ref, *refs, nq, ncb, npages, page, past, wb):
    ks_pages = refs[0:npages]
    vs_pages = refs[npages:2 * npages]
    (nks_ref, nvs_ref, nkw_ref, nvw_ref, wk_ref, wv_ref, qoff_ref, slope_ref, rmat_ref,
     o_ref, wko_ref, wvo_ref, m_ref, l_ref, acc_ref) = refs[2 * npages:]
    del pt_ref
    c = NSA_REP * nq
    qpos = past + qoff_ref[...]
    q = q_ref[0] * (NSA_HD ** -0.5)
    key_io = lax.broadcasted_iota(jnp.int32, (_KT, 1), 0)
    nsp = sel_rows = -(-(past + nq) // L_SEL)
    nsp = -(-nsp // 8) * 8
    pad_rows = lambda x: jnp.concatenate([x, jnp.zeros((_KT - nq, x.shape[1]), F32)], axis=0)

    qzs, slopes, sels, o_cs = [], [], [], []
    for g in range(NSA_KV):
        qz = _qz(q, g, nq)
        slope = slope_ref[g]
        o_c, p_c = _compressed_branch(qz, kcb_ref[0], vcbt_ref[0, g * NSA_HD:(g + 1) * NSA_HD, :], qpos, slope, ncb)
        imp = _dot_exact_rhs(p_c, rmat_ref[...])
        imp_sel = imp[0:ncb // 2] + imp[ncb // 2:ncb]
        imp_sel = jnp.concatenate([imp_sel, jnp.zeros((nsp - ncb // 2, c), F32)], axis=0)
        qzs.append(qz)
        slopes.append(slope)
        sels.append(_select_blocks(imp_sel, qpos, nsp))
        o_cs.append(o_c)
    del sel_rows

    qgs = [qzs[g][:, g * NSA_HD:(g + 1) * NSA_HD] for g in range(NSA_KV)]

    def run_tile(kv_fn, tok0, mask_fn):
        dpos = qpos - (tok0 + key_io)
        for g in range(NSA_KV):
            k, v = kv_fn(g)
            st = lax.dot_general(k.astype(BF16), qgs[g], _NT, preferred_element_type=F32)
            maskf = mask_fn(g, dpos)
            s = st - slopes[g] * dpos.astype(F32)
            sm = jnp.where(maskf > 0.0, s, NEG)
            m_old = m_ref[g]
            m_new = jnp.maximum(m_old, jnp.max(sm, axis=0, keepdims=True))
            alpha = jnp.exp(m_old - m_new)
            p = jnp.exp(sm - m_new) * maskf
            l_ref[g] = alpha * l_ref[g] + jnp.sum(p, axis=0, keepdims=True)
            pv = lax.dot_general(v.astype(BF16), p.astype(BF16), _TN, preferred_element_type=F32)
            acc_ref[g] = alpha * acc_ref[g] + pv
            m_ref[g] = m_new

    cache_tile = lambda k_ref, v_ref, lo: (lambda g: (k_ref[0, 0, lo:lo + _KT, g, :], v_ref[0, 0, lo:lo + _KT, g, :]))
    new_tile = lambda k_ref, v_ref: (lambda g: (pad_rows(k_ref[0, :, g * NSA_HD:(g + 1) * NSA_HD]),
                                                pad_rows(v_ref[0, :, g * NSA_HD:(g + 1) * NSA_HD])))

    _softmax_init(m_ref, l_ref, acc_ref)
    for j in range(npages + 1):
        blk0 = j * (page // L_SEL)

        def sel_mask(g, dpos, blk0=blk0):
            selrow = jnp.where(key_io < L_SEL, sels[g][blk0:blk0 + 1, :], sels[g][blk0 + 1:blk0 + 2, :])
            return jnp.where(dpos >= 0, selrow, 0.0)

        if j < npages:
            run_tile(cache_tile(ks_pages[j], vs_pages[j], 0), j * page, sel_mask)
        else:
            run_tile(new_tile(nks_ref, nvs_ref), past, sel_mask)
    o_ss = [_softmax_done(l_ref, acc_ref, g) for g in range(NSA_KV)]

    _softmax_init(m_ref, l_ref, acc_ref)
    win_mask = lambda g, dpos: jnp.where(dpos >= 0, jnp.where(dpos <= WINDOW, 1.0, 0.0), 0.0)
    for j in range(wb // _KT):
        run_tile(cache_tile(wk_ref, wv_ref, j * _KT), past - wb + j * _KT, win_mask)
    run_tile(new_tile(nkw_ref, nvw_ref), past, win_mask)
    for g in range(NSA_KV):
        o_w = _softmax_done(l_ref, acc_ref, g)
        gc = _sigmoid(glog_ref[0, 0, g:g + 1, :])
        gs = _sigmoid(glog_ref[0, 0, 2 + g:3 + g, :])
        gw = _sigmoid(glog_ref[0, 0, 4 + g:5 + g, :])
        o_ref[0, 0, g] = gc * o_cs[g] + gs * o_ss[g] + gw * o_w

    wko_ref[0, 0:wb - nq] = wk_ref[0, 0, nq:wb]
    wvo_ref[0, 0:wb - nq] = wv_ref[0, 0, nq:wb]
    for g in range(NSA_KV):
        wko_ref[0, wb - nq:wb, g, :] = nkw_ref[0, :, g * NSA_HD:(g + 1) * NSA_HD]
        wvo_ref[0, wb - nq:wb, g, :] = nvw_ref[0, :, g * NSA_HD:(g + 1) * NSA_HD]


def nsa_sample(q, glog, kcb, vcb, pool_k, pool_v, page_table, nks, nvs, nkw, nvw, win_k, win_v, layer):
    b, nq, _ = q.shape
    ncb = kcb.shape[1]
    npages = page_table.shape[1]
    page = pool_k.shape[2]
    past = npages * page
    wb = win_k.shape[2]
    assert page == _KT and wb % _KT == 0 and nq % 8 == 0 and nq <= L_SEL and ncb % 2 == 0
    c = NSA_REP * nq
    qoff, slope = _nsa_cols(nq)
    col = jnp.arange(c, dtype=jnp.int32)
    rmat = (col[:, None] % nq == col[None, :] % nq).astype(BF16)
    kcb_p = _even_odd(kcb).astype(BF16)
    vcbt = _even_odd(vcb).transpose(0, 2, 1).astype(BF16)
    per_b = lambda shape: pl.BlockSpec((1,) + shape, lambda i, pt: (i,) + (0,) * len(shape))
    const = lambda shape: pl.BlockSpec(shape, lambda i, pt: (0,) * len(shape))
    page_spec = lambda j: pl.BlockSpec((1, 1, page, NSA_KV, NSA_HD), lambda i, pt: (layer, pt[i, j], 0, 0, 0))
    win_spec = pl.BlockSpec((1, 1, wb, NSA_KV, NSA_HD), lambda i, pt: (layer, i, 0, 0, 0))
    in_specs = ([per_b((nq, NSA_W)), per_b((1, 3 * NSA_KV, c)), per_b((ncb, NSA_KW)), per_b((NSA_KW, ncb))]
                + [page_spec(j) for j in range(npages)] * 2
                + [per_b((nq, NSA_KW))] * 4 + [win_spec] * 2
                + [const((1, c)), const((NSA_KV, 1, c)), const((c, c))])
    out, wko, wvo = pl.pallas_call(
        functools.partial(_nsa_sample_kernel, nq=nq, ncb=ncb, npages=npages, page=page, past=past, wb=wb),
        grid_spec=pltpu.PrefetchScalarGridSpec(
            num_scalar_prefetch=1,
            grid=(b,),
            in_specs=in_specs,
            out_specs=[per_b((1, NSA_KV, NSA_HD, c)), per_b((wb, NSA_KV, NSA_HD)), per_b((wb, NSA_KV, NSA_HD))],
            scratch_shapes=[pltpu.VMEM((NSA_KV, 1, c), F32), pltpu.VMEM((NSA_KV, 1, c), F32),
                            pltpu.VMEM((NSA_KV, NSA_HD, c), F32)],
        ),
        out_shape=[jax.ShapeDtypeStruct((b, 1, NSA_KV, NSA_HD, c), F32),
                   jax.ShapeDtypeStruct((b, wb, NSA_KV, NSA_HD), F32),
                   jax.ShapeDtypeStruct((b, wb, NSA_KV, NSA_HD), F32)],
        compiler_params=_params("arbitrary"),
        name="nsa_sample",
    )(page_table, q, _gate_cols(glog, nq), kcb_p, vcbt, *([pool_k] * npages), *([pool_v] * npages),
      nks, nvs, nkw, nvw, win_k, win_v, qoff, slope, rmat)
    return _uncols(out, nq), wko, wvo


def _compress_pages_kernel(x_ref, pe_ref, w_ref, o_ref):
    for g in range(NSA_KV):
        acc = jnp.zeros(o_ref.shape[1:], F32)
        for d in range(0, NSA_HD, 2):
            y = jnp.concatenate([x_ref[0, :, g, d, :] + pe_ref[d:d + 1, :],
                                 x_ref[0, :, g, d + 1, :] + pe_ref[d + 1:d + 2, :]], axis=1)
            acc = acc + jnp.dot(y.astype(BF16), w_ref[d // 2], preferred_element_type=F32)
        o_ref[g] = acc


def nsa_compress_pages(cache_t, layer, pe, w, *, tp=256):
    n_pool, page = cache_t.shape[1], cache_t.shape[4]
    nblk = page // L_CMP
    tp = _row_tile(n_pool, tp)
    pe_t = jnp.tile(pe.T, (1, nblk))
    eye = jnp.eye(nblk, dtype=F32)
    wd = jnp.einsum('nm,lde->dnlme', eye, w).reshape(NSA_HD // 2, 2 * page, nblk * NSA_HD).astype(BF16)
    out = pl.pallas_call(
        _compress_pages_kernel,
        grid=(n_pool // tp,),
        in_specs=[pl.BlockSpec((1, tp, NSA_KV, NSA_HD, page), lambda i: (layer, i, 0, 0, 0)),
                  pl.BlockSpec((NSA_HD, page), lambda i: (0, 0)),
                  pl.BlockSpec((NSA_HD // 2, 2 * page, nblk * NSA_HD), lambda i: (0, 0, 0))],
        out_specs=pl.BlockSpec((NSA_KV, tp, nblk * NSA_HD), lambda i: (0, i, 0)),
        out_shape=jax.ShapeDtypeStruct((NSA_KV, n_pool, nblk * NSA_HD), F32),
        compiler_params=_params("parallel"),
        name="nsa_compress_pages",
    )(cache_t, pe_t, wd)
    return out.reshape(NSA_KV, n_pool, nblk, NSA_HD).transpose(1, 2, 0, 3).reshape(n_pool, nblk, NSA_KW)


def _row_softmax(s, mask):
    sm = jnp.where(mask, s, NEG)
    m = jnp.max(sm, axis=1, keepdims=True)
    p = jnp.where(mask, jnp.exp(sm - m), 0.0)
    l = jnp.sum(p, axis=1, keepdims=True)
    return p * jnp.where(l > 0.0, 1.0 / l, 0.0)


def _nsa_decode_kernel(pt_ref, q_ref, gl_ref, kcb_ref, vcb_ref, *refs, nq, ncb, npages, page, past, wb):
    ks_pages = refs[0:npages]
    vs_pages = refs[npages:2 * npages]
    (nks_ref, nvs_ref, nkw_ref, nvw_ref, wk_ref, wv_ref, slope_ref, expand_ref,
     o_ref, wko_ref, wvo_ref) = refs[2 * npages:]
    del pt_ref
    c = NSA_REP * nq
    nblk_lanes = _KT
    q = q_ref[0] * (NSA_HD ** -0.5)
    row = lax.broadcasted_iota(jnp.int32, (c, 1), 0)
    qpos = past + (row & (nq - 1))
    qposf = qpos.astype(F32)
    pad_rows = lambda x: jnp.concatenate([x, jnp.zeros((_KT - nq, x.shape[1]), F32)], axis=0)
    lane = lax.broadcasted_iota(jnp.int32, (1, _KT), 1)
    n_sel_keys = (npages + 1) * page
    key_all = lax.broadcasted_iota(jnp.int32, (1, n_sel_keys), 1)
    expand = expand_ref[...]
    pr = lax.broadcasted_iota(jnp.int32, (ncb, nblk_lanes), 0)
    pc = lax.broadcasted_iota(jnp.int32, (ncb, nblk_lanes), 1)
    pair = jnp.where(jnp.where(pr < ncb // 2, pr, pr - ncb // 2) == pc, 1.0, 0.0).astype(BF16)
    new_kt = pad_rows(nkw_ref[0]).T
    new_vt = pad_rows(nvw_ref[0]).T

    for g in range(NSA_KV):
        gs = slice(g * NSA_HD, (g + 1) * NSA_HD)
        qg = jnp.concatenate([q[:, (g * NSA_REP + r) * NSA_HD:(g * NSA_REP + r + 1) * NSA_HD]
                              for r in range(NSA_REP)], axis=0).astype(BF16)
        slope = slope_ref[g]
        s_c = lax.dot_general(qg, kcb_ref[0, :, gs], _NT, preferred_element_type=F32)
        cl = lax.broadcasted_iota(jnp.int32, (1, ncb), 1)
        half = ncb // 2
        c_pos = jnp.where(cl < half, 2 * cl, 2 * (cl - half) + 1) * L_CMP + (L_CMP - 1)
        d_c = qpos - c_pos
        p_c = _row_softmax(s_c - slope * d_c.astype(F32), d_c >= 0)
        o_c = jnp.dot(p_c.astype(BF16), vcb_ref[0, :, gs], preferred_element_type=F32)
        imp = p_c[0:nq]
        for r in range(1, NSA_REP):
            imp = imp + p_c[r * nq:(r + 1) * nq]
        imp_sel = _dot_exact_rhs(imp, pair)
        cur = (past + lax.broadcasted_iota(jnp.int32, (nq, 1), 0)) >> 6
        valid = lane <= cur
        forced = jnp.where(valid, jnp.where(lane == 0, 1.0, jnp.where(lane >= cur - 1, 1.0, 0.0)), 0.0)
        score = jnp.where(forced > 0.0, _BIG, jnp.where(valid, imp_sel, -1.0))
        sel = jnp.zeros((nq, nblk_lanes), F32)
        for _ in range(N_SEL):
            m = jnp.max(score, axis=1, keepdims=True)
            idx = jnp.min(jnp.where(score == m, lane, nblk_lanes + 1), axis=1, keepdims=True)
            pick = lane == idx
            sel = jnp.where(pick, 1.0, sel)
            score = jnp.where(pick, -2.0, score)
        sel = jnp.where(valid, sel, 0.0)
        sel_keys = jnp.dot(sel.astype(BF16), expand, preferred_element_type=F32)
        sel_keys = jnp.concatenate([sel_keys] * NSA_REP, axis=0)

        nk = pad_rows(nks_ref[0, :, gs]).astype(BF16)
        nv = pad_rows(nvs_ref[0, :, gs]).astype(BF16)
        s_parts = [jnp.dot(qg, ks_pages[j][0, 0, g].astype(BF16), preferred_element_type=F32) for j in range(npages)]
        s_parts.append(lax.dot_general(qg, nk, _NT, preferred_element_type=F32))
        s_s = jnp.concatenate(s_parts, axis=1)
        d_s = qpos - key_all
        p_s = _row_softmax(s_s - slope * d_s.astype(F32), jnp.where(d_s >= 0, sel_keys, 0.0) > 0.0).astype(BF16)
        o_s = jnp.dot(p_s[:, npages * page:], nv, preferred_element_type=F32)
        for j in range(npages):
            o_s = o_s + lax.dot_general(p_s[:, j * page:(j + 1) * page], vs_pages[j][0, 0, g].astype(BF16), _NT,
                                        preferred_element_type=F32)

        nkw = pad_rows(nkw_ref[0, :, gs]).astype(BF16)
        nvw = pad_rows(nvw_ref[0, :, gs]).astype(BF16)
        s_w = jnp.concatenate([jnp.dot(qg, wk_ref[0, 0, g].astype(BF16), preferred_element_type=F32),
                               lax.dot_general(qg, nkw, _NT, preferred_element_type=F32)], axis=1)
        w_pos = past - wb + lax.broadcasted_iota(jnp.int32, (1, wb + _KT), 1)
        d_w = qpos - w_pos
        p_w = _row_softmax(s_w - slope * d_w.astype(F32), jnp.where(d_w >= 0, d_w, WINDOW + 1) <= WINDOW).astype(BF16)
        o_w = (lax.dot_general(p_w[:, 0:wb], wv_ref[0, 0, g].astype(BF16), _NT, preferred_element_type=F32)
               + jnp.dot(p_w[:, wb:], nvw, preferred_element_type=F32))

        gate = _sigmoid(gl_ref[0, g])
        o_ref[0, g] = gate[:, 0:1] * o_c + gate[:, 1:2] * o_s + gate[:, 2:3] * o_w

        for src_ref, new_full, dst_ref in ((wk_ref, new_kt, wko_ref), (wv_ref, new_vt, wvo_ref)):
            new_t = pltpu.roll(new_full[gs, :], _KT - nq, axis=1)
            shifted = pltpu.roll(src_ref[0, 0, g], wb - nq, axis=1)
            dst_ref[0, g, :, 0:wb - _KT] = shifted[:, 0:wb - _KT]
            dst_ref[0, g, :, wb - _KT:wb] = jnp.where(lane >= _KT - nq, new_t, shifted[:, wb - _KT:wb])


_DECODE_BB = 4


def _nsa_decode_kernel2(pt_ref, q_ref, gl_ref, kcb_ref, vcb_ref, *refs, nq, ncb, npages, page, past, wb, nbb):
    ks_pages = refs[0:nbb * npages]
    vs_pages = refs[nbb * npages:2 * nbb * npages]
    (nks_ref, nvs_ref, nkw_ref, nvw_ref, wk_ref, wv_ref, slope_ref, expand_ref,
     o_ref, wko_ref, wvo_ref) = refs[2 * nbb * npages:]
    del pt_ref
    c = NSA_REP * nq
    nblk_lanes = _KT
    n_sel = -(-(past + nq) // L_SEL)
    row = lax.broadcasted_iota(jnp.int32, (c, 1), 0)
    qpos = past + (row & (nq - 1))
    pad_rows = lambda x: jnp.concatenate([x, jnp.zeros((_KT - nq, x.shape[1]), F32)], axis=0)
    lane = lax.broadcasted_iota(jnp.int32, (1, _KT), 1)
    n_sel_keys = (npages + 1) * page
    key_all = lax.broadcasted_iota(jnp.int32, (1, n_sel_keys), 1)
    expand = expand_ref[...]
    pr = lax.broadcasted_iota(jnp.int32, (ncb, nblk_lanes), 0)
    pc = lax.broadcasted_iota(jnp.int32, (ncb, nblk_lanes), 1)
    half = ncb // 2
    pair = jnp.where(jnp.where(pr < half, pr, pr - half) == pc, 1.0, 0.0).astype(BF16)
    cl = lax.broadcasted_iota(jnp.int32, (1, ncb), 1)
    c_pos = jnp.where(cl < half, 2 * cl, 2 * (cl - half) + 1) * L_CMP + (L_CMP - 1)
    d_c = qpos - c_pos
    d_cf = d_c.astype(F32)
    d_s = qpos - key_all
    d_sf = d_s.astype(F32)
    d_w = qpos - (past - wb + lax.broadcasted_iota(jnp.int32, (1, wb + _KT), 1))
    d_wf = d_w.astype(F32)
    in_window = jnp.where(d_w >= 0, d_w, WINDOW + 1) <= WINDOW
    cur = (past + lax.broadcasted_iota(jnp.int32, (nq, 1), 0)) >> 6
    valid = lane <= cur
    forced = jnp.where(valid, jnp.where(lane == 0, 1.0, jnp.where(lane >= cur - 1, 1.0, 0.0)), 0.0)

    for bb in range(nbb):
        q = q_ref[bb] * (NSA_HD ** -0.5)
        new_kt = pad_rows(nkw_ref[bb]).T
        new_vt = pad_rows(nvw_ref[bb]).T
        for g in range(NSA_KV):
            gs = slice(g * NSA_HD, (g + 1) * NSA_HD)
            qg = jnp.concatenate([q[:, (g * NSA_REP + r) * NSA_HD:(g * NSA_REP + r + 1) * NSA_HD]
                                  for r in range(NSA_REP)], axis=0).astype(BF16)
            slope = slope_ref[g]
            s_c = lax.dot_general(qg, kcb_ref[bb, :, gs], _NT, preferred_element_type=F32)
            p_c = _row_softmax(s_c - slope * d_cf, d_c >= 0)
            o_c = jnp.dot(p_c.astype(BF16), vcb_ref[bb, :, gs], preferred_element_type=F32)
            imp = p_c[0:nq]
            for r in range(1, NSA_REP):
                imp = imp + p_c[r * nq:(r + 1) * nq]
            imp_sel = _dot_exact_rhs(imp, pair)
            score = jnp.where(forced > 0.0, _BIG, jnp.where(valid, imp_sel, -1.0))
            before = jnp.zeros((nq, nblk_lanes), F32)
            for bi in range(n_sel):
                sb = score[:, bi:bi + 1]
                before = before + jnp.where(sb > score, 1.0, jnp.where(sb == score, jnp.where(lane > bi, 1.0, 0.0), 0.0))
            sel = jnp.where(valid, jnp.where(before < N_SEL, 1.0, 0.0), 0.0)
            sel_keys = jnp.dot(sel.astype(BF16), expand, preferred_element_type=F32)
            sel_keys = jnp.concatenate([sel_keys] * NSA_REP, axis=0)

            nk = pad_rows(nks_ref[bb, :, gs]).astype(BF16)
            nv = pad_rows(nvs_ref[bb, :, gs]).astype(BF16)
            s_parts = [jnp.dot(qg, ks_pages[bb * npages + j][0, 0, g].astype(BF16), preferred_element_type=F32)
                       for j in range(npages)]
            s_parts.append(lax.dot_general(qg, nk, _NT, preferred_element_type=F32))
            s_s = jnp.concatenate(s_parts, axis=1)
            p_s = _row_softmax(s_s - slope * d_sf, jnp.where(d_s >= 0, sel_keys, 0.0) > 0.0).astype(BF16)
            o_s = jnp.dot(p_s[:, npages * page:], nv, preferred_element_type=F32)
            for j in range(npages):
                o_s = o_s + lax.dot_general(p_s[:, j * page:(j + 1) * page],
                                            vs_pages[bb * npages + j][0, 0, g].astype(BF16), _NT,
                                            preferred_element_type=F32)

            nkw = pad_rows(nkw_ref[bb, :, gs]).astype(BF16)
            nvw = pad_rows(nvw_ref[bb, :, gs]).astype(BF16)
            s_w = jnp.concatenate([jnp.dot(qg, wk_ref[0, bb, g].astype(BF16), preferred_element_type=F32),
                                   lax.dot_general(qg, nkw, _NT, preferred_element_type=F32)], axis=1)
            p_w = _row_softmax(s_w - slope * d_wf, in_window).astype(BF16)
            o_w = (lax.dot_general(p_w[:, 0:wb], wv_ref[0, bb, g].astype(BF16), _NT, preferred_element_type=F32)
                   + jnp.dot(p_w[:, wb:], nvw, preferred_element_type=F32))

            gate = _sigmoid(gl_ref[bb, g])
            o_ref[bb, g] = gate[:, 0:1] * o_c + gate[:, 1:2] * o_s + gate[:, 2:3] * o_w

            for src_ref, new_full, dst_ref in ((wk_ref, new_kt, wko_ref), (wv_ref, new_vt, wvo_ref)):
                new_t = pltpu.roll(new_full[gs, :], _KT - nq, axis=1)
                shifted = pltpu.roll(src_ref[0, bb, g], wb - nq, axis=1)
                dst_ref[bb, g, :, 0:wb - _KT] = shifted[:, 0:wb - _KT]
                dst_ref[bb, g, :, wb - _KT:wb] = jnp.where(lane >= _KT - nq, new_t, shifted[:, wb - _KT:wb])


def nsa_decode(q, glog, kcb, vcb, pool_k, pool_v, page_table, nks, nvs, nkw, nvw, win_k, win_v, layer):
    b, nq, _ = q.shape
    ncb = kcb.shape[1]
    npages = page_table.shape[1]
    page = pool_k.shape[4]
    past = npages * page
    wb = win_k.shape[4]
    assert page == _KT and wb % _KT == 0 and nq & (nq - 1) == 0 and nq % 8 == 0 and nq <= L_SEL
    assert ncb % 2 == 0 and ncb // 2 <= _KT and -(-(past + nq) // L_SEL) <= _KT
    c = NSA_REP * nq
    _, slope = _nsa_cols(nq)
    gl = glog.reshape(b, nq, NSA_KV, NSA_REP, 3).transpose(0, 2, 3, 1, 4).reshape(b, NSA_KV, c, 3)
    nbb = _DECODE_BB if b % _DECODE_BB == 0 else 1
    per_b = lambda shape: pl.BlockSpec((nbb,) + shape, lambda i, pt: (i,) + (0,) * len(shape))
    const = lambda shape: pl.BlockSpec(shape, lambda i, pt: (0,) * len(shape))
    page_spec = lambda bb, j: pl.BlockSpec((1, 1, NSA_KV, NSA_HD, page),
                                           lambda i, pt: (layer, pt[i * nbb + bb, j], 0, 0, 0))
    page_specs = [page_spec(bb, j) for bb in range(nbb) for j in range(npages)]
    win_spec = pl.BlockSpec((1, nbb, NSA_KV, NSA_HD, wb), lambda i, pt: (layer, i, 0, 0, 0))
    n_keys = (npages + 1) * page
    expand = (jnp.arange(n_keys)[None, :] // L_SEL == jnp.arange(_KT)[:, None]).astype(BF16)
    in_specs = ([per_b((nq, NSA_W)), per_b((NSA_KV, c, 3)), per_b((ncb, NSA_KW)), per_b((ncb, NSA_KW))]
                + page_specs * 2
                + [per_b((nq, NSA_KW))] * 4 + [win_spec] * 2 + [const((NSA_KV, c, 1)), const((_KT, n_keys))])
    out, wko, wvo = pl.pallas_call(
        functools.partial(_nsa_decode_kernel2, nq=nq, ncb=ncb, npages=npages, page=page, past=past, wb=wb, nbb=nbb),
        grid_spec=pltpu.PrefetchScalarGridSpec(
            num_scalar_prefetch=1,
            grid=(b // nbb,),
            in_specs=in_specs,
            out_specs=[per_b((NSA_KV, c, NSA_HD)), per_b((NSA_KV, NSA_HD, wb)), per_b((NSA_KV, NSA_HD, wb))],
        ),
        out_shape=[jax.ShapeDtypeStruct((b, NSA_KV, c, NSA_HD), F32),
                   jax.ShapeDtypeStruct((b, NSA_KV, NSA_HD, wb), F32),
                   jax.ShapeDtypeStruct((b, NSA_KV, NSA_HD, wb), F32)],
        compiler_params=_params("arbitrary"),
        name="nsa_decode",
    )(page_table, q, gl, _even_odd(kcb).astype(BF16), _even_odd(vcb).astype(BF16),
      *([pool_k] * (nbb * npages)), *([pool_v] * (nbb * npages)), nks, nvs, nkw, nvw, win_k, win_v,
      slope.reshape(NSA_KV, c, 1), expand)
    o = out.reshape(b, NSA_KV, NSA_REP, nq, NSA_HD).transpose(0, 3, 1, 2, 4).reshape(b, nq, NSA_W)
    return o, wko, wvo


def _xattn_cache_kernel(q_ref, k_ref, v_ref, o_ref, *, nq):
    scale = XA_HD ** -0.5
    q = jnp.concatenate([q_ref[0, :, h * XA_HD:(h + 1) * XA_HD] for h in range(XA_HEADS)], axis=0)
    k = k_ref[0, 0].astype(BF16)
    v = v_ref[0, 0].astype(BF16)
    s = lax.dot_general(q.astype(BF16), k, _NT, preferred_element_type=F32) * scale
    col_h = lax.broadcasted_iota(jnp.int32, s.shape, 1) & (XA_HEADS - 1)
    row_h = lax.broadcasted_iota(jnp.int32, s.shape, 0) >> (nq.bit_length() - 1)
    mine = col_h == row_h
    m = jnp.max(jnp.where(mine, s, NEG), axis=1, keepdims=True)
    p = jnp.where(mine, jnp.exp(s - m), 0.0)
    p = p / jnp.sum(p, axis=1, keepdims=True)
    o = jnp.dot(p.astype(BF16), v, preferred_element_type=F32)
    for h in range(XA_HEADS):
        o_ref[0, :, h * XA_HD:(h + 1) * XA_HD] = o[h * nq:(h + 1) * nq]


def xattn_cache(q, cache_k, cache_v, layer):
    b, nq, w = q.shape
    m = cache_k.shape[2]
    assert XA_HEADS & (XA_HEADS - 1) == 0 and nq % 8 == 0
    kv = lambda a: a.reshape(a.shape[0], b, m * XA_HEADS, XA_HD)
    kv_spec = pl.BlockSpec((1, 1, m * XA_HEADS, XA_HD), lambda i: (layer, i, 0, 0))
    return pl.pallas_call(
        functools.partial(_xattn_cache_kernel, nq=nq),
        grid=(b,),
        in_specs=[pl.BlockSpec((1, nq, w), lambda i: (i, 0, 0)), kv_spec, kv_spec],
        out_specs=pl.BlockSpec((1, nq, w), lambda i: (i, 0, 0)),
        out_shape=jax.ShapeDtypeStruct((b, nq, w), F32),
        compiler_params=_params("parallel"),
        name="xattn_cache",
    )(q, kv(cache_k), kv(cache_v))


_HALO_M = 8
_TN = (((0,), (0,)), ((), ()))


def _softplus(x):
    return jnp.maximum(x, 0.0) + jnp.log1p(jnp.exp(-jnp.abs(x)))


def _ssd_kernel(xbc_ref, z_ref, sm_ref, dtt_ref, cs_ref, h0_ref, cw_ref, cb_ref, dtb_ref, dtbt_ref,
                al_ref, alt_ref, dsk_ref, ng_ref, y_ref, ncs_ref, hf_ref, ext_ref, h_ref, yh_ref, *, ql, dt_col):
    c = pl.program_id(1)
    nc = pl.num_programs(1)

    @pl.when(c == 0)
    def _():
        ext_ref[...] = jnp.zeros_like(ext_ref)
        ext_ref[_HALO_M - (M_CONV_W - 1):_HALO_M, :] = cs_ref[0]
        h_ref[...] = h0_ref[0]

    @pl.when(c > 0)
    def _():
        ext_ref[0:_HALO_M, :] = ext_ref[ql:ql + _HALO_M, :]

    ext_ref[_HALO_M:_HALO_M + ql, :] = xbc_ref[0]
    acc = jnp.zeros((ql, M_CONV_DIM), F32)
    for k in range(M_CONV_W):
        off = _HALO_M - (M_CONV_W - 1) + k
        acc = acc + ext_ref[off:off + ql, :].astype(BF16).astype(F32) * cw_ref[k:k + 1, :]
    xbc = _silu(acc + cb_ref[...])
    xs = xbc[:, 0:M_DIN]
    bm = xbc[:, M_DIN:M_DIN + M_GROUPS * M_DSTATE]
    cm = xbc[:, M_DIN + M_GROUPS * M_DSTATE:M_CONV_DIM]

    dt = _softplus(sm_ref[0, :, dt_col:dt_col + M_HEADS] + dtb_ref[...])
    dtt = _softplus(dtt_ref[0] + dtbt_ref[...])
    dta = dt * (-jnp.exp(al_ref[...]))
    dtat = dtt * (-jnp.exp(alt_ref[...]))
    ti = lax.broadcasted_iota(jnp.int32, (ql, ql), 0)
    si = lax.broadcasted_iota(jnp.int32, (ql, ql), 1)
    causal = si <= ti
    cum = _dot_exact_lhs(jnp.where(causal, 1.0, 0.0).astype(BF16), dta)
    cumt = _dot_exact_rhs(dtat, jnp.where(ti <= si, 1.0, 0.0).astype(BF16))
    cum_last = cum[ql - 1:ql, :]
    edec = jnp.exp(cum)
    eend = jnp.exp(cum_last - cum)
    elast = jnp.exp(cum_last)

    rep = M_HEADS // M_GROUPS
    for gi in range(M_GROUPS):
        b_g = bm[:, gi * M_DSTATE:(gi + 1) * M_DSTATE]
        c_g = cm[:, gi * M_DSTATE:(gi + 1) * M_DSTATE].astype(BF16)
        cb = lax.dot_general(c_g, b_g.astype(BF16), _NT, preferred_element_type=F32)
        for hh in range(rep):
            h = gi * rep + hh
            hs = slice(h * M_HDIM, (h + 1) * M_HDIM)
            lmat = jnp.where(causal, jnp.exp(cum[:, h:h + 1] - cumt[h:h + 1, :]), 0.0)
            x_h = xs[:, hs]
            xdt = (x_h * dt[:, h:h + 1]).astype(BF16)
            y_diag = jnp.dot((cb * lmat).astype(BF16), xdt, preferred_element_type=F32)
            h_in = h_ref[h]
            y_off = lax.dot_general(c_g, h_in.astype(BF16), _NT, preferred_element_type=F32) * edec[:, h:h + 1]
            bd = (b_g * eend[:, h:h + 1]).astype(BF16)
            s_chunk = lax.dot_general(xdt, bd, _TN, preferred_element_type=F32)
            h_ref[h] = elast[:, h:h + 1] * h_in + s_chunk
            yh_ref[:, hs] = y_diag + y_off + dsk_ref[:, hs] * x_h

    yz = yh_ref[...] * _silu(z_ref[0])
    y_ref[0] = _rms(yz, ng_ref[...])

    @pl.when(c == nc - 1)
    def _():
        ncs_ref[0] = ext_ref[_HALO_M + ql - (M_CONV_W - 1):_HALO_M + ql, :]
        hf_ref[0] = h_ref[...]


def ssd_mixer(xbc, z, small, dt_col, conv_state, h0, conv_w, conv_b, dt_bias, a_log, d_skip, norm_g, *, ql):
    b, t, _ = xbc.shape
    nc = t // ql
    sw = small.shape[2]
    dtt = small[:, :, dt_col:dt_col + M_HEADS].transpose(0, 2, 1)
    const = lambda shape: pl.BlockSpec(shape, lambda i, j: (0,) * len(shape))
    per_b = lambda shape: pl.BlockSpec((1,) + shape, lambda i, j: (i,) + (0,) * len(shape))
    row = lambda x: x.reshape(1, -1)
    colv = lambda x: x.reshape(-1, 1)
    return pl.pallas_call(
        functools.partial(_ssd_kernel, ql=ql, dt_col=dt_col),
        grid=(b, nc),
        in_specs=[pl.BlockSpec((1, ql, M_CONV_DIM), lambda i, j: (i, j, 0)),
                  pl.BlockSpec((1, ql, M_DIN), lambda i, j: (i, j, 0)),
                  pl.BlockSpec((1, ql, sw), lambda i, j: (i, j, 0)),
                  pl.BlockSpec((1, M_HEADS, ql), lambda i, j: (i, 0, j)),
                  per_b((M_CONV_W - 1, M_CONV_DIM)), per_b((M_HEADS, M_HDIM, M_DSTATE)),
                  const((M_CONV_W, M_CONV_DIM)), const((1, M_CONV_DIM)),
                  const((1, M_HEADS)), const((M_HEADS, 1)), const((1, M_HEADS)), const((M_HEADS, 1)),
                  const((1, M_DIN)), const((1, M_DIN))],
        out_specs=[pl.BlockSpec((1, ql, M_DIN), lambda i, j: (i, j, 0)),
                   per_b((M_CONV_W - 1, M_CONV_DIM)), per_b((M_HEADS, M_HDIM, M_DSTATE))],
        out_shape=[jax.ShapeDtypeStruct((b, t, M_DIN), F32),
                   jax.ShapeDtypeStruct((b, M_CONV_W - 1, M_CONV_DIM), F32),
                   jax.ShapeDtypeStruct((b, M_HEADS, M_HDIM, M_DSTATE), F32)],
        scratch_shapes=[pltpu.VMEM((_HALO_M + ql, M_CONV_DIM), F32),
                        pltpu.VMEM((M_HEADS, M_HDIM, M_DSTATE), F32),
                        pltpu.VMEM((ql, M_DIN), F32)],
        compiler_params=_params("parallel", "arbitrary"),
        name="ssd_mixer",
    )(xbc, z, small, dtt, conv_state, h0, conv_w, row(conv_b), row(dt_bias), colv(dt_bias),
      row(a_log), colv(a_log), row(jnp.repeat(d_skip, M_HDIM)), row(norm_g))


_SMALL_W = 128
_OD_SPLITS = (NSA_W, 6 * NSA_KW, M_DIN, M_CONV_DIM, _SMALL_W)


def _odd_w_in(w):
    o_kv = NSA_W
    o_gate = o_kv + 6 * NSA_KW
    o_z = o_gate + 3 * NSA_HEADS
    o_xbc = o_z + M_DIN
    o_dt = o_xbc + M_CONV_DIM
    pad = jnp.zeros((w.shape[0], _SMALL_W - 3 * NSA_HEADS - M_HEADS), F32)
    return jnp.concatenate([w[:, :o_gate], w[:, o_z:o_xbc], w[:, o_xbc:o_dt],
                            w[:, o_gate:o_z], w[:, o_dt:], pad], axis=1)


def kernel(x_prompt, x_sample, state_conv_a, state_conv_b, cache_cmp_k, cache_cmp_v, cache_sel_k, cache_sel_v, cache_win_k, cache_win_v, state_ssm, state_ssm_conv, cache_mem_k, cache_mem_v, page_table, mem_prompt, norm_mix, norm_xattn, norm_ffn, norm_final, ev_w_in, ev_conv_a, ev_conv_b, ev_conv_b_bias, ev_ln_g, ev_ln_b, ev_w_out, od_w_in, od_cmp_pe, od_cmp_wk, od_cmp_wv, od_ssm_conv_w, od_ssm_conv_b, od_dt_bias, od_a_log, od_d_skip, od_ssm_norm, od_w_out, xa_wq, xa_wk, xa_wv, xa_wo, moe_wg, moe_bg, moe_we, moe_be, moe_w1, moe_w3, moe_w2):
    bp, tp, d = x_prompt.shape
    bs, ts, _ = x_sample.shape
    n_p, n_s = bp * tp, bs * ts
    n_mem = mem_prompt.shape[1]
    depth = norm_mix.shape[0]
    n_pool, page = cache_cmp_k.shape[1:3]
    wb = cache_win_k.shape[2]
    dt_col = 3 * NSA_HEADS

    def groups(a):
        return a[:n_p].reshape(bp, tp, a.shape[-1]), a[n_p:].reshape(bs, ts, a.shape[-1])

    def rows(a_p, a_s):
        return jnp.concatenate([a_p.reshape(n_p, a_p.shape[-1]), a_s.reshape(n_s, a_s.shape[-1])], axis=0)

    h = rows(x_prompt, x_sample)
    out = {k: [] for k in ("ca_p", "ca_s", "cb_p", "cb_s", "wk_p", "wk_s", "wv_p", "wv_s",
                           "sm_p", "sm_s", "sc_p", "sc_s", "mk_p", "mv_p")}
    rows_p = [[], [], [], []]
    rows_s = [[], [], [], []]
    for i in range(depth):
        j = i // 2
        if i % 2 == 0:
            u_p, u_s = groups(norm_matmul(h, norm_mix[i], ev_w_in[j]))
            ev = (ev_conv_a[j], ev_conv_b[j], ev_conv_b_bias[j], ev_ln_g[j], ev_ln_b[j])
            y_p, na_p, nb_p = even_conv(u_p, jnp.zeros((bp, CONV_A_W - 1, D_A), F32),
                                        jnp.zeros((bp, CONV_B_W - 1, D_B), F32), *ev)
            y_s, na_s, nb_s = even_conv(u_s, state_conv_a[j], state_conv_b[j], *ev)
            h = matmul_res([rows(y_p, y_s)], [ev_w_out[j]], h)
            out["ca_p"].append(na_p)
            out["ca_s"].append(na_s)
            out["cb_p"].append(nb_p)
            out["cb_s"].append(nb_s)
        else:
            uq, ukv, uz, uxbc, usm = norm_matmul(h, norm_mix[i], _odd_w_in(od_w_in[j]), splits=_OD_SPLITS)
            q_p, q_s = groups(uq)
            kv_p, kv_s = groups(ukv)
            z_p, z_s = groups(uz)
            xbc_p, xbc_s = groups(uxbc)
            sm_p, sm_s = groups(usm)
            part = lambda a, k: a[:, :, k * NSA_KW:(k + 1) * NSA_KW]
            kvp = [part(kv_p, k) for k in range(6)]
            kvs = [part(kv_s, k) for k in range(6)]
            pe, wck, wcv = od_cmp_pe[j], od_cmp_wk[j], od_cmp_wv[j]
            mw = (od_ssm_conv_w[j], od_ssm_conv_b[j], od_dt_bias[j], od_a_log[j], od_d_skip[j], od_ssm_norm[j])
            blocks = lambda a: a.reshape(-1, L_CMP, NSA_KW)
            ncb = tp // L_CMP
            kcb_p = nsa_compress(blocks(kvp[0][:, :ncb * L_CMP]), pe, wck).reshape(bp, ncb, NSA_KW)
            vcb_p = nsa_compress(blocks(kvp[1][:, :ncb * L_CMP]), pe, wcv).reshape(bp, ncb, NSA_KW)
            o_p = nsa_prompt(q_p, sm_p[:, :, :dt_col], kcb_p, vcb_p, kvp[2], kvp[3], kvp[4], kvp[5])
            keep = min(WINDOW, tp)
            y_p, nsc_p, nsm_p = ssd_mixer(xbc_p, z_p, sm_p, dt_col, jnp.zeros((bp, M_CONV_W - 1, M_CONV_DIM), F32),
                                          jnp.zeros((bp, M_HEADS, M_HDIM, M_DSTATE), F32), *mw, ql=128)
            tokens_last = lambda a: jnp.transpose(a, (0, 1, 3, 4, 2))
            kcp = nsa_compress_pages(tokens_last(cache_cmp_k), j, pe, wck)
            vcp = nsa_compress_pages(tokens_last(cache_cmp_v), j, pe, wcv)
            kcb_s = kcp[page_table].reshape(bs, -1, NSA_KW)
            vcb_s = vcp[page_table].reshape(bs, -1, NSA_KW)
            o_s, nwk_s, nwv_s = nsa_decode(
                q_s, sm_s[:, :, :dt_col], kcb_s, vcb_s, tokens_last(cache_sel_k), tokens_last(cache_sel_v),
                page_table, kvs[2], kvs[3], kvs[4], kvs[5], tokens_last(cache_win_k), tokens_last(cache_win_v), j)
            nwk_s = jnp.transpose(nwk_s, (0, 3, 1, 2))
            nwv_s = jnp.transpose(nwv_s, (0, 3, 1, 2))
            y_s, nsc_s, nsm_s = ssd_mixer(xbc_s, z_s, sm_s, dt_col, state_ssm_conv[j], state_ssm[j], *mw, ql=ts)
            w_out = od_w_out[j]
            h = matmul_res([rows(o_p, o_s), rows(y_p, y_s)], [w_out[:NSA_W], w_out[NSA_W:]], h)
            heads = lambda a: a.reshape(a.shape[0], a.shape[1], NSA_KV, NSA_HD)
            for k in range(4):
                rows_p[k].append(heads(kvp[k]))
                rows_s[k].append(heads(kvs[k]))
            out["wk_p"].append(heads(kvp[4][:, tp - keep:]))
            out["wv_p"].append(heads(kvp[5][:, tp - keep:]))
            out["wk_s"].append(nwk_s)
            out["wv_s"].append(nwv_s)
            out["sc_p"].append(nsc_p)
            out["sc_s"].append(nsc_s)
            out["sm_p"].append(nsm_p)
            out["sm_s"].append(nsm_s)
        mk, mv = norm_matmul(mem_prompt.reshape(bp * n_mem, d), None,
                             jnp.concatenate([xa_wk[i], xa_wv[i]], axis=1), norm=False,
                             splits=(XA_HEADS * XA_HD, XA_HEADS * XA_HD))
        mk = mk.reshape(bp, n_mem, XA_HEADS * XA_HD)
        mv = mv.reshape(bp, n_mem, XA_HEADS * XA_HD)
        out["mk_p"].append(mk.reshape(bp, n_mem, XA_HEADS, XA_HD))
        out["mv_p"].append(mv.reshape(bp, n_mem, XA_HEADS, XA_HD))
        qx_p, qx_s = groups(norm_matmul(h, norm_xattn[i], xa_wq[i]))
        ox_p = xattn(qx_p, mk, mv)
        ox_s = xattn_cache(qx_s, cache_mem_k, cache_mem_v, i)
        h = matmul_res([rows(ox_p, ox_s)], [xa_wo[i]], h)
        h = moe_layer(h, norm_ffn[i], moe_wg[i], moe_bg[i], moe_we[i], moe_be[i], moe_w1, moe_w3, moe_w2, i)
    y = rmsnorm_rows(h, norm_final)
    y_prompt = y[:n_p].reshape(bp, tp, d)
    y_sample = y[n_p:].reshape(bs, ts, d)
    st = lambda k: jnp.stack(out[k])
    return (y_prompt, y_sample, st("ca_p"), st("ca_s"), st("cb_p"), st("cb_s"),
            jnp.stack(rows_p[0]), jnp.stack(rows_s[0]), jnp.stack(rows_p[1]), jnp.stack(rows_s[1]),
            jnp.stack(rows_p[2]), jnp.stack(rows_s[2]), jnp.stack(rows_p[3]), jnp.stack(rows_s[3]),
            st("wk_p"), st("wk_s"), st("wv_p"), st("wv_s"), st("sm_p"), st("sm_s"), st("sc_p"), st("sc_s"),
            st("mk_p"), st("mv_p"))
```

```python
import functools

import jax
import jax.numpy as jnp
from jax import lax
from jax.experimental import pallas as pl
from jax.experimental.pallas import tpu as pltpu

F32 = jnp.float32
BF16 = jnp.bfloat16
EPS = 1e-6
NEG = -1e30
VMEM_LIMIT = 56 * 1024 * 1024

D_A = 512
D_B = 512
CONV_A_W = 3
CONV_B_W = 31
NSA_HEADS = 8
NSA_HD = 64
NSA_KV = 2
NSA_REP = NSA_HEADS // NSA_KV
NSA_W = NSA_HEADS * NSA_HD
NSA_KW = NSA_KV * NSA_HD
L_CMP = 32
L_SEL = 64
N_SEL = 16
WINDOW = 512
M_DIN = 512
M_HDIM = 64
M_HEADS = 8
M_DSTATE = 64
M_GROUPS = 2
M_CONV_W = 4
M_CONV_DIM = M_DIN + 2 * M_GROUPS * M_DSTATE
XA_HEADS = 4
XA_HD = 128
MOE_GROUPS = 4
MOE_EPG = 8
MOE_E = 32
MOE_TOPK = 2


def _params(*sem):
    return pltpu.CompilerParams(dimension_semantics=sem, vmem_limit_bytes=VMEM_LIMIT)


def _row_tile(n, pref):
    t = min(n, pref)
    while n % t or (t % 8 and t != n):
        t -= 1
    return t


def _bdot(a, b):
    return jnp.dot(a.astype(BF16), b.astype(BF16), preferred_element_type=F32)


def _split3(a):
    hi = a.astype(BF16)
    r1 = a - hi.astype(F32)
    mid = r1.astype(BF16)
    lo = (r1 - mid.astype(F32)).astype(BF16)
    return hi, mid, lo


def _dot_exact_rhs(a, b_bf16):
    hi, mid, lo = _split3(a)
    d = lambda x: jnp.dot(x, b_bf16, preferred_element_type=F32)
    return d(hi) + d(mid) + d(lo)


def _dot_exact_lhs(a_bf16, b):
    hi, mid, lo = _split3(b)
    d = lambda x: jnp.dot(a_bf16, x, preferred_element_type=F32)
    return d(hi) + d(mid) + d(lo)


def _rms(x, g):
    ms = jnp.mean(x * x, axis=-1, keepdims=True)
    return x * lax.rsqrt(ms + EPS) * g


def _sigmoid(x):
    return 1.0 / (1.0 + jnp.exp(-x))


def _silu(x):
    return x * _sigmoid(x)


def _norm_matmul_kernel(x_ref, g_ref, w_ref, *o_refs, norm, splits):
    x = x_ref[...]
    if norm:
        x = _rms(x, g_ref[...])
    res = jnp.dot(x.astype(BF16), w_ref[...].astype(BF16), preferred_element_type=F32)
    off = 0
    for o_ref, width in zip(o_refs, splits):
        o_ref[...] = res[:, off:off + width]
        off += width


def norm_matmul(x, g, w, *, norm=True, splits=None, tm=512):
    n, k = x.shape
    m = w.shape[1]
    tm = _row_tile(n, tm)
    if g is None:
        g = jnp.ones((k,), F32)
    widths = (m,) if splits is None else tuple(splits)
    assert sum(widths) == m
    outs = pl.pallas_call(
        functools.partial(_norm_matmul_kernel, norm=norm, splits=widths),
        grid=(n // tm,),
        in_specs=[pl.BlockSpec((tm, k), lambda i: (i, 0)),
                  pl.BlockSpec((1, k), lambda i: (0, 0)),
                  pl.BlockSpec((k, m), lambda i: (0, 0))],
        out_specs=[pl.BlockSpec((tm, wd), lambda i: (i, 0)) for wd in widths],
        out_shape=[jax.ShapeDtypeStruct((n, wd), F32) for wd in widths],
        compiler_params=_params("parallel"),
        name="norm_matmul",
    )(x, g.reshape(1, k), w)
    return outs[0] if splits is None else outs


def _rmsnorm_kernel(x_ref, g_ref, o_ref):
    o_ref[...] = _rms(x_ref[...], g_ref[...])


def rmsnorm_rows(x, g, *, tm=512):
    n, k = x.shape
    tm = _row_tile(n, tm)
    return pl.pallas_call(
        _rmsnorm_kernel,
        grid=(n // tm,),
        in_specs=[pl.BlockSpec((tm, k), lambda i: (i, 0)), pl.BlockSpec((1, k), lambda i: (0, 0))],
        out_specs=pl.BlockSpec((tm, k), lambda i: (i, 0)),
        out_shape=jax.ShapeDtypeStruct((n, k), F32),
        compiler_params=_params("parallel"),
        name="rmsnorm_rows",
    )(x, g.reshape(1, k))


def _matmul_res_kernel(*refs, n_in):
    res_ref = refs[2 * n_in]
    o_ref = refs[2 * n_in + 1]
    acc = res_ref[...]
    for j in range(n_in):
        acc = acc + jnp.dot(refs[2 * j][...].astype(BF16), refs[2 * j + 1][...].astype(BF16),
                            preferred_element_type=F32)
    o_ref[...] = acc


def matmul_res(xs, ws, res, *, tm=512):
    n, m = res.shape
    tm = _row_tile(n, tm)
    in_specs, args = [], []
    for x, w in zip(xs, ws):
        k = x.shape[1]
        in_specs += [pl.BlockSpec((tm, k), lambda i: (i, 0)), pl.BlockSpec((k, m), lambda i: (0, 0))]
        args += [x, w]
    in_specs.append(pl.BlockSpec((tm, m), lambda i: (i, 0)))
    return pl.pallas_call(
        functools.partial(_matmul_res_kernel, n_in=len(xs)),
        grid=(n // tm,),
        in_specs=in_specs,
        out_specs=pl.BlockSpec((tm, m), lambda i: (i, 0)),
        out_shape=jax.ShapeDtypeStruct((n, m), F32),
        compiler_params=_params("parallel"),
        name="matmul_res",
    )(*args, res)


_HALO_A = 8
_HALO_B = 32


def _even_conv_kernel(u_ref, sa_ref, sb_ref, wa_ref, wb_ref, bb_ref, lg_ref, lb_ref,
                      y_ref, na_ref, nb_ref, ea_ref, eb_ref, ear_ref, ebr_ref, *, tt):
    rnd = lambda x: x.astype(BF16).astype(F32)
    t = pl.program_id(1)
    nt = pl.num_programs(1)

    @pl.when(t == 0)
    def _():
        ea_ref[...] = jnp.zeros_like(ea_ref)
        eb_ref[...] = jnp.zeros_like(eb_ref)
        ea_ref[_HALO_A - (CONV_A_W - 1):_HALO_A, :] = sa_ref[0]
        eb_ref[_HALO_B - (CONV_B_W - 1):_HALO_B, :] = sb_ref[0]
        ear_ref[...] = rnd(ea_ref[...])
        ebr_ref[...] = rnd(eb_ref[...])

    @pl.when(t > 0)
    def _():
        ea_ref[0:_HALO_A, :] = ea_ref[tt:tt + _HALO_A, :]
        eb_ref[0:_HALO_B, :] = eb_ref[tt:tt + _HALO_B, :]
        ear_ref[0:_HALO_A, :] = ear_ref[tt:tt + _HALO_A, :]
        ebr_ref[0:_HALO_B, :] = ebr_ref[tt:tt + _HALO_B, :]

    xa = u_ref[0, :, 0:D_A]
    ba = u_ref[0, :, D_A:2 * D_A]
    ca = u_ref[0, :, 2 * D_A:3 * D_A]
    pb = u_ref[0, :, 3 * D_A:3 * D_A + D_B]
    gb = u_ref[0, :, 3 * D_A + D_B:3 * D_A + 2 * D_B]
    va = ca * xa
    vb = pb * _sigmoid(gb)
    ea_ref[_HALO_A:_HALO_A + tt, :] = va
    eb_ref[_HALO_B:_HALO_B + tt, :] = vb
    ear_ref[_HALO_A:_HALO_A + tt, :] = rnd(va)
    ebr_ref[_HALO_B:_HALO_B + tt, :] = rnd(vb)

    acc = jnp.zeros((tt, D_A), F32)
    for k in range(CONV_A_W):
        off = _HALO_A - (CONV_A_W - 1) + k
        acc = acc + ear_ref[off:off + tt, :] * wa_ref[k:k + 1, :]
    y_ref[0, :, 0:D_A] = ba * acc

    acc = jnp.zeros((tt, D_B), F32)
    for k in range(CONV_B_W):
        off = _HALO_B - (CONV_B_W - 1) + k
        acc = acc + ebr_ref[off:off + tt, :] * wb_ref[k:k + 1, :]
    acc = acc + bb_ref[...]
    mu = jnp.mean(acc, axis=-1, keepdims=True)
    xc = acc - mu
    var = jnp.mean(xc * xc, axis=-1, keepdims=True)
    yb = xc * lax.rsqrt(var + EPS) * lg_ref[...] + lb_ref[...]
    y_ref[0, :, D_A:D_A + D_B] = _silu(yb)

    @pl.when(t == nt - 1)
    def _():
        na_ref[0] = ea_ref[_HALO_A + tt - (CONV_A_W - 1):_HALO_A + tt, :]
        nb_ref[0] = eb_ref[_HALO_B + tt - (CONV_B_W - 1):_HALO_B + tt, :]


def even_conv(u, sa, sb, wa, wb, bb, lg, lb, *, tt=256):
    b, t, w = u.shape
    tt = _row_tile(t, tt)
    full = lambda shape: pl.BlockSpec(shape, lambda i, j: (0,) * len(shape))
    return pl.pallas_call(
        functools.partial(_even_conv_kernel, tt=tt),
        grid=(b, t // tt),
        in_specs=[pl.BlockSpec((1, tt, w), lambda i, j: (i, j, 0)),
                  pl.BlockSpec((1, CONV_A_W - 1, D_A), lambda i, j: (i, 0, 0)),
                  pl.BlockSpec((1, CONV_B_W - 1, D_B), lambda i, j: (i, 0, 0)),
                  full((CONV_A_W, D_A)), full((CONV_B_W, D_B)), full((1, D_B)),
                  full((1, D_B)), full((1, D_B))],
        out_specs=[pl.BlockSpec((1, tt, D_A + D_B), lambda i, j: (i, j, 0)),
                   pl.BlockSpec((1, CONV_A_W - 1, D_A), lambda i, j: (i, 0, 0)),
                   pl.BlockSpec((1, CONV_B_W - 1, D_B), lambda i, j: (i, 0, 0))],
        out_shape=[jax.ShapeDtypeStruct((b, t, D_A + D_B), F32),
                   jax.ShapeDtypeStruct((b, CONV_A_W - 1, D_A), F32),
                   jax.ShapeDtypeStruct((b, CONV_B_W - 1, D_B), F32)],
        scratch_shapes=[pltpu.VMEM((_HALO_A + tt, D_A), F32), pltpu.VMEM((_HALO_B + tt, D_B), F32),
                        pltpu.VMEM((_HALO_A + tt, D_A), F32), pltpu.VMEM((_HALO_B + tt, D_B), F32)],
        compiler_params=_params("parallel", "arbitrary"),
        name="even_conv",
    )(u, sa, sb, wa, wb, bb.reshape(1, D_B), lg.reshape(1, D_B), lb.reshape(1, D_B))


def _xattn_kernel(q_ref, k_ref, v_ref, o_ref, *, cache_layout):
    scale = XA_HD ** -0.5
    for h in range(XA_HEADS):
        sl = slice(h * XA_HD, (h + 1) * XA_HD)
        q = q_ref[0, :, sl].astype(BF16)
        if cache_layout:
            k = k_ref[0, 0, :, h, :].astype(BF16)
            v = v_ref[0, 0, :, h, :].astype(BF16)
        else:
            k = k_ref[0, :, sl].astype(BF16)
            v = v_ref[0, :, sl].astype(BF16)
        s = lax.dot_general(q, k, (((1,), (1,)), ((), ())), preferred_element_type=F32) * scale
        m = jnp.max(s, axis=-1, keepdims=True)
        p = jnp.exp(s - m)
        p = p / jnp.sum(p, axis=-1, keepdims=True)
        o_ref[0, :, sl] = jnp.dot(p.astype(BF16), v, preferred_element_type=F32)


def xattn(q, k, v, *, layer=None, tq=512):
    b, t, w = q.shape
    tq = _row_tile(t, tq)
    if layer is None:
        m = k.shape[1]
        kv_spec = pl.BlockSpec((1, m, w), lambda i, j: (i, 0, 0))
    else:
        m = k.shape[2]
        kv_spec = pl.BlockSpec((1, 1, m, XA_HEADS, XA_HD), lambda i, j: (layer, i, 0, 0, 0))
    return pl.pallas_call(
        functools.partial(_xattn_kernel, cache_layout=layer is not None),
        grid=(b, t // tq),
        in_specs=[pl.BlockSpec((1, tq, w), lambda i, j: (i, j, 0)), kv_spec, kv_spec],
        out_specs=pl.BlockSpec((1, tq, w), lambda i, j: (i, j, 0)),
        out_shape=jax.ShapeDtypeStruct((b, t, w), F32),
        compiler_params=_params("parallel", "parallel"),
        name="xattn",
    )(q, k, v)


MOE_BLK = 256
_ROUTER_W = 128


def _router_kernel(x_ref, g_ref, w_ref, lg_ref, xn_ref):
    x = _rms(x_ref[...], g_ref[...])
    xb = x.astype(BF16)
    lg_ref[...] = jnp.dot(xb, w_ref[...].astype(BF16), preferred_element_type=F32)
    half = x.shape[1] // 2
    bits = lambda v: lax.bitcast_convert_type(v.astype(F32), jnp.uint32)
    words = (bits(xb[:, half:]) & jnp.uint32(0xFFFF0000)) | (bits(xb[:, :half]) >> 16)
    xn_ref[...] = lax.bitcast_convert_type(words, F32)


def moe_router(x, g, w_router, *, tm=512):
    n, k = x.shape
    tm = _row_tile(n, tm)
    return pl.pallas_call(
        _router_kernel,
        grid=(n // tm,),
        in_specs=[pl.BlockSpec((tm, k), lambda i: (i, 0)),
                  pl.BlockSpec((1, k), lambda i: (0, 0)),
                  pl.BlockSpec((k, _ROUTER_W), lambda i: (0, 0))],
        out_specs=[pl.BlockSpec((tm, _ROUTER_W), lambda i: (i, 0)),
                   pl.BlockSpec((tm, k // 2), lambda i: (i, 0))],
        out_shape=[jax.ShapeDtypeStruct((n, _ROUTER_W), F32), jax.ShapeDtypeStruct((n, k // 2), F32)],
        compiler_params=_params("parallel"),
        name="moe_router",
    )(x, g.reshape(1, k), w_router)


def _expert_kernel(be_ref, act_ref, x_ref, gate_ref, w1_ref, w3_ref, w2_ref, o_ref, w1b_ref, w3b_ref, w2b_ref):
    i = pl.program_id(0)
    prev = be_ref[jnp.maximum(i - 1, 0)]

    @pl.when((act_ref[i] > 0) & ((i == 0) | (be_ref[i] != prev)))
    def _():
        w1b_ref[...] = w1_ref[0, 0].astype(BF16)
        w3b_ref[...] = w3_ref[0, 0].astype(BF16)
        w2b_ref[...] = w2_ref[0, 0].astype(BF16)

    @pl.when(act_ref[i] > 0)
    def _():
        words = lax.bitcast_convert_type(x_ref[...], jnp.uint32)
        unpack = lambda v: lax.bitcast_convert_type(v, F32).astype(BF16)
        x = jnp.concatenate([unpack(words << 16), unpack(words & jnp.uint32(0xFFFF0000))], axis=1)
        h1 = jnp.dot(x, w1b_ref[...], preferred_element_type=F32)
        h3 = jnp.dot(x, w3b_ref[...], preferred_element_type=F32)
        hid = (_silu(h1) * h3).astype(BF16)
        out = jnp.dot(hid, w2b_ref[...], preferred_element_type=F32)
        o_ref[...] = out * gate_ref[...]

    @pl.when(act_ref[i] == 0)
    def _():
        o_ref[...] = jnp.zeros_like(o_ref)


def moe_experts(xg, gate, blk_exp, blk_act, w1, w3, w2, layer):
    rows = xg.shape[0]
    d = w1.shape[2]
    nb = rows // MOE_BLK
    ff = w1.shape[3]
    return pl.pallas_call(
        _expert_kernel,
        grid_spec=pltpu.PrefetchScalarGridSpec(
            num_scalar_prefetch=2,
            grid=(nb,),
            in_specs=[pl.BlockSpec((MOE_BLK, d // 2), lambda i, be, act: (i, 0)),
                      pl.BlockSpec((MOE_BLK, 1), lambda i, be, act: (i, 0)),
                      pl.BlockSpec((1, 1, d, ff), lambda i, be, act: (layer, be[i], 0, 0)),
                      pl.BlockSpec((1, 1, d, ff), lambda i, be, act: (layer, be[i], 0, 0)),
                      pl.BlockSpec((1, 1, ff, d), lambda i, be, act: (layer, be[i], 0, 0))],
            out_specs=pl.BlockSpec((MOE_BLK, d), lambda i, be, act: (i, 0)),
            scratch_shapes=[pltpu.VMEM((d, ff), BF16), pltpu.VMEM((d, ff), BF16), pltpu.VMEM((ff, d), BF16)],
        ),
        out_shape=jax.ShapeDtypeStruct((rows, d), F32),
        compiler_params=_params("arbitrary"),
        name="moe_experts",
    )(blk_exp, blk_act, xg, gate, w1, w3, w2)


def _expert_gather_kernel(tok_ref, be_ref, act_ref, xn_hbm, gate_ref, w1_ref, w3_ref, w2_ref, o_ref,
                          xbuf_ref, sem_ref, w1b_ref, w3b_ref, w2b_ref):
    i = pl.program_id(0)
    nb = pl.num_programs(0)

    def row_copy(blk, slot, r):
        tok = tok_ref[blk * MOE_BLK + r]
        return pltpu.make_async_copy(xn_hbm.at[pl.ds(tok, 1)], xbuf_ref.at[slot, pl.ds(r, 1)], sem_ref.at[slot])

    def start_gather(blk, slot):
        def body(r, carry):
            row_copy(blk, slot, r).start()
            return carry
        lax.fori_loop(0, MOE_BLK, body, 0, unroll=8)

    def wait_gather(slot):
        pltpu.make_async_copy(xbuf_ref.at[slot], xbuf_ref.at[slot], sem_ref.at[slot]).wait()

    slot = lax.rem(i, 2)
    nxt = jnp.minimum(i + 1, nb - 1)

    @pl.when((i == 0) & (act_ref[0] > 0))
    def _():
        start_gather(0, 0)

    @pl.when((i + 1 < nb) & (act_ref[nxt] > 0))
    def _():
        start_gather(nxt, 1 - slot)

    prev = be_ref[jnp.maximum(i - 1, 0)]

    @pl.when((act_ref[i] > 0) & ((i == 0) | (be_ref[i] != prev)))
    def _():
        w1b_ref[...] = w1_ref[0, 0].astype(BF16)
        w3b_ref[...] = w3_ref[0, 0].astype(BF16)
        w2b_ref[...] = w2_ref[0, 0].astype(BF16)

    @pl.when(act_ref[i] > 0)
    def _():
        wait_gather(slot)
        words = lax.bitcast_convert_type(xbuf_ref[slot], jnp.uint32)
        unpack = lambda v: lax.bitcast_convert_type(v, F32).astype(BF16)
        x = jnp.concatenate([unpack(words << 16), unpack(words & jnp.uint32(0xFFFF0000))], axis=1)
        h1 = jnp.dot(x, w1b_ref[...], preferred_element_type=F32)
        h3 = jnp.dot(x, w3b_ref[...], preferred_element_type=F32)
        hid = (_silu(h1) * h3).astype(BF16)
        out = jnp.dot(hid, w2b_ref[...], preferred_element_type=F32)
        o_ref[...] = out * gate_ref[...]

    @pl.when(act_ref[i] == 0)
    def _():
        o_ref[...] = jnp.zeros_like(o_ref)


def moe_experts_gather(xn, buf_tok, gate, blk_exp, blk_act, w1, w3, w2, layer):
    rows = buf_tok.shape[0]
    d = w1.shape[2]
    nb = rows // MOE_BLK
    ff = w1.shape[3]
    return pl.pallas_call(
        _expert_gather_kernel,
        grid_spec=pltpu.PrefetchScalarGridSpec(
            num_scalar_prefetch=3,
            grid=(nb,),
            in_specs=[pl.BlockSpec(memory_space=pl.ANY),
                      pl.BlockSpec((MOE_BLK, 1), lambda i, tok, be, act: (i, 0)),
                      pl.BlockSpec((1, 1, d, ff), lambda i, tok, be, act: (layer, be[i], 0, 0)),
                      pl.BlockSpec((1, 1, d, ff), lambda i, tok, be, act: (layer, be[i], 0, 0)),
                      pl.BlockSpec((1, 1, ff, d), lambda i, tok, be, act: (layer, be[i], 0, 0))],
            out_specs=pl.BlockSpec((MOE_BLK, d), lambda i, tok, be, act: (i, 0)),
            scratch_shapes=[pltpu.VMEM((2, MOE_BLK, d // 2), F32), pltpu.SemaphoreType.DMA((2,)),
                            pltpu.VMEM((d, ff), BF16), pltpu.VMEM((d, ff), BF16), pltpu.VMEM((ff, d), BF16)],
        ),
        out_shape=jax.ShapeDtypeStruct((rows, d), F32),
        compiler_params=_params("arbitrary"),
        name="moe_experts",
    )(buf_tok, blk_exp, blk_act, xn, gate, w1, w3, w2)


def moe_layer(h, g, wg, bg, we, be, w1, w3, w2, layer):
    n, d = h.shape
    w_router = jnp.concatenate([wg, we, jnp.zeros((d, _ROUTER_W - MOE_GROUPS - MOE_E), F32)], axis=1)
    logits, xn = moe_router(h, g, w_router)
    lg = logits[:, :MOE_GROUPS] + bg
    grp = jnp.argmax(lg, axis=-1)
    gw = jnp.take_along_axis(jax.nn.softmax(lg, axis=-1), grp[:, None], axis=1)
    le = (logits[:, MOE_GROUPS:MOE_GROUPS + MOE_E] + be).reshape(n, MOE_GROUPS, MOE_EPG)
    le = jnp.take_along_axis(le, grp[:, None, None], axis=1)[:, 0]
    tv, ti = lax.top_k(jax.nn.softmax(le, axis=-1), MOE_TOPK)
    wts = gw * tv / jnp.sum(tv, axis=-1, keepdims=True)
    eid = (grp[:, None] * MOE_EPG + ti).reshape(-1).astype(jnp.int32)
    npair = n * MOE_TOPK
    experts = jnp.arange(MOE_E, dtype=jnp.int32)
    order = jnp.argsort(eid).astype(jnp.int32)
    rank = jnp.argsort(order).astype(jnp.int32)
    counts = jnp.sum(eid[:, None] == experts[None, :], axis=0).astype(jnp.int32)
    start = jnp.cumsum(counts) - counts
    padded = (counts + MOE_BLK - 1) // MOE_BLK * MOE_BLK
    pend = jnp.cumsum(padded)
    shift = pend - padded - start
    nb = -(-npair // MOE_BLK) + MOE_E
    blk_lo = jnp.arange(nb, dtype=jnp.int32) * MOE_BLK
    blk_exp = jnp.minimum(jnp.sum(pend[None, :] <= blk_lo[:, None], axis=1), MOE_E - 1).astype(jnp.int32)
    blk_act = (blk_lo < pend[-1]).astype(jnp.int32)
    src = (blk_lo - shift[blk_exp])[:, None] + jnp.arange(MOE_BLK, dtype=jnp.int32)[None, :]
    live = src < (start + counts)[blk_exp][:, None]
    pair = order[jnp.where(live, src, 0).reshape(-1)]
    buf_tok = pair // MOE_TOPK
    buf_gate = jnp.where(live.reshape(-1), wts.reshape(-1)[pair], 0.0)
    dest = (rank + shift[eid]).reshape(n, MOE_TOPK)
    out = moe_experts_gather(xn, buf_tok.astype(jnp.int32), buf_gate[:, None], blk_exp, blk_act, w1, w3, w2, layer)
    return h + (out[dest[:, 0]] + out[dest[:, 1]])


_KT = 128
_NT = (((1,), (1,)), ((), ()))
_BIG = 3e38
_M0 = -1e29


def _compress_kernel(x_ref, pe_ref, w_ref, o_ref):
    acc = jnp.zeros(o_ref.shape, F32)
    for l in range(L_CMP):
        y = x_ref[:, l, :] + pe_ref[l:l + 1, :]
        acc = acc + jnp.dot(y.astype(BF16), w_ref[l], preferred_element_type=F32)
    o_ref[...] = acc


def nsa_compress(x, pe, w, *, tb=256):
    nb = x.shape[0]
    tb = _row_tile(nb, tb)
    pe2 = jnp.concatenate([pe] * NSA_KV, axis=1)
    z = jnp.zeros_like(w)
    w2 = jnp.concatenate([jnp.concatenate([w, z], axis=2), jnp.concatenate([z, w], axis=2)], axis=1).astype(BF16)
    return pl.pallas_call(
        _compress_kernel,
        grid=(nb // tb,),
        in_specs=[pl.BlockSpec((tb, L_CMP, NSA_KW), lambda i: (i, 0, 0)),
                  pl.BlockSpec((L_CMP, NSA_KW), lambda i: (0, 0)),
                  pl.BlockSpec((L_CMP, NSA_KW, NSA_KW), lambda i: (0, 0, 0))],
        out_specs=pl.BlockSpec((tb, NSA_KW), lambda i: (i, 0)),
        out_shape=jax.ShapeDtypeStruct((nb, NSA_KW), F32),
        compiler_params=_params("parallel"),
        name="nsa_compress",
    )(x, pe2, w2)


def _compress_cache_kernel(x_ref, pe_ref, w_ref, o_ref):
    for g in range(NSA_KV):
        acc = jnp.zeros((o_ref.shape[0], NSA_HD), F32)
        for l in range(L_CMP):
            y = x_ref[0, :, l, g, :] + pe_ref[l:l + 1, :]
            acc = acc + jnp.dot(y.astype(BF16), w_ref[l], preferred_element_type=F32)
        o_ref[:, g * NSA_HD:(g + 1) * NSA_HD] = acc


def nsa_compress_cache(cache, layer, pe, w, *, tb=256):
    nl, n_pool, page = cache.shape[:3]
    nb = n_pool * page // L_CMP
    tb = _row_tile(nb, tb)
    x = cache.reshape(nl, nb, L_CMP, NSA_KV, NSA_HD)
    return pl.pallas_call(
        _compress_cache_kernel,
        grid=(nb // tb,),
        in_specs=[pl.BlockSpec((1, tb, L_CMP, NSA_KV, NSA_HD), lambda i: (layer, i, 0, 0, 0)),
                  pl.BlockSpec((L_CMP, NSA_HD), lambda i: (0, 0)),
                  pl.BlockSpec((L_CMP, NSA_HD, NSA_HD), lambda i: (0, 0, 0))],
        out_specs=pl.BlockSpec((tb, NSA_KW), lambda i: (i, 0)),
        out_shape=jax.ShapeDtypeStruct((nb, NSA_KW), F32),
        compiler_params=_params("parallel"),
        name="nsa_compress_cache",
    )(x, pe, w.astype(BF16))


def _qz(q, g, nq):
    lane = lax.broadcasted_iota(jnp.int32, (nq, NSA_KW), 1)
    keep = (lane >> 6) == g
    parts = []
    for r in range(NSA_REP):
        h = g * NSA_REP + r
        slab = q[:, (h // 2) * NSA_KW:(h // 2 + 1) * NSA_KW]
        if h % 2 != g:
            slab = pltpu.roll(slab, NSA_HD, axis=1)
        parts.append(jnp.where(keep, slab, 0.0))
    return jnp.concatenate(parts, axis=0).astype(BF16)


def _softmax_init(m_ref, l_ref, acc_ref):
    m_ref[...] = jnp.full(m_ref.shape, NEG, F32)
    l_ref[...] = jnp.zeros(l_ref.shape, F32)
    acc_ref[...] = jnp.zeros(acc_ref.shape, F32)


def _softmax_tile(st, dpos, maskf, slope, vt, m_ref, l_ref, acc_ref, g):
    s = st - slope * dpos
    sm = jnp.where(maskf > 0.0, s, NEG)
    m_old = m_ref[g]
    m_new = jnp.maximum(m_old, jnp.max(sm, axis=0, keepdims=True))
    alpha = jnp.exp(m_old - m_new)
    p = jnp.exp(sm - m_new) * maskf
    l_ref[g] = alpha * l_ref[g] + jnp.sum(p, axis=0, keepdims=True)
    acc_ref[g] = alpha * acc_ref[g] + jnp.dot(vt, p.astype(BF16), preferred_element_type=F32)
    m_ref[g] = m_new


def _softmax_done(l_ref, acc_ref, g):
    l = l_ref[g]
    return acc_ref[g] * jnp.where(l > 0.0, 1.0 / l, 0.0)


def _compressed_branch(qz, kcb, vcbt_g, qpos, slope, ncb):
    st = lax.dot_general(kcb, qz, _NT, preferred_element_type=F32)
    row = lax.broadcasted_iota(jnp.int32, (ncb, 1), 0)
    half = ncb // 2
    blk = jnp.where(row < half, 2 * row, 2 * (row - half) + 1)
    c_pos = blk * L_CMP + (L_CMP - 1)
    d_c = qpos - c_pos
    maskf = jnp.where(d_c >= 0, 1.0, 0.0)
    s = st - slope * d_c.astype(F32)
    sm = jnp.where(d_c >= 0, s, NEG)
    m = jnp.max(sm, axis=0, keepdims=True)
    p = jnp.exp(sm - m) * maskf
    l = jnp.sum(p, axis=0, keepdims=True)
    p = p * jnp.where(l > 0.0, 1.0 / l, 0.0)
    o = jnp.dot(vcbt_g, p.astype(BF16), preferred_element_type=F32)
    return o, p


def _select_blocks(imp_sel, qpos, nsp):
    cols = imp_sel.shape[1]
    blk = lax.broadcasted_iota(jnp.int32, (nsp, cols), 0)
    cur = qpos >> 6
    valid = blk <= cur
    forced = jnp.where(valid, jnp.where(blk == 0, 1.0, jnp.where(blk >= cur - 1, 1.0, 0.0)), 0.0)
    score = jnp.where(forced > 0.0, _BIG, jnp.where(valid, imp_sel, -1.0))
    sel = jnp.zeros((nsp, cols), F32)
    for _ in range(N_SEL):
        m = jnp.max(score, axis=0, keepdims=True)
        idx = jnp.min(jnp.where(score == m, blk, nsp + 1), axis=0, keepdims=True)
        pick = blk == idx
        sel = jnp.where(pick, 1.0, sel)
        score = jnp.where(pick, -2.0, score)
    return jnp.where(valid, sel, 0.0)


def _nsa_prompt_kernel(q_ref, glog_ref, kcb_ref, vcbt_ref, ks_ref, vst_ref, kw_ref, vwt_ref,
                       qoff_ref, slope_ref, qaux_ref, o_ref, sel_ref, m_ref, l_ref, acc_ref, *, nq, ncb):
    i = pl.program_id(1)
    st0 = i * nq
    qoff = qoff_ref[...]
    qpos = st0 + qoff
    q = q_ref[0] * (NSA_HD ** -0.5)
    key_io = lax.broadcasted_iota(jnp.int32, (_KT, 1), 0)
    lane = lax.broadcasted_iota(jnp.int32, (_KT, NSA_KW), 1)
    key_lane = jnp.where(lane == 1, lax.broadcasted_iota(jnp.int32, (_KT, NSA_KW), 0), 0).astype(F32)
    causal = key_io <= qoff
    anti = key_io >= qoff
    nsp = ncb // 2
    wt = WINDOW // _KT

    def m_init():
        m_ref[...] = jnp.full(m_ref.shape, _M0, F32)
        l_ref[...] = jnp.zeros(l_ref.shape, F32)
        acc_ref[...] = jnp.zeros(acc_ref.shape, F32)

    for g in range(NSA_KV):
        qz = _qz(q, g, nq)
        qx = jnp.concatenate([qz, qaux_ref[g]], axis=1)
        slope = slope_ref[g]
        rows = slice(g * NSA_HD, (g + 1) * NSA_HD)
        o_c, p_c = _compressed_branch(qz, kcb_ref[0], vcbt_ref[0, rows, :], qpos, slope, ncb)
        imp = p_c[:, 0:nq]
        for r in range(1, NSA_REP):
            imp = imp + p_c[:, r * nq:(r + 1) * nq]
        imp_sel = imp[0:nsp] + imp[nsp:ncb]
        sel = _select_blocks(imp_sel, qpos[:, 0:nq], nsp)
        sel_ref[g] = jnp.concatenate([jnp.where(sel > 0.0, 0.0, NEG)] * NSA_REP, axis=1)
        blk = lax.broadcasted_iota(jnp.int32, (nsp, 1), 0)
        row_any = jnp.max(sel, axis=1, keepdims=True)
        first = jnp.min(jnp.where(row_any > 0.0, jnp.where(blk >= 2, blk, 2 * nsp), 2 * nsp), axis=0, keepdims=True)
        kt_lo = jnp.minimum(first[0, 0] >> 1, i)

        def tile(k_ref, vt_ref, kt, mode, with_sel):
            off = kt * _KT if isinstance(kt, int) else pl.multiple_of(kt * _KT, _KT)
            kaux =jnp.where(lane == 0, (kt - i).astype(F32), key_lane).astype(BF16)
            kx = jnp.concatenate([k_ref[0, pl.ds(off, _KT), :], kaux], axis=1)
            s = lax.dot_general(kx, qx, _NT, preferred_element_type=F32)
            if with_sel:
                half = _KT // 2
                s = jnp.concatenate([s[0:half] + sel_ref[g, pl.ds(2 * kt, 1), :],
                                     s[half:_KT] + sel_ref[g, pl.ds(2 * kt + 1, 1), :]], axis=0)
            if mode == "causal":
                s = jnp.where(causal, s, NEG)
            elif mode == "anti":
                s = jnp.where(anti, s, NEG)
            m_old = m_ref[g]
            m_new = jnp.maximum(m_old, jnp.max(s, axis=0, keepdims=True))
            alpha = jnp.exp(m_old - m_new)
            p = jnp.exp(s - m_new)
            l_ref[g] = alpha * l_ref[g] + jnp.sum(p, axis=0, keepdims=True)
            vt = vt_ref[0, rows, pl.ds(off, _KT)]
            acc_ref[g] = alpha * acc_ref[g] + jnp.dot(vt, p.astype(BF16), preferred_element_type=F32)
            m_ref[g] = m_new

        m_init()

        @pl.when(i > 0)
        def _():
            tile(ks_ref, vst_ref, 0, "none", True)

        def sel_body(kt, carry):
            tile(ks_ref, vst_ref, kt, "none", True)
            return carry

        lax.fori_loop(jnp.maximum(kt_lo, 1), i, sel_body, 0)
        tile(ks_ref, vst_ref, i, "causal", True)
        o_s = _softmax_done(l_ref, acc_ref, g)

        m_init()

        @pl.when(i >= wt)
        def _():
            tile(kw_ref, vwt_ref, i - wt, "anti", False)

        def win_body(kt, carry):
            tile(kw_ref, vwt_ref, kt, "none", False)
            return carry

        lax.fori_loop(jnp.maximum(i - wt + 1, 0), i, win_body, 0)
        tile(kw_ref, vwt_ref, i, "causal", False)
        o_w = _softmax_done(l_ref, acc_ref, g)

        gc = _sigmoid(glog_ref[0, 0, g:g + 1, :])
        gs = _sigmoid(glog_ref[0, 0, 2 + g:3 + g, :])
        gw = _sigmoid(glog_ref[0, 0, 4 + g:5 + g, :])
        o_ref[0, 0, g] = gc * o_c + gs * o_s + gw * o_w


_TS = 256


def _pos_lanes(nkeys, tile_off):
    lane = lax.broadcasted_iota(jnp.int32, (nkeys, NSA_KW), 1)
    key = lax.broadcasted_iota(jnp.int32, (nkeys, NSA_KW), 0)
    hi = (tile_off + (key >> 7)).astype(F32)
    lo = (key & (_KT - 1)).astype(F32)
    return jnp.where(lane == 0, hi, jnp.where(lane == 1, lo, 0.0)).astype(BF16)


def _nsa_prompt_kernel2(q_ref, glog_ref, kcb_ref, vcbt_ref, ks_ref, vst_ref, kw_ref, vwt_ref,
                        qoff_ref, slope_ref, qaux_ref, o_ref, sel_ref, m_ref, l_ref, acc_ref, *, nq, ncb):
    i = pl.program_id(1)
    st0 = i * nq
    qoff = qoff_ref[...]
    qpos = st0 + qoff
    q = q_ref[0] * (NSA_HD ** -0.5)
    nsp = ncb // 2
    bpt = _TS // L_SEL
    idiag = st0 // _TS
    groups = range(NSA_KV)
    rows = [slice(g * NSA_HD, (g + 1) * NSA_HD) for g in groups]

    qxs, o_cs, firsts = [], [], []
    for g in groups:
        qz = _qz(q, g, nq)
        qxs.append(jnp.concatenate([qz, qaux_ref[g]], axis=1))
        o_c, p_c = _compressed_branch(qz, kcb_ref[0], vcbt_ref[0, rows[g], :], qpos, slope_ref[g], ncb)
        o_cs.append(o_c)
        imp = p_c[:, 0:nq]
        for r in range(1, NSA_REP):
            imp = imp + p_c[:, r * nq:(r + 1) * nq]
        sel = _select_blocks(imp[0:nsp] + imp[nsp:ncb], qpos[:, 0:nq], nsp)
        sel_ref[g] = jnp.concatenate([jnp.where(sel > 0.0, 0.0, NEG)] * NSA_REP, axis=1)
        blk = lax.broadcasted_iota(jnp.int32, (nsp, 1), 0)
        row_any = jnp.max(sel, axis=1, keepdims=True)
        first = jnp.min(jnp.where(row_any > 0.0, jnp.where(blk >= bpt, blk, nsp * bpt), nsp * bpt),
                        axis=0, keepdims=True)
        firsts.append(first[0, 0])
    lo = jnp.minimum(jnp.minimum(firsts[0], firsts[1]) // bpt, idiag)

    m_ref[...] = jnp.full(m_ref.shape, _M0, F32)
    l_ref[...] = jnp.zeros(l_ref.shape, F32)
    acc_ref[...] = jnp.zeros(acc_ref.shape, F32)

    def sel_tile(kt, diag):
        off = kt * _TS if isinstance(kt, int) else pl.multiple_of(kt * _TS, _TS)
        kx = jnp.concatenate([ks_ref[0, pl.ds(off, _TS), :], _pos_lanes(_TS, kt * (_TS // _KT) - i)], axis=1)
        if diag:
            visible = (off + lax.broadcasted_iota(jnp.int32, (_TS, 1), 0)) <= qpos
        for g in groups:
            s = lax.dot_general(kx, qxs[g], _NT, preferred_element_type=F32)
            s = jnp.concatenate([s[b * L_SEL:(b + 1) * L_SEL] + sel_ref[g, pl.ds(bpt * kt + b, 1), :]
                                 for b in range(bpt)], axis=0)
            if diag:
                s = jnp.where(visible, s, NEG)
            m_old = m_ref[g]
            m_new = jnp.maximum(m_old, jnp.max(s, axis=0, keepdims=True))
            alpha = jnp.exp(m_old - m_new)
            p = jnp.exp(s - m_new)
            l_ref[g] = alpha * l_ref[g] + jnp.sum(p, axis=0, keepdims=True)
            vt = vst_ref[0, rows[g], pl.ds(off, _TS)]
            acc_ref[g] = alpha * acc_ref[g] + jnp.dot(vt, p.astype(BF16), preferred_element_type=F32)
            m_ref[g] = m_new

    @pl.when(idiag > 0)
    def _():
        sel_tile(0, False)

    def sel_body(kt, carry):
        sel_tile(kt, False)
        return carry

    lax.fori_loop(lo, idiag, sel_body, 0)
    sel_tile(idiag, True)

    wk = WINDOW + nq
    wstart = pl.multiple_of(jnp.maximum(i - WINDOW // nq, 0) * nq, nq)
    kxw = jnp.concatenate([kw_ref[0, pl.ds(wstart, wk), :], _pos_lanes(wk, wstart // _KT - i)], axis=1)
    d_w = qpos - (wstart + lax.broadcasted_iota(jnp.int32, (wk, 1), 0))
    in_window = jnp.where(d_w >= 0, d_w, WINDOW + 1) <= WINDOW
    for g in groups:
        s = jnp.where(in_window, lax.dot_general(kxw, qxs[g], _NT, preferred_element_type=F32), NEG)
        m = jnp.max(s, axis=0, keepdims=True)
        p = jnp.exp(s - m)
        l = jnp.sum(p, axis=0, keepdims=True)
        o_w = jnp.dot(vwt_ref[0, rows[g], pl.ds(wstart, wk)], p.astype(BF16), preferred_element_type=F32) / l
        o_s = _softmax_done(l_ref, acc_ref, g)
        gc = _sigmoid(glog_ref[0, 0, g:g + 1, :])
        gs = _sigmoid(glog_ref[0, 0, 2 + g:3 + g, :])
        gw = _sigmoid(glog_ref[0, 0, 4 + g:5 + g, :])
        o_ref[0, 0, g] = gc * o_cs[g] + gs * o_s + gw * o_w


def _nsa_cols(nq):
    c = NSA_REP * nq
    qoff = (jnp.arange(c, dtype=jnp.int32) % nq).reshape(1, c)
    slopes = 2.0 ** (-8.0 * jnp.arange(1, NSA_HEADS + 1, dtype=F32) / NSA_HEADS)
    slope = jnp.repeat(slopes.reshape(NSA_KV, NSA_REP), nq, axis=1).reshape(NSA_KV, 1, c)
    return qoff, slope


def _gate_cols(glog, nq):
    b, t, _ = glog.shape
    x = glog.reshape(b, t // nq, nq, NSA_KV, NSA_REP, 3)
    return x.transpose(0, 1, 5, 3, 4, 2).reshape(b, t // nq, 3 * NSA_KV, NSA_REP * nq)


def _even_odd(x):
    return jnp.concatenate([x[:, 0::2], x[:, 1::2]], axis=1)


def _uncols(o, nq):
    b, nb = o.shape[:2]
    x = o.reshape(b, nb, NSA_KV, NSA_HD, NSA_REP, nq)
    return x.transpose(0, 1, 5, 2, 4, 3).reshape(b, nb * nq, NSA_W)


def nsa_prompt(q, glog, kcb, vcb, ks, vs, kw, vw, *, nq=128):
    b, t, _ = q.shape
    ncb = kcb.shape[1]
    c = NSA_REP * nq
    nblk = t // nq
    assert nq == _KT and t % _TS == 0 and t >= WINDOW + nq and ncb == t // L_CMP
    qoff, slope = _nsa_cols(nq)
    lane = jnp.arange(NSA_KW)[None, None, :]
    slope_col = slope.reshape(NSA_KV, c, 1)
    qaux = jnp.where(lane == 0, slope_col * _KT, jnp.where(lane == 1, slope_col, 0.0)).astype(BF16)
    kcb_p = _even_odd(kcb).astype(BF16)
    vcbt = _even_odd(vcb).transpose(0, 2, 1).astype(BF16)
    per_b = lambda shape: pl.BlockSpec((1,) + shape, lambda i, j: (i, 0, 0))
    out = pl.pallas_call(
        functools.partial(_nsa_prompt_kernel2, nq=nq, ncb=ncb),
        grid=(b, nblk),
        in_specs=[pl.BlockSpec((1, nq, NSA_W), lambda i, j: (i, j, 0)),
                  pl.BlockSpec((1, 1, 3 * NSA_KV, c), lambda i, j: (i, j, 0, 0)),
                  per_b((ncb, NSA_KW)), per_b((NSA_KW, ncb)),
                  per_b((t, NSA_KW)), per_b((NSA_KW, t)), per_b((t, NSA_KW)), per_b((NSA_KW, t)),
                  pl.BlockSpec((1, c), lambda i, j: (0, 0)),
                  pl.BlockSpec((NSA_KV, 1, c), lambda i, j: (0, 0, 0)),
                  pl.BlockSpec((NSA_KV, c, NSA_KW), lambda i, j: (0, 0, 0))],
        out_specs=pl.BlockSpec((1, 1, NSA_KV, NSA_HD, c), lambda i, j: (i, j, 0, 0, 0)),
        out_shape=jax.ShapeDtypeStruct((b, nblk, NSA_KV, NSA_HD, c), F32),
        scratch_shapes=[pltpu.VMEM((NSA_KV, ncb // 2, c), F32),
                        pltpu.VMEM((NSA_KV, 1, c), F32), pltpu.VMEM((NSA_KV, 1, c), F32),
                        pltpu.VMEM((NSA_KV, NSA_HD, c), F32)],
        compiler_params=_params("parallel", "arbitrary"),
        name="nsa_prompt",
    )(q, _gate_cols(glog, nq), kcb_p, vcbt, ks.astype(BF16), vs.transpose(0, 2, 1).astype(BF16),
      kw.astype(BF16), vw.transpose(0, 2, 1).astype(BF16), qoff, slope, qaux)
    return _uncols(out, nq)


def _nsa_sample_kernel(pt_ref, q_ref, glog_ref, kcb_ref, vcbt_ref, *refs, nq, ncb, npages, page, past, wb):
    ks_pages = refs[0:npages]
    vs_pages = refs[npages:2 * npages]
    (nks_ref, nvs_ref, nkw_ref, nvw_ref, wk_ref, wv_ref, qoff_ref, slope_ref, rmat_ref,
     o_ref, wko_ref, wvo_ref, m_ref, l_ref, acc_ref) = refs[2 * npages:]
    del pt_ref
    c = NSA_REP * nq
    qpos = past + qoff_ref[...]
    q = q_ref[0] * (NSA_HD ** -0.5)
    key_io = lax.broadcasted_iota(jnp.int32, (_KT, 1), 0)
    nsp = sel_rows = -(-(past + nq) // L_SEL)
    nsp = -(-nsp // 8) * 8
    pad_rows = lambda x: jnp.concatenate([x, jnp.zeros((_KT - nq, x.shape[1]), F32)], axis=0)

    qzs, slopes, sels, o_cs = [], [], [], []
    for g in range(NSA_KV):
        qz = _qz(q, g, nq)
        slope = slope_ref[g]
        o_c, p_c = _compressed_branch(qz, kcb_ref[0], vcbt_ref[0, g * NSA_HD:(g + 1) * NSA_HD, :], qpos, slope, ncb)
        imp = _dot_exact_rhs(p_c, rmat_ref[...])
        imp_sel = imp[0:ncb // 2] + imp[ncb // 2:ncb]
        imp_sel = jnp.concatenate([imp_sel, jnp.zeros((nsp - ncb // 2, c), F32)], axis=0)
        qzs.append(qz)
        slopes.append(slope)
        sels.append(_select_blocks(imp_sel, qpos, nsp))
        o_cs.append(o_c)
    del sel_rows

    qgs = [qzs[g][:, g * NSA_HD:(g + 1) * NSA_HD] for g in range(NSA_KV)]

    def run_tile(kv_fn, tok0, mask_fn):
        dpos = qpos - (tok0 + key_io)
        for g in range(NSA_KV):
            k, v = kv_fn(g)
            st = lax.dot_general(k.astype(BF16), qgs[g], _NT, preferred_element_type=F32)
            maskf = mask_fn(g, dpos)
            s = st - slopes[g] * dpos.astype(F32)
            sm = jnp.where(maskf > 0.0, s, NEG)
            m_old = m_ref[g]
            m_new = jnp.maximum(m_old, jnp.max(sm, axis=0, keepdims=True))
            alpha = jnp.exp(m_old - m_new)
            p = jnp.exp(sm - m_new) * maskf
            l_ref[g] = alpha * l_ref[g] + jnp.sum(p, axis=0, keepdims=True)
            pv = lax.dot_general(v.astype(BF16), p.astype(BF16), _TN, preferred_element_type=F32)
            acc_ref[g] = alpha * acc_ref[g] + pv
            m_ref[g] = m_new

    cache_tile = lambda k_ref, v_ref, lo: (lambda g: (k_ref[0, 0, lo:lo + _KT, g, :], v_ref[0, 0, lo:lo + _KT, g, :]))
    new_tile = lambda k_ref, v_ref: (lambda g: (pad_rows(k_ref[0, :, g * NSA_HD:(g + 1) * NSA_HD]),
                                                pad_rows(v_ref[0, :, g * NSA_HD:(g + 1) * NSA_HD])))

    _softmax_init(m_ref, l_ref, acc_ref)
    for j in range(npages + 1):
        blk0 = j * (page // L_SEL)

        def sel_mask(g, dpos, blk0=blk0):
            selrow = jnp.where(key_io < L_SEL, sels[g][blk0:blk0 + 1, :], sels[g][blk0 + 1:blk0 + 2, :])
            return jnp.where(dpos >= 0, selrow, 0.0)

        if j < npages:
            run_tile(cache_tile(ks_pages[j], vs_pages[j], 0), j * page, sel_mask)
        else:
            run_tile(new_tile(nks_ref, nvs_ref), past, sel_mask)
    o_ss = [_softmax_done(l_ref, acc_ref, g) for g in range(NSA_KV)]

    _softmax_init(m_ref, l_ref, acc_ref)
    win_mask = lambda g, dpos: jnp.where(dpos >= 0, jnp.where(dpos <= WINDOW, 1.0, 0.0), 0.0)
    for j in range(wb // _KT):
        run_tile(cache_tile(wk_ref, wv_ref, j * _KT), past - wb + j * _KT, win_mask)
    run_tile(new_tile(nkw_ref, nvw_ref), past, win_mask)
    for g in range(NSA_KV):
        o_w = _softmax_done(l_ref, acc_ref, g)
        gc = _sigmoid(glog_ref[0, 0, g:g + 1, :])
        gs = _sigmoid(glog_ref[0, 0, 2 + g:3 + g, :])
        gw = _sigmoid(glog_ref[0, 0, 4 + g:5 + g, :])
        o_ref[0, 0, g] = gc * o_cs[g] + gs * o_ss[g] + gw * o_w

    wko_ref[0, 0:wb - nq] = wk_ref[0, 0, nq:wb]
    wvo_ref[0, 0:wb - nq] = wv_ref[0, 0, nq:wb]
    for g in range(NSA_KV):
        wko_ref[0, wb - nq:wb, g, :] = nkw_ref[0, :, g * NSA_HD:(g + 1) * NSA_HD]
        wvo_ref[0, wb - nq:wb, g, :] = nvw_ref[0, :, g * NSA_HD:(g + 1) * NSA_HD]


def nsa_sample(q, glog, kcb, vcb, pool_k, pool_v, page_table, nks, nvs, nkw, nvw, win_k, win_v, layer):
    b, nq, _ = q.shape
    ncb = kcb.shape[1]
    npages = page_table.shape[1]
    page = pool_k.shape[2]
    past = npages * page
    wb = win_k.shape[2]
    assert page == _KT and wb % _KT == 0 and nq % 8 == 0 and nq <= L_SEL and ncb % 2 == 0
    c = NSA_REP * nq
    qoff, slope = _nsa_cols(nq)
    col = jnp.arange(c, dtype=jnp.int32)
    rmat = (col[:, None] % nq == col[None, :] % nq).astype(BF16)
    kcb_p = _even_odd(kcb).astype(BF16)
    vcbt = _even_odd(vcb).transpose(0, 2, 1).astype(BF16)
    per_b = lambda shape: pl.BlockSpec((1,) + shape, lambda i, pt: (i,) + (0,) * len(shape))
    const = lambda shape: pl.BlockSpec(shape, lambda i, pt: (0,) * len(shape))
    page_spec = lambda j: pl.BlockSpec((1, 1, page, NSA_KV, NSA_HD), lambda i, pt: (layer, pt[i, j], 0, 0, 0))
    win_spec = pl.BlockSpec((1, 1, wb, NSA_KV, NSA_HD), lambda i, pt: (layer, i, 0, 0, 0))
    in_specs = ([per_b((nq, NSA_W)), per_b((1, 3 * NSA_KV, c)), per_b((ncb, NSA_KW)), per_b((NSA_KW, ncb))]
                + [page_spec(j) for j in range(npages)] * 2
                + [per_b((nq, NSA_KW))] * 4 + [win_spec] * 2
                + [const((1, c)), const((NSA_KV, 1, c)), const((c, c))])
    out, wko, wvo = pl.pallas_call(
        functools.partial(_nsa_sample_kernel, nq=nq, ncb=ncb, npages=npages, page=page, past=past, wb=wb),
        grid_spec=pltpu.PrefetchScalarGridSpec(
            num_scalar_prefetch=1,
            grid=(b,),
            in_specs=in_specs,
            out_specs=[per_b((1, NSA_KV, NSA_HD, c)), per_b((wb, NSA_KV, NSA_HD)), per_b((wb, NSA_KV, NSA_HD))],
            scratch_shapes=[pltpu.VMEM((NSA_KV, 1, c), F32), pltpu.VMEM((NSA_KV, 1, c), F32),
                            pltpu.VMEM((NSA_KV, NSA_HD, c), F32)],
        ),
        out_shape=[jax.ShapeDtypeStruct((b, 1, NSA_KV, NSA_HD, c), F32),
                   jax.ShapeDtypeStruct((b, wb, NSA_KV, NSA_HD), F32),
                   jax.ShapeDtypeStruct((b, wb, NSA_KV, NSA_HD), F32)],
        compiler_params=_params("arbitrary"),
        name="nsa_sample",
    )(page_table, q, _gate_cols(glog, nq), kcb_p, vcbt, *([pool_k] * npages), *([pool_v] * npages),
      nks, nvs, nkw, nvw, win_k, win_v, qoff, slope, rmat)
    return _uncols(out, nq), wko, wvo


def _compress_pages_kernel(x_ref, pe_ref, w_ref, o_ref):
    for g in range(NSA_KV):
        acc = jnp.zeros(o_ref.shape[1:], F32)
        for d in range(0, NSA_HD, 2):
            y = jnp.concatenate([x_ref[0, :, g, d, :] + pe_ref[d:d + 1, :],
                                 x_ref[0, :, g, d + 1, :] + pe_ref[d + 1:d + 2, :]], axis=1)
            acc = acc + jnp.dot(y.astype(BF16), w_ref[d // 2], preferred_element_type=F32)
        o_ref[g] = acc


def nsa_compress_pages(cache_t, layer, pe, w, *, tp=256):
    n_pool, page = cache_t.shape[1], cache_t.shape[4]
    nblk = page // L_CMP
    tp = _row_tile(n_pool, tp)
    pe_t = jnp.tile(pe.T, (1, nblk))
    eye = jnp.eye(nblk, dtype=F32)
    wd = jnp.einsum('nm,lde->dnlme', eye, w).reshape(NSA_HD // 2, 2 * page, nblk * NSA_HD).astype(BF16)
    out = pl.pallas_call(
        _compress_pages_kernel,
        grid=(n_pool // tp,),
        in_specs=[pl.BlockSpec((1, tp, NSA_KV, NSA_HD, page), lambda i: (layer, i, 0, 0, 0)),
                  pl.BlockSpec((NSA_HD, page), lambda i: (0, 0)),
                  pl.BlockSpec((NSA_HD // 2, 2 * page, nblk * NSA_HD), lambda i: (0, 0, 0))],
        out_specs=pl.BlockSpec((NSA_KV, tp, nblk * NSA_HD), lambda i: (0, i, 0)),
        out_shape=jax.ShapeDtypeStruct((NSA_KV, n_pool, nblk * NSA_HD), F32),
        compiler_params=_params("parallel"),
        name="nsa_compress_pages",
    )(cache_t, pe_t, wd)
    return out.reshape(NSA_KV, n_pool, nblk, NSA_HD).transpose(1, 2, 0, 3).reshape(n_pool, nblk, NSA_KW)


def _row_softmax(s, mask):
    sm = jnp.where(mask, s, NEG)
    m = jnp.max(sm, axis=1, keepdims=True)
    p = jnp.where(mask, jnp.exp(sm - m), 0.0)
    l = jnp.sum(p, axis=1, keepdims=True)
    return p * jnp.where(l > 0.0, 1.0 / l, 0.0)


def _nsa_decode_kernel(pt_ref, q_ref, gl_ref, kcb_ref, vcb_ref, *refs, nq, ncb, npages, page, past, wb):
    ks_pages = refs[0:npages]
    vs_pages = refs[npages:2 * npages]
    (nks_ref, nvs_ref, nkw_ref, nvw_ref, wk_ref, wv_ref, slope_ref, expand_ref,
     o_ref, wko_ref, wvo_ref) = refs[2 * npages:]
    del pt_ref
    c = NSA_REP * nq
    nblk_lanes = _KT
    q = q_ref[0] * (NSA_HD ** -0.5)
    row = lax.broadcasted_iota(jnp.int32, (c, 1), 0)
    qpos = past + (row & (nq - 1))
    qposf = qpos.astype(F32)
    pad_rows = lambda x: jnp.concatenate([x, jnp.zeros((_KT - nq, x.shape[1]), F32)], axis=0)
    lane = lax.broadcasted_iota(jnp.int32, (1, _KT), 1)
    n_sel_keys = (npages + 1) * page
    key_all = lax.broadcasted_iota(jnp.int32, (1, n_sel_keys), 1)
    expand = expand_ref[...]
    pr = lax.broadcasted_iota(jnp.int32, (ncb, nblk_lanes), 0)
    pc = lax.broadcasted_iota(jnp.int32, (ncb, nblk_lanes), 1)
    pair = jnp.where(jnp.where(pr < ncb // 2, pr, pr - ncb // 2) == pc, 1.0, 0.0).astype(BF16)
    new_kt = pad_rows(nkw_ref[0]).T
    new_vt = pad_rows(nvw_ref[0]).T

    for g in range(NSA_KV):
        gs = slice(g * NSA_HD, (g + 1) * NSA_HD)
        qg = jnp.concatenate([q[:, (g * NSA_REP + r) * NSA_HD:(g * NSA_REP + r + 1) * NSA_HD]
                              for r in range(NSA_REP)], axis=0).astype(BF16)
        slope = slope_ref[g]
        s_c = lax.dot_general(qg, kcb_ref[0, :, gs], _NT, preferred_element_type=F32)
        cl = lax.broadcasted_iota(jnp.int32, (1, ncb), 1)
        half = ncb // 2
        c_pos = jnp.where(cl < half, 2 * cl, 2 * (cl - half) + 1) * L_CMP + (L_CMP - 1)
        d_c = qpos - c_pos
        p_c = _row_softmax(s_c - slope * d_c.astype(F32), d_c >= 0)
        o_c = jnp.dot(p_c.astype(BF16), vcb_ref[0, :, gs], preferred_element_type=F32)
        imp = p_c[0:nq]
        for r in range(1, NSA_REP):
            imp = imp + p_c[r * nq:(r + 1) * nq]
        imp_sel = _dot_exact_rhs(imp, pair)
        cur = (past + lax.broadcasted_iota(jnp.int32, (nq, 1), 0)) >> 6
        valid = lane <= cur
        forced = jnp.where(valid, jnp.where(lane == 0, 1.0, jnp.where(lane >= cur - 1, 1.0, 0.0)), 0.0)
        score = jnp.where(forced > 0.0, _BIG, jnp.where(valid, imp_sel, -1.0))
        sel = jnp.zeros((nq, nblk_lanes), F32)
        for _ in range(N_SEL):
            m = jnp.max(score, axis=1, keepdims=True)
            idx = jnp.min(jnp.where(score == m, lane, nblk_lanes + 1), axis=1, keepdims=True)
            pick = lane == idx
            sel = jnp.where(pick, 1.0, sel)
            score = jnp.where(pick, -2.0, score)
        sel = jnp.where(valid, sel, 0.0)
        sel_keys = jnp.dot(sel.astype(BF16), expand, preferred_element_type=F32)
        sel_keys = jnp.concatenate([sel_keys] * NSA_REP, axis=0)

        nk = pad_rows(nks_ref[0, :, gs]).astype(BF16)
        nv = pad_rows(nvs_ref[0, :, gs]).astype(BF16)
        s_parts = [jnp.dot(qg, ks_pages[j][0, 0, g].astype(BF16), preferred_element_type=F32) for j in range(npages)]
        s_parts.append(lax.dot_general(qg, nk, _NT, preferred_element_type=F32))
        s_s = jnp.concatenate(s_parts, axis=1)
        d_s = qpos - key_all
        p_s = _row_softmax(s_s - slope * d_s.astype(F32), jnp.where(d_s >= 0, sel_keys, 0.0) > 0.0).astype(BF16)
        o_s = jnp.dot(p_s[:, npages * page:], nv, preferred_element_type=F32)
        for j in range(npages):
            o_s = o_s + lax.dot_general(p_s[:, j * page:(j + 1) * page], vs_pages[j][0, 0, g].astype(BF16), _NT,
                                        preferred_element_type=F32)

        nkw = pad_rows(nkw_ref[0, :, gs]).astype(BF16)
        nvw = pad_rows(nvw_ref[0, :, gs]).astype(BF16)
        s_w = jnp.concatenate([jnp.dot(qg, wk_ref[0, 0, g].astype(BF16), preferred_element_type=F32),
                               lax.dot_general(qg, nkw, _NT, preferred_element_type=F32)], axis=1)
        w_pos = past - wb + lax.broadcasted_iota(jnp.int32, (1, wb + _KT), 1)
        d_w = qpos - w_pos
        p_w = _row_softmax(s_w - slope * d_w.astype(F32), jnp.where(d_w >= 0, d_w, WINDOW + 1) <= WINDOW).astype(BF16)
        o_w = (lax.dot_general(p_w[:, 0:wb], wv_ref[0, 0, g].astype(BF16), _NT, preferred_element_type=F32)
               + jnp.dot(p_w[:, wb:], nvw, preferred_element_type=F32))

        gate = _sigmoid(gl_ref[0, g])
        o_ref[0, g] = gate[:, 0:1] * o_c + gate[:, 1:2] * o_s + gate[:, 2:3] * o_w

        for src_ref, new_full, dst_ref in ((wk_ref, new_kt, wko_ref), (wv_ref, new_vt, wvo_ref)):
            new_t = pltpu.roll(new_full[gs, :], _KT - nq, axis=1)
            shifted = pltpu.roll(src_ref[0, 0, g], wb - nq, axis=1)
            dst_ref[0, g, :, 0:wb - _KT] = shifted[:, 0:wb - _KT]
            dst_ref[0, g, :, wb - _KT:wb] = jnp.where(lane >= _KT - nq, new_t, shifted[:, wb - _KT:wb])


_DECODE_BB = 4


def _nsa_decode_kernel2(pt_ref, q_ref, gl_ref, kcb_ref, vcb_ref, *refs, nq, ncb, npages, page, past, wb, nbb):
    ks_pages = refs[0:nbb * npages]
    vs_pages = refs[nbb * npages:2 * nbb * npages]
    (nks_ref, nvs_ref, nkw_ref, nvw_ref, wk_ref, wv_ref, slope_ref, expand_ref,
     o_ref, wko_ref, wvo_ref) = refs[2 * nbb * npages:]
    del pt_ref
    c = NSA_REP * nq
    nblk_lanes = _KT
    n_sel = -(-(past + nq) // L_SEL)
    row = lax.broadcasted_iota(jnp.int32, (c, 1), 0)
    qpos = past + (row & (nq - 1))
    pad_rows = lambda x: jnp.concatenate([x, jnp.zeros((_KT - nq, x.shape[1]), F32)], axis=0)
    lane = lax.broadcasted_iota(jnp.int32, (1, _KT), 1)
    n_sel_keys = (npages + 1) * page
    key_all = lax.broadcasted_iota(jnp.int32, (1, n_sel_keys), 1)
    expand = expand_ref[...]
    pr = lax.broadcasted_iota(jnp.int32, (ncb, nblk_lanes), 0)
    pc = lax.broadcasted_iota(jnp.int32, (ncb, nblk_lanes), 1)
    half = ncb // 2
    pair = jnp.where(jnp.where(pr < half, pr, pr - half) == pc, 1.0, 0.0).astype(BF16)
    cl = lax.broadcasted_iota(jnp.int32, (1, ncb), 1)
    c_pos = jnp.where(cl < half, 2 * cl, 2 * (cl - half) + 1) * L_CMP + (L_CMP - 1)
    d_c = qpos - c_pos
    d_cf = d_c.astype(F32)
    d_s = qpos - key_all
    d_sf = d_s.astype(F32)
    d_w = qpos - (past - wb + lax.broadcasted_iota(jnp.int32, (1, wb + _KT), 1))
    d_wf = d_w.astype(F32)
    in_window = jnp.where(d_w >= 0, d_w, WINDOW + 1) <= WINDOW
    cur = (past + lax.broadcasted_iota(jnp.int32, (nq, 1), 0)) >> 6
    valid = lane <= cur
    forced = jnp.where(valid, jnp.where(lane == 0, 1.0, jnp.where(lane >= cur - 1, 1.0, 0.0)), 0.0)

    for bb in range(nbb):
        q = q_ref[bb] * (NSA_HD ** -0.5)
        new_kt = pad_rows(nkw_ref[bb]).T
        new_vt = pad_rows(nvw_ref[bb]).T
        for g in range(NSA_KV):
            gs = slice(g * NSA_HD, (g + 1) * NSA_HD)
            qg = jnp.concatenate([q[:, (g * NSA_REP + r) * NSA_HD:(g * NSA_REP + r + 1) * NSA_HD]
                                  for r in range(NSA_REP)], axis=0).astype(BF16)
            slope = slope_ref[g]
            s_c = lax.dot_general(qg, kcb_ref[bb, :, gs], _NT, preferred_element_type=F32)
            p_c = _row_softmax(s_c - slope * d_cf, d_c >= 0)
            o_c = jnp.dot(p_c.astype(BF16), vcb_ref[bb, :, gs], preferred_element_type=F32)
            imp = p_c[0:nq]
            for r in range(1, NSA_REP):
                imp = imp + p_c[r * nq:(r + 1) * nq]
            imp_sel = _dot_exact_rhs(imp, pair)
            score = jnp.where(forced > 0.0, _BIG, jnp.where(valid, imp_sel, -1.0))
            before = jnp.zeros((nq, nblk_lanes), F32)
            for bi in range(n_sel):
                sb = score[:, bi:bi + 1]
                before = before + jnp.where(sb > score, 1.0, jnp.where(sb == score, jnp.where(lane > bi, 1.0, 0.0), 0.0))
            sel = jnp.where(valid, jnp.where(before < N_SEL, 1.0, 0.0), 0.0)
            sel_keys = jnp.dot(sel.astype(BF16), expand, preferred_element_type=F32)
            sel_keys = jnp.concatenate([sel_keys] * NSA_REP, axis=0)

            nk = pad_rows(nks_ref[bb, :, gs]).astype(BF16)
            nv = pad_rows(nvs_ref[bb, :, gs]).astype(BF16)
            s_parts = [jnp.dot(qg, ks_pages[bb * npages + j][0, 0, g].astype(BF16), preferred_element_type=F32)
                       for j in range(npages)]
            s_parts.append(lax.dot_general(qg, nk, _NT, preferred_element_type=F32))
            s_s = jnp.concatenate(s_parts, axis=1)
            p_s = _row_softmax(s_s - slope * d_sf, jnp.where(d_s >= 0, sel_keys, 0.0) > 0.0).astype(BF16)
            o_s = jnp.dot(p_s[:, npages * page:], nv, preferred_element_type=F32)
            for j in range(npages):
                o_s = o_s + lax.dot_general(p_s[:, j * page:(j + 1) * page],
                                            vs_pages[bb * npages + j][0, 0, g].astype(BF16), _NT,
                                            preferred_element_type=F32)

            nkw = pad_rows(nkw_ref[bb, :, gs]).astype(BF16)
            nvw = pad_rows(nvw_ref[bb, :, gs]).astype(BF16)
            s_w = jnp.concatenate([jnp.dot(qg, wk_ref[0, bb, g].astype(BF16), preferred_element_type=F32),
                                   lax.dot_general(qg, nkw, _NT, preferred_element_type=F32)], axis=1)
            p_w = _row_softmax(s_w - slope * d_wf, in_window).astype(BF16)
            o_w = (lax.dot_general(p_w[:, 0:wb], wv_ref[0, bb, g].astype(BF16), _NT, preferred_element_type=F32)
                   + jnp.dot(p_w[:, wb:], nvw, preferred_element_type=F32))

            gate = _sigmoid(gl_ref[bb, g])
            o_ref[bb, g] = gate[:, 0:1] * o_c + gate[:, 1:2] * o_s + gate[:, 2:3] * o_w

            for src_ref, new_full, dst_ref in ((wk_ref, new_kt, wko_ref), (wv_ref, new_vt, wvo_ref)):
                new_t = pltpu.roll(new_full[gs, :], _KT - nq, axis=1)
                shifted = pltpu.roll(src_ref[0, bb, g], wb - nq, axis=1)
                dst_ref[bb, g, :, 0:wb - _KT] = shifted[:, 0:wb - _KT]
                dst_ref[bb, g, :, wb - _KT:wb] = jnp.where(lane >= _KT - nq, new_t, shifted[:, wb - _KT:wb])


def nsa_decode(q, glog, kcb, vcb, pool_k, pool_v, page_table, nks, nvs, nkw, nvw, win_k, win_v, layer):
    b, nq, _ = q.shape
    ncb = kcb.shape[1]
    npages = page_table.shape[1]
    page = pool_k.shape[4]
    past = npages * page
    wb = win_k.shape[4]
    assert page == _KT and wb % _KT == 0 and nq & (nq - 1) == 0 and nq % 8 == 0 and nq <= L_SEL
    assert ncb % 2 == 0 and ncb // 2 <= _KT and -(-(past + nq) // L_SEL) <= _KT
    c = NSA_REP * nq
    _, slope = _nsa_cols(nq)
    gl = glog.reshape(b, nq, NSA_KV, NSA_REP, 3).transpose(0, 2, 3, 1, 4).reshape(b, NSA_KV, c, 3)
    nbb = _DECODE_BB if b % _DECODE_BB == 0 else 1
    per_b = lambda shape: pl.BlockSpec((nbb,) + shape, lambda i, pt: (i,) + (0,) * len(shape))
    const = lambda shape: pl.BlockSpec(shape, lambda i, pt: (0,) * len(shape))
    page_spec = lambda bb, j: pl.BlockSpec((1, 1, NSA_KV, NSA_HD, page),
                                           lambda i, pt: (layer, pt[i * nbb + bb, j], 0, 0, 0))
    page_specs = [page_spec(bb, j) for bb in range(nbb) for j in range(npages)]
    win_spec = pl.BlockSpec((1, nbb, NSA_KV, NSA_HD, wb), lambda i, pt: (layer, i, 0, 0, 0))
    n_keys = (npages + 1) * page
    expand = (jnp.arange(n_keys)[None, :] // L_SEL == jnp.arange(_KT)[:, None]).astype(BF16)
    in_specs = ([per_b((nq, NSA_W)), per_b((NSA_KV, c, 3)), per_b((ncb, NSA_KW)), per_b((ncb, NSA_KW))]
                + page_specs * 2
                + [per_b((nq, NSA_KW))] * 4 + [win_spec] * 2 + [const((NSA_KV, c, 1)), const((_KT, n_keys))])
    out, wko, wvo = pl.pallas_call(
        functools.partial(_nsa_decode_kernel2, nq=nq, ncb=ncb, npages=npages, page=page, past=past, wb=wb, nbb=nbb),
        grid_spec=pltpu.PrefetchScalarGridSpec(
            num_scalar_prefetch=1,
            grid=(b // nbb,),
            in_specs=in_specs,
            out_specs=[per_b((NSA_KV, c, NSA_HD)), per_b((NSA_KV, NSA_HD, wb)), per_b((NSA_KV, NSA_HD, wb))],
        ),
        out_shape=[jax.ShapeDtypeStruct((b, NSA_KV, c, NSA_HD), F32),
                   jax.ShapeDtypeStruct((b, NSA_KV, NSA_HD, wb), F32),
                   jax.ShapeDtypeStruct((b, NSA_KV, NSA_HD, wb), F32)],
        compiler_params=_params("arbitrary"),
        name="nsa_decode",
    )(page_table, q, gl, _even_odd(kcb).astype(BF16), _even_odd(vcb).astype(BF16),
      *([pool_k] * (nbb * npages)), *([pool_v] * (nbb * npages)), nks, nvs, nkw, nvw, win_k, win_v,
      slope.reshape(NSA_KV, c, 1), expand)
    o = out.reshape(b, NSA_KV, NSA_REP, nq, NSA_HD).transpose(0, 3, 1, 2, 4).reshape(b, nq, NSA_W)
    return o, wko, wvo


def _xattn_cache_kernel(q_ref, k_ref, v_ref, o_ref, *, nq):
    scale = XA_HD ** -0.5
    q = jnp.concatenate([q_ref[0, :, h * XA_HD:(h + 1) * XA_HD] for h in range(XA_HEADS)], axis=0)
    k = k_ref[0, 0].astype(BF16)
    v = v_ref[0, 0].astype(BF16)
    s = lax.dot_general(q.astype(BF16), k, _NT, preferred_element_type=F32) * scale
    col_h = lax.broadcasted_iota(jnp.int32, s.shape, 1) & (XA_HEADS - 1)
    row_h = lax.broadcasted_iota(jnp.int32, s.shape, 0) >> (nq.bit_length() - 1)
    mine = col_h == row_h
    m = jnp.max(jnp.where(mine, s, NEG), axis=1, keepdims=True)
    p = jnp.where(mine, jnp.exp(s - m), 0.0)
    p = p / jnp.sum(p, axis=1, keepdims=True)
    o = jnp.dot(p.astype(BF16), v, preferred_element_type=F32)
    for h in range(XA_HEADS):
        o_ref[0, :, h * XA_HD:(h + 1) * XA_HD] = o[h * nq:(h + 1) * nq]


def xattn_cache(q, cache_k, cache_v, layer):
    b, nq, w = q.shape
    m = cache_k.shape[2]
    assert XA_HEADS & (XA_HEADS - 1) == 0 and nq % 8 == 0
    kv = lambda a: a.reshape(a.shape[0], b, m * XA_HEADS, XA_HD)
    kv_spec = pl.BlockSpec((1, 1, m * XA_HEADS, XA_HD), lambda i: (layer, i, 0, 0))
    return pl.pallas_call(
        functools.partial(_xattn_cache_kernel, nq=nq),
        grid=(b,),
        in_specs=[pl.BlockSpec((1, nq, w), lambda i: (i, 0, 0)), kv_spec, kv_spec],
        out_specs=pl.BlockSpec((1, nq, w), lambda i: (i, 0, 0)),
        out_shape=jax.ShapeDtypeStruct((b, nq, w), F32),
        compiler_params=_params("parallel"),
        name="xattn_cache",
    )(q, kv(cache_k), kv(cache_v))


_HALO_M = 8
_TN = (((0,), (0,)), ((), ()))


def _softplus(x):
    return jnp.maximum(x, 0.0) + jnp.log1p(jnp.exp(-jnp.abs(x)))


def _ssd_kernel(xbc_ref, z_ref, sm_ref, dtt_ref, cs_ref, h0_ref, cw_ref, cb_ref, dtb_ref, dtbt_ref,
                al_ref, alt_ref, dsk_ref, ng_ref, y_ref, ncs_ref, hf_ref, ext_ref, h_ref, yh_ref, *, ql, dt_col):
    c = pl.program_id(1)
    nc = pl.num_programs(1)

    @pl.when(c == 0)
    def _():
        ext_ref[...] = jnp.zeros_like(ext_ref)
        ext_ref[_HALO_M - (M_CONV_W - 1):_HALO_M, :] = cs_ref[0]
        h_ref[...] = h0_ref[0]

    @pl.when(c > 0)
    def _():
        ext_ref[0:_HALO_M, :] = ext_ref[ql:ql + _HALO_M, :]

    ext_ref[_HALO_M:_HALO_M + ql, :] = xbc_ref[0]
    acc = jnp.zeros((ql, M_CONV_DIM), F32)
    for k in range(M_CONV_W):
        off = _HALO_M - (M_CONV_W - 1) + k
        acc = acc + ext_ref[off:off + ql, :].astype(BF16).astype(F32) * cw_ref[k:k + 1, :]
    xbc = _silu(acc + cb_ref[...])
    xs = xbc[:, 0:M_DIN]
    bm = xbc[:, M_DIN:M_DIN + M_GROUPS * M_DSTATE]
    cm = xbc[:, M_DIN + M_GROUPS * M_DSTATE:M_CONV_DIM]

    dt = _softplus(sm_ref[0, :, dt_col:dt_col + M_HEADS] + dtb_ref[...])
    dtt = _softplus(dtt_ref[0] + dtbt_ref[...])
    dta = dt * (-jnp.exp(al_ref[...]))
    dtat = dtt * (-jnp.exp(alt_ref[...]))
    ti = lax.broadcasted_iota(jnp.int32, (ql, ql), 0)
    si = lax.broadcasted_iota(jnp.int32, (ql, ql), 1)
    causal = si <= ti
    cum = _dot_exact_lhs(jnp.where(causal, 1.0, 0.0).astype(BF16), dta)
    cumt = _dot_exact_rhs(dtat, jnp.where(ti <= si, 1.0, 0.0).astype(BF16))
    cum_last = cum[ql - 1:ql, :]
    edec = jnp.exp(cum)
    eend = jnp.exp(cum_last - cum)
    elast = jnp.exp(cum_last)

    rep = M_HEADS // M_GROUPS
    for gi in range(M_GROUPS):
        b_g = bm[:, gi * M_DSTATE:(gi + 1) * M_DSTATE]
        c_g = cm[:, gi * M_DSTATE:(gi + 1) * M_DSTATE].astype(BF16)
        cb = lax.dot_general(c_g, b_g.astype(BF16), _NT, preferred_element_type=F32)
        for hh in range(rep):
            h = gi * rep + hh
            hs = slice(h * M_HDIM, (h + 1) * M_HDIM)
            lmat = jnp.where(causal, jnp.exp(cum[:, h:h + 1] - cumt[h:h + 1, :]), 0.0)
            x_h = xs[:, hs]
            xdt = (x_h * dt[:, h:h + 1]).astype(BF16)
            y_diag = jnp.dot((cb * lmat).astype(BF16), xdt, preferred_element_type=F32)
            h_in = h_ref[h]
            y_off = lax.dot_general(c_g, h_in.astype(BF16), _NT, preferred_element_type=F32) * edec[:, h:h + 1]
            bd = (b_g * eend[:, h:h + 1]).astype(BF16)
            s_chunk = lax.dot_general(xdt, bd, _TN, preferred_element_type=F32)
            h_ref[h] = elast[:, h:h + 1] * h_in + s_chunk
            yh_ref[:, hs] = y_diag + y_off + dsk_ref[:, hs] * x_h

    yz = yh_ref[...] * _silu(z_ref[0])
    y_ref[0] = _rms(yz, ng_ref[...])

    @pl.when(c == nc - 1)
    def _():
        ncs_ref[0] = ext_ref[_HALO_M + ql - (M_CONV_W - 1):_HALO_M + ql, :]
        hf_ref[0] = h_ref[...]


def ssd_mixer(xbc, z, small, dt_col, conv_state, h0, conv_w, conv_b, dt_bias, a_log, d_skip, norm_g, *, ql):
    b, t, _ = xbc.shape
    nc = t // ql
    sw = small.shape[2]
    dtt = small[:, :, dt_col:dt_col + M_HEADS].transpose(0, 2, 1)
    const = lambda shape: pl.BlockSpec(shape, lambda i, j: (0,) * len(shape))
    per_b = lambda shape: pl.BlockSpec((1,) + shape, lambda i, j: (i,) + (0,) * len(shape))
    row = lambda x: x.reshape(1, -1)
    colv = lambda x: x.reshape(-1, 1)
    return pl.pallas_call(
        functools.partial(_ssd_kernel, ql=ql, dt_col=dt_col),
        grid=(b, nc),
        in_specs=[pl.BlockSpec((1, ql, M_CONV_DIM), lambda i, j: (i, j, 0)),
                  pl.BlockSpec((1, ql, M_DIN), lambda i, j: (i, j, 0)),
                  pl.BlockSpec((1, ql, sw), lambda i, j: (i, j, 0)),
                  pl.BlockSpec((1, M_HEADS, ql), lambda i, j: (i, 0, j)),
                  per_b((M_CONV_W - 1, M_CONV_DIM)), per_b((M_HEADS, M_HDIM, M_DSTATE)),
                  const((M_CONV_W, M_CONV_DIM)), const((1, M_CONV_DIM)),
                  const((1, M_HEADS)), const((M_HEADS, 1)), const((1, M_HEADS)), const((M_HEADS, 1)),
                  const((1, M_DIN)), const((1, M_DIN))],
        out_specs=[pl.BlockSpec((1, ql, M_DIN), lambda i, j: (i, j, 0)),
                   per_b((M_CONV_W - 1, M_CONV_DIM)), per_b((M_HEADS, M_HDIM, M_DSTATE))],
        out_shape=[jax.ShapeDtypeStruct((b, t, M_DIN), F32),
                   jax.ShapeDtypeStruct((b, M_CONV_W - 1, M_CONV_DIM), F32),
                   jax.ShapeDtypeStruct((b, M_HEADS, M_HDIM, M_DSTATE), F32)],
        scratch_shapes=[pltpu.VMEM((_HALO_M + ql, M_CONV_DIM), F32),
                        pltpu.VMEM((M_HEADS, M_HDIM, M_DSTATE), F32),
                        pltpu.VMEM((ql, M_DIN), F32)],
        compiler_params=_params("parallel", "arbitrary"),
        name="ssd_mixer",
    )(xbc, z, small, dtt, conv_state, h0, conv_w, row(conv_b), row(dt_bias), colv(dt_bias),
      row(a_log), colv(a_log), row(jnp.repeat(d_skip, M_HDIM)), row(norm_g))


_SMALL_W = 128
_OD_SPLITS = (NSA_W, 6 * NSA_KW, M_DIN, M_CONV_DIM, _SMALL_W)


def _odd_w_in(w):
    o_kv = NSA_W
    o_gate = o_kv + 6 * NSA_KW
    o_z = o_gate + 3 * NSA_HEADS
    o_xbc = o_z + M_DIN
    o_dt = o_xbc + M_CONV_DIM
    pad = jnp.zeros((w.shape[0], _SMALL_W - 3 * NSA_HEADS - M_HEADS), F32)
    return jnp.concatenate([w[:, :o_gate], w[:, o_z:o_xbc], w[:, o_xbc:o_dt],
                            w[:, o_gate:o_z], w[:, o_dt:], pad], axis=1)


def kernel(x_prompt, x_sample, state_conv_a, state_conv_b, cache_cmp_k, cache_cmp_v, cache_sel_k, cache_sel_v, cache_win_k, cache_win_v, state_ssm, state_ssm_conv, cache_mem_k, cache_mem_v, page_table, mem_prompt, norm_mix, norm_xattn, norm_ffn, norm_final, ev_w_in, ev_conv_a, ev_conv_b, ev_conv_b_bias, ev_ln_g, ev_ln_b, ev_w_out, od_w_in, od_cmp_pe, od_cmp_wk, od_cmp_wv, od_ssm_conv_w, od_ssm_conv_b, od_dt_bias, od_a_log, od_d_skip, od_ssm_norm, od_w_out, xa_wq, xa_wk, xa_wv, xa_wo, moe_wg, moe_bg, moe_we, moe_be, moe_w1, moe_w3, moe_w2):
    bp, tp, d = x_prompt.shape
    bs, ts, _ = x_sample.shape
    n_p, n_s = bp * tp, bs * ts
    n_mem = mem_prompt.shape[1]
    depth = norm_mix.shape[0]
    n_pool, page = cache_cmp_k.shape[1:3]
    wb = cache_win_k.shape[2]
    dt_col = 3 * NSA_HEADS

    def groups(a):
        return a[:n_p].reshape(bp, tp, a.shape[-1]), a[n_p:].reshape(bs, ts, a.shape[-1])

    def rows(a_p, a_s):
        return jnp.concatenate([a_p.reshape(n_p, a_p.shape[-1]), a_s.reshape(n_s, a_s.shape[-1])], axis=0)

    h = rows(x_prompt, x_sample)
    out = {k: [] for k in ("ca_p", "ca_s", "cb_p", "cb_s", "wk_p", "wk_s", "wv_p", "wv_s",
                           "sm_p", "sm_s", "sc_p", "sc_s", "mk_p", "mv_p")}
    rows_p = [[], [], [], []]
    rows_s = [[], [], [], []]
    for i in range(depth):
        j = i // 2
        if i % 2 == 0:
            u_p, u_s = groups(norm_matmul(h, norm_mix[i], ev_w_in[j]))
            ev = (ev_conv_a[j], ev_conv_b[j], ev_conv_b_bias[j], ev_ln_g[j], ev_ln_b[j])
            y_p, na_p, nb_p = even_conv(u_p, jnp.zeros((bp, CONV_A_W - 1, D_A), F32),
                                        jnp.zeros((bp, CONV_B_W - 1, D_B), F32), *ev)
            y_s, na_s, nb_s = even_conv(u_s, state_conv_a[j], state_conv_b[j], *ev)
            h = matmul_res([rows(y_p, y_s)], [ev_w_out[j]], h)
            out["ca_p"].append(na_p)
            out["ca_s"].append(na_s)
            out["cb_p"].append(nb_p)
            out["cb_s"].append(nb_s)
        else:
            uq, ukv, uz, uxbc, usm = norm_matmul(h, norm_mix[i], _odd_w_in(od_w_in[j]), splits=_OD_SPLITS)
            q_p, q_s = groups(uq)
            kv_p, kv_s = groups(ukv)
            z_p, z_s = groups(uz)
            xbc_p, xbc_s = groups(uxbc)
            sm_p, sm_s = groups(usm)
            part = lambda a, k: a[:, :, k * NSA_KW:(k + 1) * NSA_KW]
            kvp = [part(kv_p, k) for k in range(6)]
            kvs = [part(kv_s, k) for k in range(6)]
            pe, wck, wcv = od_cmp_pe[j], od_cmp_wk[j], od_cmp_wv[j]
            mw = (od_ssm_conv_w[j], od_ssm_conv_b[j], od_dt_bias[j], od_a_log[j], od_d_skip[j], od_ssm_norm[j])
            blocks = lambda a: a.reshape(-1, L_CMP, NSA_KW)
            ncb = tp // L_CMP
            kcb_p = nsa_compress(blocks(kvp[0][:, :ncb * L_CMP]), pe, wck).reshape(bp, ncb, NSA_KW)
            vcb_p = nsa_compress(blocks(kvp[1][:, :ncb * L_CMP]), pe, wcv).reshape(bp, ncb, NSA_KW)
            o_p = nsa_prompt(q_p, sm_p[:, :, :dt_col], kcb_p, vcb_p, kvp[2], kvp[3], kvp[4], kvp[5])
            keep = min(WINDOW, tp)
            y_p, nsc_p, nsm_p = ssd_mixer(xbc_p, z_p, sm_p, dt_col, jnp.zeros((bp, M_CONV_W - 1, M_CONV_DIM), F32),
                                          jnp.zeros((bp, M_HEADS, M_HDIM, M_DSTATE), F32), *mw, ql=128)
            tokens_last = lambda a: jnp.transpose(a, (0, 1, 3, 4, 2))
            kcp = nsa_compress_pages(tokens_last(cache_cmp_k), j, pe, wck)
            vcp = nsa_compress_pages(tokens_last(cache_cmp_v), j, pe, wcv)
            kcb_s = kcp[page_table].reshape(bs, -1, NSA_KW)
            vcb_s = vcp[page_table].reshape(bs, -1, NSA_KW)
            o_s, nwk_s, nwv_s = nsa_decode(
                q_s, sm_s[:, :, :dt_col], kcb_s, vcb_s, tokens_last(cache_sel_k), tokens_last(cache_sel_v),
                page_table, kvs[2], kvs[3], kvs[4], kvs[5], tokens_last(cache_win_k), tokens_last(cache_win_v), j)
            nwk_s = jnp.transpose(nwk_s, (0, 3, 1, 2))
            nwv_s = jnp.transpose(nwv_s, (0, 3, 1, 2))
            y_s, nsc_s, nsm_s = ssd_mixer(xbc_s, z_s, sm_s, dt_col, state_ssm_conv[j], state_ssm[j], *mw, ql=ts)
            w_out = od_w_out[j]
            h = matmul_res([rows(o_p, o_s), rows(y_p, y_s)], [w_out[:NSA_W], w_out[NSA_W:]], h)
            heads = lambda a: a.reshape(a.shape[0], a.shape[1], NSA_KV, NSA_HD)
            for k in range(4):
                rows_p[k].append(heads(kvp[k]))
                rows_s[k].append(heads(kvs[k]))
            out["wk_p"].append(heads(kvp[4][:, tp - keep:]))
            out["wv_p"].append(heads(kvp[5][:, tp - keep:]))
            out["wk_s"].append(nwk_s)
            out["wv_s"].append(nwv_s)
            out["sc_p"].append(nsc_p)
            out["sc_s"].append(nsc_s)
            out["sm_p"].append(nsm_p)
            out["sm_s"].append(nsm_s)
        mk, mv = norm_matmul(mem_prompt.reshape(bp * n_mem, d), None,
                             jnp.concatenate([xa_wk[i], xa_wv[i]], axis=1), norm=False,
                             splits=(XA_HEADS * XA_HD, XA_HEADS * XA_HD))
        mk = mk.reshape(bp, n_mem, XA_HEADS * XA_HD)
        mv = mv.reshape(bp, n_mem, XA_HEADS * XA_HD)
        out["mk_p"].append(mk.reshape(bp, n_mem, XA_HEADS, XA_HD))
        out["mv_p"].append(mv.reshape(bp, n_mem, XA_HEADS, XA_HD))
        qx_p, qx_s = groups(norm_matmul(h, norm_xattn[i], xa_wq[i]))
        ox_p = xattn(qx_p, mk, mv)
        ox_s = xattn_cache(qx_s, cache_mem_k, cache_mem_v, i)
        h = matmul_res([rows(ox_p, ox_s)], [xa_wo[i]], h)
        h = moe_layer(h, norm_ffn[i], moe_wg[i], moe_bg[i], moe_we[i], moe_be[i], moe_w1, moe_w3, moe_w2, i)
    y = rmsnorm_rows(h, norm_final)
    y_prompt = y[:n_p].reshape(bp, tp, d)
    y_sample = y[n_p:].reshape(bs, ts, d)
    st = lambda k: jnp.stack(out[k])
    return (y_prompt, y_sample, st("ca_p"), st("ca_s"), st("cb_p"), st("cb_s"),
            jnp.stack(rows_p[0]), jnp.stack(rows_s[0]), jnp.stack(rows_p[1]), jnp.stack(rows_s[1]),
            jnp.stack(rows_p[2]), jnp.stack(rows_s[2]), jnp.stack(rows_p[3]), jnp.stack(rows_s[3]),
            st("wk_p"), st("wk_s"), st("wv_p"), st("wv_s"), st("sm_p"), st("sm_s"), st("sc_p"), st("sc_s"),
            st("mk_p"), st("mv_p"))
```

```python
import functools

import jax
import jax.numpy as jnp
from jax import lax
from jax.experimental import pallas as pl
from jax.experimental.pallas import tpu as pltpu

F32 = jnp.float32
BF16 = jnp.bfloat16
EPS = 1e-6
NEG = -1e30
VMEM_LIMIT = 56 * 1024 * 1024

D_A = 512
D_B = 512
CONV_A_W = 3
CONV_B_W = 31
NSA_HEADS = 8
NSA_HD = 64
NSA_KV = 2
NSA_REP = NSA_HEADS // NSA_KV
NSA_W = NSA_HEADS * NSA_HD
NSA_KW = NSA_KV * NSA_HD
L_CMP = 32
L_SEL = 64
N_SEL = 16
WINDOW = 512
M_DIN = 512
M_HDIM = 64
M_HEADS = 8
M_DSTATE = 64
M_GROUPS = 2
M_CONV_W = 4
M_CONV_DIM = M_DIN + 2 * M_GROUPS * M_DSTATE
XA_HEADS = 4
XA_HD = 128
MOE_GROUPS = 4
MOE_EPG = 8
MOE_E = 32
MOE_TOPK = 2


def _params(*sem):
    return pltpu.CompilerParams(dimension_semantics=sem, vmem_limit_bytes=VMEM_LIMIT)


def _row_tile(n, pref):
    t = min(n, pref)
    while n % t or (t % 8 and t != n):
        t -= 1
    return t


def _bdot(a, b):
    return jnp.dot(a.astype(BF16), b.astype(BF16), preferred_element_type=F32)


def _split3(a):
    hi = a.astype(BF16)
    r1 = a - hi.astype(F32)
    mid = r1.astype(BF16)
    lo = (r1 - mid.astype(F32)).astype(BF16)
    return hi, mid, lo


def _dot_exact_rhs(a, b_bf16):
    hi, mid, lo = _split3(a)
    d = lambda x: jnp.dot(x, b_bf16, preferred_element_type=F32)
    return d(hi) + d(mid) + d(lo)


def _dot_exact_lhs(a_bf16, b):
    hi, mid, lo = _split3(b)
    d = lambda x: jnp.dot(a_bf16, x, preferred_element_type=F32)
    return d(hi) + d(mid) + d(lo)


def _rms(x, g):
    ms = jnp.mean(x * x, axis=-1, keepdims=True)
    return x * lax.rsqrt(ms + EPS) * g


def _sigmoid(x):
    return 1.0 / (1.0 + jnp.exp(-x))


def _silu(x):
    return x * _sigmoid(x)


def _norm_matmul_kernel(x_ref, g_ref, w_ref, *o_refs, norm, splits):
    x = x_ref[...]
    if norm:
        x = _rms(x, g_ref[...])
    res = jnp.dot(x.astype(BF16), w_ref[...].astype(BF16), preferred_element_type=F32)
    off = 0
    for o_ref, width in zip(o_refs, splits):
        o_ref[...] = res[:, off:off + width]
        off += width


def norm_matmul(x, g, w, *, norm=True, splits=None, tm=512):
    n, k = x.shape
    m = w.shape[1]
    tm = _row_tile(n, tm)
    if g is None:
        g = jnp.ones((k,), F32)
    widths = (m,) if splits is None else tuple(splits)
    assert sum(widths) == m
    outs = pl.pallas_call(
        functools.partial(_norm_matmul_kernel, norm=norm, splits=widths),
        grid=(n // tm,),
        in_specs=[pl.BlockSpec((tm, k), lambda i: (i, 0)),
                  pl.BlockSpec((1, k), lambda i: (0, 0)),
                  pl.BlockSpec((k, m), lambda i: (0, 0))],
        out_specs=[pl.BlockSpec((tm, wd), lambda i: (i, 0)) for wd in widths],
        out_shape=[jax.ShapeDtypeStruct((n, wd), F32) for wd in widths],
        compiler_params=_params("parallel"),
        name="norm_matmul",
    )(x, g.reshape(1, k), w)
    return outs[0] if splits is None else outs


def _rmsnorm_kernel(x_ref, g_ref, o_ref):
    o_ref[...] = _rms(x_ref[...], g_ref[...])


def rmsnorm_rows(x, g, *, tm=512):
    n, k = x.shape
    tm = _row_tile(n, tm)
    return pl.pallas_call(
        _rmsnorm_kernel,
        grid=(n // tm,),
        in_specs=[pl.BlockSpec((tm, k), lambda i: (i, 0)), pl.BlockSpec((1, k), lambda i: (0, 0))],
        out_specs=pl.BlockSpec((tm, k), lambda i: (i, 0)),
        out_shape=jax.ShapeDtypeStruct((n, k), F32),
        compiler_params=_params("parallel"),
        name="rmsnorm_rows",
    )(x, g.reshape(1, k))


def _matmul_res_kernel(*refs, n_in):
    res_ref = refs[2 * n_in]
    o_ref = refs[2 * n_in + 1]
    acc = res_ref[...]
    for j in range(n_in):
        acc = acc + jnp.dot(refs[2 * j][...].astype(BF16), refs[2 * j + 1][...].astype(BF16),
                            preferred_element_type=F32)
    o_ref[...] = acc


def matmul_res(xs, ws, res, *, tm=512):
    n, m = res.shape
    tm = _row_tile(n, tm)
    in_specs, args = [], []
    for x, w in zip(xs, ws):
        k = x.shape[1]
        in_specs += [pl.BlockSpec((tm, k), lambda i: (i, 0)), pl.BlockSpec((k, m), lambda i: (0, 0))]
        args += [x, w]
    in_specs.append(pl.BlockSpec((tm, m), lambda i: (i, 0)))
    return pl.pallas_call(
        functools.partial(_matmul_res_kernel, n_in=len(xs)),
        grid=(n // tm,),
        in_specs=in_specs,
        out_specs=pl.BlockSpec((tm, m), lambda i: (i, 0)),
        out_shape=jax.ShapeDtypeStruct((n, m), F32),
        compiler_params=_params("parallel"),
        name="matmul_res",
    )(*args, res)


_TM = 512


def _rowwise_call(body, row_inputs, shared, out_widths, out_dtypes, name, joint_outputs=False, transposed=()):
    ns = [a.shape[0] for a in row_inputs[0]]
    assert all(n % _TM == 0 for n in ns)
    nbs = [n // _TM for n in ns]
    starts = [sum(nbs[:g]) for g in range(len(ns))]
    n_groups, n_row, n_out = len(ns), len(row_inputs), len(out_widths)

    def group_map(g):
        return lambda i: (jnp.clip(i - starts[g], 0, nbs[g] - 1), 0)

    in_specs, args = [], []
    for k in range(n_row):
        for g in range(n_groups):
            a = row_inputs[k][g]
            in_specs.append(pl.BlockSpec((_TM, a.shape[1]), group_map(g)))
            args.append(a)
    for a in shared:
        in_specs.append(pl.BlockSpec(a.shape, lambda i, nd=a.ndim: (0,) * nd))
        args.append(a)
    if joint_outputs:
        out_specs = [pl.BlockSpec((_TM, w), lambda i: (i, 0)) for w in out_widths]
        out_shape = [jax.ShapeDtypeStruct((sum(ns), w), dt) for w, dt in zip(out_widths, out_dtypes)]
    else:
        def group_map_t(g):
            return lambda i: (0, jnp.clip(i - starts[g], 0, nbs[g] - 1))

        out_specs, out_shape = [], []
        for j, (w, dt) in enumerate(zip(out_widths, out_dtypes)):
            for g in range(n_groups):
                if j in transposed:
                    out_specs.append(pl.BlockSpec((w, _TM), group_map_t(g)))
                    out_shape.append(jax.ShapeDtypeStruct((w, ns[g]), dt))
                else:
                    out_specs.append(pl.BlockSpec((_TM, w), group_map(g)))
                    out_shape.append(jax.ShapeDtypeStruct((ns[g], w), dt))

    def kernel(*refs):
        x_refs = refs[:n_row * n_groups]
        s_refs = refs[n_row * n_groups:n_row * n_groups + len(shared)]
        o_refs = refs[n_row * n_groups + len(shared):]
        i = pl.program_id(0)
        for g in range(n_groups):
            @pl.when((i >= starts[g]) & (i < starts[g] + nbs[g]))
            def _(g=g):
                vals = body([x_refs[k * n_groups + g][...] for k in range(n_row)], s_refs)
                for j, v in enumerate(vals):
                    o_ref = o_refs[j] if joint_outputs else o_refs[j * n_groups + g]
                    o_ref[...] = v.astype(o_ref.dtype)

    outs = pl.pallas_call(
        kernel, grid=(sum(nbs),), in_specs=in_specs, out_specs=out_specs, out_shape=out_shape,
        compiler_params=_params("arbitrary"), name=name)(*args)
    if joint_outputs:
        return list(outs)
    return [list(outs[j * n_groups:(j + 1) * n_groups]) for j in range(n_out)]


def g_norm_matmul(hs, g, w, *, splits=None, norm=True):
    widths = (w.shape[1],) if splits is None else tuple(splits)
    assert sum(widths) == w.shape[1]

    def body(xs, s_refs):
        x = _rms(xs[0], s_refs[0][...]) if norm else xs[0]
        res = jnp.dot(x.astype(BF16), s_refs[1][...].astype(BF16), preferred_element_type=F32)
        offs = [sum(widths[:j]) for j in range(len(widths))]
        return [res[:, o:o + wd] for o, wd in zip(offs, widths)]

    k = hs[0].shape[1]
    gv = jnp.ones((1, k), F32) if g is None else g.reshape(1, k)
    outs = _rowwise_call(body, [hs], [gv, w], widths, [F32] * len(widths), "norm_matmul")
    return outs[0] if splits is None else outs


def g_odd_in_proj(hs, g, w, w_vt, k_cols):
    widths = _OD_SPLITS + (NSA_KW, NSA_KW, NSA_KW, NSA_KW)
    offs = [sum(_OD_SPLITS[:j]) for j in range(len(_OD_SPLITS))]

    def body(xs, s_refs):
        x = _rms(xs[0], s_refs[0][...]).astype(BF16)
        res = jnp.dot(x, s_refs[1][...].astype(BF16), preferred_element_type=F32)
        vt = lax.dot_general(s_refs[2][...].astype(BF16), x, _NT, preferred_element_type=F32)
        outs = [res[:, o:o + wd] for o, wd in zip(offs, _OD_SPLITS)]
        outs += [res[:, c:c + NSA_KW] for c in k_cols]
        outs += [vt[0:NSA_KW], vt[NSA_KW:2 * NSA_KW]]
        return outs

    n_main = len(_OD_SPLITS)
    dts = [F32] * n_main + [BF16] * 4
    return _rowwise_call(body, [hs], [g.reshape(1, -1), w, w_vt], widths, dts, "odd_in_proj",
                         transposed=(n_main + 2, n_main + 3))


def g_matmul_res(xs_list, ws, hs):
    def body(xs, s_refs):
        acc = xs[-1]
        for j in range(len(ws)):
            acc = acc + jnp.dot(xs[j].astype(BF16), s_refs[j][...].astype(BF16), preferred_element_type=F32)
        return [acc]

    return _rowwise_call(body, list(xs_list) + [hs], list(ws), (hs[0].shape[1],), [F32], "matmul_res")[0]


def g_rmsnorm(hs, g):
    body = lambda xs, s_refs: [_rms(xs[0], s_refs[0][...])]
    return _rowwise_call(body, [hs], [g.reshape(1, -1)], (hs[0].shape[1],), [F32], "rmsnorm_rows")[0]


def g_moe_router(hs, g, w_router):
    def body(xs, s_refs):
        xb = _rms(xs[0], s_refs[0][...]).astype(BF16)
        logits = jnp.dot(xb, s_refs[1][...].astype(BF16), preferred_element_type=F32)
        half = xb.shape[1] // 2
        bits = lambda v: lax.bitcast_convert_type(v.astype(F32), jnp.uint32)
        words = (bits(xb[:, half:]) & jnp.uint32(0xFFFF0000)) | (bits(xb[:, :half]) >> 16)
        return [logits, lax.bitcast_convert_type(words, F32)]

    k = hs[0].shape[1]
    return _rowwise_call(body, [hs], [g.reshape(1, k), w_router], (_ROUTER_W, k // 2), [F32, F32], "moe_router",
                         joint_outputs=True)


_HALO_A = 8
_HALO_B = 32


def _even_conv_kernel(u_ref, sa_ref, sb_ref, wa_ref, wb_ref, bb_ref, lg_ref, lb_ref,
                      y_ref, na_ref, nb_ref, ea_ref, eb_ref, ear_ref, ebr_ref, *, tt):
    rnd = lambda x: x.astype(BF16).astype(F32)
    t = pl.program_id(1)
    nt = pl.num_programs(1)

    @pl.when(t == 0)
    def _():
        ea_ref[...] = jnp.zeros_like(ea_ref)
        eb_ref[...] = jnp.zeros_like(eb_ref)
        ea_ref[_HALO_A - (CONV_A_W - 1):_HALO_A, :] = sa_ref[0]
        eb_ref[_HALO_B - (CONV_B_W - 1):_HALO_B, :] = sb_ref[0]
        ear_ref[...] = rnd(ea_ref[...])
        ebr_ref[...] = rnd(eb_ref[...])

    @pl.when(t > 0)
    def _():
        ea_ref[0:_HALO_A, :] = ea_ref[tt:tt + _HALO_A, :]
        eb_ref[0:_HALO_B, :] = eb_ref[tt:tt + _HALO_B, :]
        ear_ref[0:_HALO_A, :] = ear_ref[tt:tt + _HALO_A, :]
        ebr_ref[0:_HALO_B, :] = ebr_ref[tt:tt + _HALO_B, :]

    xa = u_ref[0, :, 0:D_A]
    ba = u_ref[0, :, D_A:2 * D_A]
    ca = u_ref[0, :, 2 * D_A:3 * D_A]
    pb = u_ref[0, :, 3 * D_A:3 * D_A + D_B]
    gb = u_ref[0, :, 3 * D_A + D_B:3 * D_A + 2 * D_B]
    va = ca * xa
    vb = pb * _sigmoid(gb)
    ea_ref[_HALO_A:_HALO_A + tt, :] = va
    eb_ref[_HALO_B:_HALO_B + tt, :] = vb
    ear_ref[_HALO_A:_HALO_A + tt, :] = rnd(va)
    ebr_ref[_HALO_B:_HALO_B + tt, :] = rnd(vb)

    acc = jnp.zeros((tt, D_A), F32)
    for k in range(CONV_A_W):
        off = _HALO_A - (CONV_A_W - 1) + k
        acc = acc + ear_ref[off:off + tt, :] * wa_ref[k:k + 1, :]
    y_ref[0, :, 0:D_A] = ba * acc

    acc = jnp.zeros((tt, D_B), F32)
    for k in range(CONV_B_W):
        off = _HALO_B - (CONV_B_W - 1) + k
        acc = acc + ebr_ref[off:off + tt, :] * wb_ref[k:k + 1, :]
    acc = acc + bb_ref[...]
    mu = jnp.mean(acc, axis=-1, keepdims=True)
    xc = acc - mu
    var = jnp.mean(xc * xc, axis=-1, keepdims=True)
    yb = xc * lax.rsqrt(var + EPS) * lg_ref[...] + lb_ref[...]
    y_ref[0, :, D_A:D_A + D_B] = _silu(yb)

    @pl.when(t == nt - 1)
    def _():
        na_ref[0] = ea_ref[_HALO_A + tt - (CONV_A_W - 1):_HALO_A + tt, :]
        nb_ref[0] = eb_ref[_HALO_B + tt - (CONV_B_W - 1):_HALO_B + tt, :]


def even_conv(u, sa, sb, wa, wb, bb, lg, lb, *, tt=256):
    b, t, w = u.shape
    tt = _row_tile(t, tt)
    full = lambda shape: pl.BlockSpec(shape, lambda i, j: (0,) * len(shape))
    return pl.pallas_call(
        functools.partial(_even_conv_kernel, tt=tt),
        grid=(b, t // tt),
        in_specs=[pl.BlockSpec((1, tt, w), lambda i, j: (i, j, 0)),
                  pl.BlockSpec((1, CONV_A_W - 1, D_A), lambda i, j: (i, 0, 0)),
                  pl.BlockSpec((1, CONV_B_W - 1, D_B), lambda i, j: (i, 0, 0)),
                  full((CONV_A_W, D_A)), full((CONV_B_W, D_B)), full((1, D_B)),
                  full((1, D_B)), full((1, D_B))],
        out_specs=[pl.BlockSpec((1, tt, D_A + D_B), lambda i, j: (i, j, 0)),
                   pl.BlockSpec((1, CONV_A_W - 1, D_A), lambda i, j: (i, 0, 0)),
                   pl.BlockSpec((1, CONV_B_W - 1, D_B), lambda i, j: (i, 0, 0))],
        out_shape=[jax.ShapeDtypeStruct((b, t, D_A + D_B), F32),
                   jax.ShapeDtypeStruct((b, CONV_A_W - 1, D_A), F32),
                   jax.ShapeDtypeStruct((b, CONV_B_W - 1, D_B), F32)],
        scratch_shapes=[pltpu.VMEM((_HALO_A + tt, D_A), F32), pltpu.VMEM((_HALO_B + tt, D_B), F32),
                        pltpu.VMEM((_HALO_A + tt, D_A), F32), pltpu.VMEM((_HALO_B + tt, D_B), F32)],
        compiler_params=_params("parallel", "arbitrary"),
        name="even_conv",
    )(u, sa, sb, wa, wb, bb.reshape(1, D_B), lg.reshape(1, D_B), lb.reshape(1, D_B))


def _xattn_kernel(q_ref, k_ref, v_ref, o_ref, *, cache_layout):
    scale = XA_HD ** -0.5
    for h in range(XA_HEADS):
        sl = slice(h * XA_HD, (h + 1) * XA_HD)
        q = q_ref[0, :, sl].astype(BF16)
        if cache_layout:
            k = k_ref[0, 0, :, h, :].astype(BF16)
            v = v_ref[0, 0, :, h, :].astype(BF16)
        else:
            k = k_ref[0, :, sl].astype(BF16)
            v = v_ref[0, :, sl].astype(BF16)
        s = lax.dot_general(q, k, (((1,), (1,)), ((), ())), preferred_element_type=F32) * scale
        m = jnp.max(s, axis=-1, keepdims=True)
        p = jnp.exp(s - m)
        p = p / jnp.sum(p, axis=-1, keepdims=True)
        o_ref[0, :, sl] = jnp.dot(p.astype(BF16), v, preferred_element_type=F32)


def xattn(q, k, v, *, layer=None, tq=512):
    b, t, w = q.shape
    tq = _row_tile(t, tq)
    if layer is None:
        m = k.shape[1]
        kv_spec = pl.BlockSpec((1, m, w), lambda i, j: (i, 0, 0))
    else:
        m = k.shape[2]
        kv_spec = pl.BlockSpec((1, 1, m, XA_HEADS, XA_HD), lambda i, j: (layer, i, 0, 0, 0))
    return pl.pallas_call(
        functools.partial(_xattn_kernel, cache_layout=layer is not None),
        grid=(b, t // tq),
        in_specs=[pl.BlockSpec((1, tq, w), lambda i, j: (i, j, 0)), kv_spec, kv_spec],
        out_specs=pl.BlockSpec((1, tq, w), lambda i, j: (i, j, 0)),
        out_shape=jax.ShapeDtypeStruct((b, t, w), F32),
        compiler_params=_params("parallel", "parallel"),
        name="xattn",
    )(q, k, v)


MOE_BLK = 256
_ROUTER_W = 128


def _router_kernel(x_ref, g_ref, w_ref, lg_ref, xn_ref):
    x = _rms(x_ref[...], g_ref[...])
    xb = x.astype(BF16)
    lg_ref[...] = jnp.dot(xb, w_ref[...].astype(BF16), preferred_element_type=F32)
    half = x.shape[1] // 2
    bits = lambda v: lax.bitcast_convert_type(v.astype(F32), jnp.uint32)
    words = (bits(xb[:, half:]) & jnp.uint32(0xFFFF0000)) | (bits(xb[:, :half]) >> 16)
    xn_ref[...] = lax.bitcast_convert_type(words, F32)


def moe_router(x, g, w_router, *, tm=512):
    n, k = x.shape
    tm = _row_tile(n, tm)
    return pl.pallas_call(
        _router_kernel,
        grid=(n // tm,),
        in_specs=[pl.BlockSpec((tm, k), lambda i: (i, 0)),
                  pl.BlockSpec((1, k), lambda i: (0, 0)),
                  pl.BlockSpec((k, _ROUTER_W), lambda i: (0, 0))],
        out_specs=[pl.BlockSpec((tm, _ROUTER_W), lambda i: (i, 0)),
                   pl.BlockSpec((tm, k // 2), lambda i: (i, 0))],
        out_shape=[jax.ShapeDtypeStruct((n, _ROUTER_W), F32), jax.ShapeDtypeStruct((n, k // 2), F32)],
        compiler_params=_params("parallel"),
        name="moe_router",
    )(x, g.reshape(1, k), w_router)


def _expert_kernel(be_ref, act_ref, x_ref, gate_ref, w1_ref, w3_ref, w2_ref, o_ref, w1b_ref, w3b_ref, w2b_ref):
    i = pl.program_id(0)
    prev = be_ref[jnp.maximum(i - 1, 0)]

    @pl.when((act_ref[i] > 0) & ((i == 0) | (be_ref[i] != prev)))
    def _():
        w1b_ref[...] = w1_ref[0, 0].astype(BF16)
        w3b_ref[...] = w3_ref[0, 0].astype(BF16)
        w2b_ref[...] = w2_ref[0, 0].astype(BF16)

    @pl.when(act_ref[i] > 0)
    def _():
        words = lax.bitcast_convert_type(x_ref[...], jnp.uint32)
        unpack = lambda v: lax.bitcast_convert_type(v, F32).astype(BF16)
        x = jnp.concatenate([unpack(words << 16), unpack(words & jnp.uint32(0xFFFF0000))], axis=1)
        h1 = jnp.dot(x, w1b_ref[...], preferred_element_type=F32)
        h3 = jnp.dot(x, w3b_ref[...], preferred_element_type=F32)
        hid = (_silu(h1) * h3).astype(BF16)
        out = jnp.dot(hid, w2b_ref[...], preferred_element_type=F32)
        o_ref[...] = out * gate_ref[...]

    @pl.when(act_ref[i] == 0)
    def _():
        o_ref[...] = jnp.zeros_like(o_ref)


def moe_experts(xg, gate, blk_exp, blk_act, w1, w3, w2, layer):
    rows = xg.shape[0]
    d = w1.shape[2]
    nb = rows // MOE_BLK
    ff = w1.shape[3]
    return pl.pallas_call(
        _expert_kernel,
        grid_spec=pltpu.PrefetchScalarGridSpec(
            num_scalar_prefetch=2,
            grid=(nb,),
            in_specs=[pl.BlockSpec((MOE_BLK, d // 2), lambda i, be, act: (i, 0)),
                      pl.BlockSpec((MOE_BLK, 1), lambda i, be, act: (i, 0)),
                      pl.BlockSpec((1, 1, d, ff), lambda i, be, act: (layer, be[i], 0, 0)),
                      pl.BlockSpec((1, 1, d, ff), lambda i, be, act: (layer, be[i], 0, 0)),
                      pl.BlockSpec((1, 1, ff, d), lambda i, be, act: (layer, be[i], 0, 0))],
            out_specs=pl.BlockSpec((MOE_BLK, d), lambda i, be, act: (i, 0)),
            scratch_shapes=[pltpu.VMEM((d, ff), BF16), pltpu.VMEM((d, ff), BF16), pltpu.VMEM((ff, d), BF16)],
        ),
        out_shape=jax.ShapeDtypeStruct((rows, d), F32),
        compiler_params=_params("arbitrary"),
        name="moe_experts",
    )(blk_exp, blk_act, xg, gate, w1, w3, w2)


def _expert_gather_kernel(tok_ref, be_ref, act_ref, xn_hbm, gate_ref, w1_ref, w3_ref, w2_ref, o_ref,
                          xbuf_ref, sem_ref, w1b_ref, w3b_ref, w2b_ref):
    i = pl.program_id(0)
    nb = pl.num_programs(0)

    def row_copy(blk, slot, r):
        tok = tok_ref[blk * MOE_BLK + r]
        return pltpu.make_async_copy(xn_hbm.at[pl.ds(tok, 1)], xbuf_ref.at[slot, pl.ds(r, 1)], sem_ref.at[slot])

    def start_gather(blk, slot):
        def body(r, carry):
            row_copy(blk, slot, r).start()
            return carry
        lax.fori_loop(0, MOE_BLK, body, 0, unroll=8)

    def wait_gather(slot):
        pltpu.make_async_copy(xbuf_ref.at[slot], xbuf_ref.at[slot], sem_ref.at[slot]).wait()

    slot = lax.rem(i, 2)
    nxt = jnp.minimum(i + 1, nb - 1)

    @pl.when((i == 0) & (act_ref[0] > 0))
    def _():
        start_gather(0, 0)

    @pl.when((i + 1 < nb) & (act_ref[nxt] > 0))
    def _():
        start_gather(nxt, 1 - slot)

    prev = be_ref[jnp.maximum(i - 1, 0)]

    @pl.when((act_ref[i] > 0) & ((i == 0) | (be_ref[i] != prev)))
    def _():
        w1b_ref[...] = w1_ref[0, 0].astype(BF16)
        w3b_ref[...] = w3_ref[0, 0].astype(BF16)
        w2b_ref[...] = w2_ref[0, 0].astype(BF16)

    @pl.when(act_ref[i] > 0)
    def _():
        wait_gather(slot)
        words = lax.bitcast_convert_type(xbuf_ref[slot], jnp.uint32)
        unpack = lambda v: lax.bitcast_convert_type(v, F32).astype(BF16)
        x = jnp.concatenate([unpack(words << 16), unpack(words & jnp.uint32(0xFFFF0000))], axis=1)
        h1 = jnp.dot(x, w1b_ref[...], preferred_element_type=F32)
        h3 = jnp.dot(x, w3b_ref[...], preferred_element_type=F32)
        hid = (_silu(h1) * h3).astype(BF16)
        out = jnp.dot(hid, w2b_ref[...], preferred_element_type=F32)
        o_ref[...] = out * gate_ref[...]

    @pl.when(act_ref[i] == 0)
    def _():
        o_ref[...] = jnp.zeros_like(o_ref)


def moe_experts_gather(xn, buf_tok, gate, blk_exp, blk_act, w1, w3, w2, layer):
    rows = buf_tok.shape[0]
    d = w1.shape[2]
    nb = rows // MOE_BLK
    ff = w1.shape[3]
    return pl.pallas_call(
        _expert_gather_kernel,
        grid_spec=pltpu.PrefetchScalarGridSpec(
            num_scalar_prefetch=3,
            grid=(nb,),
            in_specs=[pl.BlockSpec(memory_space=pl.ANY),
                      pl.BlockSpec((MOE_BLK, 1), lambda i, tok, be, act: (i, 0)),
                      pl.BlockSpec((1, 1, d, ff), lambda i, tok, be, act: (layer, be[i], 0, 0)),
                      pl.BlockSpec((1, 1, d, ff), lambda i, tok, be, act: (layer, be[i], 0, 0)),
                      pl.BlockSpec((1, 1, ff, d), lambda i, tok, be, act: (layer, be[i], 0, 0))],
            out_specs=pl.BlockSpec((MOE_BLK, d), lambda i, tok, be, act: (i, 0)),
            scratch_shapes=[pltpu.VMEM((2, MOE_BLK, d // 2), F32), pltpu.SemaphoreType.DMA((2,)),
                            pltpu.VMEM((d, ff), BF16), pltpu.VMEM((d, ff), BF16), pltpu.VMEM((ff, d), BF16)],
        ),
        out_shape=jax.ShapeDtypeStruct((rows, d), F32),
        compiler_params=_params("arbitrary"),
        name="moe_experts",
    )(buf_tok, blk_exp, blk_act, xn, gate, w1, w3, w2)


def moe_layer(hs, g, wg, bg, we, be, w1, w3, w2, layer):
    d = hs[0].shape[1]
    n = sum(h.shape[0] for h in hs)
    w_router = jnp.concatenate([wg, we, jnp.zeros((d, _ROUTER_W - MOE_GROUPS - MOE_E), F32)], axis=1)
    logits, xn = g_moe_router(hs, g, w_router)
    lg = logits[:, :MOE_GROUPS] + bg
    grp = jnp.argmax(lg, axis=-1)
    gw = jnp.take_along_axis(jax.nn.softmax(lg, axis=-1), grp[:, None], axis=1)
    le = (logits[:, MOE_GROUPS:MOE_GROUPS + MOE_E] + be).reshape(n, MOE_GROUPS, MOE_EPG)
    le = jnp.take_along_axis(le, grp[:, None, None], axis=1)[:, 0]
    tv, ti = lax.top_k(jax.nn.softmax(le, axis=-1), MOE_TOPK)
    wts = gw * tv / jnp.sum(tv, axis=-1, keepdims=True)
    eid = (grp[:, None] * MOE_EPG + ti).reshape(-1).astype(jnp.int32)
    npair = n * MOE_TOPK
    experts = jnp.arange(MOE_E, dtype=jnp.int32)
    order = jnp.argsort(eid).astype(jnp.int32)
    rank = jnp.argsort(order).astype(jnp.int32)
    counts = jnp.sum(eid[:, None] == experts[None, :], axis=0).astype(jnp.int32)
    start = jnp.cumsum(counts) - counts
    padded = (counts + MOE_BLK - 1) // MOE_BLK * MOE_BLK
    pend = jnp.cumsum(padded)
    shift = pend - padded - start
    nb = -(-npair // MOE_BLK) + MOE_E
    blk_lo = jnp.arange(nb, dtype=jnp.int32) * MOE_BLK
    blk_exp = jnp.minimum(jnp.sum(pend[None, :] <= blk_lo[:, None], axis=1), MOE_E - 1).astype(jnp.int32)
    blk_act = (blk_lo < pend[-1]).astype(jnp.int32)
    src = (blk_lo - shift[blk_exp])[:, None] + jnp.arange(MOE_BLK, dtype=jnp.int32)[None, :]
    live = src < (start + counts)[blk_exp][:, None]
    pair = order[jnp.where(live, src, 0).reshape(-1)]
    buf_tok = pair // MOE_TOPK
    buf_gate = jnp.where(live.reshape(-1), wts.reshape(-1)[pair], 0.0)
    dest = (rank + shift[eid]).reshape(n, MOE_TOPK)
    out = moe_experts_gather(xn, buf_tok.astype(jnp.int32), buf_gate[:, None], blk_exp, blk_act, w1, w3, w2, layer)
    y = out[dest[:, 0]] + out[dest[:, 1]]
    starts = [sum(h.shape[0] for h in hs[:k]) for k in range(len(hs))]
    return [h + y[s:s + h.shape[0]] for h, s in zip(hs, starts)]


_KT = 128
_NT = (((1,), (1,)), ((), ()))
_BIG = 3e38
_M0 = -1e29


def _compress_kernel(x_ref, pe_ref, w_ref, o_ref):
    acc = jnp.zeros(o_ref.shape, F32)
    for l in range(L_CMP):
        y = x_ref[:, l, :] + pe_ref[l:l + 1, :]
        acc = acc + jnp.dot(y.astype(BF16), w_ref[l], preferred_element_type=F32)
    o_ref[...] = acc


def nsa_compress(x, pe, w, *, tb=256):
    nb = x.shape[0]
    tb = _row_tile(nb, tb)
    pe2 = jnp.concatenate([pe] * NSA_KV, axis=1)
    z = jnp.zeros_like(w)
    w2 = jnp.concatenate([jnp.concatenate([w, z], axis=2), jnp.concatenate([z, w], axis=2)], axis=1).astype(BF16)
    return pl.pallas_call(
        _compress_kernel,
        grid=(nb // tb,),
        in_specs=[pl.BlockSpec((tb, L_CMP, NSA_KW), lambda i: (i, 0, 0)),
                  pl.BlockSpec((L_CMP, NSA_KW), lambda i: (0, 0)),
                  pl.BlockSpec((L_CMP, NSA_KW, NSA_KW), lambda i: (0, 0, 0))],
        out_specs=pl.BlockSpec((tb, NSA_KW), lambda i: (i, 0)),
        out_shape=jax.ShapeDtypeStruct((nb, NSA_KW), F32),
        compiler_params=_params("parallel"),
        name="nsa_compress",
    )(x, pe2, w2)


def _compress_cache_kernel(x_ref, pe_ref, w_ref, o_ref):
    for g in range(NSA_KV):
        acc = jnp.zeros((o_ref.shape[0], NSA_HD), F32)
        for l in range(L_CMP):
            y = x_ref[0, :, l, g, :] + pe_ref[l:l + 1, :]
            acc = acc + jnp.dot(y.astype(BF16), w_ref[l], preferred_element_type=F32)
        o_ref[:, g * NSA_HD:(g + 1) * NSA_HD] = acc


def nsa_compress_cache(cache, layer, pe, w, *, tb=256):
    nl, n_pool, page = cache.shape[:3]
    nb = n_pool * page // L_CMP
    tb = _row_tile(nb, tb)
    x = cache.reshape(nl, nb, L_CMP, NSA_KV, NSA_HD)
    return pl.pallas_call(
        _compress_cache_kernel,
        grid=(nb // tb,),
        in_specs=[pl.BlockSpec((1, tb, L_CMP, NSA_KV, NSA_HD), lambda i: (layer, i, 0, 0, 0)),
                  pl.BlockSpec((L_CMP, NSA_HD), lambda i: (0, 0)),
                  pl.BlockSpec((L_CMP, NSA_HD, NSA_HD), lambda i: (0, 0, 0))],
        out_specs=pl.BlockSpec((tb, NSA_KW), lambda i: (i, 0)),
        out_shape=jax.ShapeDtypeStruct((nb, NSA_KW), F32),
        compiler_params=_params("parallel"),
        name="nsa_compress_cache",
    )(x, pe, w.astype(BF16))


def _qz(q, g, nq):
    lane = lax.broadcasted_iota(jnp.int32, (nq, NSA_KW), 1)
    keep = (lane >> 6) == g
    parts = []
    for r in range(NSA_REP):
        h = g * NSA_REP + r
        slab = q[:, (h // 2) * NSA_KW:(h // 2 + 1) * NSA_KW]
        if h % 2 != g:
            slab = pltpu.roll(slab, NSA_HD, axis=1)
        parts.append(jnp.where(keep, slab, 0.0))
    return jnp.concatenate(parts, axis=0).astype(BF16)


def _softmax_init(m_ref, l_ref, acc_ref):
    m_ref[...] = jnp.full(m_ref.shape, NEG, F32)
    l_ref[...] = jnp.zeros(l_ref.shape, F32)
    acc_ref[...] = jnp.zeros(acc_ref.shape, F32)


def _softmax_tile(st, dpos, maskf, slope, vt, m_ref, l_ref, acc_ref, g):
    s = st - slope * dpos
    sm = jnp.where(maskf > 0.0, s, NEG)
    m_old = m_ref[g]
    m_new = jnp.maximum(m_old, jnp.max(sm, axis=0, keepdims=True))
    alpha = jnp.exp(m_old - m_new)
    p = jnp.exp(sm - m_new) * maskf
    l_ref[g] = alpha * l_ref[g] + jnp.sum(p, axis=0, keepdims=True)
    acc_ref[g] = alpha * acc_ref[g] + jnp.dot(vt, p.astype(BF16), preferred_element_type=F32)
    m_ref[g] = m_new


def _softmax_done(l_ref, acc_ref, g):
    l = l_ref[g]
    return acc_ref[g] * jnp.where(l > 0.0, 1.0 / l, 0.0)


def _compressed_branch(qz, kcb, vcbt_g, qpos, slope, ncb):
    st = lax.dot_general(kcb, qz, _NT, preferred_element_type=F32)
    row = lax.broadcasted_iota(jnp.int32, (ncb, 1), 0)
    half = ncb // 2
    blk = jnp.where(row < half, 2 * row, 2 * (row - half) + 1)
    c_pos = blk * L_CMP + (L_CMP - 1)
    d_c = qpos - c_pos
    maskf = jnp.where(d_c >= 0, 1.0, 0.0)
    s = st - slope * d_c.astype(F32)
    sm = jnp.where(d_c >= 0, s, NEG)
    m = jnp.max(sm, axis=0, keepdims=True)
    p = jnp.exp(sm - m) * maskf
    l = jnp.sum(p, axis=0, keepdims=True)
    p = p * jnp.where(l > 0.0, 1.0 / l, 0.0)
    o = jnp.dot(vcbt_g, p.astype(BF16), preferred_element_type=F32)
    return o, p


def _select_blocks(imp_sel, qpos, nsp):
    cols = imp_sel.shape[1]
    blk = lax.broadcasted_iota(jnp.int32, (nsp, cols), 0)
    cur = qpos >> 6
    valid = blk <= cur
    forced = jnp.where(valid, jnp.where(blk == 0, 1.0, jnp.where(blk >= cur - 1, 1.0, 0.0)), 0.0)
    score = jnp.where(forced > 0.0, _BIG, jnp.where(valid, imp_sel, -1.0))
    sel = jnp.zeros((nsp, cols), F32)
    for _ in range(N_SEL):
        m = jnp.max(score, axis=0, keepdims=True)
        idx = jnp.min(jnp.where(score == m, blk, nsp + 1), axis=0, keepdims=True)
        pick = blk == idx
        sel = jnp.where(pick, 1.0, sel)
        score = jnp.where(pick, -2.0, score)
    return jnp.where(valid, sel, 0.0)


def _nsa_prompt_kernel(q_ref, glog_ref, kcb_ref, vcbt_ref, ks_ref, vst_ref, kw_ref, vwt_ref,
                       qoff_ref, slope_ref, qaux_ref, o_ref, sel_ref, m_ref, l_ref, acc_ref, *, nq, ncb):
    i = pl.program_id(1)
    st0 = i * nq
    qoff = qoff_ref[...]
    qpos = st0 + qoff
    q = q_ref[0] * (NSA_HD ** -0.5)
    key_io = lax.broadcasted_iota(jnp.int32, (_KT, 1), 0)
    lane = lax.broadcasted_iota(jnp.int32, (_KT, NSA_KW), 1)
    key_lane = jnp.where(lane == 1, lax.broadcasted_iota(jnp.int32, (_KT, NSA_KW), 0), 0).astype(F32)
    causal = key_io <= qoff
    anti = key_io >= qoff
    nsp = ncb // 2
    wt = WINDOW // _KT

    def m_init():
        m_ref[...] = jnp.full(m_ref.shape, _M0, F32)
        l_ref[...] = jnp.zeros(l_ref.shape, F32)
        acc_ref[...] = jnp.zeros(acc_ref.shape, F32)

    for g in range(NSA_KV):
        qz = _qz(q, g, nq)
        qx = jnp.concatenate([qz, qaux_ref[g]], axis=1)
        slope = slope_ref[g]
        rows = slice(g * NSA_HD, (g + 1) * NSA_HD)
        o_c, p_c = _compressed_branch(qz, kcb_ref[0], vcbt_ref[0, rows, :], qpos, slope, ncb)
        imp = p_c[:, 0:nq]
        for r in range(1, NSA_REP):
            imp = imp + p_c[:, r * nq:(r + 1) * nq]
        imp_sel = imp[0:nsp] + imp[nsp:ncb]
        sel = _select_blocks(imp_sel, qpos[:, 0:nq], nsp)
        sel_ref[g] = jnp.concatenate([jnp.where(sel > 0.0, 0.0, NEG)] * NSA_REP, axis=1)
        blk = lax.broadcasted_iota(jnp.int32, (nsp, 1), 0)
        row_any = jnp.max(sel, axis=1, keepdims=True)
        first = jnp.min(jnp.where(row_any > 0.0, jnp.where(blk >= 2, blk, 2 * nsp), 2 * nsp), axis=0, keepdims=True)
        kt_lo = jnp.minimum(first[0, 0] >> 1, i)

        def tile(k_ref, vt_ref, kt, mode, with_sel):
            off = kt * _KT if isinstance(kt, int) else pl.multiple_of(kt * _KT, _KT)
            kaux =jnp.where(lane == 0, (kt - i).astype(F32), key_lane).astype(BF16)
            kx = jnp.concatenate([k_ref[0, pl.ds(off, _KT), :], kaux], axis=1)
            s = lax.dot_general(kx, qx, _NT, preferred_element_type=F32)
            if with_sel:
                half = _KT // 2
                s = jnp.concatenate([s[0:half] + sel_ref[g, pl.ds(2 * kt, 1), :],
                                     s[half:_KT] + sel_ref[g, pl.ds(2 * kt + 1, 1), :]], axis=0)
            if mode == "causal":
                s = jnp.where(causal, s, NEG)
            elif mode == "anti":
                s = jnp.where(anti, s, NEG)
            m_old = m_ref[g]
            m_new = jnp.maximum(m_old, jnp.max(s, axis=0, keepdims=True))
            alpha = jnp.exp(m_old - m_new)
            p = jnp.exp(s - m_new)
            l_ref[g] = alpha * l_ref[g] + jnp.sum(p, axis=0, keepdims=True)
            vt = vt_ref[0, rows, pl.ds(off, _KT)]
            acc_ref[g] = alpha * acc_ref[g] + jnp.dot(vt, p.astype(BF16), preferred_element_type=F32)
            m_ref[g] = m_new

        m_init()

        @pl.when(i > 0)
        def _():
            tile(ks_ref, vst_ref, 0, "none", True)

        def sel_body(kt, carry):
            tile(ks_ref, vst_ref, kt, "none", True)
            return carry

        lax.fori_loop(jnp.maximum(kt_lo, 1), i, sel_body, 0)
        tile(ks_ref, vst_ref, i, "causal", True)
        o_s = _softmax_done(l_ref, acc_ref, g)

        m_init()

        @pl.when(i >= wt)
        def _():
            tile(kw_ref, vwt_ref, i - wt, "anti", False)

        def win_body(kt, carry):
            tile(kw_ref, vwt_ref, kt, "none", False)
            return carry

        lax.fori_loop(jnp.maximum(i - wt + 1, 0), i, win_body, 0)
        tile(kw_ref, vwt_ref, i, "causal", False)
        o_w = _softmax_done(l_ref, acc_ref, g)

        gc = _sigmoid(glog_ref[0, 0, g:g + 1, :])
        gs = _sigmoid(glog_ref[0, 0, 2 + g:3 + g, :])
        gw = _sigmoid(glog_ref[0, 0, 4 + g:5 + g, :])
        o_ref[0, 0, g] = gc * o_c + gs * o_s + gw * o_w


_TS = 256


def _pos_lanes(nkeys, tile_off):
    lane = lax.broadcasted_iota(jnp.int32, (nkeys, NSA_KW), 1)
    key = lax.broadcasted_iota(jnp.int32, (nkeys, NSA_KW), 0)
    hi = (tile_off + (key >> 7)).astype(F32)
    lo = (key & (_KT - 1)).astype(F32)
    return jnp.where(lane == 0, hi, jnp.where(lane == 1, lo, 0.0)).astype(BF16)


def _nsa_prompt_kernel2(q_ref, glog_ref, kcb_ref, vcbt_ref, ks_ref, vst_ref, kw_ref, vwt_ref,
                        qoff_ref, slope_ref, qaux_ref, o_ref, sel_ref, m_ref, l_ref, acc_ref, *, nq, ncb):
    i = pl.program_id(1)
    st0 = i * nq
    qoff = qoff_ref[...]
    qpos = st0 + qoff
    q = q_ref[0] * (NSA_HD ** -0.5)
    nsp = ncb // 2
    bpt = _TS // L_SEL
    idiag = st0 // _TS
    groups = range(NSA_KV)
    rows = [slice(g * NSA_HD, (g + 1) * NSA_HD) for g in groups]

    qxs, o_cs, firsts = [], [], []
    for g in groups:
        qz = _qz(q, g, nq)
        qxs.append(jnp.concatenate([qz, qaux_ref[g]], axis=1))
        o_c, p_c = _compressed_branch(qz, kcb_ref[0], vcbt_ref[0, rows[g], :], qpos, slope_ref[g], ncb)
        o_cs.append(o_c)
        imp = p_c[:, 0:nq]
        for r in range(1, NSA_REP):
            imp = imp + p_c[:, r * nq:(r + 1) * nq]
        sel = _select_blocks(imp[0:nsp] + imp[nsp:ncb], qpos[:, 0:nq], nsp)
        sel_ref[g] = jnp.concatenate([jnp.where(sel > 0.0, 0.0, NEG)] * NSA_REP, axis=1)
        blk = lax.broadcasted_iota(jnp.int32, (nsp, 1), 0)
        row_any = jnp.max(sel, axis=1, keepdims=True)
        first = jnp.min(jnp.where(row_any > 0.0, jnp.where(blk >= bpt, blk, nsp * bpt), nsp * bpt),
                        axis=0, keepdims=True)
        firsts.append(first[0, 0])
    lo = jnp.minimum(jnp.minimum(firsts[0], firsts[1]) // bpt, idiag)

    m_ref[...] = jnp.full(m_ref.shape, _M0, F32)
    l_ref[...] = jnp.zeros(l_ref.shape, F32)
    acc_ref[...] = jnp.zeros(acc_ref.shape, F32)

    def sel_tile(kt, diag):
        off = kt * _TS if isinstance(kt, int) else pl.multiple_of(kt * _TS, _TS)
        kx = jnp.concatenate([ks_ref[0, pl.ds(off, _TS), :], _pos_lanes(_TS, kt * (_TS // _KT) - i)], axis=1)
        if diag:
            visible = (off + lax.broadcasted_iota(jnp.int32, (_TS, 1), 0)) <= qpos
        for g in groups:
            s = lax.dot_general(kx, qxs[g], _NT, preferred_element_type=F32)
            s = jnp.concatenate([s[b * L_SEL:(b + 1) * L_SEL] + sel_ref[g, pl.ds(bpt * kt + b, 1), :]
                                 for b in range(bpt)], axis=0)
            if diag:
                s = jnp.where(visible, s, NEG)
            m_old = m_ref[g]
            m_new = jnp.maximum(m_old, jnp.max(s, axis=0, keepdims=True))
            alpha = jnp.exp(m_old - m_new)
            p = jnp.exp(s - m_new)
            l_ref[g] = alpha * l_ref[g] + jnp.sum(p, axis=0, keepdims=True)
            vt = vst_ref[rows[g], pl.ds(off, _TS)]
            acc_ref[g] = alpha * acc_ref[g] + jnp.dot(vt, p.astype(BF16), preferred_element_type=F32)
            m_ref[g] = m_new

    @pl.when(idiag > 0)
    def _():
        sel_tile(0, False)

    def sel_body(kt, carry):
        sel_tile(kt, False)
        return carry

    lax.fori_loop(lo, idiag, sel_body, 0)
    sel_tile(idiag, True)

    wk = WINDOW + nq
    wstart = pl.multiple_of(jnp.maximum(i - WINDOW // nq, 0) * nq, nq)
    kxw = jnp.concatenate([kw_ref[0, pl.ds(wstart, wk), :], _pos_lanes(wk, wstart // _KT - i)], axis=1)
    d_w = qpos - (wstart + lax.broadcasted_iota(jnp.int32, (wk, 1), 0))
    in_window = jnp.where(d_w >= 0, d_w, WINDOW + 1) <= WINDOW
    for g in groups:
        s = jnp.where(in_window, lax.dot_general(kxw, qxs[g], _NT, preferred_element_type=F32), NEG)
        m = jnp.max(s, axis=0, keepdims=True)
        p = jnp.exp(s - m)
        l = jnp.sum(p, axis=0, keepdims=True)
        o_w = jnp.dot(vwt_ref[rows[g], pl.ds(wstart, wk)], p.astype(BF16), preferred_element_type=F32) / l
        o_s = _softmax_done(l_ref, acc_ref, g)
        gc = _sigmoid(glog_ref[0, 0, g:g + 1, :])
        gs = _sigmoid(glog_ref[0, 0, 2 + g:3 + g, :])
        gw = _sigmoid(glog_ref[0, 0, 4 + g:5 + g, :])
        o_t = gc * o_cs[g] + gs * o_s + gw * o_w
        for pp in range(NSA_REP // 2):
            pair = jnp.concatenate([o_t[:, (2 * pp) * nq:(2 * pp + 1) * nq],
                                    o_t[:, (2 * pp + 1) * nq:(2 * pp + 2) * nq]], axis=0)
            slab = g * (NSA_REP // 2) + pp
            o_ref[0, :, slab * NSA_KW:(slab + 1) * NSA_KW] = pair.T


def _nsa_cols(nq):
    c = NSA_REP * nq
    qoff = (jnp.arange(c, dtype=jnp.int32) % nq).reshape(1, c)
    slopes = 2.0 ** (-8.0 * jnp.arange(1, NSA_HEADS + 1, dtype=F32) / NSA_HEADS)
    slope = jnp.repeat(slopes.reshape(NSA_KV, NSA_REP), nq, axis=1).reshape(NSA_KV, 1, c)
    return qoff, slope


def _gate_cols(glog, nq):
    b, t, _ = glog.shape
    x = glog.reshape(b, t // nq, nq, NSA_KV, NSA_REP, 3)
    return x.transpose(0, 1, 5, 3, 4, 2).reshape(b, t // nq, 3 * NSA_KV, NSA_REP * nq)


def _even_odd(x):
    return jnp.concatenate([x[:, 0::2], x[:, 1::2]], axis=1)


def _uncols(o, nq):
    b, nb = o.shape[:2]
    x = o.reshape(b, nb, NSA_KV, NSA_HD, NSA_REP, nq)
    return x.transpose(0, 1, 5, 2, 4, 3).reshape(b, nb * nq, NSA_W)


def nsa_prompt(q, glog, kcb, vcb, ks, vs_t, kw, vw_t, *, nq=128):
    b, t, _ = q.shape
    ncb = kcb.shape[1]
    c = NSA_REP * nq
    nblk = t // nq
    assert nq == _KT and t % _TS == 0 and t >= WINDOW + nq and ncb == t // L_CMP
    qoff, slope = _nsa_cols(nq)
    lane = jnp.arange(NSA_KW)[None, None, :]
    slope_col = slope.reshape(NSA_KV, c, 1)
    qaux = jnp.where(lane == 0, slope_col * _KT, jnp.where(lane == 1, slope_col, 0.0)).astype(BF16)
    kcb_p = _even_odd(kcb).astype(BF16)
    vcbt = _even_odd(vcb).transpose(0, 2, 1).astype(BF16)
    per_b = lambda shape: pl.BlockSpec((1,) + shape, lambda i, j: (i, 0, 0))
    out = pl.pallas_call(
        functools.partial(_nsa_prompt_kernel2, nq=nq, ncb=ncb),
        grid=(b, nblk),
        in_specs=[pl.BlockSpec((1, nq, NSA_W), lambda i, j: (i, j, 0)),
                  pl.BlockSpec((1, 1, 3 * NSA_KV, c), lambda i, j: (i, j, 0, 0)),
                  per_b((ncb, NSA_KW)), per_b((NSA_KW, ncb)),
                  per_b((t, NSA_KW)), pl.BlockSpec((NSA_KW, t), lambda i, j: (0, i)),
                  per_b((t, NSA_KW)), pl.BlockSpec((NSA_KW, t), lambda i, j: (0, i)),
                  pl.BlockSpec((1, c), lambda i, j: (0, 0)),
                  pl.BlockSpec((NSA_KV, 1, c), lambda i, j: (0, 0, 0)),
                  pl.BlockSpec((NSA_KV, c, NSA_KW), lambda i, j: (0, 0, 0))],
        out_specs=pl.BlockSpec((1, nq, NSA_W), lambda i, j: (i, j, 0)),
        out_shape=jax.ShapeDtypeStruct((b, t, NSA_W), F32),
        scratch_shapes=[pltpu.VMEM((NSA_KV, ncb // 2, c), F32),
                        pltpu.VMEM((NSA_KV, 1, c), F32), pltpu.VMEM((NSA_KV, 1, c), F32),
                        pltpu.VMEM((NSA_KV, NSA_HD, c), F32)],
        compiler_params=_params("parallel", "arbitrary"),
        name="nsa_prompt",
    )(q, _gate_cols(glog, nq), kcb_p, vcbt, ks, vs_t, kw, vw_t, qoff, slope, qaux)
    return out


def _nsa_sample_kernel(pt_ref, q_ref, glog_ref, kcb_ref, vcbt_ref, *refs, nq, ncb, npages, page, past, wb):
    ks_pages = refs[0:npages]
    vs_pages = refs[npages:2 * npages]
    (nks_ref, nvs_ref, nkw_ref, nvw_ref, wk_ref, wv_ref, qoff_ref, slope_ref, rmat_ref,
     o_ref, wko_ref, wvo_ref, m_ref, l_ref, acc_ref) = refs[2 * npages:]
    del pt_ref
    c = NSA_REP * nq
    qpos = past + qoff_ref[...]
    q = q_ref[0] * (NSA_HD ** -0.5)
    key_io = lax.broadcasted_iota(jnp.int32, (_KT, 1), 0)
    nsp = sel_rows = -(-(past + nq) // L_SEL)
    nsp = -(-nsp // 8) * 8
    pad_rows = lambda x: jnp.concatenate([x, jnp.zeros((_KT - nq, x.shape[1]), F32)], axis=0)

    qzs, slopes, sels, o_cs = [], [], [], []
    for g in range(NSA_KV):
        qz = _qz(q, g, nq)
        slope = slope_ref[g]
        o_c, p_c = _compressed_branch(qz, kcb_ref[0], vcbt_ref[0, g * NSA_HD:(g + 1) * NSA_HD, :], qpos, slope, ncb)
        imp = _dot_exact_rhs(p_c, rmat_ref[...])
        imp_sel = imp[0:ncb // 2] + imp[ncb // 2:ncb]
        imp_sel = jnp.concatenate([imp_sel, jnp.zeros((nsp - ncb // 2, c), F32)], axis=0)
        qzs.append(qz)
        slopes.append(slope)
        sels.append(_select_blocks(imp_sel, qpos, nsp))
        o_cs.append(o_c)
    del sel_rows

    qgs = [qzs[g][:, g * NSA_HD:(g + 1) * NSA_HD] for g in range(NSA_KV)]

    def run_tile(kv_fn, tok0, mask_fn):
        dpos = qpos - (tok0 + key_io)
        for g in range(NSA_KV):
            k, v = kv_fn(g)
            st = lax.dot_general(k.astype(BF16), qgs[g], _NT, preferred_element_type=F32)
            maskf = mask_fn(g, dpos)
            s = st - slopes[g] * dpos.astype(F32)
            sm = jnp.where(maskf > 0.0, s, NEG)
            m_old = m_ref[g]
            m_new = jnp.maximum(m_old, jnp.max(sm, axis=0, keepdims=True))
            alpha = jnp.exp(m_old - m_new)
            p = jnp.exp(sm - m_new) * maskf
            l_ref[g] = alpha * l_ref[g] + jnp.sum(p, axis=0, keepdims=True)
            pv = lax.dot_general(v.astype(BF16), p.astype(BF16), _TN, preferred_element_type=F32)
            acc_ref[g] = alpha * acc_ref[g] + pv
            m_ref[g] = m_new

    cache_tile = lambda k_ref, v_ref, lo: (lambda g: (k_ref[0, 0, lo:lo + _KT, g, :], v_ref[0, 0, lo:lo + _KT, g, :]))
    new_tile = lambda k_ref, v_ref: (lambda g: (pad_rows(k_ref[0, :, g * NSA_HD:(g + 1) * NSA_HD]),
                                                pad_rows(v_ref[0, :, g * NSA_HD:(g + 1) * NSA_HD])))

    _softmax_init(m_ref, l_ref, acc_ref)
    for j in range(npages + 1):
        blk0 = j * (page // L_SEL)

        def sel_mask(g, dpos, blk0=blk0):
            selrow = jnp.where(key_io < L_SEL, sels[g][blk0:blk0 + 1, :], sels[g][blk0 + 1:blk0 + 2, :])
            return jnp.where(dpos >= 0, selrow, 0.0)

        if j < npages:
            run_tile(cache_tile(ks_pages[j], vs_pages[j], 0), j * page, sel_mask)
        else:
            run_tile(new_tile(nks_ref, nvs_ref), past, sel_mask)
    o_ss = [_softmax_done(l_ref, acc_ref, g) for g in range(NSA_KV)]

    _softmax_init(m_ref, l_ref, acc_ref)
    win_mask = lambda g, dpos: jnp.where(dpos >= 0, jnp.where(dpos <= WINDOW, 1.0, 0.0), 0.0)
    for j in range(wb // _KT):
        run_tile(cache_tile(wk_ref, wv_ref, j * _KT), past - wb + j * _KT, win_mask)
    run_tile(new_tile(nkw_ref, nvw_ref), past, win_mask)
    for g in range(NSA_KV):
        o_w = _softmax_done(l_ref, acc_ref, g)
        gc = _sigmoid(glog_ref[0, 0, g:g + 1, :])
        gs = _sigmoid(glog_ref[0, 0, 2 + g:3 + g, :])
        gw = _sigmoid(glog_ref[0, 0, 4 + g:5 + g, :])
        o_ref[0, 0, g] = gc * o_cs[g] + gs * o_ss[g] + gw * o_w

    wko_ref[0, 0:wb - nq] = wk_ref[0, 0, nq:wb]
    wvo_ref[0, 0:wb - nq] = wv_ref[0, 0, nq:wb]
    for g in range(NSA_KV):
        wko_ref[0, wb - nq:wb, g, :] = nkw_ref[0, :, g * NSA_HD:(g + 1) * NSA_HD]
        wvo_ref[0, wb - nq:wb, g, :] = nvw_ref[0, :, g * NSA_HD:(g + 1) * NSA_HD]


def nsa_sample(q, glog, kcb, vcb, pool_k, pool_v, page_table, nks, nvs, nkw, nvw, win_k, win_v, layer):
    b, nq, _ = q.shape
    ncb = kcb.shape[1]
    npages = page_table.shape[1]
    page = pool_k.shape[2]
    past = npages * page
    wb = win_k.shape[2]
    assert page == _KT and wb % _KT == 0 and nq % 8 == 0 and nq <= L_SEL and ncb % 2 == 0
    c = NSA_REP * nq
    qoff, slope = _nsa_cols(nq)
    col = jnp.arange(c, dtype=jnp.int32)
    rmat = (col[:, None] % nq == col[None, :] % nq).astype(BF16)
    kcb_p = _even_odd(kcb).astype(BF16)
    vcbt = _even_odd(vcb).transpose(0, 2, 1).astype(BF16)
    per_b = lambda shape: pl.BlockSpec((1,) + shape, lambda i, pt: (i,) + (0,) * len(shape))
    const = lambda shape: pl.BlockSpec(shape, lambda i, pt: (0,) * len(shape))
    page_spec = lambda j: pl.BlockSpec((1, 1, page, NSA_KV, NSA_HD), lambda i, pt: (layer, pt[i, j], 0, 0, 0))
    win_spec = pl.BlockSpec((1, 1, wb, NSA_KV, NSA_HD), lambda i, pt: (layer, i, 0, 0, 0))
    in_specs = ([per_b((nq, NSA_W)), per_b((1, 3 * NSA_KV, c)), per_b((ncb, NSA_KW)), per_b((NSA_KW, ncb))]
                + [page_spec(j) for j in range(npages)] * 2
                + [per_b((nq, NSA_KW))] * 4 + [win_spec] * 2
                + [const((1, c)), const((NSA_KV, 1, c)), const((c, c))])
    out, wko, wvo = pl.pallas_call(
        functools.partial(_nsa_sample_kernel, nq=nq, ncb=ncb, npages=npages, page=page, past=past, wb=wb),
        grid_spec=pltpu.PrefetchScalarGridSpec(
            num_scalar_prefetch=1,
            grid=(b,),
            in_specs=in_specs,
            out_specs=[per_b((1, NSA_KV, NSA_HD, c)), per_b((wb, NSA_KV, NSA_HD)), per_b((wb, NSA_KV, NSA_HD))],
            scratch_shapes=[pltpu.VMEM((NSA_KV, 1, c), F32), pltpu.VMEM((NSA_KV, 1, c), F32),
                            pltpu.VMEM((NSA_KV, NSA_HD, c), F32)],
        ),
        out_shape=[jax.ShapeDtypeStruct((b, 1, NSA_KV, NSA_HD, c), F32),
                   jax.ShapeDtypeStruct((b, wb, NSA_KV, NSA_HD), F32),
                   jax.ShapeDtypeStruct((b, wb, NSA_KV, NSA_HD), F32)],
        compiler_params=_params("arbitrary"),
        name="nsa_sample",
    )(page_table, q, _gate_cols(glog, nq), kcb_p, vcbt, *([pool_k] * npages), *([pool_v] * npages),
      nks, nvs, nkw, nvw, win_k, win_v, qoff, slope, rmat)
    return _uncols(out, nq), wko, wvo


def _compress_pages_kernel(x_ref, pe_ref, w_ref, o_ref):
    for g in range(NSA_KV):
        acc = jnp.zeros(o_ref.shape[1:], F32)
        for d in range(0, NSA_HD, 2):
            y = jnp.concatenate([x_ref[0, :, g, d, :] + pe_ref[d:d + 1, :],
                                 x_ref[0, :, g, d + 1, :] + pe_ref[d + 1:d + 2, :]], axis=1)
            acc = acc + jnp.dot(y.astype(BF16), w_ref[d // 2], preferred_element_type=F32)
        o_ref[g] = acc


def nsa_compress_pages(cache_t, layer, pe, w, *, tp=256):
    n_pool, page = cache_t.shape[1], cache_t.shape[4]
    nblk = page // L_CMP
    tp = _row_tile(n_pool, tp)
    pe_t = jnp.tile(pe.T, (1, nblk))
    eye = jnp.eye(nblk, dtype=F32)
    wd = jnp.einsum('nm,lde->dnlme', eye, w).reshape(NSA_HD // 2, 2 * page, nblk * NSA_HD).astype(BF16)
    out = pl.pallas_call(
        _compress_pages_kernel,
        grid=(n_pool // tp,),
        in_specs=[pl.BlockSpec((1, tp, NSA_KV, NSA_HD, page), lambda i: (layer, i, 0, 0, 0)),
                  pl.BlockSpec((NSA_HD, page), lambda i: (0, 0)),
                  pl.BlockSpec((NSA_HD // 2, 2 * page, nblk * NSA_HD), lambda i: (0, 0, 0))],
        out_specs=pl.BlockSpec((NSA_KV, tp, nblk * NSA_HD), lambda i: (0, i, 0)),
        out_shape=jax.ShapeDtypeStruct((NSA_KV, n_pool, nblk * NSA_HD), F32),
        compiler_params=_params("parallel"),
        name="nsa_compress_pages",
    )(cache_t, pe_t, wd)
    return out.reshape(NSA_KV, n_pool, nblk, NSA_HD).transpose(1, 2, 0, 3).reshape(n_pool, nblk, NSA_KW)


def _row_softmax(s, mask):
    sm = jnp.where(mask, s, NEG)
    m = jnp.max(sm, axis=1, keepdims=True)
    p = jnp.where(mask, jnp.exp(sm - m), 0.0)
    l = jnp.sum(p, axis=1, keepdims=True)
    return p * jnp.where(l > 0.0, 1.0 / l, 0.0)


def _nsa_decode_kernel(pt_ref, q_ref, gl_ref, kcb_ref, vcb_ref, *refs, nq, ncb, npages, page, past, wb):
    ks_pages = refs[0:npages]
    vs_pages = refs[npages:2 * npages]
    (nks_ref, nvs_ref, nkw_ref, nvw_ref, wk_ref, wv_ref, slope_ref, expand_ref,
     o_ref, wko_ref, wvo_ref) = refs[2 * npages:]
    del pt_ref
    c = NSA_REP * nq
    nblk_lanes = _KT
    q = q_ref[0] * (NSA_HD ** -0.5)
    row = lax.broadcasted_iota(jnp.int32, (c, 1), 0)
    qpos = past + (row & (nq - 1))
    qposf = qpos.astype(F32)
    pad_rows = lambda x: jnp.concatenate([x, jnp.zeros((_KT - nq, x.shape[1]), F32)], axis=0)
    lane = lax.broadcasted_iota(jnp.int32, (1, _KT), 1)
    n_sel_keys = (npages + 1) * page
    key_all = lax.broadcasted_iota(jnp.int32, (1, n_sel_keys), 1)
    expand = expand_ref[...]
    pr = lax.broadcasted_iota(jnp.int32, (ncb, nblk_lanes), 0)
    pc = lax.broadcasted_iota(jnp.int32, (ncb, nblk_lanes), 1)
    pair = jnp.where(jnp.where(pr < ncb // 2, pr, pr - ncb // 2) == pc, 1.0, 0.0).astype(BF16)
    new_kt = pad_rows(nkw_ref[0]).T
    new_vt = pad_rows(nvw_ref[0]).T

    for g in range(NSA_KV):
        gs = slice(g * NSA_HD, (g + 1) * NSA_HD)
        qg = jnp.concatenate([q[:, (g * NSA_REP + r) * NSA_HD:(g * NSA_REP + r + 1) * NSA_HD]
                              for r in range(NSA_REP)], axis=0).astype(BF16)
        slope = slope_ref[g]
        s_c = lax.dot_general(qg, kcb_ref[0, :, gs], _NT, preferred_element_type=F32)
        cl = lax.broadcasted_iota(jnp.int32, (1, ncb), 1)
        half = ncb // 2
        c_pos = jnp.where(cl < half, 2 * cl, 2 * (cl - half) + 1) * L_CMP + (L_CMP - 1)
        d_c = qpos - c_pos
        p_c = _row_softmax(s_c - slope * d_c.astype(F32), d_c >= 0)
        o_c = jnp.dot(p_c.astype(BF16), vcb_ref[0, :, gs], preferred_element_type=F32)
        imp = p_c[0:nq]
        for r in range(1, NSA_REP):
            imp = imp + p_c[r * nq:(r + 1) * nq]
        imp_sel = _dot_exact_rhs(imp, pair)
        cur = (past + lax.broadcasted_iota(jnp.int32, (nq, 1), 0)) >> 6
        valid = lane <= cur
        forced = jnp.where(valid, jnp.where(lane == 0, 1.0, jnp.where(lane >= cur - 1, 1.0, 0.0)), 0.0)
        score = jnp.where(forced > 0.0, _BIG, jnp.where(valid, imp_sel, -1.0))
        sel = jnp.zeros((nq, nblk_lanes), F32)
        for _ in range(N_SEL):
            m = jnp.max(score, axis=1, keepdims=True)
            idx = jnp.min(jnp.where(score == m, lane, nblk_lanes + 1), axis=1, keepdims=True)
            pick = lane == idx
            sel = jnp.where(pick, 1.0, sel)
            score = jnp.where(pick, -2.0, score)
        sel = jnp.where(valid, sel, 0.0)
        sel_keys = jnp.dot(sel.astype(BF16), expand, preferred_element_type=F32)
        sel_keys = jnp.concatenate([sel_keys] * NSA_REP, axis=0)

        nk = pad_rows(nks_ref[0, :, gs]).astype(BF16)
        nv = pad_rows(nvs_ref[0, :, gs]).astype(BF16)
        s_parts = [jnp.dot(qg, ks_pages[j][0, 0, g].astype(BF16), preferred_element_type=F32) for j in range(npages)]
        s_parts.append(lax.dot_general(qg, nk, _NT, preferred_element_type=F32))
        s_s = jnp.concatenate(s_parts, axis=1)
        d_s = qpos - key_all
        p_s = _row_softmax(s_s - slope * d_s.astype(F32), jnp.where(d_s >= 0, sel_keys, 0.0) > 0.0).astype(BF16)
        o_s = jnp.dot(p_s[:, npages * page:], nv, preferred_element_type=F32)
        for j in range(npages):
            o_s = o_s + lax.dot_general(p_s[:, j * page:(j + 1) * page], vs_pages[j][0, 0, g].astype(BF16), _NT,
                                        preferred_element_type=F32)

        nkw = pad_rows(nkw_ref[0, :, gs]).astype(BF16)
        nvw = pad_rows(nvw_ref[0, :, gs]).astype(BF16)
        s_w = jnp.concatenate([jnp.dot(qg, wk_ref[0, 0, g].astype(BF16), preferred_element_type=F32),
                               lax.dot_general(qg, nkw, _NT, preferred_element_type=F32)], axis=1)
        w_pos = past - wb + lax.broadcasted_iota(jnp.int32, (1, wb + _KT), 1)
        d_w = qpos - w_pos
        p_w = _row_softmax(s_w - slope * d_w.astype(F32), jnp.where(d_w >= 0, d_w, WINDOW + 1) <= WINDOW).astype(BF16)
        o_w = (lax.dot_general(p_w[:, 0:wb], wv_ref[0, 0, g].astype(BF16), _NT, preferred_element_type=F32)
               + jnp.dot(p_w[:, wb:], nvw, preferred_element_type=F32))

        gate = _sigmoid(gl_ref[0, g])
        o_ref[0, g] = gate[:, 0:1] * o_c + gate[:, 1:2] * o_s + gate[:, 2:3] * o_w

        for src_ref, new_full, dst_ref in ((wk_ref, new_kt, wko_ref), (wv_ref, new_vt, wvo_ref)):
            new_t = pltpu.roll(new_full[gs, :], _KT - nq, axis=1)
            shifted = pltpu.roll(src_ref[0, 0, g], wb - nq, axis=1)
            dst_ref[0, g, :, 0:wb - _KT] = shifted[:, 0:wb - _KT]
            dst_ref[0, g, :, wb - _KT:wb] = jnp.where(lane >= _KT - nq, new_t, shifted[:, wb - _KT:wb])


_DECODE_BB = 4


def _nsa_decode_kernel2(pt_ref, q_ref, gl_ref, kcb_ref, vcb_ref, *refs, nq, ncb, npages, page, past, wb, nbb):
    ks_pages = refs[0:nbb * npages]
    vs_pages = refs[nbb * npages:2 * nbb * npages]
    (nks_ref, nvs_ref, nkw_ref, nvw_ref, wk_ref, wv_ref, slope_ref, expand_ref,
     o_ref, wko_ref, wvo_ref) = refs[2 * nbb * npages:]
    del pt_ref
    c = NSA_REP * nq
    nblk_lanes = _KT
    n_sel = -(-(past + nq) // L_SEL)
    row = lax.broadcasted_iota(jnp.int32, (c, 1), 0)
    qpos = past + (row & (nq - 1))
    pad_rows = lambda x: jnp.concatenate([x, jnp.zeros((_KT - nq, x.shape[1]), F32)], axis=0)
    lane = lax.broadcasted_iota(jnp.int32, (1, _KT), 1)
    n_sel_keys = (npages + 1) * page
    key_all = lax.broadcasted_iota(jnp.int32, (1, n_sel_keys), 1)
    expand = expand_ref[...]
    pr = lax.broadcasted_iota(jnp.int32, (ncb, nblk_lanes), 0)
    pc = lax.broadcasted_iota(jnp.int32, (ncb, nblk_lanes), 1)
    half = ncb // 2
    pair = jnp.where(jnp.where(pr < half, pr, pr - half) == pc, 1.0, 0.0).astype(BF16)
    cl = lax.broadcasted_iota(jnp.int32, (1, ncb), 1)
    c_pos = jnp.where(cl < half, 2 * cl, 2 * (cl - half) + 1) * L_CMP + (L_CMP - 1)
    d_c = qpos - c_pos
    d_cf = d_c.astype(F32)
    d_s = qpos - key_all
    d_sf = d_s.astype(F32)
    d_w = qpos - (past - wb + lax.broadcasted_iota(jnp.int32, (1, wb + _KT), 1))
    d_wf = d_w.astype(F32)
    in_window = jnp.where(d_w >= 0, d_w, WINDOW + 1) <= WINDOW
    cur = (past + lax.broadcasted_iota(jnp.int32, (nq, 1), 0)) >> 6
    valid = lane <= cur
    forced = jnp.where(valid, jnp.where(lane == 0, 1.0, jnp.where(lane >= cur - 1, 1.0, 0.0)), 0.0)

    for bb in range(nbb):
        q = q_ref[bb] * (NSA_HD ** -0.5)
        new_kt = pad_rows(nkw_ref[bb]).T
        new_vt = pad_rows(nvw_ref[bb]).T
        for g in range(NSA_KV):
            gs = slice(g * NSA_HD, (g + 1) * NSA_HD)
            qg = jnp.concatenate([q[:, (g * NSA_REP + r) * NSA_HD:(g * NSA_REP + r + 1) * NSA_HD]
                                  for r in range(NSA_REP)], axis=0).astype(BF16)
            slope = slope_ref[g]
            s_c = lax.dot_general(qg, kcb_ref[bb, :, gs], _NT, preferred_element_type=F32)
            p_c = _row_softmax(s_c - slope * d_cf, d_c >= 0)
            o_c = jnp.dot(p_c.astype(BF16), vcb_ref[bb, :, gs], preferred_element_type=F32)
            imp = p_c[0:nq]
            for r in range(1, NSA_REP):
                imp = imp + p_c[r * nq:(r + 1) * nq]
            imp_sel = _dot_exact_rhs(imp, pair)
            score = jnp.where(forced > 0.0, _BIG, jnp.where(valid, imp_sel, -1.0))
            before = jnp.zeros((nq, nblk_lanes), F32)
            for bi in range(n_sel):
                sb = score[:, bi:bi + 1]
                before = before + jnp.where(sb > score, 1.0, jnp.where(sb == score, jnp.where(lane > bi, 1.0, 0.0), 0.0))
            sel = jnp.where(valid, jnp.where(before < N_SEL, 1.0, 0.0), 0.0)
            sel_keys = jnp.dot(sel.astype(BF16), expand, preferred_element_type=F32)
            sel_keys = jnp.concatenate([sel_keys] * NSA_REP, axis=0)

            nk = pad_rows(nks_ref[bb, :, gs]).astype(BF16)
            nv = pad_rows(nvs_ref[bb, :, gs]).astype(BF16)
            s_parts = [jnp.dot(qg, ks_pages[bb * npages + j][0, 0, g].astype(BF16), preferred_element_type=F32)
                       for j in range(npages)]
            s_parts.append(lax.dot_general(qg, nk, _NT, preferred_element_type=F32))
            s_s = jnp.concatenate(s_parts, axis=1)
            p_s = _row_softmax(s_s - slope * d_sf, jnp.where(d_s >= 0, sel_keys, 0.0) > 0.0).astype(BF16)
            o_s = jnp.dot(p_s[:, npages * page:], nv, preferred_element_type=F32)
            for j in range(npages):
                o_s = o_s + lax.dot_general(p_s[:, j * page:(j + 1) * page],
                                            vs_pages[bb * npages + j][0, 0, g].astype(BF16), _NT,
                                            preferred_element_type=F32)

            nkw = pad_rows(nkw_ref[bb, :, gs]).astype(BF16)
            nvw = pad_rows(nvw_ref[bb, :, gs]).astype(BF16)
            s_w = jnp.concatenate([jnp.dot(qg, wk_ref[0, bb, g].astype(BF16), preferred_element_type=F32),
                                   lax.dot_general(qg, nkw, _NT, preferred_element_type=F32)], axis=1)
            p_w = _row_softmax(s_w - slope * d_wf, in_window).astype(BF16)
            o_w = (lax.dot_general(p_w[:, 0:wb], wv_ref[0, bb, g].astype(BF16), _NT, preferred_element_type=F32)
                   + jnp.dot(p_w[:, wb:], nvw, preferred_element_type=F32))

            gate = _sigmoid(gl_ref[bb, g])
            o_ref[bb, g] = gate[:, 0:1] * o_c + gate[:, 1:2] * o_s + gate[:, 2:3] * o_w

            for src_ref, new_full, dst_ref in ((wk_ref, new_kt, wko_ref), (wv_ref, new_vt, wvo_ref)):
                new_t = pltpu.roll(new_full[gs, :], _KT - nq, axis=1)
                shifted = pltpu.roll(src_ref[0, bb, g], wb - nq, axis=1)
                dst_ref[bb, g, :, 0:wb - _KT] = shifted[:, 0:wb - _KT]
                dst_ref[bb, g, :, wb - _KT:wb] = jnp.where(lane >= _KT - nq, new_t, shifted[:, wb - _KT:wb])


def nsa_decode(q, glog, kcb, vcb, pool_k, pool_v, page_table, nks, nvs, nkw, nvw, win_k, win_v, layer):
    b, nq, _ = q.shape
    ncb = kcb.shape[1]
    npages = page_table.shape[1]
    page = pool_k.shape[4]
    past = npages * page
    wb = win_k.shape[4]
    assert page == _KT and wb % _KT == 0 and nq & (nq - 1) == 0 and nq % 8 == 0 and nq <= L_SEL
    assert ncb % 2 == 0 and ncb // 2 <= _KT and -(-(past + nq) // L_SEL) <= _KT
    c = NSA_REP * nq
    _, slope = _nsa_cols(nq)
    gl = glog.reshape(b, nq, NSA_KV, NSA_REP, 3).transpose(0, 2, 3, 1, 4).reshape(b, NSA_KV, c, 3)
    nbb = _DECODE_BB if b % _DECODE_BB == 0 else 1
    per_b = lambda shape: pl.BlockSpec((nbb,) + shape, lambda i, pt: (i,) + (0,) * len(shape))
    const = lambda shape: pl.BlockSpec(shape, lambda i, pt: (0,) * len(shape))
    page_spec = lambda bb, j: pl.BlockSpec((1, 1, NSA_KV, NSA_HD, page),
                                           lambda i, pt: (layer, pt[i * nbb + bb, j], 0, 0, 0))
    page_specs = [page_spec(bb, j) for bb in range(nbb) for j in range(npages)]
    win_spec = pl.BlockSpec((1, nbb, NSA_KV, NSA_HD, wb), lambda i, pt: (layer, i, 0, 0, 0))
    n_keys = (npages + 1) * page
    expand = (jnp.arange(n_keys)[None, :] // L_SEL == jnp.arange(_KT)[:, None]).astype(BF16)
    in_specs = ([per_b((nq, NSA_W)), per_b((NSA_KV, c, 3)), per_b((ncb, NSA_KW)), per_b((ncb, NSA_KW))]
                + page_specs * 2
                + [per_b((nq, NSA_KW))] * 4 + [win_spec] * 2 + [const((NSA_KV, c, 1)), const((_KT, n_keys))])
    out, wko, wvo = pl.pallas_call(
        functools.partial(_nsa_decode_kernel2, nq=nq, ncb=ncb, npages=npages, page=page, past=past, wb=wb, nbb=nbb),
        grid_spec=pltpu.PrefetchScalarGridSpec(
            num_scalar_prefetch=1,
            grid=(b // nbb,),
            in_specs=in_specs,
            out_specs=[per_b((NSA_KV, c, NSA_HD)), per_b((NSA_KV, NSA_HD, wb)), per_b((NSA_KV, NSA_HD, wb))],
        ),
        out_shape=[jax.ShapeDtypeStruct((b, NSA_KV, c, NSA_HD), F32),
                   jax.ShapeDtypeStruct((b, NSA_KV, NSA_HD, wb), F32),
                   jax.ShapeDtypeStruct((b, NSA_KV, NSA_HD, wb), F32)],
        compiler_params=_params("arbitrary"),
        name="nsa_decode",
    )(page_table, q, gl, _even_odd(kcb).astype(BF16), _even_odd(vcb).astype(BF16),
      *([pool_k] * (nbb * npages)), *([pool_v] * (nbb * npages)), nks, nvs, nkw, nvw, win_k, win_v,
      slope.reshape(NSA_KV, c, 1), expand)
    o = out.reshape(b, NSA_KV, NSA_REP, nq, NSA_HD).transpose(0, 3, 1, 2, 4).reshape(b, nq, NSA_W)
    return o, wko, wvo


def _xattn_cache_kernel(q_ref, k_ref, v_ref, o_ref, *, nq):
    scale = XA_HD ** -0.5
    q = jnp.concatenate([q_ref[0, :, h * XA_HD:(h + 1) * XA_HD] for h in range(XA_HEADS)], axis=0)
    k = k_ref[0, 0].astype(BF16)
    v = v_ref[0, 0].astype(BF16)
    s = lax.dot_general(q.astype(BF16), k, _NT, preferred_element_type=F32) * scale
    col_h = lax.broadcasted_iota(jnp.int32, s.shape, 1) & (XA_HEADS - 1)
    row_h = lax.broadcasted_iota(jnp.int32, s.shape, 0) >> (nq.bit_length() - 1)
    mine = col_h == row_h
    m = jnp.max(jnp.where(mine, s, NEG), axis=1, keepdims=True)
    p = jnp.where(mine, jnp.exp(s - m), 0.0)
    p = p / jnp.sum(p, axis=1, keepdims=True)
    o = jnp.dot(p.astype(BF16), v, preferred_element_type=F32)
    for h in range(XA_HEADS):
        o_ref[0, :, h * XA_HD:(h + 1) * XA_HD] = o[h * nq:(h + 1) * nq]


def xattn_cache(q, cache_k, cache_v, layer):
    b, nq, w = q.shape
    m = cache_k.shape[2]
    assert XA_HEADS & (XA_HEADS - 1) == 0 and nq % 8 == 0
    kv = lambda a: a.reshape(a.shape[0], b, m * XA_HEADS, XA_HD)
    kv_spec = pl.BlockSpec((1, 1, m * XA_HEADS, XA_HD), lambda i: (layer, i, 0, 0))
    return pl.pallas_call(
        functools.partial(_xattn_cache_kernel, nq=nq),
        grid=(b,),
        in_specs=[pl.BlockSpec((1, nq, w), lambda i: (i, 0, 0)), kv_spec, kv_spec],
        out_specs=pl.BlockSpec((1, nq, w), lambda i: (i, 0, 0)),
        out_shape=jax.ShapeDtypeStruct((b, nq, w), F32),
        compiler_params=_params("parallel"),
        name="xattn_cache",
    )(q, kv(cache_k), kv(cache_v))


_HALO_M = 8
_TN = (((0,), (0,)), ((), ()))


def _softplus(x):
    return jnp.maximum(x, 0.0) + jnp.log1p(jnp.exp(-jnp.abs(x)))


def _ssd_kernel(xbc_ref, z_ref, sm_ref, dtt_ref, cs_ref, h0_ref, cw_ref, cb_ref, dtb_ref, dtbt_ref,
                al_ref, alt_ref, dsk_ref, ng_ref, y_ref, ncs_ref, hf_ref, ext_ref, h_ref, yh_ref, *, ql, dt_col):
    c = pl.program_id(1)
    nc = pl.num_programs(1)

    @pl.when(c == 0)
    def _():
        ext_ref[...] = jnp.zeros_like(ext_ref)
        ext_ref[_HALO_M - (M_CONV_W - 1):_HALO_M, :] = cs_ref[0]
        h_ref[...] = h0_ref[0]

    @pl.when(c > 0)
    def _():
        ext_ref[0:_HALO_M, :] = ext_ref[ql:ql + _HALO_M, :]

    ext_ref[_HALO_M:_HALO_M + ql, :] = xbc_ref[0]
    acc = jnp.zeros((ql, M_CONV_DIM), F32)
    for k in range(M_CONV_W):
        off = _HALO_M - (M_CONV_W - 1) + k
        acc = acc + ext_ref[off:off + ql, :].astype(BF16).astype(F32) * cw_ref[k:k + 1, :]
    xbc = _silu(acc + cb_ref[...])
    xs = xbc[:, 0:M_DIN]
    bm = xbc[:, M_DIN:M_DIN + M_GROUPS * M_DSTATE]
    cm = xbc[:, M_DIN + M_GROUPS * M_DSTATE:M_CONV_DIM]

    dt = _softplus(sm_ref[0, :, dt_col:dt_col + M_HEADS] + dtb_ref[...])
    dtt = _softplus(dtt_ref[0] + dtbt_ref[...])
    dta = dt * (-jnp.exp(al_ref[...]))
    dtat = dtt * (-jnp.exp(alt_ref[...]))
    ti = lax.broadcasted_iota(jnp.int32, (ql, ql), 0)
    si = lax.broadcasted_iota(jnp.int32, (ql, ql), 1)
    causal = si <= ti
    cum = _dot_exact_lhs(jnp.where(causal, 1.0, 0.0).astype(BF16), dta)
    cumt = _dot_exact_rhs(dtat, jnp.where(ti <= si, 1.0, 0.0).astype(BF16))
    cum_last = cum[ql - 1:ql, :]
    edec = jnp.exp(cum)
    eend = jnp.exp(cum_last - cum)
    elast = jnp.exp(cum_last)

    rep = M_HEADS // M_GROUPS
    for gi in range(M_GROUPS):
        b_g = bm[:, gi * M_DSTATE:(gi + 1) * M_DSTATE]
        c_g = cm[:, gi * M_DSTATE:(gi + 1) * M_DSTATE].astype(BF16)
        cb = lax.dot_general(c_g, b_g.astype(BF16), _NT, preferred_element_type=F32)
        for hh in range(rep):
            h = gi * rep + hh
            hs = slice(h * M_HDIM, (h + 1) * M_HDIM)
            lmat = jnp.where(causal, jnp.exp(cum[:, h:h + 1] - cumt[h:h + 1, :]), 0.0)
            x_h = xs[:, hs]
            xdt = (x_h * dt[:, h:h + 1]).astype(BF16)
            y_diag = jnp.dot((cb * lmat).astype(BF16), xdt, preferred_element_type=F32)
            h_in = h_ref[h]
            y_off = lax.dot_general(c_g, h_in.astype(BF16), _NT, preferred_element_type=F32) * edec[:, h:h + 1]
            bd = (b_g * eend[:, h:h + 1]).astype(BF16)
            s_chunk = lax.dot_general(xdt, bd, _TN, preferred_element_type=F32)
            h_ref[h] = elast[:, h:h + 1] * h_in + s_chunk
            yh_ref[:, hs] = y_diag + y_off + dsk_ref[:, hs] * x_h

    yz = yh_ref[...] * _silu(z_ref[0])
    y_ref[0] = _rms(yz, ng_ref[...])

    @pl.when(c == nc - 1)
    def _():
        ncs_ref[0] = ext_ref[_HALO_M + ql - (M_CONV_W - 1):_HALO_M + ql, :]
        hf_ref[0] = h_ref[...]


def ssd_mixer(xbc, z, small, dt_col, conv_state, h0, conv_w, conv_b, dt_bias, a_log, d_skip, norm_g, *, ql):
    b, t, _ = xbc.shape
    nc = t // ql
    sw = small.shape[2]
    dtt = small[:, :, dt_col:dt_col + M_HEADS].transpose(0, 2, 1)
    const = lambda shape: pl.BlockSpec(shape, lambda i, j: (0,) * len(shape))
    per_b = lambda shape: pl.BlockSpec((1,) + shape, lambda i, j: (i,) + (0,) * len(shape))
    row = lambda x: x.reshape(1, -1)
    colv = lambda x: x.reshape(-1, 1)
    return pl.pallas_call(
        functools.partial(_ssd_kernel, ql=ql, dt_col=dt_col),
        grid=(b, nc),
        in_specs=[pl.BlockSpec((1, ql, M_CONV_DIM), lambda i, j: (i, j, 0)),
                  pl.BlockSpec((1, ql, M_DIN), lambda i, j: (i, j, 0)),
                  pl.BlockSpec((1, ql, sw), lambda i, j: (i, j, 0)),
                  pl.BlockSpec((1, M_HEADS, ql), lambda i, j: (i, 0, j)),
                  per_b((M_CONV_W - 1, M_CONV_DIM)), per_b((M_HEADS, M_HDIM, M_DSTATE)),
                  const((M_CONV_W, M_CONV_DIM)), const((1, M_CONV_DIM)),
                  const((1, M_HEADS)), const((M_HEADS, 1)), const((1, M_HEADS)), const((M_HEADS, 1)),
                  const((1, M_DIN)), const((1, M_DIN))],
        out_specs=[pl.BlockSpec((1, ql, M_DIN), lambda i, j: (i, j, 0)),
                   per_b((M_CONV_W - 1, M_CONV_DIM)), per_b((M_HEADS, M_HDIM, M_DSTATE))],
        out_shape=[jax.ShapeDtypeStruct((b, t, M_DIN), F32),
                   jax.ShapeDtypeStruct((b, M_CONV_W - 1, M_CONV_DIM), F32),
                   jax.ShapeDtypeStruct((b, M_HEADS, M_HDIM, M_DSTATE), F32)],
        scratch_shapes=[pltpu.VMEM((_HALO_M + ql, M_CONV_DIM), F32),
                        pltpu.VMEM((M_HEADS, M_HDIM, M_DSTATE), F32),
                        pltpu.VMEM((ql, M_DIN), F32)],
        compiler_params=_params("parallel", "arbitrary"),
        name="ssd_mixer",
    )(xbc, z, small, dtt, conv_state, h0, conv_w, row(conv_b), row(dt_bias), colv(dt_bias),
      row(a_log), colv(a_log), row(jnp.repeat(d_skip, M_HDIM)), row(norm_g))


_SMALL_W = 128
_OD_SPLITS = (NSA_W,) + (NSA_KW,) * 6 + (M_DIN, M_CONV_DIM, _SMALL_W)


def _odd_w_in(w):
    o_kv = NSA_W
    o_gate = o_kv + 6 * NSA_KW
    o_z = o_gate + 3 * NSA_HEADS
    o_xbc = o_z + M_DIN
    o_dt = o_xbc + M_CONV_DIM
    pad = jnp.zeros((w.shape[0], _SMALL_W - 3 * NSA_HEADS - M_HEADS), F32)
    return jnp.concatenate([w[:, :o_gate], w[:, o_z:o_xbc], w[:, o_xbc:o_dt],
                            w[:, o_gate:o_z], w[:, o_dt:], pad], axis=1)


def kernel(x_prompt, x_sample, state_conv_a, state_conv_b, cache_cmp_k, cache_cmp_v, cache_sel_k, cache_sel_v, cache_win_k, cache_win_v, state_ssm, state_ssm_conv, cache_mem_k, cache_mem_v, page_table, mem_prompt, norm_mix, norm_xattn, norm_ffn, norm_final, ev_w_in, ev_conv_a, ev_conv_b, ev_conv_b_bias, ev_ln_g, ev_ln_b, ev_w_out, od_w_in, od_cmp_pe, od_cmp_wk, od_cmp_wv, od_ssm_conv_w, od_ssm_conv_b, od_dt_bias, od_a_log, od_d_skip, od_ssm_norm, od_w_out, xa_wq, xa_wk, xa_wv, xa_wo, moe_wg, moe_bg, moe_we, moe_be, moe_w1, moe_w3, moe_w2):
    bp, tp, d = x_prompt.shape
    bs, ts, _ = x_sample.shape
    n_p, n_s = bp * tp, bs * ts
    n_mem = mem_prompt.shape[1]
    depth = norm_mix.shape[0]
    n_pool, page = cache_cmp_k.shape[1:3]
    wb = cache_win_k.shape[2]
    dt_col = 3 * NSA_HEADS

    def groups(pair):
        return pair[0].reshape(bp, tp, -1), pair[1].reshape(bs, ts, -1)

    def rows(a_p, a_s):
        return [a_p.reshape(n_p, a_p.shape[-1]), a_s.reshape(n_s, a_s.shape[-1])]

    hs = rows(x_prompt, x_sample)
    out = {k: [] for k in ("ca_p", "ca_s", "cb_p", "cb_s", "wk_p", "wk_s", "wv_p", "wv_s",
                           "sm_p", "sm_s", "sc_p", "sc_s", "mk_p", "mv_p")}
    rows_p = [[], [], [], []]
    rows_s = [[], [], [], []]
    for i in range(depth):
        j = i // 2
        if i % 2 == 0:
            u_p, u_s = groups(g_norm_matmul(hs, norm_mix[i], ev_w_in[j]))
            ev = (ev_conv_a[j], ev_conv_b[j], ev_conv_b_bias[j], ev_ln_g[j], ev_ln_b[j])
            y_p, na_p, nb_p = even_conv(u_p, jnp.zeros((bp, CONV_A_W - 1, D_A), F32),
                                        jnp.zeros((bp, CONV_B_W - 1, D_B), F32), *ev)
            y_s, na_s, nb_s = even_conv(u_s, state_conv_a[j], state_conv_b[j], *ev)
            hs = g_matmul_res([rows(y_p, y_s)], [ev_w_out[j]], hs)
            out["ca_p"].append(na_p)
            out["ca_s"].append(na_s)
            out["cb_p"].append(nb_p)
            out["cb_s"].append(nb_s)
        else:
            w_in = _odd_w_in(od_w_in[j])
            c_ks, c_vs, c_kw, c_vw = (NSA_W + k * NSA_KW for k in (2, 3, 4, 5))
            w_vt = jnp.concatenate([w_in[:, c_vs:c_vs + NSA_KW], w_in[:, c_vw:c_vw + NSA_KW]], axis=1).T
            u = g_odd_in_proj(hs, norm_mix[i], w_in, w_vt, (c_ks, c_kw))
            ks_b, kw_b, vs_t, vw_t = (u[len(_OD_SPLITS) + k][0] for k in range(4))
            q_p, q_s = groups(u[0])
            kv = [groups(u[1 + k]) for k in range(6)]
            kvp = [a for a, _ in kv]
            kvs = [b for _, b in kv]
            z_p, z_s = groups(u[7])
            xbc_p, xbc_s = groups(u[8])
            sm_p, sm_s = groups(u[9])
            pe, wck, wcv = od_cmp_pe[j], od_cmp_wk[j], od_cmp_wv[j]
            mw = (od_ssm_conv_w[j], od_ssm_conv_b[j], od_dt_bias[j], od_a_log[j], od_d_skip[j], od_ssm_norm[j])
            blocks = lambda a: a.reshape(-1, L_CMP, NSA_KW)
            ncb = tp // L_CMP
            kcb_p = nsa_compress(blocks(kvp[0][:, :ncb * L_CMP]), pe, wck).reshape(bp, ncb, NSA_KW)
            vcb_p = nsa_compress(blocks(kvp[1][:, :ncb * L_CMP]), pe, wcv).reshape(bp, ncb, NSA_KW)
            o_p = nsa_prompt(q_p, sm_p[:, :, :dt_col], kcb_p, vcb_p, ks_b.reshape(bp, tp, NSA_KW), vs_t,
                             kw_b.reshape(bp, tp, NSA_KW), vw_t)
            keep = min(WINDOW, tp)
            y_p, nsc_p, nsm_p = ssd_mixer(xbc_p, z_p, sm_p, dt_col, jnp.zeros((bp, M_CONV_W - 1, M_CONV_DIM), F32),
                                          jnp.zeros((bp, M_HEADS, M_HDIM, M_DSTATE), F32), *mw, ql=128)
            tokens_last = lambda a: jnp.transpose(a, (0, 1, 3, 4, 2))
            kcp = nsa_compress_pages(tokens_last(cache_cmp_k), j, pe, wck)
            vcp = nsa_compress_pages(tokens_last(cache_cmp_v), j, pe, wcv)
            kcb_s = kcp[page_table].reshape(bs, -1, NSA_KW)
            vcb_s = vcp[page_table].reshape(bs, -1, NSA_KW)
            o_s, nwk_s, nwv_s = nsa_decode(
                q_s, sm_s[:, :, :dt_col], kcb_s, vcb_s, tokens_last(cache_sel_k), tokens_last(cache_sel_v),
                page_table, kvs[2], kvs[3], kvs[4], kvs[5], tokens_last(cache_win_k), tokens_last(cache_win_v), j)
            nwk_s = jnp.transpose(nwk_s, (0, 3, 1, 2))
            nwv_s = jnp.transpose(nwv_s, (0, 3, 1, 2))
            y_s, nsc_s, nsm_s = ssd_mixer(xbc_s, z_s, sm_s, dt_col, state_ssm_conv[j], state_ssm[j], *mw, ql=ts)
            w_out = od_w_out[j]
            hs = g_matmul_res([rows(o_p, o_s), rows(y_p, y_s)], [w_out[:NSA_W], w_out[NSA_W:]], hs)
            heads = lambda a: a.reshape(a.shape[0], a.shape[1], NSA_KV, NSA_HD)
            for k in range(4):
                rows_p[k].append(heads(kvp[k]))
                rows_s[k].append(heads(kvs[k]))
            out["wk_p"].append(heads(kvp[4][:, tp - keep:]))
            out["wv_p"].append(heads(kvp[5][:, tp - keep:]))
            out["wk_s"].append(nwk_s)
            out["wv_s"].append(nwv_s)
            out["sc_p"].append(nsc_p)
            out["sc_s"].append(nsc_s)
            out["sm_p"].append(nsm_p)
            out["sm_s"].append(nsm_s)
        mk, mv = norm_matmul(mem_prompt.reshape(bp * n_mem, d), None,
                             jnp.concatenate([xa_wk[i], xa_wv[i]], axis=1), norm=False,
                             splits=(XA_HEADS * XA_HD, XA_HEADS * XA_HD))
        mk = mk.reshape(bp, n_mem, XA_HEADS * XA_HD)
        mv = mv.reshape(bp, n_mem, XA_HEADS * XA_HD)
        out["mk_p"].append(mk.reshape(bp, n_mem, XA_HEADS, XA_HD))
        out["mv_p"].append(mv.reshape(bp, n_mem, XA_HEADS, XA_HD))
        qx_p, qx_s = groups(g_norm_matmul(hs, norm_xattn[i], xa_wq[i]))
        ox_p = xattn(qx_p, mk, mv)
        ox_s = xattn_cache(qx_s, cache_mem_k, cache_mem_v, i)
        hs = g_matmul_res([rows(ox_p, ox_s)], [xa_wo[i]], hs)
        hs = moe_layer(hs, norm_ffn[i], moe_wg[i], moe_bg[i], moe_we[i], moe_be[i], moe_w1, moe_w3, moe_w2, i)
    y_prompt, y_sample = groups(g_rmsnorm(hs, norm_final))
    st = lambda k: jnp.stack(out[k])
    return (y_prompt, y_sample, st("ca_p"), st("ca_s"), st("cb_p"), st("cb_s"),
            jnp.stack(rows_p[0]), jnp.stack(rows_s[0]), jnp.stack(rows_p[1]), jnp.stack(rows_s[1]),
            jnp.stack(rows_p[2]), jnp.stack(rows_s[2]), jnp.stack(rows_p[3]), jnp.stack(rows_s[3]),
            st("wk_p"), st("wk_s"), st("wv_p"), st("wv_s"), st("sm_p"), st("sm_s"), st("sc_p"), st("sc_s"),
            st("mk_p"), st("mv_p"))
```

```python
import functools

import jax
import jax.numpy as jnp
from jax import lax
from jax.experimental import pallas as pl
from jax.experimental.pallas import tpu as pltpu

F32 = jnp.float32
BF16 = jnp.bfloat16
EPS = 1e-6
NEG = -1e30
VMEM_LIMIT = 56 * 1024 * 1024

D_A = 512
D_B = 512
CONV_A_W = 3
CONV_B_W = 31
NSA_HEADS = 8
NSA_HD = 64
NSA_KV = 2
NSA_REP = NSA_HEADS // NSA_KV
NSA_W = NSA_HEADS * NSA_HD
NSA_KW = NSA_KV * NSA_HD
L_CMP = 32
L_SEL = 64
N_SEL = 16
WINDOW = 512
M_DIN = 512
M_HDIM = 64
M_HEADS = 8
M_DSTATE = 64
M_GROUPS = 2
M_CONV_W = 4
M_CONV_DIM = M_DIN + 2 * M_GROUPS * M_DSTATE
XA_HEADS = 4
XA_HD = 128
MOE_GROUPS = 4
MOE_EPG = 8
MOE_E = 32
MOE_TOPK = 2


def _params(*sem):
    return pltpu.CompilerParams(dimension_semantics=sem, vmem_limit_bytes=VMEM_LIMIT)


def _row_tile(n, pref):
    t = min(n, pref)
    while n % t or (t % 8 and t != n):
        t -= 1
    return t


def _bdot(a, b):
    return jnp.dot(a.astype(BF16), b.astype(BF16), preferred_element_type=F32)


def _split3(a):
    hi = a.astype(BF16)
    r1 = a - hi.astype(F32)
    mid = r1.astype(BF16)
    lo = (r1 - mid.astype(F32)).astype(BF16)
    return hi, mid, lo


def _dot_exact_rhs(a, b_bf16):
    hi, mid, lo = _split3(a)
    d = lambda x: jnp.dot(x, b_bf16, preferred_element_type=F32)
    return d(hi) + d(mid) + d(lo)


def _dot_exact_lhs(a_bf16, b):
    hi, mid, lo = _split3(b)
    d = lambda x: jnp.dot(a_bf16, x, preferred_element_type=F32)
    return d(hi) + d(mid) + d(lo)


def _rms(x, g):
    ms = jnp.mean(x * x, axis=-1, keepdims=True)
    return x * lax.rsqrt(ms + EPS) * g


def _sigmoid(x):
    return 1.0 / (1.0 + jnp.exp(-x))


def _silu(x):
    return x * _sigmoid(x)


def _norm_matmul_kernel(x_ref, g_ref, w_ref, *o_refs, norm, splits):
    x = x_ref[...]
    if norm:
        x = _rms(x, g_ref[...])
    res = jnp.dot(x.astype(BF16), w_ref[...].astype(BF16), preferred_element_type=F32)
    off = 0
    for o_ref, width in zip(o_refs, splits):
        o_ref[...] = res[:, off:off + width]
        off += width


def norm_matmul(x, g, w, *, norm=True, splits=None, tm=512):
    n, k = x.shape
    m = w.shape[1]
    tm = _row_tile(n, tm)
    if g is None:
        g = jnp.ones((k,), F32)
    widths = (m,) if splits is None else tuple(splits)
    assert sum(widths) == m
    outs = pl.pallas_call(
        functools.partial(_norm_matmul_kernel, norm=norm, splits=widths),
        grid=(n // tm,),
        in_specs=[pl.BlockSpec((tm, k), lambda i: (i, 0)),
                  pl.BlockSpec((1, k), lambda i: (0, 0)),
                  pl.BlockSpec((k, m), lambda i: (0, 0))],
        out_specs=[pl.BlockSpec((tm, wd), lambda i: (i, 0)) for wd in widths],
        out_shape=[jax.ShapeDtypeStruct((n, wd), F32) for wd in widths],
        compiler_params=_params("parallel"),
        name="norm_matmul",
    )(x, g.reshape(1, k), w)
    return outs[0] if splits is None else outs


def _rmsnorm_kernel(x_ref, g_ref, o_ref):
    o_ref[...] = _rms(x_ref[...], g_ref[...])


def rmsnorm_rows(x, g, *, tm=512):
    n, k = x.shape
    tm = _row_tile(n, tm)
    return pl.pallas_call(
        _rmsnorm_kernel,
        grid=(n // tm,),
        in_specs=[pl.BlockSpec((tm, k), lambda i: (i, 0)), pl.BlockSpec((1, k), lambda i: (0, 0))],
        out_specs=pl.BlockSpec((tm, k), lambda i: (i, 0)),
        out_shape=jax.ShapeDtypeStruct((n, k), F32),
        compiler_params=_params("parallel"),
        name="rmsnorm_rows",
    )(x, g.reshape(1, k))


def _matmul_res_kernel(*refs, n_in):
    res_ref = refs[2 * n_in]
    o_ref = refs[2 * n_in + 1]
    acc = res_ref[...]
    for j in range(n_in):
        acc = acc + jnp.dot(refs[2 * j][...].astype(BF16), refs[2 * j + 1][...].astype(BF16),
                            preferred_element_type=F32)
    o_ref[...] = acc


def matmul_res(xs, ws, res, *, tm=512):
    n, m = res.shape
    tm = _row_tile(n, tm)
    in_specs, args = [], []
    for x, w in zip(xs, ws):
        k = x.shape[1]
        in_specs += [pl.BlockSpec((tm, k), lambda i: (i, 0)), pl.BlockSpec((k, m), lambda i: (0, 0))]
        args += [x, w]
    in_specs.append(pl.BlockSpec((tm, m), lambda i: (i, 0)))
    return pl.pallas_call(
        functools.partial(_matmul_res_kernel, n_in=len(xs)),
        grid=(n // tm,),
        in_specs=in_specs,
        out_specs=pl.BlockSpec((tm, m), lambda i: (i, 0)),
        out_shape=jax.ShapeDtypeStruct((n, m), F32),
        compiler_params=_params("parallel"),
        name="matmul_res",
    )(*args, res)


_TM = 512


def _rowwise_call(body, row_inputs, shared, out_widths, out_dtypes, name, joint_outputs=False, transposed=()):
    ns = [a.shape[0] for a in row_inputs[0]]
    assert all(n % _TM == 0 for n in ns)
    nbs = [n // _TM for n in ns]
    starts = [sum(nbs[:g]) for g in range(len(ns))]
    n_groups, n_row, n_out = len(ns), len(row_inputs), len(out_widths)

    def group_map(g):
        return lambda i: (jnp.clip(i - starts[g], 0, nbs[g] - 1), 0)

    in_specs, args = [], []
    for k in range(n_row):
        for g in range(n_groups):
            a = row_inputs[k][g]
            in_specs.append(pl.BlockSpec((_TM, a.shape[1]), group_map(g)))
            args.append(a)
    for a in shared:
        in_specs.append(pl.BlockSpec(a.shape, lambda i, nd=a.ndim: (0,) * nd))
        args.append(a)
    if joint_outputs:
        out_specs = [pl.BlockSpec((_TM, w), lambda i: (i, 0)) for w in out_widths]
        out_shape = [jax.ShapeDtypeStruct((sum(ns), w), dt) for w, dt in zip(out_widths, out_dtypes)]
    else:
        def group_map_t(g):
            return lambda i: (0, jnp.clip(i - starts[g], 0, nbs[g] - 1))

        out_specs, out_shape = [], []
        for j, (w, dt) in enumerate(zip(out_widths, out_dtypes)):
            for g in range(n_groups):
                if j in transposed:
                    out_specs.append(pl.BlockSpec((w, _TM), group_map_t(g)))
                    out_shape.append(jax.ShapeDtypeStruct((w, ns[g]), dt))
                else:
                    out_specs.append(pl.BlockSpec((_TM, w), group_map(g)))
                    out_shape.append(jax.ShapeDtypeStruct((ns[g], w), dt))

    def kernel(*refs):
        x_refs = refs[:n_row * n_groups]
        s_refs = refs[n_row * n_groups:n_row * n_groups + len(shared)]
        o_refs = refs[n_row * n_groups + len(shared):]
        i = pl.program_id(0)
        for g in range(n_groups):
            @pl.when((i >= starts[g]) & (i < starts[g] + nbs[g]))
            def _(g=g):
                vals = body([x_refs[k * n_groups + g][...] for k in range(n_row)], s_refs)
                for j, v in enumerate(vals):
                    o_ref = o_refs[j] if joint_outputs else o_refs[j * n_groups + g]
                    o_ref[...] = v.astype(o_ref.dtype)

    outs = pl.pallas_call(
        kernel, grid=(sum(nbs),), in_specs=in_specs, out_specs=out_specs, out_shape=out_shape,
        compiler_params=_params("arbitrary"), name=name)(*args)
    if joint_outputs:
        return list(outs)
    return [list(outs[j * n_groups:(j + 1) * n_groups]) for j in range(n_out)]


def g_norm_matmul(hs, g, w, *, splits=None, norm=True):
    widths = (w.shape[1],) if splits is None else tuple(splits)
    assert sum(widths) == w.shape[1]

    def body(xs, s_refs):
        x = _rms(xs[0], s_refs[0][...]) if norm else xs[0]
        res = jnp.dot(x.astype(BF16), s_refs[1][...].astype(BF16), preferred_element_type=F32)
        offs = [sum(widths[:j]) for j in range(len(widths))]
        return [res[:, o:o + wd] for o, wd in zip(offs, widths)]

    k = hs[0].shape[1]
    gv = jnp.ones((1, k), F32) if g is None else g.reshape(1, k)
    outs = _rowwise_call(body, [hs], [gv, w], widths, [F32] * len(widths), "norm_matmul")
    return outs[0] if splits is None else outs


def g_odd_in_proj(hs, g, w, w_kvt, k_cols):
    widths = _OD_SPLITS + (NSA_KW,) * 4 + (NSA_KW,) * 6
    offs = [sum(_OD_SPLITS[:j]) for j in range(len(_OD_SPLITS))]

    def body(xs, s_refs):
        x = _rms(xs[0], s_refs[0][...]).astype(BF16)
        res = jnp.dot(x, s_refs[1][...].astype(BF16), preferred_element_type=F32)
        kvt = lax.dot_general(s_refs[2][...].astype(BF16), x, _NT, preferred_element_type=F32)
        part = lambda k: kvt[k * NSA_KW:(k + 1) * NSA_KW]
        outs = [res[:, o:o + wd] for o, wd in zip(offs, _OD_SPLITS)]
        outs += [res[:, c:c + NSA_KW] for c in k_cols]
        outs += [part(3), part(5)]
        outs += [part(k) for k in range(6)]
        return outs

    n_main = len(_OD_SPLITS)
    dts = [F32] * n_main + [BF16] * 4 + [F32] * 6
    return _rowwise_call(body, [hs], [g.reshape(1, -1), w, w_kvt], widths, dts, "odd_in_proj",
                         transposed=(n_main + 2, n_main + 3) + tuple(range(n_main + 4, n_main + 10)))


def g_matmul_res(xs_list, ws, hs):
    def body(xs, s_refs):
        acc = xs[-1]
        for j in range(len(ws)):
            acc = acc + jnp.dot(xs[j].astype(BF16), s_refs[j][...].astype(BF16), preferred_element_type=F32)
        return [acc]

    return _rowwise_call(body, list(xs_list) + [hs], list(ws), (hs[0].shape[1],), [F32], "matmul_res")[0]


def g_rmsnorm(hs, g):
    body = lambda xs, s_refs: [_rms(xs[0], s_refs[0][...])]
    return _rowwise_call(body, [hs], [g.reshape(1, -1)], (hs[0].shape[1],), [F32], "rmsnorm_rows")[0]


def g_moe_router(hs, g, w_router):
    def body(xs, s_refs):
        xb = _rms(xs[0], s_refs[0][...]).astype(BF16)
        logits = jnp.dot(xb, s_refs[1][...].astype(BF16), preferred_element_type=F32)
        half = xb.shape[1] // 2
        bits = lambda v: lax.bitcast_convert_type(v.astype(F32), jnp.uint32)
        words = (bits(xb[:, half:]) & jnp.uint32(0xFFFF0000)) | (bits(xb[:, :half]) >> 16)
        return [logits, lax.bitcast_convert_type(words, F32)]

    k = hs[0].shape[1]
    return _rowwise_call(body, [hs], [g.reshape(1, k), w_router], (_ROUTER_W, k // 2), [F32, F32], "moe_router",
                         joint_outputs=True)


_HALO_A = 8
_HALO_B = 32


def _even_conv_kernel(u_ref, sa_ref, sb_ref, wa_ref, wb_ref, bb_ref, lg_ref, lb_ref,
                      y_ref, na_ref, nb_ref, ea_ref, eb_ref, ear_ref, ebr_ref, *, tt):
    rnd = lambda x: x.astype(BF16).astype(F32)
    t = pl.program_id(1)
    nt = pl.num_programs(1)

    @pl.when(t == 0)
    def _():
        ea_ref[...] = jnp.zeros_like(ea_ref)
        eb_ref[...] = jnp.zeros_like(eb_ref)
        ea_ref[_HALO_A - (CONV_A_W - 1):_HALO_A, :] = sa_ref[0]
        eb_ref[_HALO_B - (CONV_B_W - 1):_HALO_B, :] = sb_ref[0]
        ear_ref[...] = rnd(ea_ref[...])
        ebr_ref[...] = rnd(eb_ref[...])

    @pl.when(t > 0)
    def _():
        ea_ref[0:_HALO_A, :] = ea_ref[tt:tt + _HALO_A, :]
        eb_ref[0:_HALO_B, :] = eb_ref[tt:tt + _HALO_B, :]
        ear_ref[0:_HALO_A, :] = ear_ref[tt:tt + _HALO_A, :]
        ebr_ref[0:_HALO_B, :] = ebr_ref[tt:tt + _HALO_B, :]

    xa = u_ref[0, :, 0:D_A]
    ba = u_ref[0, :, D_A:2 * D_A]
    ca = u_ref[0, :, 2 * D_A:3 * D_A]
    pb = u_ref[0, :, 3 * D_A:3 * D_A + D_B]
    gb = u_ref[0, :, 3 * D_A + D_B:3 * D_A + 2 * D_B]
    va = ca * xa
    vb = pb * _sigmoid(gb)
    ea_ref[_HALO_A:_HALO_A + tt, :] = va
    eb_ref[_HALO_B:_HALO_B + tt, :] = vb
    ear_ref[_HALO_A:_HALO_A + tt, :] = rnd(va)
    ebr_ref[_HALO_B:_HALO_B + tt, :] = rnd(vb)

    acc = jnp.zeros((tt, D_A), F32)
    for k in range(CONV_A_W):
        off = _HALO_A - (CONV_A_W - 1) + k
        acc = acc + ear_ref[off:off + tt, :] * wa_ref[k:k + 1, :]
    y_ref[0, :, 0:D_A] = ba * acc

    acc = jnp.zeros((tt, D_B), F32)
    for k in range(CONV_B_W):
        off = _HALO_B - (CONV_B_W - 1) + k
        acc = acc + ebr_ref[off:off + tt, :] * wb_ref[k:k + 1, :]
    acc = acc + bb_ref[...]
    mu = jnp.mean(acc, axis=-1, keepdims=True)
    xc = acc - mu
    var = jnp.mean(xc * xc, axis=-1, keepdims=True)
    yb = xc * lax.rsqrt(var + EPS) * lg_ref[...] + lb_ref[...]
    y_ref[0, :, D_A:D_A + D_B] = _silu(yb)

    @pl.when(t == nt - 1)
    def _():
        na_ref[0] = ea_ref[_HALO_A + tt - (CONV_A_W - 1):_HALO_A + tt, :]
        nb_ref[0] = eb_ref[_HALO_B + tt - (CONV_B_W - 1):_HALO_B + tt, :]


def even_conv(u, sa, sb, wa, wb, bb, lg, lb, *, tt=256):
    b, t, w = u.shape
    tt = _row_tile(t, tt)
    full = lambda shape: pl.BlockSpec(shape, lambda i, j: (0,) * len(shape))
    return pl.pallas_call(
        functools.partial(_even_conv_kernel, tt=tt),
        grid=(b, t // tt),
        in_specs=[pl.BlockSpec((1, tt, w), lambda i, j: (i, j, 0)),
                  pl.BlockSpec((1, CONV_A_W - 1, D_A), lambda i, j: (i, 0, 0)),
                  pl.BlockSpec((1, CONV_B_W - 1, D_B), lambda i, j: (i, 0, 0)),
                  full((CONV_A_W, D_A)), full((CONV_B_W, D_B)), full((1, D_B)),
                  full((1, D_B)), full((1, D_B))],
        out_specs=[pl.BlockSpec((1, tt, D_A + D_B), lambda i, j: (i, j, 0)),
                   pl.BlockSpec((1, CONV_A_W - 1, D_A), lambda i, j: (i, 0, 0)),
                   pl.BlockSpec((1, CONV_B_W - 1, D_B), lambda i, j: (i, 0, 0))],
        out_shape=[jax.ShapeDtypeStruct((b, t, D_A + D_B), F32),
                   jax.ShapeDtypeStruct((b, CONV_A_W - 1, D_A), F32),
                   jax.ShapeDtypeStruct((b, CONV_B_W - 1, D_B), F32)],
        scratch_shapes=[pltpu.VMEM((_HALO_A + tt, D_A), F32), pltpu.VMEM((_HALO_B + tt, D_B), F32),
                        pltpu.VMEM((_HALO_A + tt, D_A), F32), pltpu.VMEM((_HALO_B + tt, D_B), F32)],
        compiler_params=_params("parallel", "arbitrary"),
        name="even_conv",
    )(u, sa, sb, wa, wb, bb.reshape(1, D_B), lg.reshape(1, D_B), lb.reshape(1, D_B))


def _xattn_kernel(q_ref, k_ref, v_ref, o_ref, *, cache_layout):
    scale = XA_HD ** -0.5
    for h in range(XA_HEADS):
        sl = slice(h * XA_HD, (h + 1) * XA_HD)
        q = q_ref[0, :, sl].astype(BF16)
        if cache_layout:
            k = k_ref[0, 0, :, h, :].astype(BF16)
            v = v_ref[0, 0, :, h, :].astype(BF16)
        else:
            k = k_ref[0, :, sl].astype(BF16)
            v = v_ref[0, :, sl].astype(BF16)
        s = lax.dot_general(q, k, (((1,), (1,)), ((), ())), preferred_element_type=F32) * scale
        m = jnp.max(s, axis=-1, keepdims=True)
        p = jnp.exp(s - m)
        p = p / jnp.sum(p, axis=-1, keepdims=True)
        o_ref[0, :, sl] = jnp.dot(p.astype(BF16), v, preferred_element_type=F32)


def xattn(q, k, v, *, layer=None, tq=512):
    b, t, w = q.shape
    tq = _row_tile(t, tq)
    if layer is None:
        m = k.shape[1]
        kv_spec = pl.BlockSpec((1, m, w), lambda i, j: (i, 0, 0))
    else:
        m = k.shape[2]
        kv_spec = pl.BlockSpec((1, 1, m, XA_HEADS, XA_HD), lambda i, j: (layer, i, 0, 0, 0))
    return pl.pallas_call(
        functools.partial(_xattn_kernel, cache_layout=layer is not None),
        grid=(b, t // tq),
        in_specs=[pl.BlockSpec((1, tq, w), lambda i, j: (i, j, 0)), kv_spec, kv_spec],
        out_specs=pl.BlockSpec((1, tq, w), lambda i, j: (i, j, 0)),
        out_shape=jax.ShapeDtypeStruct((b, t, w), F32),
        compiler_params=_params("parallel", "parallel"),
        name="xattn",
    )(q, k, v)


MOE_BLK = 256
_ROUTER_W = 128


def _router_kernel(x_ref, g_ref, w_ref, lg_ref, xn_ref):
    x = _rms(x_ref[...], g_ref[...])
    xb = x.astype(BF16)
    lg_ref[...] = jnp.dot(xb, w_ref[...].astype(BF16), preferred_element_type=F32)
    half = x.shape[1] // 2
    bits = lambda v: lax.bitcast_convert_type(v.astype(F32), jnp.uint32)
    words = (bits(xb[:, half:]) & jnp.uint32(0xFFFF0000)) | (bits(xb[:, :half]) >> 16)
    xn_ref[...] = lax.bitcast_convert_type(words, F32)


def moe_router(x, g, w_router, *, tm=512):
    n, k = x.shape
    tm = _row_tile(n, tm)
    return pl.pallas_call(
        _router_kernel,
        grid=(n // tm,),
        in_specs=[pl.BlockSpec((tm, k), lambda i: (i, 0)),
                  pl.BlockSpec((1, k), lambda i: (0, 0)),
                  pl.BlockSpec((k, _ROUTER_W), lambda i: (0, 0))],
        out_specs=[pl.BlockSpec((tm, _ROUTER_W), lambda i: (i, 0)),
                   pl.BlockSpec((tm, k // 2), lambda i: (i, 0))],
        out_shape=[jax.ShapeDtypeStruct((n, _ROUTER_W), F32), jax.ShapeDtypeStruct((n, k // 2), F32)],
        compiler_params=_params("parallel"),
        name="moe_router",
    )(x, g.reshape(1, k), w_router)


def _expert_kernel(be_ref, act_ref, x_ref, gate_ref, w1_ref, w3_ref, w2_ref, o_ref, w1b_ref, w3b_ref, w2b_ref):
    i = pl.program_id(0)
    prev = be_ref[jnp.maximum(i - 1, 0)]

    @pl.when((act_ref[i] > 0) & ((i == 0) | (be_ref[i] != prev)))
    def _():
        w1b_ref[...] = w1_ref[0, 0].astype(BF16)
        w3b_ref[...] = w3_ref[0, 0].astype(BF16)
        w2b_ref[...] = w2_ref[0, 0].astype(BF16)

    @pl.when(act_ref[i] > 0)
    def _():
        words = lax.bitcast_convert_type(x_ref[...], jnp.uint32)
        unpack = lambda v: lax.bitcast_convert_type(v, F32).astype(BF16)
        x = jnp.concatenate([unpack(words << 16), unpack(words & jnp.uint32(0xFFFF0000))], axis=1)
        h1 = jnp.dot(x, w1b_ref[...], preferred_element_type=F32)
        h3 = jnp.dot(x, w3b_ref[...], preferred_element_type=F32)
        hid = (_silu(h1) * h3).astype(BF16)
        out = jnp.dot(hid, w2b_ref[...], preferred_element_type=F32)
        o_ref[...] = out * gate_ref[...]

    @pl.when(act_ref[i] == 0)
    def _():
        o_ref[...] = jnp.zeros_like(o_ref)


def moe_experts(xg, gate, blk_exp, blk_act, w1, w3, w2, layer):
    rows = xg.shape[0]
    d = w1.shape[2]
    nb = rows // MOE_BLK
    ff = w1.shape[3]
    return pl.pallas_call(
        _expert_kernel,
        grid_spec=pltpu.PrefetchScalarGridSpec(
            num_scalar_prefetch=2,
            grid=(nb,),
            in_specs=[pl.BlockSpec((MOE_BLK, d // 2), lambda i, be, act: (i, 0)),
                      pl.BlockSpec((MOE_BLK, 1), lambda i, be, act: (i, 0)),
                      pl.BlockSpec((1, 1, d, ff), lambda i, be, act: (layer, be[i], 0, 0)),
                      pl.BlockSpec((1, 1, d, ff), lambda i, be, act: (layer, be[i], 0, 0)),
                      pl.BlockSpec((1, 1, ff, d), lambda i, be, act: (layer, be[i], 0, 0))],
            out_specs=pl.BlockSpec((MOE_BLK, d), lambda i, be, act: (i, 0)),
            scratch_shapes=[pltpu.VMEM((d, ff), BF16), pltpu.VMEM((d, ff), BF16), pltpu.VMEM((ff, d), BF16)],
        ),
        out_shape=jax.ShapeDtypeStruct((rows, d), F32),
        compiler_params=_params("arbitrary"),
        name="moe_experts",
    )(blk_exp, blk_act, xg, gate, w1, w3, w2)


def _expert_gather_kernel(tok_ref, be_ref, act_ref, xn_hbm, gate_ref, w1_ref, w3_ref, w2_ref, o_ref,
                          xbuf_ref, sem_ref, w1b_ref, w3b_ref, w2b_ref):
    i = pl.program_id(0)
    nb = pl.num_programs(0)

    def row_copy(blk, slot, r):
        tok = tok_ref[blk * MOE_BLK + r]
        return pltpu.make_async_copy(xn_hbm.at[pl.ds(tok, 1)], xbuf_ref.at[slot, pl.ds(r, 1)], sem_ref.at[slot])

    def start_gather(blk, slot):
        if isinstance(slot, int):
            for r in range(MOE_BLK):
                row_copy(blk, slot, r).start()
            return
        for s in range(2):
            @pl.when(slot == s)
            def _(s=s):
                for r in range(MOE_BLK):
                    row_copy(blk, s, r).start()

    def wait_gather(slot):
        pltpu.make_async_copy(xbuf_ref.at[slot], xbuf_ref.at[slot], sem_ref.at[slot]).wait()

    slot = lax.rem(i, 2)
    nxt = jnp.minimum(i + 1, nb - 1)

    @pl.when((i == 0) & (act_ref[0] > 0))
    def _():
        start_gather(0, 0)

    @pl.when((i + 1 < nb) & (act_ref[nxt] > 0))
    def _():
        start_gather(nxt, 1 - slot)

    prev = be_ref[jnp.maximum(i - 1, 0)]

    @pl.when((act_ref[i] > 0) & ((i == 0) | (be_ref[i] != prev)))
    def _():
        w1b_ref[...] = w1_ref[0, 0].astype(BF16)
        w3b_ref[...] = w3_ref[0, 0].astype(BF16)
        w2b_ref[...] = w2_ref[0, 0].astype(BF16)

    @pl.when(act_ref[i] > 0)
    def _():
        wait_gather(slot)
        words = lax.bitcast_convert_type(xbuf_ref[slot], jnp.uint32)
        unpack = lambda v: lax.bitcast_convert_type(v, F32).astype(BF16)
        x = jnp.concatenate([unpack(words << 16), unpack(words & jnp.uint32(0xFFFF0000))], axis=1)
        h1 = jnp.dot(x, w1b_ref[...], preferred_element_type=F32)
        h3 = jnp.dot(x, w3b_ref[...], preferred_element_type=F32)
        hid = (_silu(h1) * h3).astype(BF16)
        out = jnp.dot(hid, w2b_ref[...], preferred_element_type=F32)
        o_ref[...] = out * gate_ref[...]

    @pl.when(act_ref[i] == 0)
    def _():
        o_ref[...] = jnp.zeros_like(o_ref)


def moe_experts_gather(xn, buf_tok, gate, blk_exp, blk_act, w1, w3, w2, layer):
    rows = buf_tok.shape[0]
    d = w1.shape[2]
    nb = rows // MOE_BLK
    ff = w1.shape[3]
    return pl.pallas_call(
        _expert_gather_kernel,
        grid_spec=pltpu.PrefetchScalarGridSpec(
            num_scalar_prefetch=3,
            grid=(nb,),
            in_specs=[pl.BlockSpec(memory_space=pl.ANY),
                      pl.BlockSpec((MOE_BLK, 1), lambda i, tok, be, act: (i, 0)),
                      pl.BlockSpec((1, 1, d, ff), lambda i, tok, be, act: (layer, be[i], 0, 0)),
                      pl.BlockSpec((1, 1, d, ff), lambda i, tok, be, act: (layer, be[i], 0, 0)),
                      pl.BlockSpec((1, 1, ff, d), lambda i, tok, be, act: (layer, be[i], 0, 0))],
            out_specs=pl.BlockSpec((MOE_BLK, d), lambda i, tok, be, act: (i, 0)),
            scratch_shapes=[pltpu.VMEM((2, MOE_BLK, d // 2), F32), pltpu.SemaphoreType.DMA((2,)),
                            pltpu.VMEM((d, ff), BF16), pltpu.VMEM((d, ff), BF16), pltpu.VMEM((ff, d), BF16)],
        ),
        out_shape=jax.ShapeDtypeStruct((rows, d), F32),
        compiler_params=_params("arbitrary"),
        name="moe_experts",
    )(buf_tok, blk_exp, blk_act, xn, gate, w1, w3, w2)


def moe_layer(hs, g, wg, bg, we, be, w1, w3, w2, layer):
    d = hs[0].shape[1]
    n = sum(h.shape[0] for h in hs)
    w_router = jnp.concatenate([wg, we, jnp.zeros((d, _ROUTER_W - MOE_GROUPS - MOE_E), F32)], axis=1)
    logits, xn = g_moe_router(hs, g, w_router)
    lg = logits[:, :MOE_GROUPS] + bg
    grp = jnp.argmax(lg, axis=-1)
    gw = jnp.take_along_axis(jax.nn.softmax(lg, axis=-1), grp[:, None], axis=1)
    le = (logits[:, MOE_GROUPS:MOE_GROUPS + MOE_E] + be).reshape(n, MOE_GROUPS, MOE_EPG)
    le = jnp.take_along_axis(le, grp[:, None, None], axis=1)[:, 0]
    tv, ti = lax.top_k(jax.nn.softmax(le, axis=-1), MOE_TOPK)
    wts = gw * tv / jnp.sum(tv, axis=-1, keepdims=True)
    eid = (grp[:, None] * MOE_EPG + ti).reshape(-1).astype(jnp.int32)
    npair = n * MOE_TOPK
    experts = jnp.arange(MOE_E, dtype=jnp.int32)
    order = jnp.argsort(eid).astype(jnp.int32)
    rank = jnp.argsort(order).astype(jnp.int32)
    counts = jnp.sum(eid[:, None] == experts[None, :], axis=0).astype(jnp.int32)
    start = jnp.cumsum(counts) - counts
    padded = (counts + MOE_BLK - 1) // MOE_BLK * MOE_BLK
    pend = jnp.cumsum(padded)
    shift = pend - padded - start
    nb = -(-npair // MOE_BLK) + MOE_E
    blk_lo = jnp.arange(nb, dtype=jnp.int32) * MOE_BLK
    blk_exp = jnp.minimum(jnp.sum(pend[None, :] <= blk_lo[:, None], axis=1), MOE_E - 1).astype(jnp.int32)
    blk_act = (blk_lo < pend[-1]).astype(jnp.int32)
    src = (blk_lo - shift[blk_exp])[:, None] + jnp.arange(MOE_BLK, dtype=jnp.int32)[None, :]
    live = src < (start + counts)[blk_exp][:, None]
    pair = order[jnp.where(live, src, 0).reshape(-1)]
    buf_tok = pair // MOE_TOPK
    buf_gate = jnp.where(live.reshape(-1), wts.reshape(-1)[pair], 0.0)
    dest = (rank + shift[eid]).reshape(n, MOE_TOPK)
    out = moe_experts_gather(xn, buf_tok.astype(jnp.int32), buf_gate[:, None], blk_exp, blk_act, w1, w3, w2, layer)
    y = out[dest[:, 0]] + out[dest[:, 1]]
    starts = [sum(h.shape[0] for h in hs[:k]) for k in range(len(hs))]
    return [h + y[s:s + h.shape[0]] for h, s in zip(hs, starts)]


_KT = 128
_NT = (((1,), (1,)), ((), ()))
_BIG = 3e38
_M0 = -1e29


def _compress_kernel(x_ref, pe_ref, w_ref, o_ref):
    acc = jnp.zeros(o_ref.shape, F32)
    for l in range(L_CMP):
        y = x_ref[:, l, :] + pe_ref[l:l + 1, :]
        acc = acc + jnp.dot(y.astype(BF16), w_ref[l], preferred_element_type=F32)
    o_ref[...] = acc


def nsa_compress(x, pe, w, *, tb=256):
    nb = x.shape[0]
    tb = _row_tile(nb, tb)
    pe2 = jnp.concatenate([pe] * NSA_KV, axis=1)
    z = jnp.zeros_like(w)
    w2 = jnp.concatenate([jnp.concatenate([w, z], axis=2), jnp.concatenate([z, w], axis=2)], axis=1).astype(BF16)
    return pl.pallas_call(
        _compress_kernel,
        grid=(nb // tb,),
        in_specs=[pl.BlockSpec((tb, L_CMP, NSA_KW), lambda i: (i, 0, 0)),
                  pl.BlockSpec((L_CMP, NSA_KW), lambda i: (0, 0)),
                  pl.BlockSpec((L_CMP, NSA_KW, NSA_KW), lambda i: (0, 0, 0))],
        out_specs=pl.BlockSpec((tb, NSA_KW), lambda i: (i, 0)),
        out_shape=jax.ShapeDtypeStruct((nb, NSA_KW), F32),
        compiler_params=_params("parallel"),
        name="nsa_compress",
    )(x, pe2, w2)


def _compress_cache_kernel(x_ref, pe_ref, w_ref, o_ref):
    for g in range(NSA_KV):
        acc = jnp.zeros((o_ref.shape[0], NSA_HD), F32)
        for l in range(L_CMP):
            y = x_ref[0, :, l, g, :] + pe_ref[l:l + 1, :]
            acc = acc + jnp.dot(y.astype(BF16), w_ref[l], preferred_element_type=F32)
        o_ref[:, g * NSA_HD:(g + 1) * NSA_HD] = acc


def nsa_compress_cache(cache, layer, pe, w, *, tb=256):
    nl, n_pool, page = cache.shape[:3]
    nb = n_pool * page // L_CMP
    tb = _row_tile(nb, tb)
    x = cache.reshape(nl, nb, L_CMP, NSA_KV, NSA_HD)
    return pl.pallas_call(
        _compress_cache_kernel,
        grid=(nb // tb,),
        in_specs=[pl.BlockSpec((1, tb, L_CMP, NSA_KV, NSA_HD), lambda i: (layer, i, 0, 0, 0)),
                  pl.BlockSpec((L_CMP, NSA_HD), lambda i: (0, 0)),
                  pl.BlockSpec((L_CMP, NSA_HD, NSA_HD), lambda i: (0, 0, 0))],
        out_specs=pl.BlockSpec((tb, NSA_KW), lambda i: (i, 0)),
        out_shape=jax.ShapeDtypeStruct((nb, NSA_KW), F32),
        compiler_params=_params("parallel"),
        name="nsa_compress_cache",
    )(x, pe, w.astype(BF16))


def _qz(q, g, nq):
    lane = lax.broadcasted_iota(jnp.int32, (nq, NSA_KW), 1)
    keep = (lane >> 6) == g
    parts = []
    for r in range(NSA_REP):
        h = g * NSA_REP + r
        slab = q[:, (h // 2) * NSA_KW:(h // 2 + 1) * NSA_KW]
        if h % 2 != g:
            slab = pltpu.roll(slab, NSA_HD, axis=1)
        parts.append(jnp.where(keep, slab, 0.0))
    return jnp.concatenate(parts, axis=0).astype(BF16)


def _softmax_init(m_ref, l_ref, acc_ref):
    m_ref[...] = jnp.full(m_ref.shape, NEG, F32)
    l_ref[...] = jnp.zeros(l_ref.shape, F32)
    acc_ref[...] = jnp.zeros(acc_ref.shape, F32)


def _softmax_tile(st, dpos, maskf, slope, vt, m_ref, l_ref, acc_ref, g):
    s = st - slope * dpos
    sm = jnp.where(maskf > 0.0, s, NEG)
    m_old = m_ref[g]
    m_new = jnp.maximum(m_old, jnp.max(sm, axis=0, keepdims=True))
    alpha = jnp.exp(m_old - m_new)
    p = jnp.exp(sm - m_new) * maskf
    l_ref[g] = alpha * l_ref[g] + jnp.sum(p, axis=0, keepdims=True)
    acc_ref[g] = alpha * acc_ref[g] + jnp.dot(vt, p.astype(BF16), preferred_element_type=F32)
    m_ref[g] = m_new


def _softmax_done(l_ref, acc_ref, g):
    l = l_ref[g]
    return acc_ref[g] * jnp.where(l > 0.0, 1.0 / l, 0.0)


def _compressed_branch(qz, kcb, vcbt_g, qpos, slope, ncb):
    st = lax.dot_general(kcb, qz, _NT, preferred_element_type=F32)
    row = lax.broadcasted_iota(jnp.int32, (ncb, 1), 0)
    half = ncb // 2
    blk = jnp.where(row < half, 2 * row, 2 * (row - half) + 1)
    c_pos = blk * L_CMP + (L_CMP - 1)
    d_c = qpos - c_pos
    maskf = jnp.where(d_c >= 0, 1.0, 0.0)
    s = st - slope * d_c.astype(F32)
    sm = jnp.where(d_c >= 0, s, NEG)
    m = jnp.max(sm, axis=0, keepdims=True)
    p = jnp.exp(sm - m) * maskf
    l = jnp.sum(p, axis=0, keepdims=True)
    p = p * jnp.where(l > 0.0, 1.0 / l, 0.0)
    o = jnp.dot(vcbt_g, p.astype(BF16), preferred_element_type=F32)
    return o, p


def _select_blocks(imp_sel, qpos, nsp):
    cols = imp_sel.shape[1]
    blk = lax.broadcasted_iota(jnp.int32, (nsp, cols), 0)
    cur = qpos >> 6
    valid = blk <= cur
    forced = jnp.where(valid, jnp.where(blk == 0, 1.0, jnp.where(blk >= cur - 1, 1.0, 0.0)), 0.0)
    score = jnp.where(forced > 0.0, _BIG, jnp.where(valid, imp_sel, -1.0))
    sel = jnp.zeros((nsp, cols), F32)
    for _ in range(N_SEL):
        m = jnp.max(score, axis=0, keepdims=True)
        idx = jnp.min(jnp.where(score == m, blk, nsp + 1), axis=0, keepdims=True)
        pick = blk == idx
        sel = jnp.where(pick, 1.0, sel)
        score = jnp.where(pick, -2.0, score)
    return jnp.where(valid, sel, 0.0)


def _nsa_prompt_kernel(q_ref, glog_ref, kcb_ref, vcbt_ref, ks_ref, vst_ref, kw_ref, vwt_ref,
                       qoff_ref, slope_ref, qaux_ref, o_ref, sel_ref, m_ref, l_ref, acc_ref, *, nq, ncb):
    i = pl.program_id(1)
    st0 = i * nq
    qoff = qoff_ref[...]
    qpos = st0 + qoff
    q = q_ref[0] * (NSA_HD ** -0.5)
    key_io = lax.broadcasted_iota(jnp.int32, (_KT, 1), 0)
    lane = lax.broadcasted_iota(jnp.int32, (_KT, NSA_KW), 1)
    key_lane = jnp.where(lane == 1, lax.broadcasted_iota(jnp.int32, (_KT, NSA_KW), 0), 0).astype(F32)
    causal = key_io <= qoff
    anti = key_io >= qoff
    nsp = ncb // 2
    wt = WINDOW // _KT

    def m_init():
        m_ref[...] = jnp.full(m_ref.shape, _M0, F32)
        l_ref[...] = jnp.zeros(l_ref.shape, F32)
        acc_ref[...] = jnp.zeros(acc_ref.shape, F32)

    for g in range(NSA_KV):
        qz = _qz(q, g, nq)
        qx = jnp.concatenate([qz, qaux_ref[g]], axis=1)
        slope = slope_ref[g]
        rows = slice(g * NSA_HD, (g + 1) * NSA_HD)
        o_c, p_c = _compressed_branch(qz, kcb_ref[0], vcbt_ref[0, rows, :], qpos, slope, ncb)
        imp = p_c[:, 0:nq]
        for r in range(1, NSA_REP):
            imp = imp + p_c[:, r * nq:(r + 1) * nq]
        imp_sel = imp[0:nsp] + imp[nsp:ncb]
        sel = _select_blocks(imp_sel, qpos[:, 0:nq], nsp)
        sel_ref[g] = jnp.concatenate([jnp.where(sel > 0.0, 0.0, NEG)] * NSA_REP, axis=1)
        blk = lax.broadcasted_iota(jnp.int32, (nsp, 1), 0)
        row_any = jnp.max(sel, axis=1, keepdims=True)
        first = jnp.min(jnp.where(row_any > 0.0, jnp.where(blk >= 2, blk, 2 * nsp), 2 * nsp), axis=0, keepdims=True)
        kt_lo = jnp.minimum(first[0, 0] >> 1, i)

        def tile(k_ref, vt_ref, kt, mode, with_sel):
            off = kt * _KT if isinstance(kt, int) else pl.multiple_of(kt * _KT, _KT)
            kaux =jnp.where(lane == 0, (kt - i).astype(F32), key_lane).astype(BF16)
            kx = jnp.concatenate([k_ref[0, pl.ds(off, _KT), :], kaux], axis=1)
            s = lax.dot_general(kx, qx, _NT, preferred_element_type=F32)
            if with_sel:
                half = _KT // 2
                s = jnp.concatenate([s[0:half] + sel_ref[g, pl.ds(2 * kt, 1), :],
                                     s[half:_KT] + sel_ref[g, pl.ds(2 * kt + 1, 1), :]], axis=0)
            if mode == "causal":
                s = jnp.where(causal, s, NEG)
            elif mode == "anti":
                s = jnp.where(anti, s, NEG)
            m_old = m_ref[g]
            m_new = jnp.maximum(m_old, jnp.max(s, axis=0, keepdims=True))
            alpha = jnp.exp(m_old - m_new)
            p = jnp.exp(s - m_new)
            l_ref[g] = alpha * l_ref[g] + jnp.sum(p, axis=0, keepdims=True)
            vt = vt_ref[0, rows, pl.ds(off, _KT)]
            acc_ref[g] = alpha * acc_ref[g] + jnp.dot(vt, p.astype(BF16), preferred_element_type=F32)
            m_ref[g] = m_new

        m_init()

        @pl.when(i > 0)
        def _():
            tile(ks_ref, vst_ref, 0, "none", True)

        def sel_body(kt, carry):
            tile(ks_ref, vst_ref, kt, "none", True)
            return carry

        lax.fori_loop(jnp.maximum(kt_lo, 1), i, sel_body, 0)
        tile(ks_ref, vst_ref, i, "causal", True)
        o_s = _softmax_done(l_ref, acc_ref, g)

        m_init()

        @pl.when(i >= wt)
        def _():
            tile(kw_ref, vwt_ref, i - wt, "anti", False)

        def win_body(kt, carry):
            tile(kw_ref, vwt_ref, kt, "none", False)
            return carry

        lax.fori_loop(jnp.maximum(i - wt + 1, 0), i, win_body, 0)
        tile(kw_ref, vwt_ref, i, "causal", False)
        o_w = _softmax_done(l_ref, acc_ref, g)

        gc = _sigmoid(glog_ref[0, 0, g:g + 1, :])
        gs = _sigmoid(glog_ref[0, 0, 2 + g:3 + g, :])
        gw = _sigmoid(glog_ref[0, 0, 4 + g:5 + g, :])
        o_ref[0, 0, g] = gc * o_c + gs * o_s + gw * o_w


_TS = 256


def _pos_lanes(nkeys, tile_off):
    lane = lax.broadcasted_iota(jnp.int32, (nkeys, NSA_KW), 1)
    key = lax.broadcasted_iota(jnp.int32, (nkeys, NSA_KW), 0)
    hi = (tile_off + (key >> 7)).astype(F32)
    lo = (key & (_KT - 1)).astype(F32)
    return jnp.where(lane == 0, hi, jnp.where(lane == 1, lo, 0.0)).astype(BF16)


def _nsa_prompt_kernel2(q_ref, glog_ref, kcb_ref, vcbt_ref, ks_ref, vst_ref, kw_ref, vwt_ref,
                        qoff_ref, slope_ref, qaux_ref, o_ref, sel_ref, m_ref, l_ref, acc_ref, *, nq, ncb):
    i = pl.program_id(1)
    st0 = i * nq
    qoff = qoff_ref[...]
    qpos = st0 + qoff
    q = q_ref[0] * (NSA_HD ** -0.5)
    nsp = ncb // 2
    bpt = _TS // L_SEL
    idiag = st0 // _TS
    groups = range(NSA_KV)
    rows = [slice(g * NSA_HD, (g + 1) * NSA_HD) for g in groups]

    qxs, o_cs, firsts = [], [], []
    for g in groups:
        qz = _qz(q, g, nq)
        qxs.append(jnp.concatenate([qz, qaux_ref[g]], axis=1))
        o_c, p_c = _compressed_branch(qz, kcb_ref[0], vcbt_ref[0, rows[g], :], qpos, slope_ref[g], ncb)
        o_cs.append(o_c)
        imp = p_c[:, 0:nq]
        for r in range(1, NSA_REP):
            imp = imp + p_c[:, r * nq:(r + 1) * nq]
        sel = _select_blocks(imp[0:nsp] + imp[nsp:ncb], qpos[:, 0:nq], nsp)
        sel_ref[g] = jnp.concatenate([jnp.where(sel > 0.0, 0.0, NEG)] * NSA_REP, axis=1)
        blk = lax.broadcasted_iota(jnp.int32, (nsp, 1), 0)
        row_any = jnp.max(sel, axis=1, keepdims=True)
        first = jnp.min(jnp.where(row_any > 0.0, jnp.where(blk >= bpt, blk, nsp * bpt), nsp * bpt),
                        axis=0, keepdims=True)
        firsts.append(first[0, 0])
    lo = jnp.minimum(jnp.minimum(firsts[0], firsts[1]) // bpt, idiag)

    m_ref[...] = jnp.full(m_ref.shape, _M0, F32)
    l_ref[...] = jnp.zeros(l_ref.shape, F32)
    acc_ref[...] = jnp.zeros(acc_ref.shape, F32)

    def sel_tile(kt, diag):
        off = kt * _TS if isinstance(kt, int) else pl.multiple_of(kt * _TS, _TS)
        kx = jnp.concatenate([ks_ref[0, pl.ds(off, _TS), :], _pos_lanes(_TS, kt * (_TS // _KT) - i)], axis=1)
        if diag:
            visible = (off + lax.broadcasted_iota(jnp.int32, (_TS, 1), 0)) <= qpos
        for g in groups:
            s = lax.dot_general(kx, qxs[g], _NT, preferred_element_type=F32)
            s = jnp.concatenate([s[b * L_SEL:(b + 1) * L_SEL] + sel_ref[g, pl.ds(bpt * kt + b, 1), :]
                                 for b in range(bpt)], axis=0)
            if diag:
                s = jnp.where(visible, s, NEG)
            m_old = m_ref[g]
            m_new = jnp.maximum(m_old, jnp.max(s, axis=0, keepdims=True))
            alpha = jnp.exp(m_old - m_new)
            p = jnp.exp(s - m_new)
            l_ref[g] = alpha * l_ref[g] + jnp.sum(p, axis=0, keepdims=True)
            vt = vst_ref[rows[g], pl.ds(off, _TS)]
            acc_ref[g] = alpha * acc_ref[g] + jnp.dot(vt, p.astype(BF16), preferred_element_type=F32)
            m_ref[g] = m_new

    @pl.when(idiag > 0)
    def _():
        sel_tile(0, False)

    def sel_body(kt, carry):
        sel_tile(kt, False)
        return carry

    lax.fori_loop(lo, idiag, sel_body, 0)
    sel_tile(idiag, True)

    wk = WINDOW + nq
    wstart = pl.multiple_of(jnp.maximum(i - WINDOW // nq, 0) * nq, nq)
    kxw = jnp.concatenate([kw_ref[0, pl.ds(wstart, wk), :], _pos_lanes(wk, wstart // _KT - i)], axis=1)
    d_w = qpos - (wstart + lax.broadcasted_iota(jnp.int32, (wk, 1), 0))
    in_window = jnp.where(d_w >= 0, d_w, WINDOW + 1) <= WINDOW
    for g in groups:
        s = jnp.where(in_window, lax.dot_general(kxw, qxs[g], _NT, preferred_element_type=F32), NEG)
        m = jnp.max(s, axis=0, keepdims=True)
        p = jnp.exp(s - m)
        l = jnp.sum(p, axis=0, keepdims=True)
        o_w = jnp.dot(vwt_ref[rows[g], pl.ds(wstart, wk)], p.astype(BF16), preferred_element_type=F32) / l
        o_s = _softmax_done(l_ref, acc_ref, g)
        gc = _sigmoid(glog_ref[0, 0, g:g + 1, :])
        gs = _sigmoid(glog_ref[0, 0, 2 + g:3 + g, :])
        gw = _sigmoid(glog_ref[0, 0, 4 + g:5 + g, :])
        o_t = gc * o_cs[g] + gs * o_s + gw * o_w
        for pp in range(NSA_REP // 2):
            pair = jnp.concatenate([o_t[:, (2 * pp) * nq:(2 * pp + 1) * nq],
                                    o_t[:, (2 * pp + 1) * nq:(2 * pp + 2) * nq]], axis=0)
            slab = g * (NSA_REP // 2) + pp
            o_ref[0, :, slab * NSA_KW:(slab + 1) * NSA_KW] = pair.T


def _nsa_cols(nq):
    c = NSA_REP * nq
    qoff = (jnp.arange(c, dtype=jnp.int32) % nq).reshape(1, c)
    slopes = 2.0 ** (-8.0 * jnp.arange(1, NSA_HEADS + 1, dtype=F32) / NSA_HEADS)
    slope = jnp.repeat(slopes.reshape(NSA_KV, NSA_REP), nq, axis=1).reshape(NSA_KV, 1, c)
    return qoff, slope


def _gate_cols(glog, nq):
    b, t, _ = glog.shape
    x = glog.reshape(b, t // nq, nq, NSA_KV, NSA_REP, 3)
    return x.transpose(0, 1, 5, 3, 4, 2).reshape(b, t // nq, 3 * NSA_KV, NSA_REP * nq)


def _even_odd(x):
    return jnp.concatenate([x[:, 0::2], x[:, 1::2]], axis=1)


def _uncols(o, nq):
    b, nb = o.shape[:2]
    x = o.reshape(b, nb, NSA_KV, NSA_HD, NSA_REP, nq)
    return x.transpose(0, 1, 5, 2, 4, 3).reshape(b, nb * nq, NSA_W)


def nsa_prompt(q, glog, kcb, vcb, ks, vs_t, kw, vw_t, *, nq=128):
    b, t, _ = q.shape
    ncb = kcb.shape[1]
    c = NSA_REP * nq
    nblk = t // nq
    assert nq == _KT and t % _TS == 0 and t >= WINDOW + nq and ncb == t // L_CMP
    qoff, slope = _nsa_cols(nq)
    lane = jnp.arange(NSA_KW)[None, None, :]
    slope_col = slope.reshape(NSA_KV, c, 1)
    qaux = jnp.where(lane == 0, slope_col * _KT, jnp.where(lane == 1, slope_col, 0.0)).astype(BF16)
    kcb_p = _even_odd(kcb).astype(BF16)
    vcbt = _even_odd(vcb).transpose(0, 2, 1).astype(BF16)
    per_b = lambda shape: pl.BlockSpec((1,) + shape, lambda i, j: (i, 0, 0))
    out = pl.pallas_call(
        functools.partial(_nsa_prompt_kernel2, nq=nq, ncb=ncb),
        grid=(b, nblk),
        in_specs=[pl.BlockSpec((1, nq, NSA_W), lambda i, j: (i, j, 0)),
                  pl.BlockSpec((1, 1, 3 * NSA_KV, c), lambda i, j: (i, j, 0, 0)),
                  per_b((ncb, NSA_KW)), per_b((NSA_KW, ncb)),
                  per_b((t, NSA_KW)), pl.BlockSpec((NSA_KW, t), lambda i, j: (0, i)),
                  per_b((t, NSA_KW)), pl.BlockSpec((NSA_KW, t), lambda i, j: (0, i)),
                  pl.BlockSpec((1, c), lambda i, j: (0, 0)),
                  pl.BlockSpec((NSA_KV, 1, c), lambda i, j: (0, 0, 0)),
                  pl.BlockSpec((NSA_KV, c, NSA_KW), lambda i, j: (0, 0, 0))],
        out_specs=pl.BlockSpec((1, nq, NSA_W), lambda i, j: (i, j, 0)),
        out_shape=jax.ShapeDtypeStruct((b, t, NSA_W), F32),
        scratch_shapes=[pltpu.VMEM((NSA_KV, ncb // 2, c), F32),
                        pltpu.VMEM((NSA_KV, 1, c), F32), pltpu.VMEM((NSA_KV, 1, c), F32),
                        pltpu.VMEM((NSA_KV, NSA_HD, c), F32)],
        compiler_params=_params("parallel", "arbitrary"),
        name="nsa_prompt",
    )(q, _gate_cols(glog, nq), kcb_p, vcbt, ks, vs_t, kw, vw_t, qoff, slope, qaux)
    return out


def _nsa_sample_kernel(pt_ref, q_ref, glog_ref, kcb_ref, vcbt_ref, *refs, nq, ncb, npages, page, past, wb):
    ks_pages = refs[0:npages]
    vs_pages = refs[npages:2 * npages]
    (nks_ref, nvs_ref, nkw_ref, nvw_ref, wk_ref, wv_ref, qoff_ref, slope_ref, rmat_ref,
     o_ref, wko_ref, wvo_ref, m_ref, l_ref, acc_ref) = refs[2 * npages:]
    del pt_ref
    c = NSA_REP * nq
    qpos = past + qoff_ref[...]
    q = q_ref[0] * (NSA_HD ** -0.5)
    key_io = lax.broadcasted_iota(jnp.int32, (_KT, 1), 0)
    nsp = sel_rows = -(-(past + nq) // L_SEL)
    nsp = -(-nsp // 8) * 8
    pad_rows = lambda x: jnp.concatenate([x, jnp.zeros((_KT - nq, x.shape[1]), F32)], axis=0)

    qzs, slopes, sels, o_cs = [], [], [], []
    for g in range(NSA_KV):
        qz = _qz(q, g, nq)
        slope = slope_ref[g]
        o_c, p_c = _compressed_branch(qz, kcb_ref[0], vcbt_ref[0, g * NSA_HD:(g + 1) * NSA_HD, :], qpos, slope, ncb)
        imp = _dot_exact_rhs(p_c, rmat_ref[...])
        imp_sel = imp[0:ncb // 2] + imp[ncb // 2:ncb]
        imp_sel = jnp.concatenate([imp_sel, jnp.zeros((nsp - ncb // 2, c), F32)], axis=0)
        qzs.append(qz)
        slopes.append(slope)
        sels.append(_select_blocks(imp_sel, qpos, nsp))
        o_cs.append(o_c)
    del sel_rows

    qgs = [qzs[g][:, g * NSA_HD:(g + 1) * NSA_HD] for g in range(NSA_KV)]

    def run_tile(kv_fn, tok0, mask_fn):
        dpos = qpos - (tok0 + key_io)
        for g in range(NSA_KV):
            k, v = kv_fn(g)
            st = lax.dot_general(k.astype(BF16), qgs[g], _NT, preferred_element_type=F32)
            maskf = mask_fn(g, dpos)
            s = st - slopes[g] * dpos.astype(F32)
            sm = jnp.where(maskf > 0.0, s, NEG)
            m_old = m_ref[g]
            m_new = jnp.maximum(m_old, jnp.max(sm, axis=0, keepdims=True))
            alpha = jnp.exp(m_old - m_new)
            p = jnp.exp(sm - m_new) * maskf
            l_ref[g] = alpha * l_ref[g] + jnp.sum(p, axis=0, keepdims=True)
            pv = lax.dot_general(v.astype(BF16), p.astype(BF16), _TN, preferred_element_type=F32)
            acc_ref[g] = alpha * acc_ref[g] + pv
            m_ref[g] = m_new

    cache_tile = lambda k_ref, v_ref, lo: (lambda g: (k_ref[0, 0, lo:lo + _KT, g, :], v_ref[0, 0, lo:lo + _KT, g, :]))
    new_tile = lambda k_ref, v_ref: (lambda g: (pad_rows(k_ref[0, :, g * NSA_HD:(g + 1) * NSA_HD]),
                                                pad_rows(v_ref[0, :, g * NSA_HD:(g + 1) * NSA_HD])))

    _softmax_init(m_ref, l_ref, acc_ref)
    for j in range(npages + 1):
        blk0 = j * (page // L_SEL)

        def sel_mask(g, dpos, blk0=blk0):
            selrow = jnp.where(key_io < L_SEL, sels[g][blk0:blk0 + 1, :], sels[g][blk0 + 1:blk0 + 2, :])
            return jnp.where(dpos >= 0, selrow, 0.0)

        if j < npages:
            run_tile(cache_tile(ks_pages[j], vs_pages[j], 0), j * page, sel_mask)
        else:
            run_tile(new_tile(nks_ref, nvs_ref), past, sel_mask)
    o_ss = [_softmax_done(l_ref, acc_ref, g) for g in range(NSA_KV)]

    _softmax_init(m_ref, l_ref, acc_ref)
    win_mask = lambda g, dpos: jnp.where(dpos >= 0, jnp.where(dpos <= WINDOW, 1.0, 0.0), 0.0)
    for j in range(wb // _KT):
        run_tile(cache_tile(wk_ref, wv_ref, j * _KT), past - wb + j * _KT, win_mask)
    run_tile(new_tile(nkw_ref, nvw_ref), past, win_mask)
    for g in range(NSA_KV):
        o_w = _softmax_done(l_ref, acc_ref, g)
        gc = _sigmoid(glog_ref[0, 0, g:g + 1, :])
        gs = _sigmoid(glog_ref[0, 0, 2 + g:3 + g, :])
        gw = _sigmoid(glog_ref[0, 0, 4 + g:5 + g, :])
        o_ref[0, 0, g] = gc * o_cs[g] + gs * o_ss[g] + gw * o_w

    wko_ref[0, 0:wb - nq] = wk_ref[0, 0, nq:wb]
    wvo_ref[0, 0:wb - nq] = wv_ref[0, 0, nq:wb]
    for g in range(NSA_KV):
        wko_ref[0, wb - nq:wb, g, :] = nkw_ref[0, :, g * NSA_HD:(g + 1) * NSA_HD]
        wvo_ref[0, wb - nq:wb, g, :] = nvw_ref[0, :, g * NSA_HD:(g + 1) * NSA_HD]


def nsa_sample(q, glog, kcb, vcb, pool_k, pool_v, page_table, nks, nvs, nkw, nvw, win_k, win_v, layer):
    b, nq, _ = q.shape
    ncb = kcb.shape[1]
    npages = page_table.shape[1]
    page = pool_k.shape[2]
    past = npages * page
    wb = win_k.shape[2]
    assert page == _KT and wb % _KT == 0 and nq % 8 == 0 and nq <= L_SEL and ncb % 2 == 0
    c = NSA_REP * nq
    qoff, slope = _nsa_cols(nq)
    col = jnp.arange(c, dtype=jnp.int32)
    rmat = (col[:, None] % nq == col[None, :] % nq).astype(BF16)
    kcb_p = _even_odd(kcb).astype(BF16)
    vcbt = _even_odd(vcb).transpose(0, 2, 1).astype(BF16)
    per_b = lambda shape: pl.BlockSpec((1,) + shape, lambda i, pt: (i,) + (0,) * len(shape))
    const = lambda shape: pl.BlockSpec(shape, lambda i, pt: (0,) * len(shape))
    page_spec = lambda j: pl.BlockSpec((1, 1, page, NSA_KV, NSA_HD), lambda i, pt: (layer, pt[i, j], 0, 0, 0))
    win_spec = pl.BlockSpec((1, 1, wb, NSA_KV, NSA_HD), lambda i, pt: (layer, i, 0, 0, 0))
    in_specs = ([per_b((nq, NSA_W)), per_b((1, 3 * NSA_KV, c)), per_b((ncb, NSA_KW)), per_b((NSA_KW, ncb))]
                + [page_spec(j) for j in range(npages)] * 2
                + [per_b((nq, NSA_KW))] * 4 + [win_spec] * 2
                + [const((1, c)), const((NSA_KV, 1, c)), const((c, c))])
    out, wko, wvo = pl.pallas_call(
        functools.partial(_nsa_sample_kernel, nq=nq, ncb=ncb, npages=npages, page=page, past=past, wb=wb),
        grid_spec=pltpu.PrefetchScalarGridSpec(
            num_scalar_prefetch=1,
            grid=(b,),
            in_specs=in_specs,
            out_specs=[per_b((1, NSA_KV, NSA_HD, c)), per_b((wb, NSA_KV, NSA_HD)), per_b((wb, NSA_KV, NSA_HD))],
            scratch_shapes=[pltpu.VMEM((NSA_KV, 1, c), F32), pltpu.VMEM((NSA_KV, 1, c), F32),
                            pltpu.VMEM((NSA_KV, NSA_HD, c), F32)],
        ),
        out_shape=[jax.ShapeDtypeStruct((b, 1, NSA_KV, NSA_HD, c), F32),
                   jax.ShapeDtypeStruct((b, wb, NSA_KV, NSA_HD), F32),
                   jax.ShapeDtypeStruct((b, wb, NSA_KV, NSA_HD), F32)],
        compiler_params=_params("arbitrary"),
        name="nsa_sample",
    )(page_table, q, _gate_cols(glog, nq), kcb_p, vcbt, *([pool_k] * npages), *([pool_v] * npages),
      nks, nvs, nkw, nvw, win_k, win_v, qoff, slope, rmat)
    return _uncols(out, nq), wko, wvo


def _compress_pages_kernel(x_ref, pe_ref, w_ref, o_ref):
    for g in range(NSA_KV):
        acc = jnp.zeros(o_ref.shape[1:], F32)
        for d in range(0, NSA_HD, 2):
            y = jnp.concatenate([x_ref[0, :, g, d, :] + pe_ref[d:d + 1, :],
                                 x_ref[0, :, g, d + 1, :] + pe_ref[d + 1:d + 2, :]], axis=1)
            acc = acc + jnp.dot(y.astype(BF16), w_ref[d // 2], preferred_element_type=F32)
        o_ref[g] = acc


def nsa_compress_pages(cache_t, layer, pe, w, *, tp=256):
    n_pool, page = cache_t.shape[1], cache_t.shape[4]
    nblk = page // L_CMP
    tp = _row_tile(n_pool, tp)
    pe_t = jnp.tile(pe.T, (1, nblk))
    eye = jnp.eye(nblk, dtype=F32)
    wd = jnp.einsum('nm,lde->dnlme', eye, w).reshape(NSA_HD // 2, 2 * page, nblk * NSA_HD).astype(BF16)
    out = pl.pallas_call(
        _compress_pages_kernel,
        grid=(n_pool // tp,),
        in_specs=[pl.BlockSpec((1, tp, NSA_KV, NSA_HD, page), lambda i: (layer, i, 0, 0, 0)),
                  pl.BlockSpec((NSA_HD, page), lambda i: (0, 0)),
                  pl.BlockSpec((NSA_HD // 2, 2 * page, nblk * NSA_HD), lambda i: (0, 0, 0))],
        out_specs=pl.BlockSpec((NSA_KV, tp, nblk * NSA_HD), lambda i: (0, i, 0)),
        out_shape=jax.ShapeDtypeStruct((NSA_KV, n_pool, nblk * NSA_HD), F32),
        compiler_params=_params("parallel"),
        name="nsa_compress_pages",
    )(cache_t, pe_t, wd)
    return out.reshape(NSA_KV, n_pool, nblk, NSA_HD).transpose(1, 2, 0, 3).reshape(n_pool, nblk, NSA_KW)


def _row_softmax(s, mask):
    sm = jnp.where(mask, s, NEG)
    m = jnp.max(sm, axis=1, keepdims=True)
    p = jnp.where(mask, jnp.exp(sm - m), 0.0)
    l = jnp.sum(p, axis=1, keepdims=True)
    return p * jnp.where(l > 0.0, 1.0 / l, 0.0)


def _nsa_decode_kernel(pt_ref, q_ref, gl_ref, kcb_ref, vcb_ref, *refs, nq, ncb, npages, page, past, wb):
    ks_pages = refs[0:npages]
    vs_pages = refs[npages:2 * npages]
    (nks_ref, nvs_ref, nkw_ref, nvw_ref, wk_ref, wv_ref, slope_ref, expand_ref,
     o_ref, wko_ref, wvo_ref) = refs[2 * npages:]
    del pt_ref
    c = NSA_REP * nq
    nblk_lanes = _KT
    q = q_ref[0] * (NSA_HD ** -0.5)
    row = lax.broadcasted_iota(jnp.int32, (c, 1), 0)
    qpos = past + (row & (nq - 1))
    qposf = qpos.astype(F32)
    pad_rows = lambda x: jnp.concatenate([x, jnp.zeros((_KT - nq, x.shape[1]), F32)], axis=0)
    lane = lax.broadcasted_iota(jnp.int32, (1, _KT), 1)
    n_sel_keys = (npages + 1) * page
    key_all = lax.broadcasted_iota(jnp.int32, (1, n_sel_keys), 1)
    expand = expand_ref[...]
    pr = lax.broadcasted_iota(jnp.int32, (ncb, nblk_lanes), 0)
    pc = lax.broadcasted_iota(jnp.int32, (ncb, nblk_lanes), 1)
    pair = jnp.where(jnp.where(pr < ncb // 2, pr, pr - ncb // 2) == pc, 1.0, 0.0).astype(BF16)
    new_kt = pad_rows(nkw_ref[0]).T
    new_vt = pad_rows(nvw_ref[0]).T

    for g in range(NSA_KV):
        gs = slice(g * NSA_HD, (g + 1) * NSA_HD)
        qg = jnp.concatenate([q[:, (g * NSA_REP + r) * NSA_HD:(g * NSA_REP + r + 1) * NSA_HD]
                              for r in range(NSA_REP)], axis=0).astype(BF16)
        slope = slope_ref[g]
        s_c = lax.dot_general(qg, kcb_ref[0, :, gs], _NT, preferred_element_type=F32)
        cl = lax.broadcasted_iota(jnp.int32, (1, ncb), 1)
        half = ncb // 2
        c_pos = jnp.where(cl < half, 2 * cl, 2 * (cl - half) + 1) * L_CMP + (L_CMP - 1)
        d_c = qpos - c_pos
        p_c = _row_softmax(s_c - slope * d_c.astype(F32), d_c >= 0)
        o_c = jnp.dot(p_c.astype(BF16), vcb_ref[0, :, gs], preferred_element_type=F32)
        imp = p_c[0:nq]
        for r in range(1, NSA_REP):
            imp = imp + p_c[r * nq:(r + 1) * nq]
        imp_sel = _dot_exact_rhs(imp, pair)
        cur = (past + lax.broadcasted_iota(jnp.int32, (nq, 1), 0)) >> 6
        valid = lane <= cur
        forced = jnp.where(valid, jnp.where(lane == 0, 1.0, jnp.where(lane >= cur - 1, 1.0, 0.0)), 0.0)
        score = jnp.where(forced > 0.0, _BIG, jnp.where(valid, imp_sel, -1.0))
        sel = jnp.zeros((nq, nblk_lanes), F32)
        for _ in range(N_SEL):
            m = jnp.max(score, axis=1, keepdims=True)
            idx = jnp.min(jnp.where(score == m, lane, nblk_lanes + 1), axis=1, keepdims=True)
            pick = lane == idx
            sel = jnp.where(pick, 1.0, sel)
            score = jnp.where(pick, -2.0, score)
        sel = jnp.where(valid, sel, 0.0)
        sel_keys = jnp.dot(sel.astype(BF16), expand, preferred_element_type=F32)
        sel_keys = jnp.concatenate([sel_keys] * NSA_REP, axis=0)

        nk = pad_rows(nks_ref[0, :, gs]).astype(BF16)
        nv = pad_rows(nvs_ref[0, :, gs]).astype(BF16)
        s_parts = [jnp.dot(qg, ks_pages[j][0, 0, g].astype(BF16), preferred_element_type=F32) for j in range(npages)]
        s_parts.append(lax.dot_general(qg, nk, _NT, preferred_element_type=F32))
        s_s = jnp.concatenate(s_parts, axis=1)
        d_s = qpos - key_all
        p_s = _row_softmax(s_s - slope * d_s.astype(F32), jnp.where(d_s >= 0, sel_keys, 0.0) > 0.0).astype(BF16)
        o_s = jnp.dot(p_s[:, npages * page:], nv, preferred_element_type=F32)
        for j in range(npages):
            o_s = o_s + lax.dot_general(p_s[:, j * page:(j + 1) * page], vs_pages[j][0, 0, g].astype(BF16), _NT,
                                        preferred_element_type=F32)

        nkw = pad_rows(nkw_ref[0, :, gs]).astype(BF16)
        nvw = pad_rows(nvw_ref[0, :, gs]).astype(BF16)
        s_w = jnp.concatenate([jnp.dot(qg, wk_ref[0, 0, g].astype(BF16), preferred_element_type=F32),
                               lax.dot_general(qg, nkw, _NT, preferred_element_type=F32)], axis=1)
        w_pos = past - wb + lax.broadcasted_iota(jnp.int32, (1, wb + _KT), 1)
        d_w = qpos - w_pos
        p_w = _row_softmax(s_w - slope * d_w.astype(F32), jnp.where(d_w >= 0, d_w, WINDOW + 1) <= WINDOW).astype(BF16)
        o_w = (lax.dot_general(p_w[:, 0:wb], wv_ref[0, 0, g].astype(BF16), _NT, preferred_element_type=F32)
               + jnp.dot(p_w[:, wb:], nvw, preferred_element_type=F32))

        gate = _sigmoid(gl_ref[0, g])
        o_ref[0, g] = gate[:, 0:1] * o_c + gate[:, 1:2] * o_s + gate[:, 2:3] * o_w

        for src_ref, new_full, dst_ref in ((wk_ref, new_kt, wko_ref), (wv_ref, new_vt, wvo_ref)):
            new_t = pltpu.roll(new_full[gs, :], _KT - nq, axis=1)
            shifted = pltpu.roll(src_ref[0, 0, g], wb - nq, axis=1)
            dst_ref[0, g, :, 0:wb - _KT] = shifted[:, 0:wb - _KT]
            dst_ref[0, g, :, wb - _KT:wb] = jnp.where(lane >= _KT - nq, new_t, shifted[:, wb - _KT:wb])


_DECODE_BB = 4


def _nsa_decode_kernel2(pt_ref, q_ref, gl_ref, kcb_ref, vcb_ref, *refs, nq, ncb, npages, page, past, wb, nbb):
    ks_pages = refs[0:nbb * npages]
    vs_pages = refs[nbb * npages:2 * nbb * npages]
    (nks_ref, nvs_ref, nkw_ref, nvw_ref, wk_ref, wv_ref, slope_ref, expand_ref,
     o_ref, wko_ref, wvo_ref) = refs[2 * nbb * npages:]
    del pt_ref
    c = NSA_REP * nq
    nblk_lanes = _KT
    n_sel = -(-(past + nq) // L_SEL)
    row = lax.broadcasted_iota(jnp.int32, (c, 1), 0)
    qpos = past + (row & (nq - 1))
    pad_rows = lambda x: jnp.concatenate([x, jnp.zeros((_KT - nq, x.shape[1]), F32)], axis=0)
    lane = lax.broadcasted_iota(jnp.int32, (1, _KT), 1)
    n_sel_keys = (npages + 1) * page
    key_all = lax.broadcasted_iota(jnp.int32, (1, n_sel_keys), 1)
    expand = expand_ref[...]
    pr = lax.broadcasted_iota(jnp.int32, (ncb, nblk_lanes), 0)
    pc = lax.broadcasted_iota(jnp.int32, (ncb, nblk_lanes), 1)
    half = ncb // 2
    pair = jnp.where(jnp.where(pr < half, pr, pr - half) == pc, 1.0, 0.0).astype(BF16)
    cl = lax.broadcasted_iota(jnp.int32, (1, ncb), 1)
    c_pos = jnp.where(cl < half, 2 * cl, 2 * (cl - half) + 1) * L_CMP + (L_CMP - 1)
    d_c = qpos - c_pos
    d_cf = d_c.astype(F32)
    d_s = qpos - key_all
    d_sf = d_s.astype(F32)
    d_w = qpos - (past - wb + lax.broadcasted_iota(jnp.int32, (1, wb + _KT), 1))
    d_wf = d_w.astype(F32)
    in_window = jnp.where(d_w >= 0, d_w, WINDOW + 1) <= WINDOW
    cur = (past + lax.broadcasted_iota(jnp.int32, (nq, 1), 0)) >> 6
    valid = lane <= cur
    forced = jnp.where(valid, jnp.where(lane == 0, 1.0, jnp.where(lane >= cur - 1, 1.0, 0.0)), 0.0)

    for bb in range(nbb):
        q = q_ref[bb] * (NSA_HD ** -0.5)
        new_kt = pad_rows(nkw_ref[bb]).T
        new_vt = pad_rows(nvw_ref[bb]).T
        for g in range(NSA_KV):
            gs = slice(g * NSA_HD, (g + 1) * NSA_HD)
            qg = jnp.concatenate([q[:, (g * NSA_REP + r) * NSA_HD:(g * NSA_REP + r + 1) * NSA_HD]
                                  for r in range(NSA_REP)], axis=0).astype(BF16)
            slope = slope_ref[g]
            s_c = lax.dot_general(qg, kcb_ref[bb, :, gs], _NT, preferred_element_type=F32)
            p_c = _row_softmax(s_c - slope * d_cf, d_c >= 0)
            o_c = jnp.dot(p_c.astype(BF16), vcb_ref[bb, :, gs], preferred_element_type=F32)
            imp = p_c[0:nq]
            for r in range(1, NSA_REP):
                imp = imp + p_c[r * nq:(r + 1) * nq]
            imp_sel = _dot_exact_rhs(imp, pair)
            score = jnp.where(forced > 0.0, _BIG, jnp.where(valid, imp_sel, -1.0))
            before = jnp.zeros((nq, nblk_lanes), F32)
            for bi in range(n_sel):
                sb = score[:, bi:bi + 1]
                before = before + jnp.where(sb > score, 1.0, jnp.where(sb == score, jnp.where(lane > bi, 1.0, 0.0), 0.0))
            sel = jnp.where(valid, jnp.where(before < N_SEL, 1.0, 0.0), 0.0)
            sel_keys = jnp.dot(sel.astype(BF16), expand, preferred_element_type=F32)
            sel_keys = jnp.concatenate([sel_keys] * NSA_REP, axis=0)

            nk = pad_rows(nks_ref[bb, :, gs]).astype(BF16)
            nv = pad_rows(nvs_ref[bb, :, gs]).astype(BF16)
            s_parts = [jnp.dot(qg, ks_pages[bb * npages + j][0, 0, g].astype(BF16), preferred_element_type=F32)
                       for j in range(npages)]
            s_parts.append(lax.dot_general(qg, nk, _NT, preferred_element_type=F32))
            s_s = jnp.concatenate(s_parts, axis=1)
            p_s = _row_softmax(s_s - slope * d_sf, jnp.where(d_s >= 0, sel_keys, 0.0) > 0.0).astype(BF16)
            o_s = jnp.dot(p_s[:, npages * page:], nv, preferred_element_type=F32)
            for j in range(npages):
                o_s = o_s + lax.dot_general(p_s[:, j * page:(j + 1) * page],
                                            vs_pages[bb * npages + j][0, 0, g].astype(BF16), _NT,
                                            preferred_element_type=F32)

            nkw = pad_rows(nkw_ref[bb, :, gs]).astype(BF16)
            nvw = pad_rows(nvw_ref[bb, :, gs]).astype(BF16)
            s_w = jnp.concatenate([jnp.dot(qg, wk_ref[0, bb, g].astype(BF16), preferred_element_type=F32),
                                   lax.dot_general(qg, nkw, _NT, preferred_element_type=F32)], axis=1)
            p_w = _row_softmax(s_w - slope * d_wf, in_window).astype(BF16)
            o_w = (lax.dot_general(p_w[:, 0:wb], wv_ref[0, bb, g].astype(BF16), _NT, preferred_element_type=F32)
                   + jnp.dot(p_w[:, wb:], nvw, preferred_element_type=F32))

            gate = _sigmoid(gl_ref[bb, g])
            o_ref[bb, g] = gate[:, 0:1] * o_c + gate[:, 1:2] * o_s + gate[:, 2:3] * o_w

            for src_ref, new_full, dst_ref in ((wk_ref, new_kt, wko_ref), (wv_ref, new_vt, wvo_ref)):
                new_t = pltpu.roll(new_full[gs, :], _KT - nq, axis=1)
                shifted = pltpu.roll(src_ref[0, bb, g], wb - nq, axis=1)
                dst_ref[bb, g, :, 0:wb - _KT] = shifted[:, 0:wb - _KT]
                dst_ref[bb, g, :, wb - _KT:wb] = jnp.where(lane >= _KT - nq, new_t, shifted[:, wb - _KT:wb])


def nsa_decode(q, glog, kcb, vcb, pool_k, pool_v, page_table, nks, nvs, nkw, nvw, win_k, win_v, layer):
    b, nq, _ = q.shape
    ncb = kcb.shape[1]
    npages = page_table.shape[1]
    page = pool_k.shape[4]
    past = npages * page
    wb = win_k.shape[4]
    assert page == _KT and wb % _KT == 0 and nq & (nq - 1) == 0 and nq % 8 == 0 and nq <= L_SEL
    assert ncb % 2 == 0 and ncb // 2 <= _KT and -(-(past + nq) // L_SEL) <= _KT
    c = NSA_REP * nq
    _, slope = _nsa_cols(nq)
    gl = glog.reshape(b, nq, NSA_KV, NSA_REP, 3).transpose(0, 2, 3, 1, 4).reshape(b, NSA_KV, c, 3)
    nbb = _DECODE_BB if b % _DECODE_BB == 0 else 1
    per_b = lambda shape: pl.BlockSpec((nbb,) + shape, lambda i, pt: (i,) + (0,) * len(shape))
    const = lambda shape: pl.BlockSpec(shape, lambda i, pt: (0,) * len(shape))
    page_spec = lambda bb, j: pl.BlockSpec((1, 1, NSA_KV, NSA_HD, page),
                                           lambda i, pt: (layer, pt[i * nbb + bb, j], 0, 0, 0))
    page_specs = [page_spec(bb, j) for bb in range(nbb) for j in range(npages)]
    win_spec = pl.BlockSpec((1, nbb, NSA_KV, NSA_HD, wb), lambda i, pt: (layer, i, 0, 0, 0))
    n_keys = (npages + 1) * page
    expand = (jnp.arange(n_keys)[None, :] // L_SEL == jnp.arange(_KT)[:, None]).astype(BF16)
    in_specs = ([per_b((nq, NSA_W)), per_b((NSA_KV, c, 3)), per_b((ncb, NSA_KW)), per_b((ncb, NSA_KW))]
                + page_specs * 2
                + [per_b((nq, NSA_KW))] * 4 + [win_spec] * 2 + [const((NSA_KV, c, 1)), const((_KT, n_keys))])
    out, wko, wvo = pl.pallas_call(
        functools.partial(_nsa_decode_kernel2, nq=nq, ncb=ncb, npages=npages, page=page, past=past, wb=wb, nbb=nbb),
        grid_spec=pltpu.PrefetchScalarGridSpec(
            num_scalar_prefetch=1,
            grid=(b // nbb,),
            in_specs=in_specs,
            out_specs=[per_b((NSA_KV, c, NSA_HD)), per_b((NSA_KV, NSA_HD, wb)), per_b((NSA_KV, NSA_HD, wb))],
        ),
        out_shape=[jax.ShapeDtypeStruct((b, NSA_KV, c, NSA_HD), F32),
                   jax.ShapeDtypeStruct((b, NSA_KV, NSA_HD, wb), F32),
                   jax.ShapeDtypeStruct((b, NSA_KV, NSA_HD, wb), F32)],
        compiler_params=_params("arbitrary"),
        name="nsa_decode",
    )(page_table, q, gl, _even_odd(kcb).astype(BF16), _even_odd(vcb).astype(BF16),
      *([pool_k] * (nbb * npages)), *([pool_v] * (nbb * npages)), nks, nvs, nkw, nvw, win_k, win_v,
      slope.reshape(NSA_KV, c, 1), expand)
    o = out.reshape(b, NSA_KV, NSA_REP, nq, NSA_HD).transpose(0, 3, 1, 2, 4).reshape(b, nq, NSA_W)
    return o, wko, wvo


def _xattn_cache_kernel(q_ref, k_ref, v_ref, o_ref, *, nq, nbb):
    scale = XA_HD ** -0.5
    for bb in range(nbb):
        q = jnp.concatenate([q_ref[bb, :, h * XA_HD:(h + 1) * XA_HD] for h in range(XA_HEADS)], axis=0)
        k = k_ref[0, bb].astype(BF16)
        v = v_ref[0, bb].astype(BF16)
        s = lax.dot_general(q.astype(BF16), k, _NT, preferred_element_type=F32) * scale
        col_h = lax.broadcasted_iota(jnp.int32, s.shape, 1) & (XA_HEADS - 1)
        row_h = lax.broadcasted_iota(jnp.int32, s.shape, 0) >> (nq.bit_length() - 1)
        mine = col_h == row_h
        m = jnp.max(jnp.where(mine, s, NEG), axis=1, keepdims=True)
        p = jnp.where(mine, jnp.exp(s - m), 0.0)
        p = p / jnp.sum(p, axis=1, keepdims=True)
        o = jnp.dot(p.astype(BF16), v, preferred_element_type=F32)
        for h in range(XA_HEADS):
            o_ref[bb, :, h * XA_HD:(h + 1) * XA_HD] = o[h * nq:(h + 1) * nq]


def xattn_cache(q, cache_k, cache_v, layer):
    b, nq, w = q.shape
    m = cache_k.shape[2]
    assert XA_HEADS & (XA_HEADS - 1) == 0 and nq % 8 == 0
    kv = lambda a: a.reshape(a.shape[0], b, m * XA_HEADS, XA_HD)
    nbb = _DECODE_BB if b % _DECODE_BB == 0 else 1
    kv_spec = pl.BlockSpec((1, nbb, m * XA_HEADS, XA_HD), lambda i: (layer, i, 0, 0))
    return pl.pallas_call(
        functools.partial(_xattn_cache_kernel, nq=nq, nbb=nbb),
        grid=(b // nbb,),
        in_specs=[pl.BlockSpec((nbb, nq, w), lambda i: (i, 0, 0)), kv_spec, kv_spec],
        out_specs=pl.BlockSpec((nbb, nq, w), lambda i: (i, 0, 0)),
        out_shape=jax.ShapeDtypeStruct((b, nq, w), F32),
        compiler_params=_params("parallel"),
        name="xattn_cache",
    )(q, kv(cache_k), kv(cache_v))


_HALO_M = 8
_TN = (((0,), (0,)), ((), ()))


def _softplus(x):
    return jnp.maximum(x, 0.0) + jnp.log1p(jnp.exp(-jnp.abs(x)))


def _ssd_kernel(xbc_ref, z_ref, sm_ref, dtt_ref, cs_ref, h0_ref, cw_ref, cb_ref, dtb_ref, dtbt_ref,
                al_ref, alt_ref, dsk_ref, ng_ref, y_ref, ncs_ref, hf_ref, ext_ref, h_ref, yh_ref, *, ql, dt_col):
    c = pl.program_id(1)
    nc = pl.num_programs(1)

    @pl.when(c == 0)
    def _():
        ext_ref[...] = jnp.zeros_like(ext_ref)
        ext_ref[_HALO_M - (M_CONV_W - 1):_HALO_M, :] = cs_ref[0]
        h_ref[...] = h0_ref[0]

    @pl.when(c > 0)
    def _():
        ext_ref[0:_HALO_M, :] = ext_ref[ql:ql + _HALO_M, :]

    ext_ref[_HALO_M:_HALO_M + ql, :] = xbc_ref[0]
    acc = jnp.zeros((ql, M_CONV_DIM), F32)
    for k in range(M_CONV_W):
        off = _HALO_M - (M_CONV_W - 1) + k
        acc = acc + ext_ref[off:off + ql, :].astype(BF16).astype(F32) * cw_ref[k:k + 1, :]
    xbc = _silu(acc + cb_ref[...])
    xs = xbc[:, 0:M_DIN]
    bm = xbc[:, M_DIN:M_DIN + M_GROUPS * M_DSTATE]
    cm = xbc[:, M_DIN + M_GROUPS * M_DSTATE:M_CONV_DIM]

    dt = _softplus(sm_ref[0, :, dt_col:dt_col + M_HEADS] + dtb_ref[...])
    dtt = _softplus(dtt_ref[0] + dtbt_ref[...])
    dta = dt * (-jnp.exp(al_ref[...]))
    dtat = dtt * (-jnp.exp(alt_ref[...]))
    ti = lax.broadcasted_iota(jnp.int32, (ql, ql), 0)
    si = lax.broadcasted_iota(jnp.int32, (ql, ql), 1)
    causal = si <= ti
    cum = _dot_exact_lhs(jnp.where(causal, 1.0, 0.0).astype(BF16), dta)
    cumt = _dot_exact_rhs(dtat, jnp.where(ti <= si, 1.0, 0.0).astype(BF16))
    cum_last = cum[ql - 1:ql, :]
    edec = jnp.exp(cum)
    eend = jnp.exp(cum_last - cum)
    elast = jnp.exp(cum_last)

    rep = M_HEADS // M_GROUPS
    for gi in range(M_GROUPS):
        b_g = bm[:, gi * M_DSTATE:(gi + 1) * M_DSTATE]
        c_g = cm[:, gi * M_DSTATE:(gi + 1) * M_DSTATE].astype(BF16)
        cb = lax.dot_general(c_g, b_g.astype(BF16), _NT, preferred_element_type=F32)
        for hh in range(rep):
            h = gi * rep + hh
            hs = slice(h * M_HDIM, (h + 1) * M_HDIM)
            lmat = jnp.where(causal, jnp.exp(cum[:, h:h + 1] - cumt[h:h + 1, :]), 0.0)
            x_h = xs[:, hs]
            xdt = (x_h * dt[:, h:h + 1]).astype(BF16)
            y_diag = jnp.dot((cb * lmat).astype(BF16), xdt, preferred_element_type=F32)
            h_in = h_ref[h]
            y_off = lax.dot_general(c_g, h_in.astype(BF16), _NT, preferred_element_type=F32) * edec[:, h:h + 1]
            bd = (b_g * eend[:, h:h + 1]).astype(BF16)
            s_chunk = lax.dot_general(xdt, bd, _TN, preferred_element_type=F32)
            h_ref[h] = elast[:, h:h + 1] * h_in + s_chunk
            yh_ref[:, hs] = y_diag + y_off + dsk_ref[:, hs] * x_h

    yz = yh_ref[...] * _silu(z_ref[0])
    y_ref[0] = _rms(yz, ng_ref[...])

    @pl.when(c == nc - 1)
    def _():
        ncs_ref[0] = ext_ref[_HALO_M + ql - (M_CONV_W - 1):_HALO_M + ql, :]
        hf_ref[0] = h_ref[...]


def ssd_mixer(xbc, z, small, dt_col, conv_state, h0, conv_w, conv_b, dt_bias, a_log, d_skip, norm_g, *, ql):
    b, t, _ = xbc.shape
    nc = t // ql
    sw = small.shape[2]
    dtt = small[:, :, dt_col:dt_col + M_HEADS].transpose(0, 2, 1)
    const = lambda shape: pl.BlockSpec(shape, lambda i, j: (0,) * len(shape))
    per_b = lambda shape: pl.BlockSpec((1,) + shape, lambda i, j: (i,) + (0,) * len(shape))
    row = lambda x: x.reshape(1, -1)
    colv = lambda x: x.reshape(-1, 1)
    return pl.pallas_call(
        functools.partial(_ssd_kernel, ql=ql, dt_col=dt_col),
        grid=(b, nc),
        in_specs=[pl.BlockSpec((1, ql, M_CONV_DIM), lambda i, j: (i, j, 0)),
                  pl.BlockSpec((1, ql, M_DIN), lambda i, j: (i, j, 0)),
                  pl.BlockSpec((1, ql, sw), lambda i, j: (i, j, 0)),
                  pl.BlockSpec((1, M_HEADS, ql), lambda i, j: (i, 0, j)),
                  per_b((M_CONV_W - 1, M_CONV_DIM)), per_b((M_HEADS, M_HDIM, M_DSTATE)),
                  const((M_CONV_W, M_CONV_DIM)), const((1, M_CONV_DIM)),
                  const((1, M_HEADS)), const((M_HEADS, 1)), const((1, M_HEADS)), const((M_HEADS, 1)),
                  const((1, M_DIN)), const((1, M_DIN))],
        out_specs=[pl.BlockSpec((1, ql, M_DIN), lambda i, j: (i, j, 0)),
                   per_b((M_CONV_W - 1, M_CONV_DIM)), per_b((M_HEADS, M_HDIM, M_DSTATE))],
        out_shape=[jax.ShapeDtypeStruct((b, t, M_DIN), F32),
                   jax.ShapeDtypeStruct((b, M_CONV_W - 1, M_CONV_DIM), F32),
                   jax.ShapeDtypeStruct((b, M_HEADS, M_HDIM, M_DSTATE), F32)],
        scratch_shapes=[pltpu.VMEM((_HALO_M + ql, M_CONV_DIM), F32),
                        pltpu.VMEM((M_HEADS, M_HDIM, M_DSTATE), F32),
                        pltpu.VMEM((ql, M_DIN), F32)],
        compiler_params=_params("parallel", "arbitrary"),
        name="ssd_mixer",
    )(xbc, z, small, dtt, conv_state, h0, conv_w, row(conv_b), row(dt_bias), colv(dt_bias),
      row(a_log), colv(a_log), row(jnp.repeat(d_skip, M_HDIM)), row(norm_g))


_SMALL_W = 128
_OD_SPLITS = (NSA_W,) + (NSA_KW,) * 6 + (M_DIN, M_CONV_DIM, _SMALL_W)


def _odd_w_in(w):
    o_kv = NSA_W
    o_gate = o_kv + 6 * NSA_KW
    o_z = o_gate + 3 * NSA_HEADS
    o_xbc = o_z + M_DIN
    o_dt = o_xbc + M_CONV_DIM
    pad = jnp.zeros((w.shape[0], _SMALL_W - 3 * NSA_HEADS - M_HEADS), F32)
    return jnp.concatenate([w[:, :o_gate], w[:, o_z:o_xbc], w[:, o_xbc:o_dt],
                            w[:, o_gate:o_z], w[:, o_dt:], pad], axis=1)


def kernel(x_prompt, x_sample, state_conv_a, state_conv_b, cache_cmp_k, cache_cmp_v, cache_sel_k, cache_sel_v, cache_win_k, cache_win_v, state_ssm, state_ssm_conv, cache_mem_k, cache_mem_v, page_table, mem_prompt, norm_mix, norm_xattn, norm_ffn, norm_final, ev_w_in, ev_conv_a, ev_conv_b, ev_conv_b_bias, ev_ln_g, ev_ln_b, ev_w_out, od_w_in, od_cmp_pe, od_cmp_wk, od_cmp_wv, od_ssm_conv_w, od_ssm_conv_b, od_dt_bias, od_a_log, od_d_skip, od_ssm_norm, od_w_out, xa_wq, xa_wk, xa_wv, xa_wo, moe_wg, moe_bg, moe_we, moe_be, moe_w1, moe_w3, moe_w2):
    bp, tp, d = x_prompt.shape
    bs, ts, _ = x_sample.shape
    n_p, n_s = bp * tp, bs * ts
    n_mem = mem_prompt.shape[1]
    depth = norm_mix.shape[0]
    n_pool, page = cache_cmp_k.shape[1:3]
    wb = cache_win_k.shape[2]
    dt_col = 3 * NSA_HEADS

    def groups(pair):
        return pair[0].reshape(bp, tp, -1), pair[1].reshape(bs, ts, -1)

    def rows(a_p, a_s):
        return [a_p.reshape(n_p, a_p.shape[-1]), a_s.reshape(n_s, a_s.shape[-1])]

    hs = rows(x_prompt, x_sample)
    out = {k: [] for k in ("ca_p", "ca_s", "cb_p", "cb_s", "wk_p", "wk_s", "wv_p", "wv_s",
                           "sm_p", "sm_s", "sc_p", "sc_s", "mk_p", "mv_p")}
    rows_p = [[], [], [], []]
    rows_s = [[], [], [], []]
    for i in range(depth):
        j = i // 2
        if i % 2 == 0:
            u_p, u_s = groups(g_norm_matmul(hs, norm_mix[i], ev_w_in[j]))
            ev = (ev_conv_a[j], ev_conv_b[j], ev_conv_b_bias[j], ev_ln_g[j], ev_ln_b[j])
            y_p, na_p, nb_p = even_conv(u_p, jnp.zeros((bp, CONV_A_W - 1, D_A), F32),
                                        jnp.zeros((bp, CONV_B_W - 1, D_B), F32), *ev)
            y_s, na_s, nb_s = even_conv(u_s, state_conv_a[j], state_conv_b[j], *ev)
            hs = g_matmul_res([rows(y_p, y_s)], [ev_w_out[j]], hs)
            out["ca_p"].append(na_p)
            out["ca_s"].append(na_s)
            out["cb_p"].append(nb_p)
            out["cb_s"].append(nb_s)
        else:
            w_in = _odd_w_in(od_w_in[j])
            c_ks, c_kw = NSA_W + 2 * NSA_KW, NSA_W + 4 * NSA_KW
            u = g_odd_in_proj(hs, norm_mix[i], w_in, w_in[:, NSA_W:NSA_W + 6 * NSA_KW].T, (c_ks, c_kw))
            n_main = len(_OD_SPLITS)
            ks_b, kw_b, vs_t, vw_t = (u[n_main + k][0] for k in range(4))
            kvt_p = [u[n_main + 4 + k][0].reshape(NSA_KV, NSA_HD, bp, tp).transpose(2, 3, 0, 1) for k in range(6)]
            q_p, q_s = groups(u[0])
            kv = [groups(u[1 + k]) for k in range(6)]
            kvp = [a for a, _ in kv]
            kvs = [b for _, b in kv]
            z_p, z_s = groups(u[7])
            xbc_p, xbc_s = groups(u[8])
            sm_p, sm_s = groups(u[9])
            pe, wck, wcv = od_cmp_pe[j], od_cmp_wk[j], od_cmp_wv[j]
            mw = (od_ssm_conv_w[j], od_ssm_conv_b[j], od_dt_bias[j], od_a_log[j], od_d_skip[j], od_ssm_norm[j])
            blocks = lambda a: a.reshape(-1, L_CMP, NSA_KW)
            ncb = tp // L_CMP
            kcb_p = nsa_compress(blocks(kvp[0][:, :ncb * L_CMP]), pe, wck).reshape(bp, ncb, NSA_KW)
            vcb_p = nsa_compress(blocks(kvp[1][:, :ncb * L_CMP]), pe, wcv).reshape(bp, ncb, NSA_KW)
            o_p = nsa_prompt(q_p, sm_p[:, :, :dt_col], kcb_p, vcb_p, ks_b.reshape(bp, tp, NSA_KW), vs_t,
                             kw_b.reshape(bp, tp, NSA_KW), vw_t)
            keep = min(WINDOW, tp)
            y_p, nsc_p, nsm_p = ssd_mixer(xbc_p, z_p, sm_p, dt_col, jnp.zeros((bp, M_CONV_W - 1, M_CONV_DIM), F32),
                                          jnp.zeros((bp, M_HEADS, M_HDIM, M_DSTATE), F32), *mw, ql=128)
            tokens_last = lambda a: jnp.transpose(a, (0, 1, 3, 4, 2))
            kcp = nsa_compress_pages(tokens_last(cache_cmp_k), j, pe, wck)
            vcp = nsa_compress_pages(tokens_last(cache_cmp_v), j, pe, wcv)
            kcb_s = kcp[page_table].reshape(bs, -1, NSA_KW)
            vcb_s = vcp[page_table].reshape(bs, -1, NSA_KW)
            o_s, nwk_s, nwv_s = nsa_decode(
                q_s, sm_s[:, :, :dt_col], kcb_s, vcb_s, tokens_last(cache_sel_k), tokens_last(cache_sel_v),
                page_table, kvs[2], kvs[3], kvs[4], kvs[5], tokens_last(cache_win_k), tokens_last(cache_win_v), j)
            nwk_s = jnp.transpose(nwk_s, (0, 3, 1, 2))
            nwv_s = jnp.transpose(nwv_s, (0, 3, 1, 2))
            y_s, nsc_s, nsm_s = ssd_mixer(xbc_s, z_s, sm_s, dt_col, state_ssm_conv[j], state_ssm[j], *mw, ql=ts)
            w_out = od_w_out[j]
            hs = g_matmul_res([rows(o_p, o_s), rows(y_p, y_s)], [w_out[:NSA_W], w_out[NSA_W:]], hs)
            heads = lambda a: a.reshape(a.shape[0], a.shape[1], NSA_KV, NSA_HD)
            for k in range(4):
                rows_p[k].append(kvt_p[k])
                rows_s[k].append(heads(kvs[k]))
            out["wk_p"].append(kvt_p[4][:, tp - keep:])
            out["wv_p"].append(kvt_p[5][:, tp - keep:])
            out["wk_s"].append(nwk_s)
            out["wv_s"].append(nwv_s)
            out["sc_p"].append(nsc_p)
            out["sc_s"].append(nsc_s)
            out["sm_p"].append(nsm_p)
            out["sm_s"].append(nsm_s)
        mk, mv = norm_matmul(mem_prompt.reshape(bp * n_mem, d), None,
                             jnp.concatenate([xa_wk[i], xa_wv[i]], axis=1), norm=False,
                             splits=(XA_HEADS * XA_HD, XA_HEADS * XA_HD))
        mk = mk.reshape(bp, n_mem, XA_HEADS * XA_HD)
        mv = mv.reshape(bp, n_mem, XA_HEADS * XA_HD)
        out["mk_p"].append(mk.reshape(bp, n_mem, XA_HEADS, XA_HD))
        out["mv_p"].append(mv.reshape(bp, n_mem, XA_HEADS, XA_HD))
        qx_p, qx_s = groups(g_norm_matmul(hs, norm_xattn[i], xa_wq[i]))
        ox_p = xattn(qx_p, mk, mv)
        ox_s = xattn_cache(qx_s, cache_mem_k, cache_mem_v, i)
        hs = g_matmul_res([rows(ox_p, ox_s)], [xa_wo[i]], hs)
        hs = moe_layer(hs, norm_ffn[i], moe_wg[i], moe_bg[i], moe_we[i], moe_be[i], moe_w1, moe_w3, moe_w2, i)
    y_prompt, y_sample = groups(g_rmsnorm(hs, norm_final))
    st = lambda k: jnp.stack(out[k])
    return (y_prompt, y_sample, st("ca_p"), st("ca_s"), st("cb_p"), st("cb_s"),
            jnp.stack(rows_p[0]), jnp.stack(rows_s[0]), jnp.stack(rows_p[1]), jnp.stack(rows_s[1]),
            jnp.stack(rows_p[2]), jnp.stack(rows_s[2]), jnp.stack(rows_p[3]), jnp.stack(rows_s[3]),
            st("wk_p"), st("wk_s"), st("wv_p"), st("wv_s"), st("sm_p"), st("sm_s"), st("sc_p"), st("sc_s"),
            st("mk_p"), st("mv_p"))
```

```python
import functools

import jax
import jax.numpy as jnp
from jax import lax
from jax.experimental import pallas as pl
from jax.experimental.pallas import tpu as pltpu

F32 = jnp.float32
BF16 = jnp.bfloat16
EPS = 1e-6
NEG = -1e30
VMEM_LIMIT = 56 * 1024 * 1024

D_A = 512
D_B = 512
CONV_A_W = 3
CONV_B_W = 31
NSA_HEADS = 8
NSA_HD = 64
NSA_KV = 2
NSA_REP = NSA_HEADS // NSA_KV
NSA_W = NSA_HEADS * NSA_HD
NSA_KW = NSA_KV * NSA_HD
L_CMP = 32
L_SEL = 64
N_SEL = 16
WINDOW = 512
M_DIN = 512
M_HDIM = 64
M_HEADS = 8
M_DSTATE = 64
M_GROUPS = 2
M_CONV_W = 4
M_CONV_DIM = M_DIN + 2 * M_GROUPS * M_DSTATE
XA_HEADS = 4
XA_HD = 128
MOE_GROUPS = 4
MOE_EPG = 8
MOE_E = 32
MOE_TOPK = 2


def _params(*sem):
    return pltpu.CompilerParams(dimension_semantics=sem, vmem_limit_bytes=VMEM_LIMIT)


def _row_tile(n, pref):
    t = min(n, pref)
    while n % t or (t % 8 and t != n):
        t -= 1
    return t


def _split3(a):
    hi = a.astype(BF16)
    r1 = a - hi.astype(F32)
    mid = r1.astype(BF16)
    lo = (r1 - mid.astype(F32)).astype(BF16)
    return hi, mid, lo


def _dot_exact_rhs(a, b_bf16):
    hi, mid, lo = _split3(a)
    d = lambda x: jnp.dot(x, b_bf16, preferred_element_type=F32)
    return d(hi) + d(mid) + d(lo)


def _dot_exact_lhs(a_bf16, b):
    hi, mid, lo = _split3(b)
    d = lambda x: jnp.dot(a_bf16, x, preferred_element_type=F32)
    return d(hi) + d(mid) + d(lo)


def _rms(x, g):
    ms = jnp.mean(x * x, axis=-1, keepdims=True)
    return x * lax.rsqrt(ms + EPS) * g


def _sigmoid(x):
    return 1.0 / (1.0 + jnp.exp(-x))


def _silu(x):
    return x * _sigmoid(x)


def _norm_matmul_kernel(x_ref, g_ref, w_ref, *o_refs, norm, splits):
    x = x_ref[...]
    if norm:
        x = _rms(x, g_ref[...])
    res = jnp.dot(x.astype(BF16), w_ref[...].astype(BF16), preferred_element_type=F32)
    off = 0
    for o_ref, width in zip(o_refs, splits):
        o_ref[...] = res[:, off:off + width]
        off += width


def norm_matmul(x, g, w, *, norm=True, splits=None, tm=512):
    n, k = x.shape
    m = w.shape[1]
    tm = _row_tile(n, tm)
    if g is None:
        g = jnp.ones((k,), F32)
    widths = (m,) if splits is None else tuple(splits)
    assert sum(widths) == m
    outs = pl.pallas_call(
        functools.partial(_norm_matmul_kernel, norm=norm, splits=widths),
        grid=(n // tm,),
        in_specs=[pl.BlockSpec((tm, k), lambda i: (i, 0)),
                  pl.BlockSpec((1, k), lambda i: (0, 0)),
                  pl.BlockSpec((k, m), lambda i: (0, 0))],
        out_specs=[pl.BlockSpec((tm, wd), lambda i: (i, 0)) for wd in widths],
        out_shape=[jax.ShapeDtypeStruct((n, wd), F32) for wd in widths],
        compiler_params=_params("parallel"),
        name="norm_matmul",
    )(x, g.reshape(1, k), w)
    return outs[0] if splits is None else outs


_TM = 512


def _rowwise_call(body, row_inputs, shared, out_widths, out_dtypes, name, joint_outputs=False, transposed=()):
    ns = [a.shape[0] for a in row_inputs[0]]
    assert all(n % _TM == 0 for n in ns)
    nbs = [n // _TM for n in ns]
    starts = [sum(nbs[:g]) for g in range(len(ns))]
    n_groups, n_row, n_out = len(ns), len(row_inputs), len(out_widths)

    def group_map(g):
        return lambda i: (jnp.clip(i - starts[g], 0, nbs[g] - 1), 0)

    in_specs, args = [], []
    for k in range(n_row):
        for g in range(n_groups):
            a = row_inputs[k][g]
            in_specs.append(pl.BlockSpec((_TM, a.shape[1]), group_map(g)))
            args.append(a)
    for a in shared:
        in_specs.append(pl.BlockSpec(a.shape, lambda i, nd=a.ndim: (0,) * nd))
        args.append(a)
    if joint_outputs:
        out_specs = [pl.BlockSpec((_TM, w), lambda i: (i, 0)) for w in out_widths]
        out_shape = [jax.ShapeDtypeStruct((sum(ns), w), dt) for w, dt in zip(out_widths, out_dtypes)]
    else:
        def group_map_t(g):
            return lambda i: (0, jnp.clip(i - starts[g], 0, nbs[g] - 1))

        out_specs, out_shape = [], []
        for j, (w, dt) in enumerate(zip(out_widths, out_dtypes)):
            for g in range(n_groups):
                if j in transposed:
                    out_specs.append(pl.BlockSpec((w, _TM), group_map_t(g)))
                    out_shape.append(jax.ShapeDtypeStruct((w, ns[g]), dt))
                else:
                    out_specs.append(pl.BlockSpec((_TM, w), group_map(g)))
                    out_shape.append(jax.ShapeDtypeStruct((ns[g], w), dt))

    def kernel(*refs):
        x_refs = refs[:n_row * n_groups]
        s_refs = refs[n_row * n_groups:n_row * n_groups + len(shared)]
        o_refs = refs[n_row * n_groups + len(shared):]
        i = pl.program_id(0)
        for g in range(n_groups):
            @pl.when((i >= starts[g]) & (i < starts[g] + nbs[g]))
            def _(g=g):
                vals = body([x_refs[k * n_groups + g][...] for k in range(n_row)], s_refs)
                for j, v in enumerate(vals):
                    o_ref = o_refs[j] if joint_outputs else o_refs[j * n_groups + g]
                    o_ref[...] = v.astype(o_ref.dtype)

    outs = pl.pallas_call(
        kernel, grid=(sum(nbs),), in_specs=in_specs, out_specs=out_specs, out_shape=out_shape,
        compiler_params=_params("arbitrary"), name=name)(*args)
    if joint_outputs:
        return list(outs)
    return [list(outs[j * n_groups:(j + 1) * n_groups]) for j in range(n_out)]


def g_norm_matmul(hs, g, w, *, splits=None, norm=True):
    widths = (w.shape[1],) if splits is None else tuple(splits)
    assert sum(widths) == w.shape[1]

    def body(xs, s_refs):
        x = _rms(xs[0], s_refs[0][...]) if norm else xs[0]
        res = jnp.dot(x.astype(BF16), s_refs[1][...].astype(BF16), preferred_element_type=F32)
        offs = [sum(widths[:j]) for j in range(len(widths))]
        return [res[:, o:o + wd] for o, wd in zip(offs, widths)]

    k = hs[0].shape[1]
    gv = jnp.ones((1, k), F32) if g is None else g.reshape(1, k)
    outs = _rowwise_call(body, [hs], [gv, w], widths, [F32] * len(widths), "norm_matmul")
    return outs[0] if splits is None else outs


def g_odd_in_proj(hs, g, w, w_kvt, k_cols):
    widths = _OD_SPLITS + (NSA_KW,) * 4 + (NSA_KW,) * 6
    offs = [sum(_OD_SPLITS[:j]) for j in range(len(_OD_SPLITS))]

    def body(xs, s_refs):
        x = _rms(xs[0], s_refs[0][...]).astype(BF16)
        res = jnp.dot(x, s_refs[1][...].astype(BF16), preferred_element_type=F32)
        kvt = lax.dot_general(s_refs[2][...].astype(BF16), x, _NT, preferred_element_type=F32)
        part = lambda k: kvt[k * NSA_KW:(k + 1) * NSA_KW]
        outs = [res[:, o:o + wd] for o, wd in zip(offs, _OD_SPLITS)]
        outs += [res[:, c:c + NSA_KW] for c in k_cols]
        outs += [part(3), part(5)]
        outs += [part(k) for k in range(6)]
        return outs

    n_main = len(_OD_SPLITS)
    dts = [F32] * n_main + [BF16] * 4 + [F32] * 6
    return _rowwise_call(body, [hs], [g.reshape(1, -1), w, w_kvt], widths, dts, "odd_in_proj",
                         transposed=(n_main + 2, n_main + 3) + tuple(range(n_main + 4, n_main + 10)))


def g_matmul_res(xs_list, ws, hs):
    def body(xs, s_refs):
        acc = xs[-1]
        for j in range(len(ws)):
            acc = acc + jnp.dot(xs[j].astype(BF16), s_refs[j][...].astype(BF16), preferred_element_type=F32)
        return [acc]

    return _rowwise_call(body, list(xs_list) + [hs], list(ws), (hs[0].shape[1],), [F32], "matmul_res")[0]


def g_rmsnorm(hs, g):
    body = lambda xs, s_refs: [_rms(xs[0], s_refs[0][...])]
    return _rowwise_call(body, [hs], [g.reshape(1, -1)], (hs[0].shape[1],), [F32], "rmsnorm_rows")[0]


def g_moe_router(hs, g, w_router):
    def body(xs, s_refs):
        xb = _rms(xs[0], s_refs[0][...]).astype(BF16)
        logits = jnp.dot(xb, s_refs[1][...].astype(BF16), preferred_element_type=F32)
        half = xb.shape[1] // 2
        bits = lambda v: lax.bitcast_convert_type(v.astype(F32), jnp.uint32)
        words = (bits(xb[:, half:]) & jnp.uint32(0xFFFF0000)) | (bits(xb[:, :half]) >> 16)
        return [logits, lax.bitcast_convert_type(words, F32)]

    k = hs[0].shape[1]
    return _rowwise_call(body, [hs], [g.reshape(1, k), w_router], (_ROUTER_W, k // 2), [F32, F32], "moe_router",
                         joint_outputs=True)


_HALO_A = 8
_HALO_B = 32


def _even_conv_kernel(u_ref, sa_ref, sb_ref, wa_ref, wb_ref, bb_ref, lg_ref, lb_ref,
                      y_ref, na_ref, nb_ref, ea_ref, eb_ref, ear_ref, ebr_ref, sh_ref, *, tt):
    rnd = lambda x: x.astype(BF16).astype(F32)
    t = pl.program_id(1)
    nt = pl.num_programs(1)

    @pl.when(t == 0)
    def _():
        ea_ref[...] = jnp.zeros_like(ea_ref)
        eb_ref[...] = jnp.zeros_like(eb_ref)
        ea_ref[_HALO_A - (CONV_A_W - 1):_HALO_A, :] = sa_ref[0]
        eb_ref[_HALO_B - (CONV_B_W - 1):_HALO_B, :] = sb_ref[0]
        ear_ref[...] = rnd(ea_ref[...])
        ebr_ref[...] = rnd(eb_ref[...])

    @pl.when(t > 0)
    def _():
        ea_ref[0:_HALO_A, :] = ea_ref[tt:tt + _HALO_A, :]
        eb_ref[0:_HALO_B, :] = eb_ref[tt:tt + _HALO_B, :]
        ear_ref[0:_HALO_A, :] = ear_ref[tt:tt + _HALO_A, :]
        ebr_ref[0:_HALO_B, :] = ebr_ref[tt:tt + _HALO_B, :]

    xa = u_ref[0, :, 0:D_A]
    ba = u_ref[0, :, D_A:2 * D_A]
    ca = u_ref[0, :, 2 * D_A:3 * D_A]
    pb = u_ref[0, :, 3 * D_A:3 * D_A + D_B]
    gb = u_ref[0, :, 3 * D_A + D_B:3 * D_A + 2 * D_B]
    va = ca * xa
    vb = pb * _sigmoid(gb)
    ea_ref[_HALO_A:_HALO_A + tt, :] = va
    eb_ref[_HALO_B:_HALO_B + tt, :] = vb
    ear_ref[_HALO_A:_HALO_A + tt, :] = rnd(va)
    ebr_ref[_HALO_B:_HALO_B + tt, :] = rnd(vb)

    acc = jnp.zeros((tt, D_A), F32)
    for k in range(CONV_A_W):
        off = _HALO_A - (CONV_A_W - 1) + k
        acc = acc + ear_ref[off:off + tt, :] * wa_ref[k:k + 1, :]
    y_ref[0, :, 0:D_A] = ba * acc

    span = tt + _HALO_B - 8
    for r in range(1, 8):
        sh_ref[r - 1, 0:span, :] = ebr_ref[r:r + span, :]
    acc = jnp.zeros((tt, D_B), F32)
    for k in range(CONV_B_W):
        off = _HALO_B - (CONV_B_W - 1) + k
        base = off - off % 8
        rows = ebr_ref[base:base + tt, :] if off % 8 == 0 else sh_ref[off % 8 - 1, base:base + tt, :]
        acc = acc + rows * wb_ref[k:k + 1, :]
    acc = acc + bb_ref[...]
    mu = jnp.mean(acc, axis=-1, keepdims=True)
    xc = acc - mu
    var = jnp.mean(xc * xc, axis=-1, keepdims=True)
    yb = xc * lax.rsqrt(var + EPS) * lg_ref[...] + lb_ref[...]
    y_ref[0, :, D_A:D_A + D_B] = _silu(yb)

    @pl.when(t == nt - 1)
    def _():
        na_ref[0] = ea_ref[_HALO_A + tt - (CONV_A_W - 1):_HALO_A + tt, :]
        nb_ref[0] = eb_ref[_HALO_B + tt - (CONV_B_W - 1):_HALO_B + tt, :]


def even_conv(u, sa, sb, wa, wb, bb, lg, lb, *, tt=256):
    b, t, w = u.shape
    tt = _row_tile(t, tt)
    full = lambda shape: pl.BlockSpec(shape, lambda i, j: (0,) * len(shape))
    return pl.pallas_call(
        functools.partial(_even_conv_kernel, tt=tt),
        grid=(b, t // tt),
        in_specs=[pl.BlockSpec((1, tt, w), lambda i, j: (i, j, 0)),
                  pl.BlockSpec((1, CONV_A_W - 1, D_A), lambda i, j: (i, 0, 0)),
                  pl.BlockSpec((1, CONV_B_W - 1, D_B), lambda i, j: (i, 0, 0)),
                  full((CONV_A_W, D_A)), full((CONV_B_W, D_B)), full((1, D_B)),
                  full((1, D_B)), full((1, D_B))],
        out_specs=[pl.BlockSpec((1, tt, D_A + D_B), lambda i, j: (i, j, 0)),
                   pl.BlockSpec((1, CONV_A_W - 1, D_A), lambda i, j: (i, 0, 0)),
                   pl.BlockSpec((1, CONV_B_W - 1, D_B), lambda i, j: (i, 0, 0))],
        out_shape=[jax.ShapeDtypeStruct((b, t, D_A + D_B), F32),
                   jax.ShapeDtypeStruct((b, CONV_A_W - 1, D_A), F32),
                   jax.ShapeDtypeStruct((b, CONV_B_W - 1, D_B), F32)],
        scratch_shapes=[pltpu.VMEM((_HALO_A + tt, D_A), F32), pltpu.VMEM((_HALO_B + tt, D_B), F32),
                        pltpu.VMEM((_HALO_A + tt, D_A), F32), pltpu.VMEM((_HALO_B + tt, D_B), F32),
                        pltpu.VMEM((7, _HALO_B + tt - 8, D_B), F32)],
        compiler_params=_params("parallel", "arbitrary"),
        name="even_conv",
    )(u, sa, sb, wa, wb, bb.reshape(1, D_B), lg.reshape(1, D_B), lb.reshape(1, D_B))


def _xattn_kernel(q_ref, k_ref, v_ref, o_ref):
    scale = XA_HD ** -0.5
    for h in range(XA_HEADS):
        sl = slice(h * XA_HD, (h + 1) * XA_HD)
        q = q_ref[0, :, sl].astype(BF16)
        k = k_ref[0, :, sl].astype(BF16)
        v = v_ref[0, :, sl].astype(BF16)
        s = lax.dot_general(q, k, (((1,), (1,)), ((), ())), preferred_element_type=F32) * scale
        m = jnp.max(s, axis=-1, keepdims=True)
        p = jnp.exp(s - m)
        p = p / jnp.sum(p, axis=-1, keepdims=True)
        o_ref[0, :, sl] = jnp.dot(p.astype(BF16), v, preferred_element_type=F32)


def xattn(q, k, v, *, tq=512):
    b, t, w = q.shape
    tq = _row_tile(t, tq)
    m = k.shape[1]
    kv_spec = pl.BlockSpec((1, m, w), lambda i, j: (i, 0, 0))
    return pl.pallas_call(
        _xattn_kernel,
        grid=(b, t // tq),
        in_specs=[pl.BlockSpec((1, tq, w), lambda i, j: (i, j, 0)), kv_spec, kv_spec],
        out_specs=pl.BlockSpec((1, tq, w), lambda i, j: (i, j, 0)),
        out_shape=jax.ShapeDtypeStruct((b, t, w), F32),
        compiler_params=_params("parallel", "parallel"),
        name="xattn",
    )(q, k, v)


MOE_BLK = 256
_ROUTER_W = 128


def _expert_gather_kernel(tok_ref, be_ref, act_ref, xn_hbm, gate_ref, w1_ref, w3_ref, w2_ref, o_ref,
                          xbuf_ref, sem_ref, w1b_ref, w3b_ref, w2b_ref):
    i = pl.program_id(0)
    nb = pl.num_programs(0)

    def row_copy(blk, slot, r):
        tok = tok_ref[blk * MOE_BLK + r]
        return pltpu.make_async_copy(xn_hbm.at[pl.ds(tok, 1)], xbuf_ref.at[slot, pl.ds(r, 1)], sem_ref.at[slot])

    def start_gather(blk, slot):
        if isinstance(slot, int):
            for r in range(MOE_BLK):
                row_copy(blk, slot, r).start()
            return
        for s in range(2):
            @pl.when(slot == s)
            def _(s=s):
                for r in range(MOE_BLK):
                    row_copy(blk, s, r).start()

    def wait_gather(slot):
        pltpu.make_async_copy(xbuf_ref.at[slot], xbuf_ref.at[slot], sem_ref.at[slot]).wait()

    slot = lax.rem(i, 2)
    nxt = jnp.minimum(i + 1, nb - 1)

    @pl.when((i == 0) & (act_ref[0] > 0))
    def _():
        start_gather(0, 0)

    @pl.when((i + 1 < nb) & (act_ref[nxt] > 0))
    def _():
        start_gather(nxt, 1 - slot)

    prev = be_ref[jnp.maximum(i - 1, 0)]

    @pl.when((act_ref[i] > 0) & ((i == 0) | (be_ref[i] != prev)))
    def _():
        w1b_ref[...] = w1_ref[0, 0].astype(BF16)
        w3b_ref[...] = w3_ref[0, 0].astype(BF16)
        w2b_ref[...] = w2_ref[0, 0].astype(BF16)

    @pl.when(act_ref[i] > 0)
    def _():
        wait_gather(slot)
        words = lax.bitcast_convert_type(xbuf_ref[slot], jnp.uint32)
        unpack = lambda v: lax.bitcast_convert_type(v, F32).astype(BF16)
        x = jnp.concatenate([unpack(words << 16), unpack(words & jnp.uint32(0xFFFF0000))], axis=1)
        h1 = jnp.dot(x, w1b_ref[...], preferred_element_type=F32)
        h3 = jnp.dot(x, w3b_ref[...], preferred_element_type=F32)
        hid = (_silu(h1) * h3).astype(BF16)
        out = jnp.dot(hid, w2b_ref[...], preferred_element_type=F32)
        o_ref[...] = out * gate_ref[...]

    @pl.when(act_ref[i] == 0)
    def _():
        o_ref[...] = jnp.zeros_like(o_ref)


def moe_experts_gather(xn, buf_tok, gate, blk_exp, blk_act, w1, w3, w2, layer):
    rows = buf_tok.shape[0]
    d = w1.shape[2]
    nb = rows // MOE_BLK
    ff = w1.shape[3]
    return pl.pallas_call(
        _expert_gather_kernel,
        grid_spec=pltpu.PrefetchScalarGridSpec(
            num_scalar_prefetch=3,
            grid=(nb,),
            in_specs=[pl.BlockSpec(memory_space=pl.ANY),
                      pl.BlockSpec((MOE_BLK, 1), lambda i, tok, be, act: (i, 0)),
                      pl.BlockSpec((1, 1, d, ff), lambda i, tok, be, act: (layer, be[i], 0, 0)),
                      pl.BlockSpec((1, 1, d, ff), lambda i, tok, be, act: (layer, be[i], 0, 0)),
                      pl.BlockSpec((1, 1, ff, d), lambda i, tok, be, act: (layer, be[i], 0, 0))],
            out_specs=pl.BlockSpec((MOE_BLK, d), lambda i, tok, be, act: (i, 0)),
            scratch_shapes=[pltpu.VMEM((2, MOE_BLK, d // 2), F32), pltpu.SemaphoreType.DMA((2,)),
                            pltpu.VMEM((d, ff), BF16), pltpu.VMEM((d, ff), BF16), pltpu.VMEM((ff, d), BF16)],
        ),
        out_shape=jax.ShapeDtypeStruct((rows, d), F32),
        compiler_params=_params("arbitrary"),
        name="moe_experts",
    )(buf_tok, blk_exp, blk_act, xn, gate, w1, w3, w2)


def moe_layer(hs, g, wg, bg, we, be, w1, w3, w2, layer):
    d = hs[0].shape[1]
    n = sum(h.shape[0] for h in hs)
    w_router = jnp.concatenate([wg, we, jnp.zeros((d, _ROUTER_W - MOE_GROUPS - MOE_E), F32)], axis=1)
    logits, xn = g_moe_router(hs, g, w_router)
    lg = logits[:, :MOE_GROUPS] + bg
    grp = jnp.argmax(lg, axis=-1)
    gw = jnp.take_along_axis(jax.nn.softmax(lg, axis=-1), grp[:, None], axis=1)
    le = (logits[:, MOE_GROUPS:MOE_GROUPS + MOE_E] + be).reshape(n, MOE_GROUPS, MOE_EPG)
    le = jnp.take_along_axis(le, grp[:, None, None], axis=1)[:, 0]
    tv, ti = lax.top_k(jax.nn.softmax(le, axis=-1), MOE_TOPK)
    wts = gw * tv / jnp.sum(tv, axis=-1, keepdims=True)
    eid = (grp[:, None] * MOE_EPG + ti).reshape(-1).astype(jnp.int32)
    npair = n * MOE_TOPK
    experts = jnp.arange(MOE_E, dtype=jnp.int32)
    order = jnp.argsort(eid).astype(jnp.int32)
    rank = jnp.argsort(order).astype(jnp.int32)
    counts = jnp.sum(eid[:, None] == experts[None, :], axis=0).astype(jnp.int32)
    start = jnp.cumsum(counts) - counts
    padded = (counts + MOE_BLK - 1) // MOE_BLK * MOE_BLK
    pend = jnp.cumsum(padded)
    shift = pend - padded - start
    nb = -(-npair // MOE_BLK) + MOE_E
    blk_lo = jnp.arange(nb, dtype=jnp.int32) * MOE_BLK
    blk_exp = jnp.minimum(jnp.sum(pend[None, :] <= blk_lo[:, None], axis=1), MOE_E - 1).astype(jnp.int32)
    blk_act = (blk_lo < pend[-1]).astype(jnp.int32)
    src = (blk_lo - shift[blk_exp])[:, None] + jnp.arange(MOE_BLK, dtype=jnp.int32)[None, :]
    live = src < (start + counts)[blk_exp][:, None]
    pair = order[jnp.where(live, src, 0).reshape(-1)]
    buf_tok = pair // MOE_TOPK
    buf_gate = jnp.where(live.reshape(-1), wts.reshape(-1)[pair], 0.0)
    dest = (rank + shift[eid]).reshape(n, MOE_TOPK)
    out = moe_experts_gather(xn, buf_tok.astype(jnp.int32), buf_gate[:, None], blk_exp, blk_act, w1, w3, w2, layer)
    y = out[dest[:, 0]] + out[dest[:, 1]]
    starts = [sum(h.shape[0] for h in hs[:k]) for k in range(len(hs))]
    return [h + y[s:s + h.shape[0]] for h, s in zip(hs, starts)]


_KT = 128
_NT = (((1,), (1,)), ((), ()))
_BIG = 3e38
_M0 = -1e29


def _compress_kernel(x_ref, pe_ref, w_ref, o_ref):
    acc = jnp.zeros(o_ref.shape, F32)
    for l in range(L_CMP):
        y = x_ref[:, l, :] + pe_ref[l:l + 1, :]
        acc = acc + jnp.dot(y.astype(BF16), w_ref[l], preferred_element_type=F32)
    o_ref[...] = acc


def nsa_compress(x, pe, w, *, tb=256):
    nb = x.shape[0]
    tb = _row_tile(nb, tb)
    pe2 = jnp.concatenate([pe] * NSA_KV, axis=1)
    z = jnp.zeros_like(w)
    w2 = jnp.concatenate([jnp.concatenate([w, z], axis=2), jnp.concatenate([z, w], axis=2)], axis=1).astype(BF16)
    return pl.pallas_call(
        _compress_kernel,
        grid=(nb // tb,),
        in_specs=[pl.BlockSpec((tb, L_CMP, NSA_KW), lambda i: (i, 0, 0)),
                  pl.BlockSpec((L_CMP, NSA_KW), lambda i: (0, 0)),
                  pl.BlockSpec((L_CMP, NSA_KW, NSA_KW), lambda i: (0, 0, 0))],
        out_specs=pl.BlockSpec((tb, NSA_KW), lambda i: (i, 0)),
        out_shape=jax.ShapeDtypeStruct((nb, NSA_KW), F32),
        compiler_params=_params("parallel"),
        name="nsa_compress",
    )(x, pe2, w2)


def _qz(q, g, nq):
    lane = lax.broadcasted_iota(jnp.int32, (nq, NSA_KW), 1)
    keep = (lane >> 6) == g
    parts = []
    for r in range(NSA_REP):
        h = g * NSA_REP + r
        slab = q[:, (h // 2) * NSA_KW:(h // 2 + 1) * NSA_KW]
        if h % 2 != g:
            slab = pltpu.roll(slab, NSA_HD, axis=1)
        parts.append(jnp.where(keep, slab, 0.0))
    return jnp.concatenate(parts, axis=0).astype(BF16)


def _softmax_done(l_ref, acc_ref, g):
    l = l_ref[g]
    return acc_ref[g] * jnp.where(l > 0.0, 1.0 / l, 0.0)


def _compressed_branch(qz, kcb, vcbt_g, qpos, slope, ncb):
    st = lax.dot_general(kcb, qz, _NT, preferred_element_type=F32)
    row = lax.broadcasted_iota(jnp.int32, (ncb, 1), 0)
    half = ncb // 2
    blk = jnp.where(row < half, 2 * row, 2 * (row - half) + 1)
    c_pos = blk * L_CMP + (L_CMP - 1)
    d_c = qpos - c_pos
    maskf = jnp.where(d_c >= 0, 1.0, 0.0)
    s = st - slope * d_c.astype(F32)
    sm = jnp.where(d_c >= 0, s, NEG)
    m = jnp.max(sm, axis=0, keepdims=True)
    p = jnp.exp(sm - m) * maskf
    l = jnp.sum(p, axis=0, keepdims=True)
    p = p * jnp.where(l > 0.0, 1.0 / l, 0.0)
    o = jnp.dot(vcbt_g, p.astype(BF16), preferred_element_type=F32)
    return o, p


def _select_blocks(imp_sel, qpos, nsp):
    cols = imp_sel.shape[1]
    blk = lax.broadcasted_iota(jnp.int32, (nsp, cols), 0)
    cur = qpos >> 6
    valid = blk <= cur
    forced = jnp.where(valid, jnp.where(blk == 0, 1.0, jnp.where(blk >= cur - 1, 1.0, 0.0)), 0.0)
    score = jnp.where(forced > 0.0, _BIG, jnp.where(valid, imp_sel, -1.0))
    sel = jnp.zeros((nsp, cols), F32)
    for _ in range(N_SEL):
        m = jnp.max(score, axis=0, keepdims=True)
        idx = jnp.min(jnp.where(score == m, blk, nsp + 1), axis=0, keepdims=True)
        pick = blk == idx
        sel = jnp.where(pick, 1.0, sel)
        score = jnp.where(pick, -2.0, score)
    return jnp.where(valid, sel, 0.0)


_TS = 256


def _pos_lanes(nkeys, tile_off):
    lane = lax.broadcasted_iota(jnp.int32, (nkeys, NSA_KW), 1)
    key = lax.broadcasted_iota(jnp.int32, (nkeys, NSA_KW), 0)
    hi = (tile_off + (key >> 7)).astype(F32)
    lo = (key & (_KT - 1)).astype(F32)
    return jnp.where(lane == 0, hi, jnp.where(lane == 1, lo, 0.0)).astype(BF16)


def _nsa_prompt_kernel(q_ref, glog_ref, kcb_ref, vcbt_ref, ks_ref, vst_ref, kw_ref, vwt_ref,
                        qoff_ref, slope_ref, qaux_ref, o_ref, sel_ref, m_ref, l_ref, acc_ref, *, nq, ncb):
    i = pl.program_id(1)
    st0 = i * nq
    qoff = qoff_ref[...]
    qpos = st0 + qoff
    q = q_ref[0] * (NSA_HD ** -0.5)
    nsp = ncb // 2
    bpt = _TS // L_SEL
    idiag = st0 // _TS
    groups = range(NSA_KV)
    rows = [slice(g * NSA_HD, (g + 1) * NSA_HD) for g in groups]

    qxs, o_cs, firsts = [], [], []
    for g in groups:
        qz = _qz(q, g, nq)
        qxs.append(jnp.concatenate([qz, qaux_ref[g]], axis=1))
        o_c, p_c = _compressed_branch(qz, kcb_ref[0], vcbt_ref[0, rows[g], :], qpos, slope_ref[g], ncb)
        o_cs.append(o_c)
        imp = p_c[:, 0:nq]
        for r in range(1, NSA_REP):
            imp = imp + p_c[:, r * nq:(r + 1) * nq]
        sel = _select_blocks(imp[0:nsp] + imp[nsp:ncb], qpos[:, 0:nq], nsp)
        sel_ref[g] = jnp.concatenate([jnp.where(sel > 0.0, 0.0, NEG)] * NSA_REP, axis=1)
        blk = lax.broadcasted_iota(jnp.int32, (nsp, 1), 0)
        row_any = jnp.max(sel, axis=1, keepdims=True)
        first = jnp.min(jnp.where(row_any > 0.0, jnp.where(blk >= bpt, blk, nsp * bpt), nsp * bpt),
                        axis=0, keepdims=True)
        firsts.append(first[0, 0])
    lo = jnp.minimum(jnp.minimum(firsts[0], firsts[1]) // bpt, idiag)

    m_ref[...] = jnp.full(m_ref.shape, _M0, F32)
    l_ref[...] = jnp.zeros(l_ref.shape, F32)
    acc_ref[...] = jnp.zeros(acc_ref.shape, F32)

    def sel_tile(kt, diag):
        off = kt * _TS if isinstance(kt, int) else pl.multiple_of(kt * _TS, _TS)
        kx = jnp.concatenate([ks_ref[0, pl.ds(off, _TS), :], _pos_lanes(_TS, kt * (_TS // _KT) - i)], axis=1)
        if diag:
            visible = (off + lax.broadcasted_iota(jnp.int32, (_TS, 1), 0)) <= qpos
        for g in groups:
            s = lax.dot_general(kx, qxs[g], _NT, preferred_element_type=F32)
            s = jnp.concatenate([s[b * L_SEL:(b + 1) * L_SEL] + sel_ref[g, pl.ds(bpt * kt + b, 1), :]
                                 for b in range(bpt)], axis=0)
            if diag:
                s = jnp.where(visible, s, NEG)
            m_old = m_ref[g]
            m_new = jnp.maximum(m_old, jnp.max(s, axis=0, keepdims=True))
            alpha = jnp.exp(m_old - m_new)
            p = jnp.exp(s - m_new)
            l_ref[g] = alpha * l_ref[g] + jnp.sum(p, axis=0, keepdims=True)
            vt = vst_ref[rows[g], pl.ds(off, _TS)]
            acc_ref[g] = alpha * acc_ref[g] + jnp.dot(vt, p.astype(BF16), preferred_element_type=F32)
            m_ref[g] = m_new

    @pl.when(idiag > 0)
    def _():
        sel_tile(0, False)

    def sel_body(kt, carry):
        sel_tile(kt, False)
        return carry

    lax.fori_loop(lo, idiag, sel_body, 0)
    sel_tile(idiag, True)

    wk = WINDOW + nq
    wstart = pl.multiple_of(jnp.maximum(i - WINDOW // nq, 0) * nq, nq)
    kxw = jnp.concatenate([kw_ref[0, pl.ds(wstart, wk), :], _pos_lanes(wk, wstart // _KT - i)], axis=1)
    d_w = qpos - (wstart + lax.broadcasted_iota(jnp.int32, (wk, 1), 0))
    in_window = jnp.where(d_w >= 0, d_w, WINDOW + 1) <= WINDOW
    for g in groups:
        s = jnp.where(in_window, lax.dot_general(kxw, qxs[g], _NT, preferred_element_type=F32), NEG)
        m = jnp.max(s, axis=0, keepdims=True)
        p = jnp.exp(s - m)
        l = jnp.sum(p, axis=0, keepdims=True)
        o_w = jnp.dot(vwt_ref[rows[g], pl.ds(wstart, wk)], p.astype(BF16), preferred_element_type=F32) / l
        o_s = _softmax_done(l_ref, acc_ref, g)
        gc = _sigmoid(glog_ref[0, 0, g:g + 1, :])
        gs = _sigmoid(glog_ref[0, 0, 2 + g:3 + g, :])
        gw = _sigmoid(glog_ref[0, 0, 4 + g:5 + g, :])
        o_t = gc * o_cs[g] + gs * o_s + gw * o_w
        for pp in range(NSA_REP // 2):
            pair = jnp.concatenate([o_t[:, (2 * pp) * nq:(2 * pp + 1) * nq],
                                    o_t[:, (2 * pp + 1) * nq:(2 * pp + 2) * nq]], axis=0)
            slab = g * (NSA_REP // 2) + pp
            o_ref[0, :, slab * NSA_KW:(slab + 1) * NSA_KW] = pair.T


def _nsa_cols(nq):
    c = NSA_REP * nq
    qoff = (jnp.arange(c, dtype=jnp.int32) % nq).reshape(1, c)
    slopes = 2.0 ** (-8.0 * jnp.arange(1, NSA_HEADS + 1, dtype=F32) / NSA_HEADS)
    slope = jnp.repeat(slopes.reshape(NSA_KV, NSA_REP), nq, axis=1).reshape(NSA_KV, 1, c)
    return qoff, slope


def _gate_cols(glog, nq):
    b, t, _ = glog.shape
    x = glog.reshape(b, t // nq, nq, NSA_KV, NSA_REP, 3)
    return x.transpose(0, 1, 5, 3, 4, 2).reshape(b, t // nq, 3 * NSA_KV, NSA_REP * nq)


def _even_odd(x):
    return jnp.concatenate([x[:, 0::2], x[:, 1::2]], axis=1)


def nsa_prompt(q, glog, kcb, vcb, ks, vs_t, kw, vw_t, *, nq=128):
    b, t, _ = q.shape
    ncb = kcb.shape[1]
    c = NSA_REP * nq
    nblk = t // nq
    assert nq == _KT and t % _TS == 0 and t >= WINDOW + nq and ncb == t // L_CMP
    qoff, slope = _nsa_cols(nq)
    lane = jnp.arange(NSA_KW)[None, None, :]
    slope_col = slope.reshape(NSA_KV, c, 1)
    qaux = jnp.where(lane == 0, slope_col * _KT, jnp.where(lane == 1, slope_col, 0.0)).astype(BF16)
    kcb_p = _even_odd(kcb).astype(BF16)
    vcbt = _even_odd(vcb).transpose(0, 2, 1).astype(BF16)
    per_b = lambda shape: pl.BlockSpec((1,) + shape, lambda i, j: (i, 0, 0))
    out = pl.pallas_call(
        functools.partial(_nsa_prompt_kernel, nq=nq, ncb=ncb),
        grid=(b, nblk),
        in_specs=[pl.BlockSpec((1, nq, NSA_W), lambda i, j: (i, j, 0)),
                  pl.BlockSpec((1, 1, 3 * NSA_KV, c), lambda i, j: (i, j, 0, 0)),
                  per_b((ncb, NSA_KW)), per_b((NSA_KW, ncb)),
                  per_b((t, NSA_KW)), pl.BlockSpec((NSA_KW, t), lambda i, j: (0, i)),
                  per_b((t, NSA_KW)), pl.BlockSpec((NSA_KW, t), lambda i, j: (0, i)),
                  pl.BlockSpec((1, c), lambda i, j: (0, 0)),
                  pl.BlockSpec((NSA_KV, 1, c), lambda i, j: (0, 0, 0)),
                  pl.BlockSpec((NSA_KV, c, NSA_KW), lambda i, j: (0, 0, 0))],
        out_specs=pl.BlockSpec((1, nq, NSA_W), lambda i, j: (i, j, 0)),
        out_shape=jax.ShapeDtypeStruct((b, t, NSA_W), F32),
        scratch_shapes=[pltpu.VMEM((NSA_KV, ncb // 2, c), F32),
                        pltpu.VMEM((NSA_KV, 1, c), F32), pltpu.VMEM((NSA_KV, 1, c), F32),
                        pltpu.VMEM((NSA_KV, NSA_HD, c), F32)],
        compiler_params=_params("parallel", "arbitrary"),
        name="nsa_prompt",
    )(q, _gate_cols(glog, nq), kcb_p, vcbt, ks, vs_t, kw, vw_t, qoff, slope, qaux)
    return out


def _compress_pages_kernel(x_ref, pe_ref, w_ref, o_ref):
    for g in range(NSA_KV):
        acc = jnp.zeros(o_ref.shape[1:], F32)
        for d in range(0, NSA_HD, 2):
            y = jnp.concatenate([x_ref[0, :, g, d, :] + pe_ref[d:d + 1, :],
                                 x_ref[0, :, g, d + 1, :] + pe_ref[d + 1:d + 2, :]], axis=1)
            acc = acc + jnp.dot(y.astype(BF16), w_ref[d // 2], preferred_element_type=F32)
        o_ref[g] = acc


def nsa_compress_pages(cache_t, layer, pe, w, *, tp=256):
    n_pool, page = cache_t.shape[1], cache_t.shape[4]
    nblk = page // L_CMP
    tp = _row_tile(n_pool, tp)
    pe_t = jnp.tile(pe.T, (1, nblk))
    eye = jnp.eye(nblk, dtype=F32)
    wd = jnp.einsum('nm,lde->dnlme', eye, w).reshape(NSA_HD // 2, 2 * page, nblk * NSA_HD).astype(BF16)
    out = pl.pallas_call(
        _compress_pages_kernel,
        grid=(n_pool // tp,),
        in_specs=[pl.BlockSpec((1, tp, NSA_KV, NSA_HD, page), lambda i: (layer, i, 0, 0, 0)),
                  pl.BlockSpec((NSA_HD, page), lambda i: (0, 0)),
                  pl.BlockSpec((NSA_HD // 2, 2 * page, nblk * NSA_HD), lambda i: (0, 0, 0))],
        out_specs=pl.BlockSpec((NSA_KV, tp, nblk * NSA_HD), lambda i: (0, i, 0)),
        out_shape=jax.ShapeDtypeStruct((NSA_KV, n_pool, nblk * NSA_HD), F32),
        compiler_params=_params("parallel"),
        name="nsa_compress_pages",
    )(cache_t, pe_t, wd)
    return out.reshape(NSA_KV, n_pool, nblk, NSA_HD).transpose(1, 2, 0, 3).reshape(n_pool, nblk, NSA_KW)


def _row_softmax(s, mask):
    sm = jnp.where(mask, s, NEG)
    m = jnp.max(sm, axis=1, keepdims=True)
    p = jnp.where(mask, jnp.exp(sm - m), 0.0)
    l = jnp.sum(p, axis=1, keepdims=True)
    return p * jnp.where(l > 0.0, 1.0 / l, 0.0)


_DECODE_BB = 4


def _nsa_decode_kernel(pt_ref, q_ref, gl_ref, kcb_ref, vcb_ref, *refs, nq, ncb, npages, page, past, wb, nbb):
    ks_pages = refs[0:nbb * npages]
    vs_pages = refs[nbb * npages:2 * nbb * npages]
    (nks_ref, nvs_ref, nkw_ref, nvw_ref, wk_ref, wv_ref, slope_ref, expand_ref,
     o_ref, wko_ref, wvo_ref) = refs[2 * nbb * npages:]
    del pt_ref
    c = NSA_REP * nq
    nblk_lanes = _KT
    n_sel = -(-(past + nq) // L_SEL)
    row = lax.broadcasted_iota(jnp.int32, (c, 1), 0)
    qpos = past + (row & (nq - 1))
    pad_rows = lambda x: jnp.concatenate([x, jnp.zeros((_KT - nq, x.shape[1]), F32)], axis=0)
    lane = lax.broadcasted_iota(jnp.int32, (1, _KT), 1)
    n_sel_keys = (npages + 1) * page
    key_all = lax.broadcasted_iota(jnp.int32, (1, n_sel_keys), 1)
    expand = expand_ref[...]
    pr = lax.broadcasted_iota(jnp.int32, (ncb, nblk_lanes), 0)
    pc = lax.broadcasted_iota(jnp.int32, (ncb, nblk_lanes), 1)
    half = ncb // 2
    pair = jnp.where(jnp.where(pr < half, pr, pr - half) == pc, 1.0, 0.0).astype(BF16)
    cl = lax.broadcasted_iota(jnp.int32, (1, ncb), 1)
    c_pos = jnp.where(cl < half, 2 * cl, 2 * (cl - half) + 1) * L_CMP + (L_CMP - 1)
    d_c = qpos - c_pos
    d_cf = d_c.astype(F32)
    d_s = qpos - key_all
    d_sf = d_s.astype(F32)
    d_w = qpos - (past - wb + lax.broadcasted_iota(jnp.int32, (1, wb + _KT), 1))
    d_wf = d_w.astype(F32)
    in_window = jnp.where(d_w >= 0, d_w, WINDOW + 1) <= WINDOW
    cur = (past + lax.broadcasted_iota(jnp.int32, (nq, 1), 0)) >> 6
    valid = lane <= cur
    forced = jnp.where(valid, jnp.where(lane == 0, 1.0, jnp.where(lane >= cur - 1, 1.0, 0.0)), 0.0)

    for bb in range(nbb):
        q = q_ref[bb] * (NSA_HD ** -0.5)
        new_kt = pad_rows(nkw_ref[bb]).T
        new_vt = pad_rows(nvw_ref[bb]).T
        for g in range(NSA_KV):
            gs = slice(g * NSA_HD, (g + 1) * NSA_HD)
            qg = jnp.concatenate([q[:, (g * NSA_REP + r) * NSA_HD:(g * NSA_REP + r + 1) * NSA_HD]
                                  for r in range(NSA_REP)], axis=0).astype(BF16)
            slope = slope_ref[g]
            s_c = lax.dot_general(qg, kcb_ref[bb, :, gs], _NT, preferred_element_type=F32)
            p_c = _row_softmax(s_c - slope * d_cf, d_c >= 0)
            o_c = jnp.dot(p_c.astype(BF16), vcb_ref[bb, :, gs], preferred_element_type=F32)
            imp = p_c[0:nq]
            for r in range(1, NSA_REP):
                imp = imp + p_c[r * nq:(r + 1) * nq]
            imp_sel = _dot_exact_rhs(imp, pair)
            score = jnp.where(forced > 0.0, _BIG, jnp.where(valid, imp_sel, -1.0))
            before = jnp.zeros((nq, nblk_lanes), F32)
            for bi in range(n_sel):
                sb = score[:, bi:bi + 1]
                before = before + jnp.where(sb > score, 1.0, jnp.where(sb == score, jnp.where(lane > bi, 1.0, 0.0), 0.0))
            sel = jnp.where(valid, jnp.where(before < N_SEL, 1.0, 0.0), 0.0)
            sel_keys = jnp.dot(sel.astype(BF16), expand, preferred_element_type=F32)
            sel_keys = jnp.concatenate([sel_keys] * NSA_REP, axis=0)

            nk = pad_rows(nks_ref[bb, :, gs]).astype(BF16)
            nv = pad_rows(nvs_ref[bb, :, gs]).astype(BF16)
            s_parts = [jnp.dot(qg, ks_pages[bb * npages + j][0, 0, g].astype(BF16), preferred_element_type=F32)
                       for j in range(npages)]
            s_parts.append(lax.dot_general(qg, nk, _NT, preferred_element_type=F32))
            s_s = jnp.concatenate(s_parts, axis=1)
            p_s = _row_softmax(s_s - slope * d_sf, jnp.where(d_s >= 0, sel_keys, 0.0) > 0.0).astype(BF16)
            o_s = jnp.dot(p_s[:, npages * page:], nv, preferred_element_type=F32)
            for j in range(npages):
                o_s = o_s + lax.dot_general(p_s[:, j * page:(j + 1) * page],
                                            vs_pages[bb * npages + j][0, 0, g].astype(BF16), _NT,
                                            preferred_element_type=F32)

            nkw = pad_rows(nkw_ref[bb, :, gs]).astype(BF16)
            nvw = pad_rows(nvw_ref[bb, :, gs]).astype(BF16)
            s_w = jnp.concatenate([jnp.dot(qg, wk_ref[0, bb, g].astype(BF16), preferred_element_type=F32),
                                   lax.dot_general(qg, nkw, _NT, preferred_element_type=F32)], axis=1)
            p_w = _row_softmax(s_w - slope * d_wf, in_window).astype(BF16)
            o_w = (lax.dot_general(p_w[:, 0:wb], wv_ref[0, bb, g].astype(BF16), _NT, preferred_element_type=F32)
                   + jnp.dot(p_w[:, wb:], nvw, preferred_element_type=F32))

            gate = _sigmoid(gl_ref[bb, g])
            o_ref[bb, g] = gate[:, 0:1] * o_c + gate[:, 1:2] * o_s + gate[:, 2:3] * o_w

            for src_ref, new_full, dst_ref in ((wk_ref, new_kt, wko_ref), (wv_ref, new_vt, wvo_ref)):
                new_t = pltpu.roll(new_full[gs, :], _KT - nq, axis=1)
                shifted = pltpu.roll(src_ref[0, bb, g], wb - nq, axis=1)
                dst_ref[bb, g, :, 0:wb - _KT] = shifted[:, 0:wb - _KT]
                dst_ref[bb, g, :, wb - _KT:wb] = jnp.where(lane >= _KT - nq, new_t, shifted[:, wb - _KT:wb])


def nsa_decode(q, glog, kcb, vcb, pool_k, pool_v, page_table, nks, nvs, nkw, nvw, win_k, win_v, layer):
    b, nq, _ = q.shape
    ncb = kcb.shape[1]
    npages = page_table.shape[1]
    page = pool_k.shape[4]
    past = npages * page
    wb = win_k.shape[4]
    assert page == _KT and wb % _KT == 0 and nq & (nq - 1) == 0 and nq % 8 == 0 and nq <= L_SEL
    assert ncb % 2 == 0 and ncb // 2 <= _KT and -(-(past + nq) // L_SEL) <= _KT
    c = NSA_REP * nq
    _, slope = _nsa_cols(nq)
    gl = glog.reshape(b, nq, NSA_KV, NSA_REP, 3).transpose(0, 2, 3, 1, 4).reshape(b, NSA_KV, c, 3)
    nbb = _DECODE_BB if b % _DECODE_BB == 0 else 1
    per_b = lambda shape: pl.BlockSpec((nbb,) + shape, lambda i, pt: (i,) + (0,) * len(shape))
    const = lambda shape: pl.BlockSpec(shape, lambda i, pt: (0,) * len(shape))
    page_spec = lambda bb, j: pl.BlockSpec((1, 1, NSA_KV, NSA_HD, page),
                                           lambda i, pt: (layer, pt[i * nbb + bb, j], 0, 0, 0))
    page_specs = [page_spec(bb, j) for bb in range(nbb) for j in range(npages)]
    win_spec = pl.BlockSpec((1, nbb, NSA_KV, NSA_HD, wb), lambda i, pt: (layer, i, 0, 0, 0))
    n_keys = (npages + 1) * page
    expand = (jnp.arange(n_keys)[None, :] // L_SEL == jnp.arange(_KT)[:, None]).astype(BF16)
    in_specs = ([per_b((nq, NSA_W)), per_b((NSA_KV, c, 3)), per_b((ncb, NSA_KW)), per_b((ncb, NSA_KW))]
                + page_specs * 2
                + [per_b((nq, NSA_KW))] * 4 + [win_spec] * 2 + [const((NSA_KV, c, 1)), const((_KT, n_keys))])
    out, wko, wvo = pl.pallas_call(
        functools.partial(_nsa_decode_kernel, nq=nq, ncb=ncb, npages=npages, page=page, past=past, wb=wb, nbb=nbb),
        grid_spec=pltpu.PrefetchScalarGridSpec(
            num_scalar_prefetch=1,
            grid=(b // nbb,),
            in_specs=in_specs,
            out_specs=[per_b((NSA_KV, c, NSA_HD)), per_b((NSA_KV, NSA_HD, wb)), per_b((NSA_KV, NSA_HD, wb))],
        ),
        out_shape=[jax.ShapeDtypeStruct((b, NSA_KV, c, NSA_HD), F32),
                   jax.ShapeDtypeStruct((b, NSA_KV, NSA_HD, wb), F32),
                   jax.ShapeDtypeStruct((b, NSA_KV, NSA_HD, wb), F32)],
        compiler_params=_params("arbitrary"),
        name="nsa_decode",
    )(page_table, q, gl, _even_odd(kcb).astype(BF16), _even_odd(vcb).astype(BF16),
      *([pool_k] * (nbb * npages)), *([pool_v] * (nbb * npages)), nks, nvs, nkw, nvw, win_k, win_v,
      slope.reshape(NSA_KV, c, 1), expand)
    o = out.reshape(b, NSA_KV, NSA_REP, nq, NSA_HD).transpose(0, 3, 1, 2, 4).reshape(b, nq, NSA_W)
    return o, wko, wvo


def _xattn_cache_kernel(q_ref, k_ref, v_ref, o_ref, *, nq, nbb):
    scale = XA_HD ** -0.5
    for bb in range(nbb):
        q = jnp.concatenate([q_ref[bb, :, h * XA_HD:(h + 1) * XA_HD] for h in range(XA_HEADS)], axis=0)
        k = k_ref[0, bb].astype(BF16)
        v = v_ref[0, bb].astype(BF16)
        s = lax.dot_general(q.astype(BF16), k, _NT, preferred_element_type=F32) * scale
        col_h = lax.broadcasted_iota(jnp.int32, s.shape, 1) & (XA_HEADS - 1)
        row_h = lax.broadcasted_iota(jnp.int32, s.shape, 0) >> (nq.bit_length() - 1)
        mine = col_h == row_h
        m = jnp.max(jnp.where(mine, s, NEG), axis=1, keepdims=True)
        p = jnp.where(mine, jnp.exp(s - m), 0.0)
        p = p / jnp.sum(p, axis=1, keepdims=True)
        o = jnp.dot(p.astype(BF16), v, preferred_element_type=F32)
        for h in range(XA_HEADS):
            o_ref[bb, :, h * XA_HD:(h + 1) * XA_HD] = o[h * nq:(h + 1) * nq]


def xattn_cache(q, cache_k, cache_v, layer):
    b, nq, w = q.shape
    m = cache_k.shape[2]
    assert XA_HEADS & (XA_HEADS - 1) == 0 and nq % 8 == 0
    kv = lambda a: a.reshape(a.shape[0], b, m * XA_HEADS, XA_HD)
    nbb = _DECODE_BB if b % _DECODE_BB == 0 else 1
    kv_spec = pl.BlockSpec((1, nbb, m * XA_HEADS, XA_HD), lambda i: (layer, i, 0, 0))
    return pl.pallas_call(
        functools.partial(_xattn_cache_kernel, nq=nq, nbb=nbb),
        grid=(b // nbb,),
        in_specs=[pl.BlockSpec((nbb, nq, w), lambda i: (i, 0, 0)), kv_spec, kv_spec],
        out_specs=pl.BlockSpec((nbb, nq, w), lambda i: (i, 0, 0)),
        out_shape=jax.ShapeDtypeStruct((b, nq, w), F32),
        compiler_params=_params("parallel"),
        name="xattn_cache",
    )(q, kv(cache_k), kv(cache_v))


_HALO_M = 8
_TN = (((0,), (0,)), ((), ()))


def _softplus(x):
    return jnp.maximum(x, 0.0) + jnp.log1p(jnp.exp(-jnp.abs(x)))


def _ssd_kernel(xbc_ref, z_ref, sm_ref, dtt_ref, cs_ref, h0_ref, cw_ref, cb_ref, dtb_ref, dtbt_ref,
                al_ref, alt_ref, dsk_ref, ng_ref, y_ref, ncs_ref, hf_ref, ext_ref, h_ref, yh_ref, *, ql, dt_col):
    c = pl.program_id(1)
    nc = pl.num_programs(1)

    @pl.when(c == 0)
    def _():
        ext_ref[...] = jnp.zeros_like(ext_ref)
        ext_ref[_HALO_M - (M_CONV_W - 1):_HALO_M, :] = cs_ref[0]
        h_ref[...] = h0_ref[0]

    @pl.when(c > 0)
    def _():
        ext_ref[0:_HALO_M, :] = ext_ref[ql:ql + _HALO_M, :]

    ext_ref[_HALO_M:_HALO_M + ql, :] = xbc_ref[0]
    acc = jnp.zeros((ql, M_CONV_DIM), F32)
    for k in range(M_CONV_W):
        off = _HALO_M - (M_CONV_W - 1) + k
        acc = acc + ext_ref[off:off + ql, :].astype(BF16).astype(F32) * cw_ref[k:k + 1, :]
    xbc = _silu(acc + cb_ref[...])
    xs = xbc[:, 0:M_DIN]
    bm = xbc[:, M_DIN:M_DIN + M_GROUPS * M_DSTATE]
    cm = xbc[:, M_DIN + M_GROUPS * M_DSTATE:M_CONV_DIM]

    dt = _softplus(sm_ref[0, :, dt_col:dt_col + M_HEADS] + dtb_ref[...])
    dtt = _softplus(dtt_ref[0] + dtbt_ref[...])
    dta = dt * (-jnp.exp(al_ref[...]))
    dtat = dtt * (-jnp.exp(alt_ref[...]))
    ti = lax.broadcasted_iota(jnp.int32, (ql, ql), 0)
    si = lax.broadcasted_iota(jnp.int32, (ql, ql), 1)
    causal = si <= ti
    cum = _dot_exact_lhs(jnp.where(causal, 1.0, 0.0).astype(BF16), dta)
    cumt = _dot_exact_rhs(dtat, jnp.where(ti <= si, 1.0, 0.0).astype(BF16))
    cum_last = cum[ql - 1:ql, :]
    edec = jnp.exp(cum)
    eend = jnp.exp(cum_last - cum)
    elast = jnp.exp(cum_last)

    rep = M_HEADS // M_GROUPS
    for gi in range(M_GROUPS):
        b_g = bm[:, gi * M_DSTATE:(gi + 1) * M_DSTATE]
        c_g = cm[:, gi * M_DSTATE:(gi + 1) * M_DSTATE].astype(BF16)
        cb = lax.dot_general(c_g, b_g.astype(BF16), _NT, preferred_element_type=F32)
        for hh in range(rep):
            h = gi * rep + hh
            hs = slice(h * M_HDIM, (h + 1) * M_HDIM)
            lmat = jnp.where(causal, jnp.exp(cum[:, h:h + 1] - cumt[h:h + 1, :]), 0.0)
            x_h = xs[:, hs]
            xdt = (x_h * dt[:, h:h + 1]).astype(BF16)
            y_diag = jnp.dot((cb * lmat).astype(BF16), xdt, preferred_element_type=F32)
            h_in = h_ref[h]
            y_off = lax.dot_general(c_g, h_in.astype(BF16), _NT, preferred_element_type=F32) * edec[:, h:h + 1]
            bd = (b_g * eend[:, h:h + 1]).astype(BF16)
            s_chunk = lax.dot_general(xdt, bd, _TN, preferred_element_type=F32)
            h_ref[h] = elast[:, h:h + 1] * h_in + s_chunk
            yh_ref[:, hs] = y_diag + y_off + dsk_ref[:, hs] * x_h

    yz = yh_ref[...] * _silu(z_ref[0])
    y_ref[0] = _rms(yz, ng_ref[...])

    @pl.when(c == nc - 1)
    def _():
        ncs_ref[0] = ext_ref[_HALO_M + ql - (M_CONV_W - 1):_HALO_M + ql, :]
        hf_ref[0] = h_ref[...]


def ssd_mixer(xbc, z, small, dt_col, conv_state, h0, conv_w, conv_b, dt_bias, a_log, d_skip, norm_g, *, ql):
    b, t, _ = xbc.shape
    nc = t // ql
    sw = small.shape[2]
    dtt = small[:, :, dt_col:dt_col + M_HEADS].transpose(0, 2, 1)
    const = lambda shape: pl.BlockSpec(shape, lambda i, j: (0,) * len(shape))
    per_b = lambda shape: pl.BlockSpec((1,) + shape, lambda i, j: (i,) + (0,) * len(shape))
    row = lambda x: x.reshape(1, -1)
    colv = lambda x: x.reshape(-1, 1)
    return pl.pallas_call(
        functools.partial(_ssd_kernel, ql=ql, dt_col=dt_col),
        grid=(b, nc),
        in_specs=[pl.BlockSpec((1, ql, M_CONV_DIM), lambda i, j: (i, j, 0)),
                  pl.BlockSpec((1, ql, M_DIN), lambda i, j: (i, j, 0)),
                  pl.BlockSpec((1, ql, sw), lambda i, j: (i, j, 0)),
                  pl.BlockSpec((1, M_HEADS, ql), lambda i, j: (i, 0, j)),
                  per_b((M_CONV_W - 1, M_CONV_DIM)), per_b((M_HEADS, M_HDIM, M_DSTATE)),
                  const((M_CONV_W, M_CONV_DIM)), const((1, M_CONV_DIM)),
                  const((1, M_HEADS)), const((M_HEADS, 1)), const((1, M_HEADS)), const((M_HEADS, 1)),
                  const((1, M_DIN)), const((1, M_DIN))],
        out_specs=[pl.BlockSpec((1, ql, M_DIN), lambda i, j: (i, j, 0)),
                   per_b((M_CONV_W - 1, M_CONV_DIM)), per_b((M_HEADS, M_HDIM, M_DSTATE))],
        out_shape=[jax.ShapeDtypeStruct((b, t, M_DIN), F32),
                   jax.ShapeDtypeStruct((b, M_CONV_W - 1, M_CONV_DIM), F32),
                   jax.ShapeDtypeStruct((b, M_HEADS, M_HDIM, M_DSTATE), F32)],
        scratch_shapes=[pltpu.VMEM((_HALO_M + ql, M_CONV_DIM), F32),
                        pltpu.VMEM((M_HEADS, M_HDIM, M_DSTATE), F32),
                        pltpu.VMEM((ql, M_DIN), F32)],
        compiler_params=_params("parallel", "arbitrary"),
        name="ssd_mixer",
    )(xbc, z, small, dtt, conv_state, h0, conv_w, row(conv_b), row(dt_bias), colv(dt_bias),
      row(a_log), colv(a_log), row(jnp.repeat(d_skip, M_HDIM)), row(norm_g))


_SMALL_W = 128
_OD_SPLITS = (NSA_W,) + (NSA_KW,) * 6 + (M_DIN, M_CONV_DIM, _SMALL_W)


def _odd_w_in(w):
    o_kv = NSA_W
    o_gate = o_kv + 6 * NSA_KW
    o_z = o_gate + 3 * NSA_HEADS
    o_xbc = o_z + M_DIN
    o_dt = o_xbc + M_CONV_DIM
    pad = jnp.zeros((w.shape[0], _SMALL_W - 3 * NSA_HEADS - M_HEADS), F32)
    return jnp.concatenate([w[:, :o_gate], w[:, o_z:o_xbc], w[:, o_xbc:o_dt],
                            w[:, o_gate:o_z], w[:, o_dt:], pad], axis=1)


def kernel(x_prompt, x_sample, state_conv_a, state_conv_b, cache_cmp_k, cache_cmp_v, cache_sel_k, cache_sel_v, cache_win_k, cache_win_v, state_ssm, state_ssm_conv, cache_mem_k, cache_mem_v, page_table, mem_prompt, norm_mix, norm_xattn, norm_ffn, norm_final, ev_w_in, ev_conv_a, ev_conv_b, ev_conv_b_bias, ev_ln_g, ev_ln_b, ev_w_out, od_w_in, od_cmp_pe, od_cmp_wk, od_cmp_wv, od_ssm_conv_w, od_ssm_conv_b, od_dt_bias, od_a_log, od_d_skip, od_ssm_norm, od_w_out, xa_wq, xa_wk, xa_wv, xa_wo, moe_wg, moe_bg, moe_we, moe_be, moe_w1, moe_w3, moe_w2):
    bp, tp, d = x_prompt.shape
    bs, ts, _ = x_sample.shape
    n_p, n_s = bp * tp, bs * ts
    n_mem = mem_prompt.shape[1]
    depth = norm_mix.shape[0]
    n_pool, page = cache_cmp_k.shape[1:3]
    wb = cache_win_k.shape[2]
    dt_col = 3 * NSA_HEADS

    def groups(pair):
        return pair[0].reshape(bp, tp, -1), pair[1].reshape(bs, ts, -1)

    def rows(a_p, a_s):
        return [a_p.reshape(n_p, a_p.shape[-1]), a_s.reshape(n_s, a_s.shape[-1])]

    hs = rows(x_prompt, x_sample)
    out = {k: [] for k in ("ca_p", "ca_s", "cb_p", "cb_s", "wk_p", "wk_s", "wv_p", "wv_s",
                           "sm_p", "sm_s", "sc_p", "sc_s", "mk_p", "mv_p")}
    rows_p = [[], [], [], []]
    rows_s = [[], [], [], []]
    for i in range(depth):
        j = i // 2
        if i % 2 == 0:
            u_p, u_s = groups(g_norm_matmul(hs, norm_mix[i], ev_w_in[j]))
            ev = (ev_conv_a[j], ev_conv_b[j], ev_conv_b_bias[j], ev_ln_g[j], ev_ln_b[j])
            y_p, na_p, nb_p = even_conv(u_p, jnp.zeros((bp, CONV_A_W - 1, D_A), F32),
                                        jnp.zeros((bp, CONV_B_W - 1, D_B), F32), *ev)
            y_s, na_s, nb_s = even_conv(u_s, state_conv_a[j], state_conv_b[j], *ev)
            hs = g_matmul_res([rows(y_p, y_s)], [ev_w_out[j]], hs)
            out["ca_p"].append(na_p)
            out["ca_s"].append(na_s)
            out["cb_p"].append(nb_p)
            out["cb_s"].append(nb_s)
        else:
            w_in = _odd_w_in(od_w_in[j])
            c_ks, c_kw = NSA_W + 2 * NSA_KW, NSA_W + 4 * NSA_KW
            u = g_odd_in_proj(hs, norm_mix[i], w_in, w_in[:, NSA_W:NSA_W + 6 * NSA_KW].T, (c_ks, c_kw))
            n_main = len(_OD_SPLITS)
            ks_b, kw_b, vs_t, vw_t = (u[n_main + k][0] for k in range(4))
            kvt_p = [u[n_main + 4 + k][0].reshape(NSA_KV, NSA_HD, bp, tp).transpose(2, 3, 0, 1) for k in range(6)]
            q_p, q_s = groups(u[0])
            kv = [groups(u[1 + k]) for k in range(6)]
            kvp = [a for a, _ in kv]
            kvs = [b for _, b in kv]
            z_p, z_s = groups(u[7])
            xbc_p, xbc_s = groups(u[8])
            sm_p, sm_s = groups(u[9])
            pe, wck, wcv = od_cmp_pe[j], od_cmp_wk[j], od_cmp_wv[j]
            mw = (od_ssm_conv_w[j], od_ssm_conv_b[j], od_dt_bias[j], od_a_log[j], od_d_skip[j], od_ssm_norm[j])
            blocks = lambda a: a.reshape(-1, L_CMP, NSA_KW)
            ncb = tp // L_CMP
            kcb_p = nsa_compress(blocks(kvp[0][:, :ncb * L_CMP]), pe, wck).reshape(bp, ncb, NSA_KW)
            vcb_p = nsa_compress(blocks(kvp[1][:, :ncb * L_CMP]), pe, wcv).reshape(bp, ncb, NSA_KW)
            o_p = nsa_prompt(q_p, sm_p[:, :, :dt_col], kcb_p, vcb_p, ks_b.reshape(bp, tp, NSA_KW), vs_t,
                             kw_b.reshape(bp, tp, NSA_KW), vw_t)
            keep = min(WINDOW, tp)
            y_p, nsc_p, nsm_p = ssd_mixer(xbc_p, z_p, sm_p, dt_col, jnp.zeros((bp, M_CONV_W - 1, M_CONV_DIM), F32),
                                          jnp.zeros((bp, M_HEADS, M_HDIM, M_DSTATE), F32), *mw, ql=128)
            tokens_last = lambda a: jnp.transpose(a, (0, 1, 3, 4, 2))
            kcp = nsa_compress_pages(tokens_last(cache_cmp_k), j, pe, wck)
            vcp = nsa_compress_pages(tokens_last(cache_cmp_v), j, pe, wcv)
            kcb_s = kcp[page_table].reshape(bs, -1, NSA_KW)
            vcb_s = vcp[page_table].reshape(bs, -1, NSA_KW)
            o_s, nwk_s, nwv_s = nsa_decode(
                q_s, sm_s[:, :, :dt_col], kcb_s, vcb_s, tokens_last(cache_sel_k), tokens_last(cache_sel_v),
                page_table, kvs[2], kvs[3], kvs[4], kvs[5], tokens_last(cache_win_k), tokens_last(cache_win_v), j)
            nwk_s = jnp.transpose(nwk_s, (0, 3, 1, 2))
            nwv_s = jnp.transpose(nwv_s, (0, 3, 1, 2))
            y_s, nsc_s, nsm_s = ssd_mixer(xbc_s, z_s, sm_s, dt_col, state_ssm_conv[j], state_ssm[j], *mw, ql=ts)
            w_out = od_w_out[j]
            hs = g_matmul_res([rows(o_p, o_s), rows(y_p, y_s)], [w_out[:NSA_W], w_out[NSA_W:]], hs)
            heads = lambda a: a.reshape(a.shape[0], a.shape[1], NSA_KV, NSA_HD)
            for k in range(4):
                rows_p[k].append(kvt_p[k])
                rows_s[k].append(heads(kvs[k]))
            out["wk_p"].append(kvt_p[4][:, tp - keep:])
            out["wv_p"].append(kvt_p[5][:, tp - keep:])
            out["wk_s"].append(nwk_s)
            out["wv_s"].append(nwv_s)
            out["sc_p"].append(nsc_p)
            out["sc_s"].append(nsc_s)
            out["sm_p"].append(nsm_p)
            out["sm_s"].append(nsm_s)
        mk, mv = norm_matmul(mem_prompt.reshape(bp * n_mem, d), None,
                             jnp.concatenate([xa_wk[i], xa_wv[i]], axis=1), norm=False,
                             splits=(XA_HEADS * XA_HD, XA_HEADS * XA_HD))
        mk = mk.reshape(bp, n_mem, XA_HEADS * XA_HD)
        mv = mv.reshape(bp, n_mem, XA_HEADS * XA_HD)
        out["mk_p"].append(mk.reshape(bp, n_mem, XA_HEADS, XA_HD))
        out["mv_p"].append(mv.reshape(bp, n_mem, XA_HEADS, XA_HD))
        qx_p, qx_s = groups(g_norm_matmul(hs, norm_xattn[i], xa_wq[i]))
        ox_p = xattn(qx_p, mk, mv)
        ox_s = xattn_cache(qx_s, cache_mem_k, cache_mem_v, i)
        hs = g_matmul_res([rows(ox_p, ox_s)], [xa_wo[i]], hs)
        hs = moe_layer(hs, norm_ffn[i], moe_wg[i], moe_bg[i], moe_we[i], moe_be[i], moe_w1, moe_w3, moe_w2, i)
    y_prompt, y_sample = groups(g_rmsnorm(hs, norm_final))
    st = lambda k: jnp.stack(out[k])
    return (y_prompt, y_sample, st("ca_p"), st("ca_s"), st("cb_p"), st("cb_s"),
            jnp.stack(rows_p[0]), jnp.stack(rows_s[0]), jnp.stack(rows_p[1]), jnp.stack(rows_s[1]),
            jnp.stack(rows_p[2]), jnp.stack(rows_s[2]), jnp.stack(rows_p[3]), jnp.stack(rows_s[3]),
            st("wk_p"), st("wk_s"), st("wv_p"), st("wv_s"), st("sm_p"), st("sm_s"), st("sc_p"), st("sc_s"),
            st("mk_p"), st("mv_p"))
```

```python
import functools

import jax
import jax.numpy as jnp
from jax import lax
from jax.experimental import pallas as pl
from jax.experimental.pallas import tpu as pltpu

F32 = jnp.float32
BF16 = jnp.bfloat16
EPS = 1e-6
NEG = -1e30
VMEM_LIMIT = 56 * 1024 * 1024

D_A = 512
D_B = 512
CONV_A_W = 3
CONV_B_W = 31
NSA_HEADS = 8
NSA_HD = 64
NSA_KV = 2
NSA_REP = NSA_HEADS // NSA_KV
NSA_W = NSA_HEADS * NSA_HD
NSA_KW = NSA_KV * NSA_HD
L_CMP = 32
L_SEL = 64
N_SEL = 16
WINDOW = 512
M_DIN = 512
M_HDIM = 64
M_HEADS = 8
M_DSTATE = 64
M_GROUPS = 2
M_CONV_W = 4
M_CONV_DIM = M_DIN + 2 * M_GROUPS * M_DSTATE
XA_HEADS = 4
XA_HD = 128
MOE_GROUPS = 4
MOE_EPG = 8
MOE_E = 32
MOE_TOPK = 2


def _params(*sem):
    return pltpu.CompilerParams(dimension_semantics=sem, vmem_limit_bytes=VMEM_LIMIT)


def _row_tile(n, pref):
    t = min(n, pref)
    while n % t or (t % 8 and t != n):
        t -= 1
    return t


def _split3(a):
    hi = a.astype(BF16)
    r1 = a - hi.astype(F32)
    mid = r1.astype(BF16)
    lo = (r1 - mid.astype(F32)).astype(BF16)
    return hi, mid, lo


def _dot_exact_rhs(a, b_bf16):
    hi, mid, lo = _split3(a)
    d = lambda x: jnp.dot(x, b_bf16, preferred_element_type=F32)
    return d(hi) + d(mid) + d(lo)


def _dot_exact_lhs(a_bf16, b):
    hi, mid, lo = _split3(b)
    d = lambda x: jnp.dot(a_bf16, x, preferred_element_type=F32)
    return d(hi) + d(mid) + d(lo)


def _rms(x, g):
    ms = jnp.mean(x * x, axis=-1, keepdims=True)
    return x * lax.rsqrt(ms + EPS) * g


def _sigmoid(x):
    return 1.0 / (1.0 + jnp.exp(-x))


def _silu(x):
    return x * _sigmoid(x)


def _norm_matmul_kernel(x_ref, g_ref, w_ref, *o_refs, norm, splits):
    x = x_ref[...]
    if norm:
        x = _rms(x, g_ref[...])
    res = jnp.dot(x.astype(BF16), w_ref[...].astype(BF16), preferred_element_type=F32)
    off = 0
    for o_ref, width in zip(o_refs, splits):
        o_ref[...] = res[:, off:off + width]
        off += width


def norm_matmul(x, g, w, *, norm=True, splits=None, tm=512):
    n, k = x.shape
    m = w.shape[1]
    tm = _row_tile(n, tm)
    if g is None:
        g = jnp.ones((k,), F32)
    widths = (m,) if splits is None else tuple(splits)
    assert sum(widths) == m
    outs = pl.pallas_call(
        functools.partial(_norm_matmul_kernel, norm=norm, splits=widths),
        grid=(n // tm,),
        in_specs=[pl.BlockSpec((tm, k), lambda i: (i, 0)),
                  pl.BlockSpec((1, k), lambda i: (0, 0)),
                  pl.BlockSpec((k, m), lambda i: (0, 0))],
        out_specs=[pl.BlockSpec((tm, wd), lambda i: (i, 0)) for wd in widths],
        out_shape=[jax.ShapeDtypeStruct((n, wd), F32) for wd in widths],
        compiler_params=_params("parallel"),
        name="norm_matmul",
    )(x, g.reshape(1, k), w)
    return outs[0] if splits is None else outs


_TM = 512


def _rowwise_call(body, row_inputs, shared, out_widths, out_dtypes, name, joint_outputs=False, transposed=()):
    ns = [a.shape[0] for a in row_inputs[0]]
    assert all(n % _TM == 0 for n in ns)
    nbs = [n // _TM for n in ns]
    starts = [sum(nbs[:g]) for g in range(len(ns))]
    n_groups, n_row, n_out = len(ns), len(row_inputs), len(out_widths)

    def group_map(g):
        return lambda i: (jnp.clip(i - starts[g], 0, nbs[g] - 1), 0)

    in_specs, args = [], []
    for k in range(n_row):
        for g in range(n_groups):
            a = row_inputs[k][g]
            in_specs.append(pl.BlockSpec((_TM, a.shape[1]), group_map(g)))
            args.append(a)
    for a in shared:
        in_specs.append(pl.BlockSpec(a.shape, lambda i, nd=a.ndim: (0,) * nd))
        args.append(a)
    if joint_outputs:
        out_specs = [pl.BlockSpec((_TM, w), lambda i: (i, 0)) for w in out_widths]
        out_shape = [jax.ShapeDtypeStruct((sum(ns), w), dt) for w, dt in zip(out_widths, out_dtypes)]
    else:
        def group_map_t(g):
            return lambda i: (0, jnp.clip(i - starts[g], 0, nbs[g] - 1))

        out_specs, out_shape = [], []
        for j, (w, dt) in enumerate(zip(out_widths, out_dtypes)):
            for g in range(n_groups):
                if j in transposed:
                    out_specs.append(pl.BlockSpec((w, _TM), group_map_t(g)))
                    out_shape.append(jax.ShapeDtypeStruct((w, ns[g]), dt))
                else:
                    out_specs.append(pl.BlockSpec((_TM, w), group_map(g)))
                    out_shape.append(jax.ShapeDtypeStruct((ns[g], w), dt))

    def kernel(*refs):
        x_refs = refs[:n_row * n_groups]
        s_refs = refs[n_row * n_groups:n_row * n_groups + len(shared)]
        o_refs = refs[n_row * n_groups + len(shared):]
        i = pl.program_id(0)
        for g in range(n_groups):
            @pl.when((i >= starts[g]) & (i < starts[g] + nbs[g]))
            def _(g=g):
                vals = body([x_refs[k * n_groups + g][...] for k in range(n_row)], s_refs)
                for j, v in enumerate(vals):
                    o_ref = o_refs[j] if joint_outputs else o_refs[j * n_groups + g]
                    o_ref[...] = v.astype(o_ref.dtype)

    outs = pl.pallas_call(
        kernel, grid=(sum(nbs),), in_specs=in_specs, out_specs=out_specs, out_shape=out_shape,
        compiler_params=_params("arbitrary"), name=name)(*args)
    if joint_outputs:
        return list(outs)
    return [list(outs[j * n_groups:(j + 1) * n_groups]) for j in range(n_out)]


def g_norm_matmul(hs, g, w, *, splits=None, norm=True):
    widths = (w.shape[1],) if splits is None else tuple(splits)
    assert sum(widths) == w.shape[1]

    def body(xs, s_refs):
        x = _rms(xs[0], s_refs[0][...]) if norm else xs[0]
        res = jnp.dot(x.astype(BF16), s_refs[1][...].astype(BF16), preferred_element_type=F32)
        offs = [sum(widths[:j]) for j in range(len(widths))]
        return [res[:, o:o + wd] for o, wd in zip(offs, widths)]

    k = hs[0].shape[1]
    gv = jnp.ones((1, k), F32) if g is None else g.reshape(1, k)
    outs = _rowwise_call(body, [hs], [gv, w], widths, [F32] * len(widths), "norm_matmul")
    return outs[0] if splits is None else outs


def g_odd_in_proj(hs, g, w, w_kvt, k_cols):
    widths = _OD_SPLITS + (NSA_KW,) * 4 + (NSA_KW,) * 6
    offs = [sum(_OD_SPLITS[:j]) for j in range(len(_OD_SPLITS))]

    def body(xs, s_refs):
        x = _rms(xs[0], s_refs[0][...]).astype(BF16)
        res = jnp.dot(x, s_refs[1][...].astype(BF16), preferred_element_type=F32)
        kvt = lax.dot_general(s_refs[2][...].astype(BF16), x, _NT, preferred_element_type=F32)
        part = lambda k: kvt[k * NSA_KW:(k + 1) * NSA_KW]
        outs = [res[:, o:o + wd] for o, wd in zip(offs, _OD_SPLITS)]
        outs += [res[:, c:c + NSA_KW] for c in k_cols]
        outs += [part(3), part(5)]
        outs += [part(k) for k in range(6)]
        return outs

    n_main = len(_OD_SPLITS)
    dts = [F32] * n_main + [BF16] * 4 + [F32] * 6
    return _rowwise_call(body, [hs], [g.reshape(1, -1), w, w_kvt], widths, dts, "odd_in_proj",
                         transposed=(n_main + 2, n_main + 3) + tuple(range(n_main + 4, n_main + 10)))


def g_matmul_res(xs_list, ws, hs):
    def body(xs, s_refs):
        acc = xs[-1]
        for j in range(len(ws)):
            acc = acc + jnp.dot(xs[j].astype(BF16), s_refs[j][...].astype(BF16), preferred_element_type=F32)
        return [acc]

    return _rowwise_call(body, list(xs_list) + [hs], list(ws), (hs[0].shape[1],), [F32], "matmul_res")[0]


def g_rmsnorm(hs, g):
    body = lambda xs, s_refs: [_rms(xs[0], s_refs[0][...])]
    return _rowwise_call(body, [hs], [g.reshape(1, -1)], (hs[0].shape[1],), [F32], "rmsnorm_rows")[0]


def g_moe_router(hs, g, w_router):
    def body(xs, s_refs):
        xb = _rms(xs[0], s_refs[0][...]).astype(BF16)
        logits = jnp.dot(xb, s_refs[1][...].astype(BF16), preferred_element_type=F32)
        half = xb.shape[1] // 2
        bits = lambda v: lax.bitcast_convert_type(v.astype(F32), jnp.uint32)
        words = (bits(xb[:, half:]) & jnp.uint32(0xFFFF0000)) | (bits(xb[:, :half]) >> 16)
        return [logits, lax.bitcast_convert_type(words, F32)]

    k = hs[0].shape[1]
    return _rowwise_call(body, [hs], [g.reshape(1, k), w_router], (_ROUTER_W, k // 2), [F32, F32], "moe_router",
                         joint_outputs=True)


_HALO_A = 8
_HALO_B = 32


def _even_conv_kernel(u_ref, sa_ref, sb_ref, wa_ref, wb_ref, bb_ref, lg_ref, lb_ref,
                      y_ref, na_ref, nb_ref, ea_ref, eb_ref, ear_ref, ebr_ref, sh_ref, *, tt):
    rnd = lambda x: x.astype(BF16).astype(F32)
    t = pl.program_id(1)
    nt = pl.num_programs(1)

    @pl.when(t == 0)
    def _():
        ea_ref[...] = jnp.zeros_like(ea_ref)
        eb_ref[...] = jnp.zeros_like(eb_ref)
        ea_ref[_HALO_A - (CONV_A_W - 1):_HALO_A, :] = sa_ref[0]
        eb_ref[_HALO_B - (CONV_B_W - 1):_HALO_B, :] = sb_ref[0]
        ear_ref[...] = rnd(ea_ref[...])
        ebr_ref[...] = rnd(eb_ref[...])

    @pl.when(t > 0)
    def _():
        ea_ref[0:_HALO_A, :] = ea_ref[tt:tt + _HALO_A, :]
        eb_ref[0:_HALO_B, :] = eb_ref[tt:tt + _HALO_B, :]
        ear_ref[0:_HALO_A, :] = ear_ref[tt:tt + _HALO_A, :]
        ebr_ref[0:_HALO_B, :] = ebr_ref[tt:tt + _HALO_B, :]

    xa = u_ref[0, :, 0:D_A]
    ba = u_ref[0, :, D_A:2 * D_A]
    ca = u_ref[0, :, 2 * D_A:3 * D_A]
    pb = u_ref[0, :, 3 * D_A:3 * D_A + D_B]
    gb = u_ref[0, :, 3 * D_A + D_B:3 * D_A + 2 * D_B]
    va = ca * xa
    vb = pb * _sigmoid(gb)
    ea_ref[_HALO_A:_HALO_A + tt, :] = va
    eb_ref[_HALO_B:_HALO_B + tt, :] = vb
    ear_ref[_HALO_A:_HALO_A + tt, :] = rnd(va)
    ebr_ref[_HALO_B:_HALO_B + tt, :] = rnd(vb)

    acc = jnp.zeros((tt, D_A), F32)
    for k in range(CONV_A_W):
        off = _HALO_A - (CONV_A_W - 1) + k
        acc = acc + ear_ref[off:off + tt, :] * wa_ref[k:k + 1, :]
    y_ref[0, :, 0:D_A] = ba * acc

    span = tt + _HALO_B - 8
    for r in range(1, 8):
        sh_ref[r - 1, 0:span, :] = ebr_ref[r:r + span, :]
    acc = jnp.zeros((tt, D_B), F32)
    for k in range(CONV_B_W):
        off = _HALO_B - (CONV_B_W - 1) + k
        base = off - off % 8
        rows = ebr_ref[base:base + tt, :] if off % 8 == 0 else sh_ref[off % 8 - 1, base:base + tt, :]
        acc = acc + rows * wb_ref[k:k + 1, :]
    acc = acc + bb_ref[...]
    mu = jnp.mean(acc, axis=-1, keepdims=True)
    xc = acc - mu
    var = jnp.mean(xc * xc, axis=-1, keepdims=True)
    yb = xc * lax.rsqrt(var + EPS) * lg_ref[...] + lb_ref[...]
    y_ref[0, :, D_A:D_A + D_B] = _silu(yb)

    @pl.when(t == nt - 1)
    def _():
        na_ref[0] = ea_ref[_HALO_A + tt - (CONV_A_W - 1):_HALO_A + tt, :]
        nb_ref[0] = eb_ref[_HALO_B + tt - (CONV_B_W - 1):_HALO_B + tt, :]


def even_conv(u, sa, sb, wa, wb, bb, lg, lb, *, tt=256):
    b, t, w = u.shape
    tt = _row_tile(t, tt)
    full = lambda shape: pl.BlockSpec(shape, lambda i, j: (0,) * len(shape))
    return pl.pallas_call(
        functools.partial(_even_conv_kernel, tt=tt),
        grid=(b, t // tt),
        in_specs=[pl.BlockSpec((1, tt, w), lambda i, j: (i, j, 0)),
                  pl.BlockSpec((1, CONV_A_W - 1, D_A), lambda i, j: (i, 0, 0)),
                  pl.BlockSpec((1, CONV_B_W - 1, D_B), lambda i, j: (i, 0, 0)),
                  full((CONV_A_W, D_A)), full((CONV_B_W, D_B)), full((1, D_B)),
                  full((1, D_B)), full((1, D_B))],
        out_specs=[pl.BlockSpec((1, tt, D_A + D_B), lambda i, j: (i, j, 0)),
                   pl.BlockSpec((1, CONV_A_W - 1, D_A), lambda i, j: (i, 0, 0)),
                   pl.BlockSpec((1, CONV_B_W - 1, D_B), lambda i, j: (i, 0, 0))],
        out_shape=[jax.ShapeDtypeStruct((b, t, D_A + D_B), F32),
                   jax.ShapeDtypeStruct((b, CONV_A_W - 1, D_A), F32),
                   jax.ShapeDtypeStruct((b, CONV_B_W - 1, D_B), F32)],
        scratch_shapes=[pltpu.VMEM((_HALO_A + tt, D_A), F32), pltpu.VMEM((_HALO_B + tt, D_B), F32),
                        pltpu.VMEM((_HALO_A + tt, D_A), F32), pltpu.VMEM((_HALO_B + tt, D_B), F32),
                        pltpu.VMEM((7, _HALO_B + tt - 8, D_B), F32)],
        compiler_params=_params("parallel", "arbitrary"),
        name="even_conv",
    )(u, sa, sb, wa, wb, bb.reshape(1, D_B), lg.reshape(1, D_B), lb.reshape(1, D_B))


def _xattn_kernel(q_ref, k_ref, v_ref, o_ref):
    scale = XA_HD ** -0.5
    for h in range(XA_HEADS):
        sl = slice(h * XA_HD, (h + 1) * XA_HD)
        q = q_ref[0, :, sl].astype(BF16)
        k = k_ref[0, :, sl].astype(BF16)
        v = v_ref[0, :, sl].astype(BF16)
        s = lax.dot_general(q, k, (((1,), (1,)), ((), ())), preferred_element_type=F32) * scale
        m = jnp.max(s, axis=-1, keepdims=True)
        p = jnp.exp(s - m)
        p = p / jnp.sum(p, axis=-1, keepdims=True)
        o_ref[0, :, sl] = jnp.dot(p.astype(BF16), v, preferred_element_type=F32)


def xattn(q, k, v, *, tq=512):
    b, t, w = q.shape
    tq = _row_tile(t, tq)
    m = k.shape[1]
    kv_spec = pl.BlockSpec((1, m, w), lambda i, j: (i, 0, 0))
    return pl.pallas_call(
        _xattn_kernel,
        grid=(b, t // tq),
        in_specs=[pl.BlockSpec((1, tq, w), lambda i, j: (i, j, 0)), kv_spec, kv_spec],
        out_specs=pl.BlockSpec((1, tq, w), lambda i, j: (i, j, 0)),
        out_shape=jax.ShapeDtypeStruct((b, t, w), F32),
        compiler_params=_params("parallel", "parallel"),
        name="xattn",
    )(q, k, v)


MOE_BLK = 256
_ROUTER_W = 128


def _expert_gather_kernel(tok_ref, be_ref, act_ref, xn_hbm, gate_ref, w1_ref, w3_ref, w2_ref, o_ref,
                          xbuf_ref, sem_ref, w1b_ref, w3b_ref, w2b_ref):
    i = pl.program_id(0)
    nb = pl.num_programs(0)

    def row_copy(blk, slot, r):
        tok = tok_ref[blk * MOE_BLK + r]
        return pltpu.make_async_copy(xn_hbm.at[pl.ds(tok, 1)], xbuf_ref.at[slot, pl.ds(r, 1)], sem_ref.at[slot])

    def start_gather(blk, slot):
        if isinstance(slot, int):
            for r in range(MOE_BLK):
                row_copy(blk, slot, r).start()
            return
        for s in range(2):
            @pl.when(slot == s)
            def _(s=s):
                for r in range(MOE_BLK):
                    row_copy(blk, s, r).start()

    def wait_gather(slot):
        pltpu.make_async_copy(xbuf_ref.at[slot], xbuf_ref.at[slot], sem_ref.at[slot]).wait()

    slot = lax.rem(i, 2)
    nxt = jnp.minimum(i + 1, nb - 1)

    @pl.when((i == 0) & (act_ref[0] > 0))
    def _():
        start_gather(0, 0)

    @pl.when((i + 1 < nb) & (act_ref[nxt] > 0))
    def _():
        start_gather(nxt, 1 - slot)

    prev = be_ref[jnp.maximum(i - 1, 0)]

    @pl.when((act_ref[i] > 0) & ((i == 0) | (be_ref[i] != prev)))
    def _():
        w1b_ref[...] = w1_ref[0, 0].astype(BF16)
        w3b_ref[...] = w3_ref[0, 0].astype(BF16)
        w2b_ref[...] = w2_ref[0, 0].astype(BF16)

    @pl.when(act_ref[i] > 0)
    def _():
        wait_gather(slot)
        words = lax.bitcast_convert_type(xbuf_ref[slot], jnp.uint32)
        unpack = lambda v: lax.bitcast_convert_type(v, F32).astype(BF16)
        x = jnp.concatenate([unpack(words << 16), unpack(words & jnp.uint32(0xFFFF0000))], axis=1)
        h1 = jnp.dot(x, w1b_ref[...], preferred_element_type=F32)
        h3 = jnp.dot(x, w3b_ref[...], preferred_element_type=F32)
        hid = (_silu(h1) * h3).astype(BF16)
        out = jnp.dot(hid, w2b_ref[...], preferred_element_type=F32)
        o_ref[...] = out * gate_ref[...]

    @pl.when(act_ref[i] == 0)
    def _():
        o_ref[...] = jnp.zeros_like(o_ref)


def moe_experts_gather(xn, buf_tok, gate, blk_exp, blk_act, w1, w3, w2, layer):
    rows = buf_tok.shape[0]
    d = w1.shape[2]
    nb = rows // MOE_BLK
    ff = w1.shape[3]
    return pl.pallas_call(
        _expert_gather_kernel,
        grid_spec=pltpu.PrefetchScalarGridSpec(
            num_scalar_prefetch=3,
            grid=(nb,),
            in_specs=[pl.BlockSpec(memory_space=pl.ANY),
                      pl.BlockSpec((MOE_BLK, 1), lambda i, tok, be, act: (i, 0)),
                      pl.BlockSpec((1, 1, d, ff), lambda i, tok, be, act: (layer, be[i], 0, 0)),
                      pl.BlockSpec((1, 1, d, ff), lambda i, tok, be, act: (layer, be[i], 0, 0)),
                      pl.BlockSpec((1, 1, ff, d), lambda i, tok, be, act: (layer, be[i], 0, 0))],
            out_specs=pl.BlockSpec((MOE_BLK, d), lambda i, tok, be, act: (i, 0)),
            scratch_shapes=[pltpu.VMEM((2, MOE_BLK, d // 2), F32), pltpu.SemaphoreType.DMA((2,)),
                            pltpu.VMEM((d, ff), BF16), pltpu.VMEM((d, ff), BF16), pltpu.VMEM((ff, d), BF16)],
        ),
        out_shape=jax.ShapeDtypeStruct((rows, d), F32),
        compiler_params=_params("arbitrary"),
        name="moe_experts",
    )(buf_tok, blk_exp, blk_act, xn, gate, w1, w3, w2)


_COMBINE_TM = 256


def _combine_kernel(dest_ref, *refs, starts, nbs):
    n_groups = len(starts)
    h_refs = refs[:n_groups]
    out_hbm = refs[n_groups]
    o_refs = refs[n_groups + 1:2 * n_groups + 1]
    buf_ref, sem_ref = refs[2 * n_groups + 1:]
    i = pl.program_id(0)
    nsteps = pl.num_programs(0)
    tm = _COMBINE_TM

    def start_gather(step, slot):
        for s in range(2):
            @pl.when(slot == s)
            def _(s=s):
                for r in range(tm):
                    for k in range(MOE_TOPK):
                        row = dest_ref[(step * tm + r) * MOE_TOPK + k]
                        pltpu.make_async_copy(out_hbm.at[pl.ds(row, 1)], buf_ref.at[s, k, pl.ds(r, 1)],
                                              sem_ref.at[s]).start()

    slot = lax.rem(i, 2)

    @pl.when(i == 0)
    def _():
        start_gather(i, slot)

    @pl.when(i + 1 < nsteps)
    def _():
        start_gather(i + 1, 1 - slot)

    pltpu.make_async_copy(buf_ref.at[slot], buf_ref.at[slot], sem_ref.at[slot]).wait()
    y = buf_ref[slot, 0] + buf_ref[slot, 1]
    for g in range(n_groups):
        @pl.when((i >= starts[g]) & (i < starts[g] + nbs[g]))
        def _(g=g):
            o_refs[g][...] = h_refs[g][...] + y


def moe_combine(hs, out, dest):
    tm = _COMBINE_TM
    ns = [h.shape[0] for h in hs]
    d = hs[0].shape[1]
    assert all(n % tm == 0 for n in ns)
    nbs = [n // tm for n in ns]
    starts = [sum(nbs[:g]) for g in range(len(ns))]

    def group_map(g):
        return lambda i, dest: (jnp.clip(i - starts[g], 0, nbs[g] - 1), 0)

    specs = [pl.BlockSpec((tm, d), group_map(g)) for g in range(len(ns))]
    return pl.pallas_call(
        functools.partial(_combine_kernel, starts=starts, nbs=nbs),
        grid_spec=pltpu.PrefetchScalarGridSpec(
            num_scalar_prefetch=1,
            grid=(sum(nbs),),
            in_specs=specs + [pl.BlockSpec(memory_space=pl.ANY)],
            out_specs=specs,
            scratch_shapes=[pltpu.VMEM((2, MOE_TOPK, tm, d), F32), pltpu.SemaphoreType.DMA((2,))],
        ),
        out_shape=[jax.ShapeDtypeStruct((n, d), F32) for n in ns],
        compiler_params=_params("arbitrary"),
        name="moe_combine",
    )(dest.reshape(-1), *hs, out)


def moe_layer(hs, g, wg, bg, we, be, w1, w3, w2, layer):
    d = hs[0].shape[1]
    n = sum(h.shape[0] for h in hs)
    w_router = jnp.concatenate([wg, we, jnp.zeros((d, _ROUTER_W - MOE_GROUPS - MOE_E), F32)], axis=1)
    logits, xn = g_moe_router(hs, g, w_router)
    lg = logits[:, :MOE_GROUPS] + bg
    grp = jnp.argmax(lg, axis=-1)
    gw = jnp.take_along_axis(jax.nn.softmax(lg, axis=-1), grp[:, None], axis=1)
    le = (logits[:, MOE_GROUPS:MOE_GROUPS + MOE_E] + be).reshape(n, MOE_GROUPS, MOE_EPG)
    le = jnp.take_along_axis(le, grp[:, None, None], axis=1)[:, 0]
    tv, ti = lax.top_k(jax.nn.softmax(le, axis=-1), MOE_TOPK)
    wts = gw * tv / jnp.sum(tv, axis=-1, keepdims=True)
    eid = (grp[:, None] * MOE_EPG + ti).reshape(-1).astype(jnp.int32)
    npair = n * MOE_TOPK
    experts = jnp.arange(MOE_E, dtype=jnp.int32)
    order = jnp.argsort(eid).astype(jnp.int32)
    rank = jnp.argsort(order).astype(jnp.int32)
    counts = jnp.sum(eid[:, None] == experts[None, :], axis=0).astype(jnp.int32)
    start = jnp.cumsum(counts) - counts
    padded = (counts + MOE_BLK - 1) // MOE_BLK * MOE_BLK
    pend = jnp.cumsum(padded)
    shift = pend - padded - start
    nb = -(-npair // MOE_BLK) + MOE_E
    blk_lo = jnp.arange(nb, dtype=jnp.int32) * MOE_BLK
    blk_exp = jnp.minimum(jnp.sum(pend[None, :] <= blk_lo[:, None], axis=1), MOE_E - 1).astype(jnp.int32)
    blk_act = (blk_lo < pend[-1]).astype(jnp.int32)
    src = (blk_lo - shift[blk_exp])[:, None] + jnp.arange(MOE_BLK, dtype=jnp.int32)[None, :]
    live = src < (start + counts)[blk_exp][:, None]
    pair = order[jnp.where(live, src, 0).reshape(-1)]
    buf_tok = pair // MOE_TOPK
    buf_gate = jnp.where(live.reshape(-1), wts.reshape(-1)[pair], 0.0)
    dest = (rank + shift[eid]).reshape(n, MOE_TOPK)
    out = moe_experts_gather(xn, buf_tok.astype(jnp.int32), buf_gate[:, None], blk_exp, blk_act, w1, w3, w2, layer)
    return moe_combine(hs, out, dest)


_KT = 128
_NT = (((1,), (1,)), ((), ()))
_BIG = 3e38
_M0 = -1e29


def _compress_kernel(x_ref, pe_ref, w_ref, o_ref):
    acc = jnp.zeros(o_ref.shape, F32)
    for l in range(L_CMP):
        y = x_ref[:, l, :] + pe_ref[l:l + 1, :]
        acc = acc + jnp.dot(y.astype(BF16), w_ref[l], preferred_element_type=F32)
    o_ref[...] = acc


def nsa_compress(x, pe, w, *, tb=256):
    nb = x.shape[0]
    tb = _row_tile(nb, tb)
    pe2 = jnp.concatenate([pe] * NSA_KV, axis=1)
    z = jnp.zeros_like(w)
    w2 = jnp.concatenate([jnp.concatenate([w, z], axis=2), jnp.concatenate([z, w], axis=2)], axis=1).astype(BF16)
    return pl.pallas_call(
        _compress_kernel,
        grid=(nb // tb,),
        in_specs=[pl.BlockSpec((tb, L_CMP, NSA_KW), lambda i: (i, 0, 0)),
                  pl.BlockSpec((L_CMP, NSA_KW), lambda i: (0, 0)),
                  pl.BlockSpec((L_CMP, NSA_KW, NSA_KW), lambda i: (0, 0, 0))],
        out_specs=pl.BlockSpec((tb, NSA_KW), lambda i: (i, 0)),
        out_shape=jax.ShapeDtypeStruct((nb, NSA_KW), F32),
        compiler_params=_params("parallel"),
        name="nsa_compress",
    )(x, pe2, w2)


def _qz(q, g, nq):
    lane = lax.broadcasted_iota(jnp.int32, (nq, NSA_KW), 1)
    keep = (lane >> 6) == g
    parts = []
    for r in range(NSA_REP):
        h = g * NSA_REP + r
        slab = q[:, (h // 2) * NSA_KW:(h // 2 + 1) * NSA_KW]
        if h % 2 != g:
            slab = pltpu.roll(slab, NSA_HD, axis=1)
        parts.append(jnp.where(keep, slab, 0.0))
    return jnp.concatenate(parts, axis=0).astype(BF16)


def _softmax_done(l_ref, acc_ref, g):
    l = l_ref[g]
    return acc_ref[g] * jnp.where(l > 0.0, 1.0 / l, 0.0)


def _compressed_branch(qz, kcb, vcbt_g, qpos, slope, ncb):
    st = lax.dot_general(kcb, qz, _NT, preferred_element_type=F32)
    row = lax.broadcasted_iota(jnp.int32, (ncb, 1), 0)
    half = ncb // 2
    blk = jnp.where(row < half, 2 * row, 2 * (row - half) + 1)
    c_pos = blk * L_CMP + (L_CMP - 1)
    d_c = qpos - c_pos
    maskf = jnp.where(d_c >= 0, 1.0, 0.0)
    s = st - slope * d_c.astype(F32)
    sm = jnp.where(d_c >= 0, s, NEG)
    m = jnp.max(sm, axis=0, keepdims=True)
    p = jnp.exp(sm - m) * maskf
    l = jnp.sum(p, axis=0, keepdims=True)
    p = p * jnp.where(l > 0.0, 1.0 / l, 0.0)
    o = jnp.dot(vcbt_g, p.astype(BF16), preferred_element_type=F32)
    return o, p


def _select_blocks(imp_sel, qpos, nsp):
    cols = imp_sel.shape[1]
    blk = lax.broadcasted_iota(jnp.int32, (nsp, cols), 0)
    cur = qpos >> 6
    valid = blk <= cur
    forced = jnp.where(valid, jnp.where(blk == 0, 1.0, jnp.where(blk >= cur - 1, 1.0, 0.0)), 0.0)
    score = jnp.where(forced > 0.0, _BIG, jnp.where(valid, imp_sel, -1.0))
    sel = jnp.zeros((nsp, cols), F32)
    for _ in range(N_SEL):
        m = jnp.max(score, axis=0, keepdims=True)
        idx = jnp.min(jnp.where(score == m, blk, nsp + 1), axis=0, keepdims=True)
        pick = blk == idx
        sel = jnp.where(pick, 1.0, sel)
        score = jnp.where(pick, -2.0, score)
    return jnp.where(valid, sel, 0.0)


_TS = 256


def _pos_lanes(nkeys, tile_off):
    lane = lax.broadcasted_iota(jnp.int32, (nkeys, NSA_KW), 1)
    key = lax.broadcasted_iota(jnp.int32, (nkeys, NSA_KW), 0)
    hi = (tile_off + (key >> 7)).astype(F32)
    lo = (key & (_KT - 1)).astype(F32)
    return jnp.where(lane == 0, hi, jnp.where(lane == 1, lo, 0.0)).astype(BF16)


def _nsa_prompt_kernel(q_ref, glog_ref, kcb_ref, vcbt_ref, ks_ref, vst_ref, kw_ref, vwt_ref,
                        qoff_ref, slope_ref, qaux_ref, o_ref, sel_ref, m_ref, l_ref, acc_ref, *, nq, ncb):
    i = pl.program_id(1)
    st0 = i * nq
    qoff = qoff_ref[...]
    qpos = st0 + qoff
    q = q_ref[0] * (NSA_HD ** -0.5)
    nsp = ncb // 2
    bpt = _TS // L_SEL
    idiag = st0 // _TS
    groups = range(NSA_KV)
    rows = [slice(g * NSA_HD, (g + 1) * NSA_HD) for g in groups]

    qxs, o_cs, firsts = [], [], []
    for g in groups:
        qz = _qz(q, g, nq)
        qxs.append(jnp.concatenate([qz, qaux_ref[g]], axis=1))
        o_c, p_c = _compressed_branch(qz, kcb_ref[0], vcbt_ref[0, rows[g], :], qpos, slope_ref[g], ncb)
        o_cs.append(o_c)
        imp = p_c[:, 0:nq]
        for r in range(1, NSA_REP):
            imp = imp + p_c[:, r * nq:(r + 1) * nq]
        sel = _select_blocks(imp[0:nsp] + imp[nsp:ncb], qpos[:, 0:nq], nsp)
        sel_ref[g] = jnp.concatenate([jnp.where(sel > 0.0, 0.0, NEG)] * NSA_REP, axis=1)
        blk = lax.broadcasted_iota(jnp.int32, (nsp, 1), 0)
        row_any = jnp.max(sel, axis=1, keepdims=True)
        first = jnp.min(jnp.where(row_any > 0.0, jnp.where(blk >= bpt, blk, nsp * bpt), nsp * bpt),
                        axis=0, keepdims=True)
        firsts.append(first[0, 0])
    lo = jnp.minimum(jnp.minimum(firsts[0], firsts[1]) // bpt, idiag)

    m_ref[...] = jnp.full(m_ref.shape, _M0, F32)
    l_ref[...] = jnp.zeros(l_ref.shape, F32)
    acc_ref[...] = jnp.zeros(acc_ref.shape, F32)

    def sel_tile(kt, diag):
        off = kt * _TS if isinstance(kt, int) else pl.multiple_of(kt * _TS, _TS)
        kx = jnp.concatenate([ks_ref[0, pl.ds(off, _TS), :], _pos_lanes(_TS, kt * (_TS // _KT) - i)], axis=1)
        if diag:
            visible = (off + lax.broadcasted_iota(jnp.int32, (_TS, 1), 0)) <= qpos
        for g in groups:
            s = lax.dot_general(kx, qxs[g], _NT, preferred_element_type=F32)
            s = jnp.concatenate([s[b * L_SEL:(b + 1) * L_SEL] + sel_ref[g, pl.ds(bpt * kt + b, 1), :]
                                 for b in range(bpt)], axis=0)
            if diag:
                s = jnp.where(visible, s, NEG)
            m_old = m_ref[g]
            m_new = jnp.maximum(m_old, jnp.max(s, axis=0, keepdims=True))
            alpha = jnp.exp(m_old - m_new)
            p = jnp.exp(s - m_new)
            l_ref[g] = alpha * l_ref[g] + jnp.sum(p, axis=0, keepdims=True)
            vt = vst_ref[rows[g], pl.ds(off, _TS)]
            acc_ref[g] = alpha * acc_ref[g] + jnp.dot(vt, p.astype(BF16), preferred_element_type=F32)
            m_ref[g] = m_new

    @pl.when(idiag > 0)
    def _():
        sel_tile(0, False)

    def sel_body(kt, carry):
        sel_tile(kt, False)
        return carry

    lax.fori_loop(lo, idiag, sel_body, 0)
    sel_tile(idiag, True)

    wk = WINDOW + nq
    wstart = pl.multiple_of(jnp.maximum(i - WINDOW // nq, 0) * nq, nq)
    kxw = jnp.concatenate([kw_ref[0, pl.ds(wstart, wk), :], _pos_lanes(wk, wstart // _KT - i)], axis=1)
    d_w = qpos - (wstart + lax.broadcasted_iota(jnp.int32, (wk, 1), 0))
    in_window = jnp.where(d_w >= 0, d_w, WINDOW + 1) <= WINDOW
    for g in groups:
        s = jnp.where(in_window, lax.dot_general(kxw, qxs[g], _NT, preferred_element_type=F32), NEG)
        m = jnp.max(s, axis=0, keepdims=True)
        p = jnp.exp(s - m)
        l = jnp.sum(p, axis=0, keepdims=True)
        o_w = jnp.dot(vwt_ref[rows[g], pl.ds(wstart, wk)], p.astype(BF16), preferred_element_type=F32) / l
        o_s = _softmax_done(l_ref, acc_ref, g)
        gc = _sigmoid(glog_ref[0, 0, g:g + 1, :])
        gs = _sigmoid(glog_ref[0, 0, 2 + g:3 + g, :])
        gw = _sigmoid(glog_ref[0, 0, 4 + g:5 + g, :])
        o_t = gc * o_cs[g] + gs * o_s + gw * o_w
        for pp in range(NSA_REP // 2):
            pair = jnp.concatenate([o_t[:, (2 * pp) * nq:(2 * pp + 1) * nq],
                                    o_t[:, (2 * pp + 1) * nq:(2 * pp + 2) * nq]], axis=0)
            slab = g * (NSA_REP // 2) + pp
            o_ref[0, :, slab * NSA_KW:(slab + 1) * NSA_KW] = pair.T


def _nsa_cols(nq):
    c = NSA_REP * nq
    qoff = (jnp.arange(c, dtype=jnp.int32) % nq).reshape(1, c)
    slopes = 2.0 ** (-8.0 * jnp.arange(1, NSA_HEADS + 1, dtype=F32) / NSA_HEADS)
    slope = jnp.repeat(slopes.reshape(NSA_KV, NSA_REP), nq, axis=1).reshape(NSA_KV, 1, c)
    return qoff, slope


def _gate_cols(glog, nq):
    b, t, _ = glog.shape
    x = glog.reshape(b, t // nq, nq, NSA_KV, NSA_REP, 3)
    return x.transpose(0, 1, 5, 3, 4, 2).reshape(b, t // nq, 3 * NSA_KV, NSA_REP * nq)


def _even_odd(x):
    return jnp.concatenate([x[:, 0::2], x[:, 1::2]], axis=1)


def nsa_prompt(q, glog, kcb, vcb, ks, vs_t, kw, vw_t, *, nq=128):
    b, t, _ = q.shape
    ncb = kcb.shape[1]
    c = NSA_REP * nq
    nblk = t // nq
    assert nq == _KT and t % _TS == 0 and t >= WINDOW + nq and ncb == t // L_CMP
    qoff, slope = _nsa_cols(nq)
    lane = jnp.arange(NSA_KW)[None, None, :]
    slope_col = slope.reshape(NSA_KV, c, 1)
    qaux = jnp.where(lane == 0, slope_col * _KT, jnp.where(lane == 1, slope_col, 0.0)).astype(BF16)
    kcb_p = _even_odd(kcb).astype(BF16)
    vcbt = _even_odd(vcb).transpose(0, 2, 1).astype(BF16)
    per_b = lambda shape: pl.BlockSpec((1,) + shape, lambda i, j: (i, 0, 0))
    out = pl.pallas_call(
        functools.partial(_nsa_prompt_kernel, nq=nq, ncb=ncb),
        grid=(b, nblk),
        in_specs=[pl.BlockSpec((1, nq, NSA_W), lambda i, j: (i, j, 0)),
                  pl.BlockSpec((1, 1, 3 * NSA_KV, c), lambda i, j: (i, j, 0, 0)),
                  per_b((ncb, NSA_KW)), per_b((NSA_KW, ncb)),
                  per_b((t, NSA_KW)), pl.BlockSpec((NSA_KW, t), lambda i, j: (0, i)),
                  per_b((t, NSA_KW)), pl.BlockSpec((NSA_KW, t), lambda i, j: (0, i)),
                  pl.BlockSpec((1, c), lambda i, j: (0, 0)),
                  pl.BlockSpec((NSA_KV, 1, c), lambda i, j: (0, 0, 0)),
                  pl.BlockSpec((NSA_KV, c, NSA_KW), lambda i, j: (0, 0, 0))],
        out_specs=pl.BlockSpec((1, nq, NSA_W), lambda i, j: (i, j, 0)),
        out_shape=jax.ShapeDtypeStruct((b, t, NSA_W), F32),
        scratch_shapes=[pltpu.VMEM((NSA_KV, ncb // 2, c), F32),
                        pltpu.VMEM((NSA_KV, 1, c), F32), pltpu.VMEM((NSA_KV, 1, c), F32),
                        pltpu.VMEM((NSA_KV, NSA_HD, c), F32)],
        compiler_params=_params("parallel", "arbitrary"),
        name="nsa_prompt",
    )(q, _gate_cols(glog, nq), kcb_p, vcbt, ks, vs_t, kw, vw_t, qoff, slope, qaux)
    return out


def _compress_pages_kernel(x_ref, pe_ref, w_ref, o_ref):
    for g in range(NSA_KV):
        acc = jnp.zeros(o_ref.shape[1:], F32)
        for d in range(0, NSA_HD, 2):
            y = jnp.concatenate([x_ref[0, :, g, d, :] + pe_ref[d:d + 1, :],
                                 x_ref[0, :, g, d + 1, :] + pe_ref[d + 1:d + 2, :]], axis=1)
            acc = acc + jnp.dot(y.astype(BF16), w_ref[d // 2], preferred_element_type=F32)
        o_ref[g] = acc


def nsa_compress_pages(cache_t, layer, pe, w, *, tp=256):
    n_pool, page = cache_t.shape[1], cache_t.shape[4]
    nblk = page // L_CMP
    tp = _row_tile(n_pool, tp)
    pe_t = jnp.tile(pe.T, (1, nblk))
    eye = jnp.eye(nblk, dtype=F32)
    wd = jnp.einsum('nm,lde->dnlme', eye, w).reshape(NSA_HD // 2, 2 * page, nblk * NSA_HD).astype(BF16)
    out = pl.pallas_call(
        _compress_pages_kernel,
        grid=(n_pool // tp,),
        in_specs=[pl.BlockSpec((1, tp, NSA_KV, NSA_HD, page), lambda i: (layer, i, 0, 0, 0)),
                  pl.BlockSpec((NSA_HD, page), lambda i: (0, 0)),
                  pl.BlockSpec((NSA_HD // 2, 2 * page, nblk * NSA_HD), lambda i: (0, 0, 0))],
        out_specs=pl.BlockSpec((NSA_KV, tp, nblk * NSA_HD), lambda i: (0, i, 0)),
        out_shape=jax.ShapeDtypeStruct((NSA_KV, n_pool, nblk * NSA_HD), F32),
        compiler_params=_params("parallel"),
        name="nsa_compress_pages",
    )(cache_t, pe_t, wd)
    return out.reshape(NSA_KV, n_pool, nblk, NSA_HD).transpose(1, 2, 0, 3).reshape(n_pool, nblk, NSA_KW)


def _row_softmax(s, mask):
    sm = jnp.where(mask, s, NEG)
    m = jnp.max(sm, axis=1, keepdims=True)
    p = jnp.where(mask, jnp.exp(sm - m), 0.0)
    l = jnp.sum(p, axis=1, keepdims=True)
    return p * jnp.where(l > 0.0, 1.0 / l, 0.0)


_DECODE_BB = 4


def _nsa_decode_kernel(pt_ref, q_ref, gl_ref, kcb_ref, vcb_ref, *refs, nq, ncb, npages, page, past, wb, nbb):
    ks_pages = refs[0:nbb * npages]
    vs_pages = refs[nbb * npages:2 * nbb * npages]
    (nks_ref, nvs_ref, nkw_ref, nvw_ref, wk_ref, wv_ref, slope_ref, expand_ref,
     o_ref, wko_ref, wvo_ref) = refs[2 * nbb * npages:]
    del pt_ref
    c = NSA_REP * nq
    nblk_lanes = _KT
    n_sel = -(-(past + nq) // L_SEL)
    row = lax.broadcasted_iota(jnp.int32, (c, 1), 0)
    qpos = past + (row & (nq - 1))
    pad_rows = lambda x: jnp.concatenate([x, jnp.zeros((_KT - nq, x.shape[1]), F32)], axis=0)
    lane = lax.broadcasted_iota(jnp.int32, (1, _KT), 1)
    n_sel_keys = (npages + 1) * page
    key_all = lax.broadcasted_iota(jnp.int32, (1, n_sel_keys), 1)
    expand = expand_ref[...]
    pr = lax.broadcasted_iota(jnp.int32, (ncb, nblk_lanes), 0)
    pc = lax.broadcasted_iota(jnp.int32, (ncb, nblk_lanes), 1)
    half = ncb // 2
    pair = jnp.where(jnp.where(pr < half, pr, pr - half) == pc, 1.0, 0.0).astype(BF16)
    cl = lax.broadcasted_iota(jnp.int32, (1, ncb), 1)
    c_pos = jnp.where(cl < half, 2 * cl, 2 * (cl - half) + 1) * L_CMP + (L_CMP - 1)
    d_c = qpos - c_pos
    d_cf = d_c.astype(F32)
    d_s = qpos - key_all
    d_sf = d_s.astype(F32)
    d_w = qpos - (past - wb + lax.broadcasted_iota(jnp.int32, (1, wb + _KT), 1))
    d_wf = d_w.astype(F32)
    in_window = jnp.where(d_w >= 0, d_w, WINDOW + 1) <= WINDOW
    cur = (past + lax.broadcasted_iota(jnp.int32, (nq, 1), 0)) >> 6
    valid = lane <= cur
    forced = jnp.where(valid, jnp.where(lane == 0, 1.0, jnp.where(lane >= cur - 1, 1.0, 0.0)), 0.0)

    for bb in range(nbb):
        q = q_ref[bb] * (NSA_HD ** -0.5)
        new_kt = pad_rows(nkw_ref[bb]).T
        new_vt = pad_rows(nvw_ref[bb]).T
        for g in range(NSA_KV):
            gs = slice(g * NSA_HD, (g + 1) * NSA_HD)
            qg = jnp.concatenate([q[:, (g * NSA_REP + r) * NSA_HD:(g * NSA_REP + r + 1) * NSA_HD]
                                  for r in range(NSA_REP)], axis=0).astype(BF16)
            slope = slope_ref[g]
            s_c = lax.dot_general(qg, kcb_ref[bb, :, gs], _NT, preferred_element_type=F32)
            p_c = _row_softmax(s_c - slope * d_cf, d_c >= 0)
            o_c = jnp.dot(p_c.astype(BF16), vcb_ref[bb, :, gs], preferred_element_type=F32)
            imp = p_c[0:nq]
            for r in range(1, NSA_REP):
                imp = imp + p_c[r * nq:(r + 1) * nq]
            imp_sel = _dot_exact_rhs(imp, pair)
            score = jnp.where(forced > 0.0, _BIG, jnp.where(valid, imp_sel, -1.0))
            before = jnp.zeros((nq, nblk_lanes), F32)
            for bi in range(n_sel):
                sb = score[:, bi:bi + 1]
                before = before + jnp.where(sb > score, 1.0, jnp.where(sb == score, jnp.where(lane > bi, 1.0, 0.0), 0.0))
            sel = jnp.where(valid, jnp.where(before < N_SEL, 1.0, 0.0), 0.0)
            sel_keys = jnp.dot(sel.astype(BF16), expand, preferred_element_type=F32)
            sel_keys = jnp.concatenate([sel_keys] * NSA_REP, axis=0)

            nk = pad_rows(nks_ref[bb, :, gs]).astype(BF16)
            nv = pad_rows(nvs_ref[bb, :, gs]).astype(BF16)
            s_parts = [jnp.dot(qg, ks_pages[bb * npages + j][0, 0, g].astype(BF16), preferred_element_type=F32)
                       for j in range(npages)]
            s_parts.append(lax.dot_general(qg, nk, _NT, preferred_element_type=F32))
            s_s = jnp.concatenate(s_parts, axis=1)
            p_s = _row_softmax(s_s - slope * d_sf, jnp.where(d_s >= 0, sel_keys, 0.0) > 0.0).astype(BF16)
            o_s = jnp.dot(p_s[:, npages * page:], nv, preferred_element_type=F32)
            for j in range(npages):
                o_s = o_s + lax.dot_general(p_s[:, j * page:(j + 1) * page],
                                            vs_pages[bb * npages + j][0, 0, g].astype(BF16), _NT,
                                            preferred_element_type=F32)

            nkw = pad_rows(nkw_ref[bb, :, gs]).astype(BF16)
            nvw = pad_rows(nvw_ref[bb, :, gs]).astype(BF16)
            s_w = jnp.concatenate([jnp.dot(qg, wk_ref[0, bb, g].astype(BF16), preferred_element_type=F32),
                                   lax.dot_general(qg, nkw, _NT, preferred_element_type=F32)], axis=1)
            p_w = _row_softmax(s_w - slope * d_wf, in_window).astype(BF16)
            o_w = (lax.dot_general(p_w[:, 0:wb], wv_ref[0, bb, g].astype(BF16), _NT, preferred_element_type=F32)
                   + jnp.dot(p_w[:, wb:], nvw, preferred_element_type=F32))

            gate = _sigmoid(gl_ref[bb, g])
            o_ref[bb, g] = gate[:, 0:1] * o_c + gate[:, 1:2] * o_s + gate[:, 2:3] * o_w

            for src_ref, new_full, dst_ref in ((wk_ref, new_kt, wko_ref), (wv_ref, new_vt, wvo_ref)):
                new_t = pltpu.roll(new_full[gs, :], _KT - nq, axis=1)
                shifted = pltpu.roll(src_ref[0, bb, g], wb - nq, axis=1)
                dst_ref[bb, g, :, 0:wb - _KT] = shifted[:, 0:wb - _KT]
                dst_ref[bb, g, :, wb - _KT:wb] = jnp.where(lane >= _KT - nq, new_t, shifted[:, wb - _KT:wb])


def nsa_decode(q, glog, kcb, vcb, pool_k, pool_v, page_table, nks, nvs, nkw, nvw, win_k, win_v, layer):
    b, nq, _ = q.shape
    ncb = kcb.shape[1]
    npages = page_table.shape[1]
    page = pool_k.shape[4]
    past = npages * page
    wb = win_k.shape[4]
    assert page == _KT and wb % _KT == 0 and nq & (nq - 1) == 0 and nq % 8 == 0 and nq <= L_SEL
    assert ncb % 2 == 0 and ncb // 2 <= _KT and -(-(past + nq) // L_SEL) <= _KT
    c = NSA_REP * nq
    _, slope = _nsa_cols(nq)
    gl = glog.reshape(b, nq, NSA_KV, NSA_REP, 3).transpose(0, 2, 3, 1, 4).reshape(b, NSA_KV, c, 3)
    nbb = _DECODE_BB if b % _DECODE_BB == 0 else 1
    per_b = lambda shape: pl.BlockSpec((nbb,) + shape, lambda i, pt: (i,) + (0,) * len(shape))
    const = lambda shape: pl.BlockSpec(shape, lambda i, pt: (0,) * len(shape))
    page_spec = lambda bb, j: pl.BlockSpec((1, 1, NSA_KV, NSA_HD, page),
                                           lambda i, pt: (layer, pt[i * nbb + bb, j], 0, 0, 0))
    page_specs = [page_spec(bb, j) for bb in range(nbb) for j in range(npages)]
    win_spec = pl.BlockSpec((1, nbb, NSA_KV, NSA_HD, wb), lambda i, pt: (layer, i, 0, 0, 0))
    n_keys = (npages + 1) * page
    expand = (jnp.arange(n_keys)[None, :] // L_SEL == jnp.arange(_KT)[:, None]).astype(BF16)
    in_specs = ([per_b((nq, NSA_W)), per_b((NSA_KV, c, 3)), per_b((ncb, NSA_KW)), per_b((ncb, NSA_KW))]
                + page_specs * 2
                + [per_b((nq, NSA_KW))] * 4 + [win_spec] * 2 + [const((NSA_KV, c, 1)), const((_KT, n_keys))])
    out, wko, wvo = pl.pallas_call(
        functools.partial(_nsa_decode_kernel, nq=nq, ncb=ncb, npages=npages, page=page, past=past, wb=wb, nbb=nbb),
        grid_spec=pltpu.PrefetchScalarGridSpec(
            num_scalar_prefetch=1,
            grid=(b // nbb,),
            in_specs=in_specs,
            out_specs=[per_b((NSA_KV, c, NSA_HD)), per_b((NSA_KV, NSA_HD, wb)), per_b((NSA_KV, NSA_HD, wb))],
        ),
        out_shape=[jax.ShapeDtypeStruct((b, NSA_KV, c, NSA_HD), F32),
                   jax.ShapeDtypeStruct((b, NSA_KV, NSA_HD, wb), F32),
                   jax.ShapeDtypeStruct((b, NSA_KV, NSA_HD, wb), F32)],
        compiler_params=_params("arbitrary"),
        name="nsa_decode",
    )(page_table, q, gl, _even_odd(kcb).astype(BF16), _even_odd(vcb).astype(BF16),
      *([pool_k] * (nbb * npages)), *([pool_v] * (nbb * npages)), nks, nvs, nkw, nvw, win_k, win_v,
      slope.reshape(NSA_KV, c, 1), expand)
    o = out.reshape(b, NSA_KV, NSA_REP, nq, NSA_HD).transpose(0, 3, 1, 2, 4).reshape(b, nq, NSA_W)
    return o, wko, wvo


def _xattn_cache_kernel(q_ref, k_ref, v_ref, o_ref, *, nq, nbb):
    scale = XA_HD ** -0.5
    for bb in range(nbb):
        q = jnp.concatenate([q_ref[bb, :, h * XA_HD:(h + 1) * XA_HD] for h in range(XA_HEADS)], axis=0)
        k = k_ref[0, bb].astype(BF16)
        v = v_ref[0, bb].astype(BF16)
        s = lax.dot_general(q.astype(BF16), k, _NT, preferred_element_type=F32) * scale
        col_h = lax.broadcasted_iota(jnp.int32, s.shape, 1) & (XA_HEADS - 1)
        row_h = lax.broadcasted_iota(jnp.int32, s.shape, 0) >> (nq.bit_length() - 1)
        mine = col_h == row_h
        m = jnp.max(jnp.where(mine, s, NEG), axis=1, keepdims=True)
        p = jnp.where(mine, jnp.exp(s - m), 0.0)
        p = p / jnp.sum(p, axis=1, keepdims=True)
        o = jnp.dot(p.astype(BF16), v, preferred_element_type=F32)
        for h in range(XA_HEADS):
            o_ref[bb, :, h * XA_HD:(h + 1) * XA_HD] = o[h * nq:(h + 1) * nq]


def xattn_cache(q, cache_k, cache_v, layer):
    b, nq, w = q.shape
    m = cache_k.shape[2]
    assert XA_HEADS & (XA_HEADS - 1) == 0 and nq % 8 == 0
    kv = lambda a: a.reshape(a.shape[0], b, m * XA_HEADS, XA_HD)
    nbb = _DECODE_BB if b % _DECODE_BB == 0 else 1
    kv_spec = pl.BlockSpec((1, nbb, m * XA_HEADS, XA_HD), lambda i: (layer, i, 0, 0))
    return pl.pallas_call(
        functools.partial(_xattn_cache_kernel, nq=nq, nbb=nbb),
        grid=(b // nbb,),
        in_specs=[pl.BlockSpec((nbb, nq, w), lambda i: (i, 0, 0)), kv_spec, kv_spec],
        out_specs=pl.BlockSpec((nbb, nq, w), lambda i: (i, 0, 0)),
        out_shape=jax.ShapeDtypeStruct((b, nq, w), F32),
        compiler_params=_params("parallel"),
        name="xattn_cache",
    )(q, kv(cache_k), kv(cache_v))


_HALO_M = 8
_TN = (((0,), (0,)), ((), ()))


def _softplus(x):
    return jnp.maximum(x, 0.0) + jnp.log1p(jnp.exp(-jnp.abs(x)))


def _ssd_kernel(xbc_ref, z_ref, sm_ref, dtt_ref, cs_ref, h0_ref, cw_ref, cb_ref, dtb_ref, dtbt_ref,
                al_ref, alt_ref, dsk_ref, ng_ref, y_ref, ncs_ref, hf_ref, ext_ref, h_ref, yh_ref, *, ql, dt_col):
    c = pl.program_id(1)
    nc = pl.num_programs(1)

    @pl.when(c == 0)
    def _():
        ext_ref[...] = jnp.zeros_like(ext_ref)
        ext_ref[_HALO_M - (M_CONV_W - 1):_HALO_M, :] = cs_ref[0]
        h_ref[...] = h0_ref[0]

    @pl.when(c > 0)
    def _():
        ext_ref[0:_HALO_M, :] = ext_ref[ql:ql + _HALO_M, :]

    ext_ref[_HALO_M:_HALO_M + ql, :] = xbc_ref[0]
    acc = jnp.zeros((ql, M_CONV_DIM), F32)
    for k in range(M_CONV_W):
        off = _HALO_M - (M_CONV_W - 1) + k
        acc = acc + ext_ref[off:off + ql, :].astype(BF16).astype(F32) * cw_ref[k:k + 1, :]
    xbc = _silu(acc + cb_ref[...])
    xs = xbc[:, 0:M_DIN]
    bm = xbc[:, M_DIN:M_DIN + M_GROUPS * M_DSTATE]
    cm = xbc[:, M_DIN + M_GROUPS * M_DSTATE:M_CONV_DIM]

    dt = _softplus(sm_ref[0, :, dt_col:dt_col + M_HEADS] + dtb_ref[...])
    dtt = _softplus(dtt_ref[0] + dtbt_ref[...])
    dta = dt * (-jnp.exp(al_ref[...]))
    dtat = dtt * (-jnp.exp(alt_ref[...]))
    ti = lax.broadcasted_iota(jnp.int32, (ql, ql), 0)
    si = lax.broadcasted_iota(jnp.int32, (ql, ql), 1)
    causal = si <= ti
    cum = _dot_exact_lhs(jnp.where(causal, 1.0, 0.0).astype(BF16), dta)
    cumt = _dot_exact_rhs(dtat, jnp.where(ti <= si, 1.0, 0.0).astype(BF16))
    cum_last = cum[ql - 1:ql, :]
    edec = jnp.exp(cum)
    eend = jnp.exp(cum_last - cum)
    elast = jnp.exp(cum_last)

    rep = M_HEADS // M_GROUPS
    for gi in range(M_GROUPS):
        b_g = bm[:, gi * M_DSTATE:(gi + 1) * M_DSTATE]
        c_g = cm[:, gi * M_DSTATE:(gi + 1) * M_DSTATE].astype(BF16)
        cb = lax.dot_general(c_g, b_g.astype(BF16), _NT, preferred_element_type=F32)
        for hh in range(rep):
            h = gi * rep + hh
            hs = slice(h * M_HDIM, (h + 1) * M_HDIM)
            lmat = jnp.where(causal, jnp.exp(cum[:, h:h + 1] - cumt[h:h + 1, :]), 0.0)
            x_h = xs[:, hs]
            xdt = (x_h * dt[:, h:h + 1]).astype(BF16)
            y_diag = jnp.dot((cb * lmat).astype(BF16), xdt, preferred_element_type=F32)
            h_in = h_ref[h]
            y_off = lax.dot_general(c_g, h_in.astype(BF16), _NT, preferred_element_type=F32) * edec[:, h:h + 1]
            bd = (b_g * eend[:, h:h + 1]).astype(BF16)
            s_chunk = lax.dot_general(xdt, bd, _TN, preferred_element_type=F32)
            h_ref[h] = elast[:, h:h + 1] * h_in + s_chunk
            yh_ref[:, hs] = y_diag + y_off + dsk_ref[:, hs] * x_h

    yz = yh_ref[...] * _silu(z_ref[0])
    y_ref[0] = _rms(yz, ng_ref[...])

    @pl.when(c == nc - 1)
    def _():
        ncs_ref[0] = ext_ref[_HALO_M + ql - (M_CONV_W - 1):_HALO_M + ql, :]
        hf_ref[0] = h_ref[...]


def ssd_mixer(xbc, z, small, dt_col, conv_state, h0, conv_w, conv_b, dt_bias, a_log, d_skip, norm_g, *, ql):
    b, t, _ = xbc.shape
    nc = t // ql
    sw = small.shape[2]
    dtt = small[:, :, dt_col:dt_col + M_HEADS].transpose(0, 2, 1)
    const = lambda shape: pl.BlockSpec(shape, lambda i, j: (0,) * len(shape))
    per_b = lambda shape: pl.BlockSpec((1,) + shape, lambda i, j: (i,) + (0,) * len(shape))
    row = lambda x: x.reshape(1, -1)
    colv = lambda x: x.reshape(-1, 1)
    return pl.pallas_call(
        functools.partial(_ssd_kernel, ql=ql, dt_col=dt_col),
        grid=(b, nc),
        in_specs=[pl.BlockSpec((1, ql, M_CONV_DIM), lambda i, j: (i, j, 0)),
                  pl.BlockSpec((1, ql, M_DIN), lambda i, j: (i, j, 0)),
                  pl.BlockSpec((1, ql, sw), lambda i, j: (i, j, 0)),
                  pl.BlockSpec((1, M_HEADS, ql), lambda i, j: (i, 0, j)),
                  per_b((M_CONV_W - 1, M_CONV_DIM)), per_b((M_HEADS, M_HDIM, M_DSTATE)),
                  const((M_CONV_W, M_CONV_DIM)), const((1, M_CONV_DIM)),
                  const((1, M_HEADS)), const((M_HEADS, 1)), const((1, M_HEADS)), const((M_HEADS, 1)),
                  const((1, M_DIN)), const((1, M_DIN))],
        out_specs=[pl.BlockSpec((1, ql, M_DIN), lambda i, j: (i, j, 0)),
                   per_b((M_CONV_W - 1, M_CONV_DIM)), per_b((M_HEADS, M_HDIM, M_DSTATE))],
        out_shape=[jax.ShapeDtypeStruct((b, t, M_DIN), F32),
                   jax.ShapeDtypeStruct((b, M_CONV_W - 1, M_CONV_DIM), F32),
                   jax.ShapeDtypeStruct((b, M_HEADS, M_HDIM, M_DSTATE), F32)],
        scratch_shapes=[pltpu.VMEM((_HALO_M + ql, M_CONV_DIM), F32),
                        pltpu.VMEM((M_HEADS, M_HDIM, M_DSTATE), F32),
                        pltpu.VMEM((ql, M_DIN), F32)],
        compiler_params=_params("parallel", "arbitrary"),
        name="ssd_mixer",
    )(xbc, z, small, dtt, conv_state, h0, conv_w, row(conv_b), row(dt_bias), colv(dt_bias),
      row(a_log), colv(a_log), row(jnp.repeat(d_skip, M_HDIM)), row(norm_g))


_SMALL_W = 128
_OD_SPLITS = (NSA_W,) + (NSA_KW,) * 6 + (M_DIN, M_CONV_DIM, _SMALL_W)


def _odd_w_in(w):
    o_kv = NSA_W
    o_gate = o_kv + 6 * NSA_KW
    o_z = o_gate + 3 * NSA_HEADS
    o_xbc = o_z + M_DIN
    o_dt = o_xbc + M_CONV_DIM
    pad = jnp.zeros((w.shape[0], _SMALL_W - 3 * NSA_HEADS - M_HEADS), F32)
    return jnp.concatenate([w[:, :o_gate], w[:, o_z:o_xbc], w[:, o_xbc:o_dt],
                            w[:, o_gate:o_z], w[:, o_dt:], pad], axis=1)


def kernel(x_prompt, x_sample, state_conv_a, state_conv_b, cache_cmp_k, cache_cmp_v, cache_sel_k, cache_sel_v, cache_win_k, cache_win_v, state_ssm, state_ssm_conv, cache_mem_k, cache_mem_v, page_table, mem_prompt, norm_mix, norm_xattn, norm_ffn, norm_final, ev_w_in, ev_conv_a, ev_conv_b, ev_conv_b_bias, ev_ln_g, ev_ln_b, ev_w_out, od_w_in, od_cmp_pe, od_cmp_wk, od_cmp_wv, od_ssm_conv_w, od_ssm_conv_b, od_dt_bias, od_a_log, od_d_skip, od_ssm_norm, od_w_out, xa_wq, xa_wk, xa_wv, xa_wo, moe_wg, moe_bg, moe_we, moe_be, moe_w1, moe_w3, moe_w2):
    bp, tp, d = x_prompt.shape
    bs, ts, _ = x_sample.shape
    n_p, n_s = bp * tp, bs * ts
    n_mem = mem_prompt.shape[1]
    depth = norm_mix.shape[0]
    n_pool, page = cache_cmp_k.shape[1:3]
    wb = cache_win_k.shape[2]
    dt_col = 3 * NSA_HEADS

    def groups(pair):
        return pair[0].reshape(bp, tp, -1), pair[1].reshape(bs, ts, -1)

    def rows(a_p, a_s):
        return [a_p.reshape(n_p, a_p.shape[-1]), a_s.reshape(n_s, a_s.shape[-1])]

    hs = rows(x_prompt, x_sample)
    out = {k: [] for k in ("ca_p", "ca_s", "cb_p", "cb_s", "wk_p", "wk_s", "wv_p", "wv_s",
                           "sm_p", "sm_s", "sc_p", "sc_s", "mk_p", "mv_p")}
    rows_p = [[], [], [], []]
    rows_s = [[], [], [], []]
    for i in range(depth):
        j = i // 2
        if i % 2 == 0:
            u_p, u_s = groups(g_norm_matmul(hs, norm_mix[i], ev_w_in[j]))
            ev = (ev_conv_a[j], ev_conv_b[j], ev_conv_b_bias[j], ev_ln_g[j], ev_ln_b[j])
            y_p, na_p, nb_p = even_conv(u_p, jnp.zeros((bp, CONV_A_W - 1, D_A), F32),
                                        jnp.zeros((bp, CONV_B_W - 1, D_B), F32), *ev)
            y_s, na_s, nb_s = even_conv(u_s, state_conv_a[j], state_conv_b[j], *ev)
            hs = g_matmul_res([rows(y_p, y_s)], [ev_w_out[j]], hs)
            out["ca_p"].append(na_p)
            out["ca_s"].append(na_s)
            out["cb_p"].append(nb_p)
            out["cb_s"].append(nb_s)
        else:
            w_in = _odd_w_in(od_w_in[j])
            c_ks, c_kw = NSA_W + 2 * NSA_KW, NSA_W + 4 * NSA_KW
            u = g_odd_in_proj(hs, norm_mix[i], w_in, w_in[:, NSA_W:NSA_W + 6 * NSA_KW].T, (c_ks, c_kw))
            n_main = len(_OD_SPLITS)
            ks_b, kw_b, vs_t, vw_t = (u[n_main + k][0] for k in range(4))
            kvt_p = [u[n_main + 4 + k][0].reshape(NSA_KV, NSA_HD, bp, tp).transpose(2, 3, 0, 1) for k in range(6)]
            q_p, q_s = groups(u[0])
            kv = [groups(u[1 + k]) for k in range(6)]
            kvp = [a for a, _ in kv]
            kvs = [b for _, b in kv]
            z_p, z_s = groups(u[7])
            xbc_p, xbc_s = groups(u[8])
            sm_p, sm_s = groups(u[9])
            pe, wck, wcv = od_cmp_pe[j], od_cmp_wk[j], od_cmp_wv[j]
            mw = (od_ssm_conv_w[j], od_ssm_conv_b[j], od_dt_bias[j], od_a_log[j], od_d_skip[j], od_ssm_norm[j])
            blocks = lambda a: a.reshape(-1, L_CMP, NSA_KW)
            ncb = tp // L_CMP
            kcb_p = nsa_compress(blocks(kvp[0][:, :ncb * L_CMP]), pe, wck).reshape(bp, ncb, NSA_KW)
            vcb_p = nsa_compress(blocks(kvp[1][:, :ncb * L_CMP]), pe, wcv).reshape(bp, ncb, NSA_KW)
            o_p = nsa_prompt(q_p, sm_p[:, :, :dt_col], kcb_p, vcb_p, ks_b.reshape(bp, tp, NSA_KW), vs_t,
                             kw_b.reshape(bp, tp, NSA_KW), vw_t)
            keep = min(WINDOW, tp)
            y_p, nsc_p, nsm_p = ssd_mixer(xbc_p, z_p, sm_p, dt_col, jnp.zeros((bp, M_CONV_W - 1, M_CONV_DIM), F32),
                                          jnp.zeros((bp, M_HEADS, M_HDIM, M_DSTATE), F32), *mw, ql=128)
            tokens_last = lambda a: jnp.transpose(a, (0, 1, 3, 4, 2))
            kcp = nsa_compress_pages(tokens_last(cache_cmp_k), j, pe, wck)
            vcp = nsa_compress_pages(tokens_last(cache_cmp_v), j, pe, wcv)
            kcb_s = kcp[page_table].reshape(bs, -1, NSA_KW)
            vcb_s = vcp[page_table].reshape(bs, -1, NSA_KW)
            o_s, nwk_s, nwv_s = nsa_decode(
                q_s, sm_s[:, :, :dt_col], kcb_s, vcb_s, tokens_last(cache_sel_k), tokens_last(cache_sel_v),
                page_table, kvs[2], kvs[3], kvs[4], kvs[5], tokens_last(cache_win_k), tokens_last(cache_win_v), j)
            nwk_s = jnp.transpose(nwk_s, (0, 3, 1, 2))
            nwv_s = jnp.transpose(nwv_s, (0, 3, 1, 2))
            y_s, nsc_s, nsm_s = ssd_mixer(xbc_s, z_s, sm_s, dt_col, state_ssm_conv[j], state_ssm[j], *mw, ql=ts)
            w_out = od_w_out[j]
            hs = g_matmul_res([rows(o_p, o_s), rows(y_p, y_s)], [w_out[:NSA_W], w_out[NSA_W:]], hs)
            heads = lambda a: a.reshape(a.shape[0], a.shape[1], NSA_KV, NSA_HD)
            for k in range(4):
                rows_p[k].append(kvt_p[k])
                rows_s[k].append(heads(kvs[k]))
            out["wk_p"].append(kvt_p[4][:, tp - keep:])
            out["wv_p"].append(kvt_p[5][:, tp - keep:])
            out["wk_s"].append(nwk_s)
            out["wv_s"].append(nwv_s)
            out["sc_p"].append(nsc_p)
            out["sc_s"].append(nsc_s)
            out["sm_p"].append(nsm_p)
            out["sm_s"].append(nsm_s)
        mk, mv = norm_matmul(mem_prompt.reshape(bp * n_mem, d), None,
                             jnp.concatenate([xa_wk[i], xa_wv[i]], axis=1), norm=False,
                             splits=(XA_HEADS * XA_HD, XA_HEADS * XA_HD))
        mk = mk.reshape(bp, n_mem, XA_HEADS * XA_HD)
        mv = mv.reshape(bp, n_mem, XA_HEADS * XA_HD)
        out["mk_p"].append(mk.reshape(bp, n_mem, XA_HEADS, XA_HD))
        out["mv_p"].append(mv.reshape(bp, n_mem, XA_HEADS, XA_HD))
        qx_p, qx_s = groups(g_norm_matmul(hs, norm_xattn[i], xa_wq[i]))
        ox_p = xattn(qx_p, mk, mv)
        ox_s = xattn_cache(qx_s, cache_mem_k, cache_mem_v, i)
        hs = g_matmul_res([rows(ox_p, ox_s)], [xa_wo[i]], hs)
        hs = moe_layer(hs, norm_ffn[i], moe_wg[i], moe_bg[i], moe_we[i], moe_be[i], moe_w1, moe_w3, moe_w2, i)
    y_prompt, y_sample = groups(g_rmsnorm(hs, norm_final))
    st = lambda k: jnp.stack(out[k])
    return (y_prompt, y_sample, st("ca_p"), st("ca_s"), st("cb_p"), st("cb_s"),
            jnp.stack(rows_p[0]), jnp.stack(rows_s[0]), jnp.stack(rows_p[1]), jnp.stack(rows_s[1]),
            jnp.stack(rows_p[2]), jnp.stack(rows_s[2]), jnp.stack(rows_p[3]), jnp.stack(rows_s[3]),
            st("wk_p"), st("wk_s"), st("wv_p"), st("wv_s"), st("sm_p"), st("sm_s"), st("sc_p"), st("sc_s"),
            st("mk_p"), st("mv_p"))
```

```python
import functools

import jax
import jax.numpy as jnp
from jax import lax
from jax.experimental import pallas as pl
from jax.experimental.pallas import tpu as pltpu

F32 = jnp.float32
BF16 = jnp.bfloat16
EPS = 1e-6
NEG = -1e30
VMEM_LIMIT = 56 * 1024 * 1024

D_A = 512
D_B = 512
CONV_A_W = 3
CONV_B_W = 31
NSA_HEADS = 8
NSA_HD = 64
NSA_KV = 2
NSA_REP = NSA_HEADS // NSA_KV
NSA_W = NSA_HEADS * NSA_HD
NSA_KW = NSA_KV * NSA_HD
L_CMP = 32
L_SEL = 64
N_SEL = 16
WINDOW = 512
M_DIN = 512
M_HDIM = 64
M_HEADS = 8
M_DSTATE = 64
M_GROUPS = 2
M_CONV_W = 4
M_CONV_DIM = M_DIN + 2 * M_GROUPS * M_DSTATE
XA_HEADS = 4
XA_HD = 128
MOE_GROUPS = 4
MOE_EPG = 8
MOE_E = 32
MOE_TOPK = 2


def _params(*sem):
    return pltpu.CompilerParams(dimension_semantics=sem, vmem_limit_bytes=VMEM_LIMIT)


def _row_tile(n, pref):
    t = min(n, pref)
    while n % t or (t % 8 and t != n):
        t -= 1
    return t


def _split3(a):
    hi = a.astype(BF16)
    r1 = a - hi.astype(F32)
    mid = r1.astype(BF16)
    lo = (r1 - mid.astype(F32)).astype(BF16)
    return hi, mid, lo


def _dot_exact_rhs(a, b_bf16):
    hi, mid, lo = _split3(a)
    d = lambda x: jnp.dot(x, b_bf16, preferred_element_type=F32)
    return d(hi) + d(mid) + d(lo)


def _dot_exact_lhs(a_bf16, b):
    hi, mid, lo = _split3(b)
    d = lambda x: jnp.dot(a_bf16, x, preferred_element_type=F32)
    return d(hi) + d(mid) + d(lo)


def _rms(x, g):
    ms = jnp.mean(x * x, axis=-1, keepdims=True)
    return x * lax.rsqrt(ms + EPS) * g


def _sigmoid(x):
    return 1.0 / (1.0 + jnp.exp(-x))


def _silu(x):
    return x * _sigmoid(x)


def _norm_matmul_kernel(x_ref, g_ref, w_ref, *o_refs, norm, splits):
    x = x_ref[...]
    if norm:
        x = _rms(x, g_ref[...])
    res = jnp.dot(x.astype(BF16), w_ref[...].astype(BF16), preferred_element_type=F32)
    off = 0
    for o_ref, width in zip(o_refs, splits):
        o_ref[...] = res[:, off:off + width]
        off += width


def norm_matmul(x, g, w, *, norm=True, splits=None, tm=512):
    n, k = x.shape
    m = w.shape[1]
    tm = _row_tile(n, tm)
    if g is None:
        g = jnp.ones((k,), F32)
    widths = (m,) if splits is None else tuple(splits)
    assert sum(widths) == m
    outs = pl.pallas_call(
        functools.partial(_norm_matmul_kernel, norm=norm, splits=widths),
        grid=(n // tm,),
        in_specs=[pl.BlockSpec((tm, k), lambda i: (i, 0)),
                  pl.BlockSpec((1, k), lambda i: (0, 0)),
                  pl.BlockSpec((k, m), lambda i: (0, 0))],
        out_specs=[pl.BlockSpec((tm, wd), lambda i: (i, 0)) for wd in widths],
        out_shape=[jax.ShapeDtypeStruct((n, wd), F32) for wd in widths],
        compiler_params=_params("parallel"),
        name="norm_matmul",
    )(x, g.reshape(1, k), w)
    return outs[0] if splits is None else outs


_TM = 512


def _rowwise_call(body, row_inputs, shared, out_widths, out_dtypes, name, joint_outputs=False, transposed=()):
    ns = [a.shape[0] for a in row_inputs[0]]
    assert all(n % _TM == 0 for n in ns)
    nbs = [n // _TM for n in ns]
    starts = [sum(nbs[:g]) for g in range(len(ns))]
    n_groups, n_row, n_out = len(ns), len(row_inputs), len(out_widths)

    def group_map(g):
        return lambda i: (jnp.clip(i - starts[g], 0, nbs[g] - 1), 0)

    in_specs, args = [], []
    for k in range(n_row):
        for g in range(n_groups):
            a = row_inputs[k][g]
            in_specs.append(pl.BlockSpec((_TM, a.shape[1]), group_map(g)))
            args.append(a)
    for a in shared:
        in_specs.append(pl.BlockSpec(a.shape, lambda i, nd=a.ndim: (0,) * nd))
        args.append(a)
    if joint_outputs:
        out_specs = [pl.BlockSpec((_TM, w), lambda i: (i, 0)) for w in out_widths]
        out_shape = [jax.ShapeDtypeStruct((sum(ns), w), dt) for w, dt in zip(out_widths, out_dtypes)]
    else:
        def group_map_t(g):
            return lambda i: (0, jnp.clip(i - starts[g], 0, nbs[g] - 1))

        out_specs, out_shape = [], []
        for j, (w, dt) in enumerate(zip(out_widths, out_dtypes)):
            for g in range(n_groups):
                if j in transposed:
                    out_specs.append(pl.BlockSpec((w, _TM), group_map_t(g)))
                    out_shape.append(jax.ShapeDtypeStruct((w, ns[g]), dt))
                else:
                    out_specs.append(pl.BlockSpec((_TM, w), group_map(g)))
                    out_shape.append(jax.ShapeDtypeStruct((ns[g], w), dt))

    def kernel(*refs):
        x_refs = refs[:n_row * n_groups]
        s_refs = refs[n_row * n_groups:n_row * n_groups + len(shared)]
        o_refs = refs[n_row * n_groups + len(shared):]
        i = pl.program_id(0)
        for g in range(n_groups):
            @pl.when((i >= starts[g]) & (i < starts[g] + nbs[g]))
            def _(g=g):
                vals = body([x_refs[k * n_groups + g][...] for k in range(n_row)], s_refs)
                for j, v in enumerate(vals):
                    o_ref = o_refs[j] if joint_outputs else o_refs[j * n_groups + g]
                    o_ref[...] = v.astype(o_ref.dtype)

    outs = pl.pallas_call(
        kernel, grid=(sum(nbs),), in_specs=in_specs, out_specs=out_specs, out_shape=out_shape,
        compiler_params=_params("arbitrary"), name=name)(*args)
    if joint_outputs:
        return list(outs)
    return [list(outs[j * n_groups:(j + 1) * n_groups]) for j in range(n_out)]


def g_norm_matmul(hs, g, w, *, splits=None, norm=True):
    widths = (w.shape[1],) if splits is None else tuple(splits)
    assert sum(widths) == w.shape[1]

    def body(xs, s_refs):
        x = _rms(xs[0], s_refs[0][...]) if norm else xs[0]
        res = jnp.dot(x.astype(BF16), s_refs[1][...].astype(BF16), preferred_element_type=F32)
        offs = [sum(widths[:j]) for j in range(len(widths))]
        return [res[:, o:o + wd] for o, wd in zip(offs, widths)]

    k = hs[0].shape[1]
    gv = jnp.ones((1, k), F32) if g is None else g.reshape(1, k)
    outs = _rowwise_call(body, [hs], [gv, w], widths, [F32] * len(widths), "norm_matmul")
    return outs[0] if splits is None else outs


def g_odd_in_proj(hs, g, w, w_kvt, k_cols):
    widths = _OD_SPLITS + (NSA_KW,) * 4 + (NSA_KW,) * 6
    offs = [sum(_OD_SPLITS[:j]) for j in range(len(_OD_SPLITS))]

    def body(xs, s_refs):
        x = _rms(xs[0], s_refs[0][...]).astype(BF16)
        res = jnp.dot(x, s_refs[1][...].astype(BF16), preferred_element_type=F32)
        kvt = lax.dot_general(s_refs[2][...].astype(BF16), x, _NT, preferred_element_type=F32)
        part = lambda k: kvt[k * NSA_KW:(k + 1) * NSA_KW]
        outs = [res[:, o:o + wd] for o, wd in zip(offs, _OD_SPLITS)]
        outs += [res[:, c:c + NSA_KW] for c in k_cols]
        outs += [part(3), part(5)]
        outs += [part(k) for k in range(6)]
        return outs

    n_main = len(_OD_SPLITS)
    dts = [F32] * n_main + [BF16] * 4 + [F32] * 6
    return _rowwise_call(body, [hs], [g.reshape(1, -1), w, w_kvt], widths, dts, "odd_in_proj",
                         transposed=(n_main + 2, n_main + 3) + tuple(range(n_main + 4, n_main + 10)))


def g_matmul_res(xs_list, ws, hs):
    def body(xs, s_refs):
        acc = xs[-1]
        for j in range(len(ws)):
            acc = acc + jnp.dot(xs[j].astype(BF16), s_refs[j][...].astype(BF16), preferred_element_type=F32)
        return [acc]

    return _rowwise_call(body, list(xs_list) + [hs], list(ws), (hs[0].shape[1],), [F32], "matmul_res")[0]


def g_rmsnorm(hs, g):
    body = lambda xs, s_refs: [_rms(xs[0], s_refs[0][...])]
    return _rowwise_call(body, [hs], [g.reshape(1, -1)], (hs[0].shape[1],), [F32], "rmsnorm_rows")[0]


def g_moe_router(hs, g, w_router):
    def body(xs, s_refs):
        xb = _rms(xs[0], s_refs[0][...]).astype(BF16)
        logits = jnp.dot(xb, s_refs[1][...].astype(BF16), preferred_element_type=F32)
        half = xb.shape[1] // 2
        bits = lambda v: lax.bitcast_convert_type(v.astype(F32), jnp.uint32)
        words = (bits(xb[:, half:]) & jnp.uint32(0xFFFF0000)) | (bits(xb[:, :half]) >> 16)
        return [logits, lax.bitcast_convert_type(words, F32)]

    k = hs[0].shape[1]
    return _rowwise_call(body, [hs], [g.reshape(1, k), w_router], (_ROUTER_W, k // 2), [F32, F32], "moe_router",
                         joint_outputs=True)


_HALO_A = 8
_HALO_B = 32


def _even_conv_kernel(u_ref, sa_ref, sb_ref, wa_ref, wb_ref, bb_ref, lg_ref, lb_ref,
                      y_ref, na_ref, nb_ref, ea_ref, eb_ref, ear_ref, ebr_ref, sh_ref, *, tt):
    rnd = lambda x: x.astype(BF16).astype(F32)
    t = pl.program_id(1)
    nt = pl.num_programs(1)

    @pl.when(t == 0)
    def _():
        ea_ref[...] = jnp.zeros_like(ea_ref)
        eb_ref[...] = jnp.zeros_like(eb_ref)
        ea_ref[_HALO_A - (CONV_A_W - 1):_HALO_A, :] = sa_ref[0]
        eb_ref[_HALO_B - (CONV_B_W - 1):_HALO_B, :] = sb_ref[0]
        ear_ref[...] = rnd(ea_ref[...])
        ebr_ref[...] = rnd(eb_ref[...])

    @pl.when(t > 0)
    def _():
        ea_ref[0:_HALO_A, :] = ea_ref[tt:tt + _HALO_A, :]
        eb_ref[0:_HALO_B, :] = eb_ref[tt:tt + _HALO_B, :]
        ear_ref[0:_HALO_A, :] = ear_ref[tt:tt + _HALO_A, :]
        ebr_ref[0:_HALO_B, :] = ebr_ref[tt:tt + _HALO_B, :]

    xa = u_ref[0, :, 0:D_A]
    ba = u_ref[0, :, D_A:2 * D_A]
    ca = u_ref[0, :, 2 * D_A:3 * D_A]
    pb = u_ref[0, :, 3 * D_A:3 * D_A + D_B]
    gb = u_ref[0, :, 3 * D_A + D_B:3 * D_A + 2 * D_B]
    va = ca * xa
    vb = pb * _sigmoid(gb)
    ea_ref[_HALO_A:_HALO_A + tt, :] = va
    eb_ref[_HALO_B:_HALO_B + tt, :] = vb
    ear_ref[_HALO_A:_HALO_A + tt, :] = rnd(va)
    ebr_ref[_HALO_B:_HALO_B + tt, :] = rnd(vb)

    acc = jnp.zeros((tt, D_A), F32)
    for k in range(CONV_A_W):
        off = _HALO_A - (CONV_A_W - 1) + k
        acc = acc + ear_ref[off:off + tt, :] * wa_ref[k:k + 1, :]
    y_ref[0, :, 0:D_A] = ba * acc

    span = tt + _HALO_B - 8
    for r in range(1, 8):
        sh_ref[r - 1, 0:span, :] = ebr_ref[r:r + span, :]
    acc = jnp.zeros((tt, D_B), F32)
    for k in range(CONV_B_W):
        off = _HALO_B - (CONV_B_W - 1) + k
        base = off - off % 8
        rows = ebr_ref[base:base + tt, :] if off % 8 == 0 else sh_ref[off % 8 - 1, base:base + tt, :]
        acc = acc + rows * wb_ref[k:k + 1, :]
    acc = acc + bb_ref[...]
    mu = jnp.mean(acc, axis=-1, keepdims=True)
    xc = acc - mu
    var = jnp.mean(xc * xc, axis=-1, keepdims=True)
    yb = xc * lax.rsqrt(var + EPS) * lg_ref[...] + lb_ref[...]
    y_ref[0, :, D_A:D_A + D_B] = _silu(yb)

    @pl.when(t == nt - 1)
    def _():
        na_ref[0] = ea_ref[_HALO_A + tt - (CONV_A_W - 1):_HALO_A + tt, :]
        nb_ref[0] = eb_ref[_HALO_B + tt - (CONV_B_W - 1):_HALO_B + tt, :]


def even_conv(u, sa, sb, wa, wb, bb, lg, lb, *, tt=256):
    b, t, w = u.shape
    tt = _row_tile(t, tt)
    full = lambda shape: pl.BlockSpec(shape, lambda i, j: (0,) * len(shape))
    return pl.pallas_call(
        functools.partial(_even_conv_kernel, tt=tt),
        grid=(b, t // tt),
        in_specs=[pl.BlockSpec((1, tt, w), lambda i, j: (i, j, 0)),
                  pl.BlockSpec((1, CONV_A_W - 1, D_A), lambda i, j: (i, 0, 0)),
                  pl.BlockSpec((1, CONV_B_W - 1, D_B), lambda i, j: (i, 0, 0)),
                  full((CONV_A_W, D_A)), full((CONV_B_W, D_B)), full((1, D_B)),
                  full((1, D_B)), full((1, D_B))],
        out_specs=[pl.BlockSpec((1, tt, D_A + D_B), lambda i, j: (i, j, 0)),
                   pl.BlockSpec((1, CONV_A_W - 1, D_A), lambda i, j: (i, 0, 0)),
                   pl.BlockSpec((1, CONV_B_W - 1, D_B), lambda i, j: (i, 0, 0))],
        out_shape=[jax.ShapeDtypeStruct((b, t, D_A + D_B), F32),
                   jax.ShapeDtypeStruct((b, CONV_A_W - 1, D_A), F32),
                   jax.ShapeDtypeStruct((b, CONV_B_W - 1, D_B), F32)],
        scratch_shapes=[pltpu.VMEM((_HALO_A + tt, D_A), F32), pltpu.VMEM((_HALO_B + tt, D_B), F32),
                        pltpu.VMEM((_HALO_A + tt, D_A), F32), pltpu.VMEM((_HALO_B + tt, D_B), F32),
                        pltpu.VMEM((7, _HALO_B + tt - 8, D_B), F32)],
        compiler_params=_params("parallel", "arbitrary"),
        name="even_conv",
    )(u, sa, sb, wa, wb, bb.reshape(1, D_B), lg.reshape(1, D_B), lb.reshape(1, D_B))


def _xattn_kernel(q_ref, k_ref, v_ref, o_ref):
    scale = XA_HD ** -0.5
    for h in range(XA_HEADS):
        sl = slice(h * XA_HD, (h + 1) * XA_HD)
        q = q_ref[0, :, sl].astype(BF16)
        k = k_ref[0, :, sl].astype(BF16)
        v = v_ref[0, :, sl].astype(BF16)
        s = lax.dot_general(q, k, (((1,), (1,)), ((), ())), preferred_element_type=F32) * scale
        m = jnp.max(s, axis=-1, keepdims=True)
        p = jnp.exp(s - m)
        p = p / jnp.sum(p, axis=-1, keepdims=True)
        o_ref[0, :, sl] = jnp.dot(p.astype(BF16), v, preferred_element_type=F32)


def xattn(q, k, v, *, tq=512):
    b, t, w = q.shape
    tq = _row_tile(t, tq)
    m = k.shape[1]
    kv_spec = pl.BlockSpec((1, m, w), lambda i, j: (i, 0, 0))
    return pl.pallas_call(
        _xattn_kernel,
        grid=(b, t // tq),
        in_specs=[pl.BlockSpec((1, tq, w), lambda i, j: (i, j, 0)), kv_spec, kv_spec],
        out_specs=pl.BlockSpec((1, tq, w), lambda i, j: (i, j, 0)),
        out_shape=jax.ShapeDtypeStruct((b, t, w), F32),
        compiler_params=_params("parallel", "parallel"),
        name="xattn",
    )(q, k, v)


MOE_BLK = 256
_ROUTER_W = 128


def _expert_gather_kernel(tok_ref, be_ref, act_ref, xn_hbm, gate_ref, w1_ref, w3_ref, w2_ref, o_ref,
                          xbuf_ref, sem_ref, w1b_ref, w3b_ref, w2b_ref):
    i = pl.program_id(0)
    nb = pl.num_programs(0)

    def row_copy(blk, slot, r):
        tok = tok_ref[blk * MOE_BLK + r]
        return pltpu.make_async_copy(xn_hbm.at[pl.ds(tok, 1)], xbuf_ref.at[slot, pl.ds(r, 1)], sem_ref.at[slot])

    def start_gather(blk, slot):
        for r in range(MOE_BLK):
            row_copy(blk, slot, r).start()

    def wait_gather(slot):
        pltpu.make_async_copy(xbuf_ref.at[slot], xbuf_ref.at[slot], sem_ref.at[slot]).wait()

    slot = lax.rem(i, 2)
    nxt = jnp.minimum(i + 1, nb - 1)
    prev_act = act_ref[jnp.maximum(i - 1, 0)]

    @pl.when((i == 0) & (act_ref[0] > 0))
    def _():
        start_gather(0, 0)

    prev = be_ref[jnp.maximum(i - 1, 0)]

    @pl.when((act_ref[i] > 0) & ((i == 0) | (be_ref[i] != prev)))
    def _():
        w1b_ref[...] = w1_ref[0, 0].astype(BF16)
        w3b_ref[...] = w3_ref[0, 0].astype(BF16)
        w2b_ref[...] = w2_ref[0, 0].astype(BF16)

    for s in range(2):
        @pl.when((act_ref[i] > 0) & (slot == s))
        def _(s=s):
            wait_gather(s)
            words = lax.bitcast_convert_type(xbuf_ref[s], jnp.uint32)
            start_gather(nxt, 1 - s)
            unpack = lambda v: lax.bitcast_convert_type(v, F32).astype(BF16)
            x = jnp.concatenate([unpack(words << 16), unpack(words & jnp.uint32(0xFFFF0000))], axis=1)
            h1 = jnp.dot(x, w1b_ref[...], preferred_element_type=F32)
            h3 = jnp.dot(x, w3b_ref[...], preferred_element_type=F32)
            hid = (_silu(h1) * h3).astype(BF16)
            out = jnp.dot(hid, w2b_ref[...], preferred_element_type=F32)
            o_ref[...] = out * gate_ref[...]

    @pl.when((act_ref[i] > 0) & (i == nb - 1))
    def _():
        wait_gather(1 - slot)

    @pl.when(act_ref[i] == 0)
    def _():
        o_ref[...] = jnp.zeros_like(o_ref)

    @pl.when((act_ref[i] == 0) & (i > 0) & (prev_act > 0))
    def _():
        wait_gather(slot)


def moe_experts_gather(xn, buf_tok, gate, blk_exp, blk_act, w1, w3, w2, layer):
    rows = buf_tok.shape[0]
    d = w1.shape[2]
    nb = rows // MOE_BLK
    ff = w1.shape[3]
    return pl.pallas_call(
        _expert_gather_kernel,
        grid_spec=pltpu.PrefetchScalarGridSpec(
            num_scalar_prefetch=3,
            grid=(nb,),
            in_specs=[pl.BlockSpec(memory_space=pl.ANY),
                      pl.BlockSpec((MOE_BLK, 1), lambda i, tok, be, act: (i, 0)),
                      pl.BlockSpec((1, 1, d, ff), lambda i, tok, be, act: (layer, be[i], 0, 0)),
                      pl.BlockSpec((1, 1, d, ff), lambda i, tok, be, act: (layer, be[i], 0, 0)),
                      pl.BlockSpec((1, 1, ff, d), lambda i, tok, be, act: (layer, be[i], 0, 0))],
            out_specs=pl.BlockSpec((MOE_BLK, d), lambda i, tok, be, act: (i, 0)),
            scratch_shapes=[pltpu.VMEM((2, MOE_BLK, d // 2), F32), pltpu.SemaphoreType.DMA((2,)),
                            pltpu.VMEM((d, ff), BF16), pltpu.VMEM((d, ff), BF16), pltpu.VMEM((ff, d), BF16)],
        ),
        out_shape=jax.ShapeDtypeStruct((rows, d), F32),
        compiler_params=_params("arbitrary"),
        name="moe_experts",
    )(buf_tok, blk_exp, blk_act, xn, gate, w1, w3, w2)


_COMBINE_TM = 256


def _combine_kernel(dest_ref, *refs, starts, nbs):
    n_groups = len(starts)
    h_refs = refs[:n_groups]
    out_hbm = refs[n_groups]
    o_refs = refs[n_groups + 1:2 * n_groups + 1]
    buf_ref, sem_ref = refs[2 * n_groups + 1:]
    i = pl.program_id(0)
    nsteps = pl.num_programs(0)
    tm = _COMBINE_TM

    def start_gather(step, slot):
        for s in range(2):
            @pl.when(slot == s)
            def _(s=s):
                for r in range(tm):
                    for k in range(MOE_TOPK):
                        row = dest_ref[(step * tm + r) * MOE_TOPK + k]
                        pltpu.make_async_copy(out_hbm.at[pl.ds(row, 1)], buf_ref.at[s, k, pl.ds(r, 1)],
                                              sem_ref.at[s]).start()

    slot = lax.rem(i, 2)

    @pl.when(i == 0)
    def _():
        start_gather(i, slot)

    @pl.when(i + 1 < nsteps)
    def _():
        start_gather(i + 1, 1 - slot)

    pltpu.make_async_copy(buf_ref.at[slot], buf_ref.at[slot], sem_ref.at[slot]).wait()
    y = buf_ref[slot, 0] + buf_ref[slot, 1]
    for g in range(n_groups):
        @pl.when((i >= starts[g]) & (i < starts[g] + nbs[g]))
        def _(g=g):
            o_refs[g][...] = h_refs[g][...] + y


def moe_combine(hs, out, dest):
    tm = _COMBINE_TM
    ns = [h.shape[0] for h in hs]
    d = hs[0].shape[1]
    assert all(n % tm == 0 for n in ns)
    nbs = [n // tm for n in ns]
    starts = [sum(nbs[:g]) for g in range(len(ns))]

    def group_map(g):
        return lambda i, dest: (jnp.clip(i - starts[g], 0, nbs[g] - 1), 0)

    specs = [pl.BlockSpec((tm, d), group_map(g)) for g in range(len(ns))]
    return pl.pallas_call(
        functools.partial(_combine_kernel, starts=starts, nbs=nbs),
        grid_spec=pltpu.PrefetchScalarGridSpec(
            num_scalar_prefetch=1,
            grid=(sum(nbs),),
            in_specs=specs + [pl.BlockSpec(memory_space=pl.ANY)],
            out_specs=specs,
            scratch_shapes=[pltpu.VMEM((2, MOE_TOPK, tm, d), F32), pltpu.SemaphoreType.DMA((2,))],
        ),
        out_shape=[jax.ShapeDtypeStruct((n, d), F32) for n in ns],
        compiler_params=_params("arbitrary"),
        name="moe_combine",
    )(dest.reshape(-1), *hs, out)


def moe_layer(hs, g, wg, bg, we, be, w1, w3, w2, layer):
    d = hs[0].shape[1]
    n = sum(h.shape[0] for h in hs)
    w_router = jnp.concatenate([wg, we, jnp.zeros((d, _ROUTER_W - MOE_GROUPS - MOE_E), F32)], axis=1)
    logits, xn = g_moe_router(hs, g, w_router)
    lg = logits[:, :MOE_GROUPS] + bg
    grp = jnp.argmax(lg, axis=-1)
    gw = jnp.take_along_axis(jax.nn.softmax(lg, axis=-1), grp[:, None], axis=1)
    le = (logits[:, MOE_GROUPS:MOE_GROUPS + MOE_E] + be).reshape(n, MOE_GROUPS, MOE_EPG)
    le = jnp.take_along_axis(le, grp[:, None, None], axis=1)[:, 0]
    tv, ti = lax.top_k(jax.nn.softmax(le, axis=-1), MOE_TOPK)
    wts = gw * tv / jnp.sum(tv, axis=-1, keepdims=True)
    eid = (grp[:, None] * MOE_EPG + ti).reshape(-1).astype(jnp.int32)
    npair = n * MOE_TOPK
    experts = jnp.arange(MOE_E, dtype=jnp.int32)
    order = jnp.argsort(eid).astype(jnp.int32)
    rank = jnp.argsort(order).astype(jnp.int32)
    counts = jnp.sum(eid[:, None] == experts[None, :], axis=0).astype(jnp.int32)
    start = jnp.cumsum(counts) - counts
    padded = (counts + MOE_BLK - 1) // MOE_BLK * MOE_BLK
    pend = jnp.cumsum(padded)
    shift = pend - padded - start
    nb = -(-npair // MOE_BLK) + MOE_E
    blk_lo = jnp.arange(nb, dtype=jnp.int32) * MOE_BLK
    blk_exp = jnp.minimum(jnp.sum(pend[None, :] <= blk_lo[:, None], axis=1), MOE_E - 1).astype(jnp.int32)
    blk_act = (blk_lo < pend[-1]).astype(jnp.int32)
    src = (blk_lo - shift[blk_exp])[:, None] + jnp.arange(MOE_BLK, dtype=jnp.int32)[None, :]
    live = src < (start + counts)[blk_exp][:, None]
    pair = order[jnp.where(live, src, 0).reshape(-1)]
    buf_tok = pair // MOE_TOPK
    buf_gate = jnp.where(live.reshape(-1), wts.reshape(-1)[pair], 0.0)
    dest = (rank + shift[eid]).reshape(n, MOE_TOPK)
    out = moe_experts_gather(xn, buf_tok.astype(jnp.int32), buf_gate[:, None], blk_exp, blk_act, w1, w3, w2, layer)
    return moe_combine(hs, out, dest)


_KT = 128
_NT = (((1,), (1,)), ((), ()))
_BIG = 3e38
_M0 = -1e29


def _compress_kernel(x_ref, pe_ref, w_ref, o_ref):
    acc = jnp.zeros(o_ref.shape, F32)
    for l in range(L_CMP):
        y = x_ref[:, l, :] + pe_ref[l:l + 1, :]
        acc = acc + jnp.dot(y.astype(BF16), w_ref[l], preferred_element_type=F32)
    o_ref[...] = acc


def nsa_compress(x, pe, w, *, tb=256):
    nb = x.shape[0]
    tb = _row_tile(nb, tb)
    pe2 = jnp.concatenate([pe] * NSA_KV, axis=1)
    z = jnp.zeros_like(w)
    w2 = jnp.concatenate([jnp.concatenate([w, z], axis=2), jnp.concatenate([z, w], axis=2)], axis=1).astype(BF16)
    return pl.pallas_call(
        _compress_kernel,
        grid=(nb // tb,),
        in_specs=[pl.BlockSpec((tb, L_CMP, NSA_KW), lambda i: (i, 0, 0)),
                  pl.BlockSpec((L_CMP, NSA_KW), lambda i: (0, 0)),
                  pl.BlockSpec((L_CMP, NSA_KW, NSA_KW), lambda i: (0, 0, 0))],
        out_specs=pl.BlockSpec((tb, NSA_KW), lambda i: (i, 0)),
        out_shape=jax.ShapeDtypeStruct((nb, NSA_KW), F32),
        compiler_params=_params("parallel"),
        name="nsa_compress",
    )(x, pe2, w2)


def _qz(q, g, nq):
    lane = lax.broadcasted_iota(jnp.int32, (nq, NSA_KW), 1)
    keep = (lane >> 6) == g
    parts = []
    for r in range(NSA_REP):
        h = g * NSA_REP + r
        slab = q[:, (h // 2) * NSA_KW:(h // 2 + 1) * NSA_KW]
        if h % 2 != g:
            slab = pltpu.roll(slab, NSA_HD, axis=1)
        parts.append(jnp.where(keep, slab, 0.0))
    return jnp.concatenate(parts, axis=0).astype(BF16)


def _softmax_done(l_ref, acc_ref, g):
    l = l_ref[g]
    return acc_ref[g] * jnp.where(l > 0.0, 1.0 / l, 0.0)


def _compressed_branch(qz, kcb, vcbt_g, qpos, slope, ncb):
    st = lax.dot_general(kcb, qz, _NT, preferred_element_type=F32)
    row = lax.broadcasted_iota(jnp.int32, (ncb, 1), 0)
    half = ncb // 2
    blk = jnp.where(row < half, 2 * row, 2 * (row - half) + 1)
    c_pos = blk * L_CMP + (L_CMP - 1)
    d_c = qpos - c_pos
    maskf = jnp.where(d_c >= 0, 1.0, 0.0)
    s = st - slope * d_c.astype(F32)
    sm = jnp.where(d_c >= 0, s, NEG)
    m = jnp.max(sm, axis=0, keepdims=True)
    p = jnp.exp(sm - m) * maskf
    l = jnp.sum(p, axis=0, keepdims=True)
    p = p * jnp.where(l > 0.0, 1.0 / l, 0.0)
    o = jnp.dot(vcbt_g, p.astype(BF16), preferred_element_type=F32)
    return o, p


def _select_blocks(imp_sel, qpos, nsp):
    cols = imp_sel.shape[1]
    blk = lax.broadcasted_iota(jnp.int32, (nsp, cols), 0)
    cur = qpos >> 6
    valid = blk <= cur
    forced = jnp.where(valid, jnp.where(blk == 0, 1.0, jnp.where(blk >= cur - 1, 1.0, 0.0)), 0.0)
    score = jnp.where(forced > 0.0, _BIG, jnp.where(valid, imp_sel, -1.0))
    sel = jnp.zeros((nsp, cols), F32)
    for _ in range(N_SEL):
        m = jnp.max(score, axis=0, keepdims=True)
        idx = jnp.min(jnp.where(score == m, blk, nsp + 1), axis=0, keepdims=True)
        pick = blk == idx
        sel = jnp.where(pick, 1.0, sel)
        score = jnp.where(pick, -2.0, score)
    return jnp.where(valid, sel, 0.0)


_TS = 256


def _pos_lanes(nkeys, tile_off):
    lane = lax.broadcasted_iota(jnp.int32, (nkeys, NSA_KW), 1)
    key = lax.broadcasted_iota(jnp.int32, (nkeys, NSA_KW), 0)
    hi = (tile_off + (key >> 7)).astype(F32)
    lo = (key & (_KT - 1)).astype(F32)
    return jnp.where(lane == 0, hi, jnp.where(lane == 1, lo, 0.0)).astype(BF16)


def _nsa_prompt_kernel(q_ref, glog_ref, kcb_ref, vcbt_ref, ks_ref, vst_ref, kw_ref, vwt_ref,
                        qoff_ref, slope_ref, qaux_ref, o_ref, sel_ref, m_ref, l_ref, acc_ref, *, nq, ncb):
    i = pl.program_id(1)
    st0 = i * nq
    qoff = qoff_ref[...]
    qpos = st0 + qoff
    q = q_ref[0] * (NSA_HD ** -0.5)
    nsp = ncb // 2
    bpt = _TS // L_SEL
    idiag = st0 // _TS
    groups = range(NSA_KV)
    rows = [slice(g * NSA_HD, (g + 1) * NSA_HD) for g in groups]

    qxs, o_cs, firsts = [], [], []
    for g in groups:
        qz = _qz(q, g, nq)
        qxs.append(jnp.concatenate([qz, qaux_ref[g]], axis=1))
        o_c, p_c = _compressed_branch(qz, kcb_ref[0], vcbt_ref[0, rows[g], :], qpos, slope_ref[g], ncb)
        o_cs.append(o_c)
        imp = p_c[:, 0:nq]
        for r in range(1, NSA_REP):
            imp = imp + p_c[:, r * nq:(r + 1) * nq]
        sel = _select_blocks(imp[0:nsp] + imp[nsp:ncb], qpos[:, 0:nq], nsp)
        sel_ref[g] = jnp.concatenate([jnp.where(sel > 0.0, 0.0, NEG)] * NSA_REP, axis=1)
        blk = lax.broadcasted_iota(jnp.int32, (nsp, 1), 0)
        row_any = jnp.max(sel, axis=1, keepdims=True)
        first = jnp.min(jnp.where(row_any > 0.0, jnp.where(blk >= bpt, blk, nsp * bpt), nsp * bpt),
                        axis=0, keepdims=True)
        firsts.append(first[0, 0])
    lo = jnp.minimum(jnp.minimum(firsts[0], firsts[1]) // bpt, idiag)

    m_ref[...] = jnp.full(m_ref.shape, _M0, F32)
    l_ref[...] = jnp.zeros(l_ref.shape, F32)
    acc_ref[...] = jnp.zeros(acc_ref.shape, F32)

    def sel_tile(kt, diag):
        off = kt * _TS if isinstance(kt, int) else pl.multiple_of(kt * _TS, _TS)
        kx = jnp.concatenate([ks_ref[0, pl.ds(off, _TS), :], _pos_lanes(_TS, kt * (_TS // _KT) - i)], axis=1)
        if diag:
            visible = (off + lax.broadcasted_iota(jnp.int32, (_TS, 1), 0)) <= qpos
        for g in groups:
            s = lax.dot_general(kx, qxs[g], _NT, preferred_element_type=F32)
            s = jnp.concatenate([s[b * L_SEL:(b + 1) * L_SEL] + sel_ref[g, pl.ds(bpt * kt + b, 1), :]
                                 for b in range(bpt)], axis=0)
            if diag:
                s = jnp.where(visible, s, NEG)
            m_old = m_ref[g]
            m_new = jnp.maximum(m_old, jnp.max(s, axis=0, keepdims=True))
            alpha = jnp.exp(m_old - m_new)
            p = jnp.exp(s - m_new)
            l_ref[g] = alpha * l_ref[g] + jnp.sum(p, axis=0, keepdims=True)
            vt = vst_ref[rows[g], pl.ds(off, _TS)]
            acc_ref[g] = alpha * acc_ref[g] + jnp.dot(vt, p.astype(BF16), preferred_element_type=F32)
            m_ref[g] = m_new

    @pl.when(idiag > 0)
    def _():
        sel_tile(0, False)

    def sel_body(kt, carry):
        sel_tile(kt, False)
        return carry

    lax.fori_loop(lo, idiag, sel_body, 0)
    sel_tile(idiag, True)

    wk = WINDOW + nq
    wstart = pl.multiple_of(jnp.maximum(i - WINDOW // nq, 0) * nq, nq)
    kxw = jnp.concatenate([kw_ref[0, pl.ds(wstart, wk), :], _pos_lanes(wk, wstart // _KT - i)], axis=1)
    d_w = qpos - (wstart + lax.broadcasted_iota(jnp.int32, (wk, 1), 0))
    in_window = jnp.where(d_w >= 0, d_w, WINDOW + 1) <= WINDOW
    for g in groups:
        s = jnp.where(in_window, lax.dot_general(kxw, qxs[g], _NT, preferred_element_type=F32), NEG)
        m = jnp.max(s, axis=0, keepdims=True)
        p = jnp.exp(s - m)
        l = jnp.sum(p, axis=0, keepdims=True)
        o_w = jnp.dot(vwt_ref[rows[g], pl.ds(wstart, wk)], p.astype(BF16), preferred_element_type=F32) / l
        o_s = _softmax_done(l_ref, acc_ref, g)
        gc = _sigmoid(glog_ref[0, 0, g:g + 1, :])
        gs = _sigmoid(glog_ref[0, 0, 2 + g:3 + g, :])
        gw = _sigmoid(glog_ref[0, 0, 4 + g:5 + g, :])
        o_t = gc * o_cs[g] + gs * o_s + gw * o_w
        for pp in range(NSA_REP // 2):
            pair = jnp.concatenate([o_t[:, (2 * pp) * nq:(2 * pp + 1) * nq],
                                    o_t[:, (2 * pp + 1) * nq:(2 * pp + 2) * nq]], axis=0)
            slab = g * (NSA_REP // 2) + pp
            o_ref[0, :, slab * NSA_KW:(slab + 1) * NSA_KW] = pair.T


def _nsa_cols(nq):
    c = NSA_REP * nq
    qoff = (jnp.arange(c, dtype=jnp.int32) % nq).reshape(1, c)
    slopes = 2.0 ** (-8.0 * jnp.arange(1, NSA_HEADS + 1, dtype=F32) / NSA_HEADS)
    slope = jnp.repeat(slopes.reshape(NSA_KV, NSA_REP), nq, axis=1).reshape(NSA_KV, 1, c)
    return qoff, slope


def _gate_cols(glog, nq):
    b, t, _ = glog.shape
    x = glog.reshape(b, t // nq, nq, NSA_KV, NSA_REP, 3)
    return x.transpose(0, 1, 5, 3, 4, 2).reshape(b, t // nq, 3 * NSA_KV, NSA_REP * nq)


def _even_odd(x):
    return jnp.concatenate([x[:, 0::2], x[:, 1::2]], axis=1)


def nsa_prompt(q, glog, kcb, vcb, ks, vs_t, kw, vw_t, *, nq=128):
    b, t, _ = q.shape
    ncb = kcb.shape[1]
    c = NSA_REP * nq
    nblk = t // nq
    assert nq == _KT and t % _TS == 0 and t >= WINDOW + nq and ncb == t // L_CMP
    qoff, slope = _nsa_cols(nq)
    lane = jnp.arange(NSA_KW)[None, None, :]
    slope_col = slope.reshape(NSA_KV, c, 1)
    qaux = jnp.where(lane == 0, slope_col * _KT, jnp.where(lane == 1, slope_col, 0.0)).astype(BF16)
    kcb_p = _even_odd(kcb).astype(BF16)
    vcbt = _even_odd(vcb).transpose(0, 2, 1).astype(BF16)
    per_b = lambda shape: pl.BlockSpec((1,) + shape, lambda i, j: (i, 0, 0))
    out = pl.pallas_call(
        functools.partial(_nsa_prompt_kernel, nq=nq, ncb=ncb),
        grid=(b, nblk),
        in_specs=[pl.BlockSpec((1, nq, NSA_W), lambda i, j: (i, j, 0)),
                  pl.BlockSpec((1, 1, 3 * NSA_KV, c), lambda i, j: (i, j, 0, 0)),
                  per_b((ncb, NSA_KW)), per_b((NSA_KW, ncb)),
                  per_b((t, NSA_KW)), pl.BlockSpec((NSA_KW, t), lambda i, j: (0, i)),
                  per_b((t, NSA_KW)), pl.BlockSpec((NSA_KW, t), lambda i, j: (0, i)),
                  pl.BlockSpec((1, c), lambda i, j: (0, 0)),
                  pl.BlockSpec((NSA_KV, 1, c), lambda i, j: (0, 0, 0)),
                  pl.BlockSpec((NSA_KV, c, NSA_KW), lambda i, j: (0, 0, 0))],
        out_specs=pl.BlockSpec((1, nq, NSA_W), lambda i, j: (i, j, 0)),
        out_shape=jax.ShapeDtypeStruct((b, t, NSA_W), F32),
        scratch_shapes=[pltpu.VMEM((NSA_KV, ncb // 2, c), F32),
                        pltpu.VMEM((NSA_KV, 1, c), F32), pltpu.VMEM((NSA_KV, 1, c), F32),
                        pltpu.VMEM((NSA_KV, NSA_HD, c), F32)],
        compiler_params=_params("parallel", "arbitrary"),
        name="nsa_prompt",
    )(q, _gate_cols(glog, nq), kcb_p, vcbt, ks, vs_t, kw, vw_t, qoff, slope, qaux)
    return out


def _compress_pages_kernel(x_ref, pe_ref, w_ref, o_ref):
    for g in range(NSA_KV):
        acc = jnp.zeros(o_ref.shape[1:], F32)
        for d in range(0, NSA_HD, 2):
            y = jnp.concatenate([x_ref[0, :, g, d, :] + pe_ref[d:d + 1, :],
                                 x_ref[0, :, g, d + 1, :] + pe_ref[d + 1:d + 2, :]], axis=1)
            acc = acc + jnp.dot(y.astype(BF16), w_ref[d // 2], preferred_element_type=F32)
        o_ref[g] = acc


def nsa_compress_pages(cache_t, layer, pe, w, *, tp=256):
    n_pool, page = cache_t.shape[1], cache_t.shape[4]
    nblk = page // L_CMP
    tp = _row_tile(n_pool, tp)
    pe_t = jnp.tile(pe.T, (1, nblk))
    eye = jnp.eye(nblk, dtype=F32)
    wd = jnp.einsum('nm,lde->dnlme', eye, w).reshape(NSA_HD // 2, 2 * page, nblk * NSA_HD).astype(BF16)
    out = pl.pallas_call(
        _compress_pages_kernel,
        grid=(n_pool // tp,),
        in_specs=[pl.BlockSpec((1, tp, NSA_KV, NSA_HD, page), lambda i: (layer, i, 0, 0, 0)),
                  pl.BlockSpec((NSA_HD, page), lambda i: (0, 0)),
                  pl.BlockSpec((NSA_HD // 2, 2 * page, nblk * NSA_HD), lambda i: (0, 0, 0))],
        out_specs=pl.BlockSpec((NSA_KV, tp, nblk * NSA_HD), lambda i: (0, i, 0)),
        out_shape=jax.ShapeDtypeStruct((NSA_KV, n_pool, nblk * NSA_HD), F32),
        compiler_params=_params("parallel"),
        name="nsa_compress_pages",
    )(cache_t, pe_t, wd)
    return out.reshape(NSA_KV, n_pool, nblk, NSA_HD).transpose(1, 2, 0, 3).reshape(n_pool, nblk, NSA_KW)


def _row_softmax(s, mask):
    sm = jnp.where(mask, s, NEG)
    m = jnp.max(sm, axis=1, keepdims=True)
    p = jnp.where(mask, jnp.exp(sm - m), 0.0)
    l = jnp.sum(p, axis=1, keepdims=True)
    return p * jnp.where(l > 0.0, 1.0 / l, 0.0)


_DECODE_BB = 4


def _nsa_decode_kernel(pt_ref, q_ref, gl_ref, kcb_ref, vcb_ref, *refs, nq, ncb, npages, page, past, wb, nbb):
    ks_pages = refs[0:nbb * npages]
    vs_pages = refs[nbb * npages:2 * nbb * npages]
    (nks_ref, nvs_ref, nkw_ref, nvw_ref, wk_ref, wv_ref, slope_ref, expand_ref,
     o_ref, wko_ref, wvo_ref) = refs[2 * nbb * npages:]
    del pt_ref
    c = NSA_REP * nq
    nblk_lanes = _KT
    n_sel = -(-(past + nq) // L_SEL)
    row = lax.broadcasted_iota(jnp.int32, (c, 1), 0)
    qpos = past + (row & (nq - 1))
    pad_rows = lambda x: jnp.concatenate([x, jnp.zeros((_KT - nq, x.shape[1]), F32)], axis=0)
    lane = lax.broadcasted_iota(jnp.int32, (1, _KT), 1)
    n_sel_keys = (npages + 1) * page
    key_all = lax.broadcasted_iota(jnp.int32, (1, n_sel_keys), 1)
    expand = expand_ref[...]
    pr = lax.broadcasted_iota(jnp.int32, (ncb, nblk_lanes), 0)
    pc = lax.broadcasted_iota(jnp.int32, (ncb, nblk_lanes), 1)
    half = ncb // 2
    pair = jnp.where(jnp.where(pr < half, pr, pr - half) == pc, 1.0, 0.0).astype(BF16)
    cl = lax.broadcasted_iota(jnp.int32, (1, ncb), 1)
    c_pos = jnp.where(cl < half, 2 * cl, 2 * (cl - half) + 1) * L_CMP + (L_CMP - 1)
    d_c = qpos - c_pos
    d_cf = d_c.astype(F32)
    d_s = qpos - key_all
    d_sf = d_s.astype(F32)
    d_w = qpos - (past - wb + lax.broadcasted_iota(jnp.int32, (1, wb + _KT), 1))
    d_wf = d_w.astype(F32)
    in_window = jnp.where(d_w >= 0, d_w, WINDOW + 1) <= WINDOW
    cur = (past + lax.broadcasted_iota(jnp.int32, (nq, 1), 0)) >> 6
    valid = lane <= cur
    forced = jnp.where(valid, jnp.where(lane == 0, 1.0, jnp.where(lane >= cur - 1, 1.0, 0.0)), 0.0)

    for bb in range(nbb):
        q = q_ref[bb] * (NSA_HD ** -0.5)
        new_kt = pad_rows(nkw_ref[bb]).T
        new_vt = pad_rows(nvw_ref[bb]).T
        for g in range(NSA_KV):
            gs = slice(g * NSA_HD, (g + 1) * NSA_HD)
            qg = jnp.concatenate([q[:, (g * NSA_REP + r) * NSA_HD:(g * NSA_REP + r + 1) * NSA_HD]
                                  for r in range(NSA_REP)], axis=0).astype(BF16)
            slope = slope_ref[g]
            s_c = lax.dot_general(qg, kcb_ref[bb, :, gs], _NT, preferred_element_type=F32)
            p_c = _row_softmax(s_c - slope * d_cf, d_c >= 0)
            o_c = jnp.dot(p_c.astype(BF16), vcb_ref[bb, :, gs], preferred_element_type=F32)
            imp = p_c[0:nq]
            for r in range(1, NSA_REP):
                imp = imp + p_c[r * nq:(r + 1) * nq]
            imp_sel = _dot_exact_rhs(imp, pair)
            score = jnp.where(forced > 0.0, _BIG, jnp.where(valid, imp_sel, -1.0))
            before = jnp.zeros((nq, nblk_lanes), F32)
            for bi in range(n_sel):
                sb = score[:, bi:bi + 1]
                before = before + jnp.where(sb > score, 1.0, jnp.where(sb == score, jnp.where(lane > bi, 1.0, 0.0), 0.0))
            sel = jnp.where(valid, jnp.where(before < N_SEL, 1.0, 0.0), 0.0)
            sel_keys = jnp.dot(sel.astype(BF16), expand, preferred_element_type=F32)
            sel_keys = jnp.concatenate([sel_keys] * NSA_REP, axis=0)

            nk = pad_rows(nks_ref[bb, :, gs]).astype(BF16)
            nv = pad_rows(nvs_ref[bb, :, gs]).astype(BF16)
            s_parts = [jnp.dot(qg, ks_pages[bb * npages + j][0, 0, g].astype(BF16), preferred_element_type=F32)
                       for j in range(npages)]
            s_parts.append(lax.dot_general(qg, nk, _NT, preferred_element_type=F32))
            s_s = jnp.concatenate(s_parts, axis=1)
            p_s = _row_softmax(s_s - slope * d_sf, jnp.where(d_s >= 0, sel_keys, 0.0) > 0.0).astype(BF16)
            o_s = jnp.dot(p_s[:, npages * page:], nv, preferred_element_type=F32)
            for j in range(npages):
                o_s = o_s + lax.dot_general(p_s[:, j * page:(j + 1) * page],
                                            vs_pages[bb * npages + j][0, 0, g].astype(BF16), _NT,
                                            preferred_element_type=F32)

            nkw = pad_rows(nkw_ref[bb, :, gs]).astype(BF16)
            nvw = pad_rows(nvw_ref[bb, :, gs]).astype(BF16)
            s_w = jnp.concatenate([jnp.dot(qg, wk_ref[0, bb, g].astype(BF16), preferred_element_type=F32),
                                   lax.dot_general(qg, nkw, _NT, preferred_element_type=F32)], axis=1)
            p_w = _row_softmax(s_w - slope * d_wf, in_window).astype(BF16)
            o_w = (lax.dot_general(p_w[:, 0:wb], wv_ref[0, bb, g].astype(BF16), _NT, preferred_element_type=F32)
                   + jnp.dot(p_w[:, wb:], nvw, preferred_element_type=F32))

            gate = _sigmoid(gl_ref[bb, g])
            o_ref[bb, g] = gate[:, 0:1] * o_c + gate[:, 1:2] * o_s + gate[:, 2:3] * o_w

            for src_ref, new_full, dst_ref in ((wk_ref, new_kt, wko_ref), (wv_ref, new_vt, wvo_ref)):
                new_t = pltpu.roll(new_full[gs, :], _KT - nq, axis=1)
                shifted = pltpu.roll(src_ref[0, bb, g], wb - nq, axis=1)
                dst_ref[bb, g, :, 0:wb - _KT] = shifted[:, 0:wb - _KT]
                dst_ref[bb, g, :, wb - _KT:wb] = jnp.where(lane >= _KT - nq, new_t, shifted[:, wb - _KT:wb])


def nsa_decode(q, glog, kcb, vcb, pool_k, pool_v, page_table, nks, nvs, nkw, nvw, win_k, win_v, layer):
    b, nq, _ = q.shape
    ncb = kcb.shape[1]
    npages = page_table.shape[1]
    page = pool_k.shape[4]
    past = npages * page
    wb = win_k.shape[4]
    assert page == _KT and wb % _KT == 0 and nq & (nq - 1) == 0 and nq % 8 == 0 and nq <= L_SEL
    assert ncb % 2 == 0 and ncb // 2 <= _KT and -(-(past + nq) // L_SEL) <= _KT
    c = NSA_REP * nq
    _, slope = _nsa_cols(nq)
    gl = glog.reshape(b, nq, NSA_KV, NSA_REP, 3).transpose(0, 2, 3, 1, 4).reshape(b, NSA_KV, c, 3)
    nbb = _DECODE_BB if b % _DECODE_BB == 0 else 1
    per_b = lambda shape: pl.BlockSpec((nbb,) + shape, lambda i, pt: (i,) + (0,) * len(shape))
    const = lambda shape: pl.BlockSpec(shape, lambda i, pt: (0,) * len(shape))
    page_spec = lambda bb, j: pl.BlockSpec((1, 1, NSA_KV, NSA_HD, page),
                                           lambda i, pt: (layer, pt[i * nbb + bb, j], 0, 0, 0))
    page_specs = [page_spec(bb, j) for bb in range(nbb) for j in range(npages)]
    win_spec = pl.BlockSpec((1, nbb, NSA_KV, NSA_HD, wb), lambda i, pt: (layer, i, 0, 0, 0))
    n_keys = (npages + 1) * page
    expand = (jnp.arange(n_keys)[None, :] // L_SEL == jnp.arange(_KT)[:, None]).astype(BF16)
    in_specs = ([per_b((nq, NSA_W)), per_b((NSA_KV, c, 3)), per_b((ncb, NSA_KW)), per_b((ncb, NSA_KW))]
                + page_specs * 2
                + [per_b((nq, NSA_KW))] * 4 + [win_spec] * 2 + [const((NSA_KV, c, 1)), const((_KT, n_keys))])
    out, wko, wvo = pl.pallas_call(
        functools.partial(_nsa_decode_kernel, nq=nq, ncb=ncb, npages=npages, page=page, past=past, wb=wb, nbb=nbb),
        grid_spec=pltpu.PrefetchScalarGridSpec(
            num_scalar_prefetch=1,
            grid=(b // nbb,),
            in_specs=in_specs,
            out_specs=[per_b((NSA_KV, c, NSA_HD)), per_b((NSA_KV, NSA_HD, wb)), per_b((NSA_KV, NSA_HD, wb))],
        ),
        out_shape=[jax.ShapeDtypeStruct((b, NSA_KV, c, NSA_HD), F32),
                   jax.ShapeDtypeStruct((b, NSA_KV, NSA_HD, wb), F32),
                   jax.ShapeDtypeStruct((b, NSA_KV, NSA_HD, wb), F32)],
        compiler_params=_params("arbitrary"),
        name="nsa_decode",
    )(page_table, q, gl, _even_odd(kcb).astype(BF16), _even_odd(vcb).astype(BF16),
      *([pool_k] * (nbb * npages)), *([pool_v] * (nbb * npages)), nks, nvs, nkw, nvw, win_k, win_v,
      slope.reshape(NSA_KV, c, 1), expand)
    o = out.reshape(b, NSA_KV, NSA_REP, nq, NSA_HD).transpose(0, 3, 1, 2, 4).reshape(b, nq, NSA_W)
    return o, wko, wvo


def _xattn_cache_kernel(q_ref, k_ref, v_ref, o_ref, *, nq, nbb):
    scale = XA_HD ** -0.5
    for bb in range(nbb):
        q = jnp.concatenate([q_ref[bb, :, h * XA_HD:(h + 1) * XA_HD] for h in range(XA_HEADS)], axis=0)
        k = k_ref[0, bb].astype(BF16)
        v = v_ref[0, bb].astype(BF16)
        s = lax.dot_general(q.astype(BF16), k, _NT, preferred_element_type=F32) * scale
        col_h = lax.broadcasted_iota(jnp.int32, s.shape, 1) & (XA_HEADS - 1)
        row_h = lax.broadcasted_iota(jnp.int32, s.shape, 0) >> (nq.bit_length() - 1)
        mine = col_h == row_h
        m = jnp.max(jnp.where(mine, s, NEG), axis=1, keepdims=True)
        p = jnp.where(mine, jnp.exp(s - m), 0.0)
        p = p / jnp.sum(p, axis=1, keepdims=True)
        o = jnp.dot(p.astype(BF16), v, preferred_element_type=F32)
        for h in range(XA_HEADS):
            o_ref[bb, :, h * XA_HD:(h + 1) * XA_HD] = o[h * nq:(h + 1) * nq]


def xattn_cache(q, cache_k, cache_v, layer):
    b, nq, w = q.shape
    m = cache_k.shape[2]
    assert XA_HEADS & (XA_HEADS - 1) == 0 and nq % 8 == 0
    kv = lambda a: a.reshape(a.shape[0], b, m * XA_HEADS, XA_HD)
    nbb = _DECODE_BB if b % _DECODE_BB == 0 else 1
    kv_spec = pl.BlockSpec((1, nbb, m * XA_HEADS, XA_HD), lambda i: (layer, i, 0, 0))
    return pl.pallas_call(
        functools.partial(_xattn_cache_kernel, nq=nq, nbb=nbb),
        grid=(b // nbb,),
        in_specs=[pl.BlockSpec((nbb, nq, w), lambda i: (i, 0, 0)), kv_spec, kv_spec],
        out_specs=pl.BlockSpec((nbb, nq, w), lambda i: (i, 0, 0)),
        out_shape=jax.ShapeDtypeStruct((b, nq, w), F32),
        compiler_params=_params("parallel"),
        name="xattn_cache",
    )(q, kv(cache_k), kv(cache_v))


_HALO_M = 8
_TN = (((0,), (0,)), ((), ()))


def _softplus(x):
    return jnp.maximum(x, 0.0) + jnp.log1p(jnp.exp(-jnp.abs(x)))


def _ssd_kernel(xbc_ref, z_ref, sm_ref, dtt_ref, cs_ref, h0_ref, cw_ref, cb_ref, dtb_ref, dtbt_ref,
                al_ref, alt_ref, dsk_ref, ng_ref, y_ref, ncs_ref, hf_ref, ext_ref, h_ref, yh_ref, *, ql, dt_col):
    c = pl.program_id(1)
    nc = pl.num_programs(1)

    @pl.when(c == 0)
    def _():
        ext_ref[...] = jnp.zeros_like(ext_ref)
        ext_ref[_HALO_M - (M_CONV_W - 1):_HALO_M, :] = cs_ref[0]
        h_ref[...] = h0_ref[0]

    @pl.when(c > 0)
    def _():
        ext_ref[0:_HALO_M, :] = ext_ref[ql:ql + _HALO_M, :]

    ext_ref[_HALO_M:_HALO_M + ql, :] = xbc_ref[0]
    acc = jnp.zeros((ql, M_CONV_DIM), F32)
    for k in range(M_CONV_W):
        off = _HALO_M - (M_CONV_W - 1) + k
        acc = acc + ext_ref[off:off + ql, :].astype(BF16).astype(F32) * cw_ref[k:k + 1, :]
    xbc = _silu(acc + cb_ref[...])
    xs = xbc[:, 0:M_DIN]
    bm = xbc[:, M_DIN:M_DIN + M_GROUPS * M_DSTATE]
    cm = xbc[:, M_DIN + M_GROUPS * M_DSTATE:M_CONV_DIM]

    dt = _softplus(sm_ref[0, :, dt_col:dt_col + M_HEADS] + dtb_ref[...])
    dtt = _softplus(dtt_ref[0] + dtbt_ref[...])
    dta = dt * (-jnp.exp(al_ref[...]))
    dtat = dtt * (-jnp.exp(alt_ref[...]))
    ti = lax.broadcasted_iota(jnp.int32, (ql, ql), 0)
    si = lax.broadcasted_iota(jnp.int32, (ql, ql), 1)
    causal = si <= ti
    cum = _dot_exact_lhs(jnp.where(causal, 1.0, 0.0).astype(BF16), dta)
    cumt = _dot_exact_rhs(dtat, jnp.where(ti <= si, 1.0, 0.0).astype(BF16))
    cum_last = cum[ql - 1:ql, :]
    edec = jnp.exp(cum)
    eend = jnp.exp(cum_last - cum)
    elast = jnp.exp(cum_last)

    rep = M_HEADS // M_GROUPS
    for gi in range(M_GROUPS):
        b_g = bm[:, gi * M_DSTATE:(gi + 1) * M_DSTATE]
        c_g = cm[:, gi * M_DSTATE:(gi + 1) * M_DSTATE].astype(BF16)
        cb = lax.dot_general(c_g, b_g.astype(BF16), _NT, preferred_element_type=F32)
        for hh in range(rep):
            h = gi * rep + hh
            hs = slice(h * M_HDIM, (h + 1) * M_HDIM)
            lmat = jnp.where(causal, jnp.exp(cum[:, h:h + 1] - cumt[h:h + 1, :]), 0.0)
            x_h = xs[:, hs]
            xdt = (x_h * dt[:, h:h + 1]).astype(BF16)
            y_diag = jnp.dot((cb * lmat).astype(BF16), xdt, preferred_element_type=F32)
            h_in = h_ref[h]
            y_off = lax.dot_general(c_g, h_in.astype(BF16), _NT, preferred_element_type=F32) * edec[:, h:h + 1]
            bd = (b_g * eend[:, h:h + 1]).astype(BF16)
            s_chunk = lax.dot_general(xdt, bd, _TN, preferred_element_type=F32)
            h_ref[h] = elast[:, h:h + 1] * h_in + s_chunk
            yh_ref[:, hs] = y_diag + y_off + dsk_ref[:, hs] * x_h

    yz = yh_ref[...] * _silu(z_ref[0])
    y_ref[0] = _rms(yz, ng_ref[...])

    @pl.when(c == nc - 1)
    def _():
        ncs_ref[0] = ext_ref[_HALO_M + ql - (M_CONV_W - 1):_HALO_M + ql, :]
        hf_ref[0] = h_ref[...]


def ssd_mixer(xbc, z, small, dt_col, conv_state, h0, conv_w, conv_b, dt_bias, a_log, d_skip, norm_g, *, ql):
    b, t, _ = xbc.shape
    nc = t // ql
    sw = small.shape[2]
    dtt = small[:, :, dt_col:dt_col + M_HEADS].transpose(0, 2, 1)
    const = lambda shape: pl.BlockSpec(shape, lambda i, j: (0,) * len(shape))
    per_b = lambda shape: pl.BlockSpec((1,) + shape, lambda i, j: (i,) + (0,) * len(shape))
    row = lambda x: x.reshape(1, -1)
    colv = lambda x: x.reshape(-1, 1)
    return pl.pallas_call(
        functools.partial(_ssd_kernel, ql=ql, dt_col=dt_col),
        grid=(b, nc),
        in_specs=[pl.BlockSpec((1, ql, M_CONV_DIM), lambda i, j: (i, j, 0)),
                  pl.BlockSpec((1, ql, M_DIN), lambda i, j: (i, j, 0)),
                  pl.BlockSpec((1, ql, sw), lambda i, j: (i, j, 0)),
                  pl.BlockSpec((1, M_HEADS, ql), lambda i, j: (i, 0, j)),
                  per_b((M_CONV_W - 1, M_CONV_DIM)), per_b((M_HEADS, M_HDIM, M_DSTATE)),
                  const((M_CONV_W, M_CONV_DIM)), const((1, M_CONV_DIM)),
                  const((1, M_HEADS)), const((M_HEADS, 1)), const((1, M_HEADS)), const((M_HEADS, 1)),
                  const((1, M_DIN)), const((1, M_DIN))],
        out_specs=[pl.BlockSpec((1, ql, M_DIN), lambda i, j: (i, j, 0)),
                   per_b((M_CONV_W - 1, M_CONV_DIM)), per_b((M_HEADS, M_HDIM, M_DSTATE))],
        out_shape=[jax.ShapeDtypeStruct((b, t, M_DIN), F32),
                   jax.ShapeDtypeStruct((b, M_CONV_W - 1, M_CONV_DIM), F32),
                   jax.ShapeDtypeStruct((b, M_HEADS, M_HDIM, M_DSTATE), F32)],
        scratch_shapes=[pltpu.VMEM((_HALO_M + ql, M_CONV_DIM), F32),
                        pltpu.VMEM((M_HEADS, M_HDIM, M_DSTATE), F32),
                        pltpu.VMEM((ql, M_DIN), F32)],
        compiler_params=_params("parallel", "arbitrary"),
        name="ssd_mixer",
    )(xbc, z, small, dtt, conv_state, h0, conv_w, row(conv_b), row(dt_bias), colv(dt_bias),
      row(a_log), colv(a_log), row(jnp.repeat(d_skip, M_HDIM)), row(norm_g))


_SMALL_W = 128
_OD_SPLITS = (NSA_W,) + (NSA_KW,) * 6 + (M_DIN, M_CONV_DIM, _SMALL_W)


def _odd_w_in(w):
    o_kv = NSA_W
    o_gate = o_kv + 6 * NSA_KW
    o_z = o_gate + 3 * NSA_HEADS
    o_xbc = o_z + M_DIN
    o_dt = o_xbc + M_CONV_DIM
    pad = jnp.zeros((w.shape[0], _SMALL_W - 3 * NSA_HEADS - M_HEADS), F32)
    return jnp.concatenate([w[:, :o_gate], w[:, o_z:o_xbc], w[:, o_xbc:o_dt],
                            w[:, o_gate:o_z], w[:, o_dt:], pad], axis=1)


def kernel(x_prompt, x_sample, state_conv_a, state_conv_b, cache_cmp_k, cache_cmp_v, cache_sel_k, cache_sel_v, cache_win_k, cache_win_v, state_ssm, state_ssm_conv, cache_mem_k, cache_mem_v, page_table, mem_prompt, norm_mix, norm_xattn, norm_ffn, norm_final, ev_w_in, ev_conv_a, ev_conv_b, ev_conv_b_bias, ev_ln_g, ev_ln_b, ev_w_out, od_w_in, od_cmp_pe, od_cmp_wk, od_cmp_wv, od_ssm_conv_w, od_ssm_conv_b, od_dt_bias, od_a_log, od_d_skip, od_ssm_norm, od_w_out, xa_wq, xa_wk, xa_wv, xa_wo, moe_wg, moe_bg, moe_we, moe_be, moe_w1, moe_w3, moe_w2):
    bp, tp, d = x_prompt.shape
    bs, ts, _ = x_sample.shape
    n_p, n_s = bp * tp, bs * ts
    n_mem = mem_prompt.shape[1]
    depth = norm_mix.shape[0]
    n_pool, page = cache_cmp_k.shape[1:3]
    wb = cache_win_k.shape[2]
    dt_col = 3 * NSA_HEADS

    def groups(pair):
        return pair[0].reshape(bp, tp, -1), pair[1].reshape(bs, ts, -1)

    def rows(a_p, a_s):
        return [a_p.reshape(n_p, a_p.shape[-1]), a_s.reshape(n_s, a_s.shape[-1])]

    hs = rows(x_prompt, x_sample)
    out = {k: [] for k in ("ca_p", "ca_s", "cb_p", "cb_s", "wk_p", "wk_s", "wv_p", "wv_s",
                           "sm_p", "sm_s", "sc_p", "sc_s", "mk_p", "mv_p")}
    rows_p = [[], [], [], []]
    rows_s = [[], [], [], []]
    for i in range(depth):
        j = i // 2
        if i % 2 == 0:
            u_p, u_s = groups(g_norm_matmul(hs, norm_mix[i], ev_w_in[j]))
            ev = (ev_conv_a[j], ev_conv_b[j], ev_conv_b_bias[j], ev_ln_g[j], ev_ln_b[j])
            y_p, na_p, nb_p = even_conv(u_p, jnp.zeros((bp, CONV_A_W - 1, D_A), F32),
                                        jnp.zeros((bp, CONV_B_W - 1, D_B), F32), *ev)
            y_s, na_s, nb_s = even_conv(u_s, state_conv_a[j], state_conv_b[j], *ev)
            hs = g_matmul_res([rows(y_p, y_s)], [ev_w_out[j]], hs)
            out["ca_p"].append(na_p)
            out["ca_s"].append(na_s)
            out["cb_p"].append(nb_p)
            out["cb_s"].append(nb_s)
        else:
            w_in = _odd_w_in(od_w_in[j])
            c_ks, c_kw = NSA_W + 2 * NSA_KW, NSA_W + 4 * NSA_KW
            u = g_odd_in_proj(hs, norm_mix[i], w_in, w_in[:, NSA_W:NSA_W + 6 * NSA_KW].T, (c_ks, c_kw))
            n_main = len(_OD_SPLITS)
            ks_b, kw_b, vs_t, vw_t = (u[n_main + k][0] for k in range(4))
            kvt_p = [u[n_main + 4 + k][0].reshape(NSA_KV, NSA_HD, bp, tp).transpose(2, 3, 0, 1) for k in range(6)]
            q_p, q_s = groups(u[0])
            kv = [groups(u[1 + k]) for k in range(6)]
            kvp = [a for a, _ in kv]
            kvs = [b for _, b in kv]
            z_p, z_s = groups(u[7])
            xbc_p, xbc_s = groups(u[8])
            sm_p, sm_s = groups(u[9])
            pe, wck, wcv = od_cmp_pe[j], od_cmp_wk[j], od_cmp_wv[j]
            mw = (od_ssm_conv_w[j], od_ssm_conv_b[j], od_dt_bias[j], od_a_log[j], od_d_skip[j], od_ssm_norm[j])
            blocks = lambda a: a.reshape(-1, L_CMP, NSA_KW)
            ncb = tp // L_CMP
            kcb_p = nsa_compress(blocks(kvp[0][:, :ncb * L_CMP]), pe, wck).reshape(bp, ncb, NSA_KW)
            vcb_p = nsa_compress(blocks(kvp[1][:, :ncb * L_CMP]), pe, wcv).reshape(bp, ncb, NSA_KW)
            o_p = nsa_prompt(q_p, sm_p[:, :, :dt_col], kcb_p, vcb_p, ks_b.reshape(bp, tp, NSA_KW), vs_t,
                             kw_b.reshape(bp, tp, NSA_KW), vw_t)
            keep = min(WINDOW, tp)
            y_p, nsc_p, nsm_p = ssd_mixer(xbc_p, z_p, sm_p, dt_col, jnp.zeros((bp, M_CONV_W - 1, M_CONV_DIM), F32),
                                          jnp.zeros((bp, M_HEADS, M_HDIM, M_DSTATE), F32), *mw, ql=128)
            tokens_last = lambda a: jnp.transpose(a, (0, 1, 3, 4, 2))
            kcp = nsa_compress_pages(tokens_last(cache_cmp_k), j, pe, wck)
            vcp = nsa_compress_pages(tokens_last(cache_cmp_v), j, pe, wcv)
            kcb_s = kcp[page_table].reshape(bs, -1, NSA_KW)
            vcb_s = vcp[page_table].reshape(bs, -1, NSA_KW)
            o_s, nwk_s, nwv_s = nsa_decode(
                q_s, sm_s[:, :, :dt_col], kcb_s, vcb_s, tokens_last(cache_sel_k), tokens_last(cache_sel_v),
                page_table, kvs[2], kvs[3], kvs[4], kvs[5], tokens_last(cache_win_k), tokens_last(cache_win_v), j)
            nwk_s = jnp.transpose(nwk_s, (0, 3, 1, 2))
            nwv_s = jnp.transpose(nwv_s, (0, 3, 1, 2))
            y_s, nsc_s, nsm_s = ssd_mixer(xbc_s, z_s, sm_s, dt_col, state_ssm_conv[j], state_ssm[j], *mw, ql=ts)
            w_out = od_w_out[j]
            hs = g_matmul_res([rows(o_p, o_s), rows(y_p, y_s)], [w_out[:NSA_W], w_out[NSA_W:]], hs)
            heads = lambda a: a.reshape(a.shape[0], a.shape[1], NSA_KV, NSA_HD)
            for k in range(4):
                rows_p[k].append(kvt_p[k])
                rows_s[k].append(heads(kvs[k]))
            out["wk_p"].append(kvt_p[4][:, tp - keep:])
            out["wv_p"].append(kvt_p[5][:, tp - keep:])
            out["wk_s"].append(nwk_s)
            out["wv_s"].append(nwv_s)
            out["sc_p"].append(nsc_p)
            out["sc_s"].append(nsc_s)
            out["sm_p"].append(nsm_p)
            out["sm_s"].append(nsm_s)
        mk, mv = norm_matmul(mem_prompt.reshape(bp * n_mem, d), None,
                             jnp.concatenate([xa_wk[i], xa_wv[i]], axis=1), norm=False,
                             splits=(XA_HEADS * XA_HD, XA_HEADS * XA_HD))
        mk = mk.reshape(bp, n_mem, XA_HEADS * XA_HD)
        mv = mv.reshape(bp, n_mem, XA_HEADS * XA_HD)
        out["mk_p"].append(mk.reshape(bp, n_mem, XA_HEADS, XA_HD))
        out["mv_p"].append(mv.reshape(bp, n_mem, XA_HEADS, XA_HD))
        qx_p, qx_s = groups(g_norm_matmul(hs, norm_xattn[i], xa_wq[i]))
        ox_p = xattn(qx_p, mk, mv)
        ox_s = xattn_cache(qx_s, cache_mem_k, cache_mem_v, i)
        hs = g_matmul_res([rows(ox_p, ox_s)], [xa_wo[i]], hs)
        hs = moe_layer(hs, norm_ffn[i], moe_wg[i], moe_bg[i], moe_we[i], moe_be[i], moe_w1, moe_w3, moe_w2, i)
    y_prompt, y_sample = groups(g_rmsnorm(hs, norm_final))
    st = lambda k: jnp.stack(out[k])
    return (y_prompt, y_sample, st("ca_p"), st("ca_s"), st("cb_p"), st("cb_s"),
            jnp.stack(rows_p[0]), jnp.stack(rows_s[0]), jnp.stack(rows_p[1]), jnp.stack(rows_s[1]),
            jnp.stack(rows_p[2]), jnp.stack(rows_s[2]), jnp.stack(rows_p[3]), jnp.stack(rows_s[3]),
            st("wk_p"), st("wk_s"), st("wv_p"), st("wv_s"), st("sm_p"), st("sm_s"), st("sc_p"), st("sc_s"),
            st("mk_p"), st("mv_p"))
```

```python
import functools

import jax
import jax.numpy as jnp
from jax import lax
from jax.experimental import pallas as pl
from jax.experimental.pallas import tpu as pltpu

F32 = jnp.float32
BF16 = jnp.bfloat16
EPS = 1e-6
NEG = -1e30
VMEM_LIMIT = 56 * 1024 * 1024

D_A = 512
D_B = 512
CONV_A_W = 3
CONV_B_W = 31
NSA_HEADS = 8
NSA_HD = 64
NSA_KV = 2
NSA_REP = NSA_HEADS // NSA_KV
NSA_W = NSA_HEADS * NSA_HD
NSA_KW = NSA_KV * NSA_HD
L_CMP = 32
L_SEL = 64
N_SEL = 16
WINDOW = 512
M_DIN = 512
M_HDIM = 64
M_HEADS = 8
M_DSTATE = 64
M_GROUPS = 2
M_CONV_W = 4
M_CONV_DIM = M_DIN + 2 * M_GROUPS * M_DSTATE
XA_HEADS = 4
XA_HD = 128
MOE_GROUPS = 4
MOE_EPG = 8
MOE_E = 32
MOE_TOPK = 2


def _params(*sem):
    return pltpu.CompilerParams(dimension_semantics=sem, vmem_limit_bytes=VMEM_LIMIT)


def _row_tile(n, pref):
    t = min(n, pref)
    while n % t or (t % 8 and t != n):
        t -= 1
    return t


def _split3(a):
    hi = a.astype(BF16)
    r1 = a - hi.astype(F32)
    mid = r1.astype(BF16)
    lo = (r1 - mid.astype(F32)).astype(BF16)
    return hi, mid, lo


def _dot_exact_rhs(a, b_bf16):
    hi, mid, lo = _split3(a)
    d = lambda x: jnp.dot(x, b_bf16, preferred_element_type=F32)
    return d(hi) + d(mid) + d(lo)


def _dot_exact_lhs(a_bf16, b):
    hi, mid, lo = _split3(b)
    d = lambda x: jnp.dot(a_bf16, x, preferred_element_type=F32)
    return d(hi) + d(mid) + d(lo)


def _rms(x, g):
    ms = jnp.mean(x * x, axis=-1, keepdims=True)
    return x * lax.rsqrt(ms + EPS) * g


def _sigmoid(x):
    return 1.0 / (1.0 + jnp.exp(-x))


def _silu(x):
    return x * _sigmoid(x)


def _norm_matmul_kernel(x_ref, g_ref, w_ref, *o_refs, norm, splits):
    x = x_ref[...]
    if norm:
        x = _rms(x, g_ref[...])
    res = jnp.dot(x.astype(BF16), w_ref[...].astype(BF16), preferred_element_type=F32)
    off = 0
    for o_ref, width in zip(o_refs, splits):
        o_ref[...] = res[:, off:off + width]
        off += width


def norm_matmul(x, g, w, *, norm=True, splits=None, tm=512):
    n, k = x.shape
    m = w.shape[1]
    tm = _row_tile(n, tm)
    if g is None:
        g = jnp.ones((k,), F32)
    widths = (m,) if splits is None else tuple(splits)
    assert sum(widths) == m
    outs = pl.pallas_call(
        functools.partial(_norm_matmul_kernel, norm=norm, splits=widths),
        grid=(n // tm,),
        in_specs=[pl.BlockSpec((tm, k), lambda i: (i, 0)),
                  pl.BlockSpec((1, k), lambda i: (0, 0)),
                  pl.BlockSpec((k, m), lambda i: (0, 0))],
        out_specs=[pl.BlockSpec((tm, wd), lambda i: (i, 0)) for wd in widths],
        out_shape=[jax.ShapeDtypeStruct((n, wd), F32) for wd in widths],
        compiler_params=_params("parallel"),
        name="norm_matmul",
    )(x, g.reshape(1, k), w)
    return outs[0] if splits is None else outs


_TM = 512


def _rowwise_call(body, row_inputs, shared, out_widths, out_dtypes, name, joint_outputs=False, transposed=()):
    ns = [a.shape[0] for a in row_inputs[0]]
    assert all(n % _TM == 0 for n in ns)
    nbs = [n // _TM for n in ns]
    starts = [sum(nbs[:g]) for g in range(len(ns))]
    n_groups, n_row, n_out = len(ns), len(row_inputs), len(out_widths)

    def group_map(g):
        return lambda i: (jnp.clip(i - starts[g], 0, nbs[g] - 1), 0)

    in_specs, args = [], []
    for k in range(n_row):
        for g in range(n_groups):
            a = row_inputs[k][g]
            in_specs.append(pl.BlockSpec((_TM, a.shape[1]), group_map(g)))
            args.append(a)
    for a in shared:
        in_specs.append(pl.BlockSpec(a.shape, lambda i, nd=a.ndim: (0,) * nd))
        args.append(a)
    if joint_outputs:
        out_specs = [pl.BlockSpec((_TM, w), lambda i: (i, 0)) for w in out_widths]
        out_shape = [jax.ShapeDtypeStruct((sum(ns), w), dt) for w, dt in zip(out_widths, out_dtypes)]
    else:
        def group_map_t(g):
            return lambda i: (0, jnp.clip(i - starts[g], 0, nbs[g] - 1))

        out_specs, out_shape = [], []
        for j, (w, dt) in enumerate(zip(out_widths, out_dtypes)):
            for g in range(n_groups):
                if j in transposed:
                    out_specs.append(pl.BlockSpec((w, _TM), group_map_t(g)))
                    out_shape.append(jax.ShapeDtypeStruct((w, ns[g]), dt))
                else:
                    out_specs.append(pl.BlockSpec((_TM, w), group_map(g)))
                    out_shape.append(jax.ShapeDtypeStruct((ns[g], w), dt))

    def kernel(*refs):
        x_refs = refs[:n_row * n_groups]
        s_refs = refs[n_row * n_groups:n_row * n_groups + len(shared)]
        o_refs = refs[n_row * n_groups + len(shared):]
        i = pl.program_id(0)
        for g in range(n_groups):
            @pl.when((i >= starts[g]) & (i < starts[g] + nbs[g]))
            def _(g=g):
                vals = body([x_refs[k * n_groups + g][...] for k in range(n_row)], s_refs)
                for j, v in enumerate(vals):
                    o_ref = o_refs[j] if joint_outputs else o_refs[j * n_groups + g]
                    o_ref[...] = v.astype(o_ref.dtype)

    outs = pl.pallas_call(
        kernel, grid=(sum(nbs),), in_specs=in_specs, out_specs=out_specs, out_shape=out_shape,
        compiler_params=_params("arbitrary"), name=name)(*args)
    if joint_outputs:
        return list(outs)
    return [list(outs[j * n_groups:(j + 1) * n_groups]) for j in range(n_out)]


def g_norm_matmul(hs, g, w, *, splits=None, norm=True):
    widths = (w.shape[1],) if splits is None else tuple(splits)
    assert sum(widths) == w.shape[1]

    def body(xs, s_refs):
        x = _rms(xs[0], s_refs[0][...]) if norm else xs[0]
        res = jnp.dot(x.astype(BF16), s_refs[1][...].astype(BF16), preferred_element_type=F32)
        offs = [sum(widths[:j]) for j in range(len(widths))]
        return [res[:, o:o + wd] for o, wd in zip(offs, widths)]

    k = hs[0].shape[1]
    gv = jnp.ones((1, k), F32) if g is None else g.reshape(1, k)
    outs = _rowwise_call(body, [hs], [gv, w], widths, [F32] * len(widths), "norm_matmul")
    return outs[0] if splits is None else outs


def g_odd_in_proj(hs, g, w, w_kvt, k_cols):
    widths = _OD_SPLITS + (NSA_KW,) * 4 + (NSA_KW,) * 6
    offs = [sum(_OD_SPLITS[:j]) for j in range(len(_OD_SPLITS))]

    def body(xs, s_refs):
        x = _rms(xs[0], s_refs[0][...]).astype(BF16)
        res = jnp.dot(x, s_refs[1][...].astype(BF16), preferred_element_type=F32)
        kvt = lax.dot_general(s_refs[2][...].astype(BF16), x, _NT, preferred_element_type=F32)
        part = lambda k: kvt[k * NSA_KW:(k + 1) * NSA_KW]
        outs = [res[:, o:o + wd] for o, wd in zip(offs, _OD_SPLITS)]
        outs += [res[:, c:c + NSA_KW] for c in k_cols]
        outs += [part(3), part(5)]
        outs += [part(k) for k in range(6)]
        return outs

    n_main = len(_OD_SPLITS)
    dts = [F32] * n_main + [BF16] * 4 + [F32] * 6
    return _rowwise_call(body, [hs], [g.reshape(1, -1), w, w_kvt], widths, dts, "odd_in_proj",
                         transposed=(n_main + 2, n_main + 3) + tuple(range(n_main + 4, n_main + 10)))


def g_matmul_res(xs_list, ws, hs):
    def body(xs, s_refs):
        acc = xs[-1]
        for j in range(len(ws)):
            acc = acc + jnp.dot(xs[j].astype(BF16), s_refs[j][...].astype(BF16), preferred_element_type=F32)
        return [acc]

    return _rowwise_call(body, list(xs_list) + [hs], list(ws), (hs[0].shape[1],), [F32], "matmul_res")[0]


def g_rmsnorm(hs, g):
    body = lambda xs, s_refs: [_rms(xs[0], s_refs[0][...])]
    return _rowwise_call(body, [hs], [g.reshape(1, -1)], (hs[0].shape[1],), [F32], "rmsnorm_rows")[0]


def g_moe_router(hs, g, w_router):
    def body(xs, s_refs):
        xb = _rms(xs[0], s_refs[0][...]).astype(BF16)
        logits = jnp.dot(xb, s_refs[1][...].astype(BF16), preferred_element_type=F32)
        half = xb.shape[1] // 2
        bits = lambda v: lax.bitcast_convert_type(v.astype(F32), jnp.uint32)
        words = (bits(xb[:, half:]) & jnp.uint32(0xFFFF0000)) | (bits(xb[:, :half]) >> 16)
        return [logits, lax.bitcast_convert_type(words, F32)]

    k = hs[0].shape[1]
    return _rowwise_call(body, [hs], [g.reshape(1, k), w_router], (_ROUTER_W, k // 2), [F32, F32], "moe_router",
                         joint_outputs=True)


_HALO_A = 8
_HALO_B = 32


def _even_conv_kernel(u_ref, sa_ref, sb_ref, wa_ref, wb_ref, bb_ref, lg_ref, lb_ref,
                      y_ref, na_ref, nb_ref, ea_ref, eb_ref, ear_ref, ebr_ref, sh_ref, *, tt):
    rnd = lambda x: x.astype(BF16).astype(F32)
    t = pl.program_id(1)
    nt = pl.num_programs(1)

    @pl.when(t == 0)
    def _():
        ea_ref[...] = jnp.zeros_like(ea_ref)
        eb_ref[...] = jnp.zeros_like(eb_ref)
        ea_ref[_HALO_A - (CONV_A_W - 1):_HALO_A, :] = sa_ref[0]
        eb_ref[_HALO_B - (CONV_B_W - 1):_HALO_B, :] = sb_ref[0]
        ear_ref[...] = rnd(ea_ref[...])
        ebr_ref[...] = rnd(eb_ref[...])

    @pl.when(t > 0)
    def _():
        ea_ref[0:_HALO_A, :] = ea_ref[tt:tt + _HALO_A, :]
        eb_ref[0:_HALO_B, :] = eb_ref[tt:tt + _HALO_B, :]
        ear_ref[0:_HALO_A, :] = ear_ref[tt:tt + _HALO_A, :]
        ebr_ref[0:_HALO_B, :] = ebr_ref[tt:tt + _HALO_B, :]

    xa = u_ref[0, :, 0:D_A]
    ba = u_ref[0, :, D_A:2 * D_A]
    ca = u_ref[0, :, 2 * D_A:3 * D_A]
    pb = u_ref[0, :, 3 * D_A:3 * D_A + D_B]
    gb = u_ref[0, :, 3 * D_A + D_B:3 * D_A + 2 * D_B]
    va = ca * xa
    vb = pb * _sigmoid(gb)
    ea_ref[_HALO_A:_HALO_A + tt, :] = va
    eb_ref[_HALO_B:_HALO_B + tt, :] = vb
    ear_ref[_HALO_A:_HALO_A + tt, :] = rnd(va)
    ebr_ref[_HALO_B:_HALO_B + tt, :] = rnd(vb)

    acc = jnp.zeros((tt, D_A), F32)
    for k in range(CONV_A_W):
        off = _HALO_A - (CONV_A_W - 1) + k
        acc = acc + ear_ref[off:off + tt, :] * wa_ref[k:k + 1, :]
    y_ref[0, :, 0:D_A] = ba * acc

    span = tt + _HALO_B - 8
    for r in range(1, 8):
        sh_ref[r - 1, 0:span, :] = ebr_ref[r:r + span, :]
    acc = jnp.zeros((tt, D_B), F32)
    for k in range(CONV_B_W):
        off = _HALO_B - (CONV_B_W - 1) + k
        base = off - off % 8
        rows = ebr_ref[base:base + tt, :] if off % 8 == 0 else sh_ref[off % 8 - 1, base:base + tt, :]
        acc = acc + rows * wb_ref[k:k + 1, :]
    acc = acc + bb_ref[...]
    mu = jnp.mean(acc, axis=-1, keepdims=True)
    xc = acc - mu
    var = jnp.mean(xc * xc, axis=-1, keepdims=True)
    yb = xc * lax.rsqrt(var + EPS) * lg_ref[...] + lb_ref[...]
    y_ref[0, :, D_A:D_A + D_B] = _silu(yb)

    @pl.when(t == nt - 1)
    def _():
        na_ref[0] = ea_ref[_HALO_A + tt - (CONV_A_W - 1):_HALO_A + tt, :]
        nb_ref[0] = eb_ref[_HALO_B + tt - (CONV_B_W - 1):_HALO_B + tt, :]


def even_conv(u, sa, sb, wa, wb, bb, lg, lb, *, tt=256):
    b, t, w = u.shape
    tt = _row_tile(t, tt)
    full = lambda shape: pl.BlockSpec(shape, lambda i, j: (0,) * len(shape))
    return pl.pallas_call(
        functools.partial(_even_conv_kernel, tt=tt),
        grid=(b, t // tt),
        in_specs=[pl.BlockSpec((1, tt, w), lambda i, j: (i, j, 0)),
                  pl.BlockSpec((1, CONV_A_W - 1, D_A), lambda i, j: (i, 0, 0)),
                  pl.BlockSpec((1, CONV_B_W - 1, D_B), lambda i, j: (i, 0, 0)),
                  full((CONV_A_W, D_A)), full((CONV_B_W, D_B)), full((1, D_B)),
                  full((1, D_B)), full((1, D_B))],
        out_specs=[pl.BlockSpec((1, tt, D_A + D_B), lambda i, j: (i, j, 0)),
                   pl.BlockSpec((1, CONV_A_W - 1, D_A), lambda i, j: (i, 0, 0)),
                   pl.BlockSpec((1, CONV_B_W - 1, D_B), lambda i, j: (i, 0, 0))],
        out_shape=[jax.ShapeDtypeStruct((b, t, D_A + D_B), F32),
                   jax.ShapeDtypeStruct((b, CONV_A_W - 1, D_A), F32),
                   jax.ShapeDtypeStruct((b, CONV_B_W - 1, D_B), F32)],
        scratch_shapes=[pltpu.VMEM((_HALO_A + tt, D_A), F32), pltpu.VMEM((_HALO_B + tt, D_B), F32),
                        pltpu.VMEM((_HALO_A + tt, D_A), F32), pltpu.VMEM((_HALO_B + tt, D_B), F32),
                        pltpu.VMEM((7, _HALO_B + tt - 8, D_B), F32)],
        compiler_params=_params("parallel", "arbitrary"),
        name="even_conv",
    )(u, sa, sb, wa, wb, bb.reshape(1, D_B), lg.reshape(1, D_B), lb.reshape(1, D_B))


def _xattn_kernel(q_ref, k_ref, v_ref, o_ref):
    scale = XA_HD ** -0.5
    for h in range(XA_HEADS):
        sl = slice(h * XA_HD, (h + 1) * XA_HD)
        q = q_ref[0, :, sl].astype(BF16)
        k = k_ref[0, :, sl].astype(BF16)
        v = v_ref[0, :, sl].astype(BF16)
        s = lax.dot_general(q, k, (((1,), (1,)), ((), ())), preferred_element_type=F32) * scale
        m = jnp.max(s, axis=-1, keepdims=True)
        p = jnp.exp(s - m)
        p = p / jnp.sum(p, axis=-1, keepdims=True)
        o_ref[0, :, sl] = jnp.dot(p.astype(BF16), v, preferred_element_type=F32)


def xattn(q, k, v, *, tq=512):
    b, t, w = q.shape
    tq = _row_tile(t, tq)
    m = k.shape[1]
    kv_spec = pl.BlockSpec((1, m, w), lambda i, j: (i, 0, 0))
    return pl.pallas_call(
        _xattn_kernel,
        grid=(b, t // tq),
        in_specs=[pl.BlockSpec((1, tq, w), lambda i, j: (i, j, 0)), kv_spec, kv_spec],
        out_specs=pl.BlockSpec((1, tq, w), lambda i, j: (i, j, 0)),
        out_shape=jax.ShapeDtypeStruct((b, t, w), F32),
        compiler_params=_params("parallel", "parallel"),
        name="xattn",
    )(q, k, v)


MOE_BLK = 256
_ROUTER_W = 128


def _expert_gather_kernel(tok_ref, be_ref, act_ref, xn_hbm, gate_ref, w1_ref, w3_ref, w2_ref, o_ref,
                          xbuf_ref, sem_ref, w1b_ref, w3b_ref, w2b_ref):
    i = pl.program_id(0)
    nb = pl.num_programs(0)

    def row_copy(blk, slot, r):
        tok = tok_ref[blk * MOE_BLK + r]
        return pltpu.make_async_copy(xn_hbm.at[pl.ds(tok, 1)], xbuf_ref.at[slot, pl.ds(r, 1)], sem_ref.at[slot])

    def start_gather(blk, slot):
        if isinstance(slot, int):
            for r in range(MOE_BLK):
                row_copy(blk, slot, r).start()
            return
        for s in range(2):
            @pl.when(slot == s)
            def _(s=s):
                for r in range(MOE_BLK):
                    row_copy(blk, s, r).start()

    def wait_gather(slot):
        pltpu.make_async_copy(xbuf_ref.at[slot], xbuf_ref.at[slot], sem_ref.at[slot]).wait()

    slot = lax.rem(i, 2)
    nxt = jnp.minimum(i + 1, nb - 1)

    @pl.when((i == 0) & (act_ref[0] > 0))
    def _():
        start_gather(0, 0)

    @pl.when((i + 1 < nb) & (act_ref[nxt] > 0))
    def _():
        start_gather(nxt, 1 - slot)

    prev = be_ref[jnp.maximum(i - 1, 0)]

    @pl.when((act_ref[i] > 0) & ((i == 0) | (be_ref[i] != prev)))
    def _():
        w1b_ref[...] = w1_ref[0, 0].astype(BF16)
        w3b_ref[...] = w3_ref[0, 0].astype(BF16)
        w2b_ref[...] = w2_ref[0, 0].astype(BF16)

    @pl.when(act_ref[i] > 0)
    def _():
        wait_gather(slot)
        words = lax.bitcast_convert_type(xbuf_ref[slot], jnp.uint32)
        unpack = lambda v: lax.bitcast_convert_type(v, F32).astype(BF16)
        x = jnp.concatenate([unpack(words << 16), unpack(words & jnp.uint32(0xFFFF0000))], axis=1)
        h1 = jnp.dot(x, w1b_ref[...], preferred_element_type=F32)
        h3 = jnp.dot(x, w3b_ref[...], preferred_element_type=F32)
        hid = (_silu(h1) * h3).astype(BF16)
        out = jnp.dot(hid, w2b_ref[...], preferred_element_type=F32)
        o_ref[...] = out * gate_ref[...]

    @pl.when(act_ref[i] == 0)
    def _():
        o_ref[...] = jnp.zeros_like(o_ref)


def moe_experts_gather(xn, buf_tok, gate, blk_exp, blk_act, w1, w3, w2, layer):
    rows = buf_tok.shape[0]
    d = w1.shape[2]
    nb = rows // MOE_BLK
    ff = w1.shape[3]
    return pl.pallas_call(
        _expert_gather_kernel,
        grid_spec=pltpu.PrefetchScalarGridSpec(
            num_scalar_prefetch=3,
            grid=(nb,),
            in_specs=[pl.BlockSpec(memory_space=pl.ANY),
                      pl.BlockSpec((MOE_BLK, 1), lambda i, tok, be, act: (i, 0)),
                      pl.BlockSpec((1, 1, d, ff), lambda i, tok, be, act: (layer, be[i], 0, 0)),
                      pl.BlockSpec((1, 1, d, ff), lambda i, tok, be, act: (layer, be[i], 0, 0)),
                      pl.BlockSpec((1, 1, ff, d), lambda i, tok, be, act: (layer, be[i], 0, 0))],
            out_specs=pl.BlockSpec((MOE_BLK, d), lambda i, tok, be, act: (i, 0)),
            scratch_shapes=[pltpu.VMEM((2, MOE_BLK, d // 2), F32), pltpu.SemaphoreType.DMA((2,)),
                            pltpu.VMEM((d, ff), BF16), pltpu.VMEM((d, ff), BF16), pltpu.VMEM((ff, d), BF16)],
        ),
        out_shape=jax.ShapeDtypeStruct((rows, d), F32),
        compiler_params=_params("arbitrary"),
        name="moe_experts",
    )(buf_tok, blk_exp, blk_act, xn, gate, w1, w3, w2)


_COMBINE_TM = 256


def _combine_kernel(dest_ref, *refs, starts, nbs):
    n_groups = len(starts)
    h_refs = refs[:n_groups]
    out_hbm = refs[n_groups]
    o_refs = refs[n_groups + 1:2 * n_groups + 1]
    buf_ref, sem_ref = refs[2 * n_groups + 1:]
    i = pl.program_id(0)
    nsteps = pl.num_programs(0)
    tm = _COMBINE_TM

    def start_gather(step, slot):
        for s in range(2):
            @pl.when(slot == s)
            def _(s=s):
                for r in range(tm):
                    for k in range(MOE_TOPK):
                        row = dest_ref[(step * tm + r) * MOE_TOPK + k]
                        pltpu.make_async_copy(out_hbm.at[pl.ds(row, 1)], buf_ref.at[s, k, pl.ds(r, 1)],
                                              sem_ref.at[s]).start()

    slot = lax.rem(i, 2)

    @pl.when(i == 0)
    def _():
        start_gather(i, slot)

    @pl.when(i + 1 < nsteps)
    def _():
        start_gather(i + 1, 1 - slot)

    pltpu.make_async_copy(buf_ref.at[slot], buf_ref.at[slot], sem_ref.at[slot]).wait()
    y = buf_ref[slot, 0] + buf_ref[slot, 1]
    for g in range(n_groups):
        @pl.when((i >= starts[g]) & (i < starts[g] + nbs[g]))
        def _(g=g):
            o_refs[g][...] = h_refs[g][...] + y


def moe_combine(hs, out, dest):
    tm = _COMBINE_TM
    ns = [h.shape[0] for h in hs]
    d = hs[0].shape[1]
    assert all(n % tm == 0 for n in ns)
    nbs = [n // tm for n in ns]
    starts = [sum(nbs[:g]) for g in range(len(ns))]

    def group_map(g):
        return lambda i, dest: (jnp.clip(i - starts[g], 0, nbs[g] - 1), 0)

    specs = [pl.BlockSpec((tm, d), group_map(g)) for g in range(len(ns))]
    return pl.pallas_call(
        functools.partial(_combine_kernel, starts=starts, nbs=nbs),
        grid_spec=pltpu.PrefetchScalarGridSpec(
            num_scalar_prefetch=1,
            grid=(sum(nbs),),
            in_specs=specs + [pl.BlockSpec(memory_space=pl.ANY)],
            out_specs=specs,
            scratch_shapes=[pltpu.VMEM((2, MOE_TOPK, tm, d), F32), pltpu.SemaphoreType.DMA((2,))],
        ),
        out_shape=[jax.ShapeDtypeStruct((n, d), F32) for n in ns],
        compiler_params=_params("arbitrary"),
        name="moe_combine",
    )(dest.reshape(-1), *hs, out)


def moe_layer(hs, g, wg, bg, we, be, w1, w3, w2, layer):
    d = hs[0].shape[1]
    n = sum(h.shape[0] for h in hs)
    w_router = jnp.concatenate([wg, we, jnp.zeros((d, _ROUTER_W - MOE_GROUPS - MOE_E), F32)], axis=1)
    logits, xn = g_moe_router(hs, g, w_router)
    lg = logits[:, :MOE_GROUPS] + bg
    grp = jnp.argmax(lg, axis=-1)
    gw = jnp.take_along_axis(jax.nn.softmax(lg, axis=-1), grp[:, None], axis=1)
    le = (logits[:, MOE_GROUPS:MOE_GROUPS + MOE_E] + be).reshape(n, MOE_GROUPS, MOE_EPG)
    le = jnp.take_along_axis(le, grp[:, None, None], axis=1)[:, 0]
    tv, ti = lax.top_k(jax.nn.softmax(le, axis=-1), MOE_TOPK)
    wts = gw * tv / jnp.sum(tv, axis=-1, keepdims=True)
    eid = (grp[:, None] * MOE_EPG + ti).reshape(-1).astype(jnp.int32)
    npair = n * MOE_TOPK
    experts = jnp.arange(MOE_E, dtype=jnp.int32)
    order = jnp.argsort(eid).astype(jnp.int32)
    rank = jnp.argsort(order).astype(jnp.int32)
    counts = jnp.sum(eid[:, None] == experts[None, :], axis=0).astype(jnp.int32)
    start = jnp.cumsum(counts) - counts
    padded = (counts + MOE_BLK - 1) // MOE_BLK * MOE_BLK
    pend = jnp.cumsum(padded)
    shift = pend - padded - start
    nb = -(-npair // MOE_BLK) + MOE_E
    blk_lo = jnp.arange(nb, dtype=jnp.int32) * MOE_BLK
    blk_exp = jnp.minimum(jnp.sum(pend[None, :] <= blk_lo[:, None], axis=1), MOE_E - 1).astype(jnp.int32)
    blk_act = (blk_lo < pend[-1]).astype(jnp.int32)
    src = (blk_lo - shift[blk_exp])[:, None] + jnp.arange(MOE_BLK, dtype=jnp.int32)[None, :]
    live = src < (start + counts)[blk_exp][:, None]
    pair = order[jnp.where(live, src, 0).reshape(-1)]
    buf_tok = pair // MOE_TOPK
    buf_gate = jnp.where(live.reshape(-1), wts.reshape(-1)[pair], 0.0)
    dest = (rank + shift[eid]).reshape(n, MOE_TOPK)
    out = moe_experts_gather(xn, buf_tok.astype(jnp.int32), buf_gate[:, None], blk_exp, blk_act, w1, w3, w2, layer)
    return moe_combine(hs, out, dest)


_KT = 128
_NT = (((1,), (1,)), ((), ()))
_BIG = 3e38
_M0 = -1e29


def _compress_kernel(x_ref, pe_ref, w_ref, o_ref):
    acc = jnp.zeros(o_ref.shape, F32)
    for l in range(L_CMP):
        y = x_ref[:, l, :] + pe_ref[l:l + 1, :]
        acc = acc + jnp.dot(y.astype(BF16), w_ref[l], preferred_element_type=F32)
    o_ref[...] = acc


def nsa_compress(x, pe, w, *, tb=256):
    nb = x.shape[0]
    tb = _row_tile(nb, tb)
    pe2 = jnp.concatenate([pe] * NSA_KV, axis=1)
    z = jnp.zeros_like(w)
    w2 = jnp.concatenate([jnp.concatenate([w, z], axis=2), jnp.concatenate([z, w], axis=2)], axis=1).astype(BF16)
    return pl.pallas_call(
        _compress_kernel,
        grid=(nb // tb,),
        in_specs=[pl.BlockSpec((tb, L_CMP, NSA_KW), lambda i: (i, 0, 0)),
                  pl.BlockSpec((L_CMP, NSA_KW), lambda i: (0, 0)),
                  pl.BlockSpec((L_CMP, NSA_KW, NSA_KW), lambda i: (0, 0, 0))],
        out_specs=pl.BlockSpec((tb, NSA_KW), lambda i: (i, 0)),
        out_shape=jax.ShapeDtypeStruct((nb, NSA_KW), F32),
        compiler_params=_params("parallel"),
        name="nsa_compress",
    )(x, pe2, w2)


def _qz(q, g, nq):
    lane = lax.broadcasted_iota(jnp.int32, (nq, NSA_KW), 1)
    keep = (lane >> 6) == g
    parts = []
    for r in range(NSA_REP):
        h = g * NSA_REP + r
        slab = q[:, (h // 2) * NSA_KW:(h // 2 + 1) * NSA_KW]
        if h % 2 != g:
            slab = pltpu.roll(slab, NSA_HD, axis=1)
        parts.append(jnp.where(keep, slab, 0.0))
    return jnp.concatenate(parts, axis=0).astype(BF16)


def _softmax_done(l_ref, acc_ref, g):
    l = l_ref[g]
    return acc_ref[g] * jnp.where(l > 0.0, 1.0 / l, 0.0)


def _compressed_branch(qz, kcb, vcbt_g, qpos, slope, ncb):
    st = lax.dot_general(kcb, qz, _NT, preferred_element_type=F32)
    row = lax.broadcasted_iota(jnp.int32, (ncb, 1), 0)
    half = ncb // 2
    blk = jnp.where(row < half, 2 * row, 2 * (row - half) + 1)
    c_pos = blk * L_CMP + (L_CMP - 1)
    d_c = qpos - c_pos
    maskf = jnp.where(d_c >= 0, 1.0, 0.0)
    s = st - slope * d_c.astype(F32)
    sm = jnp.where(d_c >= 0, s, NEG)
    m = jnp.max(sm, axis=0, keepdims=True)
    p = jnp.exp(sm - m) * maskf
    l = jnp.sum(p, axis=0, keepdims=True)
    p = p * jnp.where(l > 0.0, 1.0 / l, 0.0)
    o = jnp.dot(vcbt_g, p.astype(BF16), preferred_element_type=F32)
    return o, p


def _select_blocks(imp_sel, qpos, nsp):
    cols = imp_sel.shape[1]
    blk = lax.broadcasted_iota(jnp.int32, (nsp, cols), 0)
    cur = qpos >> 6
    valid = blk <= cur
    forced = jnp.where(valid, jnp.where(blk == 0, 1.0, jnp.where(blk >= cur - 1, 1.0, 0.0)), 0.0)
    score = jnp.where(forced > 0.0, _BIG, jnp.where(valid, imp_sel, -1.0))
    sel = jnp.zeros((nsp, cols), F32)
    for _ in range(N_SEL):
        m = jnp.max(score, axis=0, keepdims=True)
        idx = jnp.min(jnp.where(score == m, blk, nsp + 1), axis=0, keepdims=True)
        pick = blk == idx
        sel = jnp.where(pick, 1.0, sel)
        score = jnp.where(pick, -2.0, score)
    return jnp.where(valid, sel, 0.0)


_TS = 256


def _pos_lanes(nkeys, tile_off):
    lane = lax.broadcasted_iota(jnp.int32, (nkeys, NSA_KW), 1)
    key = lax.broadcasted_iota(jnp.int32, (nkeys, NSA_KW), 0)
    hi = (tile_off + (key >> 7)).astype(F32)
    lo = (key & (_KT - 1)).astype(F32)
    return jnp.where(lane == 0, hi, jnp.where(lane == 1, lo, 0.0)).astype(BF16)


def _nsa_prompt_kernel(q_ref, glog_ref, kcb_ref, vcbt_ref, ks_ref, vst_ref, kw_ref, vwt_ref,
                        qoff_ref, slope_ref, qaux_ref, o_ref, sel_ref, m_ref, l_ref, acc_ref, *, nq, ncb):
    i = pl.program_id(1)
    st0 = i * nq
    qoff = qoff_ref[...]
    qpos = st0 + qoff
    q = q_ref[0] * (NSA_HD ** -0.5)
    nsp = ncb // 2
    bpt = _TS // L_SEL
    idiag = st0 // _TS
    groups = range(NSA_KV)
    rows = [slice(g * NSA_HD, (g + 1) * NSA_HD) for g in groups]

    qxs, o_cs, firsts = [], [], []
    for g in groups:
        qz = _qz(q, g, nq)
        qxs.append(jnp.concatenate([qz, qaux_ref[g]], axis=1))
        o_c, p_c = _compressed_branch(qz, kcb_ref[0], vcbt_ref[0, rows[g], :], qpos, slope_ref[g], ncb)
        o_cs.append(o_c)
        imp = p_c[:, 0:nq]
        for r in range(1, NSA_REP):
            imp = imp + p_c[:, r * nq:(r + 1) * nq]
        sel = _select_blocks(imp[0:nsp] + imp[nsp:ncb], qpos[:, 0:nq], nsp)
        sel_ref[g] = jnp.concatenate([jnp.where(sel > 0.0, 0.0, NEG)] * NSA_REP, axis=1)
        blk = lax.broadcasted_iota(jnp.int32, (nsp, 1), 0)
        row_any = jnp.max(sel, axis=1, keepdims=True)
        first = jnp.min(jnp.where(row_any > 0.0, jnp.where(blk >= bpt, blk, nsp * bpt), nsp * bpt),
                        axis=0, keepdims=True)
        firsts.append(first[0, 0])
    lo = jnp.minimum(jnp.minimum(firsts[0], firsts[1]) // bpt, idiag)

    m_ref[...] = jnp.full(m_ref.shape, _M0, F32)
    l_ref[...] = jnp.zeros(l_ref.shape, F32)
    acc_ref[...] = jnp.zeros(acc_ref.shape, F32)

    def sel_tile(kt, diag):
        off = kt * _TS if isinstance(kt, int) else pl.multiple_of(kt * _TS, _TS)
        kx = jnp.concatenate([ks_ref[0, pl.ds(off, _TS), :], _pos_lanes(_TS, kt * (_TS // _KT) - i)], axis=1)
        if diag:
            visible = (off + lax.broadcasted_iota(jnp.int32, (_TS, 1), 0)) <= qpos
        for g in groups:
            s = lax.dot_general(kx, qxs[g], _NT, preferred_element_type=F32)
            s = jnp.concatenate([s[b * L_SEL:(b + 1) * L_SEL] + sel_ref[g, pl.ds(bpt * kt + b, 1), :]
                                 for b in range(bpt)], axis=0)
            if diag:
                s = jnp.where(visible, s, NEG)
            m_old = m_ref[g]
            m_new = jnp.maximum(m_old, jnp.max(s, axis=0, keepdims=True))
            alpha = jnp.exp(m_old - m_new)
            p = jnp.exp(s - m_new)
            l_ref[g] = alpha * l_ref[g] + jnp.sum(p, axis=0, keepdims=True)
            vt = vst_ref[rows[g], pl.ds(off, _TS)]
            acc_ref[g] = alpha * acc_ref[g] + jnp.dot(vt, p.astype(BF16), preferred_element_type=F32)
            m_ref[g] = m_new

    @pl.when(idiag > 0)
    def _():
        sel_tile(0, False)

    def sel_body(kt, carry):
        sel_tile(kt, False)
        return carry

    lax.fori_loop(lo, idiag, sel_body, 0)
    sel_tile(idiag, True)

    wk = WINDOW + nq
    wstart = pl.multiple_of(jnp.maximum(i - WINDOW // nq, 0) * nq, nq)
    kxw = jnp.concatenate([kw_ref[0, pl.ds(wstart, wk), :], _pos_lanes(wk, wstart // _KT - i)], axis=1)
    d_w = qpos - (wstart + lax.broadcasted_iota(jnp.int32, (wk, 1), 0))
    in_window = jnp.where(d_w >= 0, d_w, WINDOW + 1) <= WINDOW
    for g in groups:
        s = jnp.where(in_window, lax.dot_general(kxw, qxs[g], _NT, preferred_element_type=F32), NEG)
        m = jnp.max(s, axis=0, keepdims=True)
        p = jnp.exp(s - m)
        l = jnp.sum(p, axis=0, keepdims=True)
        o_w = jnp.dot(vwt_ref[rows[g], pl.ds(wstart, wk)], p.astype(BF16), preferred_element_type=F32) / l
        o_s = _softmax_done(l_ref, acc_ref, g)
        gc = _sigmoid(glog_ref[0, 0, g:g + 1, :])
        gs = _sigmoid(glog_ref[0, 0, 2 + g:3 + g, :])
        gw = _sigmoid(glog_ref[0, 0, 4 + g:5 + g, :])
        o_t = gc * o_cs[g] + gs * o_s + gw * o_w
        for pp in range(NSA_REP // 2):
            pair = jnp.concatenate([o_t[:, (2 * pp) * nq:(2 * pp + 1) * nq],
                                    o_t[:, (2 * pp + 1) * nq:(2 * pp + 2) * nq]], axis=0)
            slab = g * (NSA_REP // 2) + pp
            o_ref[0, :, slab * NSA_KW:(slab + 1) * NSA_KW] = pair.T


def _nsa_cols(nq):
    c = NSA_REP * nq
    qoff = (jnp.arange(c, dtype=jnp.int32) % nq).reshape(1, c)
    slopes = 2.0 ** (-8.0 * jnp.arange(1, NSA_HEADS + 1, dtype=F32) / NSA_HEADS)
    slope = jnp.repeat(slopes.reshape(NSA_KV, NSA_REP), nq, axis=1).reshape(NSA_KV, 1, c)
    return qoff, slope


def _gate_cols(glog, nq):
    b, t, _ = glog.shape
    x = glog.reshape(b, t // nq, nq, NSA_KV, NSA_REP, 3)
    return x.transpose(0, 1, 5, 3, 4, 2).reshape(b, t // nq, 3 * NSA_KV, NSA_REP * nq)


def _even_odd(x):
    return jnp.concatenate([x[:, 0::2], x[:, 1::2]], axis=1)


def nsa_prompt(q, glog, kcb, vcb, ks, vs_t, kw, vw_t, *, nq=128):
    b, t, _ = q.shape
    ncb = kcb.shape[1]
    c = NSA_REP * nq
    nblk = t // nq
    assert nq == _KT and t % _TS == 0 and t >= WINDOW + nq and ncb == t // L_CMP
    qoff, slope = _nsa_cols(nq)
    lane = jnp.arange(NSA_KW)[None, None, :]
    slope_col = slope.reshape(NSA_KV, c, 1)
    qaux = jnp.where(lane == 0, slope_col * _KT, jnp.where(lane == 1, slope_col, 0.0)).astype(BF16)
    kcb_p = _even_odd(kcb).astype(BF16)
    vcbt = _even_odd(vcb).transpose(0, 2, 1).astype(BF16)
    per_b = lambda shape: pl.BlockSpec((1,) + shape, lambda i, j: (i, 0, 0))
    out = pl.pallas_call(
        functools.partial(_nsa_prompt_kernel, nq=nq, ncb=ncb),
        grid=(b, nblk),
        in_specs=[pl.BlockSpec((1, nq, NSA_W), lambda i, j: (i, j, 0)),
                  pl.BlockSpec((1, 1, 3 * NSA_KV, c), lambda i, j: (i, j, 0, 0)),
                  per_b((ncb, NSA_KW)), per_b((NSA_KW, ncb)),
                  per_b((t, NSA_KW)), pl.BlockSpec((NSA_KW, t), lambda i, j: (0, i)),
                  per_b((t, NSA_KW)), pl.BlockSpec((NSA_KW, t), lambda i, j: (0, i)),
                  pl.BlockSpec((1, c), lambda i, j: (0, 0)),
                  pl.BlockSpec((NSA_KV, 1, c), lambda i, j: (0, 0, 0)),
                  pl.BlockSpec((NSA_KV, c, NSA_KW), lambda i, j: (0, 0, 0))],
        out_specs=pl.BlockSpec((1, nq, NSA_W), lambda i, j: (i, j, 0)),
        out_shape=jax.ShapeDtypeStruct((b, t, NSA_W), F32),
        scratch_shapes=[pltpu.VMEM((NSA_KV, ncb // 2, c), F32),
                        pltpu.VMEM((NSA_KV, 1, c), F32), pltpu.VMEM((NSA_KV, 1, c), F32),
                        pltpu.VMEM((NSA_KV, NSA_HD, c), F32)],
        compiler_params=_params("parallel", "arbitrary"),
        name="nsa_prompt",
    )(q, _gate_cols(glog, nq), kcb_p, vcbt, ks, vs_t, kw, vw_t, qoff, slope, qaux)
    return out


def _compress_pages_kernel(x_ref, pe_ref, w_ref, o_ref):
    for g in range(NSA_KV):
        acc = jnp.zeros(o_ref.shape[1:], F32)
        for d in range(0, NSA_HD, 2):
            y = jnp.concatenate([x_ref[0, :, g, d, :] + pe_ref[d:d + 1, :],
                                 x_ref[0, :, g, d + 1, :] + pe_ref[d + 1:d + 2, :]], axis=1)
            acc = acc + jnp.dot(y.astype(BF16), w_ref[d // 2], preferred_element_type=F32)
        o_ref[g] = acc


def nsa_compress_pages(cache_t, layer, pe, w, *, tp=256):
    n_pool, page = cache_t.shape[1], cache_t.shape[4]
    nblk = page // L_CMP
    tp = _row_tile(n_pool, tp)
    pe_t = jnp.tile(pe.T, (1, nblk))
    eye = jnp.eye(nblk, dtype=F32)
    wd = jnp.einsum('nm,lde->dnlme', eye, w).reshape(NSA_HD // 2, 2 * page, nblk * NSA_HD).astype(BF16)
    out = pl.pallas_call(
        _compress_pages_kernel,
        grid=(n_pool // tp,),
        in_specs=[pl.BlockSpec((1, tp, NSA_KV, NSA_HD, page), lambda i: (layer, i, 0, 0, 0)),
                  pl.BlockSpec((NSA_HD, page), lambda i: (0, 0)),
                  pl.BlockSpec((NSA_HD // 2, 2 * page, nblk * NSA_HD), lambda i: (0, 0, 0))],
        out_specs=pl.BlockSpec((NSA_KV, tp, nblk * NSA_HD), lambda i: (0, i, 0)),
        out_shape=jax.ShapeDtypeStruct((NSA_KV, n_pool, nblk * NSA_HD), F32),
        compiler_params=_params("parallel"),
        name="nsa_compress_pages",
    )(cache_t, pe_t, wd)
    return out.reshape(NSA_KV, n_pool, nblk, NSA_HD).transpose(1, 2, 0, 3).reshape(n_pool, nblk, NSA_KW)


def _row_softmax(s, mask):
    sm = jnp.where(mask, s, NEG)
    m = jnp.max(sm, axis=1, keepdims=True)
    p = jnp.where(mask, jnp.exp(sm - m), 0.0)
    l = jnp.sum(p, axis=1, keepdims=True)
    return p * jnp.where(l > 0.0, 1.0 / l, 0.0)


_DECODE_BB = 8


def _nsa_decode_kernel(pt_ref, q_ref, gl_ref, kcb_ref, vcb_ref, *refs, nq, ncb, npages, page, past, wb, nbb):
    ks_pages = refs[0:nbb * npages]
    vs_pages = refs[nbb * npages:2 * nbb * npages]
    (nks_ref, nvs_ref, nkw_ref, nvw_ref, wk_ref, wv_ref, slope_ref, expand_ref,
     o_ref, wko_ref, wvo_ref) = refs[2 * nbb * npages:]
    del pt_ref
    c = NSA_REP * nq
    nblk_lanes = _KT
    n_sel = -(-(past + nq) // L_SEL)
    row = lax.broadcasted_iota(jnp.int32, (c, 1), 0)
    qpos = past + (row & (nq - 1))
    pad_rows = lambda x: jnp.concatenate([x, jnp.zeros((_KT - nq, x.shape[1]), F32)], axis=0)
    lane = lax.broadcasted_iota(jnp.int32, (1, _KT), 1)
    n_sel_keys = (npages + 1) * page
    key_all = lax.broadcasted_iota(jnp.int32, (1, n_sel_keys), 1)
    expand = expand_ref[...]
    pr = lax.broadcasted_iota(jnp.int32, (ncb, nblk_lanes), 0)
    pc = lax.broadcasted_iota(jnp.int32, (ncb, nblk_lanes), 1)
    half = ncb // 2
    pair = jnp.where(jnp.where(pr < half, pr, pr - half) == pc, 1.0, 0.0).astype(BF16)
    cl = lax.broadcasted_iota(jnp.int32, (1, ncb), 1)
    c_pos = jnp.where(cl < half, 2 * cl, 2 * (cl - half) + 1) * L_CMP + (L_CMP - 1)
    d_c = qpos - c_pos
    d_cf = d_c.astype(F32)
    d_s = qpos - key_all
    d_sf = d_s.astype(F32)
    d_w = qpos - (past - wb + lax.broadcasted_iota(jnp.int32, (1, wb + _KT), 1))
    d_wf = d_w.astype(F32)
    in_window = jnp.where(d_w >= 0, d_w, WINDOW + 1) <= WINDOW
    cur = (past + lax.broadcasted_iota(jnp.int32, (nq, 1), 0)) >> 6
    valid = lane <= cur
    forced = jnp.where(valid, jnp.where(lane == 0, 1.0, jnp.where(lane >= cur - 1, 1.0, 0.0)), 0.0)

    for bb in range(nbb):
        q = q_ref[bb] * (NSA_HD ** -0.5)
        new_kt = pad_rows(nkw_ref[bb]).T
        new_vt = pad_rows(nvw_ref[bb]).T
        for g in range(NSA_KV):
            gs = slice(g * NSA_HD, (g + 1) * NSA_HD)
            qg = jnp.concatenate([q[:, (g * NSA_REP + r) * NSA_HD:(g * NSA_REP + r + 1) * NSA_HD]
                                  for r in range(NSA_REP)], axis=0).astype(BF16)
            slope = slope_ref[g]
            s_c = lax.dot_general(qg, kcb_ref[bb, :, gs], _NT, preferred_element_type=F32)
            p_c = _row_softmax(s_c - slope * d_cf, d_c >= 0)
            o_c = jnp.dot(p_c.astype(BF16), vcb_ref[bb, :, gs], preferred_element_type=F32)
            imp = p_c[0:nq]
            for r in range(1, NSA_REP):
                imp = imp + p_c[r * nq:(r + 1) * nq]
            imp_sel = _dot_exact_rhs(imp, pair)
            score = jnp.where(forced > 0.0, _BIG, jnp.where(valid, imp_sel, -1.0))
            before = jnp.zeros((nq, nblk_lanes), F32)
            for bi in range(n_sel):
                sb = score[:, bi:bi + 1]
                before = before + jnp.where(sb > score, 1.0, jnp.where(sb == score, jnp.where(lane > bi, 1.0, 0.0), 0.0))
            sel = jnp.where(valid, jnp.where(before < N_SEL, 1.0, 0.0), 0.0)
            sel_keys = jnp.dot(sel.astype(BF16), expand, preferred_element_type=F32)
            sel_keys = jnp.concatenate([sel_keys] * NSA_REP, axis=0)

            nk = pad_rows(nks_ref[bb, :, gs]).astype(BF16)
            nv = pad_rows(nvs_ref[bb, :, gs]).astype(BF16)
            s_parts = [jnp.dot(qg, ks_pages[bb * npages + j][0, 0, g].astype(BF16), preferred_element_type=F32)
                       for j in range(npages)]
            s_parts.append(lax.dot_general(qg, nk, _NT, preferred_element_type=F32))
            s_s = jnp.concatenate(s_parts, axis=1)
            p_s = _row_softmax(s_s - slope * d_sf, jnp.where(d_s >= 0, sel_keys, 0.0) > 0.0).astype(BF16)
            o_s = jnp.dot(p_s[:, npages * page:], nv, preferred_element_type=F32)
            for j in range(npages):
                o_s = o_s + lax.dot_general(p_s[:, j * page:(j + 1) * page],
                                            vs_pages[bb * npages + j][0, 0, g].astype(BF16), _NT,
                                            preferred_element_type=F32)

            nkw = pad_rows(nkw_ref[bb, :, gs]).astype(BF16)
            nvw = pad_rows(nvw_ref[bb, :, gs]).astype(BF16)
            s_w = jnp.concatenate([jnp.dot(qg, wk_ref[0, bb, g].astype(BF16), preferred_element_type=F32),
                                   lax.dot_general(qg, nkw, _NT, preferred_element_type=F32)], axis=1)
            p_w = _row_softmax(s_w - slope * d_wf, in_window).astype(BF16)
            o_w = (lax.dot_general(p_w[:, 0:wb], wv_ref[0, bb, g].astype(BF16), _NT, preferred_element_type=F32)
                   + jnp.dot(p_w[:, wb:], nvw, preferred_element_type=F32))

            gate = _sigmoid(gl_ref[bb, g])
            o_ref[bb, g] = gate[:, 0:1] * o_c + gate[:, 1:2] * o_s + gate[:, 2:3] * o_w

            for src_ref, new_full, dst_ref in ((wk_ref, new_kt, wko_ref), (wv_ref, new_vt, wvo_ref)):
                new_t = pltpu.roll(new_full[gs, :], _KT - nq, axis=1)
                shifted = pltpu.roll(src_ref[0, bb, g], wb - nq, axis=1)
                dst_ref[bb, g, :, 0:wb - _KT] = shifted[:, 0:wb - _KT]
                dst_ref[bb, g, :, wb - _KT:wb] = jnp.where(lane >= _KT - nq, new_t, shifted[:, wb - _KT:wb])


def nsa_decode(q, glog, kcb, vcb, pool_k, pool_v, page_table, nks, nvs, nkw, nvw, win_k, win_v, layer):
    b, nq, _ = q.shape
    ncb = kcb.shape[1]
    npages = page_table.shape[1]
    page = pool_k.shape[4]
    past = npages * page
    wb = win_k.shape[4]
    assert page == _KT and wb % _KT == 0 and nq & (nq - 1) == 0 and nq % 8 == 0 and nq <= L_SEL
    assert ncb % 2 == 0 and ncb // 2 <= _KT and -(-(past + nq) // L_SEL) <= _KT
    c = NSA_REP * nq
    _, slope = _nsa_cols(nq)
    gl = glog.reshape(b, nq, NSA_KV, NSA_REP, 3).transpose(0, 2, 3, 1, 4).reshape(b, NSA_KV, c, 3)
    nbb = _DECODE_BB if b % _DECODE_BB == 0 else 1
    per_b = lambda shape: pl.BlockSpec((nbb,) + shape, lambda i, pt: (i,) + (0,) * len(shape))
    const = lambda shape: pl.BlockSpec(shape, lambda i, pt: (0,) * len(shape))
    page_spec = lambda bb, j: pl.BlockSpec((1, 1, NSA_KV, NSA_HD, page),
                                           lambda i, pt: (layer, pt[i * nbb + bb, j], 0, 0, 0))
    page_specs = [page_spec(bb, j) for bb in range(nbb) for j in range(npages)]
    win_spec = pl.BlockSpec((1, nbb, NSA_KV, NSA_HD, wb), lambda i, pt: (layer, i, 0, 0, 0))
    n_keys = (npages + 1) * page
    expand = (jnp.arange(n_keys)[None, :] // L_SEL == jnp.arange(_KT)[:, None]).astype(BF16)
    in_specs = ([per_b((nq, NSA_W)), per_b((NSA_KV, c, 3)), per_b((ncb, NSA_KW)), per_b((ncb, NSA_KW))]
                + page_specs * 2
                + [per_b((nq, NSA_KW))] * 4 + [win_spec] * 2 + [const((NSA_KV, c, 1)), const((_KT, n_keys))])
    out, wko, wvo = pl.pallas_call(
        functools.partial(_nsa_decode_kernel, nq=nq, ncb=ncb, npages=npages, page=page, past=past, wb=wb, nbb=nbb),
        grid_spec=pltpu.PrefetchScalarGridSpec(
            num_scalar_prefetch=1,
            grid=(b // nbb,),
            in_specs=in_specs,
            out_specs=[per_b((NSA_KV, c, NSA_HD)), per_b((NSA_KV, NSA_HD, wb)), per_b((NSA_KV, NSA_HD, wb))],
        ),
        out_shape=[jax.ShapeDtypeStruct((b, NSA_KV, c, NSA_HD), F32),
                   jax.ShapeDtypeStruct((b, NSA_KV, NSA_HD, wb), F32),
                   jax.ShapeDtypeStruct((b, NSA_KV, NSA_HD, wb), F32)],
        compiler_params=_params("arbitrary"),
        name="nsa_decode",
    )(page_table, q, gl, _even_odd(kcb).astype(BF16), _even_odd(vcb).astype(BF16),
      *([pool_k] * (nbb * npages)), *([pool_v] * (nbb * npages)), nks, nvs, nkw, nvw, win_k, win_v,
      slope.reshape(NSA_KV, c, 1), expand)
    o = out.reshape(b, NSA_KV, NSA_REP, nq, NSA_HD).transpose(0, 3, 1, 2, 4).reshape(b, nq, NSA_W)
    return o, wko, wvo


def _xattn_cache_kernel(q_ref, k_ref, v_ref, o_ref, *, nq, nbb):
    scale = XA_HD ** -0.5
    for bb in range(nbb):
        q = jnp.concatenate([q_ref[bb, :, h * XA_HD:(h + 1) * XA_HD] for h in range(XA_HEADS)], axis=0)
        k = k_ref[0, bb].astype(BF16)
        v = v_ref[0, bb].astype(BF16)
        s = lax.dot_general(q.astype(BF16), k, _NT, preferred_element_type=F32) * scale
        col_h = lax.broadcasted_iota(jnp.int32, s.shape, 1) & (XA_HEADS - 1)
        row_h = lax.broadcasted_iota(jnp.int32, s.shape, 0) >> (nq.bit_length() - 1)
        mine = col_h == row_h
        m = jnp.max(jnp.where(mine, s, NEG), axis=1, keepdims=True)
        p = jnp.where(mine, jnp.exp(s - m), 0.0)
        p = p / jnp.sum(p, axis=1, keepdims=True)
        o = jnp.dot(p.astype(BF16), v, preferred_element_type=F32)
        for h in range(XA_HEADS):
            o_ref[bb, :, h * XA_HD:(h + 1) * XA_HD] = o[h * nq:(h + 1) * nq]


def xattn_cache(q, cache_k, cache_v, layer):
    b, nq, w = q.shape
    m = cache_k.shape[2]
    assert XA_HEADS & (XA_HEADS - 1) == 0 and nq % 8 == 0
    kv = lambda a: a.reshape(a.shape[0], b, m * XA_HEADS, XA_HD)
    nbb = _DECODE_BB if b % _DECODE_BB == 0 else 1
    kv_spec = pl.BlockSpec((1, nbb, m * XA_HEADS, XA_HD), lambda i: (layer, i, 0, 0))
    return pl.pallas_call(
        functools.partial(_xattn_cache_kernel, nq=nq, nbb=nbb),
        grid=(b // nbb,),
        in_specs=[pl.BlockSpec((nbb, nq, w), lambda i: (i, 0, 0)), kv_spec, kv_spec],
        out_specs=pl.BlockSpec((nbb, nq, w), lambda i: (i, 0, 0)),
        out_shape=jax.ShapeDtypeStruct((b, nq, w), F32),
        compiler_params=_params("parallel"),
        name="xattn_cache",
    )(q, kv(cache_k), kv(cache_v))


_HALO_M = 8
_TN = (((0,), (0,)), ((), ()))


def _softplus(x):
    return jnp.maximum(x, 0.0) + jnp.log1p(jnp.exp(-jnp.abs(x)))


def _ssd_kernel(xbc_ref, z_ref, sm_ref, dtt_ref, cs_ref, h0_ref, cw_ref, cb_ref, dtb_ref, dtbt_ref,
                al_ref, alt_ref, dsk_ref, ng_ref, y_ref, ncs_ref, hf_ref, ext_ref, h_ref, yh_ref, *, ql, dt_col):
    c = pl.program_id(1)
    nc = pl.num_programs(1)

    @pl.when(c == 0)
    def _():
        ext_ref[...] = jnp.zeros_like(ext_ref)
        ext_ref[_HALO_M - (M_CONV_W - 1):_HALO_M, :] = cs_ref[0]
        h_ref[...] = h0_ref[0]

    @pl.when(c > 0)
    def _():
        ext_ref[0:_HALO_M, :] = ext_ref[ql:ql + _HALO_M, :]

    ext_ref[_HALO_M:_HALO_M + ql, :] = xbc_ref[0]
    acc = jnp.zeros((ql, M_CONV_DIM), F32)
    for k in range(M_CONV_W):
        off = _HALO_M - (M_CONV_W - 1) + k
        acc = acc + ext_ref[off:off + ql, :].astype(BF16).astype(F32) * cw_ref[k:k + 1, :]
    xbc = _silu(acc + cb_ref[...])
    xs = xbc[:, 0:M_DIN]
    bm = xbc[:, M_DIN:M_DIN + M_GROUPS * M_DSTATE]
    cm = xbc[:, M_DIN + M_GROUPS * M_DSTATE:M_CONV_DIM]

    dt = _softplus(sm_ref[0, :, dt_col:dt_col + M_HEADS] + dtb_ref[...])
    dtt = _softplus(dtt_ref[0] + dtbt_ref[...])
    dta = dt * (-jnp.exp(al_ref[...]))
    dtat = dtt * (-jnp.exp(alt_ref[...]))
    ti = lax.broadcasted_iota(jnp.int32, (ql, ql), 0)
    si = lax.broadcasted_iota(jnp.int32, (ql, ql), 1)
    causal = si <= ti
    cum = _dot_exact_lhs(jnp.where(causal, 1.0, 0.0).astype(BF16), dta)
    cumt = _dot_exact_rhs(dtat, jnp.where(ti <= si, 1.0, 0.0).astype(BF16))
    cum_last = cum[ql - 1:ql, :]
    edec = jnp.exp(cum)
    eend = jnp.exp(cum_last - cum)
    elast = jnp.exp(cum_last)

    rep = M_HEADS // M_GROUPS
    for gi in range(M_GROUPS):
        b_g = bm[:, gi * M_DSTATE:(gi + 1) * M_DSTATE]
        c_g = cm[:, gi * M_DSTATE:(gi + 1) * M_DSTATE].astype(BF16)
        cb = lax.dot_general(c_g, b_g.astype(BF16), _NT, preferred_element_type=F32)
        for hh in range(rep):
            h = gi * rep + hh
            hs = slice(h * M_HDIM, (h + 1) * M_HDIM)
            lmat = jnp.where(causal, jnp.exp(cum[:, h:h + 1] - cumt[h:h + 1, :]), 0.0)
            x_h = xs[:, hs]
            xdt = (x_h * dt[:, h:h + 1]).astype(BF16)
            y_diag = jnp.dot((cb * lmat).astype(BF16), xdt, preferred_element_type=F32)
            h_in = h_ref[h]
            y_off = lax.dot_general(c_g, h_in.astype(BF16), _NT, preferred_element_type=F32) * edec[:, h:h + 1]
            bd = (b_g * eend[:, h:h + 1]).astype(BF16)
            s_chunk = lax.dot_general(xdt, bd, _TN, preferred_element_type=F32)
            h_ref[h] = elast[:, h:h + 1] * h_in + s_chunk
            yh_ref[:, hs] = y_diag + y_off + dsk_ref[:, hs] * x_h

    yz = yh_ref[...] * _silu(z_ref[0])
    y_ref[0] = _rms(yz, ng_ref[...])

    @pl.when(c == nc - 1)
    def _():
        ncs_ref[0] = ext_ref[_HALO_M + ql - (M_CONV_W - 1):_HALO_M + ql, :]
        hf_ref[0] = h_ref[...]


def ssd_mixer(xbc, z, small, dt_col, conv_state, h0, conv_w, conv_b, dt_bias, a_log, d_skip, norm_g, *, ql):
    b, t, _ = xbc.shape
    nc = t // ql
    sw = small.shape[2]
    dtt = small[:, :, dt_col:dt_col + M_HEADS].transpose(0, 2, 1)
    const = lambda shape: pl.BlockSpec(shape, lambda i, j: (0,) * len(shape))
    per_b = lambda shape: pl.BlockSpec((1,) + shape, lambda i, j: (i,) + (0,) * len(shape))
    row = lambda x: x.reshape(1, -1)
    colv = lambda x: x.reshape(-1, 1)
    return pl.pallas_call(
        functools.partial(_ssd_kernel, ql=ql, dt_col=dt_col),
        grid=(b, nc),
        in_specs=[pl.BlockSpec((1, ql, M_CONV_DIM), lambda i, j: (i, j, 0)),
                  pl.BlockSpec((1, ql, M_DIN), lambda i, j: (i, j, 0)),
                  pl.BlockSpec((1, ql, sw), lambda i, j: (i, j, 0)),
                  pl.BlockSpec((1, M_HEADS, ql), lambda i, j: (i, 0, j)),
                  per_b((M_CONV_W - 1, M_CONV_DIM)), per_b((M_HEADS, M_HDIM, M_DSTATE)),
                  const((M_CONV_W, M_CONV_DIM)), const((1, M_CONV_DIM)),
                  const((1, M_HEADS)), const((M_HEADS, 1)), const((1, M_HEADS)), const((M_HEADS, 1)),
                  const((1, M_DIN)), const((1, M_DIN))],
        out_specs=[pl.BlockSpec((1, ql, M_DIN), lambda i, j: (i, j, 0)),
                   per_b((M_CONV_W - 1, M_CONV_DIM)), per_b((M_HEADS, M_HDIM, M_DSTATE))],
        out_shape=[jax.ShapeDtypeStruct((b, t, M_DIN), F32),
                   jax.ShapeDtypeStruct((b, M_CONV_W - 1, M_CONV_DIM), F32),
                   jax.ShapeDtypeStruct((b, M_HEADS, M_HDIM, M_DSTATE), F32)],
        scratch_shapes=[pltpu.VMEM((_HALO_M + ql, M_CONV_DIM), F32),
                        pltpu.VMEM((M_HEADS, M_HDIM, M_DSTATE), F32),
                        pltpu.VMEM((ql, M_DIN), F32)],
        compiler_params=_params("parallel", "arbitrary"),
        name="ssd_mixer",
    )(xbc, z, small, dtt, conv_state, h0, conv_w, row(conv_b), row(dt_bias), colv(dt_bias),
      row(a_log), colv(a_log), row(jnp.repeat(d_skip, M_HDIM)), row(norm_g))


_SMALL_W = 128
_OD_SPLITS = (NSA_W,) + (NSA_KW,) * 6 + (M_DIN, M_CONV_DIM, _SMALL_W)


def _odd_w_in(w):
    o_kv = NSA_W
    o_gate = o_kv + 6 * NSA_KW
    o_z = o_gate + 3 * NSA_HEADS
    o_xbc = o_z + M_DIN
    o_dt = o_xbc + M_CONV_DIM
    pad = jnp.zeros((w.shape[0], _SMALL_W - 3 * NSA_HEADS - M_HEADS), F32)
    return jnp.concatenate([w[:, :o_gate], w[:, o_z:o_xbc], w[:, o_xbc:o_dt],
                            w[:, o_gate:o_z], w[:, o_dt:], pad], axis=1)


def kernel(x_prompt, x_sample, state_conv_a, state_conv_b, cache_cmp_k, cache_cmp_v, cache_sel_k, cache_sel_v, cache_win_k, cache_win_v, state_ssm, state_ssm_conv, cache_mem_k, cache_mem_v, page_table, mem_prompt, norm_mix, norm_xattn, norm_ffn, norm_final, ev_w_in, ev_conv_a, ev_conv_b, ev_conv_b_bias, ev_ln_g, ev_ln_b, ev_w_out, od_w_in, od_cmp_pe, od_cmp_wk, od_cmp_wv, od_ssm_conv_w, od_ssm_conv_b, od_dt_bias, od_a_log, od_d_skip, od_ssm_norm, od_w_out, xa_wq, xa_wk, xa_wv, xa_wo, moe_wg, moe_bg, moe_we, moe_be, moe_w1, moe_w3, moe_w2):
    bp, tp, d = x_prompt.shape
    bs, ts, _ = x_sample.shape
    n_p, n_s = bp * tp, bs * ts
    n_mem = mem_prompt.shape[1]
    depth = norm_mix.shape[0]
    n_pool, page = cache_cmp_k.shape[1:3]
    wb = cache_win_k.shape[2]
    dt_col = 3 * NSA_HEADS

    def groups(pair):
        return pair[0].reshape(bp, tp, -1), pair[1].reshape(bs, ts, -1)

    def rows(a_p, a_s):
        return [a_p.reshape(n_p, a_p.shape[-1]), a_s.reshape(n_s, a_s.shape[-1])]

    hs = rows(x_prompt, x_sample)
    out = {k: [] for k in ("ca_p", "ca_s", "cb_p", "cb_s", "wk_p", "wk_s", "wv_p", "wv_s",
                           "sm_p", "sm_s", "sc_p", "sc_s", "mk_p", "mv_p")}
    rows_p = [[], [], [], []]
    rows_s = [[], [], [], []]
    for i in range(depth):
        j = i // 2
        if i % 2 == 0:
            u_p, u_s = groups(g_norm_matmul(hs, norm_mix[i], ev_w_in[j]))
            ev = (ev_conv_a[j], ev_conv_b[j], ev_conv_b_bias[j], ev_ln_g[j], ev_ln_b[j])
            y_p, na_p, nb_p = even_conv(u_p, jnp.zeros((bp, CONV_A_W - 1, D_A), F32),
                                        jnp.zeros((bp, CONV_B_W - 1, D_B), F32), *ev)
            y_s, na_s, nb_s = even_conv(u_s, state_conv_a[j], state_conv_b[j], *ev)
            hs = g_matmul_res([rows(y_p, y_s)], [ev_w_out[j]], hs)
            out["ca_p"].append(na_p)
            out["ca_s"].append(na_s)
            out["cb_p"].append(nb_p)
            out["cb_s"].append(nb_s)
        else:
            w_in = _odd_w_in(od_w_in[j])
            c_ks, c_kw = NSA_W + 2 * NSA_KW, NSA_W + 4 * NSA_KW
            u = g_odd_in_proj(hs, norm_mix[i], w_in, w_in[:, NSA_W:NSA_W + 6 * NSA_KW].T, (c_ks, c_kw))
            n_main = len(_OD_SPLITS)
            ks_b, kw_b, vs_t, vw_t = (u[n_main + k][0] for k in range(4))
            kvt_p = [u[n_main + 4 + k][0].reshape(NSA_KV, NSA_HD, bp, tp).transpose(2, 3, 0, 1) for k in range(6)]
            q_p, q_s = groups(u[0])
            kv = [groups(u[1 + k]) for k in range(6)]
            kvp = [a for a, _ in kv]
            kvs = [b for _, b in kv]
            z_p, z_s = groups(u[7])
            xbc_p, xbc_s = groups(u[8])
            sm_p, sm_s = groups(u[9])
            pe, wck, wcv = od_cmp_pe[j], od_cmp_wk[j], od_cmp_wv[j]
            mw = (od_ssm_conv_w[j], od_ssm_conv_b[j], od_dt_bias[j], od_a_log[j], od_d_skip[j], od_ssm_norm[j])
            blocks = lambda a: a.reshape(-1, L_CMP, NSA_KW)
            ncb = tp // L_CMP
            kcb_p = nsa_compress(blocks(kvp[0][:, :ncb * L_CMP]), pe, wck).reshape(bp, ncb, NSA_KW)
            vcb_p = nsa_compress(blocks(kvp[1][:, :ncb * L_CMP]), pe, wcv).reshape(bp, ncb, NSA_KW)
            o_p = nsa_prompt(q_p, sm_p[:, :, :dt_col], kcb_p, vcb_p, ks_b.reshape(bp, tp, NSA_KW), vs_t,
                             kw_b.reshape(bp, tp, NSA_KW), vw_t)
            keep = min(WINDOW, tp)
            y_p, nsc_p, nsm_p = ssd_mixer(xbc_p, z_p, sm_p, dt_col, jnp.zeros((bp, M_CONV_W - 1, M_CONV_DIM), F32),
                                          jnp.zeros((bp, M_HEADS, M_HDIM, M_DSTATE), F32), *mw, ql=128)
            tokens_last = lambda a: jnp.transpose(a, (0, 1, 3, 4, 2))
            kcp = nsa_compress_pages(tokens_last(cache_cmp_k), j, pe, wck)
            vcp = nsa_compress_pages(tokens_last(cache_cmp_v), j, pe, wcv)
            kcb_s = kcp[page_table].reshape(bs, -1, NSA_KW)
            vcb_s = vcp[page_table].reshape(bs, -1, NSA_KW)
            o_s, nwk_s, nwv_s = nsa_decode(
                q_s, sm_s[:, :, :dt_col], kcb_s, vcb_s, tokens_last(cache_sel_k), tokens_last(cache_sel_v),
                page_table, kvs[2], kvs[3], kvs[4], kvs[5], tokens_last(cache_win_k), tokens_last(cache_win_v), j)
            nwk_s = jnp.transpose(nwk_s, (0, 3, 1, 2))
            nwv_s = jnp.transpose(nwv_s, (0, 3, 1, 2))
            y_s, nsc_s, nsm_s = ssd_mixer(xbc_s, z_s, sm_s, dt_col, state_ssm_conv[j], state_ssm[j], *mw, ql=ts)
            w_out = od_w_out[j]
            hs = g_matmul_res([rows(o_p, o_s), rows(y_p, y_s)], [w_out[:NSA_W], w_out[NSA_W:]], hs)
            heads = lambda a: a.reshape(a.shape[0], a.shape[1], NSA_KV, NSA_HD)
            for k in range(4):
                rows_p[k].append(kvt_p[k])
                rows_s[k].append(heads(kvs[k]))
            out["wk_p"].append(kvt_p[4][:, tp - keep:])
            out["wv_p"].append(kvt_p[5][:, tp - keep:])
            out["wk_s"].append(nwk_s)
            out["wv_s"].append(nwv_s)
            out["sc_p"].append(nsc_p)
            out["sc_s"].append(nsc_s)
            out["sm_p"].append(nsm_p)
            out["sm_s"].append(nsm_s)
        mk, mv = norm_matmul(mem_prompt.reshape(bp * n_mem, d), None,
                             jnp.concatenate([xa_wk[i], xa_wv[i]], axis=1), norm=False,
                             splits=(XA_HEADS * XA_HD, XA_HEADS * XA_HD))
        mk = mk.reshape(bp, n_mem, XA_HEADS * XA_HD)
        mv = mv.reshape(bp, n_mem, XA_HEADS * XA_HD)
        out["mk_p"].append(mk.reshape(bp, n_mem, XA_HEADS, XA_HD))
        out["mv_p"].append(mv.reshape(bp, n_mem, XA_HEADS, XA_HD))
        qx_p, qx_s = groups(g_norm_matmul(hs, norm_xattn[i], xa_wq[i]))
        ox_p = xattn(qx_p, mk, mv)
        ox_s = xattn_cache(qx_s, cache_mem_k, cache_mem_v, i)
        hs = g_matmul_res([rows(ox_p, ox_s)], [xa_wo[i]], hs)
        hs = moe_layer(hs, norm_ffn[i], moe_wg[i], moe_bg[i], moe_we[i], moe_be[i], moe_w1, moe_w3, moe_w2, i)
    y_prompt, y_sample = groups(g_rmsnorm(hs, norm_final))
    st = lambda k: jnp.stack(out[k])
    return (y_prompt, y_sample, st("ca_p"), st("ca_s"), st("cb_p"), st("cb_s"),
            jnp.stack(rows_p[0]), jnp.stack(rows_s[0]), jnp.stack(rows_p[1]), jnp.stack(rows_s[1]),
            jnp.stack(rows_p[2]), jnp.stack(rows_s[2]), jnp.stack(rows_p[3]), jnp.stack(rows_s[3]),
            st("wk_p"), st("wk_s"), st("wv_p"), st("wv_s"), st("sm_p"), st("sm_s"), st("sc_p"), st("sc_s"),
            st("mk_p"), st("mv_p"))
```
